```python
import math
import jax, jax.numpy as jnp
from jax import lax
import numpy as np

D_MODEL = 1024
BATCH = 32
SEQ = 2048
DEPTH = 2

N_A_LAYERS = DEPTH // 2
N_B_LAYERS = DEPTH - N_A_LAYERS
DN_ALPHA = (2 * DEPTH) ** 0.25
DN_BETA = (8 * DEPTH) ** -0.25
SSD_EXPAND = 2
SSD_D_INNER = SSD_EXPAND * D_MODEL
SSD_HEAD_DIM = 64
SSD_N_HEADS = SSD_D_INNER // SSD_HEAD_DIM
SSD_N_GROUPS = 8
SSD_HEADS_PER_GROUP = SSD_N_HEADS // SSD_N_GROUPS
SSD_D_STATE = 128
SSD_CONV = 4
SSD_CHUNK = 128
SSD_GN = SSD_N_GROUPS * SSD_D_STATE
SSD_CONV_DIM = SSD_D_INNER + 2 * SSD_GN
SSD_IN_DIM = 2 * SSD_D_INNER + 2 * SSD_GN + SSD_N_HEADS
SSD_DT_MIN = 0.001
SSD_DT_MAX = 0.1
MLA_N_HEADS = D_MODEL // 128
MLA_Q_RANK = 384
MLA_KV_RANK = 256
MLA_NOPE = 128
MLA_ROPE = 64
MLA_V = 128
ROPE_THETA = 10000.0
Q_BLOCK = 128
MAX_POS_OFFSET = 1024
FFN_HIDDEN = 2816
FFN_CONV = 3
LN_EPS = 1e-5
RMS_EPS = 1e-6

kernel_name = 'yoco_ssd_mla_convffn_deepnorm'


def layer_norm(x, g, b):
    xf = x.astype(jnp.float32)
    mu = jnp.mean(xf, axis=-1, keepdims=True)
    var = jnp.mean(jnp.square(xf - mu), axis=-1, keepdims=True)
    return ((xf - mu) * lax.rsqrt(var + LN_EPS) * g.astype(jnp.float32) + b.astype(jnp.float32)).astype(x.dtype)


def rms_norm(x, g, eps=RMS_EPS):
    xf = x.astype(jnp.float32)
    y = xf * lax.rsqrt(jnp.mean(xf * xf, axis=-1, keepdims=True) + eps)
    return (y * g.astype(jnp.float32)).astype(x.dtype)


def causal_depthwise_conv(x, w, b):
    width, ch = w.shape
    y = lax.conv_general_dilated(x, w[:, None, :].astype(x.dtype), window_strides=(1,),
                                 padding=[(width - 1, 0)],
                                 dimension_numbers=('NWC', 'WIO', 'NWC'),
                                 feature_group_count=ch)
    return y + b.astype(x.dtype)


def rope_tables(positions):
    inv_freq = 1.0 / (ROPE_THETA ** (jnp.arange(0, MLA_ROPE, 2, dtype=jnp.float32) / MLA_ROPE))
    ang = positions.astype(jnp.float32)[..., None] * inv_freq
    return jnp.cos(ang), jnp.sin(ang)


def apply_rope(x, cos, sin):
    xf = x.astype(jnp.float32)
    x1, x2 = jnp.split(xf, 2, axis=-1)
    return jnp.concatenate([x1 * cos - x2 * sin, x2 * cos + x1 * sin], axis=-1).astype(x.dtype)


def ssd_chunked_scan(xs, dt, A, Bm, Cm):
    b, s = xs.shape[:2]
    nc = s // SSD_CHUNK

    def chunks(a):
        return jnp.moveaxis(a.reshape((b, nc, SSD_CHUNK) + a.shape[2:]), 1, 0)

    causal = jnp.tril(jnp.ones((SSD_CHUNK, SSD_CHUNK), dtype=bool))[None, :, :, None, None]

    def step(state, inp):
        xc, dtc, bc, cc = inp
        cum = jnp.cumsum(dtc * A, axis=1)
        seg = cum[:, :, None] - cum[:, None, :]
        decay = jnp.exp(jnp.where(causal, seg, -jnp.inf))
        cb = jnp.einsum('btgn,bsgn->btsg', cc, bc)
        w = cb[..., None] * decay * dtc[:, None]
        y = jnp.einsum('btsgk,bsgkp->btgkp', w, xc)
        y = y + jnp.einsum('btgn,bgkpn->btgkp', cc, state) * jnp.exp(cum)[..., None]
        w_end = jnp.exp(cum[:, -1:] - cum) * dtc
        state = (state * jnp.exp(cum[:, -1])[..., None, None]
                 + jnp.einsum('bsgk,bsgkp,bsgn->bgkpn', w_end, xc, bc))
        return state, y

    init = jnp.zeros((b, SSD_N_GROUPS, SSD_HEADS_PER_GROUP, SSD_HEAD_DIM, SSD_D_STATE), jnp.float32)
    _, y = lax.scan(step, init, (chunks(xs), chunks(dt), chunks(Bm), chunks(Cm)))
    return jnp.moveaxis(y, 0, 1).reshape(xs.shape)


def ssd_mixer(x, in_proj, conv_w, conv_b, dt_bias, A_log, D, norm_g, out_proj):
    b, s, _ = x.shape
    G, K, P, N = SSD_N_GROUPS, SSD_HEADS_PER_GROUP, SSD_HEAD_DIM, SSD_D_STATE
    zxbcdt = x @ in_proj
    z = zxbcdt[..., :SSD_D_INNER]
    xbc = zxbcdt[..., SSD_D_INNER:SSD_D_INNER + SSD_CONV_DIM]
    dt = zxbcdt[..., SSD_D_INNER + SSD_CONV_DIM:]
    xbc = jax.nn.silu(causal_depthwise_conv(xbc, conv_w, conv_b))
    xs = xbc[..., :SSD_D_INNER].reshape(b, s, G, K, P).astype(jnp.float32)
    Bm = xbc[..., SSD_D_INNER:SSD_D_INNER + SSD_GN].reshape(b, s, G, N).astype(jnp.float32)
    Cm = xbc[..., SSD_D_INNER + SSD_GN:].reshape(b, s, G, N).astype(jnp.float32)
    dt = jax.nn.softplus(dt.astype(jnp.float32) + dt_bias.astype(jnp.float32)).reshape(b, s, G, K)
    A = -jnp.exp(A_log.astype(jnp.float32)).reshape(G, K)
    y = ssd_chunked_scan(xs, dt, A, Bm, Cm)
    y = y + D.astype(jnp.float32).reshape(G, K)[..., None] * xs
    y = y.reshape(b, s, SSD_D_INNER) * jax.nn.silu(z.astype(jnp.float32))
    yg = y.reshape(b, s, G, SSD_D_INNER // G)
    yg = yg * lax.rsqrt(jnp.mean(yg * yg, axis=-1, keepdims=True) + LN_EPS)
    y = (yg.reshape(b, s, SSD_D_INNER) * norm_g.astype(jnp.float32)).astype(x.dtype)
    return y @ out_proj


def mla_shared_kv(h, kv_down_proj, kv_norm_g, kv_up_k, kv_up_v, cos, sin):
    b, s, _ = h.shape
    ckv_kr = h @ kv_down_proj
    c_kv = rms_norm(ckv_kr[..., :MLA_KV_RANK], kv_norm_g)
    k_rope = apply_rope(ckv_kr[..., MLA_KV_RANK:], cos, sin)
    k_nope = (c_kv @ kv_up_k).reshape(b, s, MLA_N_HEADS, MLA_NOPE)
    v = (c_kv @ kv_up_v).reshape(b, s, MLA_N_HEADS, MLA_V)
    return k_nope, k_rope, v


def mla_attention(h, q_down, q_norm_g, q_up, out_proj, k_nope, k_rope, v, cos, sin):
    b, s, _ = h.shape
    c_q = rms_norm(h @ q_down, q_norm_g)
    q = (c_q @ q_up).reshape(b, s, MLA_N_HEADS, MLA_NOPE + MLA_ROPE)
    q_nope = q[..., :MLA_NOPE]
    q_rope = apply_rope(q[..., MLA_NOPE:], cos[:, :, None], sin[:, :, None])
    nb = s // Q_BLOCK
    scale = (MLA_NOPE + MLA_ROPE) ** -0.5
    key_idx = jnp.arange(s)

    def to_blocks(a):
        return jnp.moveaxis(a.reshape((b, nb, Q_BLOCK) + a.shape[2:]), 1, 0)

    def attend_block(args):
        qn, qr, blk = args
        scores = (jnp.einsum('bqhd,bkhd->bhqk', qn, k_nope)
                  + jnp.einsum('bqhr,bkr->bhqk', qr, k_rope)).astype(jnp.float32) * scale
        q_idx = blk * Q_BLOCK + jnp.arange(Q_BLOCK)
        scores = jnp.where(key_idx[None, :] <= q_idx[:, None], scores, -jnp.inf)
        p = jax.nn.softmax(scores, axis=-1).astype(v.dtype)
        return jnp.einsum('bhqk,bkhd->bqhd', p, v)

    o = lax.map(attend_block, (to_blocks(q_nope), to_blocks(q_rope), jnp.arange(nb)))
    o = jnp.moveaxis(o, 0, 1).reshape(b, s, MLA_N_HEADS * MLA_V)
    return o @ out_proj


def conv_ffn(h, up, conv_w, conv_b, down):
    u = causal_depthwise_conv(h @ up, conv_w, conv_b)
    g, val = jnp.split(u, 2, axis=-1)
    return (jax.nn.silu(g) * val) @ down


def _fwd_setup_inputs(seed: int = 0) -> dict:
    key = jax.random.key(seed)
    ks = jax.random.split(key, 32)
    f32 = jnp.float32

    def nrm(k, shape, scale):
        return jax.random.normal(k, shape, f32) * scale

    na, nbl, d = N_A_LAYERS, N_B_LAYERS, D_MODEL
    x = nrm(ks[0], (BATCH, SEQ, d), 1.0)
    offset = jax.random.randint(ks[1], (BATCH, 1), 0, MAX_POS_OFFSET, dtype=jnp.int32)
    positions = (offset + jnp.arange(SEQ, dtype=jnp.int32)[None, :]).astype(jnp.int32)

    u = jax.random.uniform(ks[2], (na, SSD_N_HEADS), f32)
    dt0 = jnp.exp(u * (math.log(SSD_DT_MAX) - math.log(SSD_DT_MIN)) + math.log(SSD_DT_MIN))
    dt0 = jnp.maximum(dt0, 1e-4)
    ssd_dt_bias = dt0 + jnp.log(-jnp.expm1(-dt0))
    ssd_A_log = jnp.log(jax.random.uniform(ks[3], (na, SSD_N_HEADS), f32, 1.0, 16.0))

    return {
        'x': x,
        'positions': positions,
        'ssd_in_proj': nrm(ks[4], (na, d, SSD_IN_DIM), d ** -0.5),
        'ssd_conv_w': nrm(ks[5], (na, SSD_CONV, SSD_CONV_DIM), SSD_CONV ** -0.5),
        'ssd_conv_b': nrm(ks[6], (na, SSD_CONV_DIM), 0.02),
        'ssd_dt_bias': ssd_dt_bias,
        'ssd_A_log': ssd_A_log,
        'ssd_D': 1.0 + nrm(ks[7], (na, SSD_N_HEADS), 0.1),
        'ssd_norm_g': 1.0 + nrm(ks[8], (na, SSD_D_INNER), 0.02),
        'ssd_out_proj': nrm(ks[9], (na, SSD_D_INNER, d), DN_BETA * SSD_D_INNER ** -0.5),
        'kv_down_proj': nrm(ks[10], (d, MLA_KV_RANK + MLA_ROPE), d ** -0.5),
        'kv_norm_g': 1.0 + nrm(ks[11], (MLA_KV_RANK,), 0.02),
        'kv_up_k': nrm(ks[12], (MLA_KV_RANK, MLA_N_HEADS * MLA_NOPE), MLA_KV_RANK ** -0.5),
        'kv_up_v': nrm(ks[13], (MLA_KV_RANK, MLA_N_HEADS * MLA_V), DN_BETA * MLA_KV_RANK ** -0.5),
        'q_down_proj': nrm(ks[14], (nbl, d, MLA_Q_RANK), d ** -0.5),
        'q_norm_g': 1.0 + nrm(ks[15], (nbl, MLA_Q_RANK), 0.02),
        'q_up_proj': nrm(ks[16], (nbl, MLA_Q_RANK, MLA_N_HEADS * (MLA_NOPE + MLA_ROPE)), MLA_Q_RANK ** -0.5),
        'attn_out_proj': nrm(ks[17], (nbl, MLA_N_HEADS * MLA_V, d), DN_BETA * (MLA_N_HEADS * MLA_V) ** -0.5),
        'ffn_up': nrm(ks[18], (DEPTH, d, 2 * FFN_HIDDEN), d ** -0.5),
        'ffn_conv_w': nrm(ks[19], (DEPTH, FFN_CONV, 2 * FFN_HIDDEN), FFN_CONV ** -0.5),
        'ffn_conv_b': nrm(ks[20], (DEPTH, 2 * FFN_HIDDEN), 0.02),
        'ffn_down': nrm(ks[21], (DEPTH, FFN_HIDDEN, d), DN_BETA * FFN_HIDDEN ** -0.5),
        'ln_mix_g': 1.0 + nrm(ks[22], (DEPTH, d), 0.02),
        'ln_mix_b': nrm(ks[23], (DEPTH, d), 0.02),
        'ln_ffn_g': 1.0 + nrm(ks[24], (DEPTH, d), 0.02),
        'ln_ffn_b': nrm(ks[25], (DEPTH, d), 0.02),
    }


def _fwd_reference(x, positions, ssd_in_proj, ssd_conv_w, ssd_conv_b, ssd_dt_bias, ssd_A_log, ssd_D,
              ssd_norm_g, ssd_out_proj, kv_down_proj, kv_norm_g, kv_up_k, kv_up_v, q_down_proj,
              q_norm_g, q_up_proj, attn_out_proj, ffn_up, ffn_conv_w, ffn_conv_b, ffn_down,
              ln_mix_g, ln_mix_b, ln_ffn_g, ln_ffn_b):
    cos, sin = rope_tables(positions)
    h = x
    shared_kv = None
    for i in range(DEPTH):
        if i < N_A_LAYERS:
            mix = ssd_mixer(h, ssd_in_proj[i], ssd_conv_w[i], ssd_conv_b[i], ssd_dt_bias[i],
                            ssd_A_log[i], ssd_D[i], ssd_norm_g[i], ssd_out_proj[i])
        else:
            j = i - N_A_LAYERS
            k_nope, k_rope, v = shared_kv
            mix = mla_attention(h, q_down_proj[j], q_norm_g[j], q_up_proj[j], attn_out_proj[j],
                                k_nope, k_rope, v, cos, sin)
        h = layer_norm(DN_ALPHA * h + mix, ln_mix_g[i], ln_mix_b[i])
        ff = conv_ffn(h, ffn_up[i], ffn_conv_w[i], ffn_conv_b[i], ffn_down[i])
        h = layer_norm(DN_ALPHA * h + ff, ln_ffn_g[i], ln_ffn_b[i])
        if i == N_A_LAYERS - 1:
            shared_kv = mla_shared_kv(h, kv_down_proj, kv_norm_g, kv_up_k, kv_up_v, cos, sin)
    return h


import jax as _jax
import jax.numpy as _jnp

TWIN_FORMAT = 'train_step'
FWD_PARAMS = ['x', 'positions', 'ssd_in_proj', 'ssd_conv_w', 'ssd_conv_b', 'ssd_dt_bias', 'ssd_A_log', 'ssd_D', 'ssd_norm_g', 'ssd_out_proj', 'kv_down_proj', 'kv_norm_g', 'kv_up_k', 'kv_up_v', 'q_down_proj', 'q_norm_g', 'q_up_proj', 'attn_out_proj', 'ffn_up', 'ffn_conv_w', 'ffn_conv_b', 'ffn_down', 'ln_mix_g', 'ln_mix_b', 'ln_ffn_g', 'ln_ffn_b']
TWIN_WEIGHTS = ['ssd_in_proj', 'ssd_conv_w', 'ssd_conv_b', 'ssd_dt_bias', 'ssd_A_log', 'ssd_D', 'ssd_norm_g', 'ssd_out_proj', 'kv_down_proj', 'kv_norm_g', 'kv_up_k', 'kv_up_v', 'q_down_proj', 'q_norm_g', 'q_up_proj', 'attn_out_proj', 'ffn_up', 'ffn_conv_w', 'ffn_conv_b', 'ffn_down', 'ln_mix_g', 'ln_mix_b', 'ln_ffn_g', 'ln_ffn_b']
TWIN_DIFF_INPUT = 'x'
TWIN_INPUTS = ['x', 'positions', 'ssd_in_proj', 'ssd_conv_w', 'ssd_conv_b', 'ssd_dt_bias', 'ssd_A_log', 'ssd_D', 'ssd_norm_g', 'ssd_out_proj', 'kv_down_proj', 'kv_norm_g', 'kv_up_k', 'kv_up_v', 'q_down_proj', 'q_norm_g', 'q_up_proj', 'attn_out_proj', 'ffn_up', 'ffn_conv_w', 'ffn_conv_b', 'ffn_down', 'ln_mix_g', 'ln_mix_b', 'ln_ffn_g', 'ln_ffn_b', 'loss_target', 'm_ssd_in_proj', 'm_ssd_conv_w', 'm_ssd_conv_b', 'm_ssd_dt_bias', 'm_ssd_A_log', 'm_ssd_D', 'm_ssd_norm_g', 'm_ssd_out_proj', 'm_kv_down_proj', 'm_kv_norm_g', 'm_kv_up_k', 'm_kv_up_v', 'm_q_down_proj', 'm_q_norm_g', 'm_q_up_proj', 'm_attn_out_proj', 'm_ffn_up', 'm_ffn_conv_w', 'm_ffn_conv_b', 'm_ffn_down', 'm_ln_mix_g', 'm_ln_mix_b', 'm_ln_ffn_g', 'm_ln_ffn_b', 'v_ssd_in_proj', 'v_ssd_conv_w', 'v_ssd_conv_b', 'v_ssd_dt_bias', 'v_ssd_A_log', 'v_ssd_D', 'v_ssd_norm_g', 'v_ssd_out_proj', 'v_kv_down_proj', 'v_kv_norm_g', 'v_kv_up_k', 'v_kv_up_v', 'v_q_down_proj', 'v_q_norm_g', 'v_q_up_proj', 'v_attn_out_proj', 'v_ffn_up', 'v_ffn_conv_w', 'v_ffn_conv_b', 'v_ffn_down', 'v_ln_mix_g', 'v_ln_mix_b', 'v_ln_ffn_g', 'v_ln_ffn_b']
TWIN_OUTPUTS = ['loss', 'grad_x', 'grad_ssd_in_proj', 'grad_ssd_conv_w', 'grad_ssd_conv_b', 'grad_ssd_dt_bias', 'grad_ssd_A_log', 'grad_ssd_D', 'grad_ssd_norm_g', 'grad_ssd_out_proj', 'grad_kv_down_proj', 'grad_kv_norm_g', 'grad_kv_up_k', 'grad_kv_up_v', 'grad_q_down_proj', 'grad_q_norm_g', 'grad_q_up_proj', 'grad_attn_out_proj', 'grad_ffn_up', 'grad_ffn_conv_w', 'grad_ffn_conv_b', 'grad_ffn_down', 'grad_ln_mix_g', 'grad_ln_mix_b', 'grad_ln_ffn_g', 'grad_ln_ffn_b', 'delta_ssd_in_proj', 'delta_ssd_conv_w', 'delta_ssd_conv_b', 'delta_ssd_dt_bias', 'delta_ssd_A_log', 'delta_ssd_D', 'delta_ssd_norm_g', 'delta_ssd_out_proj', 'delta_kv_down_proj', 'delta_kv_norm_g', 'delta_kv_up_k', 'delta_kv_up_v', 'delta_q_down_proj', 'delta_q_norm_g', 'delta_q_up_proj', 'delta_attn_out_proj', 'delta_ffn_up', 'delta_ffn_conv_w', 'delta_ffn_conv_b', 'delta_ffn_down', 'delta_ln_mix_g', 'delta_ln_mix_b', 'delta_ln_ffn_g', 'delta_ln_ffn_b', 'new_m_ssd_in_proj', 'new_m_ssd_conv_w', 'new_m_ssd_conv_b', 'new_m_ssd_dt_bias', 'new_m_ssd_A_log', 'new_m_ssd_D', 'new_m_ssd_norm_g', 'new_m_ssd_out_proj', 'new_m_kv_down_proj', 'new_m_kv_norm_g', 'new_m_kv_up_k', 'new_m_kv_up_v', 'new_m_q_down_proj', 'new_m_q_norm_g', 'new_m_q_up_proj', 'new_m_attn_out_proj', 'new_m_ffn_up', 'new_m_ffn_conv_w', 'new_m_ffn_conv_b', 'new_m_ffn_down', 'new_m_ln_mix_g', 'new_m_ln_mix_b', 'new_m_ln_ffn_g', 'new_m_ln_ffn_b', 'new_v_ssd_in_proj', 'new_v_ssd_conv_w', 'new_v_ssd_conv_b', 'new_v_ssd_dt_bias', 'new_v_ssd_A_log', 'new_v_ssd_D', 'new_v_ssd_norm_g', 'new_v_ssd_out_proj', 'new_v_kv_down_proj', 'new_v_kv_norm_g', 'new_v_kv_up_k', 'new_v_kv_up_v', 'new_v_q_down_proj', 'new_v_q_norm_g', 'new_v_q_up_proj', 'new_v_attn_out_proj', 'new_v_ffn_up', 'new_v_ffn_conv_w', 'new_v_ffn_conv_b', 'new_v_ffn_down', 'new_v_ln_mix_g', 'new_v_ln_mix_b', 'new_v_ln_ffn_g', 'new_v_ln_ffn_b']
TWIN_LEAF_KINDS = {'loss': 'loss', 'grad_x': 'grad_x', 'grad_ssd_in_proj': 'grad_w', 'grad_ssd_conv_w': 'grad_w', 'grad_ssd_conv_b': 'grad_w', 'grad_ssd_dt_bias': 'grad_w', 'grad_ssd_A_log': 'grad_w', 'grad_ssd_D': 'grad_w', 'grad_ssd_norm_g': 'grad_w', 'grad_ssd_out_proj': 'grad_w', 'grad_kv_down_proj': 'grad_w', 'grad_kv_norm_g': 'grad_w', 'grad_kv_up_k': 'grad_w', 'grad_kv_up_v': 'grad_w', 'grad_q_down_proj': 'grad_w', 'grad_q_norm_g': 'grad_w', 'grad_q_up_proj': 'grad_w', 'grad_attn_out_proj': 'grad_w', 'grad_ffn_up': 'grad_w', 'grad_ffn_conv_w': 'grad_w', 'grad_ffn_conv_b': 'grad_w', 'grad_ffn_down': 'grad_w', 'grad_ln_mix_g': 'grad_w', 'grad_ln_mix_b': 'grad_w', 'grad_ln_ffn_g': 'grad_w', 'grad_ln_ffn_b': 'grad_w', 'delta_ssd_in_proj': 'delta_w', 'delta_ssd_conv_w': 'delta_w', 'delta_ssd_conv_b': 'delta_w', 'delta_ssd_dt_bias': 'delta_w', 'delta_ssd_A_log': 'delta_w', 'delta_ssd_D': 'delta_w', 'delta_ssd_norm_g': 'delta_w', 'delta_ssd_out_proj': 'delta_w', 'delta_kv_down_proj': 'delta_w', 'delta_kv_norm_g': 'delta_w', 'delta_kv_up_k': 'delta_w', 'delta_kv_up_v': 'delta_w', 'delta_q_down_proj': 'delta_w', 'delta_q_norm_g': 'delta_w', 'delta_q_up_proj': 'delta_w', 'delta_attn_out_proj': 'delta_w', 'delta_ffn_up': 'delta_w', 'delta_ffn_conv_w': 'delta_w', 'delta_ffn_conv_b': 'delta_w', 'delta_ffn_down': 'delta_w', 'delta_ln_mix_g': 'delta_w', 'delta_ln_mix_b': 'delta_w', 'delta_ln_ffn_g': 'delta_w', 'delta_ln_ffn_b': 'delta_w', 'new_m_ssd_in_proj': 'new_m', 'new_m_ssd_conv_w': 'new_m', 'new_m_ssd_conv_b': 'new_m', 'new_m_ssd_dt_bias': 'new_m', 'new_m_ssd_A_log': 'new_m', 'new_m_ssd_D': 'new_m', 'new_m_ssd_norm_g': 'new_m', 'new_m_ssd_out_proj': 'new_m', 'new_m_kv_down_proj': 'new_m', 'new_m_kv_norm_g': 'new_m', 'new_m_kv_up_k': 'new_m', 'new_m_kv_up_v': 'new_m', 'new_m_q_down_proj': 'new_m', 'new_m_q_norm_g': 'new_m', 'new_m_q_up_proj': 'new_m', 'new_m_attn_out_proj': 'new_m', 'new_m_ffn_up': 'new_m', 'new_m_ffn_conv_w': 'new_m', 'new_m_ffn_conv_b': 'new_m', 'new_m_ffn_down': 'new_m', 'new_m_ln_mix_g': 'new_m', 'new_m_ln_mix_b': 'new_m', 'new_m_ln_ffn_g': 'new_m', 'new_m_ln_ffn_b': 'new_m', 'new_v_ssd_in_proj': 'new_v', 'new_v_ssd_conv_w': 'new_v', 'new_v_ssd_conv_b': 'new_v', 'new_v_ssd_dt_bias': 'new_v', 'new_v_ssd_A_log': 'new_v', 'new_v_ssd_D': 'new_v', 'new_v_ssd_norm_g': 'new_v', 'new_v_ssd_out_proj': 'new_v', 'new_v_kv_down_proj': 'new_v', 'new_v_kv_norm_g': 'new_v', 'new_v_kv_up_k': 'new_v', 'new_v_kv_up_v': 'new_v', 'new_v_q_down_proj': 'new_v', 'new_v_q_norm_g': 'new_v', 'new_v_q_up_proj': 'new_v', 'new_v_attn_out_proj': 'new_v', 'new_v_ffn_up': 'new_v', 'new_v_ffn_conv_w': 'new_v', 'new_v_ffn_conv_b': 'new_v', 'new_v_ffn_down': 'new_v', 'new_v_ln_mix_g': 'new_v', 'new_v_ln_mix_b': 'new_v', 'new_v_ln_ffn_g': 'new_v', 'new_v_ln_ffn_b': 'new_v'}


def _forward(args):
    return _fwd_reference(*[args[k] for k in FWD_PARAMS])


def _output_shape():
    out = _jax.eval_shape(lambda: _forward(_fwd_setup_inputs(0)))
    return out.shape, out.dtype

N_MICROBATCH = 1
ADAM_LR = 0.001
ADAM_B1 = 0.9
ADAM_B2 = 0.999
ADAM_EPS = 1e-08
ADAM_WD = 0.01
ADAM_STEP = 10
PER_EXAMPLE_BATCH_AXIS = {'x': 0, 'positions': 0, 'loss_target': 0}
SHARED_INPUTS = []
_WEIGHT_DTYPES = {'ssd_in_proj': _jnp.float32, 'ssd_conv_w': _jnp.float32, 'ssd_conv_b': _jnp.float32, 'ssd_dt_bias': _jnp.float32, 'ssd_A_log': _jnp.float32, 'ssd_D': _jnp.float32, 'ssd_norm_g': _jnp.float32, 'ssd_out_proj': _jnp.float32, 'kv_down_proj': _jnp.float32, 'kv_norm_g': _jnp.float32, 'kv_up_k': _jnp.float32, 'kv_up_v': _jnp.float32, 'q_down_proj': _jnp.float32, 'q_norm_g': _jnp.float32, 'q_up_proj': _jnp.float32, 'attn_out_proj': _jnp.float32, 'ffn_up': _jnp.float32, 'ffn_conv_w': _jnp.float32, 'ffn_conv_b': _jnp.float32, 'ffn_down': _jnp.float32, 'ln_mix_g': _jnp.float32, 'ln_mix_b': _jnp.float32, 'ln_ffn_g': _jnp.float32, 'ln_ffn_b': _jnp.float32}
MOMENT_SCALE = {'ssd_in_proj': 5.369564e-02, 'ssd_conv_w': 4.623863e-02, 'ssd_conv_b': 8.391476e-02, 'ssd_dt_bias': 1.469999e-01, 'ssd_A_log': 4.766506e-01, 'ssd_D': 3.515323e-01, 'ssd_norm_g': 6.610819e-02, 'ssd_out_proj': 1.899146e-01, 'kv_down_proj': 2.395627e-02, 'kv_norm_g': 2.482299e-02, 'kv_up_k': 6.852163e-03, 'kv_up_v': 2.151082e-02, 'q_down_proj': 1.346525e-02, 'q_norm_g': 1.330270e-02, 'q_up_proj': 6.729673e-03, 'attn_out_proj': 2.144239e-02, 'ffn_up': 3.283407e-02, 'ffn_conv_w': 3.274703e-02, 'ffn_conv_b': 3.757207e-02, 'ffn_down': 1.080613e-01, 'ln_mix_g': 1.787063e+00, 'ln_mix_b': 7.983581e-01, 'ln_ffn_g': 4.530477e+01, 'ln_ffn_b': 3.065537e+00}


def _to_microbatches(a, axis):
    t = _jnp.moveaxis(a, axis, 0)
    t = t.reshape((N_MICROBATCH, t.shape[0] // N_MICROBATCH) + t.shape[1:])
    return _jnp.moveaxis(t, 1, axis + 1)


def setup_inputs(seed: int = 0) -> dict:
    inp = _fwd_setup_inputs(seed)
    key = _jax.random.fold_in(_jax.random.key(seed), 7919)
    shape, _ = _output_shape()
    out = dict(inp)
    out["loss_target"] = _jax.random.normal(_jax.random.fold_in(key, 0), shape, _jnp.float32)
    for i, name in enumerate(TWIN_WEIGHTS):
        w = inp[name].astype(_jnp.float32)
        if MOMENT_SCALE is None:
            s = _jnp.sqrt(_jnp.mean(_jnp.square(w)) + 1e-30)
        else:
            s = MOMENT_SCALE[name]
        km, kv = _jax.random.split(_jax.random.fold_in(key, i + 1))
        out[name] = w
        out["m_" + name] = s * _jax.random.normal(km, w.shape, _jnp.float32)
        out["v_" + name] = (s * s) * _jax.random.uniform(kv, w.shape, _jnp.float32, 0.5, 1.5)
    if N_MICROBATCH > 1:
        for name, axis in PER_EXAMPLE_BATCH_AXIS.items():
            out[name] = _to_microbatches(out[name], axis)
    return {'x': out['x'], 'positions': out['positions'], 'ssd_in_proj': out['ssd_in_proj'], 'ssd_conv_w': out['ssd_conv_w'], 'ssd_conv_b': out['ssd_conv_b'], 'ssd_dt_bias': out['ssd_dt_bias'], 'ssd_A_log': out['ssd_A_log'], 'ssd_D': out['ssd_D'], 'ssd_norm_g': out['ssd_norm_g'], 'ssd_out_proj': out['ssd_out_proj'], 'kv_down_proj': out['kv_down_proj'], 'kv_norm_g': out['kv_norm_g'], 'kv_up_k': out['kv_up_k'], 'kv_up_v': out['kv_up_v'], 'q_down_proj': out['q_down_proj'], 'q_norm_g': out['q_norm_g'], 'q_up_proj': out['q_up_proj'], 'attn_out_proj': out['attn_out_proj'], 'ffn_up': out['ffn_up'], 'ffn_conv_w': out['ffn_conv_w'], 'ffn_conv_b': out['ffn_conv_b'], 'ffn_down': out['ffn_down'], 'ln_mix_g': out['ln_mix_g'], 'ln_mix_b': out['ln_mix_b'], 'ln_ffn_g': out['ln_ffn_g'], 'ln_ffn_b': out['ln_ffn_b'], 'loss_target': out['loss_target'], 'm_ssd_in_proj': out['m_ssd_in_proj'], 'm_ssd_conv_w': out['m_ssd_conv_w'], 'm_ssd_conv_b': out['m_ssd_conv_b'], 'm_ssd_dt_bias': out['m_ssd_dt_bias'], 'm_ssd_A_log': out['m_ssd_A_log'], 'm_ssd_D': out['m_ssd_D'], 'm_ssd_norm_g': out['m_ssd_norm_g'], 'm_ssd_out_proj': out['m_ssd_out_proj'], 'm_kv_down_proj': out['m_kv_down_proj'], 'm_kv_norm_g': out['m_kv_norm_g'], 'm_kv_up_k': out['m_kv_up_k'], 'm_kv_up_v': out['m_kv_up_v'], 'm_q_down_proj': out['m_q_down_proj'], 'm_q_norm_g': out['m_q_norm_g'], 'm_q_up_proj': out['m_q_up_proj'], 'm_attn_out_proj': out['m_attn_out_proj'], 'm_ffn_up': out['m_ffn_up'], 'm_ffn_conv_w': out['m_ffn_conv_w'], 'm_ffn_conv_b': out['m_ffn_conv_b'], 'm_ffn_down': out['m_ffn_down'], 'm_ln_mix_g': out['m_ln_mix_g'], 'm_ln_mix_b': out['m_ln_mix_b'], 'm_ln_ffn_g': out['m_ln_ffn_g'], 'm_ln_ffn_b': out['m_ln_ffn_b'], 'v_ssd_in_proj': out['v_ssd_in_proj'], 'v_ssd_conv_w': out['v_ssd_conv_w'], 'v_ssd_conv_b': out['v_ssd_conv_b'], 'v_ssd_dt_bias': out['v_ssd_dt_bias'], 'v_ssd_A_log': out['v_ssd_A_log'], 'v_ssd_D': out['v_ssd_D'], 'v_ssd_norm_g': out['v_ssd_norm_g'], 'v_ssd_out_proj': out['v_ssd_out_proj'], 'v_kv_down_proj': out['v_kv_down_proj'], 'v_kv_norm_g': out['v_kv_norm_g'], 'v_kv_up_k': out['v_kv_up_k'], 'v_kv_up_v': out['v_kv_up_v'], 'v_q_down_proj': out['v_q_down_proj'], 'v_q_norm_g': out['v_q_norm_g'], 'v_q_up_proj': out['v_q_up_proj'], 'v_attn_out_proj': out['v_attn_out_proj'], 'v_ffn_up': out['v_ffn_up'], 'v_ffn_conv_w': out['v_ffn_conv_w'], 'v_ffn_conv_b': out['v_ffn_conv_b'], 'v_ffn_down': out['v_ffn_down'], 'v_ln_mix_g': out['v_ln_mix_g'], 'v_ln_mix_b': out['v_ln_mix_b'], 'v_ln_ffn_g': out['v_ln_ffn_g'], 'v_ln_ffn_b': out['v_ln_ffn_b']}


def _loss(weights, diff, rest, loss_target):
    with _jax.named_scope("forward"):
        args = {**rest, TWIN_DIFF_INPUT: diff, **{k: w.astype(_WEIGHT_DTYPES[k]) for k, w in weights.items()}}
        y = _forward(args)
    with _jax.named_scope("loss_head"):
        err = _jnp.square(y.astype(_jnp.float32) - loss_target)
        return 0.5 * _jnp.sum(_jnp.mean(err, axis=-1)) if err.ndim else 0.5 * err


def _adamw(w, g, m, v):
    m = ADAM_B1 * m + (1.0 - ADAM_B1) * g
    v = ADAM_B2 * v + (1.0 - ADAM_B2) * _jnp.square(g)
    m_hat = m / (1.0 - ADAM_B1 ** ADAM_STEP)
    v_hat = v / (1.0 - ADAM_B2 ** ADAM_STEP)
    delta = -ADAM_LR * (m_hat / (_jnp.sqrt(v_hat) + ADAM_EPS) + ADAM_WD * w)
    return delta, m, v


def reference(x, positions, ssd_in_proj, ssd_conv_w, ssd_conv_b, ssd_dt_bias, ssd_A_log, ssd_D, ssd_norm_g, ssd_out_proj, kv_down_proj, kv_norm_g, kv_up_k, kv_up_v, q_down_proj, q_norm_g, q_up_proj, attn_out_proj, ffn_up, ffn_conv_w, ffn_conv_b, ffn_down, ln_mix_g, ln_mix_b, ln_ffn_g, ln_ffn_b, loss_target, m_ssd_in_proj, m_ssd_conv_w, m_ssd_conv_b, m_ssd_dt_bias, m_ssd_A_log, m_ssd_D, m_ssd_norm_g, m_ssd_out_proj, m_kv_down_proj, m_kv_norm_g, m_kv_up_k, m_kv_up_v, m_q_down_proj, m_q_norm_g, m_q_up_proj, m_attn_out_proj, m_ffn_up, m_ffn_conv_w, m_ffn_conv_b, m_ffn_down, m_ln_mix_g, m_ln_mix_b, m_ln_ffn_g, m_ln_ffn_b, v_ssd_in_proj, v_ssd_conv_w, v_ssd_conv_b, v_ssd_dt_bias, v_ssd_A_log, v_ssd_D, v_ssd_norm_g, v_ssd_out_proj, v_kv_down_proj, v_kv_norm_g, v_kv_up_k, v_kv_up_v, v_q_down_proj, v_q_norm_g, v_q_up_proj, v_attn_out_proj, v_ffn_up, v_ffn_conv_w, v_ffn_conv_b, v_ffn_down, v_ln_mix_g, v_ln_mix_b, v_ln_ffn_g, v_ln_ffn_b):
    given = dict(x=x, positions=positions, ssd_in_proj=ssd_in_proj, ssd_conv_w=ssd_conv_w, ssd_conv_b=ssd_conv_b, ssd_dt_bias=ssd_dt_bias, ssd_A_log=ssd_A_log, ssd_D=ssd_D, ssd_norm_g=ssd_norm_g, ssd_out_proj=ssd_out_proj, kv_down_proj=kv_down_proj, kv_norm_g=kv_norm_g, kv_up_k=kv_up_k, kv_up_v=kv_up_v, q_down_proj=q_down_proj, q_norm_g=q_norm_g, q_up_proj=q_up_proj, attn_out_proj=attn_out_proj, ffn_up=ffn_up, ffn_conv_w=ffn_conv_w, ffn_conv_b=ffn_conv_b, ffn_down=ffn_down, ln_mix_g=ln_mix_g, ln_mix_b=ln_mix_b, ln_ffn_g=ln_ffn_g, ln_ffn_b=ln_ffn_b, loss_target=loss_target, m_ssd_in_proj=m_ssd_in_proj, m_ssd_conv_w=m_ssd_conv_w, m_ssd_conv_b=m_ssd_conv_b, m_ssd_dt_bias=m_ssd_dt_bias, m_ssd_A_log=m_ssd_A_log, m_ssd_D=m_ssd_D, m_ssd_norm_g=m_ssd_norm_g, m_ssd_out_proj=m_ssd_out_proj, m_kv_down_proj=m_kv_down_proj, m_kv_norm_g=m_kv_norm_g, m_kv_up_k=m_kv_up_k, m_kv_up_v=m_kv_up_v, m_q_down_proj=m_q_down_proj, m_q_norm_g=m_q_norm_g, m_q_up_proj=m_q_up_proj, m_attn_out_proj=m_attn_out_proj, m_ffn_up=m_ffn_up, m_ffn_conv_w=m_ffn_conv_w, m_ffn_conv_b=m_ffn_conv_b, m_ffn_down=m_ffn_down, m_ln_mix_g=m_ln_mix_g, m_ln_mix_b=m_ln_mix_b, m_ln_ffn_g=m_ln_ffn_g, m_ln_ffn_b=m_ln_ffn_b, v_ssd_in_proj=v_ssd_in_proj, v_ssd_conv_w=v_ssd_conv_w, v_ssd_conv_b=v_ssd_conv_b, v_ssd_dt_bias=v_ssd_dt_bias, v_ssd_A_log=v_ssd_A_log, v_ssd_D=v_ssd_D, v_ssd_norm_g=v_ssd_norm_g, v_ssd_out_proj=v_ssd_out_proj, v_kv_down_proj=v_kv_down_proj, v_kv_norm_g=v_kv_norm_g, v_kv_up_k=v_kv_up_k, v_kv_up_v=v_kv_up_v, v_q_down_proj=v_q_down_proj, v_q_norm_g=v_q_norm_g, v_q_up_proj=v_q_up_proj, v_attn_out_proj=v_attn_out_proj, v_ffn_up=v_ffn_up, v_ffn_conv_w=v_ffn_conv_w, v_ffn_conv_b=v_ffn_conv_b, v_ffn_down=v_ffn_down, v_ln_mix_g=v_ln_mix_g, v_ln_mix_b=v_ln_mix_b, v_ln_ffn_g=v_ln_ffn_g, v_ln_ffn_b=v_ln_ffn_b)
    weights = {n: given[n] for n in TWIN_WEIGHTS}
    shared = {n: given[n] for n in SHARED_INPUTS}
    per_example = {n: given[n] for n in ['x', 'positions']}
    grad_fn = _jax.value_and_grad(_loss, argnums=(0, 1))

    def one_microbatch(ex, loss_target):
        ex = dict(ex)
        diff = ex.pop(TWIN_DIFF_INPUT)
        return grad_fn(weights, diff, {**shared, **ex}, loss_target)

    if N_MICROBATCH == 1:
        loss, (grad_w, grad_x) = one_microbatch(per_example, given["loss_target"])
    else:
        def body(carry, xs):
            loss_sum, grad_sum = carry
            l_k, (gw_k, gx_k) = one_microbatch(xs[0], xs[1])
            with _jax.named_scope("update"):
                return (loss_sum + l_k, _jax.tree.map(_jnp.add, grad_sum, gw_k)), gx_k

        init = (_jnp.zeros((), _jnp.float32), _jax.tree.map(_jnp.zeros_like, weights))
        (loss, grad_w), grad_x = _jax.lax.scan(body, init, (per_example, given["loss_target"]))
    with _jax.named_scope("update"):
        delta_w, new_m, new_v = {}, {}, {}
        for n in TWIN_WEIGHTS:
            delta_w[n], new_m[n], new_v[n] = _adamw(weights[n], grad_w[n], given["m_" + n], given["v_" + n])
    return (loss, grad_x, *[grad_w[n] for n in TWIN_WEIGHTS], *[delta_w[n] for n in TWIN_WEIGHTS],
            *[new_m[n] for n in TWIN_WEIGHTS], *[new_v[n] for n in TWIN_WEIGHTS])
```

```python
import functools
import math

import jax
import jax.numpy as jnp
from jax import lax
from jax.experimental import pallas as pl
from jax.experimental.pallas import tpu as pltpu

F32 = jnp.float32
BF16 = jnp.bfloat16
MESH = pl.DeviceIdType.MESH

N_DEV = 8
LANES = 128
VMEM_LIMIT = 48 * 1024 * 1024
LN_EPS = 1e-5
RMS_EPS = 1e-6
SSD_HEAD_DIM = 64
SSD_N_GROUPS = 8
SSD_D_STATE = 128
SSD_CHUNK = 128
MLA_NOPE = 128
MLA_ROPE = 64
MLA_V = 128
ROPE_THETA = 10000.0
ADAM_LR = 0.001
ADAM_B1 = 0.9
ADAM_B2 = 0.999
ADAM_EPS = 1e-08
ADAM_WD = 0.01
ADAM_STEP = 10
NEG_BIG = -1e30

SHARDED = (('ssd_in_proj', 2), ('ssd_conv_w', 2), ('ssd_conv_b', 1), ('ssd_norm_g', 1), ('ssd_out_proj', 1),
           ('kv_down_proj', 0), ('kv_up_k', 1), ('kv_up_v', 1), ('q_down_proj', 1), ('q_up_proj', 2),
           ('attn_out_proj', 1), ('ffn_up', 2), ('ffn_conv_w', 2), ('ffn_down', 1))
SHARDED_F32 = ('ssd_conv_w', 'ssd_conv_b', 'ssd_norm_g', 'ffn_conv_w')
REPLICATED = ('ssd_dt_bias', 'ssd_A_log', 'ssd_D', 'kv_norm_g', 'q_norm_g', 'ffn_conv_b',
              'ln_mix_g', 'ln_mix_b', 'ln_ffn_g', 'ln_ffn_b')
WEIGHTS = ('ssd_in_proj', 'ssd_conv_w', 'ssd_conv_b', 'ssd_dt_bias', 'ssd_A_log', 'ssd_D', 'ssd_norm_g',
           'ssd_out_proj', 'kv_down_proj', 'kv_norm_g', 'kv_up_k', 'kv_up_v', 'q_down_proj', 'q_norm_g',
           'q_up_proj', 'attn_out_proj', 'ffn_up', 'ffn_conv_w', 'ffn_conv_b', 'ffn_down', 'ln_mix_g',
           'ln_mix_b', 'ln_ffn_g', 'ln_ffn_b')


def _cparams():
    return pltpu.CompilerParams(vmem_limit_bytes=VMEM_LIMIT)


def _tile(n, cands):
    for c in cands:
        if n % c == 0:
            return c
    return n


def _sigmoid(x):
    return 1.0 / (1.0 + jnp.exp(-x))


def _iota(shape, axis):
    return lax.broadcasted_iota(jnp.int32, shape, axis)


def _mm(a, b, *, ta=False, tb=False, add=None, add_scale=1.0, out_dtype=F32, name):
    if ta:
        K, M = a.shape
    else:
        M, K = a.shape
    if tb:
        N, K2 = b.shape
    else:
        K2, N = b.shape
    assert K == K2, (a.shape, b.shape, ta, tb)
    tm = _tile(M, (512, 256, 128))
    tn = _tile(N, (512, 384, 256, 128))
    tk = _tile(K, (512, 256, 128))
    nk = K // tk

    def body(a_ref, b_ref, *rest):
        if add is not None:
            add_ref, o_ref, acc = rest
        else:
            o_ref, acc = rest
        k = pl.program_id(2)

        @pl.when(k == 0)
        def _():
            acc[...] = jnp.zeros_like(acc)

        if ta:
            av = a_ref[...].astype(F32).T.astype(BF16)
        else:
            av = a_ref[...].astype(BF16)
        bv = b_ref[...].astype(BF16)
        dn = (((1,), (1 if tb else 0,)), ((), ()))
        acc[...] += lax.dot_general(av, bv, dn, preferred_element_type=F32)

        @pl.when(k == nk - 1)
        def _():
            r = acc[...]
            if add is not None:
                r = r + add_scale * add_ref[...].astype(F32)
            o_ref[...] = r.astype(out_dtype)

    a_spec = (pl.BlockSpec((tk, tm), lambda i, j, k: (k, i)) if ta
              else pl.BlockSpec((tm, tk), lambda i, j, k: (i, k)))
    b_spec = (pl.BlockSpec((tn, tk), lambda i, j, k: (j, k)) if tb
              else pl.BlockSpec((tk, tn), lambda i, j, k: (k, j)))
    o_spec = pl.BlockSpec((tm, tn), lambda i, j, k: (i, j))
    in_specs = [a_spec, b_spec]
    args = [a, b]
    if add is not None:
        in_specs.append(o_spec)
        args.append(add)
    return pl.pallas_call(
        body, name=name, grid=(M // tm, N // tn, nk),
        in_specs=in_specs, out_specs=o_spec,
        out_shape=jax.ShapeDtypeStruct((M, N), out_dtype),
        scratch_shapes=[pltpu.VMEM((tm, tn), F32)],
        compiler_params=_cparams(),
    )(*args)


def _ln_fwd(h, mix, g, b, alpha, name):
    T, D = h.shape
    tm = _tile(T, (256, 128))

    def body(h_ref, m_ref, g_ref, b_ref, y_ref, s_ref):
        s = alpha * h_ref[...] + m_ref[...]
        mu = jnp.mean(s, axis=1, keepdims=True)
        xc = s - mu
        var = jnp.mean(xc * xc, axis=1, keepdims=True)
        y_ref[...] = xc * lax.rsqrt(var + LN_EPS) * g_ref[...] + b_ref[...]
        s_ref[...] = s

    row = pl.BlockSpec((tm, D), lambda i: (i, 0))
    vec = pl.BlockSpec((1, D), lambda i: (0, 0))
    return pl.pallas_call(
        body, name=name, grid=(T // tm,), in_specs=[row, row, vec, vec], out_specs=[row, row],
        out_shape=[jax.ShapeDtypeStruct((T, D), F32)] * 2, compiler_params=_cparams(),
    )(h, mix, g, b)


def _ln_bwd(dy, s, g, name):
    T, D = s.shape
    tm = _tile(T, (256, 128))

    def body(dy_ref, s_ref, g_ref, ds_ref, dg_ref, db_ref):
        @pl.when(pl.program_id(0) == 0)
        def _():
            dg_ref[...] = jnp.zeros_like(dg_ref)
            db_ref[...] = jnp.zeros_like(db_ref)

        sv = s_ref[...]
        dyv = dy_ref[...]
        mu = jnp.mean(sv, axis=1, keepdims=True)
        xc = sv - mu
        rstd = lax.rsqrt(jnp.mean(xc * xc, axis=1, keepdims=True) + LN_EPS)
        xhat = xc * rstd
        dxh = dyv * g_ref[...]
        m1 = jnp.mean(dxh, axis=1, keepdims=True)
        m2 = jnp.mean(dxh * xhat, axis=1, keepdims=True)
        ds_ref[...] = rstd * (dxh - m1 - xhat * m2)
        dg_ref[...] += jnp.sum(dyv * xhat, axis=0, keepdims=True)
        db_ref[...] += jnp.sum(dyv, axis=0, keepdims=True)

    row = pl.BlockSpec((tm, D), lambda i: (i, 0))
    vec = pl.BlockSpec((1, D), lambda i: (0, 0))
    return pl.pallas_call(
        body, name=name, grid=(T // tm,), in_specs=[row, row, vec], out_specs=[row, vec, vec],
        out_shape=[jax.ShapeDtypeStruct((T, D), F32), jax.ShapeDtypeStruct((1, D), F32),
                   jax.ShapeDtypeStruct((1, D), F32)],
        compiler_params=_cparams(),
    )(dy, s, g)


def _loss_head(y, target, name):
    T, D = y.shape
    tm = _tile(T, (256, 128))

    def body(y_ref, t_ref, sq_ref, dy_ref):
        @pl.when(pl.program_id(0) == 0)
        def _():
            sq_ref[...] = jnp.zeros_like(sq_ref)

        e = y_ref[...] - t_ref[...]
        sq_ref[...] += jnp.sum(e * e, axis=0, keepdims=True)
        dy_ref[...] = e * (1.0 / D)

    row = pl.BlockSpec((tm, D), lambda i: (i, 0))
    vec = pl.BlockSpec((1, D), lambda i: (0, 0))
    return pl.pallas_call(
        body, name=name, grid=(T // tm,), in_specs=[row, row], out_specs=[vec, row],
        out_shape=[jax.ShapeDtypeStruct((1, D), F32), jax.ShapeDtypeStruct((T, D), F32)],
        compiler_params=_cparams(),
    )(y, target)


def _shift_down(x, j):
    if j == 0:
        return x
    return jnp.where(_iota(x.shape, 0) >= j, pltpu.roll(x, j, 0), 0.0)


def _shift_up(x, j):
    if j == 0:
        return x
    n = x.shape[0]
    return jnp.where(_iota(x.shape, 0) < n - j, pltpu.roll(x, n - j, 0), 0.0)


def _conv_val(x, w_ref, b):
    width = w_ref.shape[0]
    y = b + w_ref[width - 1:width, :] * x
    for j in range(1, width):
        y = y + w_ref[width - 1 - j:width - j, :] * _shift_down(x, j)
    return y


def _conv_bwd_val(x, du, w_ref, dw_ref, db_ref):
    width = w_ref.shape[0]
    dx = w_ref[width - 1:width, :] * du
    for j in range(1, width):
        dx = dx + w_ref[width - 1 - j:width - j, :] * _shift_up(du, j)
    for j in range(width):
        dw_ref[width - 1 - j:width - j, :] += jnp.sum(du * _shift_down(x, j), axis=0, keepdims=True)
    db_ref[...] += jnp.sum(du, axis=0, keepdims=True)
    return dx


def _ffn_act_fwd(ug, uv, cwg, cwv, cbg, cbv, S, name):
    T, H = ug.shape
    Bn = T // S
    Cb = _tile(H, (128,))
    width = cwg.shape[0]

    def body(ug_ref, uv_ref, cwg_ref, cwv_ref, cbg_ref, cbv_ref, a_ref):
        g = _conv_val(ug_ref[...], cwg_ref, cbg_ref[...])
        v = _conv_val(uv_ref[...], cwv_ref, cbv_ref[...])
        a_ref[...] = (g * _sigmoid(g) * v).astype(BF16)

    blk = pl.BlockSpec((S, Cb), lambda j, b: (b, j))
    wblk = pl.BlockSpec((width, Cb), lambda j, b: (0, j))
    bblk = pl.BlockSpec((1, Cb), lambda j, b: (0, j))
    return pl.pallas_call(
        body, name=name, grid=(H // Cb, Bn), in_specs=[blk, blk, wblk, wblk, bblk, bblk], out_specs=blk,
        out_shape=jax.ShapeDtypeStruct((T, H), BF16), compiler_params=_cparams(),
    )(ug, uv, cwg, cwv, cbg, cbv)


def _ffn_act_bwd(ug, uv, da, cwg, cwv, cbg, cbv, S, name):
    T, H = ug.shape
    Bn = T // S
    Cb = _tile(H, (128,))
    width = cwg.shape[0]

    def body(ug_ref, uv_ref, da_ref, cwg_ref, cwv_ref, cbg_ref, cbv_ref,
             dug_ref, duv_ref, dcwg_ref, dcwv_ref, dcbg_ref, dcbv_ref):
        @pl.when(pl.program_id(1) == 0)
        def _():
            for r in (dcwg_ref, dcwv_ref, dcbg_ref, dcbv_ref):
                r[...] = jnp.zeros_like(r)

        xg = ug_ref[...]
        xv = uv_ref[...]
        g = _conv_val(xg, cwg_ref, cbg_ref[...])
        v = _conv_val(xv, cwv_ref, cbv_ref[...])
        dav = da_ref[...].astype(F32)
        sg = _sigmoid(g)
        dg = dav * v * sg * (1.0 + g * (1.0 - sg))
        dv = dav * g * sg
        dug_ref[...] = _conv_bwd_val(xg, dg, cwg_ref, dcwg_ref, dcbg_ref).astype(BF16)
        duv_ref[...] = _conv_bwd_val(xv, dv, cwv_ref, dcwv_ref, dcbv_ref).astype(BF16)

    blk = pl.BlockSpec((S, Cb), lambda j, b: (b, j))
    wblk = pl.BlockSpec((width, Cb), lambda j, b: (0, j))
    bblk = pl.BlockSpec((1, Cb), lambda j, b: (0, j))
    act = jax.ShapeDtypeStruct((T, H), BF16)
    wsh = jax.ShapeDtypeStruct((width, H), F32)
    bsh = jax.ShapeDtypeStruct((1, H), F32)
    return pl.pallas_call(
        body, name=name, grid=(H // Cb, Bn),
        in_specs=[blk, blk, blk, wblk, wblk, bblk, bblk],
        out_specs=[blk, blk, wblk, wblk, bblk, bblk],
        out_shape=[act, act, wsh, wsh, bsh, bsh], compiler_params=_cparams(),
    )(ug, uv, da, cwg, cwv, cbg, cbv)


def _ssd_conv_fwd(x, cw, cb, S, name):
    T, C = x.shape
    Cb = _tile(C, (256, 128))
    width = cw.shape[0]

    def body(x_ref, w_ref, b_ref, y_ref):
        u = _conv_val(x_ref[...], w_ref, b_ref[...])
        y_ref[...] = u * _sigmoid(u)

    blk = pl.BlockSpec((S, Cb), lambda j, b: (b, j))
    wblk = pl.BlockSpec((width, Cb), lambda j, b: (0, j))
    bblk = pl.BlockSpec((1, Cb), lambda j, b: (0, j))
    return pl.pallas_call(
        body, name=name, grid=(C // Cb, T // S), in_specs=[blk, wblk, bblk], out_specs=blk,
        out_shape=jax.ShapeDtypeStruct((T, C), F32), compiler_params=_cparams(),
    )(x, cw, cb)


def _ssd_conv_bwd(x, dy, cw, cb, S, name):
    T, C = x.shape
    Cb = _tile(C, (256, 128))
    width = cw.shape[0]

    def body(x_ref, dy_ref, w_ref, b_ref, dx_ref, dw_ref, db_ref):
        @pl.when(pl.program_id(1) == 0)
        def _():
            dw_ref[...] = jnp.zeros_like(dw_ref)
            db_ref[...] = jnp.zeros_like(db_ref)

        xv = x_ref[...]
        u = _conv_val(xv, w_ref, b_ref[...])
        su = _sigmoid(u)
        du = dy_ref[...] * su * (1.0 + u * (1.0 - su))
        dx_ref[...] = _conv_bwd_val(xv, du, w_ref, dw_ref, db_ref).astype(BF16)

    blk = pl.BlockSpec((S, Cb), lambda j, b: (b, j))
    wblk = pl.BlockSpec((width, Cb), lambda j, b: (0, j))
    bblk = pl.BlockSpec((1, Cb), lambda j, b: (0, j))
    return pl.pallas_call(
        body, name=name, grid=(C // Cb, T // S), in_specs=[blk, blk, wblk, bblk],
        out_specs=[blk, wblk, bblk],
        out_shape=[jax.ShapeDtypeStruct((T, C), BF16), jax.ShapeDtypeStruct((width, C), F32),
                   jax.ShapeDtypeStruct((1, C), F32)],
        compiler_params=_cparams(),
    )(x, dy, cw, cb)


def _softplus(x):
    e = jnp.exp(-jnp.abs(x))
    return jnp.maximum(x, 0.0) + jnp.where(e < 1e-4, e * (1.0 - 0.5 * e), jnp.log(1.0 + e))


def _cumsum_rows(x):
    n = x.shape[0]
    row = _iota(x.shape, 0)
    k = 1
    while k < n:
        x = x + jnp.where(row >= k, pltpu.roll(x, k, 0), 0.0)
        k *= 2
    return x


def _rev_cumsum_rows(x):
    n = x.shape[0]
    row = _iota(x.shape, 0)
    k = 1
    while k < n:
        x = x + jnp.where(row < n - k, pltpu.roll(x, n - k, 0), 0.0)
        k *= 2
    return x


def _col(x, lane_ids, h):
    return jnp.sum(jnp.where(lane_ids == h, x, 0.0), axis=1, keepdims=True)


def _bdot(a, b, dims):
    return lax.dot_general(a.astype(BF16), b.astype(BF16), (dims, ((), ())), preferred_element_type=F32)


NN = ((1,), (0,))
NT = ((1,), (1,))


def _scan_specs(nc, d_inner, hpg):
    L = SSD_CHUNK
    G = SSD_N_GROUPS
    gw = hpg * SSD_HEAD_DIM
    nb = d_inner // LANES
    return dict(
        xs=pl.BlockSpec((L, gw), lambda b, g, c: (b * nc + c, g)),
        B=pl.BlockSpec((L, LANES), lambda b, g, c: (b * nc + c, nb + g)),
        C=pl.BlockSpec((L, LANES), lambda b, g, c: (b * nc + c, nb + G + g)),
        dt=pl.BlockSpec((L, LANES), lambda b, g, c: (b * nc + c, 0)),
        vec=pl.BlockSpec((1, LANES), lambda b, g, c: (0, 0)),
        gvec=pl.BlockSpec((1, gw), lambda b, g, c: (0, g)),
        grp=pl.BlockSpec((L, LANES), lambda b, g, c: (b * nc + c, g)),
        st=pl.BlockSpec((1, hpg // 2, SSD_D_STATE, LANES), lambda b, g, c: ((b * G + g) * nc + c, 0, 0, 0)),
    )


def _scan_fwd(xbc, dt_pre, dt_bias, a_log, S, d_inner, name):
    T = xbc.shape[0]
    Bn = T // S
    L = SSD_CHUNK
    G = SSD_N_GROUPS
    nc = S // L
    hpg = d_inner // SSD_HEAD_DIM // G
    pairs = hpg // 2
    gw = hpg * SSD_HEAD_DIM
    sp = _scan_specs(nc, d_inner, hpg)

    def body(xs_ref, b_ref, c_ref, dtp_ref, bias_ref, alog_ref, y_ref, st_ref, state):
        g = pl.program_id(1)

        @pl.when(pl.program_id(2) == 0)
        def _():
            state[...] = jnp.zeros_like(state)

        dt = _softplus(dtp_ref[...] + bias_ref[...])
        cum = _cumsum_rows(dt * (-jnp.exp(alog_ref[...])))
        Bm = b_ref[...]
        Cm = c_ref[...]
        Gm = _bdot(Cm, Bm, NT)
        tril = _iota((L, L), 1) <= _iota((L, L), 0)
        lane = _iota((L, LANES), 1)
        lane1 = _iota((1, LANES), 1)
        last = _iota((L, 1), 0) == L - 1
        BmT = Bm.T
        for p in range(pairs):
            S_in = state[p]
            st_ref[0, p] = S_in
            x_pair = xs_ref[:, p * LANES:(p + 1) * LANES]
            y_pair = jnp.zeros((L, LANES), F32)
            z_pair = jnp.zeros((L, LANES), F32)
            e_pair = jnp.zeros((L, LANES), F32)
            el_pair = jnp.zeros((1, LANES), F32)
            for k in range(2):
                h = g * hpg + 2 * p + k
                cum_col = _col(cum, lane, h)
                dt_col = _col(dt, lane, h)
                cb = jnp.broadcast_to(cum_col, (L, L))
                decay = jnp.exp(jnp.where(tril, cb - cb.T, NEG_BIG))
                mk = (lane >= SSD_HEAD_DIM) if k else (lane < SSD_HEAD_DIM)
                xd = jnp.where(mk, x_pair * dt_col, 0.0)
                y_pair = y_pair + _bdot(Gm * decay, xd, NN)
                cum_l = jnp.sum(jnp.where(last, cum_col, 0.0), axis=0, keepdims=True)
                e_pair = jnp.where(mk, jnp.exp(cum_col), e_pair)
                z_pair = z_pair + xd * jnp.exp(cum_l - cum_col)
                el_pair = jnp.where((lane1 >= SSD_HEAD_DIM) if k else (lane1 < SSD_HEAD_DIM), jnp.exp(cum_l), el_pair)
            y_pair = y_pair + e_pair * _bdot(Cm, S_in, NN)
            state[p] = S_in * el_pair + _bdot(BmT, z_pair, NN)
            y_ref[:, p * LANES:(p + 1) * LANES] = y_pair

    return pl.pallas_call(
        body, name=name, grid=(Bn, G, nc),
        in_specs=[sp['xs'], sp['B'], sp['C'], sp['dt'], sp['vec'], sp['vec']],
        out_specs=[sp['xs'], sp['st']],
        out_shape=[jax.ShapeDtypeStruct((T, d_inner), F32),
                   jax.ShapeDtypeStruct((Bn * G * nc, pairs, SSD_D_STATE, LANES), F32)],
        scratch_shapes=[pltpu.VMEM((pairs, SSD_D_STATE, LANES), F32)],
        compiler_params=_cparams(),
    )(xbc, xbc, xbc, dt_pre, dt_bias, a_log)


def _scan_bwd(xbc, dt_pre, dt_bias, a_log, states, dy, d_lane, S, d_inner, name):
    T = xbc.shape[0]
    Bn = T // S
    L = SSD_CHUNK
    G = SSD_N_GROUPS
    nc = S // L
    hpg = d_inner // SSD_HEAD_DIM // G
    pairs = hpg // 2
    sp = _scan_specs(nc, d_inner, hpg)

    def rev(spec):
        im = spec.index_map
        return pl.BlockSpec(spec.block_shape, lambda b, g, c: im(b, g, nc - 1 - c))

    def body(xs_ref, b_ref, c_ref, dtp_ref, bias_ref, alog_ref, st_ref, dy_ref, dl_ref,
             dxs_ref, db_ref, dc_ref, ddt_ref, da_ref, dstate):
        g = pl.program_id(1)

        @pl.when(pl.program_id(2) == 0)
        def _():
            dstate[...] = jnp.zeros_like(dstate)

        dt = _softplus(dtp_ref[...] + bias_ref[...])
        cum = _cumsum_rows(dt * (-jnp.exp(alog_ref[...])))
        Bm = b_ref[...]
        Cm = c_ref[...]
        Gm = _bdot(Cm, Bm, NT)
        tril = _iota((L, L), 1) <= _iota((L, L), 0)
        lane = _iota((L, LANES), 1)
        lane1 = _iota((1, LANES), 1)
        last = _iota((L, 1), 0) == L - 1
        CmT = Cm.T
        dG = jnp.zeros((L, L), F32)
        dB = jnp.zeros((L, LANES), F32)
        dC = jnp.zeros((L, LANES), F32)
        dcum = jnp.zeros((L, LANES), F32)
        ddt = jnp.zeros((L, LANES), F32)
        for p in range(pairs):
            S_in = st_ref[0, p]
            dS = dstate[p]
            x_pair = xs_ref[:, p * LANES:(p + 1) * LANES]
            dy_pair = dy_ref[:, p * LANES:(p + 1) * LANES]
            Q = _bdot(Cm, S_in, NN)
            dZ = _bdot(Bm, dS, NN)
            prod = dS * S_in
            e_pair = jnp.zeros((L, LANES), F32)
            z_pair = jnp.zeros((L, LANES), F32)
            el_pair = jnp.zeros((1, LANES), F32)
            dx_pair = dl_ref[:, p * LANES:(p + 1) * LANES] * dy_pair
            for k in range(2):
                h = g * hpg + 2 * p + k
                cum_col = _col(cum, lane, h)
                dt_col = _col(dt, lane, h)
                cb = jnp.broadcast_to(cum_col, (L, L))
                decay = jnp.exp(jnp.where(tril, cb - cb.T, NEG_BIG))
                mk = (lane >= SSD_HEAD_DIM) if k else (lane < SSD_HEAD_DIM)
                mk1 = (lane1 >= SSD_HEAD_DIM) if k else (lane1 < SSD_HEAD_DIM)
                xd = jnp.where(mk, x_pair * dt_col, 0.0)
                dy_k = jnp.where(mk, dy_pair, 0.0)
                Wm = Gm * decay
                dWm = jnp.where(tril, _bdot(dy_k, xd, NT), 0.0)
                dxd = _bdot(Wm.T, dy_k, NN)
                dseg = dWm * Wm
                dG = dG + dWm * decay
                dcum_col = (jnp.sum(dseg, axis=1, keepdims=True)
                            - jnp.sum(dseg.T, axis=1, keepdims=True))
                cum_l = jnp.sum(jnp.where(last, cum_col, 0.0), axis=0, keepdims=True)
                e_col = jnp.exp(cum_col)
                w_col = jnp.exp(cum_l - cum_col)
                el = jnp.exp(cum_l)
                dcum_col = dcum_col + jnp.sum(dy_k * Q, axis=1, keepdims=True) * e_col
                de_e = jnp.sum(dZ * xd, axis=1, keepdims=True) * w_col
                dcum_col = dcum_col - de_e
                dcum_l = (jnp.sum(de_e, axis=0, keepdims=True)
                          + el * jnp.sum(jnp.sum(jnp.where(mk, prod, 0.0), axis=1, keepdims=True),
                                         axis=0, keepdims=True))
                dcum_col = dcum_col + jnp.where(last, dcum_l, 0.0)
                dxd = dxd + jnp.where(mk, dZ * w_col, 0.0)
                dx_pair = dx_pair + dxd * dt_col
                ddt = jnp.where(lane == h, jnp.sum(dxd * x_pair, axis=1, keepdims=True), ddt)
                dcum = jnp.where(lane == h, dcum_col, dcum)
                e_pair = jnp.where(mk, e_col, e_pair)
                z_pair = z_pair + xd * w_col
                el_pair = jnp.where(mk1, el, el_pair)
            dQ = e_pair * dy_pair
            dC = dC + _bdot(dQ, S_in, NT)
            dB = dB + _bdot(z_pair, dS, NT)
            dstate[p] = dS * el_pair + _bdot(CmT, dQ, NN)
            dxs_ref[:, p * LANES:(p + 1) * LANES] = dx_pair
        dc_ref[...] = dC + _bdot(dG, Bm, NN)
        db_ref[...] = dB + _bdot(dG.T, Cm, NN)
        ddt_ref[...] = ddt
        da_ref[...] = _rev_cumsum_rows(dcum)

    grp = jax.ShapeDtypeStruct((T, G * LANES), F32)
    return pl.pallas_call(
        body, name=name, grid=(Bn, G, nc),
        in_specs=[rev(sp['xs']), rev(sp['B']), rev(sp['C']), rev(sp['dt']), sp['vec'], sp['vec'],
                  rev(sp['st']), rev(sp['xs']), sp['gvec']],
        out_specs=[rev(sp['xs']), rev(sp['grp']), rev(sp['grp']), rev(sp['grp']), rev(sp['grp'])],
        out_shape=[jax.ShapeDtypeStruct((T, d_inner), F32), grp, grp, grp, grp],
        scratch_shapes=[pltpu.VMEM((pairs, SSD_D_STATE, LANES), F32)],
        compiler_params=_cparams(),
    )(xbc, xbc, xbc, dt_pre, dt_bias, a_log, states, dy, d_lane)


def _dt_bwd(ddt_g, da_g, dt_pre, dt_bias, a_log, name):
    T = dt_pre.shape[0]
    G = SSD_N_GROUPS
    tm = _tile(T, (512, 256, 128))

    def body(ddt_ref, da_ref, dtp_ref, bias_ref, alog_ref, dx_ref, dbias_ref, dal_ref):
        @pl.when(pl.program_id(0) == 0)
        def _():
            dbias_ref[...] = jnp.zeros_like(dbias_ref)
            dal_ref[...] = jnp.zeros_like(dal_ref)

        ddt = ddt_ref[:, 0:LANES]
        da = da_ref[:, 0:LANES]
        for gi in range(1, G):
            ddt = ddt + ddt_ref[:, gi * LANES:(gi + 1) * LANES]
            da = da + da_ref[:, gi * LANES:(gi + 1) * LANES]
        xv = dtp_ref[...] + bias_ref[...]
        A = -jnp.exp(alog_ref[...])
        dx = (ddt + da * A) * _sigmoid(xv)
        dx_ref[...] = dx
        dbias_ref[...] += jnp.sum(dx, axis=0, keepdims=True)
        dal_ref[...] += jnp.sum(da * _softplus(xv), axis=0, keepdims=True) * A

    wide = pl.BlockSpec((tm, G * LANES), lambda i: (i, 0))
    row = pl.BlockSpec((tm, LANES), lambda i: (i, 0))
    vec = pl.BlockSpec((1, LANES), lambda i: (0, 0))
    vsh = jax.ShapeDtypeStruct((1, LANES), F32)
    return pl.pallas_call(
        body, name=name, grid=(T // tm,), in_specs=[wide, wide, row, vec, vec], out_specs=[row, vec, vec],
        out_shape=[jax.ShapeDtypeStruct((T, LANES), F32), vsh, vsh], compiler_params=_cparams(),
    )(ddt_g, da_g, dt_pre, dt_bias, a_log)


def _gate_fwd(y, xbc, z, d_lane, ng, d_inner, name):
    T = y.shape[0]
    G = SSD_N_GROUPS
    gw = d_inner // G
    tm = _tile(T, (512, 256, 128))

    def body(y_ref, x_ref, z_ref, dl_ref, ng_ref, o_ref):
        zv = z_ref[...]
        y2 = (y_ref[...] + dl_ref[...] * x_ref[...]) * (zv * _sigmoid(zv))
        r = lax.rsqrt(jnp.mean(y2 * y2, axis=1, keepdims=True) + LN_EPS)
        o_ref[...] = (y2 * r * ng_ref[...]).astype(BF16)

    blk = pl.BlockSpec((tm, gw), lambda g, i: (i, g))
    vec = pl.BlockSpec((1, gw), lambda g, i: (0, g))
    return pl.pallas_call(
        body, name=name, grid=(G, T // tm), in_specs=[blk, blk, blk, vec, vec], out_specs=blk,
        out_shape=jax.ShapeDtypeStruct((T, d_inner), BF16), compiler_params=_cparams(),
    )(y, xbc, z, d_lane, ng)


def _gate_bwd(dyn, y, xbc, z, d_lane, ng, d_inner, name):
    T = y.shape[0]
    G = SSD_N_GROUPS
    gw = d_inner // G
    tm = _tile(T, (512, 256, 128))

    def body(dyn_ref, y_ref, x_ref, z_ref, dl_ref, ng_ref, dy_ref, dz_ref, dng_ref, ddl_ref):
        @pl.when(pl.program_id(1) == 0)
        def _():
            dng_ref[...] = jnp.zeros_like(dng_ref)
            ddl_ref[...] = jnp.zeros_like(ddl_ref)

        zv = z_ref[...]
        xv = x_ref[...]
        sz = _sigmoid(zv)
        y1 = y_ref[...] + dl_ref[...] * xv
        y2 = y1 * (zv * sz)
        r = lax.rsqrt(jnp.mean(y2 * y2, axis=1, keepdims=True) + LN_EPS)
        dn = dyn_ref[...].astype(F32)
        dng_ref[...] += jnp.sum(dn * y2 * r, axis=0, keepdims=True)
        do = dn * ng_ref[...]
        dy2 = r * do - y2 * (r * r * r) * jnp.mean(do * y2, axis=1, keepdims=True)
        dz_ref[...] = (dy2 * y1 * sz * (1.0 + zv * (1.0 - sz))).astype(BF16)
        dy1 = dy2 * (zv * sz)
        dy_ref[...] = dy1
        ddl_ref[...] += jnp.sum(dy1 * xv, axis=0, keepdims=True)

    blk = pl.BlockSpec((tm, gw), lambda g, i: (i, g))
    vec = pl.BlockSpec((1, gw), lambda g, i: (0, g))
    vsh = jax.ShapeDtypeStruct((1, d_inner), F32)
    return pl.pallas_call(
        body, name=name, grid=(G, T // tm), in_specs=[blk, blk, blk, blk, vec, vec],
        out_specs=[blk, blk, vec, vec],
        out_shape=[jax.ShapeDtypeStruct((T, d_inner), F32), jax.ShapeDtypeStruct((T, d_inner), BF16), vsh, vsh],
        compiler_params=_cparams(),
    )(dyn, y, xbc, z, d_lane, ng)


def _rms_fwd(x, g, name):
    T, R = x.shape
    tm = _tile(T, (512, 256, 128))

    def body(x_ref, g_ref, y_ref):
        xv = x_ref[...]
        y_ref[...] = xv * lax.rsqrt(jnp.mean(xv * xv, axis=1, keepdims=True) + RMS_EPS) * g_ref[...]

    row = pl.BlockSpec((tm, R), lambda i: (i, 0))
    vec = pl.BlockSpec((1, R), lambda i: (0, 0))
    return pl.pallas_call(
        body, name=name, grid=(T // tm,), in_specs=[row, vec], out_specs=row,
        out_shape=jax.ShapeDtypeStruct((T, R), F32), compiler_params=_cparams(),
    )(x, g)


def _rms_bwd(dy, x, g, name):
    T, R = x.shape
    tm = _tile(T, (512, 256, 128))

    def body(dy_ref, x_ref, g_ref, dx_ref, dg_ref):
        @pl.when(pl.program_id(0) == 0)
        def _():
            dg_ref[...] = jnp.zeros_like(dg_ref)

        xv = x_ref[...]
        dyv = dy_ref[...]
        r = lax.rsqrt(jnp.mean(xv * xv, axis=1, keepdims=True) + RMS_EPS)
        dg_ref[...] += jnp.sum(dyv * xv * r, axis=0, keepdims=True)
        do = dyv * g_ref[...]
        dx_ref[...] = r * do - xv * (r * r * r) * jnp.mean(do * xv, axis=1, keepdims=True)

    row = pl.BlockSpec((tm, R), lambda i: (i, 0))
    vec = pl.BlockSpec((1, R), lambda i: (0, 0))
    return pl.pallas_call(
        body, name=name, grid=(T // tm,), in_specs=[row, row, vec], out_specs=[row, vec],
        out_shape=[jax.ShapeDtypeStruct((T, R), F32), jax.ShapeDtypeStruct((1, R), F32)],
        compiler_params=_cparams(),
    )(dy, x, g)


def _swap_halves(x):
    half = MLA_ROPE // 2
    return jnp.where(_iota(x.shape, 1) < half, pltpu.roll(x, LANES - half, 1), pltpu.roll(x, half, 1))


def _rope(x, cc, ss, *, transpose, sum_heads=False, name):
    T, W = x.shape
    nh = W // LANES
    tm = _tile(T, (512, 256, 128))
    n_out = 1 if sum_heads else nh

    def body(x_ref, cc_ref, ss_ref, o_ref):
        ccv = cc_ref[...]
        ssv = ss_ref[...]
        if sum_heads:
            xv = x_ref[:, 0:LANES]
            for hh in range(1, nh):
                xv = xv + x_ref[:, hh * LANES:(hh + 1) * LANES]
            heads = [xv]
        else:
            heads = [x_ref[:, hh * LANES:(hh + 1) * LANES] for hh in range(nh)]
        for hh, xv in enumerate(heads):
            if transpose:
                r = xv * ccv + _swap_halves(xv * ssv)
            else:
                r = xv * ccv + _swap_halves(xv) * ssv
            o_ref[:, hh * LANES:(hh + 1) * LANES] = jnp.where(_iota(r.shape, 1) < MLA_ROPE, r, 0.0)

    return pl.pallas_call(
        body, name=name, grid=(T // tm,),
        in_specs=[pl.BlockSpec((tm, W), lambda i: (i, 0)), pl.BlockSpec((tm, LANES), lambda i: (i, 0)),
                  pl.BlockSpec((tm, LANES), lambda i: (i, 0))],
        out_specs=pl.BlockSpec((tm, n_out * LANES), lambda i: (i, 0)),
        out_shape=jax.ShapeDtypeStruct((T, n_out * LANES), F32), compiler_params=_cparams(),
    )(x, cc, ss)


def _attn_scores(qn, qr, kn, kr, i, j, tq, tk, scale):
    s = (_bdot(qn, kn, NT) + _bdot(qr, kr, NT)) * scale
    qpos = i * tq + _iota((tq, tk), 0)
    kpos = j * tk + _iota((tq, tk), 1)
    return jnp.where(kpos <= qpos, s, NEG_BIG)


def _attn_fwd(qn, qr, kn, kr, v, S, scale, name):
    T, W = qn.shape
    nh = W // LANES
    Bn = T // S
    tq = tk = _tile(S, (512, 256, 128))
    nq = S // tq
    nk = S // tk

    def body(qn_ref, qr_ref, kn_ref, kr_ref, v_ref, o_ref, lse_ref, m_sc, l_sc, acc):
        i = pl.program_id(2)
        j = pl.program_id(3)

        @pl.when(j == 0)
        def _():
            m_sc[...] = jnp.full_like(m_sc, NEG_BIG)
            l_sc[...] = jnp.zeros_like(l_sc)
            acc[...] = jnp.zeros_like(acc)

        @pl.when(j <= i)
        def _():
            s = _attn_scores(qn_ref[...], qr_ref[...], kn_ref[...], kr_ref[...], i, j, tq, tk, scale)
            m_old = m_sc[...]
            m_new = jnp.maximum(m_old, jnp.max(s, axis=1, keepdims=True))
            corr = jnp.exp(m_old - m_new)
            p = jnp.exp(s - m_new)
            l_sc[...] = corr * l_sc[...] + jnp.sum(p, axis=1, keepdims=True)
            acc[...] = corr * acc[...] + _bdot(p, v_ref[...], NN)
            m_sc[...] = m_new

        @pl.when(j == nk - 1)
        def _():
            o_ref[...] = acc[...] / l_sc[...]
            lse_ref[...] = jnp.broadcast_to(m_sc[...] + jnp.log(l_sc[...]), (tq, LANES))

    qspec = pl.BlockSpec((tq, LANES), lambda b, h, i, j: (b * nq + i, h))
    kspec = pl.BlockSpec((tk, LANES), lambda b, h, i, j: (b * nk + jnp.minimum(j, i), h))
    krspec = pl.BlockSpec((tk, LANES), lambda b, h, i, j: (b * nk + jnp.minimum(j, i), 0))
    return pl.pallas_call(
        body, name=name, grid=(Bn, nh, nq, nk),
        in_specs=[qspec, qspec, kspec, krspec, kspec], out_specs=[qspec, qspec],
        out_shape=[jax.ShapeDtypeStruct((T, W), F32)] * 2,
        scratch_shapes=[pltpu.VMEM((tq, 1), F32), pltpu.VMEM((tq, 1), F32), pltpu.VMEM((tq, LANES), F32)],
        compiler_params=_cparams(),
    )(qn, qr, kn, kr, v)


def _attn_bwd_dq(qn, qr, kn, kr, v, o, do, lse, S, scale, name):
    T, W = qn.shape
    nh = W // LANES
    Bn = T // S
    tq = tk = _tile(S, (512, 256, 128))
    nq = S // tq
    nk = S // tk

    def body(qn_ref, qr_ref, kn_ref, kr_ref, v_ref, o_ref, do_ref, lse_ref, dqn_ref, dqr_ref, an, ar):
        i = pl.program_id(2)
        j = pl.program_id(3)

        @pl.when(j == 0)
        def _():
            an[...] = jnp.zeros_like(an)
            ar[...] = jnp.zeros_like(ar)

        @pl.when(j <= i)
        def _():
            s = _attn_scores(qn_ref[...], qr_ref[...], kn_ref[...], kr_ref[...], i, j, tq, tk, scale)
            p = jnp.exp(s - jnp.max(lse_ref[...], axis=1, keepdims=True))
            dov = do_ref[...]
            di = jnp.sum(dov * o_ref[...], axis=1, keepdims=True)
            ds = p * (_bdot(dov, v_ref[...], NT) - di) * scale
            an[...] += _bdot(ds, kn_ref[...], NN)
            ar[...] += _bdot(ds, kr_ref[...], NN)

        @pl.when(j == nk - 1)
        def _():
            dqn_ref[...] = an[...]
            dqr_ref[...] = ar[...]

    qspec = pl.BlockSpec((tq, LANES), lambda b, h, i, j: (b * nq + i, h))
    kspec = pl.BlockSpec((tk, LANES), lambda b, h, i, j: (b * nk + jnp.minimum(j, i), h))
    krspec = pl.BlockSpec((tk, LANES), lambda b, h, i, j: (b * nk + jnp.minimum(j, i), 0))
    return pl.pallas_call(
        body, name=name, grid=(Bn, nh, nq, nk),
        in_specs=[qspec, qspec, kspec, krspec, kspec, qspec, qspec, qspec], out_specs=[qspec, qspec],
        out_shape=[jax.ShapeDtypeStruct((T, W), F32)] * 2,
        scratch_shapes=[pltpu.VMEM((tq, LANES), F32), pltpu.VMEM((tq, LANES), F32)],
        compiler_params=_cparams(),
    )(qn, qr, kn, kr, v, o, do, lse)


def _attn_bwd_dkv(qn, qr, kn, kr, v, o, do, lse, S, scale, name):
    T, W = qn.shape
    nh = W // LANES
    Bn = T // S
    tq = tk = _tile(S, (512, 256, 128))
    nq = S // tq
    nk = S // tk

    def body(qn_ref, qr_ref, kn_ref, kr_ref, v_ref, o_ref, do_ref, lse_ref, dkn_ref, dv_ref, dkr_ref, akn, av):
        j = pl.program_id(1)
        h = pl.program_id(2)
        i = pl.program_id(3)

        @pl.when(jnp.logical_and(h == 0, i == 0))
        def _():
            dkr_ref[...] = jnp.zeros_like(dkr_ref)

        @pl.when(i == 0)
        def _():
            akn[...] = jnp.zeros_like(akn)
            av[...] = jnp.zeros_like(av)

        @pl.when(i >= j)
        def _():
            s = _attn_scores(qn_ref[...], qr_ref[...], kn_ref[...], kr_ref[...], i, j, tq, tk, scale)
            p = jnp.exp(s - jnp.max(lse_ref[...], axis=1, keepdims=True))
            dov = do_ref[...]
            di = jnp.sum(dov * o_ref[...], axis=1, keepdims=True)
            ds = p * (_bdot(dov, v_ref[...], NT) - di) * scale
            dsT = ds.T
            av[...] += _bdot(p.T, dov, NN)
            akn[...] += _bdot(dsT, qn_ref[...], NN)
            dkr_ref[...] += _bdot(dsT, qr_ref[...], NN)

        @pl.when(i == nq - 1)
        def _():
            dkn_ref[...] = akn[...]
            dv_ref[...] = av[...]

    qspec = pl.BlockSpec((tq, LANES), lambda b, j, h, i: (b * nq + jnp.maximum(i, j), h))
    kspec = pl.BlockSpec((tk, LANES), lambda b, j, h, i: (b * nk + j, h))
    krspec = pl.BlockSpec((tk, LANES), lambda b, j, h, i: (b * nk + j, 0))
    return pl.pallas_call(
        body, name=name, grid=(Bn, nk, nh, nq),
        in_specs=[qspec, qspec, kspec, krspec, kspec, qspec, qspec, qspec],
        out_specs=[kspec, kspec, krspec],
        out_shape=[jax.ShapeDtypeStruct((T, W), F32), jax.ShapeDtypeStruct((T, W), F32),
                   jax.ShapeDtypeStruct((T, LANES), F32)],
        scratch_shapes=[pltpu.VMEM((tk, LANES), F32), pltpu.VMEM((tk, LANES), F32)],
        compiler_params=_cparams(),
    )(qn, qr, kn, kr, v, o, do, lse)


def _pad_cols(w, n):
    return jnp.pad(w, ((0, 0), (0, n - w.shape[1])))


def _local_step(x, positions, w, target):
    Bn, S, D = x.shape
    T = Bn * S
    depth = w['ffn_up'].shape[0]
    alpha = (2 * depth) ** 0.25
    d_inner = w['ssd_out_proj'].shape[1]
    n_heads_ssd = d_inner // SSD_HEAD_DIM
    G = SSD_N_GROUPS
    gn = G * SSD_D_STATE
    conv_dim = d_inner + 2 * gn
    hpg = n_heads_ssd // G
    nh = D // MLA_NOPE
    kv_rank = w['kv_up_k'].shape[0]
    q_rank = w['q_down_proj'].shape[2]
    H = w['ffn_down'].shape[1]
    scale = (MLA_NOPE + MLA_ROPE) ** -0.5
    assert S % SSD_CHUNK == 0 and hpg % 2 == 0 and SSD_D_STATE == LANES and n_heads_ssd <= LANES

    X = x.reshape(T, D)
    tgt = target.reshape(T, D)

    inv_freq = 1.0 / (ROPE_THETA ** (jnp.arange(0, MLA_ROPE, 2, dtype=F32) / MLA_ROPE))
    ang = positions.reshape(T).astype(F32)[:, None] * inv_freq
    cos, sin = jnp.cos(ang), jnp.sin(ang)
    zpad = jnp.zeros((T, LANES - MLA_ROPE), F32)
    cc = jnp.concatenate([cos, cos, zpad], axis=1)
    ss = jnp.concatenate([-sin, sin, zpad], axis=1)

    w_in = w['ssd_in_proj'][0]
    w_z = w_in[:, :d_inner]
    w_xbc = w_in[:, d_inner:d_inner + conv_dim]
    w_dt = _pad_cols(w_in[:, d_inner + conv_dim:], LANES)
    w_out = w['ssd_out_proj'][0]
    ssd_cw = w['ssd_conv_w'][0]
    ssd_cb = w['ssd_conv_b']
    dt_bias = _pad_cols(w['ssd_dt_bias'], LANES)
    a_log = _pad_cols(w['ssd_A_log'], LANES)
    d_lane = jnp.repeat(w['ssd_D'], SSD_HEAD_DIM, axis=1)
    ssd_ng = w['ssd_norm_g']
    w_kvc = w['kv_down_proj'][:, :kv_rank]
    w_kvr = _pad_cols(w['kv_down_proj'][:, kv_rank:], LANES)
    kv_g = w['kv_norm_g'].reshape(1, kv_rank)
    w_uk = w['kv_up_k']
    w_uv = w['kv_up_v']
    w_qd = w['q_down_proj'][0]
    q_g = w['q_norm_g']
    qu = w['q_up_proj'][0].reshape(q_rank, nh, MLA_NOPE + MLA_ROPE)
    w_qn = qu[:, :, :MLA_NOPE].reshape(q_rank, nh * LANES)
    w_qr = jnp.pad(qu[:, :, MLA_NOPE:], ((0, 0), (0, 0), (0, LANES - MLA_ROPE))).reshape(q_rank, nh * LANES)
    w_ao = w['attn_out_proj'][0]

    def ffn_parts(i):
        return dict(wg=w['ffn_up'][i][:, :H], wv=w['ffn_up'][i][:, H:],
                    cwg=w['ffn_conv_w'][i][:, :H], cwv=w['ffn_conv_w'][i][:, H:],
                    cbg=w['ffn_conv_b'][i:i + 1, :H], cbv=w['ffn_conv_b'][i:i + 1, H:],
                    wd=w['ffn_down'][i])

    def ln(name, i):
        return w[name][i:i + 1]

    def ffn_fwd(h, i):
        f = ffn_parts(i)
        ug = _mm(h, f['wg'], name=f'ffn{i}_up_g')
        uv = _mm(h, f['wv'], name=f'ffn{i}_up_v')
        a = _ffn_act_fwd(ug, uv, f['cwg'], f['cwv'], f['cbg'], f['cbv'], S, name=f'ffn{i}_act')
        return _mm(a, f['wd'], name=f'ffn{i}_down'), (ug, uv, a)

    z = _mm(X, w_z, name='ssd_in_z')
    xbc_pre = _mm(X, w_xbc, name='ssd_in_xbc')
    dt_pre = _mm(X, w_dt, name='ssd_in_dt')
    xbc = _ssd_conv_fwd(xbc_pre, ssd_cw, ssd_cb, S, name='ssd_conv')
    y_scan, states = _scan_fwd(xbc, dt_pre, dt_bias, a_log, S, d_inner, name='ssd_scan')
    yn = _gate_fwd(y_scan, xbc, z, d_lane, ssd_ng, d_inner, name='ssd_gate')
    mix0 = _mm(yn, w_out, name='ssd_out')
    h1, s1 = _ln_fwd(X, mix0, ln('ln_mix_g', 0), ln('ln_mix_b', 0), alpha, name='ln_mix0')
    ff0, ffn0_saved = ffn_fwd(h1, 0)
    h2, s2 = _ln_fwd(h1, ff0, ln('ln_ffn_g', 0), ln('ln_ffn_b', 0), alpha, name='ln_ffn0')

    ckv = _mm(h2, w_kvc, name='kv_down_c')
    kr_pre = _mm(h2, w_kvr, name='kv_down_r')
    ckvn = _rms_fwd(ckv, kv_g, name='kv_norm')
    kr = _rope(kr_pre, cc, ss, transpose=False, name='k_rope')
    kn = _mm(ckvn, w_uk, name='kv_up_k')
    v = _mm(ckvn, w_uv, name='kv_up_v')
    cq_pre = _mm(h2, w_qd, name='q_down')
    cq = _rms_fwd(cq_pre, q_g, name='q_norm')
    qn = _mm(cq, w_qn, name='q_up_n')
    qr_pre = _mm(cq, w_qr, name='q_up_r')
    qr = _rope(qr_pre, cc, ss, transpose=False, name='q_rope')
    o, lse = _attn_fwd(qn, qr, kn, kr, v, S, scale, name='attn_fwd')
    mix1 = _mm(o, w_ao, name='attn_out')
    h3, s3 = _ln_fwd(h2, mix1, ln('ln_mix_g', 1), ln('ln_mix_b', 1), alpha, name='ln_mix1')
    ff1, ffn1_saved = ffn_fwd(h3, 1)
    h4, s4 = _ln_fwd(h3, ff1, ln('ln_ffn_g', 1), ln('ln_ffn_b', 1), alpha, name='ln_ffn1')
    sq, dh4 = _loss_head(h4, tgt, name='loss_head')

    g = {}

    def ffn_bwd(ds, h_in, saved, i):
        f = ffn_parts(i)
        ug, uv, a = saved
        d_wd = _mm(a, ds, ta=True, name=f'ffn{i}_down_dw')
        da = _mm(ds, f['wd'], tb=True, out_dtype=BF16, name=f'ffn{i}_down_dx')
        dug, duv, dcwg, dcwv, dcbg, dcbv = _ffn_act_bwd(ug, uv, da, f['cwg'], f['cwv'], f['cbg'], f['cbv'], S,
                                                        name=f'ffn{i}_act_bwd')
        d_wu = jnp.concatenate([_mm(h_in, dug, ta=True, name=f'ffn{i}_up_g_dw'),
                                _mm(h_in, duv, ta=True, name=f'ffn{i}_up_v_dw')], axis=1)
        dh = _mm(dug, f['wg'], tb=True, add=ds, add_scale=alpha, name=f'ffn{i}_up_g_dx')
        dh = _mm(duv, f['wv'], tb=True, add=dh, name=f'ffn{i}_up_v_dx')
        return dh, d_wd, d_wu, jnp.concatenate([dcwg, dcwv], axis=1), jnp.concatenate([dcbg, dcbv], axis=1)

    ds4, dg_f1, db_f1 = _ln_bwd(dh4, s4, ln('ln_ffn_g', 1), name='ln_ffn1_bwd')
    dh3, d_wd1, d_wu1, d_fcw1, d_fcb1 = ffn_bwd(ds4, h3, ffn1_saved, 1)
    ds3, dg_m1, db_m1 = _ln_bwd(dh3, s3, ln('ln_mix_g', 1), name='ln_mix1_bwd')

    g['attn_out_proj'] = _mm(o, ds3, ta=True, name='attn_out_dw')[None]
    do = _mm(ds3, w_ao, tb=True, name='attn_out_dx')
    dqn, dqr = _attn_bwd_dq(qn, qr, kn, kr, v, o, do, lse, S, scale, name='attn_bwd_dq')
    dkn, dv, dkr = _attn_bwd_dkv(qn, qr, kn, kr, v, o, do, lse, S, scale, name='attn_bwd_dkv')
    dqr_pre = _rope(dqr, cc, ss, transpose=True, name='q_rope_bwd')
    dkr_pre = _rope(dkr, cc, ss, transpose=True, name='k_rope_bwd')

    d_wqn = _mm(cq, dqn, ta=True, name='q_up_n_dw').reshape(q_rank, nh, LANES)
    d_wqr = _mm(cq, dqr_pre, ta=True, name='q_up_r_dw').reshape(q_rank, nh, LANES)[:, :, :MLA_ROPE]
    g['q_up_proj'] = jnp.concatenate([d_wqn, d_wqr], axis=2).reshape(1, q_rank, nh * (MLA_NOPE + MLA_ROPE))
    dcq = _mm(dqn, w_qn, tb=True, name='q_up_n_dx')
    dcq = _mm(dqr_pre, w_qr, tb=True, add=dcq, name='q_up_r_dx')
    dcq_pre, d_qg = _rms_bwd(dcq, cq_pre, q_g, name='q_norm_bwd')
    g['q_norm_g'] = d_qg
    g['q_down_proj'] = _mm(h2, dcq_pre, ta=True, name='q_down_dw')[None]

    g['kv_up_k'] = _mm(ckvn, dkn, ta=True, name='kv_up_k_dw')
    g['kv_up_v'] = _mm(ckvn, dv, ta=True, name='kv_up_v_dw')
    dckvn = _mm(dkn, w_uk, tb=True, name='kv_up_k_dx')
    dckvn = _mm(dv, w_uv, tb=True, add=dckvn, name='kv_up_v_dx')
    dckv, d_kvg = _rms_bwd(dckvn, ckv, kv_g, name='kv_norm_bwd')
    g['kv_norm_g'] = d_kvg.reshape(kv_rank)
    g['kv_down_proj'] = jnp.concatenate(
        [_mm(h2, dckv, ta=True, name='kv_down_c_dw'),
         _mm(h2, dkr_pre, ta=True, name='kv_down_r_dw')[:, :MLA_ROPE]], axis=1)

    dh2 = _mm(dcq_pre, w_qd, tb=True, add=ds3, add_scale=alpha, name='q_down_dx')
    dh2 = _mm(dckv, w_kvc, tb=True, add=dh2, name='kv_down_c_dx')
    dh2 = _mm(dkr_pre, w_kvr, tb=True, add=dh2, name='kv_down_r_dx')

    ds2, dg_f0, db_f0 = _ln_bwd(dh2, s2, ln('ln_ffn_g', 0), name='ln_ffn0_bwd')
    dh1, d_wd0, d_wu0, d_fcw0, d_fcb0 = ffn_bwd(ds2, h1, ffn0_saved, 0)
    ds1, dg_m0, db_m0 = _ln_bwd(dh1, s1, ln('ln_mix_g', 0), name='ln_mix0_bwd')

    g['ssd_out_proj'] = _mm(yn, ds1, ta=True, name='ssd_out_dw')[None]
    dyn = _mm(ds1, w_out, tb=True, out_dtype=BF16, name='ssd_out_dx')
    dy1, dz, d_ng, d_dl = _gate_bwd(dyn, y_scan, xbc, z, d_lane, ssd_ng, d_inner, name='ssd_gate_bwd')
    dxs, dB, dC, ddt_g, da_g = _scan_bwd(xbc, dt_pre, dt_bias, a_log, states, dy1, d_lane, S, d_inner,
                                         name='ssd_scan_bwd')
    ddt_pre, d_bias, d_alog = _dt_bwd(ddt_g, da_g, dt_pre, dt_bias, a_log, name='ssd_dt_bwd')
    dxbc = jnp.concatenate([dxs, dB, dC], axis=1)
    dxbc_pre, d_scw, d_scb = _ssd_conv_bwd(xbc_pre, dxbc, ssd_cw, ssd_cb, S, name='ssd_conv_bwd')
    g['ssd_in_proj'] = jnp.concatenate(
        [_mm(X, dz, ta=True, name='ssd_in_z_dw'), _mm(X, dxbc_pre, ta=True, name='ssd_in_xbc_dw'),
         _mm(X, ddt_pre, ta=True, name='ssd_in_dt_dw')[:, :n_heads_ssd]], axis=1)[None]
    dx = _mm(dz, w_z, tb=True, add=ds1, add_scale=alpha, name='ssd_in_z_dx')
    dx = _mm(dxbc_pre, w_xbc, tb=True, add=dx, name='ssd_in_xbc_dx')
    dx = _mm(ddt_pre, w_dt, tb=True, add=dx, name='ssd_in_dt_dx')

    g['ssd_conv_w'] = d_scw[None]
    g['ssd_conv_b'] = d_scb
    g['ssd_dt_bias'] = d_bias[:, :n_heads_ssd]
    g['ssd_A_log'] = d_alog[:, :n_heads_ssd]
    g['ssd_D'] = jnp.sum(d_dl.reshape(1, n_heads_ssd, SSD_HEAD_DIM), axis=2)
    g['ssd_norm_g'] = d_ng
    g['ffn_up'] = jnp.stack([d_wu0, d_wu1])
    g['ffn_conv_w'] = jnp.stack([d_fcw0, d_fcw1])
    g['ffn_conv_b'] = jnp.concatenate([d_fcb0, d_fcb1], axis=0)
    g['ffn_down'] = jnp.stack([d_wd0, d_wd1])
    g['ln_mix_g'] = jnp.concatenate([dg_m0, dg_m1], axis=0)
    g['ln_mix_b'] = jnp.concatenate([db_m0, db_m1], axis=0)
    g['ln_ffn_g'] = jnp.concatenate([dg_f0, dg_f1], axis=0)
    g['ln_ffn_b'] = jnp.concatenate([db_f0, db_f1], axis=0)
    return sq, dx.reshape(Bn, S, D), g


ANY = pl.BlockSpec(memory_space=pl.ANY)


def _all_gather(xs, name):
    def body(x_ref, out_ref, send_sems, recv_sems, local_sem):
        x, y, c = lax.axis_index("x"), lax.axis_index("y"), lax.axis_index("c")
        me, sibling = (x, y, c), (x, y, 1 - c)
        chips = [(1 - x, y), (x, 1 - y), (1 - x, 1 - y)]

        def rows(px, py, pc):
            return out_ref.at[4 * px + 2 * py + pc]

        def copy(k, block, to, src=None):
            return pltpu.make_async_remote_copy(
                src_ref=rows(*block) if src is None else src, dst_ref=rows(*block),
                send_sem=send_sems.at[k], recv_sem=recv_sems.at[k], device_id=to, device_id_type=MESH)

        mine = pltpu.make_async_copy(x_ref, rows(*me), local_sem)
        mine.start()
        first = [copy(0, me, sibling, src=x_ref)]
        first += [copy(1 + j, me, (*chip, c), src=x_ref) for j, chip in enumerate(chips)]
        for cp in first:
            cp.start()
        passed = [copy(4 + j, (*chip, c), sibling) for j, chip in enumerate(chips)]
        for j, chip in enumerate(chips):
            copy(1 + j, (*chip, c), me).wait_recv()
            passed[j].start()
        copy(0, sibling, me).wait_recv()
        for j, chip in enumerate(chips):
            copy(4 + j, (*chip, 1 - c), me).wait_recv()
        for cp in first + passed:
            cp.wait_send()
        mine.wait()

    return pl.pallas_call(
        body, name=name, out_shape=jax.ShapeDtypeStruct((N_DEV,) + xs.shape, xs.dtype),
        in_specs=[ANY], out_specs=ANY,
        scratch_shapes=[pltpu.SemaphoreType.DMA((7,)), pltpu.SemaphoreType.DMA((7,)), pltpu.SemaphoreType.DMA(())],
    )(xs)


def _exchange_sibling(gfull, name):
    _, rows, lanes = gfull.shape

    def body(g_ref, r_ref, send_sems, recv_sems):
        x, y, c = lax.axis_index("x"), lax.axis_index("y"), lax.axis_index("c")
        copies = [pltpu.make_async_remote_copy(
            src_ref=g_ref.at[2 * k + (1 - c)], dst_ref=r_ref.at[k], send_sem=send_sems.at[k],
            recv_sem=recv_sems.at[k], device_id=(x, y, 1 - c), device_id_type=MESH) for k in range(4)]
        for cp in copies:
            cp.start()
        for cp in copies:
            cp.wait()

    return pl.pallas_call(
        body, name=name, out_shape=jax.ShapeDtypeStruct((4, rows, lanes), gfull.dtype),
        in_specs=[ANY], out_specs=ANY,
        scratch_shapes=[pltpu.SemaphoreType.DMA((4,)), pltpu.SemaphoreType.DMA((4,))],
    )(gfull)


def _exchange_chips(part, name):
    _, rows, lanes = part.shape

    def body(p_ref, r_ref, send_sems, recv_sems):
        x, y, c = lax.axis_index("x"), lax.axis_index("y"), lax.axis_index("c")
        chips = [(1 - x, y), (x, 1 - y), (1 - x, 1 - y)]
        copies = [pltpu.make_async_remote_copy(
            src_ref=p_ref.at[2 * cx + cy], dst_ref=r_ref.at[j], send_sem=send_sems.at[j],
            recv_sem=recv_sems.at[j], device_id=(cx, cy, c), device_id_type=MESH)
            for j, (cx, cy) in enumerate(chips)]
        for cp in copies:
            cp.start()
        for cp in copies:
            cp.wait()

    return pl.pallas_call(
        body, name=name, out_shape=jax.ShapeDtypeStruct((3, rows, lanes), part.dtype),
        in_specs=[ANY], out_specs=ANY,
        scratch_shapes=[pltpu.SemaphoreType.DMA((3,)), pltpu.SemaphoreType.DMA((3,))],
    )(part)


def _add_sibling(gfull, r1, core, name):
    _, rows, lanes = gfull.shape
    tr = _tile(rows, (2048, 1024, 512, 256, 128, 64, 32, 16, 8))

    def body(c_ref, g_ref, r_ref, o_ref):
        o_ref[...] = g_ref[...] + r_ref[...]

    return pl.pallas_call(
        body, name=name,
        grid_spec=pltpu.PrefetchScalarGridSpec(
            num_scalar_prefetch=1, grid=(4, rows // tr),
            in_specs=[pl.BlockSpec((1, tr, lanes), lambda k, r, c_ref: (2 * k + c_ref[0], r, 0)),
                      pl.BlockSpec((1, tr, lanes), lambda k, r, c_ref: (k, r, 0))],
            out_specs=pl.BlockSpec((1, tr, lanes), lambda k, r, c_ref: (k, r, 0))),
        out_shape=jax.ShapeDtypeStruct((4, rows, lanes), F32), compiler_params=_cparams(),
    )(core, gfull, r1)


def _adam_math(wv, gv, mv, vv):
    m = ADAM_B1 * mv + (1.0 - ADAM_B1) * gv
    v = ADAM_B2 * vv + (1.0 - ADAM_B2) * (gv * gv)
    m_hat = m / (1.0 - ADAM_B1 ** ADAM_STEP)
    v_hat = v / (1.0 - ADAM_B2 ** ADAM_STEP)
    delta = -ADAM_LR * (m_hat / (jnp.sqrt(v_hat) + ADAM_EPS) + ADAM_WD * wv)
    return delta, m, v


def _adam_sharded(part, r2, chip, wts, m, v, name):
    rows, lanes = wts.shape
    tr = _tile(rows, (1024, 512, 256, 128, 64, 32, 16, 8))

    def body(k_ref, p_ref, r_ref, w_ref, m_ref, v_ref, g_ref, d_ref, mo_ref, vo_ref):
        gv = p_ref[0] + r_ref[0] + r_ref[1] + r_ref[2]
        delta, mn, vn = _adam_math(w_ref[...], gv, m_ref[...], v_ref[...])
        g_ref[...] = gv
        d_ref[...] = delta
        mo_ref[...] = mn
        vo_ref[...] = vn

    flat = pl.BlockSpec((tr, lanes), lambda r, k_ref: (r, 0))
    sh = jax.ShapeDtypeStruct((rows, lanes), F32)
    return pl.pallas_call(
        body, name=name,
        grid_spec=pltpu.PrefetchScalarGridSpec(
            num_scalar_prefetch=1, grid=(rows // tr,),
            in_specs=[pl.BlockSpec((1, tr, lanes), lambda r, k_ref: (k_ref[0], r, 0)),
                      pl.BlockSpec((3, tr, lanes), lambda r, k_ref: (0, r, 0)), flat, flat, flat],
            out_specs=[flat, flat, flat, flat]),
        out_shape=[sh, sh, sh, sh], compiler_params=_cparams(),
    )(chip, part, r2, wts, m, v)


def _adam_replicated(gall, wts, m, v, name):
    rows, lanes = wts.shape

    def body(ga_ref, w_ref, m_ref, v_ref, g_ref, d_ref, mo_ref, vo_ref):
        gv = ga_ref[0]
        for k in range(1, N_DEV):
            gv = gv + ga_ref[k]
        delta, mn, vn = _adam_math(w_ref[...], gv, m_ref[...], v_ref[...])
        g_ref[...] = gv
        d_ref[...] = delta
        mo_ref[...] = mn
        vo_ref[...] = vn

    sh = jax.ShapeDtypeStruct((rows, lanes), F32)
    return pl.pallas_call(body, name=name, out_shape=[sh, sh, sh, sh], compiler_params=_cparams())(gall, wts, m, v)


def _pack(arrs, dtype, row_mult):
    flat = jnp.concatenate([a.reshape(-1).astype(dtype) for a in arrs])
    n = flat.shape[0]
    unit = LANES * row_mult
    total = -(-n // unit) * unit
    return jnp.pad(flat, (0, total - n)).reshape(total // LANES, LANES)


def _unpack(flat2d, shapes):
    flat = flat2d.reshape(-1)
    out, off = [], 0
    for shp in shapes:
        n = math.prod(shp)
        out.append(flat[off:off + n].reshape(shp))
        off += n
    return out


def _unpack_gathered(gathered, shapes, axes):
    flat = gathered.reshape(N_DEV, -1)
    out, off = [], 0
    for shp, ax in zip(shapes, axes):
        n = math.prod(shp)
        seg = flat[:, off:off + n].reshape((N_DEV,) + tuple(shp))
        out.append(jnp.concatenate([seg[i] for i in range(N_DEV)], axis=ax))
        off += n
    return out


def _pack_chunks(fulls, axes, row_mult):
    per_dev = []
    for full, ax in zip(fulls, axes):
        parts = jnp.stack(jnp.split(full, N_DEV, axis=ax))
        per_dev.append(parts.reshape(N_DEV, -1))
    flat = jnp.concatenate(per_dev, axis=1)
    n = flat.shape[1]
    unit = LANES * row_mult
    total = -(-n // unit) * unit
    return jnp.pad(flat, ((0, 0), (0, total - n))).reshape(N_DEV, total // LANES, LANES)


def kernel(x, positions, ssd_in_proj, ssd_conv_w, ssd_conv_b, ssd_dt_bias, ssd_A_log, ssd_D, ssd_norm_g, ssd_out_proj, kv_down_proj, kv_norm_g, kv_up_k, kv_up_v, q_down_proj, q_norm_g, q_up_proj, attn_out_proj, ffn_up, ffn_conv_w, ffn_conv_b, ffn_down, ln_mix_g, ln_mix_b, ln_ffn_g, ln_ffn_b, loss_target, m_ssd_in_proj, m_ssd_conv_w, m_ssd_conv_b, m_ssd_dt_bias, m_ssd_A_log, m_ssd_D, m_ssd_norm_g, m_ssd_out_proj, m_kv_down_proj, m_kv_norm_g, m_kv_up_k, m_kv_up_v, m_q_down_proj, m_q_norm_g, m_q_up_proj, m_attn_out_proj, m_ffn_up, m_ffn_conv_w, m_ffn_conv_b, m_ffn_down, m_ln_mix_g, m_ln_mix_b, m_ln_ffn_g, m_ln_ffn_b, v_ssd_in_proj, v_ssd_conv_w, v_ssd_conv_b, v_ssd_dt_bias, v_ssd_A_log, v_ssd_D, v_ssd_norm_g, v_ssd_out_proj, v_kv_down_proj, v_kv_norm_g, v_kv_up_k, v_kv_up_v, v_q_down_proj, v_q_norm_g, v_q_up_proj, v_attn_out_proj, v_ffn_up, v_ffn_conv_w, v_ffn_conv_b, v_ffn_down, v_ln_mix_g, v_ln_mix_b, v_ln_ffn_g, v_ln_ffn_b):
    given = dict(locals())
    wl = {n: given[n] for n in WEIGHTS}
    ml = {n: given['m_' + n] for n in WEIGHTS}
    vl = {n: given['v_' + n] for n in WEIGHTS}
    sharded_axis = dict(SHARDED)
    big = [n for n, _ in SHARDED if n not in SHARDED_F32]
    small = [n for n, _ in SHARDED if n in SHARDED_F32]
    sh_names = [n for n, _ in SHARDED]

    full = {n: wl[n] for n in REPLICATED}
    gat_big = _all_gather(_pack([wl[n] for n in big], BF16, 16), name='gather_weights_bf16')
    gat_small = _all_gather(_pack([wl[n] for n in small], F32, 8), name='gather_weights_f32')
    for names, gat in ((big, gat_big), (small, gat_small)):
        fulls = _unpack_gathered(gat, [wl[n].shape for n in names], [sharded_axis[n] for n in names])
        full.update(dict(zip(names, fulls)))

    sq, grad_x, grads = _local_step(x, positions, full, loss_target)
    loss = lax.psum(0.5 * jnp.sum(sq) / x.shape[-1], ("x", "y", "c"))

    cx, cy, cc = lax.axis_index("x"), lax.axis_index("y"), lax.axis_index("c")
    core = jnp.reshape(cc, (1,)).astype(jnp.int32)
    chip = jnp.reshape(2 * cx + cy, (1,)).astype(jnp.int32)
    gfull = _pack_chunks([grads[n] for n in sh_names], [sharded_axis[n] for n in sh_names], 8)
    r1 = _exchange_sibling(gfull, name='grads_to_sibling')
    part = _add_sibling(gfull, r1, core, name='grads_add_sibling')
    r2 = _exchange_chips(part, name='grads_to_chips')
    shapes = [wl[n].shape for n in sh_names]
    outs = _adam_sharded(part, r2, chip, _pack([wl[n] for n in sh_names], F32, 8),
                         _pack([ml[n] for n in sh_names], F32, 8), _pack([vl[n] for n in sh_names], F32, 8),
                         name='adamw_sharded')
    res = {}
    for kind, packed in zip(('grad', 'delta', 'new_m', 'new_v'), outs):
        for n, a in zip(sh_names, _unpack(packed, shapes)):
            res[kind, n] = a

    rep = list(REPLICATED)
    rshapes = [wl[n].shape for n in rep]
    gall = _all_gather(_pack([grads[n] for n in rep], F32, 8), name='gather_replicated_grads')
    outs = _adam_replicated(gall, _pack([wl[n] for n in rep], F32, 8), _pack([ml[n] for n in rep], F32, 8),
                            _pack([vl[n] for n in rep], F32, 8), name='adamw_replicated')
    for kind, packed in zip(('grad', 'delta', 'new_m', 'new_v'), outs):
        for n, a in zip(rep, _unpack(packed, rshapes)):
            res[kind, n] = a

    return (loss, grad_x, *[res['grad', n] for n in WEIGHTS], *[res['delta', n] for n in WEIGHTS],
            *[res['new_m', n] for n in WEIGHTS], *[res['new_v', n] for n in WEIGHTS])
```

```python
import functools
import math

import jax
import jax.numpy as jnp
from jax import lax
from jax.experimental import pallas as pl
from jax.experimental.pallas import tpu as pltpu

F32 = jnp.float32
BF16 = jnp.bfloat16
MESH = pl.DeviceIdType.MESH

N_DEV = 8
LANES = 128
MM_TILES = (1024, 1408, 896, 512, 384, 256, 128)
MM_VMEM_BUDGET = 30 * 1024 * 1024
PACK_ROWS = 1024
VMEM_LIMIT = 48 * 1024 * 1024
LN_EPS = 1e-5
RMS_EPS = 1e-6
SSD_HEAD_DIM = 64
SSD_N_GROUPS = 8
SSD_D_STATE = 128
SSD_CHUNK = 128
MLA_NOPE = 128
MLA_ROPE = 64
MLA_V = 128
ROPE_THETA = 10000.0
ADAM_LR = 0.001
ADAM_B1 = 0.9
ADAM_B2 = 0.999
ADAM_EPS = 1e-08
ADAM_WD = 0.01
ADAM_STEP = 10
NEG_BIG = -1e30

SHARDED = (('ssd_in_proj', 2), ('ssd_conv_w', 2), ('ssd_conv_b', 1), ('ssd_norm_g', 1), ('ssd_out_proj', 1),
           ('kv_down_proj', 0), ('kv_up_k', 1), ('kv_up_v', 1), ('q_down_proj', 1), ('q_up_proj', 2),
           ('attn_out_proj', 1), ('ffn_up', 2), ('ffn_conv_w', 2), ('ffn_down', 1))
SHARDED_F32 = ('ssd_conv_w', 'ssd_conv_b', 'ssd_norm_g', 'ffn_conv_w')
REPLICATED = ('ssd_dt_bias', 'ssd_A_log', 'ssd_D', 'kv_norm_g', 'q_norm_g', 'ffn_conv_b',
              'ln_mix_g', 'ln_mix_b', 'ln_ffn_g', 'ln_ffn_b')
WEIGHTS = ('ssd_in_proj', 'ssd_conv_w', 'ssd_conv_b', 'ssd_dt_bias', 'ssd_A_log', 'ssd_D', 'ssd_norm_g',
           'ssd_out_proj', 'kv_down_proj', 'kv_norm_g', 'kv_up_k', 'kv_up_v', 'q_down_proj', 'q_norm_g',
           'q_up_proj', 'attn_out_proj', 'ffn_up', 'ffn_conv_w', 'ffn_conv_b', 'ffn_down', 'ln_mix_g',
           'ln_mix_b', 'ln_ffn_g', 'ln_ffn_b')


def _cparams():
    return pltpu.CompilerParams(vmem_limit_bytes=VMEM_LIMIT)


def _tile(n, cands):
    for c in cands:
        if n % c == 0:
            return c
    return n


def _sigmoid(x):
    return 1.0 / (1.0 + jnp.exp(-x))


def _iota(shape, axis):
    return lax.broadcasted_iota(jnp.int32, shape, axis)


def _mm(a, b, *, ta=False, tb=False, add=None, add_scale=1.0, out_dtype=F32, name):
    if ta:
        K, M = a.shape
    else:
        M, K = a.shape
    if tb:
        N, K2 = b.shape
    else:
        K2, N = b.shape
    assert K == K2, (a.shape, b.shape, ta, tb)
    tm = _tile(M, MM_TILES)
    tn = _tile(N, MM_TILES)
    tk = K if K <= MM_TILES[0] else _tile(K, MM_TILES)
    nk = K // tk

    def vmem_estimate(tm_, tn_):
        ins = 2 * (tm_ * tk * a.dtype.itemsize + tk * tn_ * b.dtype.itemsize)
        outs = tm_ * tn_ * (2 * jnp.dtype(out_dtype).itemsize + 4 + (4 if nk > 1 else 0))
        return ins + outs + (2 * tm_ * tn_ * add.dtype.itemsize if add is not None else 0)

    while vmem_estimate(tm, tn) > MM_VMEM_BUDGET:
        can_m, can_n = tm % (2 * LANES) == 0, tn % (2 * LANES) == 0
        if can_m and (tm >= tn or not can_n):
            tm //= 2
        elif can_n:
            tn //= 2
        else:
            break

    def body(a_ref, b_ref, *rest):
        add_ref = rest[0] if add is not None else None
        o_ref = rest[1] if add is not None else rest[0]
        acc = rest[-1] if nk > 1 else None
        k = pl.program_id(2)

        if ta:
            av = a_ref[...].astype(BF16).T
        else:
            av = a_ref[...].astype(BF16)
        bv = b_ref[...].astype(BF16)
        dn = (((1,), (1 if tb else 0,)), ((), ()))
        part = lax.dot_general(av, bv, dn, preferred_element_type=F32)

        def finish(r):
            if add is not None:
                r = r + add_scale * add_ref[...].astype(F32)
            o_ref[...] = r.astype(out_dtype)

        if nk == 1:
            finish(part)
        else:
            @pl.when(k == 0)
            def _():
                acc[...] = part

            @pl.when(k > 0)
            def _():
                acc[...] += part

            @pl.when(k == nk - 1)
            def _():
                finish(acc[...])

    a_spec = (pl.BlockSpec((tk, tm), lambda i, j, k: (k, i)) if ta
              else pl.BlockSpec((tm, tk), lambda i, j, k: (i, k)))
    b_spec = (pl.BlockSpec((tn, tk), lambda i, j, k: (j, k)) if tb
              else pl.BlockSpec((tk, tn), lambda i, j, k: (k, j)))
    o_spec = pl.BlockSpec((tm, tn), lambda i, j, k: (i, j))
    in_specs = [a_spec, b_spec]
    args = [a, b]
    if add is not None:
        in_specs.append(o_spec)
        args.append(add)
    return pl.pallas_call(
        body, name=name, grid=(M // tm, N // tn, nk),
        in_specs=in_specs, out_specs=o_spec,
        out_shape=jax.ShapeDtypeStruct((M, N), out_dtype),
        scratch_shapes=[pltpu.VMEM((tm, tn), F32)] if nk > 1 else [],
        compiler_params=_cparams(),
    )(*args)


def _ln_fwd(h, mix, g, b, alpha, name):
    T, D = h.shape
    tm = _tile(T, (256, 128))

    def body(h_ref, m_ref, g_ref, b_ref, y_ref, s_ref, yb_ref):
        s = alpha * h_ref[...] + m_ref[...]
        mu = jnp.mean(s, axis=1, keepdims=True)
        xc = s - mu
        var = jnp.mean(xc * xc, axis=1, keepdims=True)
        y = xc * lax.rsqrt(var + LN_EPS) * g_ref[...] + b_ref[...]
        y_ref[...] = y
        s_ref[...] = s
        yb_ref[...] = y.astype(BF16)

    row = pl.BlockSpec((tm, D), lambda i: (i, 0))
    vec = pl.BlockSpec((1, D), lambda i: (0, 0))
    return pl.pallas_call(
        body, name=name, grid=(T // tm,), in_specs=[row, row, vec, vec], out_specs=[row, row, row],
        out_shape=[jax.ShapeDtypeStruct((T, D), F32)] * 2 + [jax.ShapeDtypeStruct((T, D), BF16)],
        compiler_params=_cparams(),
    )(h, mix, g, b)


def _ln_bwd(dy, s, g, name):
    T, D = s.shape
    tm = _tile(T, (256, 128))

    def body(dy_ref, s_ref, g_ref, ds_ref, dsb_ref, dg_ref, db_ref):
        @pl.when(pl.program_id(0) == 0)
        def _():
            dg_ref[...] = jnp.zeros_like(dg_ref)
            db_ref[...] = jnp.zeros_like(db_ref)

        sv = s_ref[...]
        dyv = dy_ref[...]
        mu = jnp.mean(sv, axis=1, keepdims=True)
        xc = sv - mu
        rstd = lax.rsqrt(jnp.mean(xc * xc, axis=1, keepdims=True) + LN_EPS)
        xhat = xc * rstd
        dxh = dyv * g_ref[...]
        m1 = jnp.mean(dxh, axis=1, keepdims=True)
        m2 = jnp.mean(dxh * xhat, axis=1, keepdims=True)
        ds = rstd * (dxh - m1 - xhat * m2)
        ds_ref[...] = ds
        dsb_ref[...] = ds.astype(BF16)
        dg_ref[...] += jnp.sum(dyv * xhat, axis=0, keepdims=True)
        db_ref[...] += jnp.sum(dyv, axis=0, keepdims=True)

    row = pl.BlockSpec((tm, D), lambda i: (i, 0))
    vec = pl.BlockSpec((1, D), lambda i: (0, 0))
    return pl.pallas_call(
        body, name=name, grid=(T // tm,), in_specs=[row, row, vec], out_specs=[row, row, vec, vec],
        out_shape=[jax.ShapeDtypeStruct((T, D), F32), jax.ShapeDtypeStruct((T, D), BF16),
                   jax.ShapeDtypeStruct((1, D), F32), jax.ShapeDtypeStruct((1, D), F32)],
        compiler_params=_cparams(),
    )(dy, s, g)


def _loss_head(y, target, name):
    T, D = y.shape
    tm = _tile(T, (256, 128))

    def body(y_ref, t_ref, sq_ref, dy_ref):
        @pl.when(pl.program_id(0) == 0)
        def _():
            sq_ref[...] = jnp.zeros_like(sq_ref)

        e = y_ref[...] - t_ref[...]
        sq_ref[...] += jnp.sum(e * e, axis=0, keepdims=True)
        dy_ref[...] = e * (1.0 / D)

    row = pl.BlockSpec((tm, D), lambda i: (i, 0))
    vec = pl.BlockSpec((1, D), lambda i: (0, 0))
    return pl.pallas_call(
        body, name=name, grid=(T // tm,), in_specs=[row, row], out_specs=[vec, row],
        out_shape=[jax.ShapeDtypeStruct((1, D), F32), jax.ShapeDtypeStruct((T, D), F32)],
        compiler_params=_cparams(),
    )(y, target)


def _shift_down(x, j):
    if j == 0:
        return x
    return jnp.where(_iota(x.shape, 0) >= j, pltpu.roll(x, j, 0), 0.0)


def _shift_up(x, j):
    if j == 0:
        return x
    n = x.shape[0]
    return jnp.where(_iota(x.shape, 0) < n - j, pltpu.roll(x, n - j, 0), 0.0)


def _conv_val(x, w_ref, b_ref, cs=slice(None)):
    width = w_ref.shape[0]
    y = b_ref[:, cs] + w_ref[width - 1:width, cs] * x
    for j in range(1, width):
        y = y + w_ref[width - 1 - j:width - j, cs] * _shift_down(x, j)
    return y


def _conv_bwd_val(x, du, w_ref, dw_ref, db_ref, cs=slice(None)):
    width = w_ref.shape[0]
    dx = w_ref[width - 1:width, cs] * du
    for j in range(1, width):
        dx = dx + w_ref[width - 1 - j:width - j, cs] * _shift_up(du, j)
    for j in range(width):
        dw_ref[width - 1 - j:width - j, cs] += jnp.sum(du * _shift_down(x, j), axis=0, keepdims=True)
    db_ref[:, cs] += jnp.sum(du, axis=0, keepdims=True)
    return dx


GATE = slice(0, LANES)
VALUE = slice(LANES, 2 * LANES)


def _interleave(a, H):
    lead = a.shape[:-1]
    return jnp.swapaxes(a.reshape(lead + (2, H // LANES, LANES)), -3, -2).reshape(lead + (2 * H,))


def _deinterleave(a, H):
    lead = a.shape[:-1]
    return jnp.swapaxes(a.reshape(lead + (H // LANES, 2, LANES)), -3, -2).reshape(lead + (2 * H,))


def _ffn_act_fwd(u, cw, cb, S, name):
    T, H2 = u.shape
    H = H2 // 2
    width = cw.shape[0]

    def body(u_ref, w_ref, b_ref, a_ref):
        g = _conv_val(u_ref[:, GATE], w_ref, b_ref, GATE)
        v = _conv_val(u_ref[:, VALUE], w_ref, b_ref, VALUE)
        a_ref[...] = (g * _sigmoid(g) * v).astype(BF16)

    ublk = pl.BlockSpec((S, 2 * LANES), lambda j, b: (b, j))
    ablk = pl.BlockSpec((S, LANES), lambda j, b: (b, j))
    wblk = pl.BlockSpec((width, 2 * LANES), lambda j, b: (0, j))
    bblk = pl.BlockSpec((1, 2 * LANES), lambda j, b: (0, j))
    return pl.pallas_call(
        body, name=name, grid=(H // LANES, T // S), in_specs=[ublk, wblk, bblk], out_specs=ablk,
        out_shape=jax.ShapeDtypeStruct((T, H), BF16), compiler_params=_cparams(),
    )(u, cw, cb)


def _ffn_act_bwd(u, da, cw, cb, S, name):
    T, H2 = u.shape
    H = H2 // 2
    width = cw.shape[0]

    def body(u_ref, da_ref, w_ref, b_ref, du_ref, dw_ref, db_ref):
        @pl.when(pl.program_id(1) == 0)
        def _():
            dw_ref[...] = jnp.zeros_like(dw_ref)
            db_ref[...] = jnp.zeros_like(db_ref)

        xg = u_ref[:, GATE]
        xv = u_ref[:, VALUE]
        g = _conv_val(xg, w_ref, b_ref, GATE)
        v = _conv_val(xv, w_ref, b_ref, VALUE)
        dav = da_ref[...].astype(F32)
        sg = _sigmoid(g)
        dg = dav * v * sg * (1.0 + g * (1.0 - sg))
        dv = dav * g * sg
        du_ref[:, GATE] = _conv_bwd_val(xg, dg, w_ref, dw_ref, db_ref, GATE).astype(BF16)
        du_ref[:, VALUE] = _conv_bwd_val(xv, dv, w_ref, dw_ref, db_ref, VALUE).astype(BF16)

    ublk = pl.BlockSpec((S, 2 * LANES), lambda j, b: (b, j))
    ablk = pl.BlockSpec((S, LANES), lambda j, b: (b, j))
    wblk = pl.BlockSpec((width, 2 * LANES), lambda j, b: (0, j))
    bblk = pl.BlockSpec((1, 2 * LANES), lambda j, b: (0, j))
    return pl.pallas_call(
        body, name=name, grid=(H // LANES, T // S), in_specs=[ublk, ablk, wblk, bblk],
        out_specs=[ublk, wblk, bblk],
        out_shape=[jax.ShapeDtypeStruct((T, H2), BF16), jax.ShapeDtypeStruct((width, H2), F32),
                   jax.ShapeDtypeStruct((1, H2), F32)],
        compiler_params=_cparams(),
    )(u, da, cw, cb)


def _ssd_conv_fwd(x, cw, cb, S, name):
    T, C = x.shape
    Cb = _tile(C, (256, 128))
    width = cw.shape[0]

    def body(x_ref, w_ref, b_ref, y_ref):
        u = _conv_val(x_ref[...], w_ref, b_ref)
        y_ref[...] = u * _sigmoid(u)

    blk = pl.BlockSpec((S, Cb), lambda j, b: (b, j))
    wblk = pl.BlockSpec((width, Cb), lambda j, b: (0, j))
    bblk = pl.BlockSpec((1, Cb), lambda j, b: (0, j))
    return pl.pallas_call(
        body, name=name, grid=(C // Cb, T // S), in_specs=[blk, wblk, bblk], out_specs=blk,
        out_shape=jax.ShapeDtypeStruct((T, C), F32), compiler_params=_cparams(),
    )(x, cw, cb)


def _ssd_conv_bwd(x, dy, cw, cb, S, name):
    T, C = x.shape
    Cb = _tile(C, (256, 128))
    width = cw.shape[0]

    def body(x_ref, dy_ref, w_ref, b_ref, dx_ref, dw_ref, db_ref):
        @pl.when(pl.program_id(1) == 0)
        def _():
            dw_ref[...] = jnp.zeros_like(dw_ref)
            db_ref[...] = jnp.zeros_like(db_ref)

        xv = x_ref[...]
        u = _conv_val(xv, w_ref, b_ref)
        su = _sigmoid(u)
        du = dy_ref[...] * su * (1.0 + u * (1.0 - su))
        dx_ref[...] = _conv_bwd_val(xv, du, w_ref, dw_ref, db_ref).astype(BF16)

    blk = pl.BlockSpec((S, Cb), lambda j, b: (b, j))
    wblk = pl.BlockSpec((width, Cb), lambda j, b: (0, j))
    bblk = pl.BlockSpec((1, Cb), lambda j, b: (0, j))
    return pl.pallas_call(
        body, name=name, grid=(C // Cb, T // S), in_specs=[blk, blk, wblk, bblk],
        out_specs=[blk, wblk, bblk],
        out_shape=[jax.ShapeDtypeStruct((T, C), BF16), jax.ShapeDtypeStruct((width, C), F32),
                   jax.ShapeDtypeStruct((1, C), F32)],
        compiler_params=_cparams(),
    )(x, dy, cw, cb)


def _softplus(x):
    e = jnp.exp(-jnp.abs(x))
    return jnp.maximum(x, 0.0) + jnp.where(e < 1e-4, e * (1.0 - 0.5 * e), jnp.log(1.0 + e))


def _cumsum_rows(x):
    n = x.shape[0]
    row = _iota(x.shape, 0)
    k = 1
    while k < n:
        x = x + jnp.where(row >= k, pltpu.roll(x, k, 0), 0.0)
        k *= 2
    return x


def _rev_cumsum_rows(x):
    n = x.shape[0]
    row = _iota(x.shape, 0)
    k = 1
    while k < n:
        x = x + jnp.where(row < n - k, pltpu.roll(x, n - k, 0), 0.0)
        k *= 2
    return x


def _col(x, lane_ids, h):
    return jnp.sum(jnp.where(lane_ids == h, x, 0.0), axis=1, keepdims=True)


def _bdot(a, b, dims):
    return lax.dot_general(a.astype(BF16), b.astype(BF16), (dims, ((), ())), preferred_element_type=F32)


NN = ((1,), (0,))
NT = ((1,), (1,))


def _scan_specs(nc, d_inner, hpg):
    L = SSD_CHUNK
    G = SSD_N_GROUPS
    gw = hpg * SSD_HEAD_DIM
    nb = d_inner // LANES
    return dict(
        xs=pl.BlockSpec((L, gw), lambda b, g, c: (b * nc + c, g)),
        B=pl.BlockSpec((L, LANES), lambda b, g, c: (b * nc + c, nb + g)),
        C=pl.BlockSpec((L, LANES), lambda b, g, c: (b * nc + c, nb + G + g)),
        dt=pl.BlockSpec((L, LANES), lambda b, g, c: (b * nc + c, 0)),
        vec=pl.BlockSpec((1, LANES), lambda b, g, c: (0, 0)),
        gvec=pl.BlockSpec((1, gw), lambda b, g, c: (0, g)),
        grp=pl.BlockSpec((L, LANES), lambda b, g, c: (b * nc + c, g)),
        st=pl.BlockSpec((1, hpg // 2, SSD_D_STATE, LANES), lambda b, g, c: ((b * G + g) * nc + c, 0, 0, 0)),
    )


def _scan_fwd(xbc, dt_pre, dt_bias, a_log, S, d_inner, name):
    T = xbc.shape[0]
    Bn = T // S
    L = SSD_CHUNK
    G = SSD_N_GROUPS
    nc = S // L
    hpg = d_inner // SSD_HEAD_DIM // G
    pairs = hpg // 2
    gw = hpg * SSD_HEAD_DIM
    sp = _scan_specs(nc, d_inner, hpg)

    def body(xs_ref, b_ref, c_ref, dtp_ref, bias_ref, alog_ref, y_ref, st_ref, state):
        g = pl.program_id(1)

        @pl.when(pl.program_id(2) == 0)
        def _():
            state[...] = jnp.zeros_like(state)

        dt = _softplus(dtp_ref[...] + bias_ref[...])
        cum = _cumsum_rows(dt * (-jnp.exp(alog_ref[...])))
        Bm = b_ref[...]
        Cm = c_ref[...]
        Gm = _bdot(Cm, Bm, NT)
        tril = _iota((L, L), 1) <= _iota((L, L), 0)
        lane = _iota((L, LANES), 1)
        lane1 = _iota((1, LANES), 1)
        last = _iota((L, 1), 0) == L - 1
        BmT = Bm.T
        for p in range(pairs):
            S_in = state[p]
            st_ref[0, p] = S_in
            x_pair = xs_ref[:, p * LANES:(p + 1) * LANES]
            y_pair = jnp.zeros((L, LANES), F32)
            z_pair = jnp.zeros((L, LANES), F32)
            e_pair = jnp.zeros((L, LANES), F32)
            el_pair = jnp.zeros((1, LANES), F32)
            for k in range(2):
                h = g * hpg + 2 * p + k
                cum_col = _col(cum, lane, h)
                dt_col = _col(dt, lane, h)
                cb = jnp.broadcast_to(cum_col, (L, L))
                decay = jnp.exp(jnp.where(tril, cb - cb.T, NEG_BIG))
                mk = (lane >= SSD_HEAD_DIM) if k else (lane < SSD_HEAD_DIM)
                xd = jnp.where(mk, x_pair * dt_col, 0.0)
                y_pair = y_pair + _bdot(Gm * decay, xd, NN)
                cum_l = jnp.sum(jnp.where(last, cum_col, 0.0), axis=0, keepdims=True)
                e_pair = jnp.where(mk, jnp.exp(cum_col), e_pair)
                z_pair = z_pair + xd * jnp.exp(cum_l - cum_col)
                el_pair = jnp.where((lane1 >= SSD_HEAD_DIM) if k else (lane1 < SSD_HEAD_DIM), jnp.exp(cum_l), el_pair)
            y_pair = y_pair + e_pair * _bdot(Cm, S_in, NN)
            state[p] = S_in * el_pair + _bdot(BmT, z_pair, NN)
            y_ref[:, p * LANES:(p + 1) * LANES] = y_pair

    return pl.pallas_call(
        body, name=name, grid=(Bn, G, nc),
        in_specs=[sp['xs'], sp['B'], sp['C'], sp['dt'], sp['vec'], sp['vec']],
        out_specs=[sp['xs'], sp['st']],
        out_shape=[jax.ShapeDtypeStruct((T, d_inner), F32),
                   jax.ShapeDtypeStruct((Bn * G * nc, pairs, SSD_D_STATE, LANES), F32)],
        scratch_shapes=[pltpu.VMEM((pairs, SSD_D_STATE, LANES), F32)],
        compiler_params=_cparams(),
    )(xbc, xbc, xbc, dt_pre, dt_bias, a_log)


def _scan_bwd(xbc, dt_pre, dt_bias, a_log, states, dy, d_lane, S, d_inner, name):
    T = xbc.shape[0]
    Bn = T // S
    L = SSD_CHUNK
    G = SSD_N_GROUPS
    nc = S // L
    hpg = d_inner // SSD_HEAD_DIM // G
    pairs = hpg // 2
    sp = _scan_specs(nc, d_inner, hpg)

    def rev(spec):
        im = spec.index_map
        return pl.BlockSpec(spec.block_shape, lambda b, g, c: im(b, g, nc - 1 - c))

    def body(xs_ref, b_ref, c_ref, dtp_ref, bias_ref, alog_ref, st_ref, dy_ref, dl_ref,
             dxs_ref, db_ref, dc_ref, ddt_ref, da_ref, dstate):
        g = pl.program_id(1)

        @pl.when(pl.program_id(2) == 0)
        def _():
            dstate[...] = jnp.zeros_like(dstate)

        dt = _softplus(dtp_ref[...] + bias_ref[...])
        cum = _cumsum_rows(dt * (-jnp.exp(alog_ref[...])))
        Bm = b_ref[...]
        Cm = c_ref[...]
        Gm = _bdot(Cm, Bm, NT)
        tril = _iota((L, L), 1) <= _iota((L, L), 0)
        lane = _iota((L, LANES), 1)
        lane1 = _iota((1, LANES), 1)
        last = _iota((L, 1), 0) == L - 1
        CmT = Cm.T
        dG = jnp.zeros((L, L), F32)
        dB = jnp.zeros((L, LANES), F32)
        dC = jnp.zeros((L, LANES), F32)
        dcum = jnp.zeros((L, LANES), F32)
        ddt = jnp.zeros((L, LANES), F32)
        for p in range(pairs):
            S_in = st_ref[0, p]
            dS = dstate[p]
            x_pair = xs_ref[:, p * LANES:(p + 1) * LANES]
            dy_pair = dy_ref[:, p * LANES:(p + 1) * LANES]
            Q = _bdot(Cm, S_in, NN)
            dZ = _bdot(Bm, dS, NN)
            prod = dS * S_in
            e_pair = jnp.zeros((L, LANES), F32)
            z_pair = jnp.zeros((L, LANES), F32)
            el_pair = jnp.zeros((1, LANES), F32)
            dx_pair = dl_ref[:, p * LANES:(p + 1) * LANES] * dy_pair
            for k in range(2):
                h = g * hpg + 2 * p + k
                cum_col = _col(cum, lane, h)
                dt_col = _col(dt, lane, h)
                cb = jnp.broadcast_to(cum_col, (L, L))
                decay = jnp.exp(jnp.where(tril, cb - cb.T, NEG_BIG))
                mk = (lane >= SSD_HEAD_DIM) if k else (lane < SSD_HEAD_DIM)
                mk1 = (lane1 >= SSD_HEAD_DIM) if k else (lane1 < SSD_HEAD_DIM)
                xd = jnp.where(mk, x_pair * dt_col, 0.0)
                dy_k = jnp.where(mk, dy_pair, 0.0)
                Wm = Gm * decay
                dWm = jnp.where(tril, _bdot(dy_k, xd, NT), 0.0)
                dxd = _bdot(Wm.T, dy_k, NN)
                dseg = dWm * Wm
                dG = dG + dWm * decay
                dcum_col = (jnp.sum(dseg, axis=1, keepdims=True)
                            - jnp.sum(dseg.T, axis=1, keepdims=True))
                cum_l = jnp.sum(jnp.where(last, cum_col, 0.0), axis=0, keepdims=True)
                e_col = jnp.exp(cum_col)
                w_col = jnp.exp(cum_l - cum_col)
                el = jnp.exp(cum_l)
                dcum_col = dcum_col + jnp.sum(dy_k * Q, axis=1, keepdims=True) * e_col
                de_e = jnp.sum(dZ * xd, axis=1, keepdims=True) * w_col
                dcum_col = dcum_col - de_e
                dcum_l = (jnp.sum(de_e, axis=0, keepdims=True)
                          + el * jnp.sum(jnp.sum(jnp.where(mk, prod, 0.0), axis=1, keepdims=True),
                                         axis=0, keepdims=True))
                dcum_col = dcum_col + jnp.where(last, dcum_l, 0.0)
                dxd = dxd + jnp.where(mk, dZ * w_col, 0.0)
                dx_pair = dx_pair + dxd * dt_col
                ddt = jnp.where(lane == h, jnp.sum(dxd * x_pair, axis=1, keepdims=True), ddt)
                dcum = jnp.where(lane == h, dcum_col, dcum)
                e_pair = jnp.where(mk, e_col, e_pair)
                z_pair = z_pair + xd * w_col
                el_pair = jnp.where(mk1, el, el_pair)
            dQ = e_pair * dy_pair
            dC = dC + _bdot(dQ, S_in, NT)
            dB = dB + _bdot(z_pair, dS, NT)
            dstate[p] = dS * el_pair + _bdot(CmT, dQ, NN)
            dxs_ref[:, p * LANES:(p + 1) * LANES] = dx_pair
        dc_ref[...] = dC + _bdot(dG, Bm, NN)
        db_ref[...] = dB + _bdot(dG.T, Cm, NN)
        ddt_ref[...] = ddt
        da_ref[...] = _rev_cumsum_rows(dcum)

    grp = jax.ShapeDtypeStruct((T, G * LANES), F32)
    return pl.pallas_call(
        body, name=name, grid=(Bn, G, nc),
        in_specs=[rev(sp['xs']), rev(sp['B']), rev(sp['C']), rev(sp['dt']), sp['vec'], sp['vec'],
                  rev(sp['st']), rev(sp['xs']), sp['gvec']],
        out_specs=[rev(sp['xs']), rev(sp['grp']), rev(sp['grp']), rev(sp['grp']), rev(sp['grp'])],
        out_shape=[jax.ShapeDtypeStruct((T, d_inner), F32), grp, grp, grp, grp],
        scratch_shapes=[pltpu.VMEM((pairs, SSD_D_STATE, LANES), F32)],
        compiler_params=_cparams(),
    )(xbc, xbc, xbc, dt_pre, dt_bias, a_log, states, dy, d_lane)


def _dt_bwd(ddt_g, da_g, dt_pre, dt_bias, a_log, name):
    T = dt_pre.shape[0]
    G = SSD_N_GROUPS
    tm = _tile(T, (512, 256, 128))

    def body(ddt_ref, da_ref, dtp_ref, bias_ref, alog_ref, dx_ref, dbias_ref, dal_ref):
        @pl.when(pl.program_id(0) == 0)
        def _():
            dbias_ref[...] = jnp.zeros_like(dbias_ref)
            dal_ref[...] = jnp.zeros_like(dal_ref)

        ddt = ddt_ref[:, 0:LANES]
        da = da_ref[:, 0:LANES]
        for gi in range(1, G):
            ddt = ddt + ddt_ref[:, gi * LANES:(gi + 1) * LANES]
            da = da + da_ref[:, gi * LANES:(gi + 1) * LANES]
        xv = dtp_ref[...] + bias_ref[...]
        A = -jnp.exp(alog_ref[...])
        dx = (ddt + da * A) * _sigmoid(xv)
        dx_ref[...] = dx.astype(BF16)
        dbias_ref[...] += jnp.sum(dx, axis=0, keepdims=True)
        dal_ref[...] += jnp.sum(da * _softplus(xv), axis=0, keepdims=True) * A

    wide = pl.BlockSpec((tm, G * LANES), lambda i: (i, 0))
    row = pl.BlockSpec((tm, LANES), lambda i: (i, 0))
    vec = pl.BlockSpec((1, LANES), lambda i: (0, 0))
    vsh = jax.ShapeDtypeStruct((1, LANES), F32)
    return pl.pallas_call(
        body, name=name, grid=(T // tm,), in_specs=[wide, wide, row, vec, vec], out_specs=[row, vec, vec],
        out_shape=[jax.ShapeDtypeStruct((T, LANES), BF16), vsh, vsh], compiler_params=_cparams(),
    )(ddt_g, da_g, dt_pre, dt_bias, a_log)


def _gate_fwd(y, xbc, z, d_lane, ng, d_inner, name):
    T = y.shape[0]
    G = SSD_N_GROUPS
    gw = d_inner // G
    tm = _tile(T, (512, 256, 128))

    def body(y_ref, x_ref, z_ref, dl_ref, ng_ref, o_ref):
        zv = z_ref[...]
        y2 = (y_ref[...] + dl_ref[...] * x_ref[...]) * (zv * _sigmoid(zv))
        r = lax.rsqrt(jnp.mean(y2 * y2, axis=1, keepdims=True) + LN_EPS)
        o_ref[...] = (y2 * r * ng_ref[...]).astype(BF16)

    blk = pl.BlockSpec((tm, gw), lambda g, i: (i, g))
    vec = pl.BlockSpec((1, gw), lambda g, i: (0, g))
    return pl.pallas_call(
        body, name=name, grid=(G, T // tm), in_specs=[blk, blk, blk, vec, vec], out_specs=blk,
        out_shape=jax.ShapeDtypeStruct((T, d_inner), BF16), compiler_params=_cparams(),
    )(y, xbc, z, d_lane, ng)


def _gate_bwd(dyn, y, xbc, z, d_lane, ng, d_inner, name):
    T = y.shape[0]
    G = SSD_N_GROUPS
    gw = d_inner // G
    tm = _tile(T, (512, 256, 128))

    def body(dyn_ref, y_ref, x_ref, z_ref, dl_ref, ng_ref, dy_ref, dz_ref, dng_ref, ddl_ref):
        @pl.when(pl.program_id(1) == 0)
        def _():
            dng_ref[...] = jnp.zeros_like(dng_ref)
            ddl_ref[...] = jnp.zeros_like(ddl_ref)

        zv = z_ref[...]
        xv = x_ref[...]
        sz = _sigmoid(zv)
        y1 = y_ref[...] + dl_ref[...] * xv
        y2 = y1 * (zv * sz)
        r = lax.rsqrt(jnp.mean(y2 * y2, axis=1, keepdims=True) + LN_EPS)
        dn = dyn_ref[...].astype(F32)
        dng_ref[...] += jnp.sum(dn * y2 * r, axis=0, keepdims=True)
        do = dn * ng_ref[...]
        dy2 = r * do - y2 * (r * r * r) * jnp.mean(do * y2, axis=1, keepdims=True)
        dz_ref[...] = (dy2 * y1 * sz * (1.0 + zv * (1.0 - sz))).astype(BF16)
        dy1 = dy2 * (zv * sz)
        dy_ref[...] = dy1
        ddl_ref[...] += jnp.sum(dy1 * xv, axis=0, keepdims=True)

    blk = pl.BlockSpec((tm, gw), lambda g, i: (i, g))
    vec = pl.BlockSpec((1, gw), lambda g, i: (0, g))
    vsh = jax.ShapeDtypeStruct((1, d_inner), F32)
    return pl.pallas_call(
        body, name=name, grid=(G, T // tm), in_specs=[blk, blk, blk, blk, vec, vec],
        out_specs=[blk, blk, vec, vec],
        out_shape=[jax.ShapeDtypeStruct((T, d_inner), F32), jax.ShapeDtypeStruct((T, d_inner), BF16), vsh, vsh],
        compiler_params=_cparams(),
    )(dyn, y, xbc, z, d_lane, ng)


def _rms_fwd(x, g, name):
    T, R = x.shape
    tm = _tile(T, (512, 256, 128))

    def body(x_ref, g_ref, y_ref):
        xv = x_ref[...]
        y = xv * lax.rsqrt(jnp.mean(xv * xv, axis=1, keepdims=True) + RMS_EPS) * g_ref[...]
        y_ref[...] = y.astype(BF16)

    row = pl.BlockSpec((tm, R), lambda i: (i, 0))
    vec = pl.BlockSpec((1, R), lambda i: (0, 0))
    return pl.pallas_call(
        body, name=name, grid=(T // tm,), in_specs=[row, vec], out_specs=row,
        out_shape=jax.ShapeDtypeStruct((T, R), BF16), compiler_params=_cparams(),
    )(x, g)


def _rms_bwd(dy, x, g, name):
    T, R = x.shape
    tm = _tile(T, (512, 256, 128))

    def body(dy_ref, x_ref, g_ref, dx_ref, dg_ref):
        @pl.when(pl.program_id(0) == 0)
        def _():
            dg_ref[...] = jnp.zeros_like(dg_ref)

        xv = x_ref[...]
        dyv = dy_ref[...]
        r = lax.rsqrt(jnp.mean(xv * xv, axis=1, keepdims=True) + RMS_EPS)
        dg_ref[...] += jnp.sum(dyv * xv * r, axis=0, keepdims=True)
        do = dyv * g_ref[...]
        dx = r * do - xv * (r * r * r) * jnp.mean(do * xv, axis=1, keepdims=True)
        dx_ref[...] = dx.astype(BF16)

    row = pl.BlockSpec((tm, R), lambda i: (i, 0))
    vec = pl.BlockSpec((1, R), lambda i: (0, 0))
    return pl.pallas_call(
        body, name=name, grid=(T // tm,), in_specs=[row, row, vec], out_specs=[row, vec],
        out_shape=[jax.ShapeDtypeStruct((T, R), BF16), jax.ShapeDtypeStruct((1, R), F32)],
        compiler_params=_cparams(),
    )(dy, x, g)


def _swap_halves(x):
    half = MLA_ROPE // 2
    return jnp.where(_iota(x.shape, 1) < half, pltpu.roll(x, LANES - half, 1), pltpu.roll(x, half, 1))


def _rope(x, cc, ss, *, transpose, sum_heads=False, name):
    T, W = x.shape
    nh = W // LANES
    tm = _tile(T, (512, 256, 128))
    n_out = 1 if sum_heads else nh

    def body(x_ref, cc_ref, ss_ref, o_ref):
        ccv = cc_ref[...]
        ssv = ss_ref[...]
        if sum_heads:
            xv = x_ref[:, 0:LANES]
            for hh in range(1, nh):
                xv = xv + x_ref[:, hh * LANES:(hh + 1) * LANES]
            heads = [xv]
        else:
            heads = [x_ref[:, hh * LANES:(hh + 1) * LANES] for hh in range(nh)]
        for hh, xv in enumerate(heads):
            if transpose:
                r = xv * ccv + _swap_halves(xv * ssv)
            else:
                r = xv * ccv + _swap_halves(xv) * ssv
            r = jnp.where(_iota(r.shape, 1) < MLA_ROPE, r, 0.0)
            o_ref[:, hh * LANES:(hh + 1) * LANES] = r.astype(BF16)

    return pl.pallas_call(
        body, name=name, grid=(T // tm,),
        in_specs=[pl.BlockSpec((tm, W), lambda i: (i, 0)), pl.BlockSpec((tm, LANES), lambda i: (i, 0)),
                  pl.BlockSpec((tm, LANES), lambda i: (i, 0))],
        out_specs=pl.BlockSpec((tm, n_out * LANES), lambda i: (i, 0)),
        out_shape=jax.ShapeDtypeStruct((T, n_out * LANES), BF16), compiler_params=_cparams(),
    )(x, cc, ss)


def _attn_scores(qn, qr, kn, kr, i, j, tq, tk, scale):
    s = (_bdot(qn, kn, NT) + _bdot(qr, kr, NT)) * scale
    qpos = i * tq + _iota((tq, tk), 0)
    kpos = j * tk + _iota((tq, tk), 1)
    return jnp.where(kpos <= qpos, s, NEG_BIG)


def _attn_fwd(qn, qr, kn, kr, v, S, scale, name):
    T, W = qn.shape
    nh = W // LANES
    Bn = T // S
    tq = tk = _tile(S, (512, 256, 128))
    nq = S // tq
    nk = S // tk

    def body(qn_ref, qr_ref, kn_ref, kr_ref, v_ref, o_ref, ob_ref, lse_ref, m_sc, l_sc, acc):
        i = pl.program_id(2)
        j = pl.program_id(3)

        @pl.when(j == 0)
        def _():
            m_sc[...] = jnp.full_like(m_sc, NEG_BIG)
            l_sc[...] = jnp.zeros_like(l_sc)
            acc[...] = jnp.zeros_like(acc)

        @pl.when(j <= i)
        def _():
            s = _attn_scores(qn_ref[...], qr_ref[...], kn_ref[...], kr_ref[...], i, j, tq, tk, scale)
            m_old = m_sc[...]
            m_new = jnp.maximum(m_old, jnp.max(s, axis=1, keepdims=True))
            corr = jnp.exp(m_old - m_new)
            p = jnp.exp(s - m_new)
            l_sc[...] = corr * l_sc[...] + jnp.sum(p, axis=1, keepdims=True)
            acc[...] = corr * acc[...] + _bdot(p, v_ref[...], NN)
            m_sc[...] = m_new

        @pl.when(j == nk - 1)
        def _():
            ov = acc[...] / l_sc[...]
            o_ref[...] = ov
            ob_ref[...] = ov.astype(BF16)
            lse_ref[...] = jnp.broadcast_to(m_sc[...] + jnp.log(l_sc[...]), (tq, LANES))

    qspec = pl.BlockSpec((tq, LANES), lambda b, h, i, j: (b * nq + i, h))
    kspec = pl.BlockSpec((tk, LANES), lambda b, h, i, j: (b * nk + jnp.minimum(j, i), h))
    krspec = pl.BlockSpec((tk, LANES), lambda b, h, i, j: (b * nk + jnp.minimum(j, i), 0))
    return pl.pallas_call(
        body, name=name, grid=(Bn, nh, nq, nk),
        in_specs=[qspec, qspec, kspec, krspec, kspec], out_specs=[qspec, qspec, qspec],
        out_shape=[jax.ShapeDtypeStruct((T, W), F32), jax.ShapeDtypeStruct((T, W), BF16),
                   jax.ShapeDtypeStruct((T, W), F32)],
        scratch_shapes=[pltpu.VMEM((tq, 1), F32), pltpu.VMEM((tq, 1), F32), pltpu.VMEM((tq, LANES), F32)],
        compiler_params=_cparams(),
    )(qn, qr, kn, kr, v)


def _attn_bwd_dq(qn, qr, kn, kr, v, o, do, lse, S, scale, name):
    T, W = qn.shape
    nh = W // LANES
    Bn = T // S
    tq = tk = _tile(S, (512, 256, 128))
    nq = S // tq
    nk = S // tk

    def body(qn_ref, qr_ref, kn_ref, kr_ref, v_ref, o_ref, do_ref, lse_ref, dqn_ref, dqr_ref, an, ar):
        i = pl.program_id(2)
        j = pl.program_id(3)

        @pl.when(j == 0)
        def _():
            an[...] = jnp.zeros_like(an)
            ar[...] = jnp.zeros_like(ar)

        @pl.when(j <= i)
        def _():
            s = _attn_scores(qn_ref[...], qr_ref[...], kn_ref[...], kr_ref[...], i, j, tq, tk, scale)
            p = jnp.exp(s - jnp.max(lse_ref[...], axis=1, keepdims=True))
            dov = do_ref[...]
            di = jnp.sum(dov * o_ref[...], axis=1, keepdims=True)
            ds = p * (_bdot(dov, v_ref[...], NT) - di) * scale
            an[...] += _bdot(ds, kn_ref[...], NN)
            ar[...] += _bdot(ds, kr_ref[...], NN)

        @pl.when(j == nk - 1)
        def _():
            dqn_ref[...] = an[...].astype(BF16)
            dqr_ref[...] = ar[...]

    qspec = pl.BlockSpec((tq, LANES), lambda b, h, i, j: (b * nq + i, h))
    kspec = pl.BlockSpec((tk, LANES), lambda b, h, i, j: (b * nk + jnp.minimum(j, i), h))
    krspec = pl.BlockSpec((tk, LANES), lambda b, h, i, j: (b * nk + jnp.minimum(j, i), 0))
    return pl.pallas_call(
        body, name=name, grid=(Bn, nh, nq, nk),
        in_specs=[qspec, qspec, kspec, krspec, kspec, qspec, qspec, qspec], out_specs=[qspec, qspec],
        out_shape=[jax.ShapeDtypeStruct((T, W), BF16), jax.ShapeDtypeStruct((T, W), F32)],
        scratch_shapes=[pltpu.VMEM((tq, LANES), F32), pltpu.VMEM((tq, LANES), F32)],
        compiler_params=_cparams(),
    )(qn, qr, kn, kr, v, o, do, lse)


def _attn_bwd_dkv(qn, qr, kn, kr, v, o, do, lse, S, scale, name):
    T, W = qn.shape
    nh = W // LANES
    Bn = T // S
    tq = tk = _tile(S, (512, 256, 128))
    nq = S // tq
    nk = S // tk

    def body(qn_ref, qr_ref, kn_ref, kr_ref, v_ref, o_ref, do_ref, lse_ref, dkn_ref, dv_ref, dkr_ref, akn, av):
        j = pl.program_id(1)
        h = pl.program_id(2)
        i = pl.program_id(3)

        @pl.when(jnp.logical_and(h == 0, i == 0))
        def _():
            dkr_ref[...] = jnp.zeros_like(dkr_ref)

        @pl.when(i == 0)
        def _():
            akn[...] = jnp.zeros_like(akn)
            av[...] = jnp.zeros_like(av)

        @pl.when(i >= j)
        def _():
            s = _attn_scores(qn_ref[...], qr_ref[...], kn_ref[...], kr_ref[...], i, j, tq, tk, scale)
            p = jnp.exp(s - jnp.max(lse_ref[...], axis=1, keepdims=True))
            dov = do_ref[...]
            di = jnp.sum(dov * o_ref[...], axis=1, keepdims=True)
            ds = p * (_bdot(dov, v_ref[...], NT) - di) * scale
            dsT = ds.T
            av[...] += _bdot(p.T, dov, NN)
            akn[...] += _bdot(dsT, qn_ref[...], NN)
            dkr_ref[...] += _bdot(dsT, qr_ref[...], NN)

        @pl.when(i == nq - 1)
        def _():
            dkn_ref[...] = akn[...].astype(BF16)
            dv_ref[...] = av[...].astype(BF16)

    qspec = pl.BlockSpec((tq, LANES), lambda b, j, h, i: (b * nq + jnp.maximum(i, j), h))
    kspec = pl.BlockSpec((tk, LANES), lambda b, j, h, i: (b * nk + j, h))
    krspec = pl.BlockSpec((tk, LANES), lambda b, j, h, i: (b * nk + j, 0))
    return pl.pallas_call(
        body, name=name, grid=(Bn, nk, nh, nq),
        in_specs=[qspec, qspec, kspec, krspec, kspec, qspec, qspec, qspec],
        out_specs=[kspec, kspec, krspec],
        out_shape=[jax.ShapeDtypeStruct((T, W), BF16), jax.ShapeDtypeStruct((T, W), BF16),
                   jax.ShapeDtypeStruct((T, LANES), F32)],
        scratch_shapes=[pltpu.VMEM((tk, LANES), F32), pltpu.VMEM((tk, LANES), F32)],
        compiler_params=_cparams(),
    )(qn, qr, kn, kr, v, o, do, lse)


def _pad_cols(w, n):
    return jnp.pad(w, ((0, 0), (0, n - w.shape[1])))


def _local_step(x, positions, w, target):
    Bn, S, D = x.shape
    T = Bn * S
    depth = w['ffn_up'].shape[0]
    alpha = (2 * depth) ** 0.25
    d_inner = w['ssd_out_proj'].shape[1]
    n_heads_ssd = d_inner // SSD_HEAD_DIM
    G = SSD_N_GROUPS
    gn = G * SSD_D_STATE
    conv_dim = d_inner + 2 * gn
    hpg = n_heads_ssd // G
    nh = D // MLA_NOPE
    kv_rank = w['kv_up_k'].shape[0]
    q_rank = w['q_down_proj'].shape[2]
    H = w['ffn_down'].shape[1]
    scale = (MLA_NOPE + MLA_ROPE) ** -0.5
    assert S % SSD_CHUNK == 0 and hpg % 2 == 0 and SSD_D_STATE == LANES and n_heads_ssd <= LANES

    X = x.reshape(T, D)
    tgt = target.reshape(T, D)

    inv_freq = 1.0 / (ROPE_THETA ** (jnp.arange(0, MLA_ROPE, 2, dtype=F32) / MLA_ROPE))
    ang = positions.reshape(T).astype(F32)[:, None] * inv_freq
    cos, sin = jnp.cos(ang), jnp.sin(ang)
    zpad = jnp.zeros((T, LANES - MLA_ROPE), F32)
    cc = jnp.concatenate([cos, cos, zpad], axis=1)
    ss = jnp.concatenate([-sin, sin, zpad], axis=1)

    w_in = w['ssd_in_proj'][0]
    w_z = w_in[:, :d_inner]
    w_xbc = w_in[:, d_inner:d_inner + conv_dim]
    w_dt = _pad_cols(w_in[:, d_inner + conv_dim:], LANES)
    w_out = w['ssd_out_proj'][0]
    ssd_cw = w['ssd_conv_w'][0]
    ssd_cb = w['ssd_conv_b']
    dt_bias = _pad_cols(w['ssd_dt_bias'], LANES)
    a_log = _pad_cols(w['ssd_A_log'], LANES)
    d_lane = jnp.repeat(w['ssd_D'], SSD_HEAD_DIM, axis=1)
    ssd_ng = w['ssd_norm_g']
    w_kvc = w['kv_down_proj'][:, :kv_rank]
    w_kvr = _pad_cols(w['kv_down_proj'][:, kv_rank:], LANES)
    kv_g = w['kv_norm_g'].reshape(1, kv_rank)
    w_uk = w['kv_up_k']
    w_uv = w['kv_up_v']
    w_qd = w['q_down_proj'][0]
    q_g = w['q_norm_g']
    qu = w['q_up_proj'][0].reshape(q_rank, nh, MLA_NOPE + MLA_ROPE)
    w_qn = qu[:, :, :MLA_NOPE].reshape(q_rank, nh * LANES)
    w_qr = jnp.pad(qu[:, :, MLA_NOPE:], ((0, 0), (0, 0), (0, LANES - MLA_ROPE))).reshape(q_rank, nh * LANES)
    w_ao = w['attn_out_proj'][0]

    def ffn_parts(i):
        return dict(wu=_interleave(w['ffn_up'][i], H), cw=_interleave(w['ffn_conv_w'][i], H),
                    cb=_interleave(w['ffn_conv_b'][i:i + 1], H), wd=w['ffn_down'][i])

    def ln(name, i):
        return w[name][i:i + 1]

    def ffn_fwd(hb, i):
        f = ffn_parts(i)
        u = _mm(hb, f['wu'], name=f'ffn{i}_up')
        a = _ffn_act_fwd(u, f['cw'], f['cb'], S, name=f'ffn{i}_act')
        return _mm(a, f['wd'], name=f'ffn{i}_down'), (u, a)

    Xb = X.astype(BF16)
    z = _mm(Xb, w_z, name='ssd_in_z')
    xbc_pre = _mm(Xb, w_xbc, name='ssd_in_xbc')
    dt_pre = _mm(Xb, w_dt, name='ssd_in_dt')
    xbc = _ssd_conv_fwd(xbc_pre, ssd_cw, ssd_cb, S, name='ssd_conv')
    y_scan, states = _scan_fwd(xbc, dt_pre, dt_bias, a_log, S, d_inner, name='ssd_scan')
    yn = _gate_fwd(y_scan, xbc, z, d_lane, ssd_ng, d_inner, name='ssd_gate')
    mix0 = _mm(yn, w_out, name='ssd_out')
    h1, s1, h1b = _ln_fwd(X, mix0, ln('ln_mix_g', 0), ln('ln_mix_b', 0), alpha, name='ln_mix0')
    ff0, ffn0_saved = ffn_fwd(h1b, 0)
    h2, s2, h2b = _ln_fwd(h1, ff0, ln('ln_ffn_g', 0), ln('ln_ffn_b', 0), alpha, name='ln_ffn0')

    ckv = _mm(h2b, w_kvc, name='kv_down_c')
    kr_pre = _mm(h2b, w_kvr, name='kv_down_r')
    ckvn = _rms_fwd(ckv, kv_g, name='kv_norm')
    kr = _rope(kr_pre, cc, ss, transpose=False, name='k_rope')
    kn = _mm(ckvn, w_uk, out_dtype=BF16, name='kv_up_k')
    v = _mm(ckvn, w_uv, out_dtype=BF16, name='kv_up_v')
    cq_pre = _mm(h2b, w_qd, name='q_down')
    cq = _rms_fwd(cq_pre, q_g, name='q_norm')
    qn = _mm(cq, w_qn, out_dtype=BF16, name='q_up_n')
    qr_pre = _mm(cq, w_qr, name='q_up_r')
    qr = _rope(qr_pre, cc, ss, transpose=False, name='q_rope')
    o, ob, lse = _attn_fwd(qn, qr, kn, kr, v, S, scale, name='attn_fwd')
    mix1 = _mm(ob, w_ao, name='attn_out')
    h3, s3, h3b = _ln_fwd(h2, mix1, ln('ln_mix_g', 1), ln('ln_mix_b', 1), alpha, name='ln_mix1')
    ff1, ffn1_saved = ffn_fwd(h3b, 1)
    h4, s4, _ = _ln_fwd(h3, ff1, ln('ln_ffn_g', 1), ln('ln_ffn_b', 1), alpha, name='ln_ffn1')
    sq, dh4 = _loss_head(h4, tgt, name='loss_head')

    g = {}

    def ffn_bwd(ds, dsb, hb_in, saved, i):
        f = ffn_parts(i)
        u, a = saved
        d_wd = _mm(a, dsb, ta=True, name=f'ffn{i}_down_dw')
        da = _mm(dsb, f['wd'], tb=True, out_dtype=BF16, name=f'ffn{i}_down_dx')
        du, dcw, dcb = _ffn_act_bwd(u, da, f['cw'], f['cb'], S, name=f'ffn{i}_act_bwd')
        d_wu = _deinterleave(_mm(hb_in, du, ta=True, name=f'ffn{i}_up_dw'), H)
        dh = _mm(du, f['wu'], tb=True, add=ds, add_scale=alpha, name=f'ffn{i}_up_dx')
        return dh, d_wd, d_wu, _deinterleave(dcw, H), _deinterleave(dcb, H)

    ds4, ds4b, dg_f1, db_f1 = _ln_bwd(dh4, s4, ln('ln_ffn_g', 1), name='ln_ffn1_bwd')
    dh3, d_wd1, d_wu1, d_fcw1, d_fcb1 = ffn_bwd(ds4, ds4b, h3b, ffn1_saved, 1)
    ds3, ds3b, dg_m1, db_m1 = _ln_bwd(dh3, s3, ln('ln_mix_g', 1), name='ln_mix1_bwd')

    g['attn_out_proj'] = _mm(ob, ds3b, ta=True, name='attn_out_dw')[None]
    do = _mm(ds3b, w_ao, tb=True, name='attn_out_dx')
    dqn, dqr = _attn_bwd_dq(qn, qr, kn, kr, v, o, do, lse, S, scale, name='attn_bwd_dq')
    dkn, dv, dkr = _attn_bwd_dkv(qn, qr, kn, kr, v, o, do, lse, S, scale, name='attn_bwd_dkv')
    dqr_pre = _rope(dqr, cc, ss, transpose=True, name='q_rope_bwd')
    dkr_pre = _rope(dkr, cc, ss, transpose=True, name='k_rope_bwd')

    d_wqn = _mm(cq, dqn, ta=True, name='q_up_n_dw').reshape(q_rank, nh, LANES)
    d_wqr = _mm(cq, dqr_pre, ta=True, name='q_up_r_dw').reshape(q_rank, nh, LANES)[:, :, :MLA_ROPE]
    g['q_up_proj'] = jnp.concatenate([d_wqn, d_wqr], axis=2).reshape(1, q_rank, nh * (MLA_NOPE + MLA_ROPE))
    dcq = _mm(dqn, w_qn, tb=True, name='q_up_n_dx')
    dcq = _mm(dqr_pre, w_qr, tb=True, add=dcq, name='q_up_r_dx')
    dcq_pre, d_qg = _rms_bwd(dcq, cq_pre, q_g, name='q_norm_bwd')
    g['q_norm_g'] = d_qg
    g['q_down_proj'] = _mm(h2b, dcq_pre, ta=True, name='q_down_dw')[None]

    g['kv_up_k'] = _mm(ckvn, dkn, ta=True, name='kv_up_k_dw')
    g['kv_up_v'] = _mm(ckvn, dv, ta=True, name='kv_up_v_dw')
    dckvn = _mm(dkn, w_uk, tb=True, name='kv_up_k_dx')
    dckvn = _mm(dv, w_uv, tb=True, add=dckvn, name='kv_up_v_dx')
    dckv, d_kvg = _rms_bwd(dckvn, ckv, kv_g, name='kv_norm_bwd')
    g['kv_norm_g'] = d_kvg.reshape(kv_rank)
    g['kv_down_proj'] = jnp.concatenate(
        [_mm(h2b, dckv, ta=True, name='kv_down_c_dw'),
         _mm(h2b, dkr_pre, ta=True, name='kv_down_r_dw')[:, :MLA_ROPE]], axis=1)

    dh2 = _mm(dcq_pre, w_qd, tb=True, add=ds3, add_scale=alpha, name='q_down_dx')
    dh2 = _mm(dckv, w_kvc, tb=True, add=dh2, name='kv_down_c_dx')
    dh2 = _mm(dkr_pre, w_kvr, tb=True, add=dh2, name='kv_down_r_dx')

    ds2, ds2b, dg_f0, db_f0 = _ln_bwd(dh2, s2, ln('ln_ffn_g', 0), name='ln_ffn0_bwd')
    dh1, d_wd0, d_wu0, d_fcw0, d_fcb0 = ffn_bwd(ds2, ds2b, h1b, ffn0_saved, 0)
    ds1, ds1b, dg_m0, db_m0 = _ln_bwd(dh1, s1, ln('ln_mix_g', 0), name='ln_mix0_bwd')

    g['ssd_out_proj'] = _mm(yn, ds1b, ta=True, name='ssd_out_dw')[None]
    dyn = _mm(ds1b, w_out, tb=True, out_dtype=BF16, name='ssd_out_dx')
    dy1, dz, d_ng, d_dl = _gate_bwd(dyn, y_scan, xbc, z, d_lane, ssd_ng, d_inner, name='ssd_gate_bwd')
    dxs, dB, dC, ddt_g, da_g = _scan_bwd(xbc, dt_pre, dt_bias, a_log, states, dy1, d_lane, S, d_inner,
                                         name='ssd_scan_bwd')
    ddt_pre, d_bias, d_alog = _dt_bwd(ddt_g, da_g, dt_pre, dt_bias, a_log, name='ssd_dt_bwd')
    dxbc = jnp.concatenate([dxs, dB, dC], axis=1)
    dxbc_pre, d_scw, d_scb = _ssd_conv_bwd(xbc_pre, dxbc, ssd_cw, ssd_cb, S, name='ssd_conv_bwd')
    g['ssd_in_proj'] = jnp.concatenate(
        [_mm(Xb, dz, ta=True, name='ssd_in_z_dw'), _mm(Xb, dxbc_pre, ta=True, name='ssd_in_xbc_dw'),
         _mm(Xb, ddt_pre, ta=True, name='ssd_in_dt_dw')[:, :n_heads_ssd]], axis=1)[None]
    dx = _mm(dz, w_z, tb=True, add=ds1, add_scale=alpha, name='ssd_in_z_dx')
    dx = _mm(dxbc_pre, w_xbc, tb=True, add=dx, name='ssd_in_xbc_dx')
    dx = _mm(ddt_pre, w_dt, tb=True, add=dx, name='ssd_in_dt_dx')

    g['ssd_conv_w'] = d_scw[None]
    g['ssd_conv_b'] = d_scb
    g['ssd_dt_bias'] = d_bias[:, :n_heads_ssd]
    g['ssd_A_log'] = d_alog[:, :n_heads_ssd]
    g['ssd_D'] = jnp.sum(d_dl.reshape(1, n_heads_ssd, SSD_HEAD_DIM), axis=2)
    g['ssd_norm_g'] = d_ng
    g['ffn_up'] = jnp.stack([d_wu0, d_wu1])
    g['ffn_conv_w'] = jnp.stack([d_fcw0, d_fcw1])
    g['ffn_conv_b'] = jnp.concatenate([d_fcb0, d_fcb1], axis=0)
    g['ffn_down'] = jnp.stack([d_wd0, d_wd1])
    g['ln_mix_g'] = jnp.concatenate([dg_m0, dg_m1], axis=0)
    g['ln_mix_b'] = jnp.concatenate([db_m0, db_m1], axis=0)
    g['ln_ffn_g'] = jnp.concatenate([dg_f0, dg_f1], axis=0)
    g['ln_ffn_b'] = jnp.concatenate([db_f0, db_f1], axis=0)
    return sq, dx.reshape(Bn, S, D), g


ANY = pl.BlockSpec(memory_space=pl.ANY)


def _all_gather(xs, name):
    def body(x_ref, out_ref, send_sems, recv_sems, local_sem):
        x, y, c = lax.axis_index("x"), lax.axis_index("y"), lax.axis_index("c")
        me, sibling = (x, y, c), (x, y, 1 - c)
        chips = [(1 - x, y), (x, 1 - y), (1 - x, 1 - y)]

        def rows(px, py, pc):
            return out_ref.at[4 * px + 2 * py + pc]

        def copy(k, block, to, src=None):
            return pltpu.make_async_remote_copy(
                src_ref=rows(*block) if src is None else src, dst_ref=rows(*block),
                send_sem=send_sems.at[k], recv_sem=recv_sems.at[k], device_id=to, device_id_type=MESH)

        mine = pltpu.make_async_copy(x_ref, rows(*me), local_sem)
        mine.start()
        first = [copy(0, me, sibling, src=x_ref)]
        first += [copy(1 + j, me, (*chip, c), src=x_ref) for j, chip in enumerate(chips)]
        for cp in first:
            cp.start()
        passed = [copy(4 + j, (*chip, c), sibling) for j, chip in enumerate(chips)]
        for j, chip in enumerate(chips):
            copy(1 + j, (*chip, c), me).wait_recv()
            passed[j].start()
        copy(0, sibling, me).wait_recv()
        for j, chip in enumerate(chips):
            copy(4 + j, (*chip, 1 - c), me).wait_recv()
        for cp in first + passed:
            cp.wait_send()
        mine.wait()

    return pl.pallas_call(
        body, name=name, out_shape=jax.ShapeDtypeStruct((N_DEV,) + xs.shape, xs.dtype),
        in_specs=[ANY], out_specs=ANY,
        scratch_shapes=[pltpu.SemaphoreType.DMA((7,)), pltpu.SemaphoreType.DMA((7,)), pltpu.SemaphoreType.DMA(())],
    )(xs)


def _exchange_sibling(gfull, name):
    _, rows, lanes = gfull.shape

    def body(g_ref, r_ref, send_sems, recv_sems):
        x, y, c = lax.axis_index("x"), lax.axis_index("y"), lax.axis_index("c")
        copies = [pltpu.make_async_remote_copy(
            src_ref=g_ref.at[2 * k + (1 - c)], dst_ref=r_ref.at[k], send_sem=send_sems.at[k],
            recv_sem=recv_sems.at[k], device_id=(x, y, 1 - c), device_id_type=MESH) for k in range(4)]
        for cp in copies:
            cp.start()
        for cp in copies:
            cp.wait()

    return pl.pallas_call(
        body, name=name, out_shape=jax.ShapeDtypeStruct((4, rows, lanes), gfull.dtype),
        in_specs=[ANY], out_specs=ANY,
        scratch_shapes=[pltpu.SemaphoreType.DMA((4,)), pltpu.SemaphoreType.DMA((4,))],
    )(gfull)


def _exchange_chips(part, name):
    _, rows, lanes = part.shape

    def body(p_ref, r_ref, send_sems, recv_sems):
        x, y, c = lax.axis_index("x"), lax.axis_index("y"), lax.axis_index("c")
        chips = [(1 - x, y), (x, 1 - y), (1 - x, 1 - y)]
        copies = [pltpu.make_async_remote_copy(
            src_ref=p_ref.at[2 * cx + cy], dst_ref=r_ref.at[j], send_sem=send_sems.at[j],
            recv_sem=recv_sems.at[j], device_id=(cx, cy, c), device_id_type=MESH)
            for j, (cx, cy) in enumerate(chips)]
        for cp in copies:
            cp.start()
        for cp in copies:
            cp.wait()

    return pl.pallas_call(
        body, name=name, out_shape=jax.ShapeDtypeStruct((3, rows, lanes), part.dtype),
        in_specs=[ANY], out_specs=ANY,
        scratch_shapes=[pltpu.SemaphoreType.DMA((3,)), pltpu.SemaphoreType.DMA((3,))],
    )(part)


def _add_sibling(gfull, r1, core, name):
    _, rows, lanes = gfull.shape
    tr = _tile(rows, (2048, 1024, 512, 256, 128, 64, 32, 16, 8))

    def body(c_ref, g_ref, r_ref, o_ref):
        o_ref[...] = g_ref[...] + r_ref[...]

    return pl.pallas_call(
        body, name=name,
        grid_spec=pltpu.PrefetchScalarGridSpec(
            num_scalar_prefetch=1, grid=(4, rows // tr),
            in_specs=[pl.BlockSpec((1, tr, lanes), lambda k, r, c_ref: (2 * k + c_ref[0], r, 0)),
                      pl.BlockSpec((1, tr, lanes), lambda k, r, c_ref: (k, r, 0))],
            out_specs=pl.BlockSpec((1, tr, lanes), lambda k, r, c_ref: (k, r, 0))),
        out_shape=jax.ShapeDtypeStruct((4, rows, lanes), F32), compiler_params=_cparams(),
    )(core, gfull, r1)


def _adam_math(wv, gv, mv, vv):
    m = ADAM_B1 * mv + (1.0 - ADAM_B1) * gv
    v = ADAM_B2 * vv + (1.0 - ADAM_B2) * (gv * gv)
    m_hat = m / (1.0 - ADAM_B1 ** ADAM_STEP)
    v_hat = v / (1.0 - ADAM_B2 ** ADAM_STEP)
    delta = -ADAM_LR * (m_hat / (jnp.sqrt(v_hat) + ADAM_EPS) + ADAM_WD * wv)
    return delta, m, v


def _adam_sharded(part, r2, chip, wts, m, v, name):
    rows, lanes = wts.shape
    tr = _tile(rows, (2048, 1024, 512, 256, 128, 64, 32, 16, 8))

    def body(k_ref, p_ref, r_ref, w_ref, m_ref, v_ref, g_ref, d_ref, mo_ref, vo_ref):
        gv = p_ref[0] + r_ref[0] + r_ref[1] + r_ref[2]
        delta, mn, vn = _adam_math(w_ref[...], gv, m_ref[...], v_ref[...])
        g_ref[...] = gv
        d_ref[...] = delta
        mo_ref[...] = mn
        vo_ref[...] = vn

    flat = pl.BlockSpec((tr, lanes), lambda r, k_ref: (r, 0))
    sh = jax.ShapeDtypeStruct((rows, lanes), F32)
    return pl.pallas_call(
        body, name=name,
        grid_spec=pltpu.PrefetchScalarGridSpec(
            num_scalar_prefetch=1, grid=(rows // tr,),
            in_specs=[pl.BlockSpec((1, tr, lanes), lambda r, k_ref: (k_ref[0], r, 0)),
                      pl.BlockSpec((3, tr, lanes), lambda r, k_ref: (0, r, 0)), flat, flat, flat],
            out_specs=[flat, flat, flat, flat]),
        out_shape=[sh, sh, sh, sh], compiler_params=_cparams(),
    )(chip, part, r2, wts, m, v)


def _adam_replicated(gall, wts, m, v, name):
    rows, lanes = wts.shape

    def body(ga_ref, w_ref, m_ref, v_ref, g_ref, d_ref, mo_ref, vo_ref):
        gv = ga_ref[0]
        for k in range(1, N_DEV):
            gv = gv + ga_ref[k]
        delta, mn, vn = _adam_math(w_ref[...], gv, m_ref[...], v_ref[...])
        g_ref[...] = gv
        d_ref[...] = delta
        mo_ref[...] = mn
        vo_ref[...] = vn

    sh = jax.ShapeDtypeStruct((rows, lanes), F32)
    return pl.pallas_call(body, name=name, out_shape=[sh, sh, sh, sh], compiler_params=_cparams())(gall, wts, m, v)


def _pack(arrs, dtype, row_mult):
    flat = jnp.concatenate([a.reshape(-1).astype(dtype) for a in arrs])
    n = flat.shape[0]
    unit = LANES * row_mult
    total = -(-n // unit) * unit
    return jnp.pad(flat, (0, total - n)).reshape(total // LANES, LANES)


def _unpack(flat2d, shapes):
    flat = flat2d.reshape(-1)
    out, off = [], 0
    for shp in shapes:
        n = math.prod(shp)
        out.append(flat[off:off + n].reshape(shp))
        off += n
    return out


def _unpack_gathered(gathered, shapes, axes):
    flat = gathered.reshape(N_DEV, -1)
    out, off = [], 0
    for shp, ax in zip(shapes, axes):
        n = math.prod(shp)
        seg = flat[:, off:off + n].reshape((N_DEV,) + tuple(shp))
        out.append(jnp.concatenate([seg[i] for i in range(N_DEV)], axis=ax))
        off += n
    return out


def _pack_chunks(fulls, axes, row_mult):
    per_dev = []
    for full, ax in zip(fulls, axes):
        parts = jnp.stack(jnp.split(full, N_DEV, axis=ax))
        per_dev.append(parts.reshape(N_DEV, -1))
    flat = jnp.concatenate(per_dev, axis=1)
    n = flat.shape[1]
    unit = LANES * row_mult
    total = -(-n // unit) * unit
    return jnp.pad(flat, ((0, 0), (0, total - n))).reshape(N_DEV, total // LANES, LANES)


def kernel(x, positions, ssd_in_proj, ssd_conv_w, ssd_conv_b, ssd_dt_bias, ssd_A_log, ssd_D, ssd_norm_g, ssd_out_proj, kv_down_proj, kv_norm_g, kv_up_k, kv_up_v, q_down_proj, q_norm_g, q_up_proj, attn_out_proj, ffn_up, ffn_conv_w, ffn_conv_b, ffn_down, ln_mix_g, ln_mix_b, ln_ffn_g, ln_ffn_b, loss_target, m_ssd_in_proj, m_ssd_conv_w, m_ssd_conv_b, m_ssd_dt_bias, m_ssd_A_log, m_ssd_D, m_ssd_norm_g, m_ssd_out_proj, m_kv_down_proj, m_kv_norm_g, m_kv_up_k, m_kv_up_v, m_q_down_proj, m_q_norm_g, m_q_up_proj, m_attn_out_proj, m_ffn_up, m_ffn_conv_w, m_ffn_conv_b, m_ffn_down, m_ln_mix_g, m_ln_mix_b, m_ln_ffn_g, m_ln_ffn_b, v_ssd_in_proj, v_ssd_conv_w, v_ssd_conv_b, v_ssd_dt_bias, v_ssd_A_log, v_ssd_D, v_ssd_norm_g, v_ssd_out_proj, v_kv_down_proj, v_kv_norm_g, v_kv_up_k, v_kv_up_v, v_q_down_proj, v_q_norm_g, v_q_up_proj, v_attn_out_proj, v_ffn_up, v_ffn_conv_w, v_ffn_conv_b, v_ffn_down, v_ln_mix_g, v_ln_mix_b, v_ln_ffn_g, v_ln_ffn_b):
    given = dict(locals())
    wl = {n: given[n] for n in WEIGHTS}
    ml = {n: given['m_' + n] for n in WEIGHTS}
    vl = {n: given['v_' + n] for n in WEIGHTS}
    sharded_axis = dict(SHARDED)
    big = [n for n, _ in SHARDED if n not in SHARDED_F32]
    small = [n for n, _ in SHARDED if n in SHARDED_F32]
    sh_names = [n for n, _ in SHARDED]

    full = {n: wl[n] for n in REPLICATED}
    gat_big = _all_gather(_pack([wl[n] for n in big], BF16, 16), name='gather_weights_bf16')
    gat_small = _all_gather(_pack([wl[n] for n in small], F32, 8), name='gather_weights_f32')
    for names, gat in ((big, gat_big), (small, gat_small)):
        fulls = _unpack_gathered(gat, [wl[n].shape for n in names], [sharded_axis[n] for n in names])
        full.update(dict(zip(names, fulls)))

    sq, grad_x, grads = _local_step(x, positions, full, loss_target)
    loss = lax.psum(0.5 * jnp.sum(sq) / x.shape[-1], ("x", "y", "c"))

    cx, cy, cc = lax.axis_index("x"), lax.axis_index("y"), lax.axis_index("c")
    core = jnp.reshape(cc, (1,)).astype(jnp.int32)
    chip = jnp.reshape(2 * cx + cy, (1,)).astype(jnp.int32)
    gfull = _pack_chunks([grads[n] for n in sh_names], [sharded_axis[n] for n in sh_names], PACK_ROWS)
    r1 = _exchange_sibling(gfull, name='grads_to_sibling')
    part = _add_sibling(gfull, r1, core, name='grads_add_sibling')
    r2 = _exchange_chips(part, name='grads_to_chips')
    shapes = [wl[n].shape for n in sh_names]
    outs = _adam_sharded(part, r2, chip, _pack([wl[n] for n in sh_names], F32, PACK_ROWS),
                         _pack([ml[n] for n in sh_names], F32, PACK_ROWS),
                         _pack([vl[n] for n in sh_names], F32, PACK_ROWS), name='adamw_sharded')
    res = {}
    for kind, packed in zip(('grad', 'delta', 'new_m', 'new_v'), outs):
        for n, a in zip(sh_names, _unpack(packed, shapes)):
            res[kind, n] = a

    rep = list(REPLICATED)
    rshapes = [wl[n].shape for n in rep]
    gall = _all_gather(_pack([grads[n] for n in rep], F32, 8), name='gather_replicated_grads')
    outs = _adam_replicated(gall, _pack([wl[n] for n in rep], F32, 8), _pack([ml[n] for n in rep], F32, 8),
                            _pack([vl[n] for n in rep], F32, 8), name='adamw_replicated')
    for kind, packed in zip(('grad', 'delta', 'new_m', 'new_v'), outs):
        for n, a in zip(rep, _unpack(packed, rshapes)):
            res[kind, n] = a

    return (loss, grad_x, *[res['grad', n] for n in WEIGHTS], *[res['delta', n] for n in WEIGHTS],
            *[res['new_m', n] for n in WEIGHTS], *[res['new_v', n] for n in WEIGHTS])
```

```python
import functools
import math

import jax
import jax.numpy as jnp
from jax import lax
from jax.experimental import pallas as pl
from jax.experimental.pallas import tpu as pltpu

F32 = jnp.float32
BF16 = jnp.bfloat16
MESH = pl.DeviceIdType.MESH

N_DEV = 8
LANES = 128
MM_TILES = (1024, 1408, 896, 512, 384, 256, 128)
MM_VMEM_BUDGET = 30 * 1024 * 1024
VMEM_LIMIT = 48 * 1024 * 1024
LN_EPS = 1e-5
RMS_EPS = 1e-6
SSD_HEAD_DIM = 64
SSD_N_GROUPS = 8
SSD_D_STATE = 128
SSD_CHUNK = 128
MLA_NOPE = 128
MLA_ROPE = 64
MLA_V = 128
ROPE_THETA = 10000.0
ADAM_LR = 0.001
ADAM_B1 = 0.9
ADAM_B2 = 0.999
ADAM_EPS = 1e-08
ADAM_WD = 0.01
ADAM_STEP = 10
NEG_BIG = -1e30

SHARDED = (('ssd_in_proj', 2), ('ssd_conv_w', 2), ('ssd_conv_b', 1), ('ssd_norm_g', 1), ('ssd_out_proj', 1),
           ('kv_down_proj', 0), ('kv_up_k', 1), ('kv_up_v', 1), ('q_down_proj', 1), ('q_up_proj', 2),
           ('attn_out_proj', 1), ('ffn_up', 2), ('ffn_conv_w', 2), ('ffn_down', 1))
SHARDED_F32 = ('ssd_conv_w', 'ssd_conv_b', 'ssd_norm_g', 'ffn_conv_w')
REPLICATED = ('ssd_dt_bias', 'ssd_A_log', 'ssd_D', 'kv_norm_g', 'q_norm_g', 'ffn_conv_b',
              'ln_mix_g', 'ln_mix_b', 'ln_ffn_g', 'ln_ffn_b')
WEIGHTS = ('ssd_in_proj', 'ssd_conv_w', 'ssd_conv_b', 'ssd_dt_bias', 'ssd_A_log', 'ssd_D', 'ssd_norm_g',
           'ssd_out_proj', 'kv_down_proj', 'kv_norm_g', 'kv_up_k', 'kv_up_v', 'q_down_proj', 'q_norm_g',
           'q_up_proj', 'attn_out_proj', 'ffn_up', 'ffn_conv_w', 'ffn_conv_b', 'ffn_down', 'ln_mix_g',
           'ln_mix_b', 'ln_ffn_g', 'ln_ffn_b')


def _cparams():
    return pltpu.CompilerParams(vmem_limit_bytes=VMEM_LIMIT)


def _tile(n, cands):
    for c in cands:
        if n % c == 0:
            return c
    return n


def _sigmoid(x):
    return 1.0 / (1.0 + jnp.exp(-x))


def _iota(shape, axis):
    return lax.broadcasted_iota(jnp.int32, shape, axis)


def _mm(a, b, *, ta=False, tb=False, add=None, add_scale=1.0, out_dtype=F32, name):
    if ta:
        K, M = a.shape
    else:
        M, K = a.shape
    if tb:
        N, K2 = b.shape
    else:
        K2, N = b.shape
    assert K == K2, (a.shape, b.shape, ta, tb)
    tm = _tile(M, MM_TILES)
    tn = _tile(N, MM_TILES)
    tk = K if K <= MM_TILES[0] else _tile(K, MM_TILES)
    nk = K // tk

    def vmem_estimate(tm_, tn_):
        ins = 2 * (tm_ * tk * a.dtype.itemsize + tk * tn_ * b.dtype.itemsize)
        outs = tm_ * tn_ * (2 * jnp.dtype(out_dtype).itemsize + 4 + (4 if nk > 1 else 0))
        return ins + outs + (2 * tm_ * tn_ * add.dtype.itemsize if add is not None else 0)

    while vmem_estimate(tm, tn) > MM_VMEM_BUDGET:
        can_m, can_n = tm % (2 * LANES) == 0, tn % (2 * LANES) == 0
        if can_m and (tm >= tn or not can_n):
            tm //= 2
        elif can_n:
            tn //= 2
        else:
            break

    def body(a_ref, b_ref, *rest):
        add_ref = rest[0] if add is not None else None
        o_ref = rest[1] if add is not None else rest[0]
        acc = rest[-1] if nk > 1 else None
        k = pl.program_id(2)

        if ta:
            av = a_ref[...].astype(BF16).T
        else:
            av = a_ref[...].astype(BF16)
        bv = b_ref[...].astype(BF16)
        dn = (((1,), (1 if tb else 0,)), ((), ()))
        part = lax.dot_general(av, bv, dn, preferred_element_type=F32)

        def finish(r):
            if add is not None:
                r = r + add_scale * add_ref[...].astype(F32)
            o_ref[...] = r.astype(out_dtype)

        if nk == 1:
            finish(part)
        else:
            @pl.when(k == 0)
            def _():
                acc[...] = part

            @pl.when(k > 0)
            def _():
                acc[...] += part

            @pl.when(k == nk - 1)
            def _():
                finish(acc[...])

    a_spec = (pl.BlockSpec((tk, tm), lambda i, j, k: (k, i)) if ta
              else pl.BlockSpec((tm, tk), lambda i, j, k: (i, k)))
    b_spec = (pl.BlockSpec((tn, tk), lambda i, j, k: (j, k)) if tb
              else pl.BlockSpec((tk, tn), lambda i, j, k: (k, j)))
    o_spec = pl.BlockSpec((tm, tn), lambda i, j, k: (i, j))
    in_specs = [a_spec, b_spec]
    args = [a, b]
    if add is not None:
        in_specs.append(o_spec)
        args.append(add)
    return pl.pallas_call(
        body, name=name, grid=(M // tm, N // tn, nk),
        in_specs=in_specs, out_specs=o_spec,
        out_shape=jax.ShapeDtypeStruct((M, N), out_dtype),
        scratch_shapes=[pltpu.VMEM((tm, tn), F32)] if nk > 1 else [],
        compiler_params=_cparams(),
    )(*args)


def _ln_fwd(h, mix, g, b, alpha, name):
    T, D = h.shape
    tm = _tile(T, (256, 128))

    def body(h_ref, m_ref, g_ref, b_ref, y_ref, s_ref, yb_ref):
        s = alpha * h_ref[...] + m_ref[...]
        mu = jnp.mean(s, axis=1, keepdims=True)
        xc = s - mu
        var = jnp.mean(xc * xc, axis=1, keepdims=True)
        y = xc * lax.rsqrt(var + LN_EPS) * g_ref[...] + b_ref[...]
        y_ref[...] = y
        s_ref[...] = s
        yb_ref[...] = y.astype(BF16)

    row = pl.BlockSpec((tm, D), lambda i: (i, 0))
    vec = pl.BlockSpec((1, D), lambda i: (0, 0))
    return pl.pallas_call(
        body, name=name, grid=(T // tm,), in_specs=[row, row, vec, vec], out_specs=[row, row, row],
        out_shape=[jax.ShapeDtypeStruct((T, D), F32)] * 2 + [jax.ShapeDtypeStruct((T, D), BF16)],
        compiler_params=_cparams(),
    )(h, mix, g, b)


def _ln_bwd(dy, s, g, name):
    T, D = s.shape
    tm = _tile(T, (256, 128))

    def body(dy_ref, s_ref, g_ref, ds_ref, dsb_ref, dg_ref, db_ref):
        @pl.when(pl.program_id(0) == 0)
        def _():
            dg_ref[...] = jnp.zeros_like(dg_ref)
            db_ref[...] = jnp.zeros_like(db_ref)

        sv = s_ref[...]
        dyv = dy_ref[...]
        mu = jnp.mean(sv, axis=1, keepdims=True)
        xc = sv - mu
        rstd = lax.rsqrt(jnp.mean(xc * xc, axis=1, keepdims=True) + LN_EPS)
        xhat = xc * rstd
        dxh = dyv * g_ref[...]
        m1 = jnp.mean(dxh, axis=1, keepdims=True)
        m2 = jnp.mean(dxh * xhat, axis=1, keepdims=True)
        ds = rstd * (dxh - m1 - xhat * m2)
        ds_ref[...] = ds
        dsb_ref[...] = ds.astype(BF16)
        dg_ref[...] += jnp.sum(dyv * xhat, axis=0, keepdims=True)
        db_ref[...] += jnp.sum(dyv, axis=0, keepdims=True)

    row = pl.BlockSpec((tm, D), lambda i: (i, 0))
    vec = pl.BlockSpec((1, D), lambda i: (0, 0))
    return pl.pallas_call(
        body, name=name, grid=(T // tm,), in_specs=[row, row, vec], out_specs=[row, row, vec, vec],
        out_shape=[jax.ShapeDtypeStruct((T, D), F32), jax.ShapeDtypeStruct((T, D), BF16),
                   jax.ShapeDtypeStruct((1, D), F32), jax.ShapeDtypeStruct((1, D), F32)],
        compiler_params=_cparams(),
    )(dy, s, g)


def _loss_head(y, target, name):
    T, D = y.shape
    tm = _tile(T, (256, 128))

    def body(y_ref, t_ref, sq_ref, dy_ref):
        @pl.when(pl.program_id(0) == 0)
        def _():
            sq_ref[...] = jnp.zeros_like(sq_ref)

        e = y_ref[...] - t_ref[...]
        sq_ref[...] += jnp.sum(e * e, axis=0, keepdims=True)
        dy_ref[...] = e * (1.0 / D)

    row = pl.BlockSpec((tm, D), lambda i: (i, 0))
    vec = pl.BlockSpec((1, D), lambda i: (0, 0))
    return pl.pallas_call(
        body, name=name, grid=(T // tm,), in_specs=[row, row], out_specs=[vec, row],
        out_shape=[jax.ShapeDtypeStruct((1, D), F32), jax.ShapeDtypeStruct((T, D), F32)],
        compiler_params=_cparams(),
    )(y, target)


def _shift_down(x, j):
    if j == 0:
        return x
    return jnp.where(_iota(x.shape, 0) >= j, pltpu.roll(x, j, 0), 0.0)


def _shift_up(x, j):
    if j == 0:
        return x
    n = x.shape[0]
    return jnp.where(_iota(x.shape, 0) < n - j, pltpu.roll(x, n - j, 0), 0.0)


def _conv_val(x, w_ref, b_ref, cs=slice(None)):
    width = w_ref.shape[0]
    y = b_ref[:, cs] + w_ref[width - 1:width, cs] * x
    for j in range(1, width):
        y = y + w_ref[width - 1 - j:width - j, cs] * _shift_down(x, j)
    return y


def _conv_bwd_val(x, du, w_ref, dw_ref, db_ref, cs=slice(None)):
    width = w_ref.shape[0]
    dx = w_ref[width - 1:width, cs] * du
    for j in range(1, width):
        dx = dx + w_ref[width - 1 - j:width - j, cs] * _shift_up(du, j)
    for j in range(width):
        dw_ref[width - 1 - j:width - j, cs] += jnp.sum(du * _shift_down(x, j), axis=0, keepdims=True)
    db_ref[:, cs] += jnp.sum(du, axis=0, keepdims=True)
    return dx


GATE = slice(0, LANES)
VALUE = slice(LANES, 2 * LANES)


def _interleave(a, H):
    lead = a.shape[:-1]
    return jnp.swapaxes(a.reshape(lead + (2, H // LANES, LANES)), -3, -2).reshape(lead + (2 * H,))


def _deinterleave(a, H):
    lead = a.shape[:-1]
    return jnp.swapaxes(a.reshape(lead + (H // LANES, 2, LANES)), -3, -2).reshape(lead + (2 * H,))


def _ffn_act_fwd(u, cw, cb, S, name):
    T, H2 = u.shape
    H = H2 // 2
    width = cw.shape[0]

    def body(u_ref, w_ref, b_ref, a_ref):
        g = _conv_val(u_ref[:, GATE], w_ref, b_ref, GATE)
        v = _conv_val(u_ref[:, VALUE], w_ref, b_ref, VALUE)
        a_ref[...] = (g * _sigmoid(g) * v).astype(BF16)

    ublk = pl.BlockSpec((S, 2 * LANES), lambda j, b: (b, j))
    ablk = pl.BlockSpec((S, LANES), lambda j, b: (b, j))
    wblk = pl.BlockSpec((width, 2 * LANES), lambda j, b: (0, j))
    bblk = pl.BlockSpec((1, 2 * LANES), lambda j, b: (0, j))
    return pl.pallas_call(
        body, name=name, grid=(H // LANES, T // S), in_specs=[ublk, wblk, bblk], out_specs=ablk,
        out_shape=jax.ShapeDtypeStruct((T, H), BF16), compiler_params=_cparams(),
    )(u, cw, cb)


def _ffn_act_bwd(u, da, cw, cb, S, name):
    T, H2 = u.shape
    H = H2 // 2
    width = cw.shape[0]

    def body(u_ref, da_ref, w_ref, b_ref, du_ref, dw_ref, db_ref):
        @pl.when(pl.program_id(1) == 0)
        def _():
            dw_ref[...] = jnp.zeros_like(dw_ref)
            db_ref[...] = jnp.zeros_like(db_ref)

        xg = u_ref[:, GATE]
        xv = u_ref[:, VALUE]
        g = _conv_val(xg, w_ref, b_ref, GATE)
        v = _conv_val(xv, w_ref, b_ref, VALUE)
        dav = da_ref[...].astype(F32)
        sg = _sigmoid(g)
        dg = dav * v * sg * (1.0 + g * (1.0 - sg))
        dv = dav * g * sg
        du_ref[:, GATE] = _conv_bwd_val(xg, dg, w_ref, dw_ref, db_ref, GATE).astype(BF16)
        du_ref[:, VALUE] = _conv_bwd_val(xv, dv, w_ref, dw_ref, db_ref, VALUE).astype(BF16)

    ublk = pl.BlockSpec((S, 2 * LANES), lambda j, b: (b, j))
    ablk = pl.BlockSpec((S, LANES), lambda j, b: (b, j))
    wblk = pl.BlockSpec((width, 2 * LANES), lambda j, b: (0, j))
    bblk = pl.BlockSpec((1, 2 * LANES), lambda j, b: (0, j))
    return pl.pallas_call(
        body, name=name, grid=(H // LANES, T // S), in_specs=[ublk, ablk, wblk, bblk],
        out_specs=[ublk, wblk, bblk],
        out_shape=[jax.ShapeDtypeStruct((T, H2), BF16), jax.ShapeDtypeStruct((width, H2), F32),
                   jax.ShapeDtypeStruct((1, H2), F32)],
        compiler_params=_cparams(),
    )(u, da, cw, cb)


def _ssd_conv_fwd(x, cw, cb, S, name):
    T, C = x.shape
    Cb = _tile(C, (256, 128))
    width = cw.shape[0]

    def body(x_ref, w_ref, b_ref, y_ref):
        u = _conv_val(x_ref[...], w_ref, b_ref)
        y_ref[...] = u * _sigmoid(u)

    blk = pl.BlockSpec((S, Cb), lambda j, b: (b, j))
    wblk = pl.BlockSpec((width, Cb), lambda j, b: (0, j))
    bblk = pl.BlockSpec((1, Cb), lambda j, b: (0, j))
    return pl.pallas_call(
        body, name=name, grid=(C // Cb, T // S), in_specs=[blk, wblk, bblk], out_specs=blk,
        out_shape=jax.ShapeDtypeStruct((T, C), F32), compiler_params=_cparams(),
    )(x, cw, cb)


def _ssd_conv_bwd(x, dy, cw, cb, S, name):
    T, C = x.shape
    Cb = _tile(C, (256, 128))
    width = cw.shape[0]

    def body(x_ref, dy_ref, w_ref, b_ref, dx_ref, dw_ref, db_ref):
        @pl.when(pl.program_id(1) == 0)
        def _():
            dw_ref[...] = jnp.zeros_like(dw_ref)
            db_ref[...] = jnp.zeros_like(db_ref)

        xv = x_ref[...]
        u = _conv_val(xv, w_ref, b_ref)
        su = _sigmoid(u)
        du = dy_ref[...] * su * (1.0 + u * (1.0 - su))
        dx_ref[...] = _conv_bwd_val(xv, du, w_ref, dw_ref, db_ref).astype(BF16)

    blk = pl.BlockSpec((S, Cb), lambda j, b: (b, j))
    wblk = pl.BlockSpec((width, Cb), lambda j, b: (0, j))
    bblk = pl.BlockSpec((1, Cb), lambda j, b: (0, j))
    return pl.pallas_call(
        body, name=name, grid=(C // Cb, T // S), in_specs=[blk, blk, wblk, bblk],
        out_specs=[blk, wblk, bblk],
        out_shape=[jax.ShapeDtypeStruct((T, C), BF16), jax.ShapeDtypeStruct((width, C), F32),
                   jax.ShapeDtypeStruct((1, C), F32)],
        compiler_params=_cparams(),
    )(x, dy, cw, cb)


def _softplus(x):
    e = jnp.exp(-jnp.abs(x))
    return jnp.maximum(x, 0.0) + jnp.where(e < 1e-4, e * (1.0 - 0.5 * e), jnp.log(1.0 + e))


def _cumsum_rows(x):
    n = x.shape[0]
    row = _iota(x.shape, 0)
    k = 1
    while k < n:
        x = x + jnp.where(row >= k, pltpu.roll(x, k, 0), 0.0)
        k *= 2
    return x


def _rev_cumsum_rows(x):
    n = x.shape[0]
    row = _iota(x.shape, 0)
    k = 1
    while k < n:
        x = x + jnp.where(row < n - k, pltpu.roll(x, n - k, 0), 0.0)
        k *= 2
    return x


def _col(x, lane_ids, h):
    return jnp.sum(jnp.where(lane_ids == h, x, 0.0), axis=1, keepdims=True)


def _bdot(a, b, dims):
    return lax.dot_general(a.astype(BF16), b.astype(BF16), (dims, ((), ())), preferred_element_type=F32)


NN = ((1,), (0,))
NT = ((1,), (1,))


def _scan_specs(nc, d_inner, hpg):
    L = SSD_CHUNK
    G = SSD_N_GROUPS
    gw = hpg * SSD_HEAD_DIM
    nb = d_inner // LANES
    return dict(
        xs=pl.BlockSpec((L, gw), lambda b, g, c: (b * nc + c, g)),
        B=pl.BlockSpec((L, LANES), lambda b, g, c: (b * nc + c, nb + g)),
        C=pl.BlockSpec((L, LANES), lambda b, g, c: (b * nc + c, nb + G + g)),
        dt=pl.BlockSpec((L, LANES), lambda b, g, c: (b * nc + c, 0)),
        vec=pl.BlockSpec((1, LANES), lambda b, g, c: (0, 0)),
        gvec=pl.BlockSpec((1, gw), lambda b, g, c: (0, g)),
        grp=pl.BlockSpec((L, LANES), lambda b, g, c: (b * nc + c, g)),
        st=pl.BlockSpec((1, hpg // 2, SSD_D_STATE, LANES), lambda b, g, c: ((b * G + g) * nc + c, 0, 0, 0)),
    )


def _scan_fwd(xbc, dt_pre, dt_bias, a_log, S, d_inner, name):
    T = xbc.shape[0]
    Bn = T // S
    L = SSD_CHUNK
    G = SSD_N_GROUPS
    nc = S // L
    hpg = d_inner // SSD_HEAD_DIM // G
    pairs = hpg // 2
    gw = hpg * SSD_HEAD_DIM
    sp = _scan_specs(nc, d_inner, hpg)

    def body(xs_ref, b_ref, c_ref, dtp_ref, bias_ref, alog_ref, y_ref, st_ref, state):
        g = pl.program_id(1)

        @pl.when(pl.program_id(2) == 0)
        def _():
            state[...] = jnp.zeros_like(state)

        dt = _softplus(dtp_ref[...] + bias_ref[...])
        cum = _cumsum_rows(dt * (-jnp.exp(alog_ref[...])))
        Bm = b_ref[...]
        Cm = c_ref[...]
        Gm = _bdot(Cm, Bm, NT)
        tril = _iota((L, L), 1) <= _iota((L, L), 0)
        lane = _iota((L, LANES), 1)
        lane1 = _iota((1, LANES), 1)
        last = _iota((L, 1), 0) == L - 1
        BmT = Bm.T
        for p in range(pairs):
            S_in = state[p]
            st_ref[0, p] = S_in
            x_pair = xs_ref[:, p * LANES:(p + 1) * LANES]
            y_pair = jnp.zeros((L, LANES), F32)
            z_pair = jnp.zeros((L, LANES), F32)
            e_pair = jnp.zeros((L, LANES), F32)
            el_pair = jnp.zeros((1, LANES), F32)
            for k in range(2):
                h = g * hpg + 2 * p + k
                cum_col = _col(cum, lane, h)
                dt_col = _col(dt, lane, h)
                cb = jnp.broadcast_to(cum_col, (L, L))
                decay = jnp.exp(jnp.where(tril, cb - cb.T, NEG_BIG))
                mk = (lane >= SSD_HEAD_DIM) if k else (lane < SSD_HEAD_DIM)
                xd = jnp.where(mk, x_pair * dt_col, 0.0)
                y_pair = y_pair + _bdot(Gm * decay, xd, NN)
                cum_l = jnp.sum(jnp.where(last, cum_col, 0.0), axis=0, keepdims=True)
                e_pair = jnp.where(mk, jnp.exp(cum_col), e_pair)
                z_pair = z_pair + xd * jnp.exp(cum_l - cum_col)
                el_pair = jnp.where((lane1 >= SSD_HEAD_DIM) if k else (lane1 < SSD_HEAD_DIM), jnp.exp(cum_l), el_pair)
            y_pair = y_pair + e_pair * _bdot(Cm, S_in, NN)
            state[p] = S_in * el_pair + _bdot(BmT, z_pair, NN)
            y_ref[:, p * LANES:(p + 1) * LANES] = y_pair

    return pl.pallas_call(
        body, name=name, grid=(Bn, G, nc),
        in_specs=[sp['xs'], sp['B'], sp['C'], sp['dt'], sp['vec'], sp['vec']],
        out_specs=[sp['xs'], sp['st']],
        out_shape=[jax.ShapeDtypeStruct((T, d_inner), F32),
                   jax.ShapeDtypeStruct((Bn * G * nc, pairs, SSD_D_STATE, LANES), F32)],
        scratch_shapes=[pltpu.VMEM((pairs, SSD_D_STATE, LANES), F32)],
        compiler_params=_cparams(),
    )(xbc, xbc, xbc, dt_pre, dt_bias, a_log)


def _scan_bwd(xbc, dt_pre, dt_bias, a_log, states, dy, d_lane, S, d_inner, name):
    T = xbc.shape[0]
    Bn = T // S
    L = SSD_CHUNK
    G = SSD_N_GROUPS
    nc = S // L
    hpg = d_inner // SSD_HEAD_DIM // G
    pairs = hpg // 2
    sp = _scan_specs(nc, d_inner, hpg)

    def rev(spec):
        im = spec.index_map
        return pl.BlockSpec(spec.block_shape, lambda b, g, c: im(b, g, nc - 1 - c))

    def body(xs_ref, b_ref, c_ref, dtp_ref, bias_ref, alog_ref, st_ref, dy_ref, dl_ref,
             dxs_ref, db_ref, dc_ref, ddt_ref, da_ref, dstate):
        g = pl.program_id(1)

        @pl.when(pl.program_id(2) == 0)
        def _():
            dstate[...] = jnp.zeros_like(dstate)

        dt = _softplus(dtp_ref[...] + bias_ref[...])
        cum = _cumsum_rows(dt * (-jnp.exp(alog_ref[...])))
        Bm = b_ref[...]
        Cm = c_ref[...]
        Gm = _bdot(Cm, Bm, NT)
        tril = _iota((L, L), 1) <= _iota((L, L), 0)
        lane = _iota((L, LANES), 1)
        lane1 = _iota((1, LANES), 1)
        last = _iota((L, 1), 0) == L - 1
        CmT = Cm.T
        dG = jnp.zeros((L, L), F32)
        dB = jnp.zeros((L, LANES), F32)
        dC = jnp.zeros((L, LANES), F32)
        dcum = jnp.zeros((L, LANES), F32)
        ddt = jnp.zeros((L, LANES), F32)
        for p in range(pairs):
            S_in = st_ref[0, p]
            dS = dstate[p]
            x_pair = xs_ref[:, p * LANES:(p + 1) * LANES]
            dy_pair = dy_ref[:, p * LANES:(p + 1) * LANES]
            Q = _bdot(Cm, S_in, NN)
            dZ = _bdot(Bm, dS, NN)
            prod = dS * S_in
            e_pair = jnp.zeros((L, LANES), F32)
            z_pair = jnp.zeros((L, LANES), F32)
            el_pair = jnp.zeros((1, LANES), F32)
            dx_pair = dl_ref[:, p * LANES:(p + 1) * LANES] * dy_pair
            for k in range(2):
                h = g * hpg + 2 * p + k
                cum_col = _col(cum, lane, h)
                dt_col = _col(dt, lane, h)
                cb = jnp.broadcast_to(cum_col, (L, L))
                decay = jnp.exp(jnp.where(tril, cb - cb.T, NEG_BIG))
                mk = (lane >= SSD_HEAD_DIM) if k else (lane < SSD_HEAD_DIM)
                mk1 = (lane1 >= SSD_HEAD_DIM) if k else (lane1 < SSD_HEAD_DIM)
                xd = jnp.where(mk, x_pair * dt_col, 0.0)
                dy_k = jnp.where(mk, dy_pair, 0.0)
                Wm = Gm * decay
                dWm = jnp.where(tril, _bdot(dy_k, xd, NT), 0.0)
                dxd = _bdot(Wm.T, dy_k, NN)
                dseg = dWm * Wm
                dG = dG + dWm * decay
                dcum_col = (jnp.sum(dseg, axis=1, keepdims=True)
                            - jnp.sum(dseg.T, axis=1, keepdims=True))
                cum_l = jnp.sum(jnp.where(last, cum_col, 0.0), axis=0, keepdims=True)
                e_col = jnp.exp(cum_col)
                w_col = jnp.exp(cum_l - cum_col)
                el = jnp.exp(cum_l)
                dcum_col = dcum_col + jnp.sum(dy_k * Q, axis=1, keepdims=True) * e_col
                de_e = jnp.sum(dZ * xd, axis=1, keepdims=True) * w_col
                dcum_col = dcum_col - de_e
                dcum_l = (jnp.sum(de_e, axis=0, keepdims=True)
                          + el * jnp.sum(jnp.sum(jnp.where(mk, prod, 0.0), axis=1, keepdims=True),
                                         axis=0, keepdims=True))
                dcum_col = dcum_col + jnp.where(last, dcum_l, 0.0)
                dxd = dxd + jnp.where(mk, dZ * w_col, 0.0)
                dx_pair = dx_pair + dxd * dt_col
                ddt = jnp.where(lane == h, jnp.sum(dxd * x_pair, axis=1, keepdims=True), ddt)
                dcum = jnp.where(lane == h, dcum_col, dcum)
                e_pair = jnp.where(mk, e_col, e_pair)
                z_pair = z_pair + xd * w_col
                el_pair = jnp.where(mk1, el, el_pair)
            dQ = e_pair * dy_pair
            dC = dC + _bdot(dQ, S_in, NT)
            dB = dB + _bdot(z_pair, dS, NT)
            dstate[p] = dS * el_pair + _bdot(CmT, dQ, NN)
            dxs_ref[:, p * LANES:(p + 1) * LANES] = dx_pair
        dc_ref[...] = dC + _bdot(dG, Bm, NN)
        db_ref[...] = dB + _bdot(dG.T, Cm, NN)
        ddt_ref[...] = ddt
        da_ref[...] = _rev_cumsum_rows(dcum)

    grp = jax.ShapeDtypeStruct((T, G * LANES), F32)
    return pl.pallas_call(
        body, name=name, grid=(Bn, G, nc),
        in_specs=[rev(sp['xs']), rev(sp['B']), rev(sp['C']), rev(sp['dt']), sp['vec'], sp['vec'],
                  rev(sp['st']), rev(sp['xs']), sp['gvec']],
        out_specs=[rev(sp['xs']), rev(sp['grp']), rev(sp['grp']), rev(sp['grp']), rev(sp['grp'])],
        out_shape=[jax.ShapeDtypeStruct((T, d_inner), F32), grp, grp, grp, grp],
        scratch_shapes=[pltpu.VMEM((pairs, SSD_D_STATE, LANES), F32)],
        compiler_params=_cparams(),
    )(xbc, xbc, xbc, dt_pre, dt_bias, a_log, states, dy, d_lane)


def _dt_bwd(ddt_g, da_g, dt_pre, dt_bias, a_log, name):
    T = dt_pre.shape[0]
    G = SSD_N_GROUPS
    tm = _tile(T, (512, 256, 128))

    def body(ddt_ref, da_ref, dtp_ref, bias_ref, alog_ref, dx_ref, dbias_ref, dal_ref):
        @pl.when(pl.program_id(0) == 0)
        def _():
            dbias_ref[...] = jnp.zeros_like(dbias_ref)
            dal_ref[...] = jnp.zeros_like(dal_ref)

        ddt = ddt_ref[:, 0:LANES]
        da = da_ref[:, 0:LANES]
        for gi in range(1, G):
            ddt = ddt + ddt_ref[:, gi * LANES:(gi + 1) * LANES]
            da = da + da_ref[:, gi * LANES:(gi + 1) * LANES]
        xv = dtp_ref[...] + bias_ref[...]
        A = -jnp.exp(alog_ref[...])
        dx = (ddt + da * A) * _sigmoid(xv)
        dx_ref[...] = dx.astype(BF16)
        dbias_ref[...] += jnp.sum(dx, axis=0, keepdims=True)
        dal_ref[...] += jnp.sum(da * _softplus(xv), axis=0, keepdims=True) * A

    wide = pl.BlockSpec((tm, G * LANES), lambda i: (i, 0))
    row = pl.BlockSpec((tm, LANES), lambda i: (i, 0))
    vec = pl.BlockSpec((1, LANES), lambda i: (0, 0))
    vsh = jax.ShapeDtypeStruct((1, LANES), F32)
    return pl.pallas_call(
        body, name=name, grid=(T // tm,), in_specs=[wide, wide, row, vec, vec], out_specs=[row, vec, vec],
        out_shape=[jax.ShapeDtypeStruct((T, LANES), BF16), vsh, vsh], compiler_params=_cparams(),
    )(ddt_g, da_g, dt_pre, dt_bias, a_log)


def _gate_fwd(y, xbc, z, d_lane, ng, d_inner, name):
    T = y.shape[0]
    G = SSD_N_GROUPS
    gw = d_inner // G
    tm = _tile(T, (512, 256, 128))

    def body(y_ref, x_ref, z_ref, dl_ref, ng_ref, o_ref):
        zv = z_ref[...]
        y2 = (y_ref[...] + dl_ref[...] * x_ref[...]) * (zv * _sigmoid(zv))
        r = lax.rsqrt(jnp.mean(y2 * y2, axis=1, keepdims=True) + LN_EPS)
        o_ref[...] = (y2 * r * ng_ref[...]).astype(BF16)

    blk = pl.BlockSpec((tm, gw), lambda g, i: (i, g))
    vec = pl.BlockSpec((1, gw), lambda g, i: (0, g))
    return pl.pallas_call(
        body, name=name, grid=(G, T // tm), in_specs=[blk, blk, blk, vec, vec], out_specs=blk,
        out_shape=jax.ShapeDtypeStruct((T, d_inner), BF16), compiler_params=_cparams(),
    )(y, xbc, z, d_lane, ng)


def _gate_bwd(dyn, y, xbc, z, d_lane, ng, d_inner, name):
    T = y.shape[0]
    G = SSD_N_GROUPS
    gw = d_inner // G
    tm = _tile(T, (512, 256, 128))

    def body(dyn_ref, y_ref, x_ref, z_ref, dl_ref, ng_ref, dy_ref, dz_ref, dng_ref, ddl_ref):
        @pl.when(pl.program_id(1) == 0)
        def _():
            dng_ref[...] = jnp.zeros_like(dng_ref)
            ddl_ref[...] = jnp.zeros_like(ddl_ref)

        zv = z_ref[...]
        xv = x_ref[...]
        sz = _sigmoid(zv)
        y1 = y_ref[...] + dl_ref[...] * xv
        y2 = y1 * (zv * sz)
        r = lax.rsqrt(jnp.mean(y2 * y2, axis=1, keepdims=True) + LN_EPS)
        dn = dyn_ref[...].astype(F32)
        dng_ref[...] += jnp.sum(dn * y2 * r, axis=0, keepdims=True)
        do = dn * ng_ref[...]
        dy2 = r * do - y2 * (r * r * r) * jnp.mean(do * y2, axis=1, keepdims=True)
        dz_ref[...] = (dy2 * y1 * sz * (1.0 + zv * (1.0 - sz))).astype(BF16)
        dy1 = dy2 * (zv * sz)
        dy_ref[...] = dy1
        ddl_ref[...] += jnp.sum(dy1 * xv, axis=0, keepdims=True)

    blk = pl.BlockSpec((tm, gw), lambda g, i: (i, g))
    vec = pl.BlockSpec((1, gw), lambda g, i: (0, g))
    vsh = jax.ShapeDtypeStruct((1, d_inner), F32)
    return pl.pallas_call(
        body, name=name, grid=(G, T // tm), in_specs=[blk, blk, blk, blk, vec, vec],
        out_specs=[blk, blk, vec, vec],
        out_shape=[jax.ShapeDtypeStruct((T, d_inner), F32), jax.ShapeDtypeStruct((T, d_inner), BF16), vsh, vsh],
        compiler_params=_cparams(),
    )(dyn, y, xbc, z, d_lane, ng)


def _rms_fwd(x, g, name):
    T, R = x.shape
    tm = _tile(T, (512, 256, 128))

    def body(x_ref, g_ref, y_ref):
        xv = x_ref[...]
        y = xv * lax.rsqrt(jnp.mean(xv * xv, axis=1, keepdims=True) + RMS_EPS) * g_ref[...]
        y_ref[...] = y.astype(BF16)

    row = pl.BlockSpec((tm, R), lambda i: (i, 0))
    vec = pl.BlockSpec((1, R), lambda i: (0, 0))
    return pl.pallas_call(
        body, name=name, grid=(T // tm,), in_specs=[row, vec], out_specs=row,
        out_shape=jax.ShapeDtypeStruct((T, R), BF16), compiler_params=_cparams(),
    )(x, g)


def _rms_bwd(dy, x, g, name):
    T, R = x.shape
    tm = _tile(T, (512, 256, 128))

    def body(dy_ref, x_ref, g_ref, dx_ref, dg_ref):
        @pl.when(pl.program_id(0) == 0)
        def _():
            dg_ref[...] = jnp.zeros_like(dg_ref)

        xv = x_ref[...]
        dyv = dy_ref[...]
        r = lax.rsqrt(jnp.mean(xv * xv, axis=1, keepdims=True) + RMS_EPS)
        dg_ref[...] += jnp.sum(dyv * xv * r, axis=0, keepdims=True)
        do = dyv * g_ref[...]
        dx = r * do - xv * (r * r * r) * jnp.mean(do * xv, axis=1, keepdims=True)
        dx_ref[...] = dx.astype(BF16)

    row = pl.BlockSpec((tm, R), lambda i: (i, 0))
    vec = pl.BlockSpec((1, R), lambda i: (0, 0))
    return pl.pallas_call(
        body, name=name, grid=(T // tm,), in_specs=[row, row, vec], out_specs=[row, vec],
        out_shape=[jax.ShapeDtypeStruct((T, R), BF16), jax.ShapeDtypeStruct((1, R), F32)],
        compiler_params=_cparams(),
    )(dy, x, g)


def _swap_halves(x):
    half = MLA_ROPE // 2
    return jnp.where(_iota(x.shape, 1) < half, pltpu.roll(x, LANES - half, 1), pltpu.roll(x, half, 1))


def _rope(x, cc, ss, *, transpose, sum_heads=False, name):
    T, W = x.shape
    nh = W // LANES
    tm = _tile(T, (512, 256, 128))
    n_out = 1 if sum_heads else nh

    def body(x_ref, cc_ref, ss_ref, o_ref):
        ccv = cc_ref[...]
        ssv = ss_ref[...]
        if sum_heads:
            xv = x_ref[:, 0:LANES]
            for hh in range(1, nh):
                xv = xv + x_ref[:, hh * LANES:(hh + 1) * LANES]
            heads = [xv]
        else:
            heads = [x_ref[:, hh * LANES:(hh + 1) * LANES] for hh in range(nh)]
        for hh, xv in enumerate(heads):
            if transpose:
                r = xv * ccv + _swap_halves(xv * ssv)
            else:
                r = xv * ccv + _swap_halves(xv) * ssv
            r = jnp.where(_iota(r.shape, 1) < MLA_ROPE, r, 0.0)
            o_ref[:, hh * LANES:(hh + 1) * LANES] = r.astype(BF16)

    return pl.pallas_call(
        body, name=name, grid=(T // tm,),
        in_specs=[pl.BlockSpec((tm, W), lambda i: (i, 0)), pl.BlockSpec((tm, LANES), lambda i: (i, 0)),
                  pl.BlockSpec((tm, LANES), lambda i: (i, 0))],
        out_specs=pl.BlockSpec((tm, n_out * LANES), lambda i: (i, 0)),
        out_shape=jax.ShapeDtypeStruct((T, n_out * LANES), BF16), compiler_params=_cparams(),
    )(x, cc, ss)


def _attn_scores(qn, qr, kn, kr, i, j, tq, tk, scale):
    s = (_bdot(qn, kn, NT) + _bdot(qr, kr, NT)) * scale
    qpos = i * tq + _iota((tq, tk), 0)
    kpos = j * tk + _iota((tq, tk), 1)
    return jnp.where(kpos <= qpos, s, NEG_BIG)


def _attn_fwd(qn, qr, kn, kr, v, S, scale, name):
    T, W = qn.shape
    nh = W // LANES
    Bn = T // S
    tq = tk = _tile(S, (512, 256, 128))
    nq = S // tq
    nk = S // tk

    def body(qn_ref, qr_ref, kn_ref, kr_ref, v_ref, o_ref, ob_ref, lse_ref, m_sc, l_sc, acc):
        i = pl.program_id(2)
        j = pl.program_id(3)

        @pl.when(j == 0)
        def _():
            m_sc[...] = jnp.full_like(m_sc, NEG_BIG)
            l_sc[...] = jnp.zeros_like(l_sc)
            acc[...] = jnp.zeros_like(acc)

        @pl.when(j <= i)
        def _():
            s = _attn_scores(qn_ref[...], qr_ref[...], kn_ref[...], kr_ref[...], i, j, tq, tk, scale)
            m_old = m_sc[...]
            m_new = jnp.maximum(m_old, jnp.max(s, axis=1, keepdims=True))
            corr = jnp.exp(m_old - m_new)
            p = jnp.exp(s - m_new)
            l_sc[...] = corr * l_sc[...] + jnp.sum(p, axis=1, keepdims=True)
            acc[...] = corr * acc[...] + _bdot(p, v_ref[...], NN)
            m_sc[...] = m_new

        @pl.when(j == nk - 1)
        def _():
            ov = acc[...] / l_sc[...]
            o_ref[...] = ov
            ob_ref[...] = ov.astype(BF16)
            lse_ref[...] = jnp.broadcast_to(m_sc[...] + jnp.log(l_sc[...]), (tq, LANES))

    qspec = pl.BlockSpec((tq, LANES), lambda b, h, i, j: (b * nq + i, h))
    kspec = pl.BlockSpec((tk, LANES), lambda b, h, i, j: (b * nk + jnp.minimum(j, i), h))
    krspec = pl.BlockSpec((tk, LANES), lambda b, h, i, j: (b * nk + jnp.minimum(j, i), 0))
    return pl.pallas_call(
        body, name=name, grid=(Bn, nh, nq, nk),
        in_specs=[qspec, qspec, kspec, krspec, kspec], out_specs=[qspec, qspec, qspec],
        out_shape=[jax.ShapeDtypeStruct((T, W), F32), jax.ShapeDtypeStruct((T, W), BF16),
                   jax.ShapeDtypeStruct((T, W), F32)],
        scratch_shapes=[pltpu.VMEM((tq, 1), F32), pltpu.VMEM((tq, 1), F32), pltpu.VMEM((tq, LANES), F32)],
        compiler_params=_cparams(),
    )(qn, qr, kn, kr, v)


def _attn_bwd_dq(qn, qr, kn, kr, v, o, do, lse, S, scale, name):
    T, W = qn.shape
    nh = W // LANES
    Bn = T // S
    tq = tk = _tile(S, (512, 256, 128))
    nq = S // tq
    nk = S // tk

    def body(qn_ref, qr_ref, kn_ref, kr_ref, v_ref, o_ref, do_ref, lse_ref, dqn_ref, dqr_ref, an, ar):
        i = pl.program_id(2)
        j = pl.program_id(3)

        @pl.when(j == 0)
        def _():
            an[...] = jnp.zeros_like(an)
            ar[...] = jnp.zeros_like(ar)

        @pl.when(j <= i)
        def _():
            s = _attn_scores(qn_ref[...], qr_ref[...], kn_ref[...], kr_ref[...], i, j, tq, tk, scale)
            p = jnp.exp(s - jnp.max(lse_ref[...], axis=1, keepdims=True))
            dov = do_ref[...]
            di = jnp.sum(dov * o_ref[...], axis=1, keepdims=True)
            ds = p * (_bdot(dov, v_ref[...], NT) - di) * scale
            an[...] += _bdot(ds, kn_ref[...], NN)
            ar[...] += _bdot(ds, kr_ref[...], NN)

        @pl.when(j == nk - 1)
        def _():
            dqn_ref[...] = an[...].astype(BF16)
            dqr_ref[...] = ar[...]

    qspec = pl.BlockSpec((tq, LANES), lambda b, h, i, j: (b * nq + i, h))
    kspec = pl.BlockSpec((tk, LANES), lambda b, h, i, j: (b * nk + jnp.minimum(j, i), h))
    krspec = pl.BlockSpec((tk, LANES), lambda b, h, i, j: (b * nk + jnp.minimum(j, i), 0))
    return pl.pallas_call(
        body, name=name, grid=(Bn, nh, nq, nk),
        in_specs=[qspec, qspec, kspec, krspec, kspec, qspec, qspec, qspec], out_specs=[qspec, qspec],
        out_shape=[jax.ShapeDtypeStruct((T, W), BF16), jax.ShapeDtypeStruct((T, W), F32)],
        scratch_shapes=[pltpu.VMEM((tq, LANES), F32), pltpu.VMEM((tq, LANES), F32)],
        compiler_params=_cparams(),
    )(qn, qr, kn, kr, v, o, do, lse)


def _attn_bwd_dkv(qn, qr, kn, kr, v, o, do, lse, S, scale, name):
    T, W = qn.shape
    nh = W // LANES
    Bn = T // S
    tq = tk = _tile(S, (512, 256, 128))
    nq = S // tq
    nk = S // tk

    def body(qn_ref, qr_ref, kn_ref, kr_ref, v_ref, o_ref, do_ref, lse_ref, dkn_ref, dv_ref, dkr_ref, akn, av):
        j = pl.program_id(1)
        h = pl.program_id(2)
        i = pl.program_id(3)

        @pl.when(jnp.logical_and(h == 0, i == 0))
        def _():
            dkr_ref[...] = jnp.zeros_like(dkr_ref)

        @pl.when(i == 0)
        def _():
            akn[...] = jnp.zeros_like(akn)
            av[...] = jnp.zeros_like(av)

        @pl.when(i >= j)
        def _():
            s = _attn_scores(qn_ref[...], qr_ref[...], kn_ref[...], kr_ref[...], i, j, tq, tk, scale)
            p = jnp.exp(s - jnp.max(lse_ref[...], axis=1, keepdims=True))
            dov = do_ref[...]
            di = jnp.sum(dov * o_ref[...], axis=1, keepdims=True)
            ds = p * (_bdot(dov, v_ref[...], NT) - di) * scale
            dsT = ds.T
            av[...] += _bdot(p.T, dov, NN)
            akn[...] += _bdot(dsT, qn_ref[...], NN)
            dkr_ref[...] += _bdot(dsT, qr_ref[...], NN)

        @pl.when(i == nq - 1)
        def _():
            dkn_ref[...] = akn[...].astype(BF16)
            dv_ref[...] = av[...].astype(BF16)

    qspec = pl.BlockSpec((tq, LANES), lambda b, j, h, i: (b * nq + jnp.maximum(i, j), h))
    kspec = pl.BlockSpec((tk, LANES), lambda b, j, h, i: (b * nk + j, h))
    krspec = pl.BlockSpec((tk, LANES), lambda b, j, h, i: (b * nk + j, 0))
    return pl.pallas_call(
        body, name=name, grid=(Bn, nk, nh, nq),
        in_specs=[qspec, qspec, kspec, krspec, kspec, qspec, qspec, qspec],
        out_specs=[kspec, kspec, krspec],
        out_shape=[jax.ShapeDtypeStruct((T, W), BF16), jax.ShapeDtypeStruct((T, W), BF16),
                   jax.ShapeDtypeStruct((T, LANES), F32)],
        scratch_shapes=[pltpu.VMEM((tk, LANES), F32), pltpu.VMEM((tk, LANES), F32)],
        compiler_params=_cparams(),
    )(qn, qr, kn, kr, v, o, do, lse)


def _pad_cols(w, n):
    return jnp.pad(w, ((0, 0), (0, n - w.shape[1])))


def _local_step(x, positions, w, target):
    Bn, S, D = x.shape
    T = Bn * S
    depth = w['ffn_up'].shape[0]
    alpha = (2 * depth) ** 0.25
    d_inner = w['ssd_out_proj'].shape[1]
    n_heads_ssd = d_inner // SSD_HEAD_DIM
    G = SSD_N_GROUPS
    gn = G * SSD_D_STATE
    conv_dim = d_inner + 2 * gn
    hpg = n_heads_ssd // G
    nh = D // MLA_NOPE
    kv_rank = w['kv_up_k'].shape[0]
    q_rank = w['q_down_proj'].shape[2]
    H = w['ffn_down'].shape[1]
    scale = (MLA_NOPE + MLA_ROPE) ** -0.5
    assert S % SSD_CHUNK == 0 and hpg % 2 == 0 and SSD_D_STATE == LANES and n_heads_ssd <= LANES

    X = x.reshape(T, D)
    tgt = target.reshape(T, D)

    inv_freq = 1.0 / (ROPE_THETA ** (jnp.arange(0, MLA_ROPE, 2, dtype=F32) / MLA_ROPE))
    ang = positions.reshape(T).astype(F32)[:, None] * inv_freq
    cos, sin = jnp.cos(ang), jnp.sin(ang)
    zpad = jnp.zeros((T, LANES - MLA_ROPE), F32)
    cc = jnp.concatenate([cos, cos, zpad], axis=1)
    ss = jnp.concatenate([-sin, sin, zpad], axis=1)

    w_in = w['ssd_in_proj'][0]
    w_z = w_in[:, :d_inner]
    w_xbc = w_in[:, d_inner:d_inner + conv_dim]
    w_dt = _pad_cols(w_in[:, d_inner + conv_dim:], LANES)
    w_out = w['ssd_out_proj'][0]
    ssd_cw = w['ssd_conv_w'][0]
    ssd_cb = w['ssd_conv_b']
    dt_bias = _pad_cols(w['ssd_dt_bias'], LANES)
    a_log = _pad_cols(w['ssd_A_log'], LANES)
    d_lane = jnp.repeat(w['ssd_D'], SSD_HEAD_DIM, axis=1)
    ssd_ng = w['ssd_norm_g']
    w_kvc = w['kv_down_proj'][:, :kv_rank]
    w_kvr = _pad_cols(w['kv_down_proj'][:, kv_rank:], LANES)
    kv_g = w['kv_norm_g'].reshape(1, kv_rank)
    w_uk = w['kv_up_k']
    w_uv = w['kv_up_v']
    w_qd = w['q_down_proj'][0]
    q_g = w['q_norm_g']
    qu = w['q_up_proj'][0].reshape(q_rank, nh, MLA_NOPE + MLA_ROPE)
    w_qn = qu[:, :, :MLA_NOPE].reshape(q_rank, nh * LANES)
    w_qr = jnp.pad(qu[:, :, MLA_NOPE:], ((0, 0), (0, 0), (0, LANES - MLA_ROPE))).reshape(q_rank, nh * LANES)
    w_ao = w['attn_out_proj'][0]

    def ffn_parts(i):
        return dict(wu=_interleave(w['ffn_up'][i], H), cw=_interleave(w['ffn_conv_w'][i], H),
                    cb=_interleave(w['ffn_conv_b'][i:i + 1], H), wd=w['ffn_down'][i])

    def ln(name, i):
        return w[name][i:i + 1]

    def ffn_fwd(hb, i):
        f = ffn_parts(i)
        u = _mm(hb, f['wu'], name=f'ffn{i}_up')
        a = _ffn_act_fwd(u, f['cw'], f['cb'], S, name=f'ffn{i}_act')
        return _mm(a, f['wd'], name=f'ffn{i}_down'), (u, a)

    Xb = X.astype(BF16)
    z = _mm(Xb, w_z, name='ssd_in_z')
    xbc_pre = _mm(Xb, w_xbc, name='ssd_in_xbc')
    dt_pre = _mm(Xb, w_dt, name='ssd_in_dt')
    xbc = _ssd_conv_fwd(xbc_pre, ssd_cw, ssd_cb, S, name='ssd_conv')
    y_scan, states = _scan_fwd(xbc, dt_pre, dt_bias, a_log, S, d_inner, name='ssd_scan')
    yn = _gate_fwd(y_scan, xbc, z, d_lane, ssd_ng, d_inner, name='ssd_gate')
    mix0 = _mm(yn, w_out, name='ssd_out')
    h1, s1, h1b = _ln_fwd(X, mix0, ln('ln_mix_g', 0), ln('ln_mix_b', 0), alpha, name='ln_mix0')
    ff0, ffn0_saved = ffn_fwd(h1b, 0)
    h2, s2, h2b = _ln_fwd(h1, ff0, ln('ln_ffn_g', 0), ln('ln_ffn_b', 0), alpha, name='ln_ffn0')

    ckv = _mm(h2b, w_kvc, name='kv_down_c')
    kr_pre = _mm(h2b, w_kvr, name='kv_down_r')
    ckvn = _rms_fwd(ckv, kv_g, name='kv_norm')
    kr = _rope(kr_pre, cc, ss, transpose=False, name='k_rope')
    kn = _mm(ckvn, w_uk, out_dtype=BF16, name='kv_up_k')
    v = _mm(ckvn, w_uv, out_dtype=BF16, name='kv_up_v')
    cq_pre = _mm(h2b, w_qd, name='q_down')
    cq = _rms_fwd(cq_pre, q_g, name='q_norm')
    qn = _mm(cq, w_qn, out_dtype=BF16, name='q_up_n')
    qr_pre = _mm(cq, w_qr, name='q_up_r')
    qr = _rope(qr_pre, cc, ss, transpose=False, name='q_rope')
    o, ob, lse = _attn_fwd(qn, qr, kn, kr, v, S, scale, name='attn_fwd')
    mix1 = _mm(ob, w_ao, name='attn_out')
    h3, s3, h3b = _ln_fwd(h2, mix1, ln('ln_mix_g', 1), ln('ln_mix_b', 1), alpha, name='ln_mix1')
    ff1, ffn1_saved = ffn_fwd(h3b, 1)
    h4, s4, _ = _ln_fwd(h3, ff1, ln('ln_ffn_g', 1), ln('ln_ffn_b', 1), alpha, name='ln_ffn1')
    sq, dh4 = _loss_head(h4, tgt, name='loss_head')

    g = {}

    def ffn_bwd(ds, dsb, hb_in, saved, i):
        f = ffn_parts(i)
        u, a = saved
        d_wd = _mm(a, dsb, ta=True, name=f'ffn{i}_down_dw')
        da = _mm(dsb, f['wd'], tb=True, out_dtype=BF16, name=f'ffn{i}_down_dx')
        du, dcw, dcb = _ffn_act_bwd(u, da, f['cw'], f['cb'], S, name=f'ffn{i}_act_bwd')
        d_wu = _deinterleave(_mm(hb_in, du, ta=True, name=f'ffn{i}_up_dw'), H)
        dh = _mm(du, f['wu'], tb=True, add=ds, add_scale=alpha, name=f'ffn{i}_up_dx')
        return dh, d_wd, d_wu, _deinterleave(dcw, H), _deinterleave(dcb, H)

    ds4, ds4b, dg_f1, db_f1 = _ln_bwd(dh4, s4, ln('ln_ffn_g', 1), name='ln_ffn1_bwd')
    dh3, d_wd1, d_wu1, d_fcw1, d_fcb1 = ffn_bwd(ds4, ds4b, h3b, ffn1_saved, 1)
    ds3, ds3b, dg_m1, db_m1 = _ln_bwd(dh3, s3, ln('ln_mix_g', 1), name='ln_mix1_bwd')

    g['attn_out_proj'] = _mm(ob, ds3b, ta=True, name='attn_out_dw')[None]
    do = _mm(ds3b, w_ao, tb=True, name='attn_out_dx')
    dqn, dqr = _attn_bwd_dq(qn, qr, kn, kr, v, o, do, lse, S, scale, name='attn_bwd_dq')
    dkn, dv, dkr = _attn_bwd_dkv(qn, qr, kn, kr, v, o, do, lse, S, scale, name='attn_bwd_dkv')
    dqr_pre = _rope(dqr, cc, ss, transpose=True, name='q_rope_bwd')
    dkr_pre = _rope(dkr, cc, ss, transpose=True, name='k_rope_bwd')

    d_wqn = _mm(cq, dqn, ta=True, name='q_up_n_dw').reshape(q_rank, nh, LANES)
    d_wqr = _mm(cq, dqr_pre, ta=True, name='q_up_r_dw').reshape(q_rank, nh, LANES)[:, :, :MLA_ROPE]
    g['q_up_proj'] = jnp.concatenate([d_wqn, d_wqr], axis=2).reshape(1, q_rank, nh * (MLA_NOPE + MLA_ROPE))
    dcq = _mm(dqn, w_qn, tb=True, name='q_up_n_dx')
    dcq = _mm(dqr_pre, w_qr, tb=True, add=dcq, name='q_up_r_dx')
    dcq_pre, d_qg = _rms_bwd(dcq, cq_pre, q_g, name='q_norm_bwd')
    g['q_norm_g'] = d_qg
    g['q_down_proj'] = _mm(h2b, dcq_pre, ta=True, name='q_down_dw')[None]

    g['kv_up_k'] = _mm(ckvn, dkn, ta=True, name='kv_up_k_dw')
    g['kv_up_v'] = _mm(ckvn, dv, ta=True, name='kv_up_v_dw')
    dckvn = _mm(dkn, w_uk, tb=True, name='kv_up_k_dx')
    dckvn = _mm(dv, w_uv, tb=True, add=dckvn, name='kv_up_v_dx')
    dckv, d_kvg = _rms_bwd(dckvn, ckv, kv_g, name='kv_norm_bwd')
    g['kv_norm_g'] = d_kvg.reshape(kv_rank)
    g['kv_down_proj'] = jnp.concatenate(
        [_mm(h2b, dckv, ta=True, name='kv_down_c_dw'),
         _mm(h2b, dkr_pre, ta=True, name='kv_down_r_dw')[:, :MLA_ROPE]], axis=1)

    dh2 = _mm(dcq_pre, w_qd, tb=True, add=ds3, add_scale=alpha, name='q_down_dx')
    dh2 = _mm(dckv, w_kvc, tb=True, add=dh2, name='kv_down_c_dx')
    dh2 = _mm(dkr_pre, w_kvr, tb=True, add=dh2, name='kv_down_r_dx')

    ds2, ds2b, dg_f0, db_f0 = _ln_bwd(dh2, s2, ln('ln_ffn_g', 0), name='ln_ffn0_bwd')
    dh1, d_wd0, d_wu0, d_fcw0, d_fcb0 = ffn_bwd(ds2, ds2b, h1b, ffn0_saved, 0)
    ds1, ds1b, dg_m0, db_m0 = _ln_bwd(dh1, s1, ln('ln_mix_g', 0), name='ln_mix0_bwd')

    g['ssd_out_proj'] = _mm(yn, ds1b, ta=True, name='ssd_out_dw')[None]
    dyn = _mm(ds1b, w_out, tb=True, out_dtype=BF16, name='ssd_out_dx')
    dy1, dz, d_ng, d_dl = _gate_bwd(dyn, y_scan, xbc, z, d_lane, ssd_ng, d_inner, name='ssd_gate_bwd')
    dxs, dB, dC, ddt_g, da_g = _scan_bwd(xbc, dt_pre, dt_bias, a_log, states, dy1, d_lane, S, d_inner,
                                         name='ssd_scan_bwd')
    ddt_pre, d_bias, d_alog = _dt_bwd(ddt_g, da_g, dt_pre, dt_bias, a_log, name='ssd_dt_bwd')
    dxbc = jnp.concatenate([dxs, dB, dC], axis=1)
    dxbc_pre, d_scw, d_scb = _ssd_conv_bwd(xbc_pre, dxbc, ssd_cw, ssd_cb, S, name='ssd_conv_bwd')
    g['ssd_in_proj'] = jnp.concatenate(
        [_mm(Xb, dz, ta=True, name='ssd_in_z_dw'), _mm(Xb, dxbc_pre, ta=True, name='ssd_in_xbc_dw'),
         _mm(Xb, ddt_pre, ta=True, name='ssd_in_dt_dw')[:, :n_heads_ssd]], axis=1)[None]
    dx = _mm(dz, w_z, tb=True, add=ds1, add_scale=alpha, name='ssd_in_z_dx')
    dx = _mm(dxbc_pre, w_xbc, tb=True, add=dx, name='ssd_in_xbc_dx')
    dx = _mm(ddt_pre, w_dt, tb=True, add=dx, name='ssd_in_dt_dx')

    g['ssd_conv_w'] = d_scw[None]
    g['ssd_conv_b'] = d_scb
    g['ssd_dt_bias'] = d_bias[:, :n_heads_ssd]
    g['ssd_A_log'] = d_alog[:, :n_heads_ssd]
    g['ssd_D'] = jnp.sum(d_dl.reshape(1, n_heads_ssd, SSD_HEAD_DIM), axis=2)
    g['ssd_norm_g'] = d_ng
    g['ffn_up'] = jnp.stack([d_wu0, d_wu1])
    g['ffn_conv_w'] = jnp.stack([d_fcw0, d_fcw1])
    g['ffn_conv_b'] = jnp.concatenate([d_fcb0, d_fcb1], axis=0)
    g['ffn_down'] = jnp.stack([d_wd0, d_wd1])
    g['ln_mix_g'] = jnp.concatenate([dg_m0, dg_m1], axis=0)
    g['ln_mix_b'] = jnp.concatenate([db_m0, db_m1], axis=0)
    g['ln_ffn_g'] = jnp.concatenate([dg_f0, dg_f1], axis=0)
    g['ln_ffn_b'] = jnp.concatenate([db_f0, db_f1], axis=0)
    return sq, dx.reshape(Bn, S, D), g


ANY = pl.BlockSpec(memory_space=pl.ANY)


def _all_gather(xs, name):
    n = len(xs)

    def body(*refs):
        x_refs, out_refs = refs[:n], refs[n:2 * n]
        send_sems, recv_sems, local_sems = refs[2 * n:]
        x, y, c = lax.axis_index("x"), lax.axis_index("y"), lax.axis_index("c")
        me, sibling = (x, y, c), (x, y, 1 - c)
        chips = [(1 - x, y), (x, 1 - y), (1 - x, 1 - y)]

        def rows(i, px, py, pc):
            return out_refs[i].at[4 * px + 2 * py + pc]

        def copy(i, k, block, to, own=False):
            return pltpu.make_async_remote_copy(
                src_ref=x_refs[i] if own else rows(i, *block), dst_ref=rows(i, *block),
                send_sem=send_sems.at[7 * i + k], recv_sem=recv_sems.at[7 * i + k],
                device_id=to, device_id_type=MESH)

        mine = [pltpu.make_async_copy(x_refs[i], rows(i, *me), local_sems.at[i]) for i in range(n)]
        first = []
        for i in range(n):
            first.append(copy(i, 0, me, sibling, own=True))
            first += [copy(i, 1 + j, me, (*chip, c), own=True) for j, chip in enumerate(chips)]
        for cp in mine + first:
            cp.start()
        passed = []
        for j, chip in enumerate(chips):
            for i in range(n):
                copy(i, 1 + j, (*chip, c), me).wait_recv()
                passed.append(copy(i, 4 + j, (*chip, c), sibling))
                passed[-1].start()
        for i in range(n):
            copy(i, 0, sibling, me).wait_recv()
            for j, chip in enumerate(chips):
                copy(i, 4 + j, (*chip, 1 - c), me).wait_recv()
        for cp in first + passed:
            cp.wait_send()
        for cp in mine:
            cp.wait()

    return pl.pallas_call(
        body, name=name, out_shape=[jax.ShapeDtypeStruct((N_DEV,) + a.shape, a.dtype) for a in xs],
        in_specs=[ANY] * n, out_specs=[ANY] * n,
        scratch_shapes=[pltpu.SemaphoreType.DMA((7 * n,)), pltpu.SemaphoreType.DMA((7 * n,)),
                        pltpu.SemaphoreType.DMA((n,))],
    )(*xs)


def _exchange_sibling(gs, name):
    n = len(gs)

    def body(*refs):
        g_refs, r_refs = refs[:n], refs[n:2 * n]
        send_sems, recv_sems = refs[2 * n:]
        x, y, c = lax.axis_index("x"), lax.axis_index("y"), lax.axis_index("c")
        copies = [pltpu.make_async_remote_copy(
            src_ref=g_refs[i].at[2 * k + (1 - c)], dst_ref=r_refs[i].at[k], send_sem=send_sems.at[4 * i + k],
            recv_sem=recv_sems.at[4 * i + k], device_id=(x, y, 1 - c), device_id_type=MESH)
            for i in range(n) for k in range(4)]
        for cp in copies:
            cp.start()
        for cp in copies:
            cp.wait()

    return pl.pallas_call(
        body, name=name, out_shape=[jax.ShapeDtypeStruct((4,) + g.shape[1:], g.dtype) for g in gs],
        in_specs=[ANY] * n, out_specs=[ANY] * n,
        scratch_shapes=[pltpu.SemaphoreType.DMA((4 * n,)), pltpu.SemaphoreType.DMA((4 * n,))],
    )(*gs)


def _exchange_chips(parts, name):
    n = len(parts)

    def body(*refs):
        p_refs, r_refs = refs[:n], refs[n:2 * n]
        send_sems, recv_sems = refs[2 * n:]
        x, y, c = lax.axis_index("x"), lax.axis_index("y"), lax.axis_index("c")
        chips = [(1 - x, y), (x, 1 - y), (1 - x, 1 - y)]
        copies = [pltpu.make_async_remote_copy(
            src_ref=p_refs[i].at[2 * cx + cy], dst_ref=r_refs[i].at[j], send_sem=send_sems.at[3 * i + j],
            recv_sem=recv_sems.at[3 * i + j], device_id=(cx, cy, c), device_id_type=MESH)
            for i in range(n) for j, (cx, cy) in enumerate(chips)]
        for cp in copies:
            cp.start()
        for cp in copies:
            cp.wait()

    return pl.pallas_call(
        body, name=name, out_shape=[jax.ShapeDtypeStruct((3,) + p.shape[1:], p.dtype) for p in parts],
        in_specs=[ANY] * n, out_specs=[ANY] * n,
        scratch_shapes=[pltpu.SemaphoreType.DMA((3 * n,)), pltpu.SemaphoreType.DMA((3 * n,))],
    )(*parts)


def _row_tile(rows, cols):
    want = max(8, (256 * 1024) // cols)
    for t in range(min(rows, want) // 8 * 8, 7, -8):
        if rows % t == 0:
            return t
    return rows


def _add_sibling(gfull, r1, core, name):
    _, rows, lanes = gfull.shape
    tr = _row_tile(rows, lanes)

    def body(c_ref, g_ref, r_ref, o_ref):
        o_ref[...] = g_ref[...] + r_ref[...]

    return pl.pallas_call(
        body, name=name,
        grid_spec=pltpu.PrefetchScalarGridSpec(
            num_scalar_prefetch=1, grid=(4, rows // tr),
            in_specs=[pl.BlockSpec((1, tr, lanes), lambda k, r, c_ref: (2 * k + c_ref[0], r, 0)),
                      pl.BlockSpec((1, tr, lanes), lambda k, r, c_ref: (k, r, 0))],
            out_specs=pl.BlockSpec((1, tr, lanes), lambda k, r, c_ref: (k, r, 0))),
        out_shape=jax.ShapeDtypeStruct((4, rows, lanes), F32), compiler_params=_cparams(),
    )(core, gfull, r1)


def _adam_math(wv, gv, mv, vv):
    m = ADAM_B1 * mv + (1.0 - ADAM_B1) * gv
    v = ADAM_B2 * vv + (1.0 - ADAM_B2) * (gv * gv)
    m_hat = m / (1.0 - ADAM_B1 ** ADAM_STEP)
    v_hat = v / (1.0 - ADAM_B2 ** ADAM_STEP)
    delta = -ADAM_LR * (m_hat / (jnp.sqrt(v_hat) + ADAM_EPS) + ADAM_WD * wv)
    return delta, m, v


def _adam_sharded(part, r2, chip, wts, m, v, name):
    rows, lanes = wts.shape
    tr = _row_tile(rows, lanes)

    def body(k_ref, p_ref, r_ref, w_ref, m_ref, v_ref, g_ref, d_ref, mo_ref, vo_ref):
        gv = p_ref[0] + r_ref[0] + r_ref[1] + r_ref[2]
        delta, mn, vn = _adam_math(w_ref[...], gv, m_ref[...], v_ref[...])
        g_ref[...] = gv
        d_ref[...] = delta
        mo_ref[...] = mn
        vo_ref[...] = vn

    flat = pl.BlockSpec((tr, lanes), lambda r, k_ref: (r, 0))
    sh = jax.ShapeDtypeStruct((rows, lanes), F32)
    return pl.pallas_call(
        body, name=name,
        grid_spec=pltpu.PrefetchScalarGridSpec(
            num_scalar_prefetch=1, grid=(rows // tr,),
            in_specs=[pl.BlockSpec((1, tr, lanes), lambda r, k_ref: (k_ref[0], r, 0)),
                      pl.BlockSpec((3, tr, lanes), lambda r, k_ref: (0, r, 0)), flat, flat, flat],
            out_specs=[flat, flat, flat, flat]),
        out_shape=[sh, sh, sh, sh], compiler_params=_cparams(),
    )(chip, part, r2, wts, m, v)


def _adam_replicated(gall, wts, m, v, name):
    rows, lanes = wts.shape

    def body(ga_ref, w_ref, m_ref, v_ref, g_ref, d_ref, mo_ref, vo_ref):
        gv = ga_ref[0]
        for k in range(1, N_DEV):
            gv = gv + ga_ref[k]
        delta, mn, vn = _adam_math(w_ref[...], gv, m_ref[...], v_ref[...])
        g_ref[...] = gv
        d_ref[...] = delta
        mo_ref[...] = mn
        vo_ref[...] = vn

    sh = jax.ShapeDtypeStruct((rows, lanes), F32)
    return pl.pallas_call(body, name=name, out_shape=[sh, sh, sh, sh], compiler_params=_cparams())(gall, wts, m, v)


def _pack(arrs, dtype, row_mult):
    flat = jnp.concatenate([a.reshape(-1).astype(dtype) for a in arrs])
    n = flat.shape[0]
    unit = LANES * row_mult
    total = -(-n // unit) * unit
    return jnp.pad(flat, (0, total - n)).reshape(total // LANES, LANES)


def _unpack(flat2d, shapes):
    flat = flat2d.reshape(-1)
    out, off = [], 0
    for shp in shapes:
        n = math.prod(shp)
        out.append(flat[off:off + n].reshape(shp))
        off += n
    return out


def _unpack_gathered(gathered, shapes, axes):
    flat = gathered.reshape(N_DEV, -1)
    out, off = [], 0
    for shp, ax in zip(shapes, axes):
        n = math.prod(shp)
        seg = flat[:, off:off + n].reshape((N_DEV,) + tuple(shp))
        out.append(jnp.concatenate([seg[i] for i in range(N_DEV)], axis=ax))
        off += n
    return out


def _pack_chunks(fulls, axes, row_mult):
    per_dev = []
    for full, ax in zip(fulls, axes):
        parts = jnp.stack(jnp.split(full, N_DEV, axis=ax))
        per_dev.append(parts.reshape(N_DEV, -1))
    flat = jnp.concatenate(per_dev, axis=1)
    n = flat.shape[1]
    unit = LANES * row_mult
    total = -(-n // unit) * unit
    return jnp.pad(flat, ((0, 0), (0, total - n))).reshape(N_DEV, total // LANES, LANES)


def kernel(x, positions, ssd_in_proj, ssd_conv_w, ssd_conv_b, ssd_dt_bias, ssd_A_log, ssd_D, ssd_norm_g, ssd_out_proj, kv_down_proj, kv_norm_g, kv_up_k, kv_up_v, q_down_proj, q_norm_g, q_up_proj, attn_out_proj, ffn_up, ffn_conv_w, ffn_conv_b, ffn_down, ln_mix_g, ln_mix_b, ln_ffn_g, ln_ffn_b, loss_target, m_ssd_in_proj, m_ssd_conv_w, m_ssd_conv_b, m_ssd_dt_bias, m_ssd_A_log, m_ssd_D, m_ssd_norm_g, m_ssd_out_proj, m_kv_down_proj, m_kv_norm_g, m_kv_up_k, m_kv_up_v, m_q_down_proj, m_q_norm_g, m_q_up_proj, m_attn_out_proj, m_ffn_up, m_ffn_conv_w, m_ffn_conv_b, m_ffn_down, m_ln_mix_g, m_ln_mix_b, m_ln_ffn_g, m_ln_ffn_b, v_ssd_in_proj, v_ssd_conv_w, v_ssd_conv_b, v_ssd_dt_bias, v_ssd_A_log, v_ssd_D, v_ssd_norm_g, v_ssd_out_proj, v_kv_down_proj, v_kv_norm_g, v_kv_up_k, v_kv_up_v, v_q_down_proj, v_q_norm_g, v_q_up_proj, v_attn_out_proj, v_ffn_up, v_ffn_conv_w, v_ffn_conv_b, v_ffn_down, v_ln_mix_g, v_ln_mix_b, v_ln_ffn_g, v_ln_ffn_b):
    given = dict(locals())
    wl = {n: given[n] for n in WEIGHTS}
    ml = {n: given['m_' + n] for n in WEIGHTS}
    vl = {n: given['v_' + n] for n in WEIGHTS}
    sharded_axis = dict(SHARDED)
    big = [n for n, _ in SHARDED if n not in SHARDED_F32]
    small = [n for n, _ in SHARDED if n in SHARDED_F32]

    def as2d(a):
        return a.reshape(-1, a.shape[-1])

    def to_full(gathered, n):
        shp, ax = wl[n].shape, sharded_axis[n]
        moved = jnp.moveaxis(gathered.reshape((N_DEV,) + shp), 0, ax)
        return moved.reshape(shp[:ax] + (N_DEV * shp[ax],) + shp[ax + 1:])

    def to_chunks(full_grad, n):
        shp, ax = wl[n].shape, sharded_axis[n]
        split = full_grad.reshape(shp[:ax] + (N_DEV, shp[ax]) + shp[ax + 1:])
        return jnp.moveaxis(split, ax, 0).reshape((N_DEV,) + as2d(wl[n]).shape)

    full = {n: wl[n] for n in REPLICATED}
    gathered = _all_gather([as2d(wl[n]).astype(BF16) for n in big] + [_pack([wl[n] for n in small], F32, 8)],
                           name='gather_weights')
    for n, gat in zip(big, gathered[:-1]):
        full[n] = to_full(gat, n)
    full.update(zip(small, _unpack_gathered(gathered[-1], [wl[n].shape for n in small],
                                            [sharded_axis[n] for n in small])))

    sq, grad_x, grads = _local_step(x, positions, full, loss_target)
    loss = lax.psum(0.5 * jnp.sum(sq) / x.shape[-1], ("x", "y", "c"))

    cx, cy, cc = lax.axis_index("x"), lax.axis_index("y"), lax.axis_index("c")
    core = jnp.reshape(cc, (1,)).astype(jnp.int32)
    chip = jnp.reshape(2 * cx + cy, (1,)).astype(jnp.int32)
    gfull = [to_chunks(grads[n], n) for n in big]
    gfull.append(_pack_chunks([grads[n] for n in small], [sharded_axis[n] for n in small], 8))
    r1 = _exchange_sibling(gfull, name='grads_to_sibling')
    parts = [_add_sibling(gf, r, core, name=f'grads_add_sibling_{i}') for i, (gf, r) in enumerate(zip(gfull, r1))]
    r2 = _exchange_chips(parts, name='grads_to_chips')
    res = {}
    kinds = ('grad', 'delta', 'new_m', 'new_v')
    for i, n in enumerate(big):
        outs = _adam_sharded(parts[i], r2[i], chip, as2d(wl[n]), as2d(ml[n]), as2d(vl[n]), name=f'adamw_{n}')
        for kind, a in zip(kinds, outs):
            res[kind, n] = a.reshape(wl[n].shape)
    outs = _adam_sharded(parts[-1], r2[-1], chip, _pack([wl[n] for n in small], F32, 8),
                         _pack([ml[n] for n in small], F32, 8), _pack([vl[n] for n in small], F32, 8),
                         name='adamw_small_sharded')
    for kind, packed in zip(kinds, outs):
        for n, a in zip(small, _unpack(packed, [wl[n].shape for n in small])):
            res[kind, n] = a

    rep = list(REPLICATED)
    rshapes = [wl[n].shape for n in rep]
    gall, = _all_gather([_pack([grads[n] for n in rep], F32, 8)], name='gather_replicated_grads')
    outs = _adam_replicated(gall, _pack([wl[n] for n in rep], F32, 8), _pack([ml[n] for n in rep], F32, 8),
                            _pack([vl[n] for n in rep], F32, 8), name='adamw_replicated')
    for kind, packed in zip(('grad', 'delta', 'new_m', 'new_v'), outs):
        for n, a in zip(rep, _unpack(packed, rshapes)):
            res[kind, n] = a

    return (loss, grad_x, *[res['grad', n] for n in WEIGHTS], *[res['delta', n] for n in WEIGHTS],
            *[res['new_m', n] for n in WEIGHTS], *[res['new_v', n] for n in WEIGHTS])
```

```python
import functools
import math

import jax
import jax.numpy as jnp
from jax import lax
from jax.experimental import pallas as pl
from jax.experimental.pallas import tpu as pltpu

F32 = jnp.float32
BF16 = jnp.bfloat16
MESH = pl.DeviceIdType.MESH

N_DEV = 8
LANES = 128
MM_TILES = (1024, 1408, 896, 512, 384, 256, 128)
MM_VMEM_BUDGET = 30 * 1024 * 1024
VMEM_LIMIT = 48 * 1024 * 1024
LN_EPS = 1e-5
RMS_EPS = 1e-6
SSD_HEAD_DIM = 64
SSD_N_GROUPS = 8
SSD_D_STATE = 128
SSD_CHUNK = 128
MLA_NOPE = 128
MLA_ROPE = 64
MLA_V = 128
ROPE_THETA = 10000.0
ADAM_LR = 0.001
ADAM_B1 = 0.9
ADAM_B2 = 0.999
ADAM_EPS = 1e-08
ADAM_WD = 0.01
ADAM_STEP = 10
NEG_BIG = -1e30

SHARDED = (('ssd_in_proj', 2), ('ssd_conv_w', 2), ('ssd_conv_b', 1), ('ssd_norm_g', 1), ('ssd_out_proj', 1),
           ('kv_down_proj', 0), ('kv_up_k', 1), ('kv_up_v', 1), ('q_down_proj', 1), ('q_up_proj', 2),
           ('attn_out_proj', 1), ('ffn_up', 2), ('ffn_conv_w', 2), ('ffn_down', 1))
SHARDED_F32 = ('ssd_conv_w', 'ssd_conv_b', 'ssd_norm_g', 'ffn_conv_w')
REPLICATED = ('ssd_dt_bias', 'ssd_A_log', 'ssd_D', 'kv_norm_g', 'q_norm_g', 'ffn_conv_b',
              'ln_mix_g', 'ln_mix_b', 'ln_ffn_g', 'ln_ffn_b')
WEIGHTS = ('ssd_in_proj', 'ssd_conv_w', 'ssd_conv_b', 'ssd_dt_bias', 'ssd_A_log', 'ssd_D', 'ssd_norm_g',
           'ssd_out_proj', 'kv_down_proj', 'kv_norm_g', 'kv_up_k', 'kv_up_v', 'q_down_proj', 'q_norm_g',
           'q_up_proj', 'attn_out_proj', 'ffn_up', 'ffn_conv_w', 'ffn_conv_b', 'ffn_down', 'ln_mix_g',
           'ln_mix_b', 'ln_ffn_g', 'ln_ffn_b')


def _cparams():
    return pltpu.CompilerParams(vmem_limit_bytes=VMEM_LIMIT)


def _tile(n, cands):
    for c in cands:
        if n % c == 0:
            return c
    return n


def _sigmoid(x):
    return 1.0 / (1.0 + jnp.exp(-x))


def _iota(shape, axis):
    return lax.broadcasted_iota(jnp.int32, shape, axis)


def _mm(a, b, *, ta=False, tb=False, add=None, add_scale=1.0, out_dtype=F32, name):
    if ta:
        K, M = a.shape
    else:
        M, K = a.shape
    if tb:
        N, K2 = b.shape
    else:
        K2, N = b.shape
    assert K == K2, (a.shape, b.shape, ta, tb)
    tm = _tile(M, MM_TILES)
    tn = _tile(N, MM_TILES)
    tk = K if K <= MM_TILES[0] else _tile(K, MM_TILES)
    nk = K // tk

    def vmem_estimate(tm_, tn_):
        ins = 2 * (tm_ * tk * a.dtype.itemsize + tk * tn_ * b.dtype.itemsize)
        outs = tm_ * tn_ * (2 * jnp.dtype(out_dtype).itemsize + 4 + (4 if nk > 1 else 0))
        return ins + outs + (2 * tm_ * tn_ * add.dtype.itemsize if add is not None else 0)

    while vmem_estimate(tm, tn) > MM_VMEM_BUDGET:
        can_m, can_n = tm % (2 * LANES) == 0, tn % (2 * LANES) == 0
        if can_m and (tm >= tn or not can_n):
            tm //= 2
        elif can_n:
            tn //= 2
        else:
            break

    def body(a_ref, b_ref, *rest):
        add_ref = rest[0] if add is not None else None
        o_ref = rest[1] if add is not None else rest[0]
        acc = rest[-1] if nk > 1 else None
        k = pl.program_id(2)

        if ta:
            av = a_ref[...].astype(BF16).T
        else:
            av = a_ref[...].astype(BF16)
        bv = b_ref[...].astype(BF16)
        dn = (((1,), (1 if tb else 0,)), ((), ()))
        part = lax.dot_general(av, bv, dn, preferred_element_type=F32)

        def finish(r):
            if add is not None:
                r = r + add_scale * add_ref[...].astype(F32)
            o_ref[...] = r.astype(out_dtype)

        if nk == 1:
            finish(part)
        else:
            @pl.when(k == 0)
            def _():
                acc[...] = part

            @pl.when(k > 0)
            def _():
                acc[...] += part

            @pl.when(k == nk - 1)
            def _():
                finish(acc[...])

    a_spec = (pl.BlockSpec((tk, tm), lambda i, j, k: (k, i)) if ta
              else pl.BlockSpec((tm, tk), lambda i, j, k: (i, k)))
    b_spec = (pl.BlockSpec((tn, tk), lambda i, j, k: (j, k)) if tb
              else pl.BlockSpec((tk, tn), lambda i, j, k: (k, j)))
    o_spec = pl.BlockSpec((tm, tn), lambda i, j, k: (i, j))
    in_specs = [a_spec, b_spec]
    args = [a, b]
    if add is not None:
        in_specs.append(o_spec)
        args.append(add)
    return pl.pallas_call(
        body, name=name, grid=(M // tm, N // tn, nk),
        in_specs=in_specs, out_specs=o_spec,
        out_shape=jax.ShapeDtypeStruct((M, N), out_dtype),
        scratch_shapes=[pltpu.VMEM((tm, tn), F32)] if nk > 1 else [],
        compiler_params=_cparams(),
    )(*args)


def _ln_fwd(h, mix, g, b, alpha, name):
    T, D = h.shape
    tm = _tile(T, (256, 128))

    def body(h_ref, m_ref, g_ref, b_ref, y_ref, s_ref, yb_ref):
        s = alpha * h_ref[...] + m_ref[...]
        mu = jnp.mean(s, axis=1, keepdims=True)
        xc = s - mu
        var = jnp.mean(xc * xc, axis=1, keepdims=True)
        y = xc * lax.rsqrt(var + LN_EPS) * g_ref[...] + b_ref[...]
        y_ref[...] = y
        s_ref[...] = s
        yb_ref[...] = y.astype(BF16)

    row = pl.BlockSpec((tm, D), lambda i: (i, 0))
    vec = pl.BlockSpec((1, D), lambda i: (0, 0))
    return pl.pallas_call(
        body, name=name, grid=(T // tm,), in_specs=[row, row, vec, vec], out_specs=[row, row, row],
        out_shape=[jax.ShapeDtypeStruct((T, D), F32)] * 2 + [jax.ShapeDtypeStruct((T, D), BF16)],
        compiler_params=_cparams(),
    )(h, mix, g, b)


def _ln_bwd(dy, s, g, name):
    T, D = s.shape
    tm = _tile(T, (256, 128))

    def body(dy_ref, s_ref, g_ref, ds_ref, dsb_ref, dg_ref, db_ref):
        @pl.when(pl.program_id(0) == 0)
        def _():
            dg_ref[...] = jnp.zeros_like(dg_ref)
            db_ref[...] = jnp.zeros_like(db_ref)

        sv = s_ref[...]
        dyv = dy_ref[...]
        mu = jnp.mean(sv, axis=1, keepdims=True)
        xc = sv - mu
        rstd = lax.rsqrt(jnp.mean(xc * xc, axis=1, keepdims=True) + LN_EPS)
        xhat = xc * rstd
        dxh = dyv * g_ref[...]
        m1 = jnp.mean(dxh, axis=1, keepdims=True)
        m2 = jnp.mean(dxh * xhat, axis=1, keepdims=True)
        ds = rstd * (dxh - m1 - xhat * m2)
        ds_ref[...] = ds
        dsb_ref[...] = ds.astype(BF16)
        dg_ref[...] += jnp.sum(dyv * xhat, axis=0, keepdims=True)
        db_ref[...] += jnp.sum(dyv, axis=0, keepdims=True)

    row = pl.BlockSpec((tm, D), lambda i: (i, 0))
    vec = pl.BlockSpec((1, D), lambda i: (0, 0))
    return pl.pallas_call(
        body, name=name, grid=(T // tm,), in_specs=[row, row, vec], out_specs=[row, row, vec, vec],
        out_shape=[jax.ShapeDtypeStruct((T, D), F32), jax.ShapeDtypeStruct((T, D), BF16),
                   jax.ShapeDtypeStruct((1, D), F32), jax.ShapeDtypeStruct((1, D), F32)],
        compiler_params=_cparams(),
    )(dy, s, g)


def _loss_head(y, target, name):
    T, D = y.shape
    tm = _tile(T, (256, 128))

    def body(y_ref, t_ref, sq_ref, dy_ref):
        @pl.when(pl.program_id(0) == 0)
        def _():
            sq_ref[...] = jnp.zeros_like(sq_ref)

        e = y_ref[...] - t_ref[...]
        sq_ref[...] += jnp.sum(e * e, axis=0, keepdims=True)
        dy_ref[...] = e * (1.0 / D)

    row = pl.BlockSpec((tm, D), lambda i: (i, 0))
    vec = pl.BlockSpec((1, D), lambda i: (0, 0))
    return pl.pallas_call(
        body, name=name, grid=(T // tm,), in_specs=[row, row], out_specs=[vec, row],
        out_shape=[jax.ShapeDtypeStruct((1, D), F32), jax.ShapeDtypeStruct((T, D), F32)],
        compiler_params=_cparams(),
    )(y, target)


def _shift_down(x, j):
    if j == 0:
        return x
    return jnp.where(_iota(x.shape, 0) >= j, pltpu.roll(x, j, 0), 0.0)


def _shift_up(x, j):
    if j == 0:
        return x
    n = x.shape[0]
    return jnp.where(_iota(x.shape, 0) < n - j, pltpu.roll(x, n - j, 0), 0.0)


def _conv_val(x, w_ref, b_ref, cs=slice(None)):
    width = w_ref.shape[0]
    y = b_ref[:, cs] + w_ref[width - 1:width, cs] * x
    for j in range(1, width):
        y = y + w_ref[width - 1 - j:width - j, cs] * _shift_down(x, j)
    return y


def _conv_bwd_val(x, du, w_ref, dw_ref, db_ref, cs=slice(None)):
    width = w_ref.shape[0]
    dx = w_ref[width - 1:width, cs] * du
    for j in range(1, width):
        dx = dx + w_ref[width - 1 - j:width - j, cs] * _shift_up(du, j)
    for j in range(width):
        dw_ref[width - 1 - j:width - j, cs] += jnp.sum(du * _shift_down(x, j), axis=0, keepdims=True)
    db_ref[:, cs] += jnp.sum(du, axis=0, keepdims=True)
    return dx


GATE = slice(0, LANES)
VALUE = slice(LANES, 2 * LANES)


def _interleave(a, H):
    lead = a.shape[:-1]
    return jnp.swapaxes(a.reshape(lead + (2, H // LANES, LANES)), -3, -2).reshape(lead + (2 * H,))


def _deinterleave(a, H):
    lead = a.shape[:-1]
    return jnp.swapaxes(a.reshape(lead + (H // LANES, 2, LANES)), -3, -2).reshape(lead + (2 * H,))


def _ffn_act_fwd(u, cw, cb, S, name):
    T, H2 = u.shape
    H = H2 // 2
    width = cw.shape[0]

    def body(u_ref, w_ref, b_ref, a_ref):
        g = _conv_val(u_ref[:, GATE], w_ref, b_ref, GATE)
        v = _conv_val(u_ref[:, VALUE], w_ref, b_ref, VALUE)
        a_ref[...] = (g * _sigmoid(g) * v).astype(BF16)

    ublk = pl.BlockSpec((S, 2 * LANES), lambda j, b: (b, j))
    ablk = pl.BlockSpec((S, LANES), lambda j, b: (b, j))
    wblk = pl.BlockSpec((width, 2 * LANES), lambda j, b: (0, j))
    bblk = pl.BlockSpec((1, 2 * LANES), lambda j, b: (0, j))
    return pl.pallas_call(
        body, name=name, grid=(H // LANES, T // S), in_specs=[ublk, wblk, bblk], out_specs=ablk,
        out_shape=jax.ShapeDtypeStruct((T, H), BF16), compiler_params=_cparams(),
    )(u, cw, cb)


def _ffn_act_bwd(u, da, cw, cb, S, name):
    T, H2 = u.shape
    H = H2 // 2
    width = cw.shape[0]

    def body(u_ref, da_ref, w_ref, b_ref, du_ref, dw_ref, db_ref):
        @pl.when(pl.program_id(1) == 0)
        def _():
            dw_ref[...] = jnp.zeros_like(dw_ref)
            db_ref[...] = jnp.zeros_like(db_ref)

        xg = u_ref[:, GATE]
        xv = u_ref[:, VALUE]
        g = _conv_val(xg, w_ref, b_ref, GATE)
        v = _conv_val(xv, w_ref, b_ref, VALUE)
        dav = da_ref[...].astype(F32)
        sg = _sigmoid(g)
        dg = dav * v * sg * (1.0 + g * (1.0 - sg))
        dv = dav * g * sg
        du_ref[:, GATE] = _conv_bwd_val(xg, dg, w_ref, dw_ref, db_ref, GATE).astype(BF16)
        du_ref[:, VALUE] = _conv_bwd_val(xv, dv, w_ref, dw_ref, db_ref, VALUE).astype(BF16)

    ublk = pl.BlockSpec((S, 2 * LANES), lambda j, b: (b, j))
    ablk = pl.BlockSpec((S, LANES), lambda j, b: (b, j))
    wblk = pl.BlockSpec((width, 2 * LANES), lambda j, b: (0, j))
    bblk = pl.BlockSpec((1, 2 * LANES), lambda j, b: (0, j))
    return pl.pallas_call(
        body, name=name, grid=(H // LANES, T // S), in_specs=[ublk, ablk, wblk, bblk],
        out_specs=[ublk, wblk, bblk],
        out_shape=[jax.ShapeDtypeStruct((T, H2), BF16), jax.ShapeDtypeStruct((width, H2), F32),
                   jax.ShapeDtypeStruct((1, H2), F32)],
        compiler_params=_cparams(),
    )(u, da, cw, cb)


def _ssd_conv_fwd(x, cw, cb, S, name):
    T, C = x.shape
    Cb = _tile(C, (256, 128))
    width = cw.shape[0]

    def body(x_ref, w_ref, b_ref, y_ref):
        u = _conv_val(x_ref[...], w_ref, b_ref)
        y_ref[...] = u * _sigmoid(u)

    blk = pl.BlockSpec((S, Cb), lambda j, b: (b, j))
    wblk = pl.BlockSpec((width, Cb), lambda j, b: (0, j))
    bblk = pl.BlockSpec((1, Cb), lambda j, b: (0, j))
    return pl.pallas_call(
        body, name=name, grid=(C // Cb, T // S), in_specs=[blk, wblk, bblk], out_specs=blk,
        out_shape=jax.ShapeDtypeStruct((T, C), F32), compiler_params=_cparams(),
    )(x, cw, cb)


def _ssd_conv_bwd(x, dy, cw, cb, S, name):
    T, C = x.shape
    Cb = _tile(C, (256, 128))
    width = cw.shape[0]

    def body(x_ref, dy_ref, w_ref, b_ref, dx_ref, dw_ref, db_ref):
        @pl.when(pl.program_id(1) == 0)
        def _():
            dw_ref[...] = jnp.zeros_like(dw_ref)
            db_ref[...] = jnp.zeros_like(db_ref)

        xv = x_ref[...]
        u = _conv_val(xv, w_ref, b_ref)
        su = _sigmoid(u)
        du = dy_ref[...] * su * (1.0 + u * (1.0 - su))
        dx_ref[...] = _conv_bwd_val(xv, du, w_ref, dw_ref, db_ref).astype(BF16)

    blk = pl.BlockSpec((S, Cb), lambda j, b: (b, j))
    wblk = pl.BlockSpec((width, Cb), lambda j, b: (0, j))
    bblk = pl.BlockSpec((1, Cb), lambda j, b: (0, j))
    return pl.pallas_call(
        body, name=name, grid=(C // Cb, T // S), in_specs=[blk, blk, wblk, bblk],
        out_specs=[blk, wblk, bblk],
        out_shape=[jax.ShapeDtypeStruct((T, C), BF16), jax.ShapeDtypeStruct((width, C), F32),
                   jax.ShapeDtypeStruct((1, C), F32)],
        compiler_params=_cparams(),
    )(x, dy, cw, cb)


def _softplus(x):
    e = jnp.exp(-jnp.abs(x))
    return jnp.maximum(x, 0.0) + jnp.where(e < 1e-4, e * (1.0 - 0.5 * e), jnp.log(1.0 + e))


def _cumsum_rows(x):
    n = x.shape[0]
    row = _iota(x.shape, 0)
    k = 1
    while k < n:
        x = x + jnp.where(row >= k, pltpu.roll(x, k, 0), 0.0)
        k *= 2
    return x


def _rev_cumsum_rows(x):
    n = x.shape[0]
    row = _iota(x.shape, 0)
    k = 1
    while k < n:
        x = x + jnp.where(row < n - k, pltpu.roll(x, n - k, 0), 0.0)
        k *= 2
    return x


def _col(x, lane_ids, h):
    return jnp.sum(jnp.where(lane_ids == h, x, 0.0), axis=1, keepdims=True)


def _bdot(a, b, dims):
    return lax.dot_general(a.astype(BF16), b.astype(BF16), (dims, ((), ())), preferred_element_type=F32)


NN = ((1,), (0,))
NT = ((1,), (1,))


def _scan_specs(nc, d_inner, hpg):
    L = SSD_CHUNK
    G = SSD_N_GROUPS
    gw = hpg * SSD_HEAD_DIM
    nb = d_inner // LANES
    return dict(
        xs=pl.BlockSpec((L, gw), lambda b, g, c: (b * nc + c, g)),
        B=pl.BlockSpec((L, LANES), lambda b, g, c: (b * nc + c, nb + g)),
        C=pl.BlockSpec((L, LANES), lambda b, g, c: (b * nc + c, nb + G + g)),
        dt=pl.BlockSpec((L, LANES), lambda b, g, c: (b * nc + c, 0)),
        vec=pl.BlockSpec((1, LANES), lambda b, g, c: (0, 0)),
        gvec=pl.BlockSpec((1, gw), lambda b, g, c: (0, g)),
        grp=pl.BlockSpec((L, LANES), lambda b, g, c: (b * nc + c, g)),
        st=pl.BlockSpec((1, hpg // 2, SSD_D_STATE, LANES), lambda b, g, c: ((b * G + g) * nc + c, 0, 0, 0)),
    )


def _scan_fwd(xbc, dt_pre, dt_bias, a_log, S, d_inner, name):
    T = xbc.shape[0]
    Bn = T // S
    L = SSD_CHUNK
    G = SSD_N_GROUPS
    nc = S // L
    hpg = d_inner // SSD_HEAD_DIM // G
    pairs = hpg // 2
    gw = hpg * SSD_HEAD_DIM
    sp = _scan_specs(nc, d_inner, hpg)

    def body(xs_ref, b_ref, c_ref, dtp_ref, bias_ref, alog_ref, y_ref, st_ref, state):
        g = pl.program_id(1)

        @pl.when(pl.program_id(2) == 0)
        def _():
            state[...] = jnp.zeros_like(state)

        dt = _softplus(dtp_ref[...] + bias_ref[...])
        cum = _cumsum_rows(dt * (-jnp.exp(alog_ref[...])))
        Bm = b_ref[...]
        Cm = c_ref[...]
        Gm = _bdot(Cm, Bm, NT)
        tril = _iota((L, L), 1) <= _iota((L, L), 0)
        lane = _iota((L, LANES), 1)
        lane1 = _iota((1, LANES), 1)
        last = _iota((L, 1), 0) == L - 1
        BmT = Bm.T
        for p in range(pairs):
            S_in = state[p]
            st_ref[0, p] = S_in
            x_pair = xs_ref[:, p * LANES:(p + 1) * LANES]
            y_pair = jnp.zeros((L, LANES), F32)
            z_pair = jnp.zeros((L, LANES), F32)
            e_pair = jnp.zeros((L, LANES), F32)
            el_pair = jnp.zeros((1, LANES), F32)
            for k in range(2):
                h = g * hpg + 2 * p + k
                cum_col = _col(cum, lane, h)
                dt_col = _col(dt, lane, h)
                cb = jnp.broadcast_to(cum_col, (L, L))
                decay = jnp.exp(jnp.where(tril, cb - cb.T, NEG_BIG))
                mk = (lane >= SSD_HEAD_DIM) if k else (lane < SSD_HEAD_DIM)
                xd = jnp.where(mk, x_pair * dt_col, 0.0)
                y_pair = y_pair + _bdot(Gm * decay, xd, NN)
                cum_l = jnp.sum(jnp.where(last, cum_col, 0.0), axis=0, keepdims=True)
                e_pair = jnp.where(mk, jnp.exp(cum_col), e_pair)
                z_pair = z_pair + xd * jnp.exp(cum_l - cum_col)
                el_pair = jnp.where((lane1 >= SSD_HEAD_DIM) if k else (lane1 < SSD_HEAD_DIM), jnp.exp(cum_l), el_pair)
            y_pair = y_pair + e_pair * _bdot(Cm, S_in, NN)
            state[p] = S_in * el_pair + _bdot(BmT, z_pair, NN)
            y_ref[:, p * LANES:(p + 1) * LANES] = y_pair

    return pl.pallas_call(
        body, name=name, grid=(Bn, G, nc),
        in_specs=[sp['xs'], sp['B'], sp['C'], sp['dt'], sp['vec'], sp['vec']],
        out_specs=[sp['xs'], sp['st']],
        out_shape=[jax.ShapeDtypeStruct((T, d_inner), F32),
                   jax.ShapeDtypeStruct((Bn * G * nc, pairs, SSD_D_STATE, LANES), F32)],
        scratch_shapes=[pltpu.VMEM((pairs, SSD_D_STATE, LANES), F32)],
        compiler_params=_cparams(),
    )(xbc, xbc, xbc, dt_pre, dt_bias, a_log)


def _scan_bwd(xbc, dt_pre, dt_bias, a_log, states, dy, d_lane, S, d_inner, name):
    T = xbc.shape[0]
    Bn = T // S
    L = SSD_CHUNK
    G = SSD_N_GROUPS
    nc = S // L
    hpg = d_inner // SSD_HEAD_DIM // G
    pairs = hpg // 2
    sp = _scan_specs(nc, d_inner, hpg)

    def rev(spec):
        im = spec.index_map
        return pl.BlockSpec(spec.block_shape, lambda b, g, c: im(b, g, nc - 1 - c))

    def body(xs_ref, b_ref, c_ref, dtp_ref, bias_ref, alog_ref, st_ref, dy_ref, dl_ref,
             dxs_ref, db_ref, dc_ref, ddt_ref, da_ref, dstate):
        g = pl.program_id(1)

        @pl.when(pl.program_id(2) == 0)
        def _():
            dstate[...] = jnp.zeros_like(dstate)

        dt = _softplus(dtp_ref[...] + bias_ref[...])
        cum = _cumsum_rows(dt * (-jnp.exp(alog_ref[...])))
        Bm = b_ref[...]
        Cm = c_ref[...]
        Gm = _bdot(Cm, Bm, NT)
        tril = _iota((L, L), 1) <= _iota((L, L), 0)
        lane = _iota((L, LANES), 1)
        lane1 = _iota((1, LANES), 1)
        last = _iota((L, 1), 0) == L - 1
        CmT = Cm.T
        dG = jnp.zeros((L, L), F32)
        dB = jnp.zeros((L, LANES), F32)
        dC = jnp.zeros((L, LANES), F32)
        dcum = jnp.zeros((L, LANES), F32)
        ddt = jnp.zeros((L, LANES), F32)
        for p in range(pairs):
            S_in = st_ref[0, p]
            dS = dstate[p]
            x_pair = xs_ref[:, p * LANES:(p + 1) * LANES]
            dy_pair = dy_ref[:, p * LANES:(p + 1) * LANES]
            Q = _bdot(Cm, S_in, NN)
            dZ = _bdot(Bm, dS, NN)
            prod = dS * S_in
            e_pair = jnp.zeros((L, LANES), F32)
            z_pair = jnp.zeros((L, LANES), F32)
            el_pair = jnp.zeros((1, LANES), F32)
            dx_pair = dl_ref[:, p * LANES:(p + 1) * LANES] * dy_pair
            for k in range(2):
                h = g * hpg + 2 * p + k
                cum_col = _col(cum, lane, h)
                dt_col = _col(dt, lane, h)
                cb = jnp.broadcast_to(cum_col, (L, L))
                decay = jnp.exp(jnp.where(tril, cb - cb.T, NEG_BIG))
                mk = (lane >= SSD_HEAD_DIM) if k else (lane < SSD_HEAD_DIM)
                mk1 = (lane1 >= SSD_HEAD_DIM) if k else (lane1 < SSD_HEAD_DIM)
                xd = jnp.where(mk, x_pair * dt_col, 0.0)
                dy_k = jnp.where(mk, dy_pair, 0.0)
                Wm = Gm * decay
                dWm = jnp.where(tril, _bdot(dy_k, xd, NT), 0.0)
                dxd = _bdot(Wm.T, dy_k, NN)
                dseg = dWm * Wm
                dG = dG + dWm * decay
                dcum_col = (jnp.sum(dseg, axis=1, keepdims=True)
                            - jnp.sum(dseg.T, axis=1, keepdims=True))
                cum_l = jnp.sum(jnp.where(last, cum_col, 0.0), axis=0, keepdims=True)
                e_col = jnp.exp(cum_col)
                w_col = jnp.exp(cum_l - cum_col)
                el = jnp.exp(cum_l)
                dcum_col = dcum_col + jnp.sum(dy_k * Q, axis=1, keepdims=True) * e_col
                de_e = jnp.sum(dZ * xd, axis=1, keepdims=True) * w_col
                dcum_col = dcum_col - de_e
                dcum_l = (jnp.sum(de_e, axis=0, keepdims=True)
                          + el * jnp.sum(jnp.sum(jnp.where(mk, prod, 0.0), axis=1, keepdims=True),
                                         axis=0, keepdims=True))
                dcum_col = dcum_col + jnp.where(last, dcum_l, 0.0)
                dxd = dxd + jnp.where(mk, dZ * w_col, 0.0)
                dx_pair = dx_pair + dxd * dt_col
                ddt = jnp.where(lane == h, jnp.sum(dxd * x_pair, axis=1, keepdims=True), ddt)
                dcum = jnp.where(lane == h, dcum_col, dcum)
                e_pair = jnp.where(mk, e_col, e_pair)
                z_pair = z_pair + xd * w_col
                el_pair = jnp.where(mk1, el, el_pair)
            dQ = e_pair * dy_pair
            dC = dC + _bdot(dQ, S_in, NT)
            dB = dB + _bdot(z_pair, dS, NT)
            dstate[p] = dS * el_pair + _bdot(CmT, dQ, NN)
            dxs_ref[:, p * LANES:(p + 1) * LANES] = dx_pair
        dc_ref[...] = dC + _bdot(dG, Bm, NN)
        db_ref[...] = dB + _bdot(dG.T, Cm, NN)
        ddt_ref[...] = ddt
        da_ref[...] = _rev_cumsum_rows(dcum)

    grp = jax.ShapeDtypeStruct((T, G * LANES), F32)
    return pl.pallas_call(
        body, name=name, grid=(Bn, G, nc),
        in_specs=[rev(sp['xs']), rev(sp['B']), rev(sp['C']), rev(sp['dt']), sp['vec'], sp['vec'],
                  rev(sp['st']), rev(sp['xs']), sp['gvec']],
        out_specs=[rev(sp['xs']), rev(sp['grp']), rev(sp['grp']), rev(sp['grp']), rev(sp['grp'])],
        out_shape=[jax.ShapeDtypeStruct((T, d_inner), F32), grp, grp, grp, grp],
        scratch_shapes=[pltpu.VMEM((pairs, SSD_D_STATE, LANES), F32)],
        compiler_params=_cparams(),
    )(xbc, xbc, xbc, dt_pre, dt_bias, a_log, states, dy, d_lane)


def _dt_bwd(ddt_g, da_g, dt_pre, dt_bias, a_log, name):
    T = dt_pre.shape[0]
    G = SSD_N_GROUPS
    tm = _tile(T, (512, 256, 128))

    def body(ddt_ref, da_ref, dtp_ref, bias_ref, alog_ref, dx_ref, dbias_ref, dal_ref):
        @pl.when(pl.program_id(0) == 0)
        def _():
            dbias_ref[...] = jnp.zeros_like(dbias_ref)
            dal_ref[...] = jnp.zeros_like(dal_ref)

        ddt = ddt_ref[:, 0:LANES]
        da = da_ref[:, 0:LANES]
        for gi in range(1, G):
            ddt = ddt + ddt_ref[:, gi * LANES:(gi + 1) * LANES]
            da = da + da_ref[:, gi * LANES:(gi + 1) * LANES]
        xv = dtp_ref[...] + bias_ref[...]
        A = -jnp.exp(alog_ref[...])
        dx = (ddt + da * A) * _sigmoid(xv)
        dx_ref[...] = dx.astype(BF16)
        dbias_ref[...] += jnp.sum(dx, axis=0, keepdims=True)
        dal_ref[...] += jnp.sum(da * _softplus(xv), axis=0, keepdims=True) * A

    wide = pl.BlockSpec((tm, G * LANES), lambda i: (i, 0))
    row = pl.BlockSpec((tm, LANES), lambda i: (i, 0))
    vec = pl.BlockSpec((1, LANES), lambda i: (0, 0))
    vsh = jax.ShapeDtypeStruct((1, LANES), F32)
    return pl.pallas_call(
        body, name=name, grid=(T // tm,), in_specs=[wide, wide, row, vec, vec], out_specs=[row, vec, vec],
        out_shape=[jax.ShapeDtypeStruct((T, LANES), BF16), vsh, vsh], compiler_params=_cparams(),
    )(ddt_g, da_g, dt_pre, dt_bias, a_log)


def _gate_fwd(y, xbc, z, d_lane, ng, d_inner, name):
    T = y.shape[0]
    G = SSD_N_GROUPS
    gw = d_inner // G
    tm = _tile(T, (512, 256, 128))

    def body(y_ref, x_ref, z_ref, dl_ref, ng_ref, o_ref):
        zv = z_ref[...]
        y2 = (y_ref[...] + dl_ref[...] * x_ref[...]) * (zv * _sigmoid(zv))
        r = lax.rsqrt(jnp.mean(y2 * y2, axis=1, keepdims=True) + LN_EPS)
        o_ref[...] = (y2 * r * ng_ref[...]).astype(BF16)

    blk = pl.BlockSpec((tm, gw), lambda g, i: (i, g))
    vec = pl.BlockSpec((1, gw), lambda g, i: (0, g))
    return pl.pallas_call(
        body, name=name, grid=(G, T // tm), in_specs=[blk, blk, blk, vec, vec], out_specs=blk,
        out_shape=jax.ShapeDtypeStruct((T, d_inner), BF16), compiler_params=_cparams(),
    )(y, xbc, z, d_lane, ng)


def _gate_bwd(dyn, y, xbc, z, d_lane, ng, d_inner, name):
    T = y.shape[0]
    G = SSD_N_GROUPS
    gw = d_inner // G
    tm = _tile(T, (512, 256, 128))

    def body(dyn_ref, y_ref, x_ref, z_ref, dl_ref, ng_ref, dy_ref, dz_ref, dng_ref, ddl_ref):
        @pl.when(pl.program_id(1) == 0)
        def _():
            dng_ref[...] = jnp.zeros_like(dng_ref)
            ddl_ref[...] = jnp.zeros_like(ddl_ref)

        zv = z_ref[...]
        xv = x_ref[...]
        sz = _sigmoid(zv)
        y1 = y_ref[...] + dl_ref[...] * xv
        y2 = y1 * (zv * sz)
        r = lax.rsqrt(jnp.mean(y2 * y2, axis=1, keepdims=True) + LN_EPS)
        dn = dyn_ref[...].astype(F32)
        dng_ref[...] += jnp.sum(dn * y2 * r, axis=0, keepdims=True)
        do = dn * ng_ref[...]
        dy2 = r * do - y2 * (r * r * r) * jnp.mean(do * y2, axis=1, keepdims=True)
        dz_ref[...] = (dy2 * y1 * sz * (1.0 + zv * (1.0 - sz))).astype(BF16)
        dy1 = dy2 * (zv * sz)
        dy_ref[...] = dy1
        ddl_ref[...] += jnp.sum(dy1 * xv, axis=0, keepdims=True)

    blk = pl.BlockSpec((tm, gw), lambda g, i: (i, g))
    vec = pl.BlockSpec((1, gw), lambda g, i: (0, g))
    vsh = jax.ShapeDtypeStruct((1, d_inner), F32)
    return pl.pallas_call(
        body, name=name, grid=(G, T // tm), in_specs=[blk, blk, blk, blk, vec, vec],
        out_specs=[blk, blk, vec, vec],
        out_shape=[jax.ShapeDtypeStruct((T, d_inner), F32), jax.ShapeDtypeStruct((T, d_inner), BF16), vsh, vsh],
        compiler_params=_cparams(),
    )(dyn, y, xbc, z, d_lane, ng)


def _rms_fwd(x, g, name):
    T, R = x.shape
    tm = _tile(T, (512, 256, 128))

    def body(x_ref, g_ref, y_ref):
        xv = x_ref[...]
        y = xv * lax.rsqrt(jnp.mean(xv * xv, axis=1, keepdims=True) + RMS_EPS) * g_ref[...]
        y_ref[...] = y.astype(BF16)

    row = pl.BlockSpec((tm, R), lambda i: (i, 0))
    vec = pl.BlockSpec((1, R), lambda i: (0, 0))
    return pl.pallas_call(
        body, name=name, grid=(T // tm,), in_specs=[row, vec], out_specs=row,
        out_shape=jax.ShapeDtypeStruct((T, R), BF16), compiler_params=_cparams(),
    )(x, g)


def _rms_bwd(dy, x, g, name):
    T, R = x.shape
    tm = _tile(T, (512, 256, 128))

    def body(dy_ref, x_ref, g_ref, dx_ref, dg_ref):
        @pl.when(pl.program_id(0) == 0)
        def _():
            dg_ref[...] = jnp.zeros_like(dg_ref)

        xv = x_ref[...]
        dyv = dy_ref[...]
        r = lax.rsqrt(jnp.mean(xv * xv, axis=1, keepdims=True) + RMS_EPS)
        dg_ref[...] += jnp.sum(dyv * xv * r, axis=0, keepdims=True)
        do = dyv * g_ref[...]
        dx = r * do - xv * (r * r * r) * jnp.mean(do * xv, axis=1, keepdims=True)
        dx_ref[...] = dx.astype(BF16)

    row = pl.BlockSpec((tm, R), lambda i: (i, 0))
    vec = pl.BlockSpec((1, R), lambda i: (0, 0))
    return pl.pallas_call(
        body, name=name, grid=(T // tm,), in_specs=[row, row, vec], out_specs=[row, vec],
        out_shape=[jax.ShapeDtypeStruct((T, R), BF16), jax.ShapeDtypeStruct((1, R), F32)],
        compiler_params=_cparams(),
    )(dy, x, g)


def _swap_halves(x):
    half = MLA_ROPE // 2
    return jnp.where(_iota(x.shape, 1) < half, pltpu.roll(x, LANES - half, 1), pltpu.roll(x, half, 1))


def _rope(x, cc, ss, *, transpose, sum_heads=False, name):
    T, W = x.shape
    nh = W // LANES
    tm = _tile(T, (512, 256, 128))
    n_out = 1 if sum_heads else nh

    def body(x_ref, cc_ref, ss_ref, o_ref):
        ccv = cc_ref[...]
        ssv = ss_ref[...]
        if sum_heads:
            xv = x_ref[:, 0:LANES]
            for hh in range(1, nh):
                xv = xv + x_ref[:, hh * LANES:(hh + 1) * LANES]
            heads = [xv]
        else:
            heads = [x_ref[:, hh * LANES:(hh + 1) * LANES] for hh in range(nh)]
        for hh, xv in enumerate(heads):
            if transpose:
                r = xv * ccv + _swap_halves(xv * ssv)
            else:
                r = xv * ccv + _swap_halves(xv) * ssv
            r = jnp.where(_iota(r.shape, 1) < MLA_ROPE, r, 0.0)
            o_ref[:, hh * LANES:(hh + 1) * LANES] = r.astype(BF16)

    return pl.pallas_call(
        body, name=name, grid=(T // tm,),
        in_specs=[pl.BlockSpec((tm, W), lambda i: (i, 0)), pl.BlockSpec((tm, LANES), lambda i: (i, 0)),
                  pl.BlockSpec((tm, LANES), lambda i: (i, 0))],
        out_specs=pl.BlockSpec((tm, n_out * LANES), lambda i: (i, 0)),
        out_shape=jax.ShapeDtypeStruct((T, n_out * LANES), BF16), compiler_params=_cparams(),
    )(x, cc, ss)


ATT_BLOCK = 512
ATT_SUB = 256
LOG2E = 1.4426950408889634


def _att_geometry(S):
    tb = _tile(S, (ATT_BLOCK, 256))
    return tb, S // tb, tb // ATT_SUB


def _cat(n_ref, r_ref, rows):
    return jnp.concatenate([n_ref[rows, :], r_ref[rows, :]], axis=1)


def _att_scores(a, b, diag, kv_major, off):
    s = _bdot(a, b, NT)
    if diag:
        q_ax, k_ax = (1, 0) if kv_major else (0, 1)
        s = jnp.where(_iota(s.shape, k_ax) <= _iota(s.shape, q_ax) + off, s, NEG_BIG)
    return s


def _attention_fwd(qn, qr, kn, kr, v, S, scale, name):
    T, W = qn.shape
    nh = W // LANES
    Bn = T // S
    tb, n, C = _att_geometry(S)
    pairs = [(i, j) for i in range(n) for j in range(i + 1)]
    it = jnp.asarray([p[0] for p in pairs], jnp.int32)
    jt = jnp.asarray([p[1] for p in pairs], jnp.int32)
    c2 = scale * LOG2E

    def body(it_ref, jt_ref, qn_ref, qr_ref, kn_ref, kr_ref, v_ref, o_ref, ob_ref, lse_ref, m_sc, l_sc, acc):
        p = pl.program_id(2)
        i = it_ref[p]
        j = jt_ref[p]

        @pl.when(j == 0)
        def _():
            m_sc[...] = jnp.full_like(m_sc, NEG_BIG)
            l_sc[...] = jnp.zeros_like(l_sc)
            acc[...] = jnp.zeros_like(acc)

        def step(diag):
            qc = _cat(qn_ref, qr_ref, pl.ds(0, tb))
            for c in range(C):
                r0 = c * ATT_SUB if diag else 0
                rows = pl.ds(r0, tb - r0)
                ks = pl.ds(c * ATT_SUB, ATT_SUB)
                s = _att_scores(qc[r0:], _cat(kn_ref, kr_ref, ks), diag, False, 0)
                m_old = m_sc[rows, :]
                m_new = jnp.maximum(m_old, jnp.max(s, axis=1, keepdims=True))
                pr = jnp.exp2((s - m_new) * c2)
                corr = jnp.exp2((m_old - m_new) * c2)
                l_sc[rows, :] = corr * l_sc[rows, :] + jnp.sum(pr, axis=1, keepdims=True)
                acc[rows, :] = corr * acc[rows, :] + _bdot(pr, v_ref[ks, :], NN)
                m_sc[rows, :] = m_new

        @pl.when(j < i)
        def _():
            step(False)

        @pl.when(j == i)
        def _():
            step(True)
            ov = acc[...] / l_sc[...]
            o_ref[...] = ov
            ob_ref[...] = ov.astype(BF16)
            lse_ref[...] = jnp.broadcast_to(m_sc[...] * scale + jnp.log(l_sc[...]), (tb, LANES))

    qspec = pl.BlockSpec((tb, LANES), lambda b, h, p, it_, jt_: (b * n + it_[p], h))
    kspec = pl.BlockSpec((tb, LANES), lambda b, h, p, it_, jt_: (b * n + jt_[p], h))
    krspec = pl.BlockSpec((tb, LANES), lambda b, h, p, it_, jt_: (b * n + jt_[p], 0))
    return pl.pallas_call(
        body, name=name,
        grid_spec=pltpu.PrefetchScalarGridSpec(
            num_scalar_prefetch=2, grid=(Bn, nh, len(pairs)),
            in_specs=[qspec, qspec, kspec, krspec, kspec], out_specs=[qspec, qspec, qspec],
            scratch_shapes=[pltpu.VMEM((tb, 1), F32), pltpu.VMEM((tb, 1), F32), pltpu.VMEM((tb, LANES), F32)]),
        out_shape=[jax.ShapeDtypeStruct((T, W), F32), jax.ShapeDtypeStruct((T, W), BF16),
                   jax.ShapeDtypeStruct((T, W), F32)],
        compiler_params=_cparams(),
    )(it, jt, qn, qr, kn, kr, v)


def _attention_dq(qn, qr, kn, kr, v, o, do, lse, S, scale, name):
    T, W = qn.shape
    nh = W // LANES
    Bn = T // S
    tb, n, C = _att_geometry(S)
    pairs = [(i, j) for i in range(n) for j in range(i + 1)]
    it = jnp.asarray([p[0] for p in pairs], jnp.int32)
    jt = jnp.asarray([p[1] for p in pairs], jnp.int32)
    c2 = scale * LOG2E

    def body(it_ref, jt_ref, qn_ref, qr_ref, kn_ref, kr_ref, v_ref, o_ref, do_ref, lse_ref,
             dqn_ref, dqr_ref, acc, di_sc, lse_sc):
        p = pl.program_id(2)
        i = it_ref[p]
        j = jt_ref[p]

        @pl.when(j == 0)
        def _():
            acc[...] = jnp.zeros_like(acc)
            di_sc[...] = jnp.sum(do_ref[...] * o_ref[...], axis=1, keepdims=True)
            lse_sc[...] = jnp.max(lse_ref[...], axis=1, keepdims=True) * LOG2E

        def step(diag):
            qc = _cat(qn_ref, qr_ref, pl.ds(0, tb))
            for c in range(C):
                r0 = c * ATT_SUB if diag else 0
                rows = pl.ds(r0, tb - r0)
                ks = pl.ds(c * ATT_SUB, ATT_SUB)
                kc = _cat(kn_ref, kr_ref, ks)
                s = _att_scores(qc[r0:], kc, diag, False, 0)
                pr = jnp.exp2(s * c2 - lse_sc[rows, :])
                dp = _bdot(do_ref[rows, :], v_ref[ks, :], NT)
                ds = pr * (dp - di_sc[rows, :]) * scale
                acc[rows, :] += _bdot(ds, kc, NN)

        @pl.when(j < i)
        def _():
            step(False)

        @pl.when(j == i)
        def _():
            step(True)
            dqn_ref[...] = acc[:, 0:LANES].astype(BF16)
            dqr_ref[...] = acc[:, LANES:2 * LANES]

    qspec = pl.BlockSpec((tb, LANES), lambda b, h, p, it_, jt_: (b * n + it_[p], h))
    kspec = pl.BlockSpec((tb, LANES), lambda b, h, p, it_, jt_: (b * n + jt_[p], h))
    krspec = pl.BlockSpec((tb, LANES), lambda b, h, p, it_, jt_: (b * n + jt_[p], 0))
    return pl.pallas_call(
        body, name=name,
        grid_spec=pltpu.PrefetchScalarGridSpec(
            num_scalar_prefetch=2, grid=(Bn, nh, len(pairs)),
            in_specs=[qspec, qspec, kspec, krspec, kspec, qspec, qspec, qspec], out_specs=[qspec, qspec],
            scratch_shapes=[pltpu.VMEM((tb, 2 * LANES), F32), pltpu.VMEM((tb, 1), F32), pltpu.VMEM((tb, 1), F32)]),
        out_shape=[jax.ShapeDtypeStruct((T, W), BF16), jax.ShapeDtypeStruct((T, W), F32)],
        compiler_params=_cparams(),
    )(it, jt, qn, qr, kn, kr, v, o, do, lse)


def _attention_dkv(qn, qr, kn, kr, v, o, do, lse, S, scale, name):
    T, W = qn.shape
    nh = W // LANES
    Bn = T // S
    tb, n, C = _att_geometry(S)
    triples = [(j, h, i) for j in range(n) for h in range(nh) for i in range(j, n)]
    jt = jnp.asarray([t[0] for t in triples], jnp.int32)
    ht = jnp.asarray([t[1] for t in triples], jnp.int32)
    it = jnp.asarray([t[2] for t in triples], jnp.int32)
    c2 = scale * LOG2E

    def as_row(col):
        return jnp.broadcast_to(col, (ATT_SUB, LANES)).T[0:1, :]

    def body(jt_ref, ht_ref, it_ref, qn_ref, qr_ref, kn_ref, kr_ref, v_ref, o_ref, do_ref, lse_ref,
             dkn_ref, dv_ref, dkr_ref, akc, av):
        p = pl.program_id(1)
        j = jt_ref[p]
        h = ht_ref[p]
        i = it_ref[p]

        @pl.when(jnp.logical_and(h == 0, i == j))
        def _():
            dkr_ref[...] = jnp.zeros_like(dkr_ref)

        @pl.when(i == j)
        def _():
            akc[...] = jnp.zeros_like(akc)
            av[...] = jnp.zeros_like(av)

        def step(diag):
            kc = _cat(kn_ref, kr_ref, pl.ds(0, tb))
            for c in range(C):
                rows = pl.ds(c * ATT_SUB, ATT_SUB)
                k1 = (c + 1) * ATT_SUB if diag else tb
                ks = pl.ds(0, k1)
                qc = _cat(qn_ref, qr_ref, rows)
                st = _att_scores(kc[:k1], qc, diag, True, c * ATT_SUB)
                dov = do_ref[rows, :]
                lse_row = as_row(jnp.max(lse_ref[rows, :], axis=1, keepdims=True)) * LOG2E
                di_row = as_row(jnp.sum(dov * o_ref[rows, :], axis=1, keepdims=True))
                pt = jnp.exp2(st * c2 - lse_row)
                dst = pt * (_bdot(v_ref[ks, :], dov, NT) - di_row) * scale
                av[ks, :] += _bdot(pt, dov, NN)
                akc[ks, :] += _bdot(dst, qc, NN)

        @pl.when(i > j)
        def _():
            step(False)

        @pl.when(i == j)
        def _():
            step(True)

        @pl.when(i == n - 1)
        def _():
            dkn_ref[...] = akc[:, 0:LANES].astype(BF16)
            dkr_ref[...] += akc[:, LANES:2 * LANES]
            dv_ref[...] = av[...].astype(BF16)

    qspec = pl.BlockSpec((tb, LANES), lambda b, p, jt_, ht_, it_: (b * n + it_[p], ht_[p]))
    kspec = pl.BlockSpec((tb, LANES), lambda b, p, jt_, ht_, it_: (b * n + jt_[p], ht_[p]))
    krspec = pl.BlockSpec((tb, LANES), lambda b, p, jt_, ht_, it_: (b * n + jt_[p], 0))
    return pl.pallas_call(
        body, name=name,
        grid_spec=pltpu.PrefetchScalarGridSpec(
            num_scalar_prefetch=3, grid=(Bn, len(triples)),
            in_specs=[qspec, qspec, kspec, krspec, kspec, qspec, qspec, qspec],
            out_specs=[kspec, kspec, krspec],
            scratch_shapes=[pltpu.VMEM((tb, 2 * LANES), F32), pltpu.VMEM((tb, LANES), F32)]),
        out_shape=[jax.ShapeDtypeStruct((T, W), BF16), jax.ShapeDtypeStruct((T, W), BF16),
                   jax.ShapeDtypeStruct((T, LANES), F32)],
        compiler_params=_cparams(),
    )(jt, ht, it, qn, qr, kn, kr, v, o, do, lse)


def _pad_cols(w, n):
    return jnp.pad(w, ((0, 0), (0, n - w.shape[1])))


def _local_step(x, positions, w, target):
    Bn, S, D = x.shape
    T = Bn * S
    depth = w['ffn_up'].shape[0]
    alpha = (2 * depth) ** 0.25
    d_inner = w['ssd_out_proj'].shape[1]
    n_heads_ssd = d_inner // SSD_HEAD_DIM
    G = SSD_N_GROUPS
    gn = G * SSD_D_STATE
    conv_dim = d_inner + 2 * gn
    hpg = n_heads_ssd // G
    nh = D // MLA_NOPE
    kv_rank = w['kv_up_k'].shape[0]
    q_rank = w['q_down_proj'].shape[2]
    H = w['ffn_down'].shape[1]
    scale = (MLA_NOPE + MLA_ROPE) ** -0.5
    assert S % SSD_CHUNK == 0 and hpg % 2 == 0 and SSD_D_STATE == LANES and n_heads_ssd <= LANES

    X = x.reshape(T, D)
    tgt = target.reshape(T, D)

    inv_freq = 1.0 / (ROPE_THETA ** (jnp.arange(0, MLA_ROPE, 2, dtype=F32) / MLA_ROPE))
    ang = positions.reshape(T).astype(F32)[:, None] * inv_freq
    cos, sin = jnp.cos(ang), jnp.sin(ang)
    zpad = jnp.zeros((T, LANES - MLA_ROPE), F32)
    cc = jnp.concatenate([cos, cos, zpad], axis=1)
    ss = jnp.concatenate([-sin, sin, zpad], axis=1)

    w_in = w['ssd_in_proj'][0]
    w_z = w_in[:, :d_inner]
    w_xbc = w_in[:, d_inner:d_inner + conv_dim]
    w_dt = _pad_cols(w_in[:, d_inner + conv_dim:], LANES)
    w_out = w['ssd_out_proj'][0]
    ssd_cw = w['ssd_conv_w'][0]
    ssd_cb = w['ssd_conv_b']
    dt_bias = _pad_cols(w['ssd_dt_bias'], LANES)
    a_log = _pad_cols(w['ssd_A_log'], LANES)
    d_lane = jnp.repeat(w['ssd_D'], SSD_HEAD_DIM, axis=1)
    ssd_ng = w['ssd_norm_g']
    w_kvc = w['kv_down_proj'][:, :kv_rank]
    w_kvr = _pad_cols(w['kv_down_proj'][:, kv_rank:], LANES)
    kv_g = w['kv_norm_g'].reshape(1, kv_rank)
    w_uk = w['kv_up_k']
    w_uv = w['kv_up_v']
    w_qd = w['q_down_proj'][0]
    q_g = w['q_norm_g']
    qu = w['q_up_proj'][0].reshape(q_rank, nh, MLA_NOPE + MLA_ROPE)
    w_qn = qu[:, :, :MLA_NOPE].reshape(q_rank, nh * LANES)
    w_qr = jnp.pad(qu[:, :, MLA_NOPE:], ((0, 0), (0, 0), (0, LANES - MLA_ROPE))).reshape(q_rank, nh * LANES)
    w_ao = w['attn_out_proj'][0]

    def ffn_parts(i):
        return dict(wu=_interleave(w['ffn_up'][i], H), cw=_interleave(w['ffn_conv_w'][i], H),
                    cb=_interleave(w['ffn_conv_b'][i:i + 1], H), wd=w['ffn_down'][i])

    def ln(name, i):
        return w[name][i:i + 1]

    def ffn_fwd(hb, i):
        f = ffn_parts(i)
        u = _mm(hb, f['wu'], name=f'ffn{i}_up')
        a = _ffn_act_fwd(u, f['cw'], f['cb'], S, name=f'ffn{i}_act')
        return _mm(a, f['wd'], name=f'ffn{i}_down'), (u, a)

    Xb = X.astype(BF16)
    z = _mm(Xb, w_z, name='ssd_in_z')
    xbc_pre = _mm(Xb, w_xbc, name='ssd_in_xbc')
    dt_pre = _mm(Xb, w_dt, name='ssd_in_dt')
    xbc = _ssd_conv_fwd(xbc_pre, ssd_cw, ssd_cb, S, name='ssd_conv')
    y_scan, states = _scan_fwd(xbc, dt_pre, dt_bias, a_log, S, d_inner, name='ssd_scan')
    yn = _gate_fwd(y_scan, xbc, z, d_lane, ssd_ng, d_inner, name='ssd_gate')
    mix0 = _mm(yn, w_out, name='ssd_out')
    h1, s1, h1b = _ln_fwd(X, mix0, ln('ln_mix_g', 0), ln('ln_mix_b', 0), alpha, name='ln_mix0')
    ff0, ffn0_saved = ffn_fwd(h1b, 0)
    h2, s2, h2b = _ln_fwd(h1, ff0, ln('ln_ffn_g', 0), ln('ln_ffn_b', 0), alpha, name='ln_ffn0')

    ckv = _mm(h2b, w_kvc, name='kv_down_c')
    kr_pre = _mm(h2b, w_kvr, name='kv_down_r')
    ckvn = _rms_fwd(ckv, kv_g, name='kv_norm')
    kr = _rope(kr_pre, cc, ss, transpose=False, name='k_rope')
    kn = _mm(ckvn, w_uk, out_dtype=BF16, name='kv_up_k')
    v = _mm(ckvn, w_uv, out_dtype=BF16, name='kv_up_v')
    cq_pre = _mm(h2b, w_qd, name='q_down')
    cq = _rms_fwd(cq_pre, q_g, name='q_norm')
    qn = _mm(cq, w_qn, out_dtype=BF16, name='q_up_n')
    qr_pre = _mm(cq, w_qr, name='q_up_r')
    qr = _rope(qr_pre, cc, ss, transpose=False, name='q_rope')
    o, ob, lse = _attention_fwd(qn, qr, kn, kr, v, S, scale, name='attn_fwd')
    mix1 = _mm(ob, w_ao, name='attn_out')
    h3, s3, h3b = _ln_fwd(h2, mix1, ln('ln_mix_g', 1), ln('ln_mix_b', 1), alpha, name='ln_mix1')
    ff1, ffn1_saved = ffn_fwd(h3b, 1)
    h4, s4, _ = _ln_fwd(h3, ff1, ln('ln_ffn_g', 1), ln('ln_ffn_b', 1), alpha, name='ln_ffn1')
    sq, dh4 = _loss_head(h4, tgt, name='loss_head')

    g = {}

    def ffn_bwd(ds, dsb, hb_in, saved, i):
        f = ffn_parts(i)
        u, a = saved
        d_wd = _mm(a, dsb, ta=True, name=f'ffn{i}_down_dw')
        da = _mm(dsb, f['wd'], tb=True, out_dtype=BF16, name=f'ffn{i}_down_dx')
        du, dcw, dcb = _ffn_act_bwd(u, da, f['cw'], f['cb'], S, name=f'ffn{i}_act_bwd')
        d_wu = _deinterleave(_mm(hb_in, du, ta=True, name=f'ffn{i}_up_dw'), H)
        dh = _mm(du, f['wu'], tb=True, add=ds, add_scale=alpha, name=f'ffn{i}_up_dx')
        return dh, d_wd, d_wu, _deinterleave(dcw, H), _deinterleave(dcb, H)

    ds4, ds4b, dg_f1, db_f1 = _ln_bwd(dh4, s4, ln('ln_ffn_g', 1), name='ln_ffn1_bwd')
    dh3, d_wd1, d_wu1, d_fcw1, d_fcb1 = ffn_bwd(ds4, ds4b, h3b, ffn1_saved, 1)
    ds3, ds3b, dg_m1, db_m1 = _ln_bwd(dh3, s3, ln('ln_mix_g', 1), name='ln_mix1_bwd')

    g['attn_out_proj'] = _mm(ob, ds3b, ta=True, name='attn_out_dw')[None]
    do = _mm(ds3b, w_ao, tb=True, name='attn_out_dx')
    dqn, dqr = _attention_dq(qn, qr, kn, kr, v, o, do, lse, S, scale, name='attn_bwd_dq')
    dkn, dv, dkr = _attention_dkv(qn, qr, kn, kr, v, o, do, lse, S, scale, name='attn_bwd_dkv')
    dqr_pre = _rope(dqr, cc, ss, transpose=True, name='q_rope_bwd')
    dkr_pre = _rope(dkr, cc, ss, transpose=True, name='k_rope_bwd')

    d_wqn = _mm(cq, dqn, ta=True, name='q_up_n_dw').reshape(q_rank, nh, LANES)
    d_wqr = _mm(cq, dqr_pre, ta=True, name='q_up_r_dw').reshape(q_rank, nh, LANES)[:, :, :MLA_ROPE]
    g['q_up_proj'] = jnp.concatenate([d_wqn, d_wqr], axis=2).reshape(1, q_rank, nh * (MLA_NOPE + MLA_ROPE))
    dcq = _mm(dqn, w_qn, tb=True, name='q_up_n_dx')
    dcq = _mm(dqr_pre, w_qr, tb=True, add=dcq, name='q_up_r_dx')
    dcq_pre, d_qg = _rms_bwd(dcq, cq_pre, q_g, name='q_norm_bwd')
    g['q_norm_g'] = d_qg
    g['q_down_proj'] = _mm(h2b, dcq_pre, ta=True, name='q_down_dw')[None]

    g['kv_up_k'] = _mm(ckvn, dkn, ta=True, name='kv_up_k_dw')
    g['kv_up_v'] = _mm(ckvn, dv, ta=True, name='kv_up_v_dw')
    dckvn = _mm(dkn, w_uk, tb=True, name='kv_up_k_dx')
    dckvn = _mm(dv, w_uv, tb=True, add=dckvn, name='kv_up_v_dx')
    dckv, d_kvg = _rms_bwd(dckvn, ckv, kv_g, name='kv_norm_bwd')
    g['kv_norm_g'] = d_kvg.reshape(kv_rank)
    g['kv_down_proj'] = jnp.concatenate(
        [_mm(h2b, dckv, ta=True, name='kv_down_c_dw'),
         _mm(h2b, dkr_pre, ta=True, name='kv_down_r_dw')[:, :MLA_ROPE]], axis=1)

    dh2 = _mm(dcq_pre, w_qd, tb=True, add=ds3, add_scale=alpha, name='q_down_dx')
    dh2 = _mm(dckv, w_kvc, tb=True, add=dh2, name='kv_down_c_dx')
    dh2 = _mm(dkr_pre, w_kvr, tb=True, add=dh2, name='kv_down_r_dx')

    ds2, ds2b, dg_f0, db_f0 = _ln_bwd(dh2, s2, ln('ln_ffn_g', 0), name='ln_ffn0_bwd')
    dh1, d_wd0, d_wu0, d_fcw0, d_fcb0 = ffn_bwd(ds2, ds2b, h1b, ffn0_saved, 0)
    ds1, ds1b, dg_m0, db_m0 = _ln_bwd(dh1, s1, ln('ln_mix_g', 0), name='ln_mix0_bwd')

    g['ssd_out_proj'] = _mm(yn, ds1b, ta=True, name='ssd_out_dw')[None]
    dyn = _mm(ds1b, w_out, tb=True, out_dtype=BF16, name='ssd_out_dx')
    dy1, dz, d_ng, d_dl = _gate_bwd(dyn, y_scan, xbc, z, d_lane, ssd_ng, d_inner, name='ssd_gate_bwd')
    dxs, dB, dC, ddt_g, da_g = _scan_bwd(xbc, dt_pre, dt_bias, a_log, states, dy1, d_lane, S, d_inner,
                                         name='ssd_scan_bwd')
    ddt_pre, d_bias, d_alog = _dt_bwd(ddt_g, da_g, dt_pre, dt_bias, a_log, name='ssd_dt_bwd')
    dxbc = jnp.concatenate([dxs, dB, dC], axis=1)
    dxbc_pre, d_scw, d_scb = _ssd_conv_bwd(xbc_pre, dxbc, ssd_cw, ssd_cb, S, name='ssd_conv_bwd')
    g['ssd_in_proj'] = jnp.concatenate(
        [_mm(Xb, dz, ta=True, name='ssd_in_z_dw'), _mm(Xb, dxbc_pre, ta=True, name='ssd_in_xbc_dw'),
         _mm(Xb, ddt_pre, ta=True, name='ssd_in_dt_dw')[:, :n_heads_ssd]], axis=1)[None]
    dx = _mm(dz, w_z, tb=True, add=ds1, add_scale=alpha, name='ssd_in_z_dx')
    dx = _mm(dxbc_pre, w_xbc, tb=True, add=dx, name='ssd_in_xbc_dx')
    dx = _mm(ddt_pre, w_dt, tb=True, add=dx, name='ssd_in_dt_dx')

    g['ssd_conv_w'] = d_scw[None]
    g['ssd_conv_b'] = d_scb
    g['ssd_dt_bias'] = d_bias[:, :n_heads_ssd]
    g['ssd_A_log'] = d_alog[:, :n_heads_ssd]
    g['ssd_D'] = jnp.sum(d_dl.reshape(1, n_heads_ssd, SSD_HEAD_DIM), axis=2)
    g['ssd_norm_g'] = d_ng
    g['ffn_up'] = jnp.stack([d_wu0, d_wu1])
    g['ffn_conv_w'] = jnp.stack([d_fcw0, d_fcw1])
    g['ffn_conv_b'] = jnp.concatenate([d_fcb0, d_fcb1], axis=0)
    g['ffn_down'] = jnp.stack([d_wd0, d_wd1])
    g['ln_mix_g'] = jnp.concatenate([dg_m0, dg_m1], axis=0)
    g['ln_mix_b'] = jnp.concatenate([db_m0, db_m1], axis=0)
    g['ln_ffn_g'] = jnp.concatenate([dg_f0, dg_f1], axis=0)
    g['ln_ffn_b'] = jnp.concatenate([db_f0, db_f1], axis=0)
    return sq, dx.reshape(Bn, S, D), g


ANY = pl.BlockSpec(memory_space=pl.ANY)


def _all_gather(xs, name):
    n = len(xs)

    def body(*refs):
        x_refs, out_refs = refs[:n], refs[n:2 * n]
        send_sems, recv_sems, local_sems = refs[2 * n:]
        x, y, c = lax.axis_index("x"), lax.axis_index("y"), lax.axis_index("c")
        me, sibling = (x, y, c), (x, y, 1 - c)
        chips = [(1 - x, y), (x, 1 - y), (1 - x, 1 - y)]

        def rows(i, px, py, pc):
            return out_refs[i].at[4 * px + 2 * py + pc]

        def copy(i, k, block, to, own=False):
            return pltpu.make_async_remote_copy(
                src_ref=x_refs[i] if own else rows(i, *block), dst_ref=rows(i, *block),
                send_sem=send_sems.at[7 * i + k], recv_sem=recv_sems.at[7 * i + k],
                device_id=to, device_id_type=MESH)

        mine = [pltpu.make_async_copy(x_refs[i], rows(i, *me), local_sems.at[i]) for i in range(n)]
        first = []
        for i in range(n):
            first.append(copy(i, 0, me, sibling, own=True))
            first += [copy(i, 1 + j, me, (*chip, c), own=True) for j, chip in enumerate(chips)]
        for cp in mine + first:
            cp.start()
        passed = []
        for j, chip in enumerate(chips):
            for i in range(n):
                copy(i, 1 + j, (*chip, c), me).wait_recv()
                passed.append(copy(i, 4 + j, (*chip, c), sibling))
                passed[-1].start()
        for i in range(n):
            copy(i, 0, sibling, me).wait_recv()
            for j, chip in enumerate(chips):
                copy(i, 4 + j, (*chip, 1 - c), me).wait_recv()
        for cp in first + passed:
            cp.wait_send()
        for cp in mine:
            cp.wait()

    return pl.pallas_call(
        body, name=name, out_shape=[jax.ShapeDtypeStruct((N_DEV,) + a.shape, a.dtype) for a in xs],
        in_specs=[ANY] * n, out_specs=[ANY] * n,
        scratch_shapes=[pltpu.SemaphoreType.DMA((7 * n,)), pltpu.SemaphoreType.DMA((7 * n,)),
                        pltpu.SemaphoreType.DMA((n,))],
    )(*xs)


def _exchange_sibling(gs, name):
    n = len(gs)

    def body(*refs):
        g_refs, r_refs = refs[:n], refs[n:2 * n]
        send_sems, recv_sems = refs[2 * n:]
        x, y, c = lax.axis_index("x"), lax.axis_index("y"), lax.axis_index("c")
        copies = [pltpu.make_async_remote_copy(
            src_ref=g_refs[i].at[2 * k + (1 - c)], dst_ref=r_refs[i].at[k], send_sem=send_sems.at[4 * i + k],
            recv_sem=recv_sems.at[4 * i + k], device_id=(x, y, 1 - c), device_id_type=MESH)
            for i in range(n) for k in range(4)]
        for cp in copies:
            cp.start()
        for cp in copies:
            cp.wait()

    return pl.pallas_call(
        body, name=name, out_shape=[jax.ShapeDtypeStruct((4,) + g.shape[1:], g.dtype) for g in gs],
        in_specs=[ANY] * n, out_specs=[ANY] * n,
        scratch_shapes=[pltpu.SemaphoreType.DMA((4 * n,)), pltpu.SemaphoreType.DMA((4 * n,))],
    )(*gs)


def _exchange_chips(parts, name):
    n = len(parts)

    def body(*refs):
        p_refs, r_refs = refs[:n], refs[n:2 * n]
        send_sems, recv_sems = refs[2 * n:]
        x, y, c = lax.axis_index("x"), lax.axis_index("y"), lax.axis_index("c")
        chips = [(1 - x, y), (x, 1 - y), (1 - x, 1 - y)]
        copies = [pltpu.make_async_remote_copy(
            src_ref=p_refs[i].at[2 * cx + cy], dst_ref=r_refs[i].at[j], send_sem=send_sems.at[3 * i + j],
            recv_sem=recv_sems.at[3 * i + j], device_id=(cx, cy, c), device_id_type=MESH)
            for i in range(n) for j, (cx, cy) in enumerate(chips)]
        for cp in copies:
            cp.start()
        for cp in copies:
            cp.wait()

    return pl.pallas_call(
        body, name=name, out_shape=[jax.ShapeDtypeStruct((3,) + p.shape[1:], p.dtype) for p in parts],
        in_specs=[ANY] * n, out_specs=[ANY] * n,
        scratch_shapes=[pltpu.SemaphoreType.DMA((3 * n,)), pltpu.SemaphoreType.DMA((3 * n,))],
    )(*parts)


def _row_tile(rows, cols):
    want = max(8, (256 * 1024) // cols)
    for t in range(min(rows, want) // 8 * 8, 7, -8):
        if rows % t == 0:
            return t
    return rows


def _add_sibling(gfull, r1, core, name):
    _, rows, lanes = gfull.shape
    tr = _row_tile(rows, lanes)

    def body(c_ref, g_ref, r_ref, o_ref):
        o_ref[...] = g_ref[...] + r_ref[...]

    return pl.pallas_call(
        body, name=name,
        grid_spec=pltpu.PrefetchScalarGridSpec(
            num_scalar_prefetch=1, grid=(4, rows // tr),
            in_specs=[pl.BlockSpec((1, tr, lanes), lambda k, r, c_ref: (2 * k + c_ref[0], r, 0)),
                      pl.BlockSpec((1, tr, lanes), lambda k, r, c_ref: (k, r, 0))],
            out_specs=pl.BlockSpec((1, tr, lanes), lambda k, r, c_ref: (k, r, 0))),
        out_shape=jax.ShapeDtypeStruct((4, rows, lanes), F32), compiler_params=_cparams(),
    )(core, gfull, r1)


def _adam_math(wv, gv, mv, vv):
    m = ADAM_B1 * mv + (1.0 - ADAM_B1) * gv
    v = ADAM_B2 * vv + (1.0 - ADAM_B2) * (gv * gv)
    m_hat = m / (1.0 - ADAM_B1 ** ADAM_STEP)
    v_hat = v / (1.0 - ADAM_B2 ** ADAM_STEP)
    delta = -ADAM_LR * (m_hat / (jnp.sqrt(v_hat) + ADAM_EPS) + ADAM_WD * wv)
    return delta, m, v


def _adam_sharded(part, r2, chip, wts, m, v, name):
    rows, lanes = wts.shape
    tr = _row_tile(rows, lanes)

    def body(k_ref, p_ref, r_ref, w_ref, m_ref, v_ref, g_ref, d_ref, mo_ref, vo_ref):
        gv = p_ref[0] + r_ref[0] + r_ref[1] + r_ref[2]
        delta, mn, vn = _adam_math(w_ref[...], gv, m_ref[...], v_ref[...])
        g_ref[...] = gv
        d_ref[...] = delta
        mo_ref[...] = mn
        vo_ref[...] = vn

    flat = pl.BlockSpec((tr, lanes), lambda r, k_ref: (r, 0))
    sh = jax.ShapeDtypeStruct((rows, lanes), F32)
    return pl.pallas_call(
        body, name=name,
        grid_spec=pltpu.PrefetchScalarGridSpec(
            num_scalar_prefetch=1, grid=(rows // tr,),
            in_specs=[pl.BlockSpec((1, tr, lanes), lambda r, k_ref: (k_ref[0], r, 0)),
                      pl.BlockSpec((3, tr, lanes), lambda r, k_ref: (0, r, 0)), flat, flat, flat],
            out_specs=[flat, flat, flat, flat]),
        out_shape=[sh, sh, sh, sh], compiler_params=_cparams(),
    )(chip, part, r2, wts, m, v)


def _adam_replicated(gall, wts, m, v, name):
    rows, lanes = wts.shape

    def body(ga_ref, w_ref, m_ref, v_ref, g_ref, d_ref, mo_ref, vo_ref):
        gv = ga_ref[0]
        for k in range(1, N_DEV):
            gv = gv + ga_ref[k]
        delta, mn, vn = _adam_math(w_ref[...], gv, m_ref[...], v_ref[...])
        g_ref[...] = gv
        d_ref[...] = delta
        mo_ref[...] = mn
        vo_ref[...] = vn

    sh = jax.ShapeDtypeStruct((rows, lanes), F32)
    return pl.pallas_call(body, name=name, out_shape=[sh, sh, sh, sh], compiler_params=_cparams())(gall, wts, m, v)


def _pack(arrs, dtype, row_mult):
    flat = jnp.concatenate([a.reshape(-1).astype(dtype) for a in arrs])
    n = flat.shape[0]
    unit = LANES * row_mult
    total = -(-n // unit) * unit
    return jnp.pad(flat, (0, total - n)).reshape(total // LANES, LANES)


def _unpack(flat2d, shapes):
    flat = flat2d.reshape(-1)
    out, off = [], 0
    for shp in shapes:
        n = math.prod(shp)
        out.append(flat[off:off + n].reshape(shp))
        off += n
    return out


def _unpack_gathered(gathered, shapes, axes):
    flat = gathered.reshape(N_DEV, -1)
    out, off = [], 0
    for shp, ax in zip(shapes, axes):
        n = math.prod(shp)
        seg = flat[:, off:off + n].reshape((N_DEV,) + tuple(shp))
        out.append(jnp.concatenate([seg[i] for i in range(N_DEV)], axis=ax))
        off += n
    return out


def _pack_chunks(fulls, axes, row_mult):
    per_dev = []
    for full, ax in zip(fulls, axes):
        parts = jnp.stack(jnp.split(full, N_DEV, axis=ax))
        per_dev.append(parts.reshape(N_DEV, -1))
    flat = jnp.concatenate(per_dev, axis=1)
    n = flat.shape[1]
    unit = LANES * row_mult
    total = -(-n // unit) * unit
    return jnp.pad(flat, ((0, 0), (0, total - n))).reshape(N_DEV, total // LANES, LANES)


def kernel(x, positions, ssd_in_proj, ssd_conv_w, ssd_conv_b, ssd_dt_bias, ssd_A_log, ssd_D, ssd_norm_g, ssd_out_proj, kv_down_proj, kv_norm_g, kv_up_k, kv_up_v, q_down_proj, q_norm_g, q_up_proj, attn_out_proj, ffn_up, ffn_conv_w, ffn_conv_b, ffn_down, ln_mix_g, ln_mix_b, ln_ffn_g, ln_ffn_b, loss_target, m_ssd_in_proj, m_ssd_conv_w, m_ssd_conv_b, m_ssd_dt_bias, m_ssd_A_log, m_ssd_D, m_ssd_norm_g, m_ssd_out_proj, m_kv_down_proj, m_kv_norm_g, m_kv_up_k, m_kv_up_v, m_q_down_proj, m_q_norm_g, m_q_up_proj, m_attn_out_proj, m_ffn_up, m_ffn_conv_w, m_ffn_conv_b, m_ffn_down, m_ln_mix_g, m_ln_mix_b, m_ln_ffn_g, m_ln_ffn_b, v_ssd_in_proj, v_ssd_conv_w, v_ssd_conv_b, v_ssd_dt_bias, v_ssd_A_log, v_ssd_D, v_ssd_norm_g, v_ssd_out_proj, v_kv_down_proj, v_kv_norm_g, v_kv_up_k, v_kv_up_v, v_q_down_proj, v_q_norm_g, v_q_up_proj, v_attn_out_proj, v_ffn_up, v_ffn_conv_w, v_ffn_conv_b, v_ffn_down, v_ln_mix_g, v_ln_mix_b, v_ln_ffn_g, v_ln_ffn_b):
    given = dict(locals())
    wl = {n: given[n] for n in WEIGHTS}
    ml = {n: given['m_' + n] for n in WEIGHTS}
    vl = {n: given['v_' + n] for n in WEIGHTS}
    sharded_axis = dict(SHARDED)
    big = [n for n, _ in SHARDED if n not in SHARDED_F32]
    small = [n for n, _ in SHARDED if n in SHARDED_F32]

    def as2d(a):
        return a.reshape(-1, a.shape[-1])

    def to_full(gathered, n):
        shp, ax = wl[n].shape, sharded_axis[n]
        moved = jnp.moveaxis(gathered.reshape((N_DEV,) + shp), 0, ax)
        return moved.reshape(shp[:ax] + (N_DEV * shp[ax],) + shp[ax + 1:])

    def to_chunks(full_grad, n):
        shp, ax = wl[n].shape, sharded_axis[n]
        split = full_grad.reshape(shp[:ax] + (N_DEV, shp[ax]) + shp[ax + 1:])
        return jnp.moveaxis(split, ax, 0).reshape((N_DEV,) + as2d(wl[n]).shape)

    full = {n: wl[n] for n in REPLICATED}
    gathered = _all_gather([as2d(wl[n]).astype(BF16) for n in big] + [_pack([wl[n] for n in small], F32, 8)],
                           name='gather_weights')
    for n, gat in zip(big, gathered[:-1]):
        full[n] = to_full(gat, n)
    full.update(zip(small, _unpack_gathered(gathered[-1], [wl[n].shape for n in small],
                                            [sharded_axis[n] for n in small])))

    sq, grad_x, grads = _local_step(x, positions, full, loss_target)
    loss = lax.psum(0.5 * jnp.sum(sq) / x.shape[-1], ("x", "y", "c"))

    cx, cy, cc = lax.axis_index("x"), lax.axis_index("y"), lax.axis_index("c")
    core = jnp.reshape(cc, (1,)).astype(jnp.int32)
    chip = jnp.reshape(2 * cx + cy, (1,)).astype(jnp.int32)
    gfull = [to_chunks(grads[n], n) for n in big]
    gfull.append(_pack_chunks([grads[n] for n in small], [sharded_axis[n] for n in small], 8))
    r1 = _exchange_sibling(gfull, name='grads_to_sibling')
    parts = [_add_sibling(gf, r, core, name=f'grads_add_sibling_{i}') for i, (gf, r) in enumerate(zip(gfull, r1))]
    r2 = _exchange_chips(parts, name='grads_to_chips')
    res = {}
    kinds = ('grad', 'delta', 'new_m', 'new_v')
    for i, n in enumerate(big):
        outs = _adam_sharded(parts[i], r2[i], chip, as2d(wl[n]), as2d(ml[n]), as2d(vl[n]), name=f'adamw_{n}')
        for kind, a in zip(kinds, outs):
            res[kind, n] = a.reshape(wl[n].shape)
    outs = _adam_sharded(parts[-1], r2[-1], chip, _pack([wl[n] for n in small], F32, 8),
                         _pack([ml[n] for n in small], F32, 8), _pack([vl[n] for n in small], F32, 8),
                         name='adamw_small_sharded')
    for kind, packed in zip(kinds, outs):
        for n, a in zip(small, _unpack(packed, [wl[n].shape for n in small])):
            res[kind, n] = a

    rep = list(REPLICATED)
    rshapes = [wl[n].shape for n in rep]
    gall, = _all_gather([_pack([grads[n] for n in rep], F32, 8)], name='gather_replicated_grads')
    outs = _adam_replicated(gall, _pack([wl[n] for n in rep], F32, 8), _pack([ml[n] for n in rep], F32, 8),
                            _pack([vl[n] for n in rep], F32, 8), name='adamw_replicated')
    for kind, packed in zip(('grad', 'delta', 'new_m', 'new_v'), outs):
        for n, a in zip(rep, _unpack(packed, rshapes)):
            res[kind, n] = a

    return (loss, grad_x, *[res['grad', n] for n in WEIGHTS], *[res['delta', n] for n in WEIGHTS],
            *[res['new_m', n] for n in WEIGHTS], *[res['new_v', n] for n in WEIGHTS])
```

```python
import functools
import math

import jax
import jax.numpy as jnp
from jax import lax
from jax.experimental import pallas as pl
from jax.experimental.pallas import tpu as pltpu

F32 = jnp.float32
BF16 = jnp.bfloat16
MESH = pl.DeviceIdType.MESH

N_DEV = 8
LANES = 128
MM_TILES = (1024, 1408, 896, 512, 384, 256, 128)
MM_VMEM_BUDGET = 30 * 1024 * 1024
VMEM_LIMIT = 48 * 1024 * 1024
LN_EPS = 1e-5
RMS_EPS = 1e-6
SSD_HEAD_DIM = 64
SSD_N_GROUPS = 8
SSD_D_STATE = 128
SSD_CHUNK = 128
MLA_NOPE = 128
MLA_ROPE = 64
MLA_V = 128
ROPE_THETA = 10000.0
ADAM_LR = 0.001
ADAM_B1 = 0.9
ADAM_B2 = 0.999
ADAM_EPS = 1e-08
ADAM_WD = 0.01
ADAM_STEP = 10
NEG_BIG = -1e30

SHARDED = (('ssd_in_proj', 2), ('ssd_conv_w', 2), ('ssd_conv_b', 1), ('ssd_norm_g', 1), ('ssd_out_proj', 1),
           ('kv_down_proj', 0), ('kv_up_k', 1), ('kv_up_v', 1), ('q_down_proj', 1), ('q_up_proj', 2),
           ('attn_out_proj', 1), ('ffn_up', 2), ('ffn_conv_w', 2), ('ffn_down', 1))
SHARDED_F32 = ('ssd_conv_w', 'ssd_conv_b', 'ssd_norm_g', 'ffn_conv_w')
GATHERED_FIRST = ('ssd_in_proj', 'ssd_out_proj')
REDUCED_LAST = ('ssd_in_proj',)
REPLICATED = ('ssd_dt_bias', 'ssd_A_log', 'ssd_D', 'kv_norm_g', 'q_norm_g', 'ffn_conv_b',
              'ln_mix_g', 'ln_mix_b', 'ln_ffn_g', 'ln_ffn_b')
WEIGHTS = ('ssd_in_proj', 'ssd_conv_w', 'ssd_conv_b', 'ssd_dt_bias', 'ssd_A_log', 'ssd_D', 'ssd_norm_g',
           'ssd_out_proj', 'kv_down_proj', 'kv_norm_g', 'kv_up_k', 'kv_up_v', 'q_down_proj', 'q_norm_g',
           'q_up_proj', 'attn_out_proj', 'ffn_up', 'ffn_conv_w', 'ffn_conv_b', 'ffn_down', 'ln_mix_g',
           'ln_mix_b', 'ln_ffn_g', 'ln_ffn_b')


def _cparams():
    return pltpu.CompilerParams(vmem_limit_bytes=VMEM_LIMIT)


def _tile(n, cands):
    for c in cands:
        if n % c == 0:
            return c
    return n


def _sigmoid(x):
    return 1.0 / (1.0 + jnp.exp(-x))


def _iota(shape, axis):
    return lax.broadcasted_iota(jnp.int32, shape, axis)


def _mm(a, b, *, ta=False, tb=False, add=None, add_scale=1.0, out_dtype=F32, name):
    if ta:
        K, M = a.shape
    else:
        M, K = a.shape
    if tb:
        N, K2 = b.shape
    else:
        K2, N = b.shape
    assert K == K2, (a.shape, b.shape, ta, tb)
    tm = _tile(M, MM_TILES)
    tn = _tile(N, MM_TILES)
    tk = K if K <= MM_TILES[0] else _tile(K, MM_TILES)
    nk = K // tk

    def vmem_estimate(tm_, tn_):
        ins = 2 * (tm_ * tk * a.dtype.itemsize + tk * tn_ * b.dtype.itemsize)
        outs = tm_ * tn_ * (2 * jnp.dtype(out_dtype).itemsize + 4 + (4 if nk > 1 else 0))
        return ins + outs + (2 * tm_ * tn_ * add.dtype.itemsize if add is not None else 0)

    while vmem_estimate(tm, tn) > MM_VMEM_BUDGET:
        can_m, can_n = tm % (2 * LANES) == 0, tn % (2 * LANES) == 0
        if can_m and (tm >= tn or not can_n):
            tm //= 2
        elif can_n:
            tn //= 2
        else:
            break

    def body(a_ref, b_ref, *rest):
        add_ref = rest[0] if add is not None else None
        o_ref = rest[1] if add is not None else rest[0]
        acc = rest[-1] if nk > 1 else None
        k = pl.program_id(2)

        if ta:
            av = a_ref[...].astype(BF16).T
        else:
            av = a_ref[...].astype(BF16)
        bv = b_ref[...].astype(BF16)
        dn = (((1,), (1 if tb else 0,)), ((), ()))
        part = lax.dot_general(av, bv, dn, preferred_element_type=F32)

        def finish(r):
            if add is not None:
                r = r + add_scale * add_ref[...].astype(F32)
            o_ref[...] = r.astype(out_dtype)

        if nk == 1:
            finish(part)
        else:
            @pl.when(k == 0)
            def _():
                acc[...] = part

            @pl.when(k > 0)
            def _():
                acc[...] += part

            @pl.when(k == nk - 1)
            def _():
                finish(acc[...])

    a_spec = (pl.BlockSpec((tk, tm), lambda i, j, k: (k, i)) if ta
              else pl.BlockSpec((tm, tk), lambda i, j, k: (i, k)))
    b_spec = (pl.BlockSpec((tn, tk), lambda i, j, k: (j, k)) if tb
              else pl.BlockSpec((tk, tn), lambda i, j, k: (k, j)))
    o_spec = pl.BlockSpec((tm, tn), lambda i, j, k: (i, j))
    in_specs = [a_spec, b_spec]
    args = [a, b]
    if add is not None:
        in_specs.append(o_spec)
        args.append(add)
    return pl.pallas_call(
        body, name=name, grid=(M // tm, N // tn, nk),
        in_specs=in_specs, out_specs=o_spec,
        out_shape=jax.ShapeDtypeStruct((M, N), out_dtype),
        scratch_shapes=[pltpu.VMEM((tm, tn), F32)] if nk > 1 else [],
        compiler_params=_cparams(),
    )(*args)


def _ln_fwd(h, mix, g, b, alpha, name):
    T, D = h.shape
    tm = _tile(T, (256, 128))

    def body(h_ref, m_ref, g_ref, b_ref, y_ref, s_ref, yb_ref):
        s = alpha * h_ref[...] + m_ref[...]
        mu = jnp.mean(s, axis=1, keepdims=True)
        xc = s - mu
        var = jnp.mean(xc * xc, axis=1, keepdims=True)
        y = xc * lax.rsqrt(var + LN_EPS) * g_ref[...] + b_ref[...]
        y_ref[...] = y
        s_ref[...] = s
        yb_ref[...] = y.astype(BF16)

    row = pl.BlockSpec((tm, D), lambda i: (i, 0))
    vec = pl.BlockSpec((1, D), lambda i: (0, 0))
    return pl.pallas_call(
        body, name=name, grid=(T // tm,), in_specs=[row, row, vec, vec], out_specs=[row, row, row],
        out_shape=[jax.ShapeDtypeStruct((T, D), F32)] * 2 + [jax.ShapeDtypeStruct((T, D), BF16)],
        compiler_params=_cparams(),
    )(h, mix, g, b)


def _ln_bwd(dy, s, g, name):
    T, D = s.shape
    tm = _tile(T, (256, 128))

    def body(dy_ref, s_ref, g_ref, ds_ref, dsb_ref, dg_ref, db_ref):
        @pl.when(pl.program_id(0) == 0)
        def _():
            dg_ref[...] = jnp.zeros_like(dg_ref)
            db_ref[...] = jnp.zeros_like(db_ref)

        sv = s_ref[...]
        dyv = dy_ref[...]
        mu = jnp.mean(sv, axis=1, keepdims=True)
        xc = sv - mu
        rstd = lax.rsqrt(jnp.mean(xc * xc, axis=1, keepdims=True) + LN_EPS)
        xhat = xc * rstd
        dxh = dyv * g_ref[...]
        m1 = jnp.mean(dxh, axis=1, keepdims=True)
        m2 = jnp.mean(dxh * xhat, axis=1, keepdims=True)
        ds = rstd * (dxh - m1 - xhat * m2)
        ds_ref[...] = ds
        dsb_ref[...] = ds.astype(BF16)
        dg_ref[...] += jnp.sum(dyv * xhat, axis=0, keepdims=True)
        db_ref[...] += jnp.sum(dyv, axis=0, keepdims=True)

    row = pl.BlockSpec((tm, D), lambda i: (i, 0))
    vec = pl.BlockSpec((1, D), lambda i: (0, 0))
    return pl.pallas_call(
        body, name=name, grid=(T // tm,), in_specs=[row, row, vec], out_specs=[row, row, vec, vec],
        out_shape=[jax.ShapeDtypeStruct((T, D), F32), jax.ShapeDtypeStruct((T, D), BF16),
                   jax.ShapeDtypeStruct((1, D), F32), jax.ShapeDtypeStruct((1, D), F32)],
        compiler_params=_cparams(),
    )(dy, s, g)


def _loss_head(y, target, name):
    T, D = y.shape
    tm = _tile(T, (256, 128))

    def body(y_ref, t_ref, sq_ref, dy_ref):
        @pl.when(pl.program_id(0) == 0)
        def _():
            sq_ref[...] = jnp.zeros_like(sq_ref)

        e = y_ref[...] - t_ref[...]
        sq_ref[...] += jnp.sum(e * e, axis=0, keepdims=True)
        dy_ref[...] = e * (1.0 / D)

    row = pl.BlockSpec((tm, D), lambda i: (i, 0))
    vec = pl.BlockSpec((1, D), lambda i: (0, 0))
    return pl.pallas_call(
        body, name=name, grid=(T // tm,), in_specs=[row, row], out_specs=[vec, row],
        out_shape=[jax.ShapeDtypeStruct((1, D), F32), jax.ShapeDtypeStruct((T, D), F32)],
        compiler_params=_cparams(),
    )(y, target)


def _shift_down(x, j):
    if j == 0:
        return x
    return jnp.where(_iota(x.shape, 0) >= j, pltpu.roll(x, j, 0), 0.0)


def _shift_up(x, j):
    if j == 0:
        return x
    n = x.shape[0]
    return jnp.where(_iota(x.shape, 0) < n - j, pltpu.roll(x, n - j, 0), 0.0)


def _conv_val(x, w_ref, b_ref, cs=slice(None)):
    width = w_ref.shape[0]
    y = b_ref[:, cs] + w_ref[width - 1:width, cs] * x
    for j in range(1, width):
        y = y + w_ref[width - 1 - j:width - j, cs] * _shift_down(x, j)
    return y


def _conv_bwd_val(x, du, w_ref, dw_ref, db_ref, cs=slice(None)):
    width = w_ref.shape[0]
    dx = w_ref[width - 1:width, cs] * du
    for j in range(1, width):
        dx = dx + w_ref[width - 1 - j:width - j, cs] * _shift_up(du, j)
    for j in range(width):
        dw_ref[width - 1 - j:width - j, cs] += jnp.sum(du * _shift_down(x, j), axis=0, keepdims=True)
    db_ref[:, cs] += jnp.sum(du, axis=0, keepdims=True)
    return dx


GATE = slice(0, LANES)
VALUE = slice(LANES, 2 * LANES)


def _interleave(a, H):
    lead = a.shape[:-1]
    return jnp.swapaxes(a.reshape(lead + (2, H // LANES, LANES)), -3, -2).reshape(lead + (2 * H,))


def _deinterleave(a, H):
    lead = a.shape[:-1]
    return jnp.swapaxes(a.reshape(lead + (H // LANES, 2, LANES)), -3, -2).reshape(lead + (2 * H,))


def _ffn_act_fwd(u, cw, cb, S, name):
    T, H2 = u.shape
    H = H2 // 2
    width = cw.shape[0]

    def body(u_ref, w_ref, b_ref, a_ref):
        g = _conv_val(u_ref[:, GATE], w_ref, b_ref, GATE)
        v = _conv_val(u_ref[:, VALUE], w_ref, b_ref, VALUE)
        a_ref[...] = (g * _sigmoid(g) * v).astype(BF16)

    ublk = pl.BlockSpec((S, 2 * LANES), lambda j, b: (b, j))
    ablk = pl.BlockSpec((S, LANES), lambda j, b: (b, j))
    wblk = pl.BlockSpec((width, 2 * LANES), lambda j, b: (0, j))
    bblk = pl.BlockSpec((1, 2 * LANES), lambda j, b: (0, j))
    return pl.pallas_call(
        body, name=name, grid=(H // LANES, T // S), in_specs=[ublk, wblk, bblk], out_specs=ablk,
        out_shape=jax.ShapeDtypeStruct((T, H), BF16), compiler_params=_cparams(),
    )(u, cw, cb)


def _ffn_act_bwd(u, da, cw, cb, S, name):
    T, H2 = u.shape
    H = H2 // 2
    width = cw.shape[0]

    def body(u_ref, da_ref, w_ref, b_ref, du_ref, dw_ref, db_ref):
        @pl.when(pl.program_id(1) == 0)
        def _():
            dw_ref[...] = jnp.zeros_like(dw_ref)
            db_ref[...] = jnp.zeros_like(db_ref)

        xg = u_ref[:, GATE]
        xv = u_ref[:, VALUE]
        g = _conv_val(xg, w_ref, b_ref, GATE)
        v = _conv_val(xv, w_ref, b_ref, VALUE)
        dav = da_ref[...].astype(F32)
        sg = _sigmoid(g)
        dg = dav * v * sg * (1.0 + g * (1.0 - sg))
        dv = dav * g * sg
        du_ref[:, GATE] = _conv_bwd_val(xg, dg, w_ref, dw_ref, db_ref, GATE).astype(BF16)
        du_ref[:, VALUE] = _conv_bwd_val(xv, dv, w_ref, dw_ref, db_ref, VALUE).astype(BF16)

    ublk = pl.BlockSpec((S, 2 * LANES), lambda j, b: (b, j))
    ablk = pl.BlockSpec((S, LANES), lambda j, b: (b, j))
    wblk = pl.BlockSpec((width, 2 * LANES), lambda j, b: (0, j))
    bblk = pl.BlockSpec((1, 2 * LANES), lambda j, b: (0, j))
    return pl.pallas_call(
        body, name=name, grid=(H // LANES, T // S), in_specs=[ublk, ablk, wblk, bblk],
        out_specs=[ublk, wblk, bblk],
        out_shape=[jax.ShapeDtypeStruct((T, H2), BF16), jax.ShapeDtypeStruct((width, H2), F32),
                   jax.ShapeDtypeStruct((1, H2), F32)],
        compiler_params=_cparams(),
    )(u, da, cw, cb)


def _ssd_conv_fwd(x, cw, cb, S, name):
    T, C = x.shape
    Cb = _tile(C, (256, 128))
    width = cw.shape[0]

    def body(x_ref, w_ref, b_ref, y_ref):
        u = _conv_val(x_ref[...], w_ref, b_ref)
        y_ref[...] = u * _sigmoid(u)

    blk = pl.BlockSpec((S, Cb), lambda j, b: (b, j))
    wblk = pl.BlockSpec((width, Cb), lambda j, b: (0, j))
    bblk = pl.BlockSpec((1, Cb), lambda j, b: (0, j))
    return pl.pallas_call(
        body, name=name, grid=(C // Cb, T // S), in_specs=[blk, wblk, bblk], out_specs=blk,
        out_shape=jax.ShapeDtypeStruct((T, C), F32), compiler_params=_cparams(),
    )(x, cw, cb)


def _ssd_conv_bwd(x, dy, cw, cb, S, name):
    T, C = x.shape
    Cb = _tile(C, (256, 128))
    width = cw.shape[0]

    def body(x_ref, dy_ref, w_ref, b_ref, dx_ref, dw_ref, db_ref):
        @pl.when(pl.program_id(1) == 0)
        def _():
            dw_ref[...] = jnp.zeros_like(dw_ref)
            db_ref[...] = jnp.zeros_like(db_ref)

        xv = x_ref[...]
        u = _conv_val(xv, w_ref, b_ref)
        su = _sigmoid(u)
        du = dy_ref[...] * su * (1.0 + u * (1.0 - su))
        dx_ref[...] = _conv_bwd_val(xv, du, w_ref, dw_ref, db_ref).astype(BF16)

    blk = pl.BlockSpec((S, Cb), lambda j, b: (b, j))
    wblk = pl.BlockSpec((width, Cb), lambda j, b: (0, j))
    bblk = pl.BlockSpec((1, Cb), lambda j, b: (0, j))
    return pl.pallas_call(
        body, name=name, grid=(C // Cb, T // S), in_specs=[blk, blk, wblk, bblk],
        out_specs=[blk, wblk, bblk],
        out_shape=[jax.ShapeDtypeStruct((T, C), BF16), jax.ShapeDtypeStruct((width, C), F32),
                   jax.ShapeDtypeStruct((1, C), F32)],
        compiler_params=_cparams(),
    )(x, dy, cw, cb)


def _softplus(x):
    e = jnp.exp(-jnp.abs(x))
    return jnp.maximum(x, 0.0) + jnp.where(e < 1e-4, e * (1.0 - 0.5 * e), jnp.log(1.0 + e))


def _cumsum_rows(x):
    n = x.shape[0]
    row = _iota(x.shape, 0)
    k = 1
    while k < n:
        x = x + jnp.where(row >= k, pltpu.roll(x, k, 0), 0.0)
        k *= 2
    return x


def _rev_cumsum_rows(x):
    n = x.shape[0]
    row = _iota(x.shape, 0)
    k = 1
    while k < n:
        x = x + jnp.where(row < n - k, pltpu.roll(x, n - k, 0), 0.0)
        k *= 2
    return x


def _col(x, lane_ids, h):
    return jnp.sum(jnp.where(lane_ids == h, x, 0.0), axis=1, keepdims=True)


def _bdot(a, b, dims):
    return lax.dot_general(a.astype(BF16), b.astype(BF16), (dims, ((), ())), preferred_element_type=F32)


NN = ((1,), (0,))
NT = ((1,), (1,))


def _scan_specs(nc, d_inner, hpg):
    L = SSD_CHUNK
    G = SSD_N_GROUPS
    gw = hpg * SSD_HEAD_DIM
    nb = d_inner // LANES
    return dict(
        xs=pl.BlockSpec((L, gw), lambda b, g, c: (b * nc + c, g)),
        B=pl.BlockSpec((L, LANES), lambda b, g, c: (b * nc + c, nb + g)),
        C=pl.BlockSpec((L, LANES), lambda b, g, c: (b * nc + c, nb + G + g)),
        dt=pl.BlockSpec((L, LANES), lambda b, g, c: (b * nc + c, 0)),
        vec=pl.BlockSpec((1, LANES), lambda b, g, c: (0, 0)),
        gvec=pl.BlockSpec((1, gw), lambda b, g, c: (0, g)),
        grp=pl.BlockSpec((L, LANES), lambda b, g, c: (b * nc + c, g)),
        st=pl.BlockSpec((1, hpg // 2, SSD_D_STATE, LANES), lambda b, g, c: ((b * G + g) * nc + c, 0, 0, 0)),
    )


def _grid_step(dims):
    t, total = 0, 1
    for ax, d in enumerate(dims):
        t = t * d + pl.program_id(ax)
        total *= d
    return t, total


def _scan_fwd(xbc, dt_pre, dt_bias, a_log, S, d_inner, name, gather=()):
    T = xbc.shape[0]
    Bn = T // S
    L = SSD_CHUNK
    G = SSD_N_GROUPS
    nc = S // L
    hpg = d_inner // SSD_HEAD_DIM // G
    pairs = hpg // 2
    sp = _scan_specs(nc, d_inner, hpg)
    ng = len(gather)

    def body(*refs):
        xs_ref, b_ref, c_ref, dtp_ref, bias_ref, alog_ref = refs[:6]
        y_ref, st_ref = refs[6 + ng:8 + ng]
        state = refs[8 + 2 * ng]
        g = pl.program_id(1)

        if ng:
            start, forward, finish = _gather_plan(refs[6:6 + ng], refs[8 + ng:8 + 2 * ng], *refs[9 + 2 * ng:])
            t, total = _grid_step((Bn, G, nc))
            pl.when(t == 0)(start)
            pl.when(t == total // 2)(forward)

        @pl.when(pl.program_id(2) == 0)
        def _():
            state[...] = jnp.zeros_like(state)

        dt = _softplus(dtp_ref[...] + bias_ref[...])
        cum = _cumsum_rows(dt * (-jnp.exp(alog_ref[...])))
        Bm = b_ref[...]
        Cm = c_ref[...]
        Gm = _bdot(Cm, Bm, NT)
        tril = _iota((L, L), 1) <= _iota((L, L), 0)
        lane = _iota((L, LANES), 1)
        lane1 = _iota((1, LANES), 1)
        last = _iota((L, 1), 0) == L - 1
        BmT = Bm.T
        for p in range(pairs):
            S_in = state[p]
            st_ref[0, p] = S_in
            x_pair = xs_ref[:, p * LANES:(p + 1) * LANES]
            y_pair = jnp.zeros((L, LANES), F32)
            z_pair = jnp.zeros((L, LANES), F32)
            e_pair = jnp.zeros((L, LANES), F32)
            el_pair = jnp.zeros((1, LANES), F32)
            for k in range(2):
                h = g * hpg + 2 * p + k
                cum_col = _col(cum, lane, h)
                dt_col = _col(dt, lane, h)
                cb = jnp.broadcast_to(cum_col, (L, L))
                decay = jnp.exp(jnp.where(tril, cb - cb.T, NEG_BIG))
                mk = (lane >= SSD_HEAD_DIM) if k else (lane < SSD_HEAD_DIM)
                xd = jnp.where(mk, x_pair * dt_col, 0.0)
                y_pair = y_pair + _bdot(Gm * decay, xd, NN)
                cum_l = jnp.sum(jnp.where(last, cum_col, 0.0), axis=0, keepdims=True)
                e_pair = jnp.where(mk, jnp.exp(cum_col), e_pair)
                z_pair = z_pair + xd * jnp.exp(cum_l - cum_col)
                el_pair = jnp.where((lane1 >= SSD_HEAD_DIM) if k else (lane1 < SSD_HEAD_DIM), jnp.exp(cum_l), el_pair)
            y_pair = y_pair + e_pair * _bdot(Cm, S_in, NN)
            state[p] = S_in * el_pair + _bdot(BmT, z_pair, NN)
            y_ref[:, p * LANES:(p + 1) * LANES] = y_pair

        if ng:
            pl.when(t == total - 1)(finish)

    outs = pl.pallas_call(
        body, name=name, grid=(Bn, G, nc),
        in_specs=[sp['xs'], sp['B'], sp['C'], sp['dt'], sp['vec'], sp['vec']] + [ANY] * ng,
        out_specs=[sp['xs'], sp['st']] + [ANY] * ng,
        out_shape=[jax.ShapeDtypeStruct((T, d_inner), F32),
                   jax.ShapeDtypeStruct((Bn * G * nc, pairs, SSD_D_STATE, LANES), F32)] + _gather_shapes(gather),
        scratch_shapes=[pltpu.VMEM((pairs, SSD_D_STATE, LANES), F32)] + (_gather_sems(ng) if ng else []),
        compiler_params=_cparams(),
    )(xbc, xbc, xbc, dt_pre, dt_bias, a_log, *gather)
    return outs[0], outs[1], list(outs[2:])


def _scan_bwd(xbc, dt_pre, dt_bias, a_log, states, dy, d_lane, S, d_inner, name, exchange=()):
    T = xbc.shape[0]
    Bn = T // S
    L = SSD_CHUNK
    G = SSD_N_GROUPS
    nc = S // L
    hpg = d_inner // SSD_HEAD_DIM // G
    pairs = hpg // 2
    sp = _scan_specs(nc, d_inner, hpg)
    nx = len(exchange)

    def rev(spec):
        im = spec.index_map
        return pl.BlockSpec(spec.block_shape, lambda b, g, c: im(b, g, nc - 1 - c))

    def body(*refs):
        xs_ref, b_ref, c_ref, dtp_ref, bias_ref, alog_ref, st_ref, dy_ref, dl_ref = refs[:9]
        dxs_ref, db_ref, dc_ref, ddt_ref, da_ref = refs[9 + nx:14 + nx]
        dstate = refs[14 + 2 * nx]
        g = pl.program_id(1)

        if nx:
            start, finish = _chips_plan(refs[9:9 + nx], refs[14 + nx:14 + 2 * nx], *refs[15 + 2 * nx:])
            t, total = _grid_step((Bn, G, nc))
            pl.when(t == 0)(start)

        @pl.when(pl.program_id(2) == 0)
        def _():
            dstate[...] = jnp.zeros_like(dstate)

        dt = _softplus(dtp_ref[...] + bias_ref[...])
        cum = _cumsum_rows(dt * (-jnp.exp(alog_ref[...])))
        Bm = b_ref[...]
        Cm = c_ref[...]
        Gm = _bdot(Cm, Bm, NT)
        tril = _iota((L, L), 1) <= _iota((L, L), 0)
        lane = _iota((L, LANES), 1)
        lane1 = _iota((1, LANES), 1)
        last = _iota((L, 1), 0) == L - 1
        CmT = Cm.T
        dG = jnp.zeros((L, L), F32)
        dB = jnp.zeros((L, LANES), F32)
        dC = jnp.zeros((L, LANES), F32)
        dcum = jnp.zeros((L, LANES), F32)
        ddt = jnp.zeros((L, LANES), F32)
        for p in range(pairs):
            S_in = st_ref[0, p]
            dS = dstate[p]
            x_pair = xs_ref[:, p * LANES:(p + 1) * LANES]
            dy_pair = dy_ref[:, p * LANES:(p + 1) * LANES]
            Q = _bdot(Cm, S_in, NN)
            dZ = _bdot(Bm, dS, NN)
            prod = dS * S_in
            e_pair = jnp.zeros((L, LANES), F32)
            z_pair = jnp.zeros((L, LANES), F32)
            el_pair = jnp.zeros((1, LANES), F32)
            dx_pair = dl_ref[:, p * LANES:(p + 1) * LANES] * dy_pair
            for k in range(2):
                h = g * hpg + 2 * p + k
                cum_col = _col(cum, lane, h)
                dt_col = _col(dt, lane, h)
                cb = jnp.broadcast_to(cum_col, (L, L))
                decay = jnp.exp(jnp.where(tril, cb - cb.T, NEG_BIG))
                mk = (lane >= SSD_HEAD_DIM) if k else (lane < SSD_HEAD_DIM)
                mk1 = (lane1 >= SSD_HEAD_DIM) if k else (lane1 < SSD_HEAD_DIM)
                xd = jnp.where(mk, x_pair * dt_col, 0.0)
                dy_k = jnp.where(mk, dy_pair, 0.0)
                Wm = Gm * decay
                dWm = jnp.where(tril, _bdot(dy_k, xd, NT), 0.0)
                dxd = _bdot(Wm.T, dy_k, NN)
                dseg = dWm * Wm
                dG = dG + dWm * decay
                dcum_col = (jnp.sum(dseg, axis=1, keepdims=True)
                            - jnp.sum(dseg.T, axis=1, keepdims=True))
                cum_l = jnp.sum(jnp.where(last, cum_col, 0.0), axis=0, keepdims=True)
                e_col = jnp.exp(cum_col)
                w_col = jnp.exp(cum_l - cum_col)
                el = jnp.exp(cum_l)
                dcum_col = dcum_col + jnp.sum(dy_k * Q, axis=1, keepdims=True) * e_col
                de_e = jnp.sum(dZ * xd, axis=1, keepdims=True) * w_col
                dcum_col = dcum_col - de_e
                dcum_l = (jnp.sum(de_e, axis=0, keepdims=True)
                          + el * jnp.sum(jnp.sum(jnp.where(mk, prod, 0.0), axis=1, keepdims=True),
                                         axis=0, keepdims=True))
                dcum_col = dcum_col + jnp.where(last, dcum_l, 0.0)
                dxd = dxd + jnp.where(mk, dZ * w_col, 0.0)
                dx_pair = dx_pair + dxd * dt_col
                ddt = jnp.where(lane == h, jnp.sum(dxd * x_pair, axis=1, keepdims=True), ddt)
                dcum = jnp.where(lane == h, dcum_col, dcum)
                e_pair = jnp.where(mk, e_col, e_pair)
                z_pair = z_pair + xd * w_col
                el_pair = jnp.where(mk1, el, el_pair)
            dQ = e_pair * dy_pair
            dC = dC + _bdot(dQ, S_in, NT)
            dB = dB + _bdot(z_pair, dS, NT)
            dstate[p] = dS * el_pair + _bdot(CmT, dQ, NN)
            dxs_ref[:, p * LANES:(p + 1) * LANES] = dx_pair
        dc_ref[...] = dC + _bdot(dG, Bm, NN)
        db_ref[...] = dB + _bdot(dG.T, Cm, NN)
        ddt_ref[...] = ddt
        da_ref[...] = _rev_cumsum_rows(dcum)

        if nx:
            pl.when(t == total - 1)(finish)

    grp = jax.ShapeDtypeStruct((T, G * LANES), F32)
    outs = pl.pallas_call(
        body, name=name, grid=(Bn, G, nc),
        in_specs=[rev(sp['xs']), rev(sp['B']), rev(sp['C']), rev(sp['dt']), sp['vec'], sp['vec'],
                  rev(sp['st']), rev(sp['xs']), sp['gvec']] + [ANY] * nx,
        out_specs=[rev(sp['xs']), rev(sp['grp']), rev(sp['grp']), rev(sp['grp']), rev(sp['grp'])] + [ANY] * nx,
        out_shape=[jax.ShapeDtypeStruct((T, d_inner), F32), grp, grp, grp, grp] + _chips_shapes(exchange),
        scratch_shapes=[pltpu.VMEM((pairs, SSD_D_STATE, LANES), F32)] + (_chips_sems(nx) if nx else []),
        compiler_params=_cparams(),
    )(xbc, xbc, xbc, dt_pre, dt_bias, a_log, states, dy, d_lane, *exchange)
    return outs[:5], list(outs[5:])


def _dt_bwd(ddt_g, da_g, dt_pre, dt_bias, a_log, name):
    T = dt_pre.shape[0]
    G = SSD_N_GROUPS
    tm = _tile(T, (512, 256, 128))

    def body(ddt_ref, da_ref, dtp_ref, bias_ref, alog_ref, dx_ref, dbias_ref, dal_ref):
        @pl.when(pl.program_id(0) == 0)
        def _():
            dbias_ref[...] = jnp.zeros_like(dbias_ref)
            dal_ref[...] = jnp.zeros_like(dal_ref)

        ddt = ddt_ref[:, 0:LANES]
        da = da_ref[:, 0:LANES]
        for gi in range(1, G):
            ddt = ddt + ddt_ref[:, gi * LANES:(gi + 1) * LANES]
            da = da + da_ref[:, gi * LANES:(gi + 1) * LANES]
        xv = dtp_ref[...] + bias_ref[...]
        A = -jnp.exp(alog_ref[...])
        dx = (ddt + da * A) * _sigmoid(xv)
        dx_ref[...] = dx.astype(BF16)
        dbias_ref[...] += jnp.sum(dx, axis=0, keepdims=True)
        dal_ref[...] += jnp.sum(da * _softplus(xv), axis=0, keepdims=True) * A

    wide = pl.BlockSpec((tm, G * LANES), lambda i: (i, 0))
    row = pl.BlockSpec((tm, LANES), lambda i: (i, 0))
    vec = pl.BlockSpec((1, LANES), lambda i: (0, 0))
    vsh = jax.ShapeDtypeStruct((1, LANES), F32)
    return pl.pallas_call(
        body, name=name, grid=(T // tm,), in_specs=[wide, wide, row, vec, vec], out_specs=[row, vec, vec],
        out_shape=[jax.ShapeDtypeStruct((T, LANES), BF16), vsh, vsh], compiler_params=_cparams(),
    )(ddt_g, da_g, dt_pre, dt_bias, a_log)


def _gate_fwd(y, xbc, z, d_lane, ng, d_inner, name):
    T = y.shape[0]
    G = SSD_N_GROUPS
    gw = d_inner // G
    tm = _tile(T, (512, 256, 128))

    def body(y_ref, x_ref, z_ref, dl_ref, ng_ref, o_ref):
        zv = z_ref[...]
        y2 = (y_ref[...] + dl_ref[...] * x_ref[...]) * (zv * _sigmoid(zv))
        r = lax.rsqrt(jnp.mean(y2 * y2, axis=1, keepdims=True) + LN_EPS)
        o_ref[...] = (y2 * r * ng_ref[...]).astype(BF16)

    blk = pl.BlockSpec((tm, gw), lambda g, i: (i, g))
    vec = pl.BlockSpec((1, gw), lambda g, i: (0, g))
    return pl.pallas_call(
        body, name=name, grid=(G, T // tm), in_specs=[blk, blk, blk, vec, vec], out_specs=blk,
        out_shape=jax.ShapeDtypeStruct((T, d_inner), BF16), compiler_params=_cparams(),
    )(y, xbc, z, d_lane, ng)


def _gate_bwd(dyn, y, xbc, z, d_lane, ng, d_inner, name):
    T = y.shape[0]
    G = SSD_N_GROUPS
    gw = d_inner // G
    tm = _tile(T, (512, 256, 128))

    def body(dyn_ref, y_ref, x_ref, z_ref, dl_ref, ng_ref, dy_ref, dz_ref, dng_ref, ddl_ref):
        @pl.when(pl.program_id(1) == 0)
        def _():
            dng_ref[...] = jnp.zeros_like(dng_ref)
            ddl_ref[...] = jnp.zeros_like(ddl_ref)

        zv = z_ref[...]
        xv = x_ref[...]
        sz = _sigmoid(zv)
        y1 = y_ref[...] + dl_ref[...] * xv
        y2 = y1 * (zv * sz)
        r = lax.rsqrt(jnp.mean(y2 * y2, axis=1, keepdims=True) + LN_EPS)
        dn = dyn_ref[...].astype(F32)
        dng_ref[...] += jnp.sum(dn * y2 * r, axis=0, keepdims=True)
        do = dn * ng_ref[...]
        dy2 = r * do - y2 * (r * r * r) * jnp.mean(do * y2, axis=1, keepdims=True)
        dz_ref[...] = (dy2 * y1 * sz * (1.0 + zv * (1.0 - sz))).astype(BF16)
        dy1 = dy2 * (zv * sz)
        dy_ref[...] = dy1
        ddl_ref[...] += jnp.sum(dy1 * xv, axis=0, keepdims=True)

    blk = pl.BlockSpec((tm, gw), lambda g, i: (i, g))
    vec = pl.BlockSpec((1, gw), lambda g, i: (0, g))
    vsh = jax.ShapeDtypeStruct((1, d_inner), F32)
    return pl.pallas_call(
        body, name=name, grid=(G, T // tm), in_specs=[blk, blk, blk, blk, vec, vec],
        out_specs=[blk, blk, vec, vec],
        out_shape=[jax.ShapeDtypeStruct((T, d_inner), F32), jax.ShapeDtypeStruct((T, d_inner), BF16), vsh, vsh],
        compiler_params=_cparams(),
    )(dyn, y, xbc, z, d_lane, ng)


def _rms_fwd(x, g, name):
    T, R = x.shape
    tm = _tile(T, (512, 256, 128))

    def body(x_ref, g_ref, y_ref):
        xv = x_ref[...]
        y = xv * lax.rsqrt(jnp.mean(xv * xv, axis=1, keepdims=True) + RMS_EPS) * g_ref[...]
        y_ref[...] = y.astype(BF16)

    row = pl.BlockSpec((tm, R), lambda i: (i, 0))
    vec = pl.BlockSpec((1, R), lambda i: (0, 0))
    return pl.pallas_call(
        body, name=name, grid=(T // tm,), in_specs=[row, vec], out_specs=row,
        out_shape=jax.ShapeDtypeStruct((T, R), BF16), compiler_params=_cparams(),
    )(x, g)


def _rms_bwd(dy, x, g, name):
    T, R = x.shape
    tm = _tile(T, (512, 256, 128))

    def body(dy_ref, x_ref, g_ref, dx_ref, dg_ref):
        @pl.when(pl.program_id(0) == 0)
        def _():
            dg_ref[...] = jnp.zeros_like(dg_ref)

        xv = x_ref[...]
        dyv = dy_ref[...]
        r = lax.rsqrt(jnp.mean(xv * xv, axis=1, keepdims=True) + RMS_EPS)
        dg_ref[...] += jnp.sum(dyv * xv * r, axis=0, keepdims=True)
        do = dyv * g_ref[...]
        dx = r * do - xv * (r * r * r) * jnp.mean(do * xv, axis=1, keepdims=True)
        dx_ref[...] = dx.astype(BF16)

    row = pl.BlockSpec((tm, R), lambda i: (i, 0))
    vec = pl.BlockSpec((1, R), lambda i: (0, 0))
    return pl.pallas_call(
        body, name=name, grid=(T // tm,), in_specs=[row, row, vec], out_specs=[row, vec],
        out_shape=[jax.ShapeDtypeStruct((T, R), BF16), jax.ShapeDtypeStruct((1, R), F32)],
        compiler_params=_cparams(),
    )(dy, x, g)


def _swap_halves(x):
    half = MLA_ROPE // 2
    return jnp.where(_iota(x.shape, 1) < half, pltpu.roll(x, LANES - half, 1), pltpu.roll(x, half, 1))


def _rope(x, cc, ss, *, transpose, sum_heads=False, name):
    T, W = x.shape
    nh = W // LANES
    tm = _tile(T, (512, 256, 128))
    n_out = 1 if sum_heads else nh

    def body(x_ref, cc_ref, ss_ref, o_ref):
        ccv = cc_ref[...]
        ssv = ss_ref[...]
        if sum_heads:
            xv = x_ref[:, 0:LANES]
            for hh in range(1, nh):
                xv = xv + x_ref[:, hh * LANES:(hh + 1) * LANES]
            heads = [xv]
        else:
            heads = [x_ref[:, hh * LANES:(hh + 1) * LANES] for hh in range(nh)]
        for hh, xv in enumerate(heads):
            if transpose:
                r = xv * ccv + _swap_halves(xv * ssv)
            else:
                r = xv * ccv + _swap_halves(xv) * ssv
            r = jnp.where(_iota(r.shape, 1) < MLA_ROPE, r, 0.0)
            o_ref[:, hh * LANES:(hh + 1) * LANES] = r.astype(BF16)

    return pl.pallas_call(
        body, name=name, grid=(T // tm,),
        in_specs=[pl.BlockSpec((tm, W), lambda i: (i, 0)), pl.BlockSpec((tm, LANES), lambda i: (i, 0)),
                  pl.BlockSpec((tm, LANES), lambda i: (i, 0))],
        out_specs=pl.BlockSpec((tm, n_out * LANES), lambda i: (i, 0)),
        out_shape=jax.ShapeDtypeStruct((T, n_out * LANES), BF16), compiler_params=_cparams(),
    )(x, cc, ss)


ATT_BLOCK = 512
ATT_SUB = 256
LOG2E = 1.4426950408889634


def _att_geometry(S):
    tb = _tile(S, (ATT_BLOCK, 256))
    return tb, S // tb, tb // ATT_SUB


def _cat(n_ref, r_ref, rows):
    return jnp.concatenate([n_ref[rows, :], r_ref[rows, :]], axis=1)


def _att_scores(a, b, diag, kv_major, off):
    s = _bdot(a, b, NT)
    if diag:
        q_ax, k_ax = (1, 0) if kv_major else (0, 1)
        s = jnp.where(_iota(s.shape, k_ax) <= _iota(s.shape, q_ax) + off, s, NEG_BIG)
    return s


def _attention_fwd(qn, qr, kn, kr, v, S, scale, name):
    T, W = qn.shape
    nh = W // LANES
    Bn = T // S
    tb, n, C = _att_geometry(S)
    pairs = [(i, j) for i in range(n) for j in range(i + 1)]
    it = jnp.asarray([p[0] for p in pairs], jnp.int32)
    jt = jnp.asarray([p[1] for p in pairs], jnp.int32)
    c2 = scale * LOG2E

    HP = 2 if nh % 2 == 0 else 1

    def body(it_ref, jt_ref, qn_ref, qr_ref, kn_ref, kr_ref, v_ref, o_ref, ob_ref, lse_ref, m_sc, l_sc, acc):
        p = pl.program_id(2)
        i = it_ref[p]
        j = jt_ref[p]

        @pl.when(j == 0)
        def _():
            m_sc[...] = jnp.full_like(m_sc, NEG_BIG)
            l_sc[...] = jnp.zeros_like(l_sc)
            acc[...] = jnp.zeros_like(acc)

        def step(diag):
            for hh in range(HP):
                hs = pl.ds(hh * LANES, LANES)
                qc = jnp.concatenate([qn_ref[:, hs], qr_ref[:, hs]], axis=1)
                kc = jnp.concatenate([kn_ref[:, hs], kr_ref[...]], axis=1)
                s = _att_scores(qc, kc, diag, False, 0)
                m_old = m_sc[hh]
                m_new = jnp.maximum(m_old, jnp.max(s, axis=1, keepdims=True))
                pr = jnp.exp2((s - m_new) * c2)
                corr = jnp.exp2((m_old - m_new) * c2)
                l_sc[hh] = corr * l_sc[hh] + jnp.sum(pr, axis=1, keepdims=True)
                acc[:, hs] = corr * acc[:, hs] + _bdot(pr, v_ref[:, hs], NN)
                m_sc[hh] = m_new

        @pl.when(j < i)
        def _():
            step(False)

        @pl.when(j == i)
        def _():
            step(True)
            for hh in range(HP):
                hs = pl.ds(hh * LANES, LANES)
                ov = acc[:, hs] / l_sc[hh]
                o_ref[:, hs] = ov
                ob_ref[:, hs] = ov.astype(BF16)
                lse_ref[:, hs] = jnp.broadcast_to(m_sc[hh] * scale + jnp.log(l_sc[hh]), (tb, LANES))

    qspec = pl.BlockSpec((tb, HP * LANES), lambda b, h, p, it_, jt_: (b * n + it_[p], h))
    kspec = pl.BlockSpec((tb, HP * LANES), lambda b, h, p, it_, jt_: (b * n + jt_[p], h))
    krspec = pl.BlockSpec((tb, LANES), lambda b, h, p, it_, jt_: (b * n + jt_[p], 0))
    return pl.pallas_call(
        body, name=name,
        grid_spec=pltpu.PrefetchScalarGridSpec(
            num_scalar_prefetch=2, grid=(Bn, nh // HP, len(pairs)),
            in_specs=[qspec, qspec, kspec, krspec, kspec], out_specs=[qspec, qspec, qspec],
            scratch_shapes=[pltpu.VMEM((HP, tb, 1), F32), pltpu.VMEM((HP, tb, 1), F32),
                            pltpu.VMEM((tb, HP * LANES), F32)]),
        out_shape=[jax.ShapeDtypeStruct((T, W), F32), jax.ShapeDtypeStruct((T, W), BF16),
                   jax.ShapeDtypeStruct((T, W), F32)],
        compiler_params=_cparams(),
    )(it, jt, qn, qr, kn, kr, v)


def _attention_dq(qn, qr, kn, kr, v, o, do, lse, S, scale, name):
    T, W = qn.shape
    nh = W // LANES
    Bn = T // S
    tb, n, C = _att_geometry(S)
    pairs = [(i, j) for i in range(n) for j in range(i + 1)]
    it = jnp.asarray([p[0] for p in pairs], jnp.int32)
    jt = jnp.asarray([p[1] for p in pairs], jnp.int32)
    c2 = scale * LOG2E

    def body(it_ref, jt_ref, qn_ref, qr_ref, kn_ref, kr_ref, v_ref, o_ref, do_ref, lse_ref,
             dqn_ref, dqr_ref, acc, di_sc, lse_sc):
        p = pl.program_id(2)
        i = it_ref[p]
        j = jt_ref[p]

        @pl.when(j == 0)
        def _():
            acc[...] = jnp.zeros_like(acc)
            di_sc[...] = jnp.sum(do_ref[...] * o_ref[...], axis=1, keepdims=True)
            lse_sc[...] = jnp.max(lse_ref[...], axis=1, keepdims=True) * LOG2E

        def step(diag):
            qc = _cat(qn_ref, qr_ref, pl.ds(0, tb))
            for c in range(C):
                r0 = c * ATT_SUB if diag else 0
                rows = pl.ds(r0, tb - r0)
                ks = pl.ds(c * ATT_SUB, ATT_SUB)
                kc = _cat(kn_ref, kr_ref, ks)
                s = _att_scores(qc[r0:], kc, diag, False, 0)
                pr = jnp.exp2(s * c2 - lse_sc[rows, :])
                dp = _bdot(do_ref[rows, :], v_ref[ks, :], NT)
                ds = pr * (dp - di_sc[rows, :]) * scale
                acc[rows, :] += _bdot(ds, kc, NN)

        @pl.when(j < i)
        def _():
            step(False)

        @pl.when(j == i)
        def _():
            step(True)
            dqn_ref[...] = acc[:, 0:LANES].astype(BF16)
            dqr_ref[...] = acc[:, LANES:2 * LANES]

    qspec = pl.BlockSpec((tb, LANES), lambda b, h, p, it_, jt_: (b * n + it_[p], h))
    kspec = pl.BlockSpec((tb, LANES), lambda b, h, p, it_, jt_: (b * n + jt_[p], h))
    krspec = pl.BlockSpec((tb, LANES), lambda b, h, p, it_, jt_: (b * n + jt_[p], 0))
    return pl.pallas_call(
        body, name=name,
        grid_spec=pltpu.PrefetchScalarGridSpec(
            num_scalar_prefetch=2, grid=(Bn, nh, len(pairs)),
            in_specs=[qspec, qspec, kspec, krspec, kspec, qspec, qspec, qspec], out_specs=[qspec, qspec],
            scratch_shapes=[pltpu.VMEM((tb, 2 * LANES), F32), pltpu.VMEM((tb, 1), F32), pltpu.VMEM((tb, 1), F32)]),
        out_shape=[jax.ShapeDtypeStruct((T, W), BF16), jax.ShapeDtypeStruct((T, W), F32)],
        compiler_params=_cparams(),
    )(it, jt, qn, qr, kn, kr, v, o, do, lse)


def _attention_dkv(qn, qr, kn, kr, v, o, do, lse, S, scale, name):
    T, W = qn.shape
    nh = W // LANES
    Bn = T // S
    tb, n, C = _att_geometry(S)
    triples = [(j, h, i) for j in range(n) for h in range(nh) for i in range(j, n)]
    jt = jnp.asarray([t[0] for t in triples], jnp.int32)
    ht = jnp.asarray([t[1] for t in triples], jnp.int32)
    it = jnp.asarray([t[2] for t in triples], jnp.int32)
    c2 = scale * LOG2E

    def as_row(col):
        return jnp.broadcast_to(col, (ATT_SUB, LANES)).T[0:1, :]

    def body(jt_ref, ht_ref, it_ref, qn_ref, qr_ref, kn_ref, kr_ref, v_ref, o_ref, do_ref, lse_ref,
             dkn_ref, dv_ref, dkr_ref, akc, av):
        p = pl.program_id(1)
        j = jt_ref[p]
        h = ht_ref[p]
        i = it_ref[p]

        @pl.when(jnp.logical_and(h == 0, i == j))
        def _():
            dkr_ref[...] = jnp.zeros_like(dkr_ref)

        @pl.when(i == j)
        def _():
            akc[...] = jnp.zeros_like(akc)
            av[...] = jnp.zeros_like(av)

        def step(diag):
            kc = _cat(kn_ref, kr_ref, pl.ds(0, tb))
            for c in range(C):
                rows = pl.ds(c * ATT_SUB, ATT_SUB)
                k1 = (c + 1) * ATT_SUB if diag else tb
                ks = pl.ds(0, k1)
                qc = _cat(qn_ref, qr_ref, rows)
                st = _att_scores(kc[:k1], qc, diag, True, c * ATT_SUB)
                dov = do_ref[rows, :]
                lse_row = as_row(jnp.max(lse_ref[rows, :], axis=1, keepdims=True)) * LOG2E
                di_row = as_row(jnp.sum(dov * o_ref[rows, :], axis=1, keepdims=True))
                pt = jnp.exp2(st * c2 - lse_row)
                dst = pt * (_bdot(v_ref[ks, :], dov, NT) - di_row) * scale
                av[ks, :] += _bdot(pt, dov, NN)
                akc[ks, :] += _bdot(dst, qc, NN)

        @pl.when(i > j)
        def _():
            step(False)

        @pl.when(i == j)
        def _():
            step(True)

        @pl.when(i == n - 1)
        def _():
            dkn_ref[...] = akc[:, 0:LANES].astype(BF16)
            dkr_ref[...] += akc[:, LANES:2 * LANES]
            dv_ref[...] = av[...].astype(BF16)

    qspec = pl.BlockSpec((tb, LANES), lambda b, p, jt_, ht_, it_: (b * n + it_[p], ht_[p]))
    kspec = pl.BlockSpec((tb, LANES), lambda b, p, jt_, ht_, it_: (b * n + jt_[p], ht_[p]))
    krspec = pl.BlockSpec((tb, LANES), lambda b, p, jt_, ht_, it_: (b * n + jt_[p], 0))
    return pl.pallas_call(
        body, name=name,
        grid_spec=pltpu.PrefetchScalarGridSpec(
            num_scalar_prefetch=3, grid=(Bn, len(triples)),
            in_specs=[qspec, qspec, kspec, krspec, kspec, qspec, qspec, qspec],
            out_specs=[kspec, kspec, krspec],
            scratch_shapes=[pltpu.VMEM((tb, 2 * LANES), F32), pltpu.VMEM((tb, LANES), F32)]),
        out_shape=[jax.ShapeDtypeStruct((T, W), BF16), jax.ShapeDtypeStruct((T, W), BF16),
                   jax.ShapeDtypeStruct((T, LANES), F32)],
        compiler_params=_cparams(),
    )(jt, ht, it, qn, qr, kn, kr, v, o, do, lse)


def _pad_cols(w, n):
    return jnp.pad(w, ((0, 0), (0, n - w.shape[1])))


def _local_step(x, positions, w, target, late=None, reduce_early=None):
    w = dict(w)
    Bn, S, D = x.shape
    T = Bn * S
    depth = w['ln_mix_g'].shape[0]
    alpha = (2 * depth) ** 0.25
    d_inner = w['ssd_out_proj'].shape[1]
    n_heads_ssd = d_inner // SSD_HEAD_DIM
    G = SSD_N_GROUPS
    gn = G * SSD_D_STATE
    conv_dim = d_inner + 2 * gn
    hpg = n_heads_ssd // G
    nh = D // MLA_NOPE
    kv_rank = w['kv_norm_g'].shape[0]
    q_rank = w['q_norm_g'].shape[1]
    H = w['ffn_conv_b'].shape[1] // 2
    scale = (MLA_NOPE + MLA_ROPE) ** -0.5
    assert S % SSD_CHUNK == 0 and hpg % 2 == 0 and SSD_D_STATE == LANES and n_heads_ssd <= LANES

    X = x.reshape(T, D)
    tgt = target.reshape(T, D)

    inv_freq = 1.0 / (ROPE_THETA ** (jnp.arange(0, MLA_ROPE, 2, dtype=F32) / MLA_ROPE))
    ang = positions.reshape(T).astype(F32)[:, None] * inv_freq
    cos, sin = jnp.cos(ang), jnp.sin(ang)
    zpad = jnp.zeros((T, LANES - MLA_ROPE), F32)
    cc = jnp.concatenate([cos, cos, zpad], axis=1)
    ss = jnp.concatenate([-sin, sin, zpad], axis=1)

    w_in = w['ssd_in_proj'][0]
    w_z = w_in[:, :d_inner]
    w_xbc = w_in[:, d_inner:d_inner + conv_dim]
    w_dt = _pad_cols(w_in[:, d_inner + conv_dim:], LANES)
    w_out = w['ssd_out_proj'][0]
    ssd_cw = w['ssd_conv_w'][0]
    ssd_cb = w['ssd_conv_b']
    dt_bias = _pad_cols(w['ssd_dt_bias'], LANES)
    a_log = _pad_cols(w['ssd_A_log'], LANES)
    d_lane = jnp.repeat(w['ssd_D'], SSD_HEAD_DIM, axis=1)
    ssd_ng = w['ssd_norm_g']

    Xb = X.astype(BF16)
    z = _mm(Xb, w_z, name='ssd_in_z')
    xbc_pre = _mm(Xb, w_xbc, name='ssd_in_xbc')
    dt_pre = _mm(Xb, w_dt, name='ssd_in_dt')
    xbc = _ssd_conv_fwd(xbc_pre, ssd_cw, ssd_cb, S, name='ssd_conv')
    y_scan, states, gathered = _scan_fwd(xbc, dt_pre, dt_bias, a_log, S, d_inner, name='ssd_scan',
                                         gather=late[0] if late else ())
    if late:
        w.update(late[1](gathered))

    w_kvc = w['kv_down_proj'][:, :kv_rank]
    w_kvr = _pad_cols(w['kv_down_proj'][:, kv_rank:], LANES)
    kv_g = w['kv_norm_g'].reshape(1, kv_rank)
    w_uk = w['kv_up_k']
    w_uv = w['kv_up_v']
    w_qd = w['q_down_proj'][0]
    q_g = w['q_norm_g']
    qu = w['q_up_proj'][0].reshape(q_rank, nh, MLA_NOPE + MLA_ROPE)
    w_qn = qu[:, :, :MLA_NOPE].reshape(q_rank, nh * LANES)
    w_qr = jnp.pad(qu[:, :, MLA_NOPE:], ((0, 0), (0, 0), (0, LANES - MLA_ROPE))).reshape(q_rank, nh * LANES)
    w_ao = w['attn_out_proj'][0]

    def ffn_parts(i):
        return dict(wu=_interleave(w['ffn_up'][i], H), cw=_interleave(w['ffn_conv_w'][i], H),
                    cb=_interleave(w['ffn_conv_b'][i:i + 1], H), wd=w['ffn_down'][i])

    def ln(name, i):
        return w[name][i:i + 1]

    def ffn_fwd(hb, i):
        f = ffn_parts(i)
        u = _mm(hb, f['wu'], name=f'ffn{i}_up')
        a = _ffn_act_fwd(u, f['cw'], f['cb'], S, name=f'ffn{i}_act')
        return _mm(a, f['wd'], name=f'ffn{i}_down'), (u, a)

    yn = _gate_fwd(y_scan, xbc, z, d_lane, ssd_ng, d_inner, name='ssd_gate')
    mix0 = _mm(yn, w_out, name='ssd_out')
    h1, s1, h1b = _ln_fwd(X, mix0, ln('ln_mix_g', 0), ln('ln_mix_b', 0), alpha, name='ln_mix0')
    ff0, ffn0_saved = ffn_fwd(h1b, 0)
    h2, s2, h2b = _ln_fwd(h1, ff0, ln('ln_ffn_g', 0), ln('ln_ffn_b', 0), alpha, name='ln_ffn0')

    ckv = _mm(h2b, w_kvc, name='kv_down_c')
    kr_pre = _mm(h2b, w_kvr, name='kv_down_r')
    ckvn = _rms_fwd(ckv, kv_g, name='kv_norm')
    kr = _rope(kr_pre, cc, ss, transpose=False, name='k_rope')
    kn = _mm(ckvn, w_uk, out_dtype=BF16, name='kv_up_k')
    v = _mm(ckvn, w_uv, out_dtype=BF16, name='kv_up_v')
    cq_pre = _mm(h2b, w_qd, name='q_down')
    cq = _rms_fwd(cq_pre, q_g, name='q_norm')
    qn = _mm(cq, w_qn, out_dtype=BF16, name='q_up_n')
    qr_pre = _mm(cq, w_qr, name='q_up_r')
    qr = _rope(qr_pre, cc, ss, transpose=False, name='q_rope')
    o, ob, lse = _attention_fwd(qn, qr, kn, kr, v, S, scale, name='attn_fwd')
    mix1 = _mm(ob, w_ao, name='attn_out')
    h3, s3, h3b = _ln_fwd(h2, mix1, ln('ln_mix_g', 1), ln('ln_mix_b', 1), alpha, name='ln_mix1')
    ff1, ffn1_saved = ffn_fwd(h3b, 1)
    h4, s4, _ = _ln_fwd(h3, ff1, ln('ln_ffn_g', 1), ln('ln_ffn_b', 1), alpha, name='ln_ffn1')
    sq, dh4 = _loss_head(h4, tgt, name='loss_head')

    g = {}

    def ffn_bwd(ds, dsb, hb_in, saved, i):
        f = ffn_parts(i)
        u, a = saved
        d_wd = _mm(a, dsb, ta=True, name=f'ffn{i}_down_dw')
        da = _mm(dsb, f['wd'], tb=True, out_dtype=BF16, name=f'ffn{i}_down_dx')
        du, dcw, dcb = _ffn_act_bwd(u, da, f['cw'], f['cb'], S, name=f'ffn{i}_act_bwd')
        d_wu = _deinterleave(_mm(hb_in, du, ta=True, name=f'ffn{i}_up_dw'), H)
        dh = _mm(du, f['wu'], tb=True, add=ds, add_scale=alpha, name=f'ffn{i}_up_dx')
        return dh, d_wd, d_wu, _deinterleave(dcw, H), _deinterleave(dcb, H)

    ds4, ds4b, dg_f1, db_f1 = _ln_bwd(dh4, s4, ln('ln_ffn_g', 1), name='ln_ffn1_bwd')
    dh3, d_wd1, d_wu1, d_fcw1, d_fcb1 = ffn_bwd(ds4, ds4b, h3b, ffn1_saved, 1)
    ds3, ds3b, dg_m1, db_m1 = _ln_bwd(dh3, s3, ln('ln_mix_g', 1), name='ln_mix1_bwd')

    g['attn_out_proj'] = _mm(ob, ds3b, ta=True, name='attn_out_dw')[None]
    do = _mm(ds3b, w_ao, tb=True, name='attn_out_dx')
    dqn, dqr = _attention_dq(qn, qr, kn, kr, v, o, do, lse, S, scale, name='attn_bwd_dq')
    dkn, dv, dkr = _attention_dkv(qn, qr, kn, kr, v, o, do, lse, S, scale, name='attn_bwd_dkv')
    dqr_pre = _rope(dqr, cc, ss, transpose=True, name='q_rope_bwd')
    dkr_pre = _rope(dkr, cc, ss, transpose=True, name='k_rope_bwd')

    d_wqn = _mm(cq, dqn, ta=True, name='q_up_n_dw').reshape(q_rank, nh, LANES)
    d_wqr = _mm(cq, dqr_pre, ta=True, name='q_up_r_dw').reshape(q_rank, nh, LANES)[:, :, :MLA_ROPE]
    g['q_up_proj'] = jnp.concatenate([d_wqn, d_wqr], axis=2).reshape(1, q_rank, nh * (MLA_NOPE + MLA_ROPE))
    dcq = _mm(dqn, w_qn, tb=True, name='q_up_n_dx')
    dcq = _mm(dqr_pre, w_qr, tb=True, add=dcq, name='q_up_r_dx')
    dcq_pre, d_qg = _rms_bwd(dcq, cq_pre, q_g, name='q_norm_bwd')
    g['q_norm_g'] = d_qg
    g['q_down_proj'] = _mm(h2b, dcq_pre, ta=True, name='q_down_dw')[None]

    g['kv_up_k'] = _mm(ckvn, dkn, ta=True, name='kv_up_k_dw')
    g['kv_up_v'] = _mm(ckvn, dv, ta=True, name='kv_up_v_dw')
    dckvn = _mm(dkn, w_uk, tb=True, name='kv_up_k_dx')
    dckvn = _mm(dv, w_uv, tb=True, add=dckvn, name='kv_up_v_dx')
    dckv, d_kvg = _rms_bwd(dckvn, ckv, kv_g, name='kv_norm_bwd')
    g['kv_norm_g'] = d_kvg.reshape(kv_rank)
    g['kv_down_proj'] = jnp.concatenate(
        [_mm(h2b, dckv, ta=True, name='kv_down_c_dw'),
         _mm(h2b, dkr_pre, ta=True, name='kv_down_r_dw')[:, :MLA_ROPE]], axis=1)

    dh2 = _mm(dcq_pre, w_qd, tb=True, add=ds3, add_scale=alpha, name='q_down_dx')
    dh2 = _mm(dckv, w_kvc, tb=True, add=dh2, name='kv_down_c_dx')
    dh2 = _mm(dkr_pre, w_kvr, tb=True, add=dh2, name='kv_down_r_dx')

    ds2, ds2b, dg_f0, db_f0 = _ln_bwd(dh2, s2, ln('ln_ffn_g', 0), name='ln_ffn0_bwd')
    dh1, d_wd0, d_wu0, d_fcw0, d_fcb0 = ffn_bwd(ds2, ds2b, h1b, ffn0_saved, 0)
    ds1, ds1b, dg_m0, db_m0 = _ln_bwd(dh1, s1, ln('ln_mix_g', 0), name='ln_mix0_bwd')

    g['ssd_out_proj'] = _mm(yn, ds1b, ta=True, name='ssd_out_dw')[None]
    dyn = _mm(ds1b, w_out, tb=True, out_dtype=BF16, name='ssd_out_dx')
    dy1, dz, d_ng, d_dl = _gate_bwd(dyn, y_scan, xbc, z, d_lane, ssd_ng, d_inner, name='ssd_gate_bwd')
    g['ffn_up'] = jnp.stack([d_wu0, d_wu1])
    g['ffn_down'] = jnp.stack([d_wd0, d_wd1])
    (dxs, dB, dC, ddt_g, da_g), early = _scan_bwd(xbc, dt_pre, dt_bias, a_log, states, dy1, d_lane, S, d_inner,
                                                  name='ssd_scan_bwd',
                                                  exchange=reduce_early(g) if reduce_early else ())
    ddt_pre, d_bias, d_alog = _dt_bwd(ddt_g, da_g, dt_pre, dt_bias, a_log, name='ssd_dt_bwd')
    dxbc = jnp.concatenate([dxs, dB, dC], axis=1)
    dxbc_pre, d_scw, d_scb = _ssd_conv_bwd(xbc_pre, dxbc, ssd_cw, ssd_cb, S, name='ssd_conv_bwd')
    g['ssd_in_proj'] = jnp.concatenate(
        [_mm(Xb, dz, ta=True, name='ssd_in_z_dw'), _mm(Xb, dxbc_pre, ta=True, name='ssd_in_xbc_dw'),
         _mm(Xb, ddt_pre, ta=True, name='ssd_in_dt_dw')[:, :n_heads_ssd]], axis=1)[None]
    dx = _mm(dz, w_z, tb=True, add=ds1, add_scale=alpha, name='ssd_in_z_dx')
    dx = _mm(dxbc_pre, w_xbc, tb=True, add=dx, name='ssd_in_xbc_dx')
    dx = _mm(ddt_pre, w_dt, tb=True, add=dx, name='ssd_in_dt_dx')

    g['ssd_conv_w'] = d_scw[None]
    g['ssd_conv_b'] = d_scb
    g['ssd_dt_bias'] = d_bias[:, :n_heads_ssd]
    g['ssd_A_log'] = d_alog[:, :n_heads_ssd]
    g['ssd_D'] = jnp.sum(d_dl.reshape(1, n_heads_ssd, SSD_HEAD_DIM), axis=2)
    g['ssd_norm_g'] = d_ng
    g['ffn_conv_w'] = jnp.stack([d_fcw0, d_fcw1])
    g['ffn_conv_b'] = jnp.concatenate([d_fcb0, d_fcb1], axis=0)
    g['ln_mix_g'] = jnp.concatenate([dg_m0, dg_m1], axis=0)
    g['ln_mix_b'] = jnp.concatenate([db_m0, db_m1], axis=0)
    g['ln_ffn_g'] = jnp.concatenate([dg_f0, dg_f1], axis=0)
    g['ln_ffn_b'] = jnp.concatenate([db_f0, db_f1], axis=0)
    return sq, dx.reshape(Bn, S, D), g, early


ANY = pl.BlockSpec(memory_space=pl.ANY)


def _all_gather(xs, name):
    n = len(xs)

    def body(*refs):
        start, forward, finish = _gather_plan(refs[:n], refs[n:2 * n], *refs[2 * n:])
        start()
        forward()
        finish()

    return pl.pallas_call(
        body, name=name, out_shape=_gather_shapes(xs), in_specs=[ANY] * n, out_specs=[ANY] * n,
        scratch_shapes=_gather_sems(n),
    )(*xs)


def _gather_shapes(xs):
    return [jax.ShapeDtypeStruct((N_DEV,) + a.shape, a.dtype) for a in xs]


def _gather_sems(n):
    return [pltpu.SemaphoreType.DMA((7 * n,)), pltpu.SemaphoreType.DMA((7 * n,)), pltpu.SemaphoreType.DMA((n,))]


def _gather_plan(x_refs, out_refs, send_sems, recv_sems, local_sems):
    n = len(x_refs)
    x, y, c = lax.axis_index("x"), lax.axis_index("y"), lax.axis_index("c")
    me, sibling = (x, y, c), (x, y, 1 - c)
    chips = [(1 - x, y), (x, 1 - y), (1 - x, 1 - y)]

    def rows(i, px, py, pc):
        return out_refs[i].at[4 * px + 2 * py + pc]

    def copy(i, k, block, to, own=False):
        return pltpu.make_async_remote_copy(
            src_ref=x_refs[i] if own else rows(i, *block), dst_ref=rows(i, *block),
            send_sem=send_sems.at[7 * i + k], recv_sem=recv_sems.at[7 * i + k],
            device_id=to, device_id_type=MESH)

    def mine():
        return [pltpu.make_async_copy(x_refs[i], rows(i, *me), local_sems.at[i]) for i in range(n)]

    def first():
        out = []
        for i in range(n):
            out.append(copy(i, 0, me, sibling, own=True))
            out += [copy(i, 1 + j, me, (*chip, c), own=True) for j, chip in enumerate(chips)]
        return out

    def passed():
        return [copy(i, 4 + j, (*chip, c), sibling) for j, chip in enumerate(chips) for i in range(n)]

    def start():
        for cp in mine() + first():
            cp.start()

    def forward():
        for j, chip in enumerate(chips):
            for i in range(n):
                copy(i, 1 + j, (*chip, c), me).wait_recv()
                copy(i, 4 + j, (*chip, c), sibling).start()

    def finish():
        for i in range(n):
            copy(i, 0, sibling, me).wait_recv()
            for j, chip in enumerate(chips):
                copy(i, 4 + j, (*chip, 1 - c), me).wait_recv()
        for cp in first() + passed():
            cp.wait_send()
        for cp in mine():
            cp.wait()

    return start, forward, finish


def _exchange_sibling(gs, name):
    n = len(gs)

    def body(*refs):
        g_refs, r_refs = refs[:n], refs[n:2 * n]
        send_sems, recv_sems = refs[2 * n:]
        x, y, c = lax.axis_index("x"), lax.axis_index("y"), lax.axis_index("c")
        copies = [pltpu.make_async_remote_copy(
            src_ref=g_refs[i].at[2 * k + (1 - c)], dst_ref=r_refs[i].at[k], send_sem=send_sems.at[4 * i + k],
            recv_sem=recv_sems.at[4 * i + k], device_id=(x, y, 1 - c), device_id_type=MESH)
            for i in range(n) for k in range(4)]
        for cp in copies:
            cp.start()
        for cp in copies:
            cp.wait()

    return pl.pallas_call(
        body, name=name, out_shape=[jax.ShapeDtypeStruct((4,) + g.shape[1:], g.dtype) for g in gs],
        in_specs=[ANY] * n, out_specs=[ANY] * n,
        scratch_shapes=[pltpu.SemaphoreType.DMA((4 * n,)), pltpu.SemaphoreType.DMA((4 * n,))],
    )(*gs)


def _exchange_chips(parts, name):
    n = len(parts)

    def body(*refs):
        start, finish = _chips_plan(refs[:n], refs[n:2 * n], *refs[2 * n:])
        start()
        finish()

    return pl.pallas_call(
        body, name=name, out_shape=_chips_shapes(parts), in_specs=[ANY] * n, out_specs=[ANY] * n,
        scratch_shapes=_chips_sems(n),
    )(*parts)


def _chips_shapes(parts):
    return [jax.ShapeDtypeStruct((3,) + p.shape[1:], p.dtype) for p in parts]


def _chips_sems(n):
    return [pltpu.SemaphoreType.DMA((3 * n,)), pltpu.SemaphoreType.DMA((3 * n,))]


def _chips_plan(p_refs, r_refs, send_sems, recv_sems):
    n = len(p_refs)
    x, y, c = lax.axis_index("x"), lax.axis_index("y"), lax.axis_index("c")
    chips = [(1 - x, y), (x, 1 - y), (1 - x, 1 - y)]

    def copies():
        return [pltpu.make_async_remote_copy(
            src_ref=p_refs[i].at[2 * cx + cy], dst_ref=r_refs[i].at[j], send_sem=send_sems.at[3 * i + j],
            recv_sem=recv_sems.at[3 * i + j], device_id=(cx, cy, c), device_id_type=MESH)
            for i in range(n) for j, (cx, cy) in enumerate(chips)]

    def start():
        for cp in copies():
            cp.start()

    def finish():
        for cp in copies():
            cp.wait()

    return start, finish


def _row_tile(rows, cols):
    want = max(8, (256 * 1024) // cols)
    for t in range(min(rows, want) // 8 * 8, 7, -8):
        if rows % t == 0:
            return t
    return rows


def _add_sibling(gfull, r1, core, name):
    _, rows, lanes = gfull.shape
    tr = _row_tile(rows, lanes)

    def body(c_ref, g_ref, r_ref, o_ref):
        o_ref[...] = g_ref[...] + r_ref[...]

    return pl.pallas_call(
        body, name=name,
        grid_spec=pltpu.PrefetchScalarGridSpec(
            num_scalar_prefetch=1, grid=(4, rows // tr),
            in_specs=[pl.BlockSpec((1, tr, lanes), lambda k, r, c_ref: (2 * k + c_ref[0], r, 0)),
                      pl.BlockSpec((1, tr, lanes), lambda k, r, c_ref: (k, r, 0))],
            out_specs=pl.BlockSpec((1, tr, lanes), lambda k, r, c_ref: (k, r, 0))),
        out_shape=jax.ShapeDtypeStruct((4, rows, lanes), F32), compiler_params=_cparams(),
    )(core, gfull, r1)


def _adam_math(wv, gv, mv, vv):
    m = ADAM_B1 * mv + (1.0 - ADAM_B1) * gv
    v = ADAM_B2 * vv + (1.0 - ADAM_B2) * (gv * gv)
    m_hat = m / (1.0 - ADAM_B1 ** ADAM_STEP)
    v_hat = v / (1.0 - ADAM_B2 ** ADAM_STEP)
    delta = -ADAM_LR * (m_hat / (jnp.sqrt(v_hat) + ADAM_EPS) + ADAM_WD * wv)
    return delta, m, v


def _adam_sharded(part, r2, chip, wts, m, v, name):
    rows, lanes = wts.shape
    tr = _row_tile(rows, lanes)

    def body(k_ref, p_ref, r_ref, w_ref, m_ref, v_ref, g_ref, d_ref, mo_ref, vo_ref):
        gv = p_ref[0] + r_ref[0] + r_ref[1] + r_ref[2]
        delta, mn, vn = _adam_math(w_ref[...], gv, m_ref[...], v_ref[...])
        g_ref[...] = gv
        d_ref[...] = delta
        mo_ref[...] = mn
        vo_ref[...] = vn

    flat = pl.BlockSpec((tr, lanes), lambda r, k_ref: (r, 0))
    sh = jax.ShapeDtypeStruct((rows, lanes), F32)
    return pl.pallas_call(
        body, name=name,
        grid_spec=pltpu.PrefetchScalarGridSpec(
            num_scalar_prefetch=1, grid=(rows // tr,),
            in_specs=[pl.BlockSpec((1, tr, lanes), lambda r, k_ref: (k_ref[0], r, 0)),
                      pl.BlockSpec((3, tr, lanes), lambda r, k_ref: (0, r, 0)), flat, flat, flat],
            out_specs=[flat, flat, flat, flat]),
        out_shape=[sh, sh, sh, sh], compiler_params=_cparams(),
    )(chip, part, r2, wts, m, v)


def _adam_replicated(gall, wts, m, v, name):
    rows, lanes = wts.shape

    def body(ga_ref, w_ref, m_ref, v_ref, g_ref, d_ref, mo_ref, vo_ref):
        gv = ga_ref[0]
        for k in range(1, N_DEV):
            gv = gv + ga_ref[k]
        delta, mn, vn = _adam_math(w_ref[...], gv, m_ref[...], v_ref[...])
        g_ref[...] = gv
        d_ref[...] = delta
        mo_ref[...] = mn
        vo_ref[...] = vn

    sh = jax.ShapeDtypeStruct((rows, lanes), F32)
    return pl.pallas_call(body, name=name, out_shape=[sh, sh, sh, sh], compiler_params=_cparams())(gall, wts, m, v)


def _pack(arrs, dtype, row_mult):
    flat = jnp.concatenate([a.reshape(-1).astype(dtype) for a in arrs])
    n = flat.shape[0]
    unit = LANES * row_mult
    total = -(-n // unit) * unit
    return jnp.pad(flat, (0, total - n)).reshape(total // LANES, LANES)


def _unpack(flat2d, shapes):
    flat = flat2d.reshape(-1)
    out, off = [], 0
    for shp in shapes:
        n = math.prod(shp)
        out.append(flat[off:off + n].reshape(shp))
        off += n
    return out


def _unpack_gathered(gathered, shapes, axes):
    flat = gathered.reshape(N_DEV, -1)
    out, off = [], 0
    for shp, ax in zip(shapes, axes):
        n = math.prod(shp)
        seg = flat[:, off:off + n].reshape((N_DEV,) + tuple(shp))
        out.append(jnp.concatenate([seg[i] for i in range(N_DEV)], axis=ax))
        off += n
    return out


def _pack_chunks(fulls, axes, row_mult):
    per_dev = []
    for full, ax in zip(fulls, axes):
        parts = jnp.stack(jnp.split(full, N_DEV, axis=ax))
        per_dev.append(parts.reshape(N_DEV, -1))
    flat = jnp.concatenate(per_dev, axis=1)
    n = flat.shape[1]
    unit = LANES * row_mult
    total = -(-n // unit) * unit
    return jnp.pad(flat, ((0, 0), (0, total - n))).reshape(N_DEV, total // LANES, LANES)


def kernel(x, positions, ssd_in_proj, ssd_conv_w, ssd_conv_b, ssd_dt_bias, ssd_A_log, ssd_D, ssd_norm_g, ssd_out_proj, kv_down_proj, kv_norm_g, kv_up_k, kv_up_v, q_down_proj, q_norm_g, q_up_proj, attn_out_proj, ffn_up, ffn_conv_w, ffn_conv_b, ffn_down, ln_mix_g, ln_mix_b, ln_ffn_g, ln_ffn_b, loss_target, m_ssd_in_proj, m_ssd_conv_w, m_ssd_conv_b, m_ssd_dt_bias, m_ssd_A_log, m_ssd_D, m_ssd_norm_g, m_ssd_out_proj, m_kv_down_proj, m_kv_norm_g, m_kv_up_k, m_kv_up_v, m_q_down_proj, m_q_norm_g, m_q_up_proj, m_attn_out_proj, m_ffn_up, m_ffn_conv_w, m_ffn_conv_b, m_ffn_down, m_ln_mix_g, m_ln_mix_b, m_ln_ffn_g, m_ln_ffn_b, v_ssd_in_proj, v_ssd_conv_w, v_ssd_conv_b, v_ssd_dt_bias, v_ssd_A_log, v_ssd_D, v_ssd_norm_g, v_ssd_out_proj, v_kv_down_proj, v_kv_norm_g, v_kv_up_k, v_kv_up_v, v_q_down_proj, v_q_norm_g, v_q_up_proj, v_attn_out_proj, v_ffn_up, v_ffn_conv_w, v_ffn_conv_b, v_ffn_down, v_ln_mix_g, v_ln_mix_b, v_ln_ffn_g, v_ln_ffn_b):
    given = dict(locals())
    wl = {n: given[n] for n in WEIGHTS}
    ml = {n: given['m_' + n] for n in WEIGHTS}
    vl = {n: given['v_' + n] for n in WEIGHTS}
    sharded_axis = dict(SHARDED)
    big = [n for n, _ in SHARDED if n not in SHARDED_F32]
    small = [n for n, _ in SHARDED if n in SHARDED_F32]

    def as2d(a):
        return a.reshape(-1, a.shape[-1])

    def to_full(gathered, n):
        shp, ax = wl[n].shape, sharded_axis[n]
        moved = jnp.moveaxis(gathered.reshape((N_DEV,) + shp), 0, ax)
        return moved.reshape(shp[:ax] + (N_DEV * shp[ax],) + shp[ax + 1:])

    def to_chunks(full_grad, n):
        shp, ax = wl[n].shape, sharded_axis[n]
        split = full_grad.reshape(shp[:ax] + (N_DEV, shp[ax]) + shp[ax + 1:])
        return jnp.moveaxis(split, ax, 0).reshape((N_DEV,) + as2d(wl[n]).shape)

    first = [n for n in big if n in GATHERED_FIRST]
    late = [n for n in big if n not in GATHERED_FIRST]
    full = {n: wl[n] for n in REPLICATED}
    gathered = _all_gather([as2d(wl[n]).astype(BF16) for n in first] + [_pack([wl[n] for n in small], F32, 8)],
                           name='gather_weights')
    for n, gat in zip(first, gathered[:-1]):
        full[n] = to_full(gat, n)
    full.update(zip(small, _unpack_gathered(gathered[-1], [wl[n].shape for n in small],
                                            [sharded_axis[n] for n in small])))

    cx, cy, cc = lax.axis_index("x"), lax.axis_index("y"), lax.axis_index("c")
    core = jnp.reshape(cc, (1,)).astype(jnp.int32)
    chip = jnp.reshape(2 * cx + cy, (1,)).astype(jnp.int32)
    early_names = [n for n in big if n not in REDUCED_LAST]
    last_names = [n for n in big if n in REDUCED_LAST]
    parts = {}

    def to_sibling(chunked, names, tag):
        r1 = _exchange_sibling(chunked, name=f'grads_to_sibling_{tag}')
        return [_add_sibling(gf, r, core, name=f'grads_add_sibling_{n}') for n, gf, r in zip(names, chunked, r1)]

    def reduce_early(grads):
        parts.update(zip(early_names, to_sibling([to_chunks(grads[n], n) for n in early_names], early_names, 'early')))
        return [parts[n] for n in early_names]

    sq, grad_x, grads, r2_early = _local_step(
        x, positions, full, loss_target,
        late=([as2d(wl[n]).astype(BF16) for n in late], lambda gats: {n: to_full(g, n) for n, g in zip(late, gats)}),
        reduce_early=reduce_early)
    loss = lax.psum(0.5 * jnp.sum(sq) / x.shape[-1], ("x", "y", "c"))

    chunked = [to_chunks(grads[n], n) for n in last_names]
    chunked.append(_pack_chunks([grads[n] for n in small], [sharded_axis[n] for n in small], 8))
    parts_last = to_sibling(chunked, last_names + ['small'], 'last')
    r2_last = _exchange_chips(parts_last, name='grads_to_chips')
    parts.update(zip(last_names + ['small'], parts_last))
    r2 = dict(zip(early_names, r2_early))
    r2.update(zip(last_names + ['small'], r2_last))
    res = {}
    kinds = ('grad', 'delta', 'new_m', 'new_v')
    for n in big:
        outs = _adam_sharded(parts[n], r2[n], chip, as2d(wl[n]), as2d(ml[n]), as2d(vl[n]), name=f'adamw_{n}')
        for kind, a in zip(kinds, outs):
            res[kind, n] = a.reshape(wl[n].shape)
    outs = _adam_sharded(parts['small'], r2['small'], chip, _pack([wl[n] for n in small], F32, 8),
                         _pack([ml[n] for n in small], F32, 8), _pack([vl[n] for n in small], F32, 8),
                         name='adamw_small_sharded')
    for kind, packed in zip(kinds, outs):
        for n, a in zip(small, _unpack(packed, [wl[n].shape for n in small])):
            res[kind, n] = a

    rep = list(REPLICATED)
    rshapes = [wl[n].shape for n in rep]
    gall, = _all_gather([_pack([grads[n] for n in rep], F32, 8)], name='gather_replicated_grads')
    outs = _adam_replicated(gall, _pack([wl[n] for n in rep], F32, 8), _pack([ml[n] for n in rep], F32, 8),
                            _pack([vl[n] for n in rep], F32, 8), name='adamw_replicated')
    for kind, packed in zip(('grad', 'delta', 'new_m', 'new_v'), outs):
        for n, a in zip(rep, _unpack(packed, rshapes)):
            res[kind, n] = a

    return (loss, grad_x, *[res['grad', n] for n in WEIGHTS], *[res['delta', n] for n in WEIGHTS],
            *[res['new_m', n] for n in WEIGHTS], *[res['new_v', n] for n in WEIGHTS])
```

```python
import functools
import math

import jax
import jax.numpy as jnp
from jax import lax
from jax.experimental import pallas as pl
from jax.experimental.pallas import tpu as pltpu

F32 = jnp.float32
BF16 = jnp.bfloat16
MESH = pl.DeviceIdType.MESH

N_DEV = 8
LANES = 128
MM_TILES = (1024, 1408, 896, 512, 384, 256, 128)
MM_VMEM_BUDGET = 30 * 1024 * 1024
VMEM_LIMIT = 48 * 1024 * 1024
LN_EPS = 1e-5
RMS_EPS = 1e-6
SSD_HEAD_DIM = 64
SSD_N_GROUPS = 8
SSD_D_STATE = 128
SSD_CHUNK = 128
MLA_NOPE = 128
MLA_ROPE = 64
MLA_V = 128
ROPE_THETA = 10000.0
ADAM_LR = 0.001
ADAM_B1 = 0.9
ADAM_B2 = 0.999
ADAM_EPS = 1e-08
ADAM_WD = 0.01
ADAM_STEP = 10
NEG_BIG = -1e30

SHARDED = (('ssd_in_proj', 2), ('ssd_conv_w', 2), ('ssd_conv_b', 1), ('ssd_norm_g', 1), ('ssd_out_proj', 1),
           ('kv_down_proj', 0), ('kv_up_k', 1), ('kv_up_v', 1), ('q_down_proj', 1), ('q_up_proj', 2),
           ('attn_out_proj', 1), ('ffn_up', 2), ('ffn_conv_w', 2), ('ffn_down', 1))
SHARDED_F32 = ('ssd_conv_w', 'ssd_conv_b', 'ssd_norm_g', 'ffn_conv_w')
GATHERED_FIRST = ('ssd_in_proj', 'ssd_out_proj')
REDUCED_LAST = ('ssd_in_proj',)
REPLICATED = ('ssd_dt_bias', 'ssd_A_log', 'ssd_D', 'kv_norm_g', 'q_norm_g', 'ffn_conv_b',
              'ln_mix_g', 'ln_mix_b', 'ln_ffn_g', 'ln_ffn_b')
WEIGHTS = ('ssd_in_proj', 'ssd_conv_w', 'ssd_conv_b', 'ssd_dt_bias', 'ssd_A_log', 'ssd_D', 'ssd_norm_g',
           'ssd_out_proj', 'kv_down_proj', 'kv_norm_g', 'kv_up_k', 'kv_up_v', 'q_down_proj', 'q_norm_g',
           'q_up_proj', 'attn_out_proj', 'ffn_up', 'ffn_conv_w', 'ffn_conv_b', 'ffn_down', 'ln_mix_g',
           'ln_mix_b', 'ln_ffn_g', 'ln_ffn_b')


def _cparams():
    return pltpu.CompilerParams(vmem_limit_bytes=VMEM_LIMIT)


def _tile(n, cands):
    for c in cands:
        if n % c == 0:
            return c
    return n


def _sigmoid(x):
    return 1.0 / (1.0 + jnp.exp(-x))


def _iota(shape, axis):
    return lax.broadcasted_iota(jnp.int32, shape, axis)


def _mm(a, b, *, ta=False, tb=False, add=None, add_scale=1.0, out_dtype=F32, name, b_koff=0, into=None, into_col=0):
    if ta:
        K, M = a.shape
    else:
        M, K = a.shape
    if tb:
        N, Kb = b.shape
    else:
        Kb, N = b.shape
    assert b_koff + K <= Kb, (a.shape, b.shape, ta, tb, b_koff)
    tm = _tile(M, MM_TILES)
    tn = _tile(N, MM_TILES)
    tk = K if K <= MM_TILES[0] and b_koff == 0 and Kb == K else _tile(K, MM_TILES)
    nk = K // tk
    assert b_koff % tk == 0 and (into is None or (into.shape[0] == M and into.dtype == out_dtype))
    kb = b_koff // tk

    def vmem_estimate(tm_, tn_):
        ins = 2 * (tm_ * tk * a.dtype.itemsize + tk * tn_ * b.dtype.itemsize)
        outs = tm_ * tn_ * (2 * jnp.dtype(out_dtype).itemsize + 4 + (4 if nk > 1 else 0))
        return ins + outs + (2 * tm_ * tn_ * add.dtype.itemsize if add is not None else 0)

    while vmem_estimate(tm, tn) > MM_VMEM_BUDGET:
        can_m, can_n = tm % (2 * LANES) == 0, tn % (2 * LANES) == 0
        if can_m and (tm >= tn or not can_n):
            tm //= 2
        elif can_n:
            tn //= 2
        else:
            break

    assert into_col % tn == 0
    jb = into_col // tn

    def body(a_ref, b_ref, *rest):
        add_ref = rest[0] if add is not None else None
        o_ref = rest[(add is not None) + (into is not None)]
        acc = rest[-1] if nk > 1 else None
        k = pl.program_id(2)

        if ta:
            av = a_ref[...].astype(BF16).T
        else:
            av = a_ref[...].astype(BF16)
        bv = b_ref[...].astype(BF16)
        dn = (((1,), (1 if tb else 0,)), ((), ()))
        part = lax.dot_general(av, bv, dn, preferred_element_type=F32)

        def finish(r):
            if add is not None:
                r = r + add_scale * add_ref[...].astype(F32)
            o_ref[...] = r.astype(out_dtype)

        if nk == 1:
            finish(part)
        else:
            @pl.when(k == 0)
            def _():
                acc[...] = part

            @pl.when(k > 0)
            def _():
                acc[...] += part

            @pl.when(k == nk - 1)
            def _():
                finish(acc[...])

    a_spec = (pl.BlockSpec((tk, tm), lambda i, j, k: (k, i)) if ta
              else pl.BlockSpec((tm, tk), lambda i, j, k: (i, k)))
    b_spec = (pl.BlockSpec((tn, tk), lambda i, j, k: (j, k + kb)) if tb
              else pl.BlockSpec((tk, tn), lambda i, j, k: (k + kb, j)))
    in_specs = [a_spec, b_spec]
    args = [a, b]
    if add is not None:
        in_specs.append(pl.BlockSpec((tm, tn), lambda i, j, k: (i, j)))
        args.append(add)
    aliases = {}
    if into is not None:
        aliases = {len(args): 0}
        in_specs.append(pl.BlockSpec(memory_space=pl.ANY))
        args.append(into)
    return pl.pallas_call(
        body, name=name, grid=(M // tm, N // tn, nk),
        in_specs=in_specs, out_specs=pl.BlockSpec((tm, tn), lambda i, j, k: (i, j + jb)),
        out_shape=jax.ShapeDtypeStruct((M, N) if into is None else into.shape, out_dtype),
        scratch_shapes=[pltpu.VMEM((tm, tn), F32)] if nk > 1 else [],
        input_output_aliases=aliases, compiler_params=_cparams(),
    )(*args)


def _ln_fwd(h, mix, g, b, alpha, name):
    T, D = h.shape
    tm = _tile(T, (256, 128))

    def body(h_ref, m_ref, g_ref, b_ref, y_ref, s_ref, yb_ref):
        s = alpha * h_ref[...] + m_ref[...]
        mu = jnp.mean(s, axis=1, keepdims=True)
        xc = s - mu
        var = jnp.mean(xc * xc, axis=1, keepdims=True)
        y = xc * lax.rsqrt(var + LN_EPS) * g_ref[...] + b_ref[...]
        y_ref[...] = y
        s_ref[...] = s
        yb_ref[...] = y.astype(BF16)

    row = pl.BlockSpec((tm, D), lambda i: (i, 0))
    vec = pl.BlockSpec((1, D), lambda i: (0, 0))
    return pl.pallas_call(
        body, name=name, grid=(T // tm,), in_specs=[row, row, vec, vec], out_specs=[row, row, row],
        out_shape=[jax.ShapeDtypeStruct((T, D), F32)] * 2 + [jax.ShapeDtypeStruct((T, D), BF16)],
        compiler_params=_cparams(),
    )(h, mix, g, b)


def _ln_bwd(dy, s, g, name):
    T, D = s.shape
    tm = _tile(T, (256, 128))

    def body(dy_ref, s_ref, g_ref, ds_ref, dsb_ref, dg_ref, db_ref):
        @pl.when(pl.program_id(0) == 0)
        def _():
            dg_ref[...] = jnp.zeros_like(dg_ref)
            db_ref[...] = jnp.zeros_like(db_ref)

        sv = s_ref[...]
        dyv = dy_ref[...]
        mu = jnp.mean(sv, axis=1, keepdims=True)
        xc = sv - mu
        rstd = lax.rsqrt(jnp.mean(xc * xc, axis=1, keepdims=True) + LN_EPS)
        xhat = xc * rstd
        dxh = dyv * g_ref[...]
        m1 = jnp.mean(dxh, axis=1, keepdims=True)
        m2 = jnp.mean(dxh * xhat, axis=1, keepdims=True)
        ds = rstd * (dxh - m1 - xhat * m2)
        ds_ref[...] = ds
        dsb_ref[...] = ds.astype(BF16)
        dg_ref[...] += jnp.sum(dyv * xhat, axis=0, keepdims=True)
        db_ref[...] += jnp.sum(dyv, axis=0, keepdims=True)

    row = pl.BlockSpec((tm, D), lambda i: (i, 0))
    vec = pl.BlockSpec((1, D), lambda i: (0, 0))
    return pl.pallas_call(
        body, name=name, grid=(T // tm,), in_specs=[row, row, vec], out_specs=[row, row, vec, vec],
        out_shape=[jax.ShapeDtypeStruct((T, D), F32), jax.ShapeDtypeStruct((T, D), BF16),
                   jax.ShapeDtypeStruct((1, D), F32), jax.ShapeDtypeStruct((1, D), F32)],
        compiler_params=_cparams(),
    )(dy, s, g)


def _loss_head(y, target, name):
    T, D = y.shape
    tm = _tile(T, (256, 128))

    def body(y_ref, t_ref, sq_ref, dy_ref):
        @pl.when(pl.program_id(0) == 0)
        def _():
            sq_ref[...] = jnp.zeros_like(sq_ref)

        e = y_ref[...] - t_ref[...]
        sq_ref[...] += jnp.sum(e * e, axis=0, keepdims=True)
        dy_ref[...] = e * (1.0 / D)

    row = pl.BlockSpec((tm, D), lambda i: (i, 0))
    vec = pl.BlockSpec((1, D), lambda i: (0, 0))
    return pl.pallas_call(
        body, name=name, grid=(T // tm,), in_specs=[row, row], out_specs=[vec, row],
        out_shape=[jax.ShapeDtypeStruct((1, D), F32), jax.ShapeDtypeStruct((T, D), F32)],
        compiler_params=_cparams(),
    )(y, target)


def _shift_down(x, j):
    if j == 0:
        return x
    return jnp.where(_iota(x.shape, 0) >= j, pltpu.roll(x, j, 0), 0.0)


def _shift_up(x, j):
    if j == 0:
        return x
    n = x.shape[0]
    return jnp.where(_iota(x.shape, 0) < n - j, pltpu.roll(x, n - j, 0), 0.0)


def _conv_val(x, w_ref, b_ref):
    width = w_ref.shape[0]
    y = b_ref[...] + w_ref[width - 1:width, :] * x
    for j in range(1, width):
        y = y + w_ref[width - 1 - j:width - j, :] * _shift_down(x, j)
    return y


def _conv_bwd_val(x, dc, w_ref, dw_ref, db_ref):
    width = w_ref.shape[0]
    dx = w_ref[width - 1:width, :] * dc
    dw_ref[width - 1:width, :] += jnp.sum(dc * x, axis=0, keepdims=True)
    for j in range(1, width):
        sj = _shift_up(dc, j)
        dx = dx + w_ref[width - 1 - j:width - j, :] * sj
        dw_ref[width - 1 - j:width - j, :] += jnp.sum(sj * x, axis=0, keepdims=True)
    db_ref[...] += jnp.sum(dc, axis=0, keepdims=True)
    return dx


def _ffn_specs(H, S, width):
    nC = H // LANES
    return dict(
        g=pl.BlockSpec((S, LANES), lambda j, b: (b, j)), v=pl.BlockSpec((S, LANES), lambda j, b: (b, j + nC)),
        wg=pl.BlockSpec((width, LANES), lambda j, b: (0, j)), wv=pl.BlockSpec((width, LANES), lambda j, b: (0, j + nC)),
        bg=pl.BlockSpec((1, LANES), lambda j, b: (0, j)), bv=pl.BlockSpec((1, LANES), lambda j, b: (0, j + nC)))


def _ffn_act_fwd(u, cw, cb, S, name):
    T, H2 = u.shape
    H = H2 // 2
    sp = _ffn_specs(H, S, cw.shape[0])

    def body(ug_ref, uv_ref, wg_ref, wv_ref, bg_ref, bv_ref, a_ref, cg_ref, cv_ref):
        g = _conv_val(ug_ref[...], wg_ref, bg_ref)
        v = _conv_val(uv_ref[...], wv_ref, bv_ref)
        cg_ref[...] = g
        cv_ref[...] = v
        a_ref[...] = (g * _sigmoid(g) * v).astype(BF16)

    return pl.pallas_call(
        body, name=name, grid=(H // LANES, T // S),
        in_specs=[sp['g'], sp['v'], sp['wg'], sp['wv'], sp['bg'], sp['bv']], out_specs=[sp['g']] * 3,
        out_shape=[jax.ShapeDtypeStruct((T, H), BF16), jax.ShapeDtypeStruct((T, H), F32),
                   jax.ShapeDtypeStruct((T, H), F32)],
        compiler_params=_cparams(),
    )(u, u, cw, cw, cb, cb)


def _ffn_act_bwd(u, cg, cv, da, cw, S, name):
    T, H = cg.shape
    width = cw.shape[0]
    sp = _ffn_specs(H, S, width)

    def body(ug_ref, uv_ref, cg_ref, cv_ref, da_ref, wg_ref, wv_ref,
             dug_ref, duv_ref, dwg_ref, dwv_ref, dbg_ref, dbv_ref):
        @pl.when(pl.program_id(1) == 0)
        def _():
            for r in (dwg_ref, dwv_ref, dbg_ref, dbv_ref):
                r[...] = jnp.zeros_like(r)

        g = cg_ref[...]
        dav = da_ref[...].astype(F32)
        sg = _sigmoid(g)
        dg = dav * cv_ref[...] * sg * (1.0 + g * (1.0 - sg))
        dv = dav * g * sg
        dug_ref[...] = _conv_bwd_val(ug_ref[...], dg, wg_ref, dwg_ref, dbg_ref).astype(BF16)
        duv_ref[...] = _conv_bwd_val(uv_ref[...], dv, wv_ref, dwv_ref, dbv_ref).astype(BF16)

    act = jax.ShapeDtypeStruct((T, H), BF16)
    wsh = jax.ShapeDtypeStruct((width, H), F32)
    bsh = jax.ShapeDtypeStruct((1, H), F32)
    return pl.pallas_call(
        body, name=name, grid=(H // LANES, T // S),
        in_specs=[sp['g'], sp['v'], sp['g'], sp['g'], sp['g'], sp['wg'], sp['wv']],
        out_specs=[sp['g'], sp['g'], sp['wg'], sp['wg'], sp['bg'], sp['bg']],
        out_shape=[act, act, wsh, wsh, bsh, bsh], compiler_params=_cparams(),
    )(u, u, cg, cv, da, cw, cw)


def _ssd_conv_fwd(x, cw, cb, S, name):
    T, C = x.shape
    Cb = _tile(C, (256, 128))
    width = cw.shape[0]

    def body(x_ref, w_ref, b_ref, y_ref, c_ref):
        c = _conv_val(x_ref[...], w_ref, b_ref)
        c_ref[...] = c
        y_ref[...] = c * _sigmoid(c)

    blk = pl.BlockSpec((S, Cb), lambda j, b: (b, j))
    wblk = pl.BlockSpec((width, Cb), lambda j, b: (0, j))
    bblk = pl.BlockSpec((1, Cb), lambda j, b: (0, j))
    return pl.pallas_call(
        body, name=name, grid=(C // Cb, T // S), in_specs=[blk, wblk, bblk], out_specs=[blk, blk],
        out_shape=[jax.ShapeDtypeStruct((T, C), F32)] * 2, compiler_params=_cparams(),
    )(x, cw, cb)


def _ssd_conv_bwd(x, c, dy, cw, S, name):
    T, C = x.shape
    Cb = _tile(C, (256, 128))
    width = cw.shape[0]

    def body(x_ref, c_ref, dy_ref, w_ref, dx_ref, dw_ref, db_ref):
        @pl.when(pl.program_id(1) == 0)
        def _():
            dw_ref[...] = jnp.zeros_like(dw_ref)
            db_ref[...] = jnp.zeros_like(db_ref)

        cval = c_ref[...]
        sc = _sigmoid(cval)
        dc = dy_ref[...] * sc * (1.0 + cval * (1.0 - sc))
        dx_ref[...] = _conv_bwd_val(x_ref[...], dc, w_ref, dw_ref, db_ref).astype(BF16)

    blk = pl.BlockSpec((S, Cb), lambda j, b: (b, j))
    wblk = pl.BlockSpec((width, Cb), lambda j, b: (0, j))
    bblk = pl.BlockSpec((1, Cb), lambda j, b: (0, j))
    return pl.pallas_call(
        body, name=name, grid=(C // Cb, T // S), in_specs=[blk, blk, blk, wblk],
        out_specs=[blk, wblk, bblk],
        out_shape=[jax.ShapeDtypeStruct((T, C), BF16), jax.ShapeDtypeStruct((width, C), F32),
                   jax.ShapeDtypeStruct((1, C), F32)],
        compiler_params=_cparams(),
    )(x, c, dy, cw)


def _softplus(x):
    e = jnp.exp(-jnp.abs(x))
    return jnp.maximum(x, 0.0) + jnp.where(e < 1e-4, e * (1.0 - 0.5 * e), jnp.log(1.0 + e))


def _cumsum_rows(x):
    n = x.shape[0]
    row = _iota(x.shape, 0)
    k = 1
    while k < n:
        x = x + jnp.where(row >= k, pltpu.roll(x, k, 0), 0.0)
        k *= 2
    return x


def _rev_cumsum_rows(x):
    n = x.shape[0]
    row = _iota(x.shape, 0)
    k = 1
    while k < n:
        x = x + jnp.where(row < n - k, pltpu.roll(x, n - k, 0), 0.0)
        k *= 2
    return x


def _col(x, lane_ids, h):
    return jnp.sum(jnp.where(lane_ids == h, x, 0.0), axis=1, keepdims=True)


def _bdot(a, b, dims):
    return lax.dot_general(a.astype(BF16), b.astype(BF16), (dims, ((), ())), preferred_element_type=F32)


NN = ((1,), (0,))
NT = ((1,), (1,))


def _scan_specs(nc, d_inner, hpg):
    L = SSD_CHUNK
    G = SSD_N_GROUPS
    gw = hpg * SSD_HEAD_DIM
    nb = d_inner // LANES
    return dict(
        xs=pl.BlockSpec((L, gw), lambda b, g, c: (b * nc + c, g)),
        B=pl.BlockSpec((L, LANES), lambda b, g, c: (b * nc + c, nb + g)),
        C=pl.BlockSpec((L, LANES), lambda b, g, c: (b * nc + c, nb + G + g)),
        dt=pl.BlockSpec((L, LANES), lambda b, g, c: (b * nc + c, 0)),
        vec=pl.BlockSpec((1, LANES), lambda b, g, c: (0, 0)),
        gvec=pl.BlockSpec((1, gw), lambda b, g, c: (0, g)),
        grp=pl.BlockSpec((L, LANES), lambda b, g, c: (b * nc + c, g)),
        st=pl.BlockSpec((1, hpg // 2, SSD_D_STATE, LANES), lambda b, g, c: ((b * G + g) * nc + c, 0, 0, 0)),
    )


def _grid_step(dims):
    t, total = 0, 1
    for ax, d in enumerate(dims):
        t = t * d + pl.program_id(ax)
        total *= d
    return t, total


def _scan_fwd(xbc, dt_pre, dt_bias, a_log, S, d_inner, name, gather=()):
    T = xbc.shape[0]
    Bn = T // S
    L = SSD_CHUNK
    G = SSD_N_GROUPS
    nc = S // L
    hpg = d_inner // SSD_HEAD_DIM // G
    pairs = hpg // 2
    sp = _scan_specs(nc, d_inner, hpg)
    ng = len(gather)

    def body(*refs):
        xs_ref, b_ref, c_ref, dtp_ref, bias_ref, alog_ref = refs[:6]
        y_ref, st_ref = refs[6 + ng:8 + ng]
        state = refs[8 + 2 * ng]
        g = pl.program_id(1)

        if ng:
            start, forward, finish = _gather_plan(refs[6:6 + ng], refs[8 + ng:8 + 2 * ng], *refs[9 + 2 * ng:])
            t, total = _grid_step((Bn, G, nc))
            pl.when(t == 0)(start)
            pl.when(t == total // 2)(forward)

        @pl.when(pl.program_id(2) == 0)
        def _():
            state[...] = jnp.zeros_like(state)

        dt = _softplus(dtp_ref[...] + bias_ref[...])
        cum = _cumsum_rows(dt * (-jnp.exp(alog_ref[...])))
        Bm = b_ref[...]
        Cm = c_ref[...]
        Gm = _bdot(Cm, Bm, NT)
        tril = _iota((L, L), 1) <= _iota((L, L), 0)
        lane = _iota((L, LANES), 1)
        lane1 = _iota((1, LANES), 1)
        last = _iota((L, 1), 0) == L - 1
        BmT = Bm.T
        for p in range(pairs):
            S_in = state[p]
            st_ref[0, p] = S_in
            x_pair = xs_ref[:, p * LANES:(p + 1) * LANES]
            y_pair = jnp.zeros((L, LANES), F32)
            z_pair = jnp.zeros((L, LANES), F32)
            e_pair = jnp.zeros((L, LANES), F32)
            el_pair = jnp.zeros((1, LANES), F32)
            for k in range(2):
                h = g * hpg + 2 * p + k
                cum_col = _col(cum, lane, h)
                dt_col = _col(dt, lane, h)
                cb = jnp.broadcast_to(cum_col, (L, L))
                decay = jnp.exp(jnp.where(tril, cb - cb.T, NEG_BIG))
                mk = (lane >= SSD_HEAD_DIM) if k else (lane < SSD_HEAD_DIM)
                xd = jnp.where(mk, x_pair * dt_col, 0.0)
                y_pair = y_pair + _bdot(Gm * decay, xd, NN)
                cum_l = jnp.sum(jnp.where(last, cum_col, 0.0), axis=0, keepdims=True)
                e_pair = jnp.where(mk, jnp.exp(cum_col), e_pair)
                z_pair = z_pair + xd * jnp.exp(cum_l - cum_col)
                el_pair = jnp.where((lane1 >= SSD_HEAD_DIM) if k else (lane1 < SSD_HEAD_DIM), jnp.exp(cum_l), el_pair)
            y_pair = y_pair + e_pair * _bdot(Cm, S_in, NN)
            state[p] = S_in * el_pair + _bdot(BmT, z_pair, NN)
            y_ref[:, p * LANES:(p + 1) * LANES] = y_pair

        if ng:
            pl.when(t == total - 1)(finish)

    outs = pl.pallas_call(
        body, name=name, grid=(Bn, G, nc),
        in_specs=[sp['xs'], sp['B'], sp['C'], sp['dt'], sp['vec'], sp['vec']] + [ANY] * ng,
        out_specs=[sp['xs'], sp['st']] + [ANY] * ng,
        out_shape=[jax.ShapeDtypeStruct((T, d_inner), F32),
                   jax.ShapeDtypeStruct((Bn * G * nc, pairs, SSD_D_STATE, LANES), F32)] + _gather_shapes(gather),
        scratch_shapes=[pltpu.VMEM((pairs, SSD_D_STATE, LANES), F32)] + (_gather_sems(ng) if ng else []),
        compiler_params=_cparams(),
    )(xbc, xbc, xbc, dt_pre, dt_bias, a_log, *gather)
    return outs[0], outs[1], list(outs[2:])


def _scan_bwd(xbc, dt_pre, dt_bias, a_log, states, dy, d_lane, S, d_inner, name, exchange=()):
    T = xbc.shape[0]
    Bn = T // S
    L = SSD_CHUNK
    G = SSD_N_GROUPS
    nc = S // L
    hpg = d_inner // SSD_HEAD_DIM // G
    pairs = hpg // 2
    sp = _scan_specs(nc, d_inner, hpg)
    nx = len(exchange)

    def rev(spec):
        im = spec.index_map
        return pl.BlockSpec(spec.block_shape, lambda b, g, c: im(b, g, nc - 1 - c))

    def body(*refs):
        xs_ref, b_ref, c_ref, dtp_ref, bias_ref, alog_ref, st_ref, dy_ref, dl_ref = refs[:9]
        dxs_ref, db_ref, dc_ref, ddt_ref, da_ref = refs[9 + nx:14 + nx]
        dstate = refs[14 + 2 * nx]
        g = pl.program_id(1)

        if nx:
            start, finish = _chips_plan(refs[9:9 + nx], refs[14 + nx:14 + 2 * nx], *refs[15 + 2 * nx:])
            t, total = _grid_step((Bn, G, nc))
            pl.when(t == 0)(start)

        @pl.when(pl.program_id(2) == 0)
        def _():
            dstate[...] = jnp.zeros_like(dstate)

        dt = _softplus(dtp_ref[...] + bias_ref[...])
        cum = _cumsum_rows(dt * (-jnp.exp(alog_ref[...])))
        Bm = b_ref[...]
        Cm = c_ref[...]
        Gm = _bdot(Cm, Bm, NT)
        tril = _iota((L, L), 1) <= _iota((L, L), 0)
        lane = _iota((L, LANES), 1)
        lane1 = _iota((1, LANES), 1)
        last = _iota((L, 1), 0) == L - 1
        CmT = Cm.T
        dG = jnp.zeros((L, L), F32)
        dB = jnp.zeros((L, LANES), F32)
        dC = jnp.zeros((L, LANES), F32)
        dcum = jnp.zeros((L, LANES), F32)
        ddt = jnp.zeros((L, LANES), F32)
        for p in range(pairs):
            S_in = st_ref[0, p]
            dS = dstate[p]
            x_pair = xs_ref[:, p * LANES:(p + 1) * LANES]
            dy_pair = dy_ref[:, p * LANES:(p + 1) * LANES]
            Q = _bdot(Cm, S_in, NN)
            dZ = _bdot(Bm, dS, NN)
            prod = dS * S_in
            e_pair = jnp.zeros((L, LANES), F32)
            z_pair = jnp.zeros((L, LANES), F32)
            el_pair = jnp.zeros((1, LANES), F32)
            dx_pair = dl_ref[:, p * LANES:(p + 1) * LANES] * dy_pair
            for k in range(2):
                h = g * hpg + 2 * p + k
                cum_col = _col(cum, lane, h)
                dt_col = _col(dt, lane, h)
                cb = jnp.broadcast_to(cum_col, (L, L))
                decay = jnp.exp(jnp.where(tril, cb - cb.T, NEG_BIG))
                mk = (lane >= SSD_HEAD_DIM) if k else (lane < SSD_HEAD_DIM)
                mk1 = (lane1 >= SSD_HEAD_DIM) if k else (lane1 < SSD_HEAD_DIM)
                xd = jnp.where(mk, x_pair * dt_col, 0.0)
                dy_k = jnp.where(mk, dy_pair, 0.0)
                Wm = Gm * decay
                dWm = jnp.where(tril, _bdot(dy_k, xd, NT), 0.0)
                dxd = _bdot(Wm.T, dy_k, NN)
                dseg = dWm * Wm
                dG = dG + dWm * decay
                dcum_col = (jnp.sum(dseg, axis=1, keepdims=True)
                            - jnp.sum(dseg.T, axis=1, keepdims=True))
                cum_l = jnp.sum(jnp.where(last, cum_col, 0.0), axis=0, keepdims=True)
                e_col = jnp.exp(cum_col)
                w_col = jnp.exp(cum_l - cum_col)
                el = jnp.exp(cum_l)
                dcum_col = dcum_col + jnp.sum(dy_k * Q, axis=1, keepdims=True) * e_col
                de_e = jnp.sum(dZ * xd, axis=1, keepdims=True) * w_col
                dcum_col = dcum_col - de_e
                dcum_l = (jnp.sum(de_e, axis=0, keepdims=True)
                          + el * jnp.sum(jnp.sum(jnp.where(mk, prod, 0.0), axis=1, keepdims=True),
                                         axis=0, keepdims=True))
                dcum_col = dcum_col + jnp.where(last, dcum_l, 0.0)
                dxd = dxd + jnp.where(mk, dZ * w_col, 0.0)
                dx_pair = dx_pair + dxd * dt_col
                ddt = jnp.where(lane == h, jnp.sum(dxd * x_pair, axis=1, keepdims=True), ddt)
                dcum = jnp.where(lane == h, dcum_col, dcum)
                e_pair = jnp.where(mk, e_col, e_pair)
                z_pair = z_pair + xd * w_col
                el_pair = jnp.where(mk1, el, el_pair)
            dQ = e_pair * dy_pair
            dC = dC + _bdot(dQ, S_in, NT)
            dB = dB + _bdot(z_pair, dS, NT)
            dstate[p] = dS * el_pair + _bdot(CmT, dQ, NN)
            dxs_ref[:, p * LANES:(p + 1) * LANES] = dx_pair
        dc_ref[...] = dC + _bdot(dG, Bm, NN)
        db_ref[...] = dB + _bdot(dG.T, Cm, NN)
        ddt_ref[...] = ddt
        da_ref[...] = _rev_cumsum_rows(dcum)

        if nx:
            pl.when(t == total - 1)(finish)

    grp = jax.ShapeDtypeStruct((T, G * LANES), F32)
    outs = pl.pallas_call(
        body, name=name, grid=(Bn, G, nc),
        in_specs=[rev(sp['xs']), rev(sp['B']), rev(sp['C']), rev(sp['dt']), sp['vec'], sp['vec'],
                  rev(sp['st']), rev(sp['xs']), sp['gvec']] + [ANY] * nx,
        out_specs=[rev(sp['xs']), rev(sp['grp']), rev(sp['grp']), rev(sp['grp']), rev(sp['grp'])] + [ANY] * nx,
        out_shape=[jax.ShapeDtypeStruct((T, d_inner), F32), grp, grp, grp, grp] + _chips_shapes(exchange),
        scratch_shapes=[pltpu.VMEM((pairs, SSD_D_STATE, LANES), F32)] + (_chips_sems(nx) if nx else []),
        compiler_params=_cparams(),
    )(xbc, xbc, xbc, dt_pre, dt_bias, a_log, states, dy, d_lane, *exchange)
    return outs[:5], list(outs[5:])


def _dt_bwd(ddt_g, da_g, dt_pre, dt_bias, a_log, name):
    T = dt_pre.shape[0]
    G = SSD_N_GROUPS
    tm = _tile(T, (512, 256, 128))

    def body(ddt_ref, da_ref, dtp_ref, bias_ref, alog_ref, dx_ref, dbias_ref, dal_ref):
        @pl.when(pl.program_id(0) == 0)
        def _():
            dbias_ref[...] = jnp.zeros_like(dbias_ref)
            dal_ref[...] = jnp.zeros_like(dal_ref)

        ddt = ddt_ref[:, 0:LANES]
        da = da_ref[:, 0:LANES]
        for gi in range(1, G):
            ddt = ddt + ddt_ref[:, gi * LANES:(gi + 1) * LANES]
            da = da + da_ref[:, gi * LANES:(gi + 1) * LANES]
        xv = dtp_ref[...] + bias_ref[...]
        A = -jnp.exp(alog_ref[...])
        dx = (ddt + da * A) * _sigmoid(xv)
        dx_ref[...] = dx.astype(BF16)
        dbias_ref[...] += jnp.sum(dx, axis=0, keepdims=True)
        dal_ref[...] += jnp.sum(da * _softplus(xv), axis=0, keepdims=True) * A

    wide = pl.BlockSpec((tm, G * LANES), lambda i: (i, 0))
    row = pl.BlockSpec((tm, LANES), lambda i: (i, 0))
    vec = pl.BlockSpec((1, LANES), lambda i: (0, 0))
    vsh = jax.ShapeDtypeStruct((1, LANES), F32)
    return pl.pallas_call(
        body, name=name, grid=(T // tm,), in_specs=[wide, wide, row, vec, vec], out_specs=[row, vec, vec],
        out_shape=[jax.ShapeDtypeStruct((T, LANES), BF16), vsh, vsh], compiler_params=_cparams(),
    )(ddt_g, da_g, dt_pre, dt_bias, a_log)


def _gate_fwd(y, xbc, z, d_lane, ng, d_inner, name):
    T = y.shape[0]
    G = SSD_N_GROUPS
    gw = d_inner // G
    tm = _tile(T, (512, 256, 128))

    def body(y_ref, x_ref, z_ref, dl_ref, ng_ref, o_ref):
        zv = z_ref[...]
        y2 = (y_ref[...] + dl_ref[...] * x_ref[...]) * (zv * _sigmoid(zv))
        r = lax.rsqrt(jnp.mean(y2 * y2, axis=1, keepdims=True) + LN_EPS)
        o_ref[...] = (y2 * r * ng_ref[...]).astype(BF16)

    blk = pl.BlockSpec((tm, gw), lambda g, i: (i, g))
    vec = pl.BlockSpec((1, gw), lambda g, i: (0, g))
    return pl.pallas_call(
        body, name=name, grid=(G, T // tm), in_specs=[blk, blk, blk, vec, vec], out_specs=blk,
        out_shape=jax.ShapeDtypeStruct((T, d_inner), BF16), compiler_params=_cparams(),
    )(y, xbc, z, d_lane, ng)


def _gate_bwd(dyn, y, xbc, z, d_lane, ng, d_inner, name):
    T = y.shape[0]
    G = SSD_N_GROUPS
    gw = d_inner // G
    tm = _tile(T, (512, 256, 128))

    def body(dyn_ref, y_ref, x_ref, z_ref, dl_ref, ng_ref, dy_ref, dz_ref, dng_ref, ddl_ref):
        @pl.when(pl.program_id(1) == 0)
        def _():
            dng_ref[...] = jnp.zeros_like(dng_ref)
            ddl_ref[...] = jnp.zeros_like(ddl_ref)

        zv = z_ref[...]
        xv = x_ref[...]
        sz = _sigmoid(zv)
        y1 = y_ref[...] + dl_ref[...] * xv
        y2 = y1 * (zv * sz)
        r = lax.rsqrt(jnp.mean(y2 * y2, axis=1, keepdims=True) + LN_EPS)
        dn = dyn_ref[...].astype(F32)
        dng_ref[...] += jnp.sum(dn * y2 * r, axis=0, keepdims=True)
        do = dn * ng_ref[...]
        dy2 = r * do - y2 * (r * r * r) * jnp.mean(do * y2, axis=1, keepdims=True)
        dz_ref[...] = (dy2 * y1 * sz * (1.0 + zv * (1.0 - sz))).astype(BF16)
        dy1 = dy2 * (zv * sz)
        dy_ref[...] = dy1
        ddl_ref[...] += jnp.sum(dy1 * xv, axis=0, keepdims=True)

    blk = pl.BlockSpec((tm, gw), lambda g, i: (i, g))
    vec = pl.BlockSpec((1, gw), lambda g, i: (0, g))
    vsh = jax.ShapeDtypeStruct((1, d_inner), F32)
    return pl.pallas_call(
        body, name=name, grid=(G, T // tm), in_specs=[blk, blk, blk, blk, vec, vec],
        out_specs=[blk, blk, vec, vec],
        out_shape=[jax.ShapeDtypeStruct((T, d_inner), F32), jax.ShapeDtypeStruct((T, d_inner), BF16), vsh, vsh],
        compiler_params=_cparams(),
    )(dyn, y, xbc, z, d_lane, ng)


def _rms_fwd(x, g, name):
    T, R = x.shape
    tm = _tile(T, (512, 256, 128))

    def body(x_ref, g_ref, y_ref):
        xv = x_ref[...]
        y = xv * lax.rsqrt(jnp.mean(xv * xv, axis=1, keepdims=True) + RMS_EPS) * g_ref[...]
        y_ref[...] = y.astype(BF16)

    row = pl.BlockSpec((tm, R), lambda i: (i, 0))
    vec = pl.BlockSpec((1, R), lambda i: (0, 0))
    return pl.pallas_call(
        body, name=name, grid=(T // tm,), in_specs=[row, vec], out_specs=row,
        out_shape=jax.ShapeDtypeStruct((T, R), BF16), compiler_params=_cparams(),
    )(x, g)


def _rms_bwd(dy, x, g, name):
    T, R = x.shape
    tm = _tile(T, (512, 256, 128))

    def body(dy_ref, x_ref, g_ref, dx_ref, dg_ref):
        @pl.when(pl.program_id(0) == 0)
        def _():
            dg_ref[...] = jnp.zeros_like(dg_ref)

        xv = x_ref[...]
        dyv = dy_ref[...]
        r = lax.rsqrt(jnp.mean(xv * xv, axis=1, keepdims=True) + RMS_EPS)
        dg_ref[...] += jnp.sum(dyv * xv * r, axis=0, keepdims=True)
        do = dyv * g_ref[...]
        dx = r * do - xv * (r * r * r) * jnp.mean(do * xv, axis=1, keepdims=True)
        dx_ref[...] = dx.astype(BF16)

    row = pl.BlockSpec((tm, R), lambda i: (i, 0))
    vec = pl.BlockSpec((1, R), lambda i: (0, 0))
    return pl.pallas_call(
        body, name=name, grid=(T // tm,), in_specs=[row, row, vec], out_specs=[row, vec],
        out_shape=[jax.ShapeDtypeStruct((T, R), BF16), jax.ShapeDtypeStruct((1, R), F32)],
        compiler_params=_cparams(),
    )(dy, x, g)


def _swap_halves(x):
    half = MLA_ROPE // 2
    return jnp.where(_iota(x.shape, 1) < half, pltpu.roll(x, LANES - half, 1), pltpu.roll(x, half, 1))


def _rope(x, cc, ss, *, transpose, sum_heads=False, name):
    T, W = x.shape
    nh = W // LANES
    tm = _tile(T, (512, 256, 128))
    n_out = 1 if sum_heads else nh

    def body(x_ref, cc_ref, ss_ref, o_ref):
        ccv = cc_ref[...]
        ssv = ss_ref[...]
        if sum_heads:
            xv = x_ref[:, 0:LANES]
            for hh in range(1, nh):
                xv = xv + x_ref[:, hh * LANES:(hh + 1) * LANES]
            heads = [xv]
        else:
            heads = [x_ref[:, hh * LANES:(hh + 1) * LANES] for hh in range(nh)]
        for hh, xv in enumerate(heads):
            if transpose:
                r = xv * ccv + _swap_halves(xv * ssv)
            else:
                r = xv * ccv + _swap_halves(xv) * ssv
            r = jnp.where(_iota(r.shape, 1) < MLA_ROPE, r, 0.0)
            o_ref[:, hh * LANES:(hh + 1) * LANES] = r.astype(BF16)

    return pl.pallas_call(
        body, name=name, grid=(T // tm,),
        in_specs=[pl.BlockSpec((tm, W), lambda i: (i, 0)), pl.BlockSpec((tm, LANES), lambda i: (i, 0)),
                  pl.BlockSpec((tm, LANES), lambda i: (i, 0))],
        out_specs=pl.BlockSpec((tm, n_out * LANES), lambda i: (i, 0)),
        out_shape=jax.ShapeDtypeStruct((T, n_out * LANES), BF16), compiler_params=_cparams(),
    )(x, cc, ss)


ATT_BLOCK = 512
ATT_SUB = 256
LOG2E = 1.4426950408889634


def _att_geometry(S):
    tb = _tile(S, (ATT_BLOCK, 256))
    return tb, S // tb, tb // ATT_SUB


def _heads_per_step(nh):
    return 2 if nh % 2 == 0 else 1


def _att_scores(a, b, diag, kv_major, off):
    s = _bdot(a, b, NT)
    if diag:
        q_ax, k_ax = (1, 0) if kv_major else (0, 1)
        s = jnp.where(_iota(s.shape, k_ax) <= _iota(s.shape, q_ax) + off, s, NEG_BIG)
    return s


def _attention_fwd(qn, qr, kn, kr, v, S, scale, name):
    T, W = qn.shape
    nh = W // LANES
    Bn = T // S
    tb, n, C = _att_geometry(S)
    pairs = [(i, j) for i in range(n) for j in range(i + 1)]
    it = jnp.asarray([p[0] for p in pairs], jnp.int32)
    jt = jnp.asarray([p[1] for p in pairs], jnp.int32)
    c2 = scale * LOG2E

    HP = _heads_per_step(nh)

    def body(it_ref, jt_ref, qn_ref, qr_ref, kn_ref, kr_ref, v_ref, o_ref, ob_ref, lse_ref, m_sc, l_sc, acc):
        p = pl.program_id(2)
        i = it_ref[p]
        j = jt_ref[p]

        @pl.when(j == 0)
        def _():
            m_sc[...] = jnp.full_like(m_sc, NEG_BIG)
            l_sc[...] = jnp.zeros_like(l_sc)
            acc[...] = jnp.zeros_like(acc)

        def step(diag):
            for hh in range(HP):
                hs = pl.ds(hh * LANES, LANES)
                qc = jnp.concatenate([qn_ref[:, hs], qr_ref[:, hs]], axis=1)
                kc = jnp.concatenate([kn_ref[:, hs], kr_ref[...]], axis=1)
                s = _att_scores(qc, kc, diag, False, 0)
                m_old = m_sc[hh]
                m_new = jnp.maximum(m_old, jnp.max(s, axis=1, keepdims=True))
                pr = jnp.exp2((s - m_new) * c2)
                corr = jnp.exp2((m_old - m_new) * c2)
                l_sc[hh] = corr * l_sc[hh] + jnp.sum(pr, axis=1, keepdims=True)
                acc[:, hs] = corr * acc[:, hs] + _bdot(pr, v_ref[:, hs], NN)
                m_sc[hh] = m_new

        @pl.when(j < i)
        def _():
            step(False)

        @pl.when(j == i)
        def _():
            step(True)
            for hh in range(HP):
                hs = pl.ds(hh * LANES, LANES)
                ov = acc[:, hs] / l_sc[hh]
                o_ref[:, hs] = ov
                ob_ref[:, hs] = ov.astype(BF16)
                lse_ref[:, hs] = jnp.broadcast_to(m_sc[hh] * scale + jnp.log(l_sc[hh]), (tb, LANES))

    qspec = pl.BlockSpec((tb, HP * LANES), lambda b, h, p, it_, jt_: (b * n + it_[p], h))
    kspec = pl.BlockSpec((tb, HP * LANES), lambda b, h, p, it_, jt_: (b * n + jt_[p], h))
    krspec = pl.BlockSpec((tb, LANES), lambda b, h, p, it_, jt_: (b * n + jt_[p], 0))
    return pl.pallas_call(
        body, name=name,
        grid_spec=pltpu.PrefetchScalarGridSpec(
            num_scalar_prefetch=2, grid=(Bn, nh // HP, len(pairs)),
            in_specs=[qspec, qspec, kspec, krspec, kspec], out_specs=[qspec, qspec, qspec],
            scratch_shapes=[pltpu.VMEM((HP, tb, 1), F32), pltpu.VMEM((HP, tb, 1), F32),
                            pltpu.VMEM((tb, HP * LANES), F32)]),
        out_shape=[jax.ShapeDtypeStruct((T, W), F32), jax.ShapeDtypeStruct((T, W), BF16),
                   jax.ShapeDtypeStruct((T, W), F32)],
        compiler_params=_cparams(),
    )(it, jt, qn, qr, kn, kr, v)


def _attention_dq(qn, qr, kn, kr, v, o, do, lse, S, scale, name):
    T, W = qn.shape
    nh = W // LANES
    Bn = T // S
    tb, n, C = _att_geometry(S)
    pairs = [(i, j) for i in range(n) for j in range(i + 1)]
    it = jnp.asarray([p[0] for p in pairs], jnp.int32)
    jt = jnp.asarray([p[1] for p in pairs], jnp.int32)
    c2 = scale * LOG2E
    HP = _heads_per_step(nh)

    def body(it_ref, jt_ref, qn_ref, qr_ref, kn_ref, kr_ref, v_ref, o_ref, do_ref, lse_ref,
             dqn_ref, dqr_ref, acc, di_sc, lse_sc):
        p = pl.program_id(2)
        i = it_ref[p]
        j = jt_ref[p]

        @pl.when(j == 0)
        def _():
            acc[...] = jnp.zeros_like(acc)
            for hh in range(HP):
                hs = pl.ds(hh * LANES, LANES)
                di_sc[hh] = jnp.sum(do_ref[:, hs] * o_ref[:, hs], axis=1, keepdims=True)
                lse_sc[hh] = jnp.max(lse_ref[:, hs], axis=1, keepdims=True) * LOG2E

        def step(diag):
            for hh in range(HP):
                hs = pl.ds(hh * LANES, LANES)
                qc = jnp.concatenate([qn_ref[:, hs], qr_ref[:, hs]], axis=1)
                for c in range(C):
                    r0 = c * ATT_SUB if diag else 0
                    rows = pl.ds(r0, tb - r0)
                    ks = pl.ds(c * ATT_SUB, ATT_SUB)
                    kc = jnp.concatenate([kn_ref[ks, hs], kr_ref[ks, :]], axis=1)
                    s = _att_scores(qc[r0:], kc, diag, False, 0)
                    pr = jnp.exp2(s * c2 - lse_sc[hh, rows, :])
                    dp = _bdot(do_ref[rows, hs], v_ref[ks, hs], NT)
                    ds = pr * (dp - di_sc[hh, rows, :]) * scale
                    acc[hh, rows, :] += _bdot(ds, kc, NN)

        @pl.when(j < i)
        def _():
            step(False)

        @pl.when(j == i)
        def _():
            step(True)
            for hh in range(HP):
                hs = pl.ds(hh * LANES, LANES)
                dqn_ref[:, hs] = acc[hh, :, 0:LANES].astype(BF16)
                dqr_ref[:, hs] = acc[hh, :, LANES:2 * LANES]

    qspec = pl.BlockSpec((tb, HP * LANES), lambda b, h, p, it_, jt_: (b * n + it_[p], h))
    kspec = pl.BlockSpec((tb, HP * LANES), lambda b, h, p, it_, jt_: (b * n + jt_[p], h))
    krspec = pl.BlockSpec((tb, LANES), lambda b, h, p, it_, jt_: (b * n + jt_[p], 0))
    return pl.pallas_call(
        body, name=name,
        grid_spec=pltpu.PrefetchScalarGridSpec(
            num_scalar_prefetch=2, grid=(Bn, nh // HP, len(pairs)),
            in_specs=[qspec, qspec, kspec, krspec, kspec, qspec, qspec, qspec], out_specs=[qspec, qspec],
            scratch_shapes=[pltpu.VMEM((HP, tb, 2 * LANES), F32), pltpu.VMEM((HP, tb, 1), F32),
                            pltpu.VMEM((HP, tb, 1), F32)]),
        out_shape=[jax.ShapeDtypeStruct((T, W), BF16), jax.ShapeDtypeStruct((T, W), F32)],
        compiler_params=_cparams(),
    )(it, jt, qn, qr, kn, kr, v, o, do, lse)


def _attention_dkv(qn, qr, kn, kr, v, o, do, lse, S, scale, name):
    T, W = qn.shape
    nh = W // LANES
    Bn = T // S
    tb, n, C = _att_geometry(S)
    HP = _heads_per_step(nh)
    triples = [(j, h, i) for j in range(n) for h in range(nh // HP) for i in range(j, n)]
    jt = jnp.asarray([t[0] for t in triples], jnp.int32)
    ht = jnp.asarray([t[1] for t in triples], jnp.int32)
    it = jnp.asarray([t[2] for t in triples], jnp.int32)
    c2 = scale * LOG2E

    def as_row(col):
        return jnp.broadcast_to(col, (ATT_SUB, LANES)).T[0:1, :]

    def body(jt_ref, ht_ref, it_ref, qn_ref, qr_ref, kn_ref, kr_ref, v_ref, o_ref, do_ref, lse_ref,
             dkn_ref, dv_ref, dkr_ref, akc, av):
        p = pl.program_id(1)
        j = jt_ref[p]
        h = ht_ref[p]
        i = it_ref[p]

        @pl.when(jnp.logical_and(h == 0, i == j))
        def _():
            dkr_ref[...] = jnp.zeros_like(dkr_ref)

        @pl.when(i == j)
        def _():
            akc[...] = jnp.zeros_like(akc)
            av[...] = jnp.zeros_like(av)

        def step(diag):
            for hh in range(HP):
                hs = pl.ds(hh * LANES, LANES)
                kc = jnp.concatenate([kn_ref[:, hs], kr_ref[...]], axis=1)
                for c in range(C):
                    rows = pl.ds(c * ATT_SUB, ATT_SUB)
                    k1 = (c + 1) * ATT_SUB if diag else tb
                    ks = pl.ds(0, k1)
                    qc = jnp.concatenate([qn_ref[rows, hs], qr_ref[rows, hs]], axis=1)
                    st = _att_scores(kc[:k1], qc, diag, True, c * ATT_SUB)
                    dov = do_ref[rows, hs]
                    lse_row = as_row(jnp.max(lse_ref[rows, hs], axis=1, keepdims=True)) * LOG2E
                    di_row = as_row(jnp.sum(dov * o_ref[rows, hs], axis=1, keepdims=True))
                    pt = jnp.exp2(st * c2 - lse_row)
                    dst = pt * (_bdot(v_ref[ks, hs], dov, NT) - di_row) * scale
                    av[ks, hs] += _bdot(pt, dov, NN)
                    akc[hh, ks, :] += _bdot(dst, qc, NN)

        @pl.when(i > j)
        def _():
            step(False)

        @pl.when(i == j)
        def _():
            step(True)

        @pl.when(i == n - 1)
        def _():
            for hh in range(HP):
                dkn_ref[:, pl.ds(hh * LANES, LANES)] = akc[hh, :, 0:LANES].astype(BF16)
                dkr_ref[...] += akc[hh, :, LANES:2 * LANES]
            dv_ref[...] = av[...].astype(BF16)

    qspec = pl.BlockSpec((tb, HP * LANES), lambda b, p, jt_, ht_, it_: (b * n + it_[p], ht_[p]))
    kspec = pl.BlockSpec((tb, HP * LANES), lambda b, p, jt_, ht_, it_: (b * n + jt_[p], ht_[p]))
    krspec = pl.BlockSpec((tb, LANES), lambda b, p, jt_, ht_, it_: (b * n + jt_[p], 0))
    return pl.pallas_call(
        body, name=name,
        grid_spec=pltpu.PrefetchScalarGridSpec(
            num_scalar_prefetch=3, grid=(Bn, len(triples)),
            in_specs=[qspec, qspec, kspec, krspec, kspec, qspec, qspec, qspec],
            out_specs=[kspec, kspec, krspec],
            scratch_shapes=[pltpu.VMEM((HP, tb, 2 * LANES), F32), pltpu.VMEM((tb, HP * LANES), F32)]),
        out_shape=[jax.ShapeDtypeStruct((T, W), BF16), jax.ShapeDtypeStruct((T, W), BF16),
                   jax.ShapeDtypeStruct((T, LANES), F32)],
        compiler_params=_cparams(),
    )(jt, ht, it, qn, qr, kn, kr, v, o, do, lse)


def _pad_cols(w, n):
    return jnp.pad(w, ((0, 0), (0, n - w.shape[1])))


def _local_step(x, positions, w, target, late=None, reduce_early=None):
    w = dict(w)
    Bn, S, D = x.shape
    T = Bn * S
    depth = w['ln_mix_g'].shape[0]
    alpha = (2 * depth) ** 0.25
    d_inner = w['ssd_out_proj'].shape[1]
    n_heads_ssd = d_inner // SSD_HEAD_DIM
    G = SSD_N_GROUPS
    gn = G * SSD_D_STATE
    conv_dim = d_inner + 2 * gn
    hpg = n_heads_ssd // G
    nh = D // MLA_NOPE
    kv_rank = w['kv_norm_g'].shape[0]
    q_rank = w['q_norm_g'].shape[1]
    H = w['ffn_conv_b'].shape[1] // 2
    scale = (MLA_NOPE + MLA_ROPE) ** -0.5
    assert S % SSD_CHUNK == 0 and hpg % 2 == 0 and SSD_D_STATE == LANES and n_heads_ssd <= LANES

    X = x.reshape(T, D)
    tgt = target.reshape(T, D)

    inv_freq = 1.0 / (ROPE_THETA ** (jnp.arange(0, MLA_ROPE, 2, dtype=F32) / MLA_ROPE))
    ang = positions.reshape(T).astype(F32)[:, None] * inv_freq
    cos, sin = jnp.cos(ang), jnp.sin(ang)
    zpad = jnp.zeros((T, LANES - MLA_ROPE), F32)
    cc = jnp.concatenate([cos, cos, zpad], axis=1)
    ss = jnp.concatenate([-sin, sin, zpad], axis=1)

    w_in = w['ssd_in_proj'][0]
    w_z = w_in[:, :d_inner]
    w_xbc = w_in[:, d_inner:d_inner + conv_dim]
    w_dt = _pad_cols(w_in[:, d_inner + conv_dim:], LANES)
    w_out = w['ssd_out_proj'][0]
    ssd_cw = w['ssd_conv_w'][0]
    ssd_cb = w['ssd_conv_b']
    dt_bias = _pad_cols(w['ssd_dt_bias'], LANES)
    a_log = _pad_cols(w['ssd_A_log'], LANES)
    d_lane = jnp.repeat(w['ssd_D'], SSD_HEAD_DIM, axis=1)
    ssd_ng = w['ssd_norm_g']

    Xb = X.astype(BF16)
    z = _mm(Xb, w_z, name='ssd_in_z')
    xbc_pre = _mm(Xb, w_xbc, name='ssd_in_xbc')
    dt_pre = _mm(Xb, w_dt, name='ssd_in_dt')
    xbc, xbc_conv = _ssd_conv_fwd(xbc_pre, ssd_cw, ssd_cb, S, name='ssd_conv')
    y_scan, states, gathered = _scan_fwd(xbc, dt_pre, dt_bias, a_log, S, d_inner, name='ssd_scan',
                                         gather=late[0] if late else ())
    if late:
        w.update(late[1](gathered))

    w_kvc = w['kv_down_proj'][:, :kv_rank]
    w_kvr = _pad_cols(w['kv_down_proj'][:, kv_rank:], LANES)
    kv_g = w['kv_norm_g'].reshape(1, kv_rank)
    w_uk = w['kv_up_k']
    w_uv = w['kv_up_v']
    w_qd = w['q_down_proj'][0]
    q_g = w['q_norm_g']
    qu = w['q_up_proj'][0].reshape(q_rank, nh, MLA_NOPE + MLA_ROPE)
    w_qn = qu[:, :, :MLA_NOPE].reshape(q_rank, nh * LANES)
    w_qr = jnp.pad(qu[:, :, MLA_NOPE:], ((0, 0), (0, 0), (0, LANES - MLA_ROPE))).reshape(q_rank, nh * LANES)
    w_ao = w['attn_out_proj'][0]

    def ffn_parts(i):
        return dict(wu=w['ffn_up'][i], cw=w['ffn_conv_w'][i], cb=w['ffn_conv_b'][i:i + 1], wd=w['ffn_down'][i])

    def ln(name, i):
        return w[name][i:i + 1]

    def ffn_fwd(hb, i):
        f = ffn_parts(i)
        u = _mm(hb, f['wu'], name=f'ffn{i}_up')
        a, cg, cv = _ffn_act_fwd(u, f['cw'], f['cb'], S, name=f'ffn{i}_act')
        return _mm(a, f['wd'], name=f'ffn{i}_down'), (u, a, cg, cv)

    yn = _gate_fwd(y_scan, xbc, z, d_lane, ssd_ng, d_inner, name='ssd_gate')
    mix0 = _mm(yn, w_out, name='ssd_out')
    h1, s1, h1b = _ln_fwd(X, mix0, ln('ln_mix_g', 0), ln('ln_mix_b', 0), alpha, name='ln_mix0')
    ff0, ffn0_saved = ffn_fwd(h1b, 0)
    h2, s2, h2b = _ln_fwd(h1, ff0, ln('ln_ffn_g', 0), ln('ln_ffn_b', 0), alpha, name='ln_ffn0')

    ckv = _mm(h2b, w_kvc, name='kv_down_c')
    kr_pre = _mm(h2b, w_kvr, name='kv_down_r')
    ckvn = _rms_fwd(ckv, kv_g, name='kv_norm')
    kr = _rope(kr_pre, cc, ss, transpose=False, name='k_rope')
    kn = _mm(ckvn, w_uk, out_dtype=BF16, name='kv_up_k')
    v = _mm(ckvn, w_uv, out_dtype=BF16, name='kv_up_v')
    cq_pre = _mm(h2b, w_qd, name='q_down')
    cq = _rms_fwd(cq_pre, q_g, name='q_norm')
    qn = _mm(cq, w_qn, out_dtype=BF16, name='q_up_n')
    qr_pre = _mm(cq, w_qr, name='q_up_r')
    qr = _rope(qr_pre, cc, ss, transpose=False, name='q_rope')
    o, ob, lse = _attention_fwd(qn, qr, kn, kr, v, S, scale, name='attn_fwd')
    mix1 = _mm(ob, w_ao, name='attn_out')
    h3, s3, h3b = _ln_fwd(h2, mix1, ln('ln_mix_g', 1), ln('ln_mix_b', 1), alpha, name='ln_mix1')
    ff1, ffn1_saved = ffn_fwd(h3b, 1)
    h4, s4, _ = _ln_fwd(h3, ff1, ln('ln_ffn_g', 1), ln('ln_ffn_b', 1), alpha, name='ln_ffn1')
    sq, dh4 = _loss_head(h4, tgt, name='loss_head')

    g = {}

    def ffn_bwd(ds, dsb, hb_in, saved, i):
        f = ffn_parts(i)
        u, a, cg, cv = saved
        d_wd = _mm(a, dsb, ta=True, name=f'ffn{i}_down_dw')
        da = _mm(dsb, f['wd'], tb=True, out_dtype=BF16, name=f'ffn{i}_down_dx')
        dug, duv, dwg, dwv, dbg, dbv = _ffn_act_bwd(u, cg, cv, da, f['cw'], S, name=f'ffn{i}_act_bwd')
        d_wu = _mm(hb_in, dug, ta=True, name=f'ffn{i}_up_g_dw', into=lax.empty((D, 2 * H), F32))
        d_wu = _mm(hb_in, duv, ta=True, name=f'ffn{i}_up_v_dw', into=d_wu, into_col=H)
        dh = _mm(dug, f['wu'], tb=True, add=ds, add_scale=alpha, name=f'ffn{i}_up_g_dx')
        dh = _mm(duv, f['wu'], tb=True, add=dh, name=f'ffn{i}_up_v_dx', b_koff=H)
        return dh, d_wd, d_wu, jnp.concatenate([dwg, dwv], axis=1), jnp.concatenate([dbg, dbv], axis=1)

    ds4, ds4b, dg_f1, db_f1 = _ln_bwd(dh4, s4, ln('ln_ffn_g', 1), name='ln_ffn1_bwd')
    dh3, d_wd1, d_wu1, d_fcw1, d_fcb1 = ffn_bwd(ds4, ds4b, h3b, ffn1_saved, 1)
    ds3, ds3b, dg_m1, db_m1 = _ln_bwd(dh3, s3, ln('ln_mix_g', 1), name='ln_mix1_bwd')

    g['attn_out_proj'] = _mm(ob, ds3b, ta=True, name='attn_out_dw')[None]
    do = _mm(ds3b, w_ao, tb=True, name='attn_out_dx')
    dqn, dqr = _attention_dq(qn, qr, kn, kr, v, o, do, lse, S, scale, name='attn_bwd_dq')
    dkn, dv, dkr = _attention_dkv(qn, qr, kn, kr, v, o, do, lse, S, scale, name='attn_bwd_dkv')
    dqr_pre = _rope(dqr, cc, ss, transpose=True, name='q_rope_bwd')
    dkr_pre = _rope(dkr, cc, ss, transpose=True, name='k_rope_bwd')

    d_wqn = _mm(cq, dqn, ta=True, name='q_up_n_dw').reshape(q_rank, nh, LANES)
    d_wqr = _mm(cq, dqr_pre, ta=True, name='q_up_r_dw').reshape(q_rank, nh, LANES)[:, :, :MLA_ROPE]
    g['q_up_proj'] = jnp.concatenate([d_wqn, d_wqr], axis=2).reshape(1, q_rank, nh * (MLA_NOPE + MLA_ROPE))
    dcq = _mm(dqn, w_qn, tb=True, name='q_up_n_dx')
    dcq = _mm(dqr_pre, w_qr, tb=True, add=dcq, name='q_up_r_dx')
    dcq_pre, d_qg = _rms_bwd(dcq, cq_pre, q_g, name='q_norm_bwd')
    g['q_norm_g'] = d_qg
    g['q_down_proj'] = _mm(h2b, dcq_pre, ta=True, name='q_down_dw')[None]

    g['kv_up_k'] = _mm(ckvn, dkn, ta=True, name='kv_up_k_dw')
    g['kv_up_v'] = _mm(ckvn, dv, ta=True, name='kv_up_v_dw')
    dckvn = _mm(dkn, w_uk, tb=True, name='kv_up_k_dx')
    dckvn = _mm(dv, w_uv, tb=True, add=dckvn, name='kv_up_v_dx')
    dckv, d_kvg = _rms_bwd(dckvn, ckv, kv_g, name='kv_norm_bwd')
    g['kv_norm_g'] = d_kvg.reshape(kv_rank)
    g['kv_down_proj'] = jnp.concatenate(
        [_mm(h2b, dckv, ta=True, name='kv_down_c_dw'),
         _mm(h2b, dkr_pre, ta=True, name='kv_down_r_dw')[:, :MLA_ROPE]], axis=1)

    dh2 = _mm(dcq_pre, w_qd, tb=True, add=ds3, add_scale=alpha, name='q_down_dx')
    dh2 = _mm(dckv, w_kvc, tb=True, add=dh2, name='kv_down_c_dx')
    dh2 = _mm(dkr_pre, w_kvr, tb=True, add=dh2, name='kv_down_r_dx')

    ds2, ds2b, dg_f0, db_f0 = _ln_bwd(dh2, s2, ln('ln_ffn_g', 0), name='ln_ffn0_bwd')
    dh1, d_wd0, d_wu0, d_fcw0, d_fcb0 = ffn_bwd(ds2, ds2b, h1b, ffn0_saved, 0)
    ds1, ds1b, dg_m0, db_m0 = _ln_bwd(dh1, s1, ln('ln_mix_g', 0), name='ln_mix0_bwd')

    g['ssd_out_proj'] = _mm(yn, ds1b, ta=True, name='ssd_out_dw')[None]
    dyn = _mm(ds1b, w_out, tb=True, out_dtype=BF16, name='ssd_out_dx')
    dy1, dz, d_ng, d_dl = _gate_bwd(dyn, y_scan, xbc, z, d_lane, ssd_ng, d_inner, name='ssd_gate_bwd')
    g['ffn_up'] = jnp.stack([d_wu0, d_wu1])
    g['ffn_down'] = jnp.stack([d_wd0, d_wd1])
    (dxs, dB, dC, ddt_g, da_g), early = _scan_bwd(xbc, dt_pre, dt_bias, a_log, states, dy1, d_lane, S, d_inner,
                                                  name='ssd_scan_bwd',
                                                  exchange=reduce_early(g) if reduce_early else ())
    ddt_pre, d_bias, d_alog = _dt_bwd(ddt_g, da_g, dt_pre, dt_bias, a_log, name='ssd_dt_bwd')
    dxbc = jnp.concatenate([dxs, dB, dC], axis=1)
    dxbc_pre, d_scw, d_scb = _ssd_conv_bwd(xbc_pre, xbc_conv, dxbc, ssd_cw, S, name='ssd_conv_bwd')
    g['ssd_in_proj'] = jnp.concatenate(
        [_mm(Xb, dz, ta=True, name='ssd_in_z_dw'), _mm(Xb, dxbc_pre, ta=True, name='ssd_in_xbc_dw'),
         _mm(Xb, ddt_pre, ta=True, name='ssd_in_dt_dw')[:, :n_heads_ssd]], axis=1)[None]
    dx = _mm(dz, w_z, tb=True, add=ds1, add_scale=alpha, name='ssd_in_z_dx')
    dx = _mm(dxbc_pre, w_xbc, tb=True, add=dx, name='ssd_in_xbc_dx')
    dx = _mm(ddt_pre, w_dt, tb=True, add=dx, name='ssd_in_dt_dx')

    g['ssd_conv_w'] = d_scw[None]
    g['ssd_conv_b'] = d_scb
    g['ssd_dt_bias'] = d_bias[:, :n_heads_ssd]
    g['ssd_A_log'] = d_alog[:, :n_heads_ssd]
    g['ssd_D'] = jnp.sum(d_dl.reshape(1, n_heads_ssd, SSD_HEAD_DIM), axis=2)
    g['ssd_norm_g'] = d_ng
    g['ffn_conv_w'] = jnp.stack([d_fcw0, d_fcw1])
    g['ffn_conv_b'] = jnp.concatenate([d_fcb0, d_fcb1], axis=0)
    g['ln_mix_g'] = jnp.concatenate([dg_m0, dg_m1], axis=0)
    g['ln_mix_b'] = jnp.concatenate([db_m0, db_m1], axis=0)
    g['ln_ffn_g'] = jnp.concatenate([dg_f0, dg_f1], axis=0)
    g['ln_ffn_b'] = jnp.concatenate([db_f0, db_f1], axis=0)
    return sq, dx.reshape(Bn, S, D), g, early


ANY = pl.BlockSpec(memory_space=pl.ANY)


def _all_gather(xs, name):
    n = len(xs)

    def body(*refs):
        start, forward, finish = _gather_plan(refs[:n], refs[n:2 * n], *refs[2 * n:])
        start()
        forward()
        finish()

    return pl.pallas_call(
        body, name=name, out_shape=_gather_shapes(xs), in_specs=[ANY] * n, out_specs=[ANY] * n,
        scratch_shapes=_gather_sems(n),
    )(*xs)


def _gather_shapes(xs):
    return [jax.ShapeDtypeStruct((N_DEV,) + a.shape, a.dtype) for a in xs]


def _gather_sems(n):
    return [pltpu.SemaphoreType.DMA((7 * n,)), pltpu.SemaphoreType.DMA((7 * n,)), pltpu.SemaphoreType.DMA((n,))]


def _gather_plan(x_refs, out_refs, send_sems, recv_sems, local_sems):
    n = len(x_refs)
    x, y, c = lax.axis_index("x"), lax.axis_index("y"), lax.axis_index("c")
    me, sibling = (x, y, c), (x, y, 1 - c)
    chips = [(1 - x, y), (x, 1 - y), (1 - x, 1 - y)]

    def rows(i, px, py, pc):
        return out_refs[i].at[4 * px + 2 * py + pc]

    def copy(i, k, block, to, own=False):
        return pltpu.make_async_remote_copy(
            src_ref=x_refs[i] if own else rows(i, *block), dst_ref=rows(i, *block),
            send_sem=send_sems.at[7 * i + k], recv_sem=recv_sems.at[7 * i + k],
            device_id=to, device_id_type=MESH)

    def mine():
        return [pltpu.make_async_copy(x_refs[i], rows(i, *me), local_sems.at[i]) for i in range(n)]

    def first():
        out = []
        for i in range(n):
            out.append(copy(i, 0, me, sibling, own=True))
            out += [copy(i, 1 + j, me, (*chip, c), own=True) for j, chip in enumerate(chips)]
        return out

    def passed():
        return [copy(i, 4 + j, (*chip, c), sibling) for j, chip in enumerate(chips) for i in range(n)]

    def start():
        for cp in mine() + first():
            cp.start()

    def forward():
        for j, chip in enumerate(chips):
            for i in range(n):
                copy(i, 1 + j, (*chip, c), me).wait_recv()
                copy(i, 4 + j, (*chip, c), sibling).start()

    def finish():
        for i in range(n):
            copy(i, 0, sibling, me).wait_recv()
            for j, chip in enumerate(chips):
                copy(i, 4 + j, (*chip, 1 - c), me).wait_recv()
        for cp in first() + passed():
            cp.wait_send()
        for cp in mine():
            cp.wait()

    return start, forward, finish


def _exchange_sibling(gs, name):
    n = len(gs)

    def body(*refs):
        g_refs, r_refs = refs[:n], refs[n:2 * n]
        send_sems, recv_sems = refs[2 * n:]
        x, y, c = lax.axis_index("x"), lax.axis_index("y"), lax.axis_index("c")
        copies = [pltpu.make_async_remote_copy(
            src_ref=g_refs[i].at[2 * k + (1 - c)], dst_ref=r_refs[i].at[k], send_sem=send_sems.at[4 * i + k],
            recv_sem=recv_sems.at[4 * i + k], device_id=(x, y, 1 - c), device_id_type=MESH)
            for i in range(n) for k in range(4)]
        for cp in copies:
            cp.start()
        for cp in copies:
            cp.wait()

    return pl.pallas_call(
        body, name=name, out_shape=[jax.ShapeDtypeStruct((4,) + g.shape[1:], g.dtype) for g in gs],
        in_specs=[ANY] * n, out_specs=[ANY] * n,
        scratch_shapes=[pltpu.SemaphoreType.DMA((4 * n,)), pltpu.SemaphoreType.DMA((4 * n,))],
    )(*gs)


def _exchange_chips(parts, name):
    n = len(parts)

    def body(*refs):
        start, finish = _chips_plan(refs[:n], refs[n:2 * n], *refs[2 * n:])
        start()
        finish()

    return pl.pallas_call(
        body, name=name, out_shape=_chips_shapes(parts), in_specs=[ANY] * n, out_specs=[ANY] * n,
        scratch_shapes=_chips_sems(n),
    )(*parts)


def _chips_shapes(parts):
    return [jax.ShapeDtypeStruct((3,) + p.shape[1:], p.dtype) for p in parts]


def _chips_sems(n):
    return [pltpu.SemaphoreType.DMA((3 * n,)), pltpu.SemaphoreType.DMA((3 * n,))]


def _chips_plan(p_refs, r_refs, send_sems, recv_sems):
    n = len(p_refs)
    x, y, c = lax.axis_index("x"), lax.axis_index("y"), lax.axis_index("c")
    chips = [(1 - x, y), (x, 1 - y), (1 - x, 1 - y)]

    def copies():
        return [pltpu.make_async_remote_copy(
            src_ref=p_refs[i].at[2 * cx + cy], dst_ref=r_refs[i].at[j], send_sem=send_sems.at[3 * i + j],
            recv_sem=recv_sems.at[3 * i + j], device_id=(cx, cy, c), device_id_type=MESH)
            for i in range(n) for j, (cx, cy) in enumerate(chips)]

    def start():
        for cp in copies():
            cp.start()

    def finish():
        for cp in copies():
            cp.wait()

    return start, finish


def _row_tile(rows, cols):
    want = max(8, (256 * 1024) // cols)
    for t in range(min(rows, want) // 8 * 8, 7, -8):
        if rows % t == 0:
            return t
    return rows


def _add_sibling(gfull, r1, core, name):
    _, rows, lanes = gfull.shape
    tr = _row_tile(rows, lanes)

    def body(c_ref, g_ref, r_ref, o_ref):
        o_ref[...] = g_ref[...] + r_ref[...]

    return pl.pallas_call(
        body, name=name,
        grid_spec=pltpu.PrefetchScalarGridSpec(
            num_scalar_prefetch=1, grid=(4, rows // tr),
            in_specs=[pl.BlockSpec((1, tr, lanes), lambda k, r, c_ref: (2 * k + c_ref[0], r, 0)),
                      pl.BlockSpec((1, tr, lanes), lambda k, r, c_ref: (k, r, 0))],
            out_specs=pl.BlockSpec((1, tr, lanes), lambda k, r, c_ref: (k, r, 0))),
        out_shape=jax.ShapeDtypeStruct((4, rows, lanes), F32), compiler_params=_cparams(),
    )(core, gfull, r1)


def _adam_math(wv, gv, mv, vv):
    m = ADAM_B1 * mv + (1.0 - ADAM_B1) * gv
    v = ADAM_B2 * vv + (1.0 - ADAM_B2) * (gv * gv)
    m_hat = m / (1.0 - ADAM_B1 ** ADAM_STEP)
    v_hat = v / (1.0 - ADAM_B2 ** ADAM_STEP)
    delta = -ADAM_LR * (m_hat / (jnp.sqrt(v_hat) + ADAM_EPS) + ADAM_WD * wv)
    return delta, m, v


def _adam_sharded(part, r2, chip, wts, m, v, name):
    rows, lanes = wts.shape
    tr = _row_tile(rows, lanes)

    def body(k_ref, p_ref, r_ref, w_ref, m_ref, v_ref, g_ref, d_ref, mo_ref, vo_ref):
        gv = p_ref[0] + r_ref[0] + r_ref[1] + r_ref[2]
        delta, mn, vn = _adam_math(w_ref[...], gv, m_ref[...], v_ref[...])
        g_ref[...] = gv
        d_ref[...] = delta
        mo_ref[...] = mn
        vo_ref[...] = vn

    flat = pl.BlockSpec((tr, lanes), lambda r, k_ref: (r, 0))
    sh = jax.ShapeDtypeStruct((rows, lanes), F32)
    return pl.pallas_call(
        body, name=name,
        grid_spec=pltpu.PrefetchScalarGridSpec(
            num_scalar_prefetch=1, grid=(rows // tr,),
            in_specs=[pl.BlockSpec((1, tr, lanes), lambda r, k_ref: (k_ref[0], r, 0)),
                      pl.BlockSpec((3, tr, lanes), lambda r, k_ref: (0, r, 0)), flat, flat, flat],
            out_specs=[flat, flat, flat, flat]),
        out_shape=[sh, sh, sh, sh], compiler_params=_cparams(),
    )(chip, part, r2, wts, m, v)


def _adam_replicated(gall, wts, m, v, name):
    rows, lanes = wts.shape

    def body(ga_ref, w_ref, m_ref, v_ref, g_ref, d_ref, mo_ref, vo_ref):
        gv = ga_ref[0]
        for k in range(1, N_DEV):
            gv = gv + ga_ref[k]
        delta, mn, vn = _adam_math(w_ref[...], gv, m_ref[...], v_ref[...])
        g_ref[...] = gv
        d_ref[...] = delta
        mo_ref[...] = mn
        vo_ref[...] = vn

    sh = jax.ShapeDtypeStruct((rows, lanes), F32)
    return pl.pallas_call(body, name=name, out_shape=[sh, sh, sh, sh], compiler_params=_cparams())(gall, wts, m, v)


def _pack(arrs, dtype, row_mult):
    flat = jnp.concatenate([a.reshape(-1).astype(dtype) for a in arrs])
    n = flat.shape[0]
    unit = LANES * row_mult
    total = -(-n // unit) * unit
    return jnp.pad(flat, (0, total - n)).reshape(total // LANES, LANES)


def _unpack(flat2d, shapes):
    flat = flat2d.reshape(-1)
    out, off = [], 0
    for shp in shapes:
        n = math.prod(shp)
        out.append(flat[off:off + n].reshape(shp))
        off += n
    return out


def _unpack_gathered(gathered, shapes, axes):
    flat = gathered.reshape(N_DEV, -1)
    out, off = [], 0
    for shp, ax in zip(shapes, axes):
        n = math.prod(shp)
        seg = flat[:, off:off + n].reshape((N_DEV,) + tuple(shp))
        out.append(jnp.concatenate([seg[i] for i in range(N_DEV)], axis=ax))
        off += n
    return out


def _pack_chunks(fulls, axes, row_mult):
    per_dev = []
    for full, ax in zip(fulls, axes):
        parts = jnp.stack(jnp.split(full, N_DEV, axis=ax))
        per_dev.append(parts.reshape(N_DEV, -1))
    flat = jnp.concatenate(per_dev, axis=1)
    n = flat.shape[1]
    unit = LANES * row_mult
    total = -(-n // unit) * unit
    return jnp.pad(flat, ((0, 0), (0, total - n))).reshape(N_DEV, total // LANES, LANES)


def kernel(x, positions, ssd_in_proj, ssd_conv_w, ssd_conv_b, ssd_dt_bias, ssd_A_log, ssd_D, ssd_norm_g, ssd_out_proj, kv_down_proj, kv_norm_g, kv_up_k, kv_up_v, q_down_proj, q_norm_g, q_up_proj, attn_out_proj, ffn_up, ffn_conv_w, ffn_conv_b, ffn_down, ln_mix_g, ln_mix_b, ln_ffn_g, ln_ffn_b, loss_target, m_ssd_in_proj, m_ssd_conv_w, m_ssd_conv_b, m_ssd_dt_bias, m_ssd_A_log, m_ssd_D, m_ssd_norm_g, m_ssd_out_proj, m_kv_down_proj, m_kv_norm_g, m_kv_up_k, m_kv_up_v, m_q_down_proj, m_q_norm_g, m_q_up_proj, m_attn_out_proj, m_ffn_up, m_ffn_conv_w, m_ffn_conv_b, m_ffn_down, m_ln_mix_g, m_ln_mix_b, m_ln_ffn_g, m_ln_ffn_b, v_ssd_in_proj, v_ssd_conv_w, v_ssd_conv_b, v_ssd_dt_bias, v_ssd_A_log, v_ssd_D, v_ssd_norm_g, v_ssd_out_proj, v_kv_down_proj, v_kv_norm_g, v_kv_up_k, v_kv_up_v, v_q_down_proj, v_q_norm_g, v_q_up_proj, v_attn_out_proj, v_ffn_up, v_ffn_conv_w, v_ffn_conv_b, v_ffn_down, v_ln_mix_g, v_ln_mix_b, v_ln_ffn_g, v_ln_ffn_b):
    given = dict(locals())
    wl = {n: given[n] for n in WEIGHTS}
    ml = {n: given['m_' + n] for n in WEIGHTS}
    vl = {n: given['v_' + n] for n in WEIGHTS}
    sharded_axis = dict(SHARDED)
    big = [n for n, _ in SHARDED if n not in SHARDED_F32]
    small = [n for n, _ in SHARDED if n in SHARDED_F32]

    def as2d(a):
        return a.reshape(-1, a.shape[-1])

    def to_full(gathered, n):
        shp, ax = wl[n].shape, sharded_axis[n]
        moved = jnp.moveaxis(gathered.reshape((N_DEV,) + shp), 0, ax)
        return moved.reshape(shp[:ax] + (N_DEV * shp[ax],) + shp[ax + 1:])

    def to_chunks(full_grad, n):
        shp, ax = wl[n].shape, sharded_axis[n]
        split = full_grad.reshape(shp[:ax] + (N_DEV, shp[ax]) + shp[ax + 1:])
        return jnp.moveaxis(split, ax, 0).reshape((N_DEV,) + as2d(wl[n]).shape)

    first = [n for n in big if n in GATHERED_FIRST]
    late = [n for n in big if n not in GATHERED_FIRST]
    full = {n: wl[n] for n in REPLICATED}
    gathered = _all_gather([as2d(wl[n]).astype(BF16) for n in first] + [_pack([wl[n] for n in small], F32, 8)],
                           name='gather_weights')
    for n, gat in zip(first, gathered[:-1]):
        full[n] = to_full(gat, n)
    full.update(zip(small, _unpack_gathered(gathered[-1], [wl[n].shape for n in small],
                                            [sharded_axis[n] for n in small])))

    cx, cy, cc = lax.axis_index("x"), lax.axis_index("y"), lax.axis_index("c")
    core = jnp.reshape(cc, (1,)).astype(jnp.int32)
    chip = jnp.reshape(2 * cx + cy, (1,)).astype(jnp.int32)
    early_names = [n for n in big if n not in REDUCED_LAST]
    last_names = [n for n in big if n in REDUCED_LAST]
    parts = {}

    def to_sibling(chunked, names, tag):
        r1 = _exchange_sibling(chunked, name=f'grads_to_sibling_{tag}')
        return [_add_sibling(gf, r, core, name=f'grads_add_sibling_{n}') for n, gf, r in zip(names, chunked, r1)]

    def reduce_early(grads):
        parts.update(zip(early_names, to_sibling([to_chunks(grads[n], n) for n in early_names], early_names, 'early')))
        return [parts[n] for n in early_names]

    sq, grad_x, grads, r2_early = _local_step(
        x, positions, full, loss_target,
        late=([as2d(wl[n]).astype(BF16) for n in late], lambda gats: {n: to_full(g, n) for n, g in zip(late, gats)}),
        reduce_early=reduce_early)
    loss = lax.psum(0.5 * jnp.sum(sq) / x.shape[-1], ("x", "y", "c"))

    chunked = [to_chunks(grads[n], n) for n in last_names]
    chunked.append(_pack_chunks([grads[n] for n in small], [sharded_axis[n] for n in small], 8))
    parts_last = to_sibling(chunked, last_names + ['small'], 'last')
    r2_last = _exchange_chips(parts_last, name='grads_to_chips')
    parts.update(zip(last_names + ['small'], parts_last))
    r2 = dict(zip(early_names, r2_early))
    r2.update(zip(last_names + ['small'], r2_last))
    res = {}
    kinds = ('grad', 'delta', 'new_m', 'new_v')
    for n in big:
        outs = _adam_sharded(parts[n], r2[n], chip, as2d(wl[n]), as2d(ml[n]), as2d(vl[n]), name=f'adamw_{n}')
        for kind, a in zip(kinds, outs):
            res[kind, n] = a.reshape(wl[n].shape)
    outs = _adam_sharded(parts['small'], r2['small'], chip, _pack([wl[n] for n in small], F32, 8),
                         _pack([ml[n] for n in small], F32, 8), _pack([vl[n] for n in small], F32, 8),
                         name='adamw_small_sharded')
    for kind, packed in zip(kinds, outs):
        for n, a in zip(small, _unpack(packed, [wl[n].shape for n in small])):
            res[kind, n] = a

    rep = list(REPLICATED)
    rshapes = [wl[n].shape for n in rep]
    gall, = _all_gather([_pack([grads[n] for n in rep], F32, 8)], name='gather_replicated_grads')
    outs = _adam_replicated(gall, _pack([wl[n] for n in rep], F32, 8), _pack([ml[n] for n in rep], F32, 8),
                            _pack([vl[n] for n in rep], F32, 8), name='adamw_replicated')
    for kind, packed in zip(('grad', 'delta', 'new_m', 'new_v'), outs):
        for n, a in zip(rep, _unpack(packed, rshapes)):
            res[kind, n] = a

    return (loss, grad_x, *[res['grad', n] for n in WEIGHTS], *[res['delta', n] for n in WEIGHTS],
            *[res['new_m', n] for n in WEIGHTS], *[res['new_v', n] for n in WEIGHTS])
```

```python
import functools
import math

import jax
import jax.numpy as jnp
from jax import lax
from jax.experimental import pallas as pl
from jax.experimental.pallas import tpu as pltpu

F32 = jnp.float32
BF16 = jnp.bfloat16
MESH = pl.DeviceIdType.MESH

N_DEV = 8
LANES = 128
MM_TILES = (1024, 1408, 896, 512, 384, 256, 128)
MM_VMEM_BUDGET = 38 * 1024 * 1024
VMEM_LIMIT = 56 * 1024 * 1024
LN_EPS = 1e-5
RMS_EPS = 1e-6
SSD_HEAD_DIM = 64
SSD_N_GROUPS = 8
SSD_D_STATE = 128
SSD_CHUNK = 128
MLA_NOPE = 128
MLA_ROPE = 64
MLA_V = 128
ROPE_THETA = 10000.0
ADAM_LR = 0.001
ADAM_B1 = 0.9
ADAM_B2 = 0.999
ADAM_EPS = 1e-08
ADAM_WD = 0.01
ADAM_STEP = 10
NEG_BIG = -1e30

SHARDED = (('ssd_in_proj', 2), ('ssd_conv_w', 2), ('ssd_conv_b', 1), ('ssd_norm_g', 1), ('ssd_out_proj', 1),
           ('kv_down_proj', 0), ('kv_up_k', 1), ('kv_up_v', 1), ('q_down_proj', 1), ('q_up_proj', 2),
           ('attn_out_proj', 1), ('ffn_up', 2), ('ffn_conv_w', 2), ('ffn_down', 1))
SHARDED_F32 = ('ssd_conv_w', 'ssd_conv_b', 'ssd_norm_g', 'ffn_conv_w')
GATHERED_FIRST = ('ssd_in_proj', 'ssd_out_proj')
REDUCED_LAST = ('ssd_in_proj',)
REPLICATED = ('ssd_dt_bias', 'ssd_A_log', 'ssd_D', 'kv_norm_g', 'q_norm_g', 'ffn_conv_b',
              'ln_mix_g', 'ln_mix_b', 'ln_ffn_g', 'ln_ffn_b')
WEIGHTS = ('ssd_in_proj', 'ssd_conv_w', 'ssd_conv_b', 'ssd_dt_bias', 'ssd_A_log', 'ssd_D', 'ssd_norm_g',
           'ssd_out_proj', 'kv_down_proj', 'kv_norm_g', 'kv_up_k', 'kv_up_v', 'q_down_proj', 'q_norm_g',
           'q_up_proj', 'attn_out_proj', 'ffn_up', 'ffn_conv_w', 'ffn_conv_b', 'ffn_down', 'ln_mix_g',
           'ln_mix_b', 'ln_ffn_g', 'ln_ffn_b')


def _cparams():
    return pltpu.CompilerParams(vmem_limit_bytes=VMEM_LIMIT)


def _tile(n, cands):
    for c in cands:
        if n % c == 0:
            return c
    return n


def _sigmoid(x):
    return 1.0 / (1.0 + jnp.exp(-x))


def _iota(shape, axis):
    return lax.broadcasted_iota(jnp.int32, shape, axis)


def _mm(a, b, *, ta=False, tb=False, add=None, add_scale=1.0, out_dtype=F32, name, b_koff=0, into=None, into_col=0):
    if ta:
        K, M = a.shape
    else:
        M, K = a.shape
    if tb:
        N, Kb = b.shape
    else:
        Kb, N = b.shape
    assert b_koff + K <= Kb, (a.shape, b.shape, ta, tb, b_koff)
    tm = _tile(M, MM_TILES)
    tn = _tile(N, MM_TILES)
    tk = K if K <= MM_TILES[0] and b_koff == 0 and Kb == K else _tile(K, MM_TILES)
    nk = K // tk
    assert b_koff % tk == 0 and (into is None or (into.shape[0] == M and into.dtype == out_dtype))
    kb = b_koff // tk

    def vmem_estimate(tm_, tn_):
        ins = 2 * (tm_ * tk * a.dtype.itemsize + tk * tn_ * b.dtype.itemsize)
        outs = tm_ * tn_ * (2 * jnp.dtype(out_dtype).itemsize + 4 + (4 if nk > 1 else 0))
        return ins + outs + (2 * tm_ * tn_ * add.dtype.itemsize if add is not None else 0)

    while vmem_estimate(tm, tn) > MM_VMEM_BUDGET:
        can_m, can_n = tm % (2 * LANES) == 0, tn % (2 * LANES) == 0
        if can_m and (tm >= tn or not can_n):
            tm //= 2
        elif can_n:
            tn //= 2
        else:
            break

    assert into_col % tn == 0
    jb = into_col // tn

    def body(a_ref, b_ref, *rest):
        add_ref = rest[0] if add is not None else None
        o_ref = rest[(add is not None) + (into is not None)]
        acc = rest[-1] if nk > 1 else None
        k = pl.program_id(2)

        if ta:
            av = a_ref[...].astype(BF16).T
        else:
            av = a_ref[...].astype(BF16)
        bv = b_ref[...].astype(BF16)
        dn = (((1,), (1 if tb else 0,)), ((), ()))
        part = lax.dot_general(av, bv, dn, preferred_element_type=F32)

        def finish(r):
            if add is not None:
                r = r + add_scale * add_ref[...].astype(F32)
            o_ref[...] = r.astype(out_dtype)

        if nk == 1:
            finish(part)
        else:
            @pl.when(k == 0)
            def _():
                acc[...] = part

            @pl.when(k > 0)
            def _():
                acc[...] += part

            @pl.when(k == nk - 1)
            def _():
                finish(acc[...])

    a_spec = (pl.BlockSpec((tk, tm), lambda i, j, k: (k, i)) if ta
              else pl.BlockSpec((tm, tk), lambda i, j, k: (i, k)))
    b_spec = (pl.BlockSpec((tn, tk), lambda i, j, k: (j, k + kb)) if tb
              else pl.BlockSpec((tk, tn), lambda i, j, k: (k + kb, j)))
    in_specs = [a_spec, b_spec]
    args = [a, b]
    if add is not None:
        in_specs.append(pl.BlockSpec((tm, tn), lambda i, j, k: (i, j)))
        args.append(add)
    aliases = {}
    if into is not None:
        aliases = {len(args): 0}
        in_specs.append(pl.BlockSpec(memory_space=pl.ANY))
        args.append(into)
    return pl.pallas_call(
        body, name=name, grid=(M // tm, N // tn, nk),
        in_specs=in_specs, out_specs=pl.BlockSpec((tm, tn), lambda i, j, k: (i, j + jb)),
        out_shape=jax.ShapeDtypeStruct((M, N) if into is None else into.shape, out_dtype),
        scratch_shapes=[pltpu.VMEM((tm, tn), F32)] if nk > 1 else [],
        input_output_aliases=aliases, compiler_params=_cparams(),
    )(*args)


def _ln_fwd(h, mix, g, b, alpha, name):
    T, D = h.shape
    tm = _tile(T, (256, 128))

    def body(h_ref, m_ref, g_ref, b_ref, y_ref, s_ref, yb_ref):
        s = alpha * h_ref[...] + m_ref[...]
        mu = jnp.mean(s, axis=1, keepdims=True)
        xc = s - mu
        var = jnp.mean(xc * xc, axis=1, keepdims=True)
        y = xc * lax.rsqrt(var + LN_EPS) * g_ref[...] + b_ref[...]
        y_ref[...] = y
        s_ref[...] = s
        yb_ref[...] = y.astype(BF16)

    row = pl.BlockSpec((tm, D), lambda i: (i, 0))
    vec = pl.BlockSpec((1, D), lambda i: (0, 0))
    return pl.pallas_call(
        body, name=name, grid=(T // tm,), in_specs=[row, row, vec, vec], out_specs=[row, row, row],
        out_shape=[jax.ShapeDtypeStruct((T, D), F32)] * 2 + [jax.ShapeDtypeStruct((T, D), BF16)],
        compiler_params=_cparams(),
    )(h, mix, g, b)


def _ln_bwd(dy, s, g, name):
    T, D = s.shape
    tm = _tile(T, (256, 128))

    def body(dy_ref, s_ref, g_ref, ds_ref, dsb_ref, dg_ref, db_ref):
        @pl.when(pl.program_id(0) == 0)
        def _():
            dg_ref[...] = jnp.zeros_like(dg_ref)
            db_ref[...] = jnp.zeros_like(db_ref)

        sv = s_ref[...]
        dyv = dy_ref[...]
        mu = jnp.mean(sv, axis=1, keepdims=True)
        xc = sv - mu
        rstd = lax.rsqrt(jnp.mean(xc * xc, axis=1, keepdims=True) + LN_EPS)
        xhat = xc * rstd
        dxh = dyv * g_ref[...]
        m1 = jnp.mean(dxh, axis=1, keepdims=True)
        m2 = jnp.mean(dxh * xhat, axis=1, keepdims=True)
        ds = rstd * (dxh - m1 - xhat * m2)
        ds_ref[...] = ds
        dsb_ref[...] = ds.astype(BF16)
        dg_ref[...] += jnp.sum(dyv * xhat, axis=0, keepdims=True)
        db_ref[...] += jnp.sum(dyv, axis=0, keepdims=True)

    row = pl.BlockSpec((tm, D), lambda i: (i, 0))
    vec = pl.BlockSpec((1, D), lambda i: (0, 0))
    return pl.pallas_call(
        body, name=name, grid=(T // tm,), in_specs=[row, row, vec], out_specs=[row, row, vec, vec],
        out_shape=[jax.ShapeDtypeStruct((T, D), F32), jax.ShapeDtypeStruct((T, D), BF16),
                   jax.ShapeDtypeStruct((1, D), F32), jax.ShapeDtypeStruct((1, D), F32)],
        compiler_params=_cparams(),
    )(dy, s, g)


def _loss_head(y, target, name):
    T, D = y.shape
    tm = _tile(T, (256, 128))

    def body(y_ref, t_ref, sq_ref, dy_ref):
        @pl.when(pl.program_id(0) == 0)
        def _():
            sq_ref[...] = jnp.zeros_like(sq_ref)

        e = y_ref[...] - t_ref[...]
        sq_ref[...] += jnp.sum(e * e, axis=0, keepdims=True)
        dy_ref[...] = e * (1.0 / D)

    row = pl.BlockSpec((tm, D), lambda i: (i, 0))
    vec = pl.BlockSpec((1, D), lambda i: (0, 0))
    return pl.pallas_call(
        body, name=name, grid=(T // tm,), in_specs=[row, row], out_specs=[vec, row],
        out_shape=[jax.ShapeDtypeStruct((1, D), F32), jax.ShapeDtypeStruct((T, D), F32)],
        compiler_params=_cparams(),
    )(y, target)


def _shift_down(x, j):
    if j == 0:
        return x
    return jnp.where(_iota(x.shape, 0) >= j, pltpu.roll(x, j, 0), 0.0)


def _shift_up(x, j):
    if j == 0:
        return x
    n = x.shape[0]
    return jnp.where(_iota(x.shape, 0) < n - j, pltpu.roll(x, n - j, 0), 0.0)


def _conv_val(x, w_ref, b_ref):
    width = w_ref.shape[0]
    y = b_ref[...] + w_ref[width - 1:width, :] * x
    for j in range(1, width):
        y = y + w_ref[width - 1 - j:width - j, :] * _shift_down(x, j)
    return y


def _conv_bwd_val(x, dc, w_ref, dw_ref, db_ref):
    width = w_ref.shape[0]
    dx = w_ref[width - 1:width, :] * dc
    dw_ref[width - 1:width, :] += jnp.sum(dc * x, axis=0, keepdims=True)
    for j in range(1, width):
        sj = _shift_up(dc, j)
        dx = dx + w_ref[width - 1 - j:width - j, :] * sj
        dw_ref[width - 1 - j:width - j, :] += jnp.sum(sj * x, axis=0, keepdims=True)
    db_ref[...] += jnp.sum(dc, axis=0, keepdims=True)
    return dx


def _ffn_specs(H, S, width):
    nC = H // LANES
    return dict(
        g=pl.BlockSpec((S, LANES), lambda j, b: (b, j)), v=pl.BlockSpec((S, LANES), lambda j, b: (b, j + nC)),
        wg=pl.BlockSpec((width, LANES), lambda j, b: (0, j)), wv=pl.BlockSpec((width, LANES), lambda j, b: (0, j + nC)),
        bg=pl.BlockSpec((1, LANES), lambda j, b: (0, j)), bv=pl.BlockSpec((1, LANES), lambda j, b: (0, j + nC)))


def _ffn_act_fwd(u, cw, cb, S, name):
    T, H2 = u.shape
    H = H2 // 2
    sp = _ffn_specs(H, S, cw.shape[0])

    def body(ug_ref, uv_ref, wg_ref, wv_ref, bg_ref, bv_ref, a_ref):
        g = _conv_val(ug_ref[...], wg_ref, bg_ref)
        v = _conv_val(uv_ref[...], wv_ref, bv_ref)
        a_ref[...] = (g * _sigmoid(g) * v).astype(BF16)

    return pl.pallas_call(
        body, name=name, grid=(H // LANES, T // S),
        in_specs=[sp['g'], sp['v'], sp['wg'], sp['wv'], sp['bg'], sp['bv']], out_specs=sp['g'],
        out_shape=jax.ShapeDtypeStruct((T, H), BF16), compiler_params=_cparams(),
    )(u, u, cw, cw, cb, cb)


def _ffn_act_bwd(u, da, cw, cb, S, name):
    T, H2 = u.shape
    H = H2 // 2
    width = cw.shape[0]
    sp = _ffn_specs(H, S, width)

    def body(ug_ref, uv_ref, da_ref, wg_ref, wv_ref, bg_ref, bv_ref,
             dug_ref, duv_ref, dwg_ref, dwv_ref, dbg_ref, dbv_ref):
        @pl.when(pl.program_id(1) == 0)
        def _():
            for r in (dwg_ref, dwv_ref, dbg_ref, dbv_ref):
                r[...] = jnp.zeros_like(r)

        xg = ug_ref[...]
        xv = uv_ref[...]
        g = _conv_val(xg, wg_ref, bg_ref)
        v = _conv_val(xv, wv_ref, bv_ref)
        dav = da_ref[...].astype(F32)
        sg = _sigmoid(g)
        dg = dav * v * sg * (1.0 + g * (1.0 - sg))
        dv = dav * g * sg
        dug_ref[...] = _conv_bwd_val(xg, dg, wg_ref, dwg_ref, dbg_ref).astype(BF16)
        duv_ref[...] = _conv_bwd_val(xv, dv, wv_ref, dwv_ref, dbv_ref).astype(BF16)

    act = jax.ShapeDtypeStruct((T, H), BF16)
    wsh = jax.ShapeDtypeStruct((width, H), F32)
    bsh = jax.ShapeDtypeStruct((1, H), F32)
    return pl.pallas_call(
        body, name=name, grid=(H // LANES, T // S),
        in_specs=[sp['g'], sp['v'], sp['g'], sp['wg'], sp['wv'], sp['bg'], sp['bv']],
        out_specs=[sp['g'], sp['g'], sp['wg'], sp['wg'], sp['bg'], sp['bg']],
        out_shape=[act, act, wsh, wsh, bsh, bsh], compiler_params=_cparams(),
    )(u, u, da, cw, cw, cb, cb)


def _ssd_conv_fwd(x, cw, cb, S, name):
    T, C = x.shape
    Cb = _tile(C, (256, 128))
    width = cw.shape[0]

    def body(x_ref, w_ref, b_ref, y_ref, c_ref):
        c = _conv_val(x_ref[...], w_ref, b_ref)
        c_ref[...] = c
        y_ref[...] = c * _sigmoid(c)

    blk = pl.BlockSpec((S, Cb), lambda j, b: (b, j))
    wblk = pl.BlockSpec((width, Cb), lambda j, b: (0, j))
    bblk = pl.BlockSpec((1, Cb), lambda j, b: (0, j))
    return pl.pallas_call(
        body, name=name, grid=(C // Cb, T // S), in_specs=[blk, wblk, bblk], out_specs=[blk, blk],
        out_shape=[jax.ShapeDtypeStruct((T, C), F32)] * 2, compiler_params=_cparams(),
    )(x, cw, cb)


def _ssd_conv_bwd(x, c, dys, cw, S, name):
    T, C = x.shape
    Cb = _tile(C, (256, 128))
    nC = C // Cb
    width = cw.shape[0]
    counts = [d.shape[1] // Cb for d in dys]
    starts = [sum(counts[:k]) for k in range(len(dys))]
    assert sum(counts) == nC and all(d.shape[1] % Cb == 0 for d in dys)

    def body(x_ref, c_ref, *rest):
        dy_refs = rest[:len(dys)]
        w_ref, dx_ref, dw_ref, db_ref = rest[len(dys):]
        j = pl.program_id(1)

        @pl.when(jnp.logical_and(pl.program_id(0) == 0, j == 0))
        def _():
            dw_ref[...] = jnp.zeros_like(dw_ref)
            db_ref[...] = jnp.zeros_like(db_ref)

        dy = dy_refs[0][...]
        for k in range(1, len(dys)):
            dy = jnp.where(j >= starts[k], dy_refs[k][...], dy)
        cval = c_ref[...]
        sc = _sigmoid(cval)
        dc = dy * sc * (1.0 + cval * (1.0 - sc))
        dx_ref[...] = _conv_bwd_val(x_ref[...], dc, w_ref, dw_ref.at[j], db_ref.at[j]).astype(BF16)

    def dy_spec(k):
        return pl.BlockSpec((S, Cb), lambda b, j: (b, jnp.clip(j - starts[k], 0, counts[k] - 1)))

    blk = pl.BlockSpec((S, Cb), lambda b, j: (b, j))
    dx, dw, db = pl.pallas_call(
        body, name=name, grid=(T // S, nC),
        in_specs=[blk, blk] + [dy_spec(k) for k in range(len(dys))] + [pl.BlockSpec((width, Cb), lambda b, j: (0, j))],
        out_specs=[blk, pl.BlockSpec((nC, width, Cb), lambda b, j: (0, 0, 0)),
                   pl.BlockSpec((nC, 1, Cb), lambda b, j: (0, 0, 0))],
        out_shape=[jax.ShapeDtypeStruct((T, C), BF16), jax.ShapeDtypeStruct((nC, width, Cb), F32),
                   jax.ShapeDtypeStruct((nC, 1, Cb), F32)],
        compiler_params=_cparams(),
    )(x, c, *dys, cw)
    return dx, jnp.swapaxes(dw, 0, 1).reshape(width, C), db.reshape(1, C)


def _softplus(x):
    e = jnp.exp(-jnp.abs(x))
    return jnp.maximum(x, 0.0) + jnp.where(e < 1e-4, e * (1.0 - 0.5 * e), jnp.log(1.0 + e))


def _cumsum_rows(x):
    n = x.shape[0]
    row = _iota(x.shape, 0)
    k = 1
    while k < n:
        x = x + jnp.where(row >= k, pltpu.roll(x, k, 0), 0.0)
        k *= 2
    return x


def _rev_cumsum_rows(x):
    n = x.shape[0]
    row = _iota(x.shape, 0)
    k = 1
    while k < n:
        x = x + jnp.where(row < n - k, pltpu.roll(x, n - k, 0), 0.0)
        k *= 2
    return x


def _col(x, lane_ids, h):
    return jnp.sum(jnp.where(lane_ids == h, x, 0.0), axis=1, keepdims=True)


def _bdot(a, b, dims):
    return lax.dot_general(a.astype(BF16), b.astype(BF16), (dims, ((), ())), preferred_element_type=F32)


NN = ((1,), (0,))
NT = ((1,), (1,))


def _scan_specs(nc, d_inner, hpg):
    L = SSD_CHUNK
    G = SSD_N_GROUPS
    gw = hpg * SSD_HEAD_DIM
    nb = d_inner // LANES
    return dict(
        xs=pl.BlockSpec((L, gw), lambda b, g, c: (b * nc + c, g)),
        B=pl.BlockSpec((L, LANES), lambda b, g, c: (b * nc + c, nb + g)),
        C=pl.BlockSpec((L, LANES), lambda b, g, c: (b * nc + c, nb + G + g)),
        dt=pl.BlockSpec((L, LANES), lambda b, g, c: (b * nc + c, 0)),
        vec=pl.BlockSpec((1, LANES), lambda b, g, c: (0, 0)),
        gvec=pl.BlockSpec((1, gw), lambda b, g, c: (0, g)),
        grp=pl.BlockSpec((L, LANES), lambda b, g, c: (b * nc + c, g)),
        st=pl.BlockSpec((1, hpg // 2, SSD_D_STATE, LANES), lambda b, g, c: ((b * G + g) * nc + c, 0, 0, 0)),
    )


def _grid_step(dims):
    t, total = 0, 1
    for ax, d in enumerate(dims):
        t = t * d + pl.program_id(ax)
        total *= d
    return t, total


def _scan_fwd(xbc, dt_pre, dt_bias, a_log, S, d_inner, name, gather=()):
    T = xbc.shape[0]
    Bn = T // S
    L = SSD_CHUNK
    G = SSD_N_GROUPS
    nc = S // L
    hpg = d_inner // SSD_HEAD_DIM // G
    pairs = hpg // 2
    sp = _scan_specs(nc, d_inner, hpg)
    ng = len(gather)

    def body(*refs):
        xs_ref, b_ref, c_ref, dtp_ref, bias_ref, alog_ref = refs[:6]
        y_ref, st_ref = refs[6 + ng:8 + ng]
        state = refs[8 + 2 * ng]
        g = pl.program_id(1)

        if ng:
            start, forward, finish = _gather_plan(refs[6:6 + ng], refs[8 + ng:8 + 2 * ng], *refs[9 + 2 * ng:])
            t, total = _grid_step((Bn, G, nc))
            pl.when(t == 0)(start)
            pl.when(t == total // 2)(forward)

        @pl.when(pl.program_id(2) == 0)
        def _():
            state[...] = jnp.zeros_like(state)

        dt = _softplus(dtp_ref[...] + bias_ref[...])
        cum = _cumsum_rows(dt * (-jnp.exp(alog_ref[...])))
        Bm = b_ref[...]
        Cm = c_ref[...]
        Gm = _bdot(Cm, Bm, NT)
        tril = _iota((L, L), 1) <= _iota((L, L), 0)
        lane = _iota((L, LANES), 1)
        lane1 = _iota((1, LANES), 1)
        last = _iota((L, 1), 0) == L - 1
        BmT = Bm.T
        for p in range(pairs):
            S_in = state[p]
            st_ref[0, p] = S_in
            x_pair = xs_ref[:, p * LANES:(p + 1) * LANES]
            y_pair = jnp.zeros((L, LANES), F32)
            z_pair = jnp.zeros((L, LANES), F32)
            e_pair = jnp.zeros((L, LANES), F32)
            el_pair = jnp.zeros((1, LANES), F32)
            for k in range(2):
                h = g * hpg + 2 * p + k
                cum_col = _col(cum, lane, h)
                dt_col = _col(dt, lane, h)
                cb = jnp.broadcast_to(cum_col, (L, L))
                decay = jnp.exp(jnp.where(tril, cb - cb.T, NEG_BIG))
                mk = (lane >= SSD_HEAD_DIM) if k else (lane < SSD_HEAD_DIM)
                xd = jnp.where(mk, x_pair * dt_col, 0.0)
                y_pair = y_pair + _bdot(Gm * decay, xd, NN)
                cum_l = jnp.sum(jnp.where(last, cum_col, 0.0), axis=0, keepdims=True)
                e_pair = jnp.where(mk, jnp.exp(cum_col), e_pair)
                z_pair = z_pair + xd * jnp.exp(cum_l - cum_col)
                el_pair = jnp.where((lane1 >= SSD_HEAD_DIM) if k else (lane1 < SSD_HEAD_DIM), jnp.exp(cum_l), el_pair)
            y_pair = y_pair + e_pair * _bdot(Cm, S_in, NN)
            state[p] = S_in * el_pair + _bdot(BmT, z_pair, NN)
            y_ref[:, p * LANES:(p + 1) * LANES] = y_pair

        if ng:
            pl.when(t == total - 1)(finish)

    outs = pl.pallas_call(
        body, name=name, grid=(Bn, G, nc),
        in_specs=[sp['xs'], sp['B'], sp['C'], sp['dt'], sp['vec'], sp['vec']] + [ANY] * ng,
        out_specs=[sp['xs'], sp['st']] + [ANY] * ng,
        out_shape=[jax.ShapeDtypeStruct((T, d_inner), F32),
                   jax.ShapeDtypeStruct((Bn * G * nc, pairs, SSD_D_STATE, LANES), F32)] + _gather_shapes(gather),
        scratch_shapes=[pltpu.VMEM((pairs, SSD_D_STATE, LANES), F32)] + (_gather_sems(ng) if ng else []),
        compiler_params=_cparams(),
    )(xbc, xbc, xbc, dt_pre, dt_bias, a_log, *gather)
    return outs[0], outs[1], list(outs[2:])


def _scan_bwd(xbc, dt_pre, dt_bias, a_log, states, dy, d_lane, S, d_inner, name, exchange=()):
    T = xbc.shape[0]
    Bn = T // S
    L = SSD_CHUNK
    G = SSD_N_GROUPS
    nc = S // L
    hpg = d_inner // SSD_HEAD_DIM // G
    pairs = hpg // 2
    sp = _scan_specs(nc, d_inner, hpg)
    nx = len(exchange)

    def rev(spec):
        im = spec.index_map
        return pl.BlockSpec(spec.block_shape, lambda b, g, c: im(b, g, nc - 1 - c))

    def body(*refs):
        xs_ref, b_ref, c_ref, dtp_ref, bias_ref, alog_ref, st_ref, dy_ref, dl_ref = refs[:9]
        dxs_ref, db_ref, dc_ref, ddt_ref, da_ref = refs[9 + nx:14 + nx]
        dstate = refs[14 + 2 * nx]
        g = pl.program_id(1)

        if nx:
            start, finish = _chips_plan(refs[9:9 + nx], refs[14 + nx:14 + 2 * nx], *refs[15 + 2 * nx:])
            t, total = _grid_step((Bn, G, nc))
            pl.when(t == 0)(start)

        @pl.when(pl.program_id(2) == 0)
        def _():
            dstate[...] = jnp.zeros_like(dstate)

        dt = _softplus(dtp_ref[...] + bias_ref[...])
        cum = _cumsum_rows(dt * (-jnp.exp(alog_ref[...])))
        Bm = b_ref[...]
        Cm = c_ref[...]
        Gm = _bdot(Cm, Bm, NT)
        tril = _iota((L, L), 1) <= _iota((L, L), 0)
        lane = _iota((L, LANES), 1)
        lane1 = _iota((1, LANES), 1)
        last = _iota((L, 1), 0) == L - 1
        CmT = Cm.T
        dG = jnp.zeros((L, L), F32)
        dB = jnp.zeros((L, LANES), F32)
        dC = jnp.zeros((L, LANES), F32)
        dcum = jnp.zeros((L, LANES), F32)
        ddt = jnp.zeros((L, LANES), F32)
        for p in range(pairs):
            S_in = st_ref[0, p]
            dS = dstate[p]
            x_pair = xs_ref[:, p * LANES:(p + 1) * LANES]
            dy_pair = dy_ref[:, p * LANES:(p + 1) * LANES]
            Q = _bdot(Cm, S_in, NN)
            dZ = _bdot(Bm, dS, NN)
            prod = dS * S_in
            e_pair = jnp.zeros((L, LANES), F32)
            z_pair = jnp.zeros((L, LANES), F32)
            el_pair = jnp.zeros((1, LANES), F32)
            dx_pair = dl_ref[:, p * LANES:(p + 1) * LANES] * dy_pair
            for k in range(2):
                h = g * hpg + 2 * p + k
                cum_col = _col(cum, lane, h)
                dt_col = _col(dt, lane, h)
                cb = jnp.broadcast_to(cum_col, (L, L))
                decay = jnp.exp(jnp.where(tril, cb - cb.T, NEG_BIG))
                mk = (lane >= SSD_HEAD_DIM) if k else (lane < SSD_HEAD_DIM)
                mk1 = (lane1 >= SSD_HEAD_DIM) if k else (lane1 < SSD_HEAD_DIM)
                xd = jnp.where(mk, x_pair * dt_col, 0.0)
                dy_k = jnp.where(mk, dy_pair, 0.0)
                Wm = Gm * decay
                dWm = jnp.where(tril, _bdot(dy_k, xd, NT), 0.0)
                dxd = _bdot(Wm.T, dy_k, NN)
                dseg = dWm * Wm
                dG = dG + dWm * decay
                dcum_col = (jnp.sum(dseg, axis=1, keepdims=True)
                            - jnp.sum(dseg.T, axis=1, keepdims=True))
                cum_l = jnp.sum(jnp.where(last, cum_col, 0.0), axis=0, keepdims=True)
                e_col = jnp.exp(cum_col)
                w_col = jnp.exp(cum_l - cum_col)
                el = jnp.exp(cum_l)
                dcum_col = dcum_col + jnp.sum(dy_k * Q, axis=1, keepdims=True) * e_col
                de_e = jnp.sum(dZ * xd, axis=1, keepdims=True) * w_col
                dcum_col = dcum_col - de_e
                dcum_l = (jnp.sum(de_e, axis=0, keepdims=True)
                          + el * jnp.sum(jnp.sum(jnp.where(mk, prod, 0.0), axis=1, keepdims=True),
                                         axis=0, keepdims=True))
                dcum_col = dcum_col + jnp.where(last, dcum_l, 0.0)
                dxd = dxd + jnp.where(mk, dZ * w_col, 0.0)
                dx_pair = dx_pair + dxd * dt_col
                ddt = jnp.where(lane == h, jnp.sum(dxd * x_pair, axis=1, keepdims=True), ddt)
                dcum = jnp.where(lane == h, dcum_col, dcum)
                e_pair = jnp.where(mk, e_col, e_pair)
                z_pair = z_pair + xd * w_col
                el_pair = jnp.where(mk1, el, el_pair)
            dQ = e_pair * dy_pair
            dC = dC + _bdot(dQ, S_in, NT)
            dB = dB + _bdot(z_pair, dS, NT)
            dstate[p] = dS * el_pair + _bdot(CmT, dQ, NN)
            dxs_ref[:, p * LANES:(p + 1) * LANES] = dx_pair
        dc_ref[...] = dC + _bdot(dG, Bm, NN)
        db_ref[...] = dB + _bdot(dG.T, Cm, NN)
        ddt_ref[...] = ddt
        da_ref[...] = _rev_cumsum_rows(dcum)

        if nx:
            pl.when(t == total - 1)(finish)

    grp = jax.ShapeDtypeStruct((T, G * LANES), F32)
    outs = pl.pallas_call(
        body, name=name, grid=(Bn, G, nc),
        in_specs=[rev(sp['xs']), rev(sp['B']), rev(sp['C']), rev(sp['dt']), sp['vec'], sp['vec'],
                  rev(sp['st']), rev(sp['xs']), sp['gvec']] + [ANY] * nx,
        out_specs=[rev(sp['xs']), rev(sp['grp']), rev(sp['grp']), rev(sp['grp']), rev(sp['grp'])] + [ANY] * nx,
        out_shape=[jax.ShapeDtypeStruct((T, d_inner), F32), grp, grp, grp, grp] + _chips_shapes(exchange),
        scratch_shapes=[pltpu.VMEM((pairs, SSD_D_STATE, LANES), F32)] + (_chips_sems(nx) if nx else []),
        compiler_params=_cparams(),
    )(xbc, xbc, xbc, dt_pre, dt_bias, a_log, states, dy, d_lane, *exchange)
    return outs[:5], list(outs[5:])


def _dt_bwd(ddt_g, da_g, dt_pre, dt_bias, a_log, name):
    T = dt_pre.shape[0]
    G = SSD_N_GROUPS
    tm = _tile(T, (512, 256, 128))

    def body(ddt_ref, da_ref, dtp_ref, bias_ref, alog_ref, dx_ref, dbias_ref, dal_ref):
        @pl.when(pl.program_id(0) == 0)
        def _():
            dbias_ref[...] = jnp.zeros_like(dbias_ref)
            dal_ref[...] = jnp.zeros_like(dal_ref)

        ddt = ddt_ref[:, 0:LANES]
        da = da_ref[:, 0:LANES]
        for gi in range(1, G):
            ddt = ddt + ddt_ref[:, gi * LANES:(gi + 1) * LANES]
            da = da + da_ref[:, gi * LANES:(gi + 1) * LANES]
        xv = dtp_ref[...] + bias_ref[...]
        A = -jnp.exp(alog_ref[...])
        dx = (ddt + da * A) * _sigmoid(xv)
        dx_ref[...] = dx.astype(BF16)
        dbias_ref[...] += jnp.sum(dx, axis=0, keepdims=True)
        dal_ref[...] += jnp.sum(da * _softplus(xv), axis=0, keepdims=True) * A

    wide = pl.BlockSpec((tm, G * LANES), lambda i: (i, 0))
    row = pl.BlockSpec((tm, LANES), lambda i: (i, 0))
    vec = pl.BlockSpec((1, LANES), lambda i: (0, 0))
    vsh = jax.ShapeDtypeStruct((1, LANES), F32)
    return pl.pallas_call(
        body, name=name, grid=(T // tm,), in_specs=[wide, wide, row, vec, vec], out_specs=[row, vec, vec],
        out_shape=[jax.ShapeDtypeStruct((T, LANES), BF16), vsh, vsh], compiler_params=_cparams(),
    )(ddt_g, da_g, dt_pre, dt_bias, a_log)


def _gate_fwd(y, xbc, z, d_lane, ng, d_inner, name):
    T = y.shape[0]
    G = SSD_N_GROUPS
    gw = d_inner // G
    tm = _tile(T, (512, 256, 128))

    def body(y_ref, x_ref, z_ref, dl_ref, ng_ref, o_ref):
        zv = z_ref[...]
        y2 = (y_ref[...] + dl_ref[...] * x_ref[...]) * (zv * _sigmoid(zv))
        r = lax.rsqrt(jnp.mean(y2 * y2, axis=1, keepdims=True) + LN_EPS)
        o_ref[...] = (y2 * r * ng_ref[...]).astype(BF16)

    blk = pl.BlockSpec((tm, gw), lambda g, i: (i, g))
    vec = pl.BlockSpec((1, gw), lambda g, i: (0, g))
    return pl.pallas_call(
        body, name=name, grid=(G, T // tm), in_specs=[blk, blk, blk, vec, vec], out_specs=blk,
        out_shape=jax.ShapeDtypeStruct((T, d_inner), BF16), compiler_params=_cparams(),
    )(y, xbc, z, d_lane, ng)


def _gate_bwd(dyn, y, xbc, z, d_lane, ng, d_inner, name):
    T = y.shape[0]
    G = SSD_N_GROUPS
    gw = d_inner // G
    tm = _tile(T, (512, 256, 128))

    def body(dyn_ref, y_ref, x_ref, z_ref, dl_ref, ng_ref, dy_ref, dz_ref, dng_ref, ddl_ref):
        @pl.when(pl.program_id(1) == 0)
        def _():
            dng_ref[...] = jnp.zeros_like(dng_ref)
            ddl_ref[...] = jnp.zeros_like(ddl_ref)

        zv = z_ref[...]
        xv = x_ref[...]
        sz = _sigmoid(zv)
        y1 = y_ref[...] + dl_ref[...] * xv
        y2 = y1 * (zv * sz)
        r = lax.rsqrt(jnp.mean(y2 * y2, axis=1, keepdims=True) + LN_EPS)
        dn = dyn_ref[...].astype(F32)
        dng_ref[...] += jnp.sum(dn * y2 * r, axis=0, keepdims=True)
        do = dn * ng_ref[...]
        dy2 = r * do - y2 * (r * r * r) * jnp.mean(do * y2, axis=1, keepdims=True)
        dz_ref[...] = (dy2 * y1 * sz * (1.0 + zv * (1.0 - sz))).astype(BF16)
        dy1 = dy2 * (zv * sz)
        dy_ref[...] = dy1
        ddl_ref[...] += jnp.sum(dy1 * xv, axis=0, keepdims=True)

    blk = pl.BlockSpec((tm, gw), lambda g, i: (i, g))
    vec = pl.BlockSpec((1, gw), lambda g, i: (0, g))
    vsh = jax.ShapeDtypeStruct((1, d_inner), F32)
    return pl.pallas_call(
        body, name=name, grid=(G, T // tm), in_specs=[blk, blk, blk, blk, vec, vec],
        out_specs=[blk, blk, vec, vec],
        out_shape=[jax.ShapeDtypeStruct((T, d_inner), F32), jax.ShapeDtypeStruct((T, d_inner), BF16), vsh, vsh],
        compiler_params=_cparams(),
    )(dyn, y, xbc, z, d_lane, ng)


def _rms_fwd(x, g, name):
    T, R = x.shape
    tm = _tile(T, (512, 256, 128))

    def body(x_ref, g_ref, y_ref):
        xv = x_ref[...]
        y = xv * lax.rsqrt(jnp.mean(xv * xv, axis=1, keepdims=True) + RMS_EPS) * g_ref[...]
        y_ref[...] = y.astype(BF16)

    row = pl.BlockSpec((tm, R), lambda i: (i, 0))
    vec = pl.BlockSpec((1, R), lambda i: (0, 0))
    return pl.pallas_call(
        body, name=name, grid=(T // tm,), in_specs=[row, vec], out_specs=row,
        out_shape=jax.ShapeDtypeStruct((T, R), BF16), compiler_params=_cparams(),
    )(x, g)


def _rms_bwd(dy, x, g, name):
    T, R = x.shape
    tm = _tile(T, (512, 256, 128))

    def body(dy_ref, x_ref, g_ref, dx_ref, dg_ref):
        @pl.when(pl.program_id(0) == 0)
        def _():
            dg_ref[...] = jnp.zeros_like(dg_ref)

        xv = x_ref[...]
        dyv = dy_ref[...]
        r = lax.rsqrt(jnp.mean(xv * xv, axis=1, keepdims=True) + RMS_EPS)
        dg_ref[...] += jnp.sum(dyv * xv * r, axis=0, keepdims=True)
        do = dyv * g_ref[...]
        dx = r * do - xv * (r * r * r) * jnp.mean(do * xv, axis=1, keepdims=True)
        dx_ref[...] = dx.astype(BF16)

    row = pl.BlockSpec((tm, R), lambda i: (i, 0))
    vec = pl.BlockSpec((1, R), lambda i: (0, 0))
    return pl.pallas_call(
        body, name=name, grid=(T // tm,), in_specs=[row, row, vec], out_specs=[row, vec],
        out_shape=[jax.ShapeDtypeStruct((T, R), BF16), jax.ShapeDtypeStruct((1, R), F32)],
        compiler_params=_cparams(),
    )(dy, x, g)


def _swap_halves(x):
    half = MLA_ROPE // 2
    return jnp.where(_iota(x.shape, 1) < half, pltpu.roll(x, LANES - half, 1), pltpu.roll(x, half, 1))


def _rope(x, cc, ss, *, transpose, sum_heads=False, name):
    T, W = x.shape
    nh = W // LANES
    tm = _tile(T, (512, 256, 128))
    n_out = 1 if sum_heads else nh

    def body(x_ref, cc_ref, ss_ref, o_ref):
        ccv = cc_ref[...]
        ssv = ss_ref[...]
        if sum_heads:
            xv = x_ref[:, 0:LANES]
            for hh in range(1, nh):
                xv = xv + x_ref[:, hh * LANES:(hh + 1) * LANES]
            heads = [xv]
        else:
            heads = [x_ref[:, hh * LANES:(hh + 1) * LANES] for hh in range(nh)]
        for hh, xv in enumerate(heads):
            if transpose:
                r = xv * ccv + _swap_halves(xv * ssv)
            else:
                r = xv * ccv + _swap_halves(xv) * ssv
            r = jnp.where(_iota(r.shape, 1) < MLA_ROPE, r, 0.0)
            o_ref[:, hh * LANES:(hh + 1) * LANES] = r.astype(BF16)

    return pl.pallas_call(
        body, name=name, grid=(T // tm,),
        in_specs=[pl.BlockSpec((tm, W), lambda i: (i, 0)), pl.BlockSpec((tm, LANES), lambda i: (i, 0)),
                  pl.BlockSpec((tm, LANES), lambda i: (i, 0))],
        out_specs=pl.BlockSpec((tm, n_out * LANES), lambda i: (i, 0)),
        out_shape=jax.ShapeDtypeStruct((T, n_out * LANES), BF16), compiler_params=_cparams(),
    )(x, cc, ss)


ATT_BLOCK = 512
ATT_SUB = 256
LOG2E = 1.4426950408889634


def _att_geometry(S):
    tb = _tile(S, (ATT_BLOCK, 256))
    return tb, S // tb, tb // ATT_SUB


def _heads_per_step(nh):
    return 2 if nh % 2 == 0 else 1


def _att_scores(a, b, diag, kv_major, off):
    s = _bdot(a, b, NT)
    if diag:
        q_ax, k_ax = (1, 0) if kv_major else (0, 1)
        s = jnp.where(_iota(s.shape, k_ax) <= _iota(s.shape, q_ax) + off, s, NEG_BIG)
    return s


def _attention_fwd(qn, qr, kn, kr, v, S, scale, name):
    T, W = qn.shape
    nh = W // LANES
    Bn = T // S
    tb, n, C = _att_geometry(S)
    pairs = [(i, j) for i in range(n) for j in range(i + 1)]
    it = jnp.asarray([p[0] for p in pairs], jnp.int32)
    jt = jnp.asarray([p[1] for p in pairs], jnp.int32)
    c2 = scale * LOG2E

    HP = _heads_per_step(nh)

    def body(it_ref, jt_ref, qn_ref, qr_ref, kn_ref, kr_ref, v_ref, o_ref, ob_ref, lse_ref, m_sc, l_sc, acc):
        p = pl.program_id(2)
        i = it_ref[p]
        j = jt_ref[p]

        @pl.when(j == 0)
        def _():
            m_sc[...] = jnp.full_like(m_sc, NEG_BIG)
            l_sc[...] = jnp.zeros_like(l_sc)
            acc[...] = jnp.zeros_like(acc)

        def step(diag):
            for hh in range(HP):
                hs = pl.ds(hh * LANES, LANES)
                qc = jnp.concatenate([qn_ref[:, hs], qr_ref[:, hs]], axis=1)
                kc = jnp.concatenate([kn_ref[:, hs], kr_ref[...]], axis=1)
                st = _att_scores(kc, qc, diag, True, 0)
                m_old = m_sc[hh]
                m_new = jnp.maximum(m_old, jnp.max(st, axis=0, keepdims=True))
                pt = jnp.exp2((st - m_new) * c2)
                corr = jnp.exp2((m_old - m_new) * c2)
                l_sc[hh] = corr * l_sc[hh] + jnp.sum(pt, axis=0, keepdims=True)
                acc[hh] = corr * acc[hh] + _bdot(v_ref[:, hs].T, pt, NN)
                m_sc[hh] = m_new

        @pl.when(j < i)
        def _():
            step(False)

        @pl.when(j == i)
        def _():
            step(True)
            for hh in range(HP):
                hs = pl.ds(hh * LANES, LANES)
                ov = (acc[hh] / l_sc[hh]).T
                o_ref[:, hs] = ov
                ob_ref[:, hs] = ov.astype(BF16)
                lse_row = m_sc[hh] * scale + jnp.log(l_sc[hh])
                lse_ref[:, hs] = jnp.broadcast_to(lse_row, (LANES, tb)).T

    qspec = pl.BlockSpec((tb, HP * LANES), lambda b, h, p, it_, jt_: (b * n + it_[p], h))
    kspec = pl.BlockSpec((tb, HP * LANES), lambda b, h, p, it_, jt_: (b * n + jt_[p], h))
    krspec = pl.BlockSpec((tb, LANES), lambda b, h, p, it_, jt_: (b * n + jt_[p], 0))
    return pl.pallas_call(
        body, name=name,
        grid_spec=pltpu.PrefetchScalarGridSpec(
            num_scalar_prefetch=2, grid=(Bn, nh // HP, len(pairs)),
            in_specs=[qspec, qspec, kspec, krspec, kspec], out_specs=[qspec, qspec, qspec],
            scratch_shapes=[pltpu.VMEM((HP, 1, tb), F32), pltpu.VMEM((HP, 1, tb), F32),
                            pltpu.VMEM((HP, LANES, tb), F32)]),
        out_shape=[jax.ShapeDtypeStruct((T, W), F32), jax.ShapeDtypeStruct((T, W), BF16),
                   jax.ShapeDtypeStruct((T, W), F32)],
        compiler_params=_cparams(),
    )(it, jt, qn, qr, kn, kr, v)


def _attention_dq(qn, qr, kn, kr, v, o, do, lse, S, scale, name):
    T, W = qn.shape
    nh = W // LANES
    Bn = T // S
    tb, n, C = _att_geometry(S)
    pairs = [(i, j) for i in range(n) for j in range(i + 1)]
    it = jnp.asarray([p[0] for p in pairs], jnp.int32)
    jt = jnp.asarray([p[1] for p in pairs], jnp.int32)
    c2 = scale * LOG2E
    HP = _heads_per_step(nh)

    def body(it_ref, jt_ref, qn_ref, qr_ref, kn_ref, kr_ref, v_ref, o_ref, do_ref, lse_ref,
             dqn_ref, dqr_ref, acc, di_sc, lse_sc):
        p = pl.program_id(2)
        i = it_ref[p]
        j = jt_ref[p]

        @pl.when(j == 0)
        def _():
            acc[...] = jnp.zeros_like(acc)
            for hh in range(HP):
                hs = pl.ds(hh * LANES, LANES)
                di_sc[hh] = jnp.sum(do_ref[:, hs] * o_ref[:, hs], axis=1, keepdims=True)
                lse_sc[hh] = jnp.max(lse_ref[:, hs], axis=1, keepdims=True) * LOG2E

        def step(diag):
            for hh in range(HP):
                hs = pl.ds(hh * LANES, LANES)
                qc = jnp.concatenate([qn_ref[:, hs], qr_ref[:, hs]], axis=1)
                for c in range(C):
                    r0 = c * ATT_SUB if diag else 0
                    rows = pl.ds(r0, tb - r0)
                    ks = pl.ds(c * ATT_SUB, ATT_SUB)
                    kc = jnp.concatenate([kn_ref[ks, hs], kr_ref[ks, :]], axis=1)
                    s = _att_scores(qc[r0:], kc, diag, False, 0)
                    pr = jnp.exp2(s * c2 - lse_sc[hh, rows, :])
                    dp = _bdot(do_ref[rows, hs], v_ref[ks, hs], NT)
                    ds = pr * (dp - di_sc[hh, rows, :]) * scale
                    acc[hh, rows, :] += _bdot(ds, kc, NN)

        @pl.when(j < i)
        def _():
            step(False)

        @pl.when(j == i)
        def _():
            step(True)
            for hh in range(HP):
                hs = pl.ds(hh * LANES, LANES)
                dqn_ref[:, hs] = acc[hh, :, 0:LANES].astype(BF16)
                dqr_ref[:, hs] = acc[hh, :, LANES:2 * LANES]

    qspec = pl.BlockSpec((tb, HP * LANES), lambda b, h, p, it_, jt_: (b * n + it_[p], h))
    kspec = pl.BlockSpec((tb, HP * LANES), lambda b, h, p, it_, jt_: (b * n + jt_[p], h))
    krspec = pl.BlockSpec((tb, LANES), lambda b, h, p, it_, jt_: (b * n + jt_[p], 0))
    return pl.pallas_call(
        body, name=name,
        grid_spec=pltpu.PrefetchScalarGridSpec(
            num_scalar_prefetch=2, grid=(Bn, nh // HP, len(pairs)),
            in_specs=[qspec, qspec, kspec, krspec, kspec, qspec, qspec, qspec], out_specs=[qspec, qspec],
            scratch_shapes=[pltpu.VMEM((HP, tb, 2 * LANES), F32), pltpu.VMEM((HP, tb, 1), F32),
                            pltpu.VMEM((HP, tb, 1), F32)]),
        out_shape=[jax.ShapeDtypeStruct((T, W), BF16), jax.ShapeDtypeStruct((T, W), F32)],
        compiler_params=_cparams(),
    )(it, jt, qn, qr, kn, kr, v, o, do, lse)


def _attention_dkv(qn, qr, kn, kr, v, o, do, lse, S, scale, name):
    T, W = qn.shape
    nh = W // LANES
    Bn = T // S
    tb, n, C = _att_geometry(S)
    HP = _heads_per_step(nh)
    triples = [(j, h, i) for j in range(n) for h in range(nh // HP) for i in range(j, n)]
    jt = jnp.asarray([t[0] for t in triples], jnp.int32)
    ht = jnp.asarray([t[1] for t in triples], jnp.int32)
    it = jnp.asarray([t[2] for t in triples], jnp.int32)
    c2 = scale * LOG2E

    def as_row(col):
        return jnp.broadcast_to(col, (ATT_SUB, LANES)).T[0:1, :]

    def body(jt_ref, ht_ref, it_ref, qn_ref, qr_ref, kn_ref, kr_ref, v_ref, o_ref, do_ref, lse_ref,
             dkn_ref, dv_ref, dkr_ref, akc, av):
        p = pl.program_id(1)
        j = jt_ref[p]
        h = ht_ref[p]
        i = it_ref[p]

        @pl.when(jnp.logical_and(h == 0, i == j))
        def _():
            dkr_ref[...] = jnp.zeros_like(dkr_ref)

        @pl.when(i == j)
        def _():
            akc[...] = jnp.zeros_like(akc)
            av[...] = jnp.zeros_like(av)

        def step(diag):
            for hh in range(HP):
                hs = pl.ds(hh * LANES, LANES)
                kc = jnp.concatenate([kn_ref[:, hs], kr_ref[...]], axis=1)
                for c in range(C):
                    rows = pl.ds(c * ATT_SUB, ATT_SUB)
                    k1 = (c + 1) * ATT_SUB if diag else tb
                    ks = pl.ds(0, k1)
                    qc = jnp.concatenate([qn_ref[rows, hs], qr_ref[rows, hs]], axis=1)
                    st = _att_scores(kc[:k1], qc, diag, True, c * ATT_SUB)
                    dov = do_ref[rows, hs]
                    lse_row = as_row(jnp.max(lse_ref[rows, hs], axis=1, keepdims=True)) * LOG2E
                    di_row = as_row(jnp.sum(dov * o_ref[rows, hs], axis=1, keepdims=True))
                    pt = jnp.exp2(st * c2 - lse_row)
                    dst = pt * (_bdot(v_ref[ks, hs], dov, NT) - di_row) * scale
                    av[ks, hs] += _bdot(pt, dov, NN)
                    akc[hh, ks, :] += _bdot(dst, qc, NN)

        @pl.when(i > j)
        def _():
            step(False)

        @pl.when(i == j)
        def _():
            step(True)

        @pl.when(i == n - 1)
        def _():
            for hh in range(HP):
                dkn_ref[:, pl.ds(hh * LANES, LANES)] = akc[hh, :, 0:LANES].astype(BF16)
                dkr_ref[...] += akc[hh, :, LANES:2 * LANES]
            dv_ref[...] = av[...].astype(BF16)

    qspec = pl.BlockSpec((tb, HP * LANES), lambda b, p, jt_, ht_, it_: (b * n + it_[p], ht_[p]))
    kspec = pl.BlockSpec((tb, HP * LANES), lambda b, p, jt_, ht_, it_: (b * n + jt_[p], ht_[p]))
    krspec = pl.BlockSpec((tb, LANES), lambda b, p, jt_, ht_, it_: (b * n + jt_[p], 0))
    return pl.pallas_call(
        body, name=name,
        grid_spec=pltpu.PrefetchScalarGridSpec(
            num_scalar_prefetch=3, grid=(Bn, len(triples)),
            in_specs=[qspec, qspec, kspec, krspec, kspec, qspec, qspec, qspec],
            out_specs=[kspec, kspec, krspec],
            scratch_shapes=[pltpu.VMEM((HP, tb, 2 * LANES), F32), pltpu.VMEM((tb, HP * LANES), F32)]),
        out_shape=[jax.ShapeDtypeStruct((T, W), BF16), jax.ShapeDtypeStruct((T, W), BF16),
                   jax.ShapeDtypeStruct((T, LANES), F32)],
        compiler_params=_cparams(),
    )(jt, ht, it, qn, qr, kn, kr, v, o, do, lse)


def _pad_cols(w, n):
    return jnp.pad(w, ((0, 0), (0, n - w.shape[1])))


def _local_step(x, positions, w, target, late=None, reduce_early=None):
    w = dict(w)
    Bn, S, D = x.shape
    T = Bn * S
    depth = w['ln_mix_g'].shape[0]
    alpha = (2 * depth) ** 0.25
    d_inner = w['ssd_out_proj'].shape[1]
    n_heads_ssd = d_inner // SSD_HEAD_DIM
    G = SSD_N_GROUPS
    gn = G * SSD_D_STATE
    conv_dim = d_inner + 2 * gn
    hpg = n_heads_ssd // G
    nh = D // MLA_NOPE
    kv_rank = w['kv_norm_g'].shape[0]
    q_rank = w['q_norm_g'].shape[1]
    H = w['ffn_conv_b'].shape[1] // 2
    scale = (MLA_NOPE + MLA_ROPE) ** -0.5
    assert S % SSD_CHUNK == 0 and hpg % 2 == 0 and SSD_D_STATE == LANES and n_heads_ssd <= LANES

    X = x.reshape(T, D)
    tgt = target.reshape(T, D)

    inv_freq = 1.0 / (ROPE_THETA ** (jnp.arange(0, MLA_ROPE, 2, dtype=F32) / MLA_ROPE))
    ang = positions.reshape(T).astype(F32)[:, None] * inv_freq
    cos, sin = jnp.cos(ang), jnp.sin(ang)
    zpad = jnp.zeros((T, LANES - MLA_ROPE), F32)
    cc = jnp.concatenate([cos, cos, zpad], axis=1)
    ss = jnp.concatenate([-sin, sin, zpad], axis=1)

    w_in = w['ssd_in_proj'][0]
    w_z = w_in[:, :d_inner]
    w_xbc = w_in[:, d_inner:d_inner + conv_dim]
    w_dt = _pad_cols(w_in[:, d_inner + conv_dim:], LANES)
    w_out = w['ssd_out_proj'][0]
    ssd_cw = w['ssd_conv_w'][0]
    ssd_cb = w['ssd_conv_b']
    dt_bias = _pad_cols(w['ssd_dt_bias'], LANES)
    a_log = _pad_cols(w['ssd_A_log'], LANES)
    d_lane = jnp.repeat(w['ssd_D'], SSD_HEAD_DIM, axis=1)
    ssd_ng = w['ssd_norm_g']

    Xb = X.astype(BF16)
    z = _mm(Xb, w_z, name='ssd_in_z')
    xbc_pre = _mm(Xb, w_xbc, name='ssd_in_xbc')
    dt_pre = _mm(Xb, w_dt, name='ssd_in_dt')
    xbc, xbc_conv = _ssd_conv_fwd(xbc_pre, ssd_cw, ssd_cb, S, name='ssd_conv')
    y_scan, states, gathered = _scan_fwd(xbc, dt_pre, dt_bias, a_log, S, d_inner, name='ssd_scan',
                                         gather=late[0] if late else ())
    if late:
        w.update(late[1](gathered))

    w_kvc = w['kv_down_proj'][:, :kv_rank]
    w_kvr = _pad_cols(w['kv_down_proj'][:, kv_rank:], LANES)
    kv_g = w['kv_norm_g'].reshape(1, kv_rank)
    w_uk = w['kv_up_k']
    w_uv = w['kv_up_v']
    w_qd = w['q_down_proj'][0]
    q_g = w['q_norm_g']
    qu = w['q_up_proj'][0].reshape(q_rank, nh, MLA_NOPE + MLA_ROPE)
    w_qn = qu[:, :, :MLA_NOPE].reshape(q_rank, nh * LANES)
    w_qr = jnp.pad(qu[:, :, MLA_NOPE:], ((0, 0), (0, 0), (0, LANES - MLA_ROPE))).reshape(q_rank, nh * LANES)
    w_ao = w['attn_out_proj'][0]

    def ffn_parts(i):
        return dict(wu=w['ffn_up'][i], cw=w['ffn_conv_w'][i], cb=w['ffn_conv_b'][i:i + 1], wd=w['ffn_down'][i])

    def ln(name, i):
        return w[name][i:i + 1]

    def ffn_fwd(hb, i):
        f = ffn_parts(i)
        u = _mm(hb, f['wu'], name=f'ffn{i}_up')
        a = _ffn_act_fwd(u, f['cw'], f['cb'], S, name=f'ffn{i}_act')
        return _mm(a, f['wd'], name=f'ffn{i}_down'), (u, a)

    yn = _gate_fwd(y_scan, xbc, z, d_lane, ssd_ng, d_inner, name='ssd_gate')
    mix0 = _mm(yn, w_out, name='ssd_out')
    h1, s1, h1b = _ln_fwd(X, mix0, ln('ln_mix_g', 0), ln('ln_mix_b', 0), alpha, name='ln_mix0')
    ff0, ffn0_saved = ffn_fwd(h1b, 0)
    h2, s2, h2b = _ln_fwd(h1, ff0, ln('ln_ffn_g', 0), ln('ln_ffn_b', 0), alpha, name='ln_ffn0')

    ckv = _mm(h2b, w_kvc, name='kv_down_c')
    kr_pre = _mm(h2b, w_kvr, name='kv_down_r')
    ckvn = _rms_fwd(ckv, kv_g, name='kv_norm')
    kr = _rope(kr_pre, cc, ss, transpose=False, name='k_rope')
    kn = _mm(ckvn, w_uk, out_dtype=BF16, name='kv_up_k')
    v = _mm(ckvn, w_uv, out_dtype=BF16, name='kv_up_v')
    cq_pre = _mm(h2b, w_qd, name='q_down')
    cq = _rms_fwd(cq_pre, q_g, name='q_norm')
    qn = _mm(cq, w_qn, out_dtype=BF16, name='q_up_n')
    qr_pre = _mm(cq, w_qr, name='q_up_r')
    qr = _rope(qr_pre, cc, ss, transpose=False, name='q_rope')
    o, ob, lse = _attention_fwd(qn, qr, kn, kr, v, S, scale, name='attn_fwd')
    mix1 = _mm(ob, w_ao, name='attn_out')
    h3, s3, h3b = _ln_fwd(h2, mix1, ln('ln_mix_g', 1), ln('ln_mix_b', 1), alpha, name='ln_mix1')
    ff1, ffn1_saved = ffn_fwd(h3b, 1)
    h4, s4, _ = _ln_fwd(h3, ff1, ln('ln_ffn_g', 1), ln('ln_ffn_b', 1), alpha, name='ln_ffn1')
    sq, dh4 = _loss_head(h4, tgt, name='loss_head')

    g = {}

    def ffn_bwd(ds, dsb, hb_in, saved, i):
        f = ffn_parts(i)
        u, a = saved
        d_wd = _mm(a, dsb, ta=True, name=f'ffn{i}_down_dw')
        da = _mm(dsb, f['wd'], tb=True, out_dtype=BF16, name=f'ffn{i}_down_dx')
        dug, duv, dwg, dwv, dbg, dbv = _ffn_act_bwd(u, da, f['cw'], f['cb'], S, name=f'ffn{i}_act_bwd')
        d_wu = _mm(hb_in, dug, ta=True, name=f'ffn{i}_up_g_dw', into=lax.empty((D, 2 * H), F32))
        d_wu = _mm(hb_in, duv, ta=True, name=f'ffn{i}_up_v_dw', into=d_wu, into_col=H)
        dh = _mm(dug, f['wu'], tb=True, add=ds, add_scale=alpha, name=f'ffn{i}_up_g_dx')
        dh = _mm(duv, f['wu'], tb=True, add=dh, name=f'ffn{i}_up_v_dx', b_koff=H)
        return dh, d_wd, d_wu, jnp.concatenate([dwg, dwv], axis=1), jnp.concatenate([dbg, dbv], axis=1)

    ds4, ds4b, dg_f1, db_f1 = _ln_bwd(dh4, s4, ln('ln_ffn_g', 1), name='ln_ffn1_bwd')
    dh3, d_wd1, d_wu1, d_fcw1, d_fcb1 = ffn_bwd(ds4, ds4b, h3b, ffn1_saved, 1)
    ds3, ds3b, dg_m1, db_m1 = _ln_bwd(dh3, s3, ln('ln_mix_g', 1), name='ln_mix1_bwd')

    g['attn_out_proj'] = _mm(ob, ds3b, ta=True, name='attn_out_dw')[None]
    do = _mm(ds3b, w_ao, tb=True, name='attn_out_dx')
    dqn, dqr = _attention_dq(qn, qr, kn, kr, v, o, do, lse, S, scale, name='attn_bwd_dq')
    dkn, dv, dkr = _attention_dkv(qn, qr, kn, kr, v, o, do, lse, S, scale, name='attn_bwd_dkv')
    dqr_pre = _rope(dqr, cc, ss, transpose=True, name='q_rope_bwd')
    dkr_pre = _rope(dkr, cc, ss, transpose=True, name='k_rope_bwd')

    d_wqn = _mm(cq, dqn, ta=True, name='q_up_n_dw').reshape(q_rank, nh, LANES)
    d_wqr = _mm(cq, dqr_pre, ta=True, name='q_up_r_dw').reshape(q_rank, nh, LANES)[:, :, :MLA_ROPE]
    g['q_up_proj'] = jnp.concatenate([d_wqn, d_wqr], axis=2).reshape(1, q_rank, nh * (MLA_NOPE + MLA_ROPE))
    dcq = _mm(dqn, w_qn, tb=True, name='q_up_n_dx')
    dcq = _mm(dqr_pre, w_qr, tb=True, add=dcq, name='q_up_r_dx')
    dcq_pre, d_qg = _rms_bwd(dcq, cq_pre, q_g, name='q_norm_bwd')
    g['q_norm_g'] = d_qg
    g['q_down_proj'] = _mm(h2b, dcq_pre, ta=True, name='q_down_dw')[None]

    g['kv_up_k'] = _mm(ckvn, dkn, ta=True, name='kv_up_k_dw')
    g['kv_up_v'] = _mm(ckvn, dv, ta=True, name='kv_up_v_dw')
    dckvn = _mm(dkn, w_uk, tb=True, name='kv_up_k_dx')
    dckvn = _mm(dv, w_uv, tb=True, add=dckvn, name='kv_up_v_dx')
    dckv, d_kvg = _rms_bwd(dckvn, ckv, kv_g, name='kv_norm_bwd')
    g['kv_norm_g'] = d_kvg.reshape(kv_rank)
    g['kv_down_proj'] = jnp.concatenate(
        [_mm(h2b, dckv, ta=True, name='kv_down_c_dw'),
         _mm(h2b, dkr_pre, ta=True, name='kv_down_r_dw')[:, :MLA_ROPE]], axis=1)

    dh2 = _mm(dcq_pre, w_qd, tb=True, add=ds3, add_scale=alpha, name='q_down_dx')
    dh2 = _mm(dckv, w_kvc, tb=True, add=dh2, name='kv_down_c_dx')
    dh2 = _mm(dkr_pre, w_kvr, tb=True, add=dh2, name='kv_down_r_dx')

    ds2, ds2b, dg_f0, db_f0 = _ln_bwd(dh2, s2, ln('ln_ffn_g', 0), name='ln_ffn0_bwd')
    dh1, d_wd0, d_wu0, d_fcw0, d_fcb0 = ffn_bwd(ds2, ds2b, h1b, ffn0_saved, 0)
    ds1, ds1b, dg_m0, db_m0 = _ln_bwd(dh1, s1, ln('ln_mix_g', 0), name='ln_mix0_bwd')

    g['ssd_out_proj'] = _mm(yn, ds1b, ta=True, name='ssd_out_dw')[None]
    dyn = _mm(ds1b, w_out, tb=True, out_dtype=BF16, name='ssd_out_dx')
    dy1, dz, d_ng, d_dl = _gate_bwd(dyn, y_scan, xbc, z, d_lane, ssd_ng, d_inner, name='ssd_gate_bwd')
    g['ffn_up'] = jnp.stack([d_wu0, d_wu1])
    g['ffn_down'] = jnp.stack([d_wd0, d_wd1])
    (dxs, dB, dC, ddt_g, da_g), early = _scan_bwd(xbc, dt_pre, dt_bias, a_log, states, dy1, d_lane, S, d_inner,
                                                  name='ssd_scan_bwd',
                                                  exchange=reduce_early(g) if reduce_early else ())
    ddt_pre, d_bias, d_alog = _dt_bwd(ddt_g, da_g, dt_pre, dt_bias, a_log, name='ssd_dt_bwd')
    dxbc_pre, d_scw, d_scb = _ssd_conv_bwd(xbc_pre, xbc_conv, [dxs, dB, dC], ssd_cw, S, name='ssd_conv_bwd')
    g['ssd_in_proj'] = jnp.concatenate(
        [_mm(Xb, dz, ta=True, name='ssd_in_z_dw'), _mm(Xb, dxbc_pre, ta=True, name='ssd_in_xbc_dw'),
         _mm(Xb, ddt_pre, ta=True, name='ssd_in_dt_dw')[:, :n_heads_ssd]], axis=1)[None]
    dx = _mm(dz, w_z, tb=True, add=ds1, add_scale=alpha, name='ssd_in_z_dx')
    dx = _mm(dxbc_pre, w_xbc, tb=True, add=dx, name='ssd_in_xbc_dx')
    dx = _mm(ddt_pre, w_dt, tb=True, add=dx, name='ssd_in_dt_dx')

    g['ssd_conv_w'] = d_scw[None]
    g['ssd_conv_b'] = d_scb
    g['ssd_dt_bias'] = d_bias[:, :n_heads_ssd]
    g['ssd_A_log'] = d_alog[:, :n_heads_ssd]
    g['ssd_D'] = jnp.sum(d_dl.reshape(1, n_heads_ssd, SSD_HEAD_DIM), axis=2)
    g['ssd_norm_g'] = d_ng
    g['ffn_conv_w'] = jnp.stack([d_fcw0, d_fcw1])
    g['ffn_conv_b'] = jnp.concatenate([d_fcb0, d_fcb1], axis=0)
    g['ln_mix_g'] = jnp.concatenate([dg_m0, dg_m1], axis=0)
    g['ln_mix_b'] = jnp.concatenate([db_m0, db_m1], axis=0)
    g['ln_ffn_g'] = jnp.concatenate([dg_f0, dg_f1], axis=0)
    g['ln_ffn_b'] = jnp.concatenate([db_f0, db_f1], axis=0)
    return sq, dx.reshape(Bn, S, D), g, early


ANY = pl.BlockSpec(memory_space=pl.ANY)


def _all_gather(xs, name):
    n = len(xs)

    def body(*refs):
        start, forward, finish = _gather_plan(refs[:n], refs[n:2 * n], *refs[2 * n:])
        start()
        forward()
        finish()

    return pl.pallas_call(
        body, name=name, out_shape=_gather_shapes(xs), in_specs=[ANY] * n, out_specs=[ANY] * n,
        scratch_shapes=_gather_sems(n),
    )(*xs)


def _gather_shapes(xs):
    return [jax.ShapeDtypeStruct((N_DEV,) + a.shape, a.dtype) for a in xs]


def _gather_sems(n):
    return [pltpu.SemaphoreType.DMA((7 * n,)), pltpu.SemaphoreType.DMA((7 * n,)), pltpu.SemaphoreType.DMA((n,))]


def _gather_plan(x_refs, out_refs, send_sems, recv_sems, local_sems):
    n = len(x_refs)
    x, y, c = lax.axis_index("x"), lax.axis_index("y"), lax.axis_index("c")
    me, sibling = (x, y, c), (x, y, 1 - c)
    chips = [(1 - x, y), (x, 1 - y), (1 - x, 1 - y)]

    def rows(i, px, py, pc):
        return out_refs[i].at[4 * px + 2 * py + pc]

    def copy(i, k, block, to, own=False):
        return pltpu.make_async_remote_copy(
            src_ref=x_refs[i] if own else rows(i, *block), dst_ref=rows(i, *block),
            send_sem=send_sems.at[7 * i + k], recv_sem=recv_sems.at[7 * i + k],
            device_id=to, device_id_type=MESH)

    def mine():
        return [pltpu.make_async_copy(x_refs[i], rows(i, *me), local_sems.at[i]) for i in range(n)]

    def first():
        out = []
        for i in range(n):
            out.append(copy(i, 0, me, sibling, own=True))
            out += [copy(i, 1 + j, me, (*chip, c), own=True) for j, chip in enumerate(chips)]
        return out

    def passed():
        return [copy(i, 4 + j, (*chip, c), sibling) for j, chip in enumerate(chips) for i in range(n)]

    def start():
        for cp in mine() + first():
            cp.start()

    def forward():
        for j, chip in enumerate(chips):
            for i in range(n):
                copy(i, 1 + j, (*chip, c), me).wait_recv()
                copy(i, 4 + j, (*chip, c), sibling).start()

    def finish():
        for i in range(n):
            copy(i, 0, sibling, me).wait_recv()
            for j, chip in enumerate(chips):
                copy(i, 4 + j, (*chip, 1 - c), me).wait_recv()
        for cp in first() + passed():
            cp.wait_send()
        for cp in mine():
            cp.wait()

    return start, forward, finish


def _exchange_sibling(gs, name):
    n = len(gs)

    def body(*refs):
        g_refs, r_refs = refs[:n], refs[n:2 * n]
        send_sems, recv_sems = refs[2 * n:]
        x, y, c = lax.axis_index("x"), lax.axis_index("y"), lax.axis_index("c")
        copies = [pltpu.make_async_remote_copy(
            src_ref=g_refs[i].at[2 * k + (1 - c)], dst_ref=r_refs[i].at[k], send_sem=send_sems.at[4 * i + k],
            recv_sem=recv_sems.at[4 * i + k], device_id=(x, y, 1 - c), device_id_type=MESH)
            for i in range(n) for k in range(4)]
        for cp in copies:
            cp.start()
        for cp in copies:
            cp.wait()

    return pl.pallas_call(
        body, name=name, out_shape=[jax.ShapeDtypeStruct((4,) + g.shape[1:], g.dtype) for g in gs],
        in_specs=[ANY] * n, out_specs=[ANY] * n,
        scratch_shapes=[pltpu.SemaphoreType.DMA((4 * n,)), pltpu.SemaphoreType.DMA((4 * n,))],
    )(*gs)


def _exchange_chips(parts, name):
    n = len(parts)

    def body(*refs):
        start, finish = _chips_plan(refs[:n], refs[n:2 * n], *refs[2 * n:])
        start()
        finish()

    return pl.pallas_call(
        body, name=name, out_shape=_chips_shapes(parts), in_specs=[ANY] * n, out_specs=[ANY] * n,
        scratch_shapes=_chips_sems(n),
    )(*parts)


def _chips_shapes(parts):
    return [jax.ShapeDtypeStruct((3,) + p.shape[1:], p.dtype) for p in parts]


def _chips_sems(n):
    return [pltpu.SemaphoreType.DMA((3 * n,)), pltpu.SemaphoreType.DMA((3 * n,))]


def _chips_plan(p_refs, r_refs, send_sems, recv_sems):
    n = len(p_refs)
    x, y, c = lax.axis_index("x"), lax.axis_index("y"), lax.axis_index("c")
    chips = [(1 - x, y), (x, 1 - y), (1 - x, 1 - y)]

    def copies():
        return [pltpu.make_async_remote_copy(
            src_ref=p_refs[i].at[2 * cx + cy], dst_ref=r_refs[i].at[j], send_sem=send_sems.at[3 * i + j],
            recv_sem=recv_sems.at[3 * i + j], device_id=(cx, cy, c), device_id_type=MESH)
            for i in range(n) for j, (cx, cy) in enumerate(chips)]

    def start():
        for cp in copies():
            cp.start()

    def finish():
        for cp in copies():
            cp.wait()

    return start, finish


def _row_tile(rows, cols):
    want = max(8, (256 * 1024) // cols)
    for t in range(min(rows, want) // 8 * 8, 7, -8):
        if rows % t == 0:
            return t
    return rows


def _add_sibling(gfull, r1, core, name):
    _, rows, lanes = gfull.shape
    tr = _row_tile(rows, lanes)

    def body(c_ref, g_ref, r_ref, o_ref):
        o_ref[...] = g_ref[...] + r_ref[...]

    return pl.pallas_call(
        body, name=name,
        grid_spec=pltpu.PrefetchScalarGridSpec(
            num_scalar_prefetch=1, grid=(4, rows // tr),
            in_specs=[pl.BlockSpec((1, tr, lanes), lambda k, r, c_ref: (2 * k + c_ref[0], r, 0)),
                      pl.BlockSpec((1, tr, lanes), lambda k, r, c_ref: (k, r, 0))],
            out_specs=pl.BlockSpec((1, tr, lanes), lambda k, r, c_ref: (k, r, 0))),
        out_shape=jax.ShapeDtypeStruct((4, rows, lanes), F32), compiler_params=_cparams(),
    )(core, gfull, r1)


def _adam_math(wv, gv, mv, vv):
    m = ADAM_B1 * mv + (1.0 - ADAM_B1) * gv
    v = ADAM_B2 * vv + (1.0 - ADAM_B2) * (gv * gv)
    m_hat = m / (1.0 - ADAM_B1 ** ADAM_STEP)
    v_hat = v / (1.0 - ADAM_B2 ** ADAM_STEP)
    delta = -ADAM_LR * (m_hat / (jnp.sqrt(v_hat) + ADAM_EPS) + ADAM_WD * wv)
    return delta, m, v


def _adam_sharded(part, r2, chip, wts, m, v, name):
    rows, lanes = wts.shape
    tr = _row_tile(rows, lanes)

    def body(k_ref, p_ref, r_ref, w_ref, m_ref, v_ref, g_ref, d_ref, mo_ref, vo_ref):
        gv = p_ref[0] + r_ref[0] + r_ref[1] + r_ref[2]
        delta, mn, vn = _adam_math(w_ref[...], gv, m_ref[...], v_ref[...])
        g_ref[...] = gv
        d_ref[...] = delta
        mo_ref[...] = mn
        vo_ref[...] = vn

    flat = pl.BlockSpec((tr, lanes), lambda r, k_ref: (r, 0))
    sh = jax.ShapeDtypeStruct((rows, lanes), F32)
    return pl.pallas_call(
        body, name=name,
        grid_spec=pltpu.PrefetchScalarGridSpec(
            num_scalar_prefetch=1, grid=(rows // tr,),
            in_specs=[pl.BlockSpec((1, tr, lanes), lambda r, k_ref: (k_ref[0], r, 0)),
                      pl.BlockSpec((3, tr, lanes), lambda r, k_ref: (0, r, 0)), flat, flat, flat],
            out_specs=[flat, flat, flat, flat]),
        out_shape=[sh, sh, sh, sh], compiler_params=_cparams(),
    )(chip, part, r2, wts, m, v)


def _adam_replicated(gall, wts, m, v, name):
    rows, lanes = wts.shape

    def body(ga_ref, w_ref, m_ref, v_ref, g_ref, d_ref, mo_ref, vo_ref):
        gv = ga_ref[0]
        for k in range(1, N_DEV):
            gv = gv + ga_ref[k]
        delta, mn, vn = _adam_math(w_ref[...], gv, m_ref[...], v_ref[...])
        g_ref[...] = gv
        d_ref[...] = delta
        mo_ref[...] = mn
        vo_ref[...] = vn

    sh = jax.ShapeDtypeStruct((rows, lanes), F32)
    return pl.pallas_call(body, name=name, out_shape=[sh, sh, sh, sh], compiler_params=_cparams())(gall, wts, m, v)


def _pack(arrs, dtype, row_mult):
    flat = jnp.concatenate([a.reshape(-1).astype(dtype) for a in arrs])
    n = flat.shape[0]
    unit = LANES * row_mult
    total = -(-n // unit) * unit
    return jnp.pad(flat, (0, total - n)).reshape(total // LANES, LANES)


def _unpack(flat2d, shapes):
    flat = flat2d.reshape(-1)
    out, off = [], 0
    for shp in shapes:
        n = math.prod(shp)
        out.append(flat[off:off + n].reshape(shp))
        off += n
    return out


def _unpack_gathered(gathered, shapes, axes):
    flat = gathered.reshape(N_DEV, -1)
    out, off = [], 0
    for shp, ax in zip(shapes, axes):
        n = math.prod(shp)
        seg = flat[:, off:off + n].reshape((N_DEV,) + tuple(shp))
        out.append(jnp.concatenate([seg[i] for i in range(N_DEV)], axis=ax))
        off += n
    return out


def _pack_chunks(fulls, axes, row_mult):
    per_dev = []
    for full, ax in zip(fulls, axes):
        parts = jnp.stack(jnp.split(full, N_DEV, axis=ax))
        per_dev.append(parts.reshape(N_DEV, -1))
    flat = jnp.concatenate(per_dev, axis=1)
    n = flat.shape[1]
    unit = LANES * row_mult
    total = -(-n // unit) * unit
    return jnp.pad(flat, ((0, 0), (0, total - n))).reshape(N_DEV, total // LANES, LANES)


def kernel(x, positions, ssd_in_proj, ssd_conv_w, ssd_conv_b, ssd_dt_bias, ssd_A_log, ssd_D, ssd_norm_g, ssd_out_proj, kv_down_proj, kv_norm_g, kv_up_k, kv_up_v, q_down_proj, q_norm_g, q_up_proj, attn_out_proj, ffn_up, ffn_conv_w, ffn_conv_b, ffn_down, ln_mix_g, ln_mix_b, ln_ffn_g, ln_ffn_b, loss_target, m_ssd_in_proj, m_ssd_conv_w, m_ssd_conv_b, m_ssd_dt_bias, m_ssd_A_log, m_ssd_D, m_ssd_norm_g, m_ssd_out_proj, m_kv_down_proj, m_kv_norm_g, m_kv_up_k, m_kv_up_v, m_q_down_proj, m_q_norm_g, m_q_up_proj, m_attn_out_proj, m_ffn_up, m_ffn_conv_w, m_ffn_conv_b, m_ffn_down, m_ln_mix_g, m_ln_mix_b, m_ln_ffn_g, m_ln_ffn_b, v_ssd_in_proj, v_ssd_conv_w, v_ssd_conv_b, v_ssd_dt_bias, v_ssd_A_log, v_ssd_D, v_ssd_norm_g, v_ssd_out_proj, v_kv_down_proj, v_kv_norm_g, v_kv_up_k, v_kv_up_v, v_q_down_proj, v_q_norm_g, v_q_up_proj, v_attn_out_proj, v_ffn_up, v_ffn_conv_w, v_ffn_conv_b, v_ffn_down, v_ln_mix_g, v_ln_mix_b, v_ln_ffn_g, v_ln_ffn_b):
    given = dict(locals())
    wl = {n: given[n] for n in WEIGHTS}
    ml = {n: given['m_' + n] for n in WEIGHTS}
    vl = {n: given['v_' + n] for n in WEIGHTS}
    sharded_axis = dict(SHARDED)
    big = [n for n, _ in SHARDED if n not in SHARDED_F32]
    small = [n for n, _ in SHARDED if n in SHARDED_F32]

    def as2d(a):
        return a.reshape(-1, a.shape[-1])

    def to_full(gathered, n):
        shp, ax = wl[n].shape, sharded_axis[n]
        moved = jnp.moveaxis(gathered.reshape((N_DEV,) + shp), 0, ax)
        return moved.reshape(shp[:ax] + (N_DEV * shp[ax],) + shp[ax + 1:])

    def to_chunks(full_grad, n):
        shp, ax = wl[n].shape, sharded_axis[n]
        split = full_grad.reshape(shp[:ax] + (N_DEV, shp[ax]) + shp[ax + 1:])
        return jnp.moveaxis(split, ax, 0).reshape((N_DEV,) + as2d(wl[n]).shape)

    first = [n for n in big if n in GATHERED_FIRST]
    late = [n for n in big if n not in GATHERED_FIRST]
    full = {n: wl[n] for n in REPLICATED}
    gathered = _all_gather([as2d(wl[n]).astype(BF16) for n in first] + [_pack([wl[n] for n in small], F32, 8)],
                           name='gather_weights')
    for n, gat in zip(first, gathered[:-1]):
        full[n] = to_full(gat, n)
    full.update(zip(small, _unpack_gathered(gathered[-1], [wl[n].shape for n in small],
                                            [sharded_axis[n] for n in small])))

    cx, cy, cc = lax.axis_index("x"), lax.axis_index("y"), lax.axis_index("c")
    core = jnp.reshape(cc, (1,)).astype(jnp.int32)
    chip = jnp.reshape(2 * cx + cy, (1,)).astype(jnp.int32)
    early_names = [n for n in big if n not in REDUCED_LAST]
    last_names = [n for n in big if n in REDUCED_LAST]
    parts = {}

    def to_sibling(chunked, names, tag):
        r1 = _exchange_sibling(chunked, name=f'grads_to_sibling_{tag}')
        return [_add_sibling(gf, r, core, name=f'grads_add_sibling_{n}') for n, gf, r in zip(names, chunked, r1)]

    def reduce_early(grads):
        parts.update(zip(early_names, to_sibling([to_chunks(grads[n], n) for n in early_names], early_names, 'early')))
        return [parts[n] for n in early_names]

    sq, grad_x, grads, r2_early = _local_step(
        x, positions, full, loss_target,
        late=([as2d(wl[n]).astype(BF16) for n in late], lambda gats: {n: to_full(g, n) for n, g in zip(late, gats)}),
        reduce_early=reduce_early)
    loss = lax.psum(0.5 * jnp.sum(sq) / x.shape[-1], ("x", "y", "c"))

    chunked = [to_chunks(grads[n], n) for n in last_names]
    chunked.append(_pack_chunks([grads[n] for n in small], [sharded_axis[n] for n in small], 8))
    parts_last = to_sibling(chunked, last_names + ['small'], 'last')
    r2_last = _exchange_chips(parts_last, name='grads_to_chips')
    parts.update(zip(last_names + ['small'], parts_last))
    r2 = dict(zip(early_names, r2_early))
    r2.update(zip(last_names + ['small'], r2_last))
    res = {}
    kinds = ('grad', 'delta', 'new_m', 'new_v')
    for n in big:
        outs = _adam_sharded(parts[n], r2[n], chip, as2d(wl[n]), as2d(ml[n]), as2d(vl[n]), name=f'adamw_{n}')
        for kind, a in zip(kinds, outs):
            res[kind, n] = a.reshape(wl[n].shape)
    outs = _adam_sharded(parts['small'], r2['small'], chip, _pack([wl[n] for n in small], F32, 8),
                         _pack([ml[n] for n in small], F32, 8), _pack([vl[n] for n in small], F32, 8),
                         name='adamw_small_sharded')
    for kind, packed in zip(kinds, outs):
        for n, a in zip(small, _unpack(packed, [wl[n].shape for n in small])):
            res[kind, n] = a

    rep = list(REPLICATED)
    rshapes = [wl[n].shape for n in rep]
    gall, = _all_gather([_pack([grads[n] for n in rep], F32, 8)], name='gather_replicated_grads')
    outs = _adam_replicated(gall, _pack([wl[n] for n in rep], F32, 8), _pack([ml[n] for n in rep], F32, 8),
                            _pack([vl[n] for n in rep], F32, 8), name='adamw_replicated')
    for kind, packed in zip(('grad', 'delta', 'new_m', 'new_v'), outs):
        for n, a in zip(rep, _unpack(packed, rshapes)):
            res[kind, n] = a

    return (loss, grad_x, *[res['grad', n] for n in WEIGHTS], *[res['delta', n] for n in WEIGHTS],
            *[res['new_m', n] for n in WEIGHTS], *[res['new_v', n] for n in WEIGHTS])
```

```python
import functools
import math

import jax
import jax.numpy as jnp
from jax import lax
from jax.experimental import pallas as pl
from jax.experimental.pallas import tpu as pltpu

F32 = jnp.float32
BF16 = jnp.bfloat16
MESH = pl.DeviceIdType.MESH

N_DEV = 8
LANES = 128
MM_TILES = (1024, 1408, 896, 512, 384, 256, 128)
MM_VMEM_BUDGET = 38 * 1024 * 1024
VMEM_LIMIT = 56 * 1024 * 1024
LN_EPS = 1e-5
RMS_EPS = 1e-6
SSD_HEAD_DIM = 64
SSD_N_GROUPS = 8
SSD_D_STATE = 128
SSD_CHUNK = 128
MLA_NOPE = 128
MLA_ROPE = 64
MLA_V = 128
ROPE_THETA = 10000.0
ADAM_LR = 0.001
ADAM_B1 = 0.9
ADAM_B2 = 0.999
ADAM_EPS = 1e-08
ADAM_WD = 0.01
ADAM_STEP = 10
NEG_BIG = -1e30

SHARDED = (('ssd_in_proj', 2), ('ssd_conv_w', 2), ('ssd_conv_b', 1), ('ssd_norm_g', 1), ('ssd_out_proj', 1),
           ('kv_down_proj', 0), ('kv_up_k', 1), ('kv_up_v', 1), ('q_down_proj', 1), ('q_up_proj', 2),
           ('attn_out_proj', 1), ('ffn_up', 2), ('ffn_conv_w', 2), ('ffn_down', 1))
SHARDED_F32 = ('ssd_conv_w', 'ssd_conv_b', 'ssd_norm_g', 'ffn_conv_w')
GATHERED_FIRST = ('ssd_in_proj',)
REDUCED_LAST = ('ssd_in_proj',)
REPLICATED = ('ssd_dt_bias', 'ssd_A_log', 'ssd_D', 'kv_norm_g', 'q_norm_g', 'ffn_conv_b',
              'ln_mix_g', 'ln_mix_b', 'ln_ffn_g', 'ln_ffn_b')
WEIGHTS = ('ssd_in_proj', 'ssd_conv_w', 'ssd_conv_b', 'ssd_dt_bias', 'ssd_A_log', 'ssd_D', 'ssd_norm_g',
           'ssd_out_proj', 'kv_down_proj', 'kv_norm_g', 'kv_up_k', 'kv_up_v', 'q_down_proj', 'q_norm_g',
           'q_up_proj', 'attn_out_proj', 'ffn_up', 'ffn_conv_w', 'ffn_conv_b', 'ffn_down', 'ln_mix_g',
           'ln_mix_b', 'ln_ffn_g', 'ln_ffn_b')


def _cparams():
    return pltpu.CompilerParams(vmem_limit_bytes=VMEM_LIMIT)


def _tile(n, cands):
    for c in cands:
        if n % c == 0:
            return c
    return n


def _sigmoid(x):
    return 1.0 / (1.0 + jnp.exp(-x))


def _iota(shape, axis):
    return lax.broadcasted_iota(jnp.int32, shape, axis)


def _mm(a, b, *, ta=False, tb=False, add=None, add_scale=1.0, out_dtype=F32, name, b_koff=0, into=None, into_col=0,
        exchange=()):
    if ta:
        K, M = a.shape
    else:
        M, K = a.shape
    if tb:
        N, Kb = b.shape
    else:
        Kb, N = b.shape
    assert b_koff + K <= Kb, (a.shape, b.shape, ta, tb, b_koff)
    tm = _tile(M, MM_TILES)
    tn = _tile(N, MM_TILES)
    tk = K if K <= MM_TILES[0] and b_koff == 0 and Kb == K else _tile(K, MM_TILES)
    nk = K // tk
    assert b_koff % tk == 0 and (into is None or (into.shape[0] == M and into.dtype == out_dtype))
    kb = b_koff // tk

    def vmem_estimate(tm_, tn_):
        ins = 2 * (tm_ * tk * a.dtype.itemsize + tk * tn_ * b.dtype.itemsize)
        outs = tm_ * tn_ * (2 * jnp.dtype(out_dtype).itemsize + 4 + (4 if nk > 1 else 0))
        return ins + outs + (2 * tm_ * tn_ * add.dtype.itemsize if add is not None else 0)

    while vmem_estimate(tm, tn) > MM_VMEM_BUDGET:
        can_m, can_n = tm % (2 * LANES) == 0, tn % (2 * LANES) == 0
        if can_m and (tm >= tn or not can_n):
            tm //= 2
        elif can_n:
            tn //= 2
        else:
            break

    assert into_col % tn == 0
    jb = into_col // tn

    nx = len(exchange)
    n_in = (add is not None) + (into is not None)
    grid = (M // tm, N // tn, nk)

    def body(a_ref, b_ref, *rest):
        add_ref = rest[0] if add is not None else None
        o_ref = rest[n_in + nx]
        acc = rest[n_in + 2 * nx + 1] if nk > 1 else None
        k = pl.program_id(2)
        if nx:
            start, finish_exchange = _chips_plan(rest[n_in:n_in + nx], rest[n_in + nx + 1:n_in + 2 * nx + 1],
                                                 *rest[len(rest) - 2:])
            t, total = _grid_step(grid)
            pl.when(t == 0)(start)

        if ta:
            av = a_ref[...].astype(BF16).T
        else:
            av = a_ref[...].astype(BF16)
        bv = b_ref[...].astype(BF16)
        dn = (((1,), (1 if tb else 0,)), ((), ()))
        part = lax.dot_general(av, bv, dn, preferred_element_type=F32)

        def finish(r):
            if add is not None:
                r = r + add_scale * add_ref[...].astype(F32)
            o_ref[...] = r.astype(out_dtype)

        if nk == 1:
            finish(part)
        else:
            @pl.when(k == 0)
            def _():
                acc[...] = part

            @pl.when(k > 0)
            def _():
                acc[...] += part

            @pl.when(k == nk - 1)
            def _():
                finish(acc[...])

        if nx:
            pl.when(t == total - 1)(finish_exchange)

    a_spec = (pl.BlockSpec((tk, tm), lambda i, j, k: (k, i)) if ta
              else pl.BlockSpec((tm, tk), lambda i, j, k: (i, k)))
    b_spec = (pl.BlockSpec((tn, tk), lambda i, j, k: (j, k + kb)) if tb
              else pl.BlockSpec((tk, tn), lambda i, j, k: (k + kb, j)))
    in_specs = [a_spec, b_spec]
    args = [a, b]
    if add is not None:
        in_specs.append(pl.BlockSpec((tm, tn), lambda i, j, k: (i, j)))
        args.append(add)
    aliases = {}
    if into is not None:
        aliases = {len(args): 0}
        in_specs.append(pl.BlockSpec(memory_space=pl.ANY))
        args.append(into)
    any_spec = pl.BlockSpec(memory_space=pl.ANY)
    outs = pl.pallas_call(
        body, name=name, grid=grid,
        in_specs=in_specs + [any_spec] * nx,
        out_specs=[pl.BlockSpec((tm, tn), lambda i, j, k: (i, j + jb))] + [any_spec] * nx,
        out_shape=[jax.ShapeDtypeStruct((M, N) if into is None else into.shape, out_dtype)] + _chips_shapes(exchange),
        scratch_shapes=([pltpu.VMEM((tm, tn), F32)] if nk > 1 else []) + (_chips_sems(nx) if nx else []),
        input_output_aliases=aliases, compiler_params=_cparams(),
    )(*args, *exchange)
    return (outs[0], list(outs[1:])) if nx else outs[0]


def _ln_fwd(h, mix, g, b, alpha, name):
    T, D = h.shape
    tm = _tile(T, (256, 128))

    def body(h_ref, m_ref, g_ref, b_ref, y_ref, s_ref, yb_ref):
        s = alpha * h_ref[...] + m_ref[...]
        mu = jnp.mean(s, axis=1, keepdims=True)
        xc = s - mu
        var = jnp.mean(xc * xc, axis=1, keepdims=True)
        y = xc * lax.rsqrt(var + LN_EPS) * g_ref[...] + b_ref[...]
        y_ref[...] = y
        s_ref[...] = s
        yb_ref[...] = y.astype(BF16)

    row = pl.BlockSpec((tm, D), lambda i: (i, 0))
    vec = pl.BlockSpec((1, D), lambda i: (0, 0))
    return pl.pallas_call(
        body, name=name, grid=(T // tm,), in_specs=[row, row, vec, vec], out_specs=[row, row, row],
        out_shape=[jax.ShapeDtypeStruct((T, D), F32)] * 2 + [jax.ShapeDtypeStruct((T, D), BF16)],
        compiler_params=_cparams(),
    )(h, mix, g, b)


def _ln_bwd(dy, s, g, name):
    T, D = s.shape
    tm = _tile(T, (256, 128))

    def body(dy_ref, s_ref, g_ref, ds_ref, dsb_ref, dg_ref, db_ref):
        @pl.when(pl.program_id(0) == 0)
        def _():
            dg_ref[...] = jnp.zeros_like(dg_ref)
            db_ref[...] = jnp.zeros_like(db_ref)

        sv = s_ref[...]
        dyv = dy_ref[...]
        mu = jnp.mean(sv, axis=1, keepdims=True)
        xc = sv - mu
        rstd = lax.rsqrt(jnp.mean(xc * xc, axis=1, keepdims=True) + LN_EPS)
        xhat = xc * rstd
        dxh = dyv * g_ref[...]
        m1 = jnp.mean(dxh, axis=1, keepdims=True)
        m2 = jnp.mean(dxh * xhat, axis=1, keepdims=True)
        ds = rstd * (dxh - m1 - xhat * m2)
        ds_ref[...] = ds
        dsb_ref[...] = ds.astype(BF16)
        dg_ref[...] += jnp.sum(dyv * xhat, axis=0, keepdims=True)
        db_ref[...] += jnp.sum(dyv, axis=0, keepdims=True)

    row = pl.BlockSpec((tm, D), lambda i: (i, 0))
    vec = pl.BlockSpec((1, D), lambda i: (0, 0))
    return pl.pallas_call(
        body, name=name, grid=(T // tm,), in_specs=[row, row, vec], out_specs=[row, row, vec, vec],
        out_shape=[jax.ShapeDtypeStruct((T, D), F32), jax.ShapeDtypeStruct((T, D), BF16),
                   jax.ShapeDtypeStruct((1, D), F32), jax.ShapeDtypeStruct((1, D), F32)],
        compiler_params=_cparams(),
    )(dy, s, g)


def _loss_head(y, target, name):
    T, D = y.shape
    tm = _tile(T, (256, 128))

    def body(y_ref, t_ref, sq_ref, dy_ref):
        @pl.when(pl.program_id(0) == 0)
        def _():
            sq_ref[...] = jnp.zeros_like(sq_ref)

        e = y_ref[...] - t_ref[...]
        sq_ref[...] += jnp.sum(e * e, axis=0, keepdims=True)
        dy_ref[...] = e * (1.0 / D)

    row = pl.BlockSpec((tm, D), lambda i: (i, 0))
    vec = pl.BlockSpec((1, D), lambda i: (0, 0))
    return pl.pallas_call(
        body, name=name, grid=(T // tm,), in_specs=[row, row], out_specs=[vec, row],
        out_shape=[jax.ShapeDtypeStruct((1, D), F32), jax.ShapeDtypeStruct((T, D), F32)],
        compiler_params=_cparams(),
    )(y, target)


def _shift_down(x, j):
    if j == 0:
        return x
    return jnp.where(_iota(x.shape, 0) >= j, pltpu.roll(x, j, 0), 0.0)


def _shift_up(x, j):
    if j == 0:
        return x
    n = x.shape[0]
    return jnp.where(_iota(x.shape, 0) < n - j, pltpu.roll(x, n - j, 0), 0.0)


def _conv_val(x, w_ref, b_ref):
    width = w_ref.shape[0]
    y = b_ref[...] + w_ref[width - 1:width, :] * x
    for j in range(1, width):
        y = y + w_ref[width - 1 - j:width - j, :] * _shift_down(x, j)
    return y


def _conv_bwd_val(x, dc, w_ref, dw_ref, db_ref):
    width = w_ref.shape[0]
    dx = w_ref[width - 1:width, :] * dc
    dw_ref[width - 1:width, :] += jnp.sum(dc * x, axis=0, keepdims=True)
    for j in range(1, width):
        sj = _shift_up(dc, j)
        dx = dx + w_ref[width - 1 - j:width - j, :] * sj
        dw_ref[width - 1 - j:width - j, :] += jnp.sum(sj * x, axis=0, keepdims=True)
    db_ref[...] += jnp.sum(dc, axis=0, keepdims=True)
    return dx


def _ffn_specs(H, S, width):
    nC = H // LANES
    return dict(
        g=pl.BlockSpec((S, LANES), lambda j, b: (b, j)), v=pl.BlockSpec((S, LANES), lambda j, b: (b, j + nC)),
        wg=pl.BlockSpec((width, LANES), lambda j, b: (0, j)), wv=pl.BlockSpec((width, LANES), lambda j, b: (0, j + nC)),
        bg=pl.BlockSpec((1, LANES), lambda j, b: (0, j)), bv=pl.BlockSpec((1, LANES), lambda j, b: (0, j + nC)))


def _ffn_act_fwd(u, cw, cb, S, name):
    T, H2 = u.shape
    H = H2 // 2
    sp = _ffn_specs(H, S, cw.shape[0])

    def body(ug_ref, uv_ref, wg_ref, wv_ref, bg_ref, bv_ref, a_ref):
        g = _conv_val(ug_ref[...], wg_ref, bg_ref)
        v = _conv_val(uv_ref[...], wv_ref, bv_ref)
        a_ref[...] = (g * _sigmoid(g) * v).astype(BF16)

    return pl.pallas_call(
        body, name=name, grid=(H // LANES, T // S),
        in_specs=[sp['g'], sp['v'], sp['wg'], sp['wv'], sp['bg'], sp['bv']], out_specs=sp['g'],
        out_shape=jax.ShapeDtypeStruct((T, H), BF16), compiler_params=_cparams(),
    )(u, u, cw, cw, cb, cb)


def _ffn_act_bwd(u, da, cw, cb, S, name):
    T, H2 = u.shape
    H = H2 // 2
    width = cw.shape[0]
    sp = _ffn_specs(H, S, width)

    def body(ug_ref, uv_ref, da_ref, wg_ref, wv_ref, bg_ref, bv_ref,
             dug_ref, duv_ref, dwg_ref, dwv_ref, dbg_ref, dbv_ref):
        @pl.when(pl.program_id(1) == 0)
        def _():
            for r in (dwg_ref, dwv_ref, dbg_ref, dbv_ref):
                r[...] = jnp.zeros_like(r)

        xg = [_shift_down(ug_ref[...], j) for j in range(width)]
        xv = [_shift_down(uv_ref[...], j) for j in range(width)]

        def conv(xs, w_ref, b_ref):
            y = b_ref[...]
            for j in range(width):
                y = y + w_ref[width - 1 - j:width - j, :] * xs[j]
            return y

        def conv_bwd(xs, dc, w_ref, dw_ref, db_ref):
            dx = w_ref[width - 1:width, :] * dc
            for j in range(1, width):
                dx = dx + w_ref[width - 1 - j:width - j, :] * _shift_up(dc, j)
            for j in range(width):
                dw_ref[width - 1 - j:width - j, :] += jnp.sum(dc * xs[j], axis=0, keepdims=True)
            db_ref[...] += jnp.sum(dc, axis=0, keepdims=True)
            return dx

        g = conv(xg, wg_ref, bg_ref)
        v = conv(xv, wv_ref, bv_ref)
        dav = da_ref[...].astype(F32)
        sg = _sigmoid(g)
        dg = dav * v * sg * (1.0 + g * (1.0 - sg))
        dv = dav * g * sg
        dug_ref[...] = conv_bwd(xg, dg, wg_ref, dwg_ref, dbg_ref).astype(BF16)
        duv_ref[...] = conv_bwd(xv, dv, wv_ref, dwv_ref, dbv_ref).astype(BF16)

    act = jax.ShapeDtypeStruct((T, H), BF16)
    wsh = jax.ShapeDtypeStruct((width, H), F32)
    bsh = jax.ShapeDtypeStruct((1, H), F32)
    return pl.pallas_call(
        body, name=name, grid=(H // LANES, T // S),
        in_specs=[sp['g'], sp['v'], sp['g'], sp['wg'], sp['wv'], sp['bg'], sp['bv']],
        out_specs=[sp['g'], sp['g'], sp['wg'], sp['wg'], sp['bg'], sp['bg']],
        out_shape=[act, act, wsh, wsh, bsh, bsh], compiler_params=_cparams(),
    )(u, u, da, cw, cw, cb, cb)


def _ssd_conv_fwd(x, cw, cb, S, name):
    T, C = x.shape
    Cb = _tile(C, (256, 128))
    width = cw.shape[0]

    def body(x_ref, w_ref, b_ref, y_ref, c_ref):
        c = _conv_val(x_ref[...], w_ref, b_ref)
        c_ref[...] = c
        y_ref[...] = c * _sigmoid(c)

    blk = pl.BlockSpec((S, Cb), lambda j, b: (b, j))
    wblk = pl.BlockSpec((width, Cb), lambda j, b: (0, j))
    bblk = pl.BlockSpec((1, Cb), lambda j, b: (0, j))
    return pl.pallas_call(
        body, name=name, grid=(C // Cb, T // S), in_specs=[blk, wblk, bblk], out_specs=[blk, blk],
        out_shape=[jax.ShapeDtypeStruct((T, C), F32)] * 2, compiler_params=_cparams(),
    )(x, cw, cb)


def _ssd_conv_bwd(x, c, dys, cw, S, name):
    T, C = x.shape
    Cb = _tile(C, (256, 128))
    width = cw.shape[0]
    assert all(d.shape[1] % Cb == 0 for d in dys) and sum(d.shape[1] for d in dys) == C

    def part(dy, j0, dx_prev, k):
        n = dy.shape[1] // Cb

        def body(x_ref, c_ref, dy_ref, w_ref, prev_ref, dx_ref, dw_ref, db_ref):
            @pl.when(pl.program_id(1) == 0)
            def _():
                dw_ref[...] = jnp.zeros_like(dw_ref)
                db_ref[...] = jnp.zeros_like(db_ref)

            cval = c_ref[...]
            sc = _sigmoid(cval)
            dc = dy_ref[...] * sc * (1.0 + cval * (1.0 - sc))
            dx_ref[...] = _conv_bwd_val(x_ref[...], dc, w_ref, dw_ref, db_ref).astype(BF16)

        wide = pl.BlockSpec((S, Cb), lambda j, b: (b, j + j0))
        return pl.pallas_call(
            body, name=f'{name}_{k}', grid=(n, T // S),
            in_specs=[wide, wide, pl.BlockSpec((S, Cb), lambda j, b: (b, j)),
                      pl.BlockSpec((width, Cb), lambda j, b: (0, j + j0)), pl.BlockSpec(memory_space=pl.ANY)],
            out_specs=[wide, pl.BlockSpec((width, Cb), lambda j, b: (0, j)), pl.BlockSpec((1, Cb), lambda j, b: (0, j))],
            out_shape=[jax.ShapeDtypeStruct((T, C), BF16), jax.ShapeDtypeStruct((width, n * Cb), F32),
                       jax.ShapeDtypeStruct((1, n * Cb), F32)],
            input_output_aliases={4: 0}, compiler_params=_cparams(),
        )(x, c, dy, cw, dx_prev)

    dx = lax.empty((T, C), BF16)
    dws, dbs, j0 = [], [], 0
    for k, dy in enumerate(dys):
        dx, dw, db = part(dy, j0, dx, k)
        dws.append(dw)
        dbs.append(db)
        j0 += dy.shape[1] // Cb
    return dx, jnp.concatenate(dws, axis=1), jnp.concatenate(dbs, axis=1)


def _softplus(x):
    e = jnp.exp(-jnp.abs(x))
    return jnp.maximum(x, 0.0) + jnp.where(e < 1e-4, e * (1.0 - 0.5 * e), jnp.log(1.0 + e))


def _cumsum_rows(x):
    n = x.shape[0]
    row = _iota(x.shape, 0)
    k = 1
    while k < n:
        x = x + jnp.where(row >= k, pltpu.roll(x, k, 0), 0.0)
        k *= 2
    return x


def _rev_cumsum_rows(x):
    n = x.shape[0]
    row = _iota(x.shape, 0)
    k = 1
    while k < n:
        x = x + jnp.where(row < n - k, pltpu.roll(x, n - k, 0), 0.0)
        k *= 2
    return x


def _col(x, lane_ids, h):
    return jnp.sum(jnp.where(lane_ids == h, x, 0.0), axis=1, keepdims=True)


def _bdot(a, b, dims):
    return lax.dot_general(a.astype(BF16), b.astype(BF16), (dims, ((), ())), preferred_element_type=F32)


NN = ((1,), (0,))
NT = ((1,), (1,))


def _scan_specs(nc, d_inner, hpg):
    L = SSD_CHUNK
    G = SSD_N_GROUPS
    gw = hpg * SSD_HEAD_DIM
    nb = d_inner // LANES
    return dict(
        xs=pl.BlockSpec((L, gw), lambda b, g, c: (b * nc + c, g)),
        B=pl.BlockSpec((L, LANES), lambda b, g, c: (b * nc + c, nb + g)),
        C=pl.BlockSpec((L, LANES), lambda b, g, c: (b * nc + c, nb + G + g)),
        dt=pl.BlockSpec((L, LANES), lambda b, g, c: (b * nc + c, 0)),
        vec=pl.BlockSpec((1, LANES), lambda b, g, c: (0, 0)),
        gvec=pl.BlockSpec((1, gw), lambda b, g, c: (0, g)),
        grp=pl.BlockSpec((L, LANES), lambda b, g, c: (b * nc + c, g)),
        st=pl.BlockSpec((1, hpg // 2, SSD_D_STATE, LANES), lambda b, g, c: ((b * G + g) * nc + c, 0, 0, 0)),
    )


def _grid_step(dims):
    t, total = 0, 1
    for ax, d in enumerate(dims):
        t = t * d + pl.program_id(ax)
        total *= d
    return t, total


def _scan_fwd(xbc, dt_pre, dt_bias, a_log, S, d_inner, name, gather=()):
    T = xbc.shape[0]
    Bn = T // S
    L = SSD_CHUNK
    G = SSD_N_GROUPS
    nc = S // L
    hpg = d_inner // SSD_HEAD_DIM // G
    pairs = hpg // 2
    sp = _scan_specs(nc, d_inner, hpg)
    ng = len(gather)

    def body(*refs):
        xs_ref, b_ref, c_ref, dtp_ref, bias_ref, alog_ref = refs[:6]
        y_ref, st_ref = refs[6 + ng:8 + ng]
        state = refs[8 + 2 * ng]
        g = pl.program_id(1)

        if ng:
            start, forward, finish = _gather_plan(refs[6:6 + ng], refs[8 + ng:8 + 2 * ng], *refs[9 + 2 * ng:])
            t, total = _grid_step((Bn, G, nc))
            pl.when(t == 0)(start)
            pl.when(t == total // 2)(forward)

        @pl.when(pl.program_id(2) == 0)
        def _():
            state[...] = jnp.zeros_like(state)

        dt = _softplus(dtp_ref[...] + bias_ref[...])
        cum = _cumsum_rows(dt * (-jnp.exp(alog_ref[...])))
        Bm = b_ref[...]
        Cm = c_ref[...]
        Gm = _bdot(Cm, Bm, NT)
        tril = _iota((L, L), 1) <= _iota((L, L), 0)
        lane = _iota((L, LANES), 1)
        lane1 = _iota((1, LANES), 1)
        last = _iota((L, 1), 0) == L - 1
        BmT = Bm.T
        for p in range(pairs):
            S_in = state[p]
            st_ref[0, p] = S_in
            x_pair = xs_ref[:, p * LANES:(p + 1) * LANES]
            y_pair = jnp.zeros((L, LANES), F32)
            z_pair = jnp.zeros((L, LANES), F32)
            e_pair = jnp.zeros((L, LANES), F32)
            el_pair = jnp.zeros((1, LANES), F32)
            for k in range(2):
                h = g * hpg + 2 * p + k
                cum_col = _col(cum, lane, h)
                dt_col = _col(dt, lane, h)
                cb = jnp.broadcast_to(cum_col, (L, L))
                decay = jnp.exp(jnp.where(tril, cb - cb.T, NEG_BIG))
                mk = (lane >= SSD_HEAD_DIM) if k else (lane < SSD_HEAD_DIM)
                xd = jnp.where(mk, x_pair * dt_col, 0.0)
                y_pair = y_pair + _bdot(Gm * decay, xd, NN)
                cum_l = jnp.sum(jnp.where(last, cum_col, 0.0), axis=0, keepdims=True)
                e_pair = jnp.where(mk, jnp.exp(cum_col), e_pair)
                z_pair = z_pair + xd * jnp.exp(cum_l - cum_col)
                el_pair = jnp.where((lane1 >= SSD_HEAD_DIM) if k else (lane1 < SSD_HEAD_DIM), jnp.exp(cum_l), el_pair)
            y_pair = y_pair + e_pair * _bdot(Cm, S_in, NN)
            state[p] = S_in * el_pair + _bdot(BmT, z_pair, NN)
            y_ref[:, p * LANES:(p + 1) * LANES] = y_pair

        if ng:
            pl.when(t == total - 1)(finish)

    outs = pl.pallas_call(
        body, name=name, grid=(Bn, G, nc),
        in_specs=[sp['xs'], sp['B'], sp['C'], sp['dt'], sp['vec'], sp['vec']] + [ANY] * ng,
        out_specs=[sp['xs'], sp['st']] + [ANY] * ng,
        out_shape=[jax.ShapeDtypeStruct((T, d_inner), F32),
                   jax.ShapeDtypeStruct((Bn * G * nc, pairs, SSD_D_STATE, LANES), F32)] + _gather_shapes(gather),
        scratch_shapes=[pltpu.VMEM((pairs, SSD_D_STATE, LANES), F32)] + (_gather_sems(ng) if ng else []),
        compiler_params=_cparams(),
    )(xbc, xbc, xbc, dt_pre, dt_bias, a_log, *gather)
    return outs[0], outs[1], list(outs[2:])


def _scan_bwd(xbc, dt_pre, dt_bias, a_log, states, dy, d_lane, S, d_inner, name, exchange=()):
    T = xbc.shape[0]
    Bn = T // S
    L = SSD_CHUNK
    G = SSD_N_GROUPS
    nc = S // L
    hpg = d_inner // SSD_HEAD_DIM // G
    pairs = hpg // 2
    sp = _scan_specs(nc, d_inner, hpg)
    nx = len(exchange)

    def rev(spec):
        im = spec.index_map
        return pl.BlockSpec(spec.block_shape, lambda b, g, c: im(b, g, nc - 1 - c))

    def body(*refs):
        xs_ref, b_ref, c_ref, dtp_ref, bias_ref, alog_ref, st_ref, dy_ref, dl_ref = refs[:9]
        dxs_ref, db_ref, dc_ref, ddt_ref, da_ref = refs[9 + nx:14 + nx]
        dstate = refs[14 + 2 * nx]
        g = pl.program_id(1)

        if nx:
            start, finish = _chips_plan(refs[9:9 + nx], refs[14 + nx:14 + 2 * nx], *refs[15 + 2 * nx:])
            t, total = _grid_step((Bn, G, nc))
            pl.when(t == 0)(start)

        @pl.when(pl.program_id(2) == 0)
        def _():
            dstate[...] = jnp.zeros_like(dstate)

        dt = _softplus(dtp_ref[...] + bias_ref[...])
        cum = _cumsum_rows(dt * (-jnp.exp(alog_ref[...])))
        Bm = b_ref[...]
        Cm = c_ref[...]
        Gm = _bdot(Cm, Bm, NT)
        tril = _iota((L, L), 1) <= _iota((L, L), 0)
        lane = _iota((L, LANES), 1)
        lane1 = _iota((1, LANES), 1)
        last = _iota((L, 1), 0) == L - 1
        CmT = Cm.T
        dG = jnp.zeros((L, L), F32)
        dB = jnp.zeros((L, LANES), F32)
        dC = jnp.zeros((L, LANES), F32)
        dcum = jnp.zeros((L, LANES), F32)
        ddt = jnp.zeros((L, LANES), F32)
        for p in range(pairs):
            S_in = st_ref[0, p]
            dS = dstate[p]
            x_pair = xs_ref[:, p * LANES:(p + 1) * LANES]
            dy_pair = dy_ref[:, p * LANES:(p + 1) * LANES]
            Q = _bdot(Cm, S_in, NN)
            dZ = _bdot(Bm, dS, NN)
            prod = dS * S_in
            e_pair = jnp.zeros((L, LANES), F32)
            z_pair = jnp.zeros((L, LANES), F32)
            el_pair = jnp.zeros((1, LANES), F32)
            dx_pair = dl_ref[:, p * LANES:(p + 1) * LANES] * dy_pair
            for k in range(2):
                h = g * hpg + 2 * p + k
                cum_col = _col(cum, lane, h)
                dt_col = _col(dt, lane, h)
                cb = jnp.broadcast_to(cum_col, (L, L))
                decay = jnp.exp(jnp.where(tril, cb - cb.T, NEG_BIG))
                mk = (lane >= SSD_HEAD_DIM) if k else (lane < SSD_HEAD_DIM)
                mk1 = (lane1 >= SSD_HEAD_DIM) if k else (lane1 < SSD_HEAD_DIM)
                xd = jnp.where(mk, x_pair * dt_col, 0.0)
                dy_k = jnp.where(mk, dy_pair, 0.0)
                Wm = Gm * decay
                dWm = jnp.where(tril, _bdot(dy_k, xd, NT), 0.0)
                dxd = _bdot(Wm.T, dy_k, NN)
                dseg = dWm * Wm
                dG = dG + dWm * decay
                dcum_col = (jnp.sum(dseg, axis=1, keepdims=True)
                            - jnp.sum(dseg.T, axis=1, keepdims=True))
                cum_l = jnp.sum(jnp.where(last, cum_col, 0.0), axis=0, keepdims=True)
                e_col = jnp.exp(cum_col)
                w_col = jnp.exp(cum_l - cum_col)
                el = jnp.exp(cum_l)
                dcum_col = dcum_col + jnp.sum(dy_k * Q, axis=1, keepdims=True) * e_col
                de_e = jnp.sum(dZ * xd, axis=1, keepdims=True) * w_col
                dcum_col = dcum_col - de_e
                dcum_l = (jnp.sum(de_e, axis=0, keepdims=True)
                          + el * jnp.sum(jnp.sum(jnp.where(mk, prod, 0.0), axis=1, keepdims=True),
                                         axis=0, keepdims=True))
                dcum_col = dcum_col + jnp.where(last, dcum_l, 0.0)
                dxd = dxd + jnp.where(mk, dZ * w_col, 0.0)
                dx_pair = dx_pair + dxd * dt_col
                ddt = jnp.where(lane == h, jnp.sum(dxd * x_pair, axis=1, keepdims=True), ddt)
                dcum = jnp.where(lane == h, dcum_col, dcum)
                e_pair = jnp.where(mk, e_col, e_pair)
                z_pair = z_pair + xd * w_col
                el_pair = jnp.where(mk1, el, el_pair)
            dQ = e_pair * dy_pair
            dC = dC + _bdot(dQ, S_in, NT)
            dB = dB + _bdot(z_pair, dS, NT)
            dstate[p] = dS * el_pair + _bdot(CmT, dQ, NN)
            dxs_ref[:, p * LANES:(p + 1) * LANES] = dx_pair
        dc_ref[...] = dC + _bdot(dG, Bm, NN)
        db_ref[...] = dB + _bdot(dG.T, Cm, NN)
        ddt_ref[...] = ddt
        da_ref[...] = _rev_cumsum_rows(dcum)

        if nx:
            pl.when(t == total - 1)(finish)

    grp = jax.ShapeDtypeStruct((T, G * LANES), F32)
    outs = pl.pallas_call(
        body, name=name, grid=(Bn, G, nc),
        in_specs=[rev(sp['xs']), rev(sp['B']), rev(sp['C']), rev(sp['dt']), sp['vec'], sp['vec'],
                  rev(sp['st']), rev(sp['xs']), sp['gvec']] + [ANY] * nx,
        out_specs=[rev(sp['xs']), rev(sp['grp']), rev(sp['grp']), rev(sp['grp']), rev(sp['grp'])] + [ANY] * nx,
        out_shape=[jax.ShapeDtypeStruct((T, d_inner), F32), grp, grp, grp, grp] + _chips_shapes(exchange),
        scratch_shapes=[pltpu.VMEM((pairs, SSD_D_STATE, LANES), F32)] + (_chips_sems(nx) if nx else []),
        compiler_params=_cparams(),
    )(xbc, xbc, xbc, dt_pre, dt_bias, a_log, states, dy, d_lane, *exchange)
    return outs[:5], list(outs[5:])


def _dt_bwd(ddt_g, da_g, dt_pre, dt_bias, a_log, name):
    T = dt_pre.shape[0]
    G = SSD_N_GROUPS
    tm = _tile(T, (512, 256, 128))

    def body(ddt_ref, da_ref, dtp_ref, bias_ref, alog_ref, dx_ref, dbias_ref, dal_ref):
        @pl.when(pl.program_id(0) == 0)
        def _():
            dbias_ref[...] = jnp.zeros_like(dbias_ref)
            dal_ref[...] = jnp.zeros_like(dal_ref)

        ddt = ddt_ref[:, 0:LANES]
        da = da_ref[:, 0:LANES]
        for gi in range(1, G):
            ddt = ddt + ddt_ref[:, gi * LANES:(gi + 1) * LANES]
            da = da + da_ref[:, gi * LANES:(gi + 1) * LANES]
        xv = dtp_ref[...] + bias_ref[...]
        A = -jnp.exp(alog_ref[...])
        dx = (ddt + da * A) * _sigmoid(xv)
        dx_ref[...] = dx.astype(BF16)
        dbias_ref[...] += jnp.sum(dx, axis=0, keepdims=True)
        dal_ref[...] += jnp.sum(da * _softplus(xv), axis=0, keepdims=True) * A

    wide = pl.BlockSpec((tm, G * LANES), lambda i: (i, 0))
    row = pl.BlockSpec((tm, LANES), lambda i: (i, 0))
    vec = pl.BlockSpec((1, LANES), lambda i: (0, 0))
    vsh = jax.ShapeDtypeStruct((1, LANES), F32)
    return pl.pallas_call(
        body, name=name, grid=(T // tm,), in_specs=[wide, wide, row, vec, vec], out_specs=[row, vec, vec],
        out_shape=[jax.ShapeDtypeStruct((T, LANES), BF16), vsh, vsh], compiler_params=_cparams(),
    )(ddt_g, da_g, dt_pre, dt_bias, a_log)


def _gate_fwd(y, xbc, z, d_lane, ng, d_inner, name):
    T = y.shape[0]
    G = SSD_N_GROUPS
    gw = d_inner // G
    tm = _tile(T, (512, 256, 128))

    def body(y_ref, x_ref, z_ref, dl_ref, ng_ref, o_ref):
        zv = z_ref[...]
        y2 = (y_ref[...] + dl_ref[...] * x_ref[...]) * (zv * _sigmoid(zv))
        r = lax.rsqrt(jnp.mean(y2 * y2, axis=1, keepdims=True) + LN_EPS)
        o_ref[...] = (y2 * r * ng_ref[...]).astype(BF16)

    blk = pl.BlockSpec((tm, gw), lambda g, i: (i, g))
    vec = pl.BlockSpec((1, gw), lambda g, i: (0, g))
    return pl.pallas_call(
        body, name=name, grid=(G, T // tm), in_specs=[blk, blk, blk, vec, vec], out_specs=blk,
        out_shape=jax.ShapeDtypeStruct((T, d_inner), BF16), compiler_params=_cparams(),
    )(y, xbc, z, d_lane, ng)


def _gate_bwd(dyn, y, xbc, z, d_lane, ng, d_inner, name):
    T = y.shape[0]
    G = SSD_N_GROUPS
    gw = d_inner // G
    tm = _tile(T, (512, 256, 128))

    def body(dyn_ref, y_ref, x_ref, z_ref, dl_ref, ng_ref, dy_ref, dz_ref, dng_ref, ddl_ref):
        @pl.when(pl.program_id(1) == 0)
        def _():
            dng_ref[...] = jnp.zeros_like(dng_ref)
            ddl_ref[...] = jnp.zeros_like(ddl_ref)

        zv = z_ref[...]
        xv = x_ref[...]
        sz = _sigmoid(zv)
        y1 = y_ref[...] + dl_ref[...] * xv
        y2 = y1 * (zv * sz)
        r = lax.rsqrt(jnp.mean(y2 * y2, axis=1, keepdims=True) + LN_EPS)
        dn = dyn_ref[...].astype(F32)
        dng_ref[...] += jnp.sum(dn * y2 * r, axis=0, keepdims=True)
        do = dn * ng_ref[...]
        dy2 = r * do - y2 * (r * r * r) * jnp.mean(do * y2, axis=1, keepdims=True)
        dz_ref[...] = (dy2 * y1 * sz * (1.0 + zv * (1.0 - sz))).astype(BF16)
        dy1 = dy2 * (zv * sz)
        dy_ref[...] = dy1
        ddl_ref[...] += jnp.sum(dy1 * xv, axis=0, keepdims=True)

    blk = pl.BlockSpec((tm, gw), lambda g, i: (i, g))
    vec = pl.BlockSpec((1, gw), lambda g, i: (0, g))
    vsh = jax.ShapeDtypeStruct((1, d_inner), F32)
    return pl.pallas_call(
        body, name=name, grid=(G, T // tm), in_specs=[blk, blk, blk, blk, vec, vec],
        out_specs=[blk, blk, vec, vec],
        out_shape=[jax.ShapeDtypeStruct((T, d_inner), F32), jax.ShapeDtypeStruct((T, d_inner), BF16), vsh, vsh],
        compiler_params=_cparams(),
    )(dyn, y, xbc, z, d_lane, ng)


def _rms_fwd(x, g, name):
    T, R = x.shape
    tm = _tile(T, (512, 256, 128))

    def body(x_ref, g_ref, y_ref):
        xv = x_ref[...]
        y = xv * lax.rsqrt(jnp.mean(xv * xv, axis=1, keepdims=True) + RMS_EPS) * g_ref[...]
        y_ref[...] = y.astype(BF16)

    row = pl.BlockSpec((tm, R), lambda i: (i, 0))
    vec = pl.BlockSpec((1, R), lambda i: (0, 0))
    return pl.pallas_call(
        body, name=name, grid=(T // tm,), in_specs=[row, vec], out_specs=row,
        out_shape=jax.ShapeDtypeStruct((T, R), BF16), compiler_params=_cparams(),
    )(x, g)


def _rms_bwd(dy, x, g, name):
    T, R = x.shape
    tm = _tile(T, (512, 256, 128))

    def body(dy_ref, x_ref, g_ref, dx_ref, dg_ref):
        @pl.when(pl.program_id(0) == 0)
        def _():
            dg_ref[...] = jnp.zeros_like(dg_ref)

        xv = x_ref[...]
        dyv = dy_ref[...]
        r = lax.rsqrt(jnp.mean(xv * xv, axis=1, keepdims=True) + RMS_EPS)
        dg_ref[...] += jnp.sum(dyv * xv * r, axis=0, keepdims=True)
        do = dyv * g_ref[...]
        dx = r * do - xv * (r * r * r) * jnp.mean(do * xv, axis=1, keepdims=True)
        dx_ref[...] = dx.astype(BF16)

    row = pl.BlockSpec((tm, R), lambda i: (i, 0))
    vec = pl.BlockSpec((1, R), lambda i: (0, 0))
    return pl.pallas_call(
        body, name=name, grid=(T // tm,), in_specs=[row, row, vec], out_specs=[row, vec],
        out_shape=[jax.ShapeDtypeStruct((T, R), BF16), jax.ShapeDtypeStruct((1, R), F32)],
        compiler_params=_cparams(),
    )(dy, x, g)


def _swap_halves(x):
    half = MLA_ROPE // 2
    return jnp.where(_iota(x.shape, 1) < half, pltpu.roll(x, LANES - half, 1), pltpu.roll(x, half, 1))


def _rope(x, cc, ss, *, transpose, sum_heads=False, name):
    T, W = x.shape
    nh = W // LANES
    tm = _tile(T, (512, 256, 128))
    n_out = 1 if sum_heads else nh

    def body(x_ref, cc_ref, ss_ref, o_ref):
        ccv = cc_ref[...]
        ssv = ss_ref[...]
        if sum_heads:
            xv = x_ref[:, 0:LANES]
            for hh in range(1, nh):
                xv = xv + x_ref[:, hh * LANES:(hh + 1) * LANES]
            heads = [xv]
        else:
            heads = [x_ref[:, hh * LANES:(hh + 1) * LANES] for hh in range(nh)]
        for hh, xv in enumerate(heads):
            if transpose:
                r = xv * ccv + _swap_halves(xv * ssv)
            else:
                r = xv * ccv + _swap_halves(xv) * ssv
            r = jnp.where(_iota(r.shape, 1) < MLA_ROPE, r, 0.0)
            o_ref[:, hh * LANES:(hh + 1) * LANES] = r.astype(BF16)

    return pl.pallas_call(
        body, name=name, grid=(T // tm,),
        in_specs=[pl.BlockSpec((tm, W), lambda i: (i, 0)), pl.BlockSpec((tm, LANES), lambda i: (i, 0)),
                  pl.BlockSpec((tm, LANES), lambda i: (i, 0))],
        out_specs=pl.BlockSpec((tm, n_out * LANES), lambda i: (i, 0)),
        out_shape=jax.ShapeDtypeStruct((T, n_out * LANES), BF16), compiler_params=_cparams(),
    )(x, cc, ss)


ATT_BLOCK = 512
ATT_SUB = 256
LOG2E = 1.4426950408889634


def _att_geometry(S):
    tb = _tile(S, (ATT_BLOCK, 256))
    return tb, S // tb, tb // ATT_SUB


def _heads_per_step(nh):
    return 2 if nh % 2 == 0 else 1


def _att_scores(a, b, diag, kv_major, off):
    s = _bdot(a, b, NT)
    if diag:
        q_ax, k_ax = (1, 0) if kv_major else (0, 1)
        s = jnp.where(_iota(s.shape, k_ax) <= _iota(s.shape, q_ax) + off, s, NEG_BIG)
    return s


def _attention_fwd(qn, qr, kn, kr, v, S, scale, name):
    T, W = qn.shape
    nh = W // LANES
    Bn = T // S
    tb, n, C = _att_geometry(S)
    pairs = [(i, j) for i in range(n) for j in range(i + 1)]
    it = jnp.asarray([p[0] for p in pairs], jnp.int32)
    jt = jnp.asarray([p[1] for p in pairs], jnp.int32)
    c2 = scale * LOG2E

    HP = _heads_per_step(nh)

    def body(it_ref, jt_ref, qn_ref, qr_ref, kn_ref, kr_ref, v_ref, o_ref, ob_ref, lse_ref, m_sc, l_sc, acc):
        p = pl.program_id(2)
        i = it_ref[p]
        j = jt_ref[p]

        @pl.when(j == 0)
        def _():
            m_sc[...] = jnp.full_like(m_sc, NEG_BIG)
            l_sc[...] = jnp.zeros_like(l_sc)
            acc[...] = jnp.zeros_like(acc)

        def step(diag):
            for hh in range(HP):
                hs = pl.ds(hh * LANES, LANES)
                qc = jnp.concatenate([qn_ref[:, hs], qr_ref[:, hs]], axis=1)
                kc = jnp.concatenate([kn_ref[:, hs], kr_ref[...]], axis=1)
                st = _att_scores(kc, qc, diag, True, 0)
                m_old = m_sc[hh]
                m_new = jnp.maximum(m_old, jnp.max(st, axis=0, keepdims=True))
                pt = jnp.exp2((st - m_new) * c2)
                corr = jnp.exp2((m_old - m_new) * c2)
                l_sc[hh] = corr * l_sc[hh] + jnp.sum(pt, axis=0, keepdims=True)
                acc[hh] = corr * acc[hh] + _bdot(v_ref[:, hs].T, pt, NN)
                m_sc[hh] = m_new

        @pl.when(j < i)
        def _():
            step(False)

        @pl.when(j == i)
        def _():
            step(True)
            for hh in range(HP):
                hs = pl.ds(hh * LANES, LANES)
                ov = (acc[hh] / l_sc[hh]).T
                o_ref[:, hs] = ov
                ob_ref[:, hs] = ov.astype(BF16)
                lse_row = m_sc[hh] * scale + jnp.log(l_sc[hh])
                lse_ref[:, hs] = jnp.broadcast_to(lse_row, (LANES, tb)).T

    qspec = pl.BlockSpec((tb, HP * LANES), lambda b, h, p, it_, jt_: (b * n + it_[p], h))
    kspec = pl.BlockSpec((tb, HP * LANES), lambda b, h, p, it_, jt_: (b * n + jt_[p], h))
    krspec = pl.BlockSpec((tb, LANES), lambda b, h, p, it_, jt_: (b * n + jt_[p], 0))
    return pl.pallas_call(
        body, name=name,
        grid_spec=pltpu.PrefetchScalarGridSpec(
            num_scalar_prefetch=2, grid=(Bn, nh // HP, len(pairs)),
            in_specs=[qspec, qspec, kspec, krspec, kspec], out_specs=[qspec, qspec, qspec],
            scratch_shapes=[pltpu.VMEM((HP, 1, tb), F32), pltpu.VMEM((HP, 1, tb), F32),
                            pltpu.VMEM((HP, LANES, tb), F32)]),
        out_shape=[jax.ShapeDtypeStruct((T, W), F32), jax.ShapeDtypeStruct((T, W), BF16),
                   jax.ShapeDtypeStruct((T, W), F32)],
        compiler_params=_cparams(),
    )(it, jt, qn, qr, kn, kr, v)


def _attention_dq(qn, qr, kn, kr, v, o, do, lse, S, scale, name):
    T, W = qn.shape
    nh = W // LANES
    Bn = T // S
    tb, n, C = _att_geometry(S)
    pairs = [(i, j) for i in range(n) for j in range(i + 1)]
    it = jnp.asarray([p[0] for p in pairs], jnp.int32)
    jt = jnp.asarray([p[1] for p in pairs], jnp.int32)
    c2 = scale * LOG2E
    HP = _heads_per_step(nh)

    def body(it_ref, jt_ref, qn_ref, qr_ref, kn_ref, kr_ref, v_ref, o_ref, do_ref, lse_ref,
             dqn_ref, dqr_ref, acc, di_sc, lse_sc):
        p = pl.program_id(2)
        i = it_ref[p]
        j = jt_ref[p]

        @pl.when(j == 0)
        def _():
            acc[...] = jnp.zeros_like(acc)
            for hh in range(HP):
                hs = pl.ds(hh * LANES, LANES)
                di_sc[hh] = jnp.sum(do_ref[:, hs] * o_ref[:, hs], axis=1, keepdims=True)
                lse_sc[hh] = jnp.max(lse_ref[:, hs], axis=1, keepdims=True) * LOG2E

        def step(diag):
            for hh in range(HP):
                hs = pl.ds(hh * LANES, LANES)
                qc = jnp.concatenate([qn_ref[:, hs], qr_ref[:, hs]], axis=1)
                for c in range(C):
                    r0 = c * ATT_SUB if diag else 0
                    rows = pl.ds(r0, tb - r0)
                    ks = pl.ds(c * ATT_SUB, ATT_SUB)
                    kc = jnp.concatenate([kn_ref[ks, hs], kr_ref[ks, :]], axis=1)
                    s = _att_scores(qc[r0:], kc, diag, False, 0)
                    pr = jnp.exp2(s * c2 - lse_sc[hh, rows, :])
                    dp = _bdot(do_ref[rows, hs], v_ref[ks, hs], NT)
                    ds = pr * (dp - di_sc[hh, rows, :]) * scale
                    acc[hh, rows, :] += _bdot(ds, kc, NN)

        @pl.when(j < i)
        def _():
            step(False)

        @pl.when(j == i)
        def _():
            step(True)
            for hh in range(HP):
                hs = pl.ds(hh * LANES, LANES)
                dqn_ref[:, hs] = acc[hh, :, 0:LANES].astype(BF16)
                dqr_ref[:, hs] = acc[hh, :, LANES:2 * LANES]

    qspec = pl.BlockSpec((tb, HP * LANES), lambda b, h, p, it_, jt_: (b * n + it_[p], h))
    kspec = pl.BlockSpec((tb, HP * LANES), lambda b, h, p, it_, jt_: (b * n + jt_[p], h))
    krspec = pl.BlockSpec((tb, LANES), lambda b, h, p, it_, jt_: (b * n + jt_[p], 0))
    return pl.pallas_call(
        body, name=name,
        grid_spec=pltpu.PrefetchScalarGridSpec(
            num_scalar_prefetch=2, grid=(Bn, nh // HP, len(pairs)),
            in_specs=[qspec, qspec, kspec, krspec, kspec, qspec, qspec, qspec], out_specs=[qspec, qspec],
            scratch_shapes=[pltpu.VMEM((HP, tb, 2 * LANES), F32), pltpu.VMEM((HP, tb, 1), F32),
                            pltpu.VMEM((HP, tb, 1), F32)]),
        out_shape=[jax.ShapeDtypeStruct((T, W), BF16), jax.ShapeDtypeStruct((T, W), F32)],
        compiler_params=_cparams(),
    )(it, jt, qn, qr, kn, kr, v, o, do, lse)


def _attention_dkv(qn, qr, kn, kr, v, o, do, lse, S, scale, name):
    T, W = qn.shape
    nh = W // LANES
    Bn = T // S
    tb, n, C = _att_geometry(S)
    HP = _heads_per_step(nh)
    triples = [(j, h, i) for j in range(n) for h in range(nh // HP) for i in range(j, n)]
    jt = jnp.asarray([t[0] for t in triples], jnp.int32)
    ht = jnp.asarray([t[1] for t in triples], jnp.int32)
    it = jnp.asarray([t[2] for t in triples], jnp.int32)
    c2 = scale * LOG2E

    def as_row(col):
        return jnp.broadcast_to(col, (ATT_SUB, LANES)).T[0:1, :]

    def body(jt_ref, ht_ref, it_ref, qn_ref, qr_ref, kn_ref, kr_ref, v_ref, o_ref, do_ref, lse_ref,
             dkn_ref, dv_ref, dkr_ref, akc, av):
        p = pl.program_id(1)
        j = jt_ref[p]
        h = ht_ref[p]
        i = it_ref[p]

        @pl.when(jnp.logical_and(h == 0, i == j))
        def _():
            dkr_ref[...] = jnp.zeros_like(dkr_ref)

        @pl.when(i == j)
        def _():
            akc[...] = jnp.zeros_like(akc)
            av[...] = jnp.zeros_like(av)

        def step(diag):
            for hh in range(HP):
                hs = pl.ds(hh * LANES, LANES)
                kc = jnp.concatenate([kn_ref[:, hs], kr_ref[...]], axis=1)
                for c in range(C):
                    rows = pl.ds(c * ATT_SUB, ATT_SUB)
                    k1 = (c + 1) * ATT_SUB if diag else tb
                    ks = pl.ds(0, k1)
                    qc = jnp.concatenate([qn_ref[rows, hs], qr_ref[rows, hs]], axis=1)
                    st = _att_scores(kc[:k1], qc, diag, True, c * ATT_SUB)
                    dov = do_ref[rows, hs]
                    lse_row = as_row(jnp.max(lse_ref[rows, hs], axis=1, keepdims=True)) * LOG2E
                    di_row = as_row(jnp.sum(dov * o_ref[rows, hs], axis=1, keepdims=True))
                    pt = jnp.exp2(st * c2 - lse_row)
                    dst = pt * (_bdot(v_ref[ks, hs], dov, NT) - di_row) * scale
                    av[ks, hs] += _bdot(pt, dov, NN)
                    akc[hh, ks, :] += _bdot(dst, qc, NN)

        @pl.when(i > j)
        def _():
            step(False)

        @pl.when(i == j)
        def _():
            step(True)

        @pl.when(i == n - 1)
        def _():
            for hh in range(HP):
                dkn_ref[:, pl.ds(hh * LANES, LANES)] = akc[hh, :, 0:LANES].astype(BF16)
                dkr_ref[...] += akc[hh, :, LANES:2 * LANES]
            dv_ref[...] = av[...].astype(BF16)

    qspec = pl.BlockSpec((tb, HP * LANES), lambda b, p, jt_, ht_, it_: (b * n + it_[p], ht_[p]))
    kspec = pl.BlockSpec((tb, HP * LANES), lambda b, p, jt_, ht_, it_: (b * n + jt_[p], ht_[p]))
    krspec = pl.BlockSpec((tb, LANES), lambda b, p, jt_, ht_, it_: (b * n + jt_[p], 0))
    return pl.pallas_call(
        body, name=name,
        grid_spec=pltpu.PrefetchScalarGridSpec(
            num_scalar_prefetch=3, grid=(Bn, len(triples)),
            in_specs=[qspec, qspec, kspec, krspec, kspec, qspec, qspec, qspec],
            out_specs=[kspec, kspec, krspec],
            scratch_shapes=[pltpu.VMEM((HP, tb, 2 * LANES), F32), pltpu.VMEM((tb, HP * LANES), F32)]),
        out_shape=[jax.ShapeDtypeStruct((T, W), BF16), jax.ShapeDtypeStruct((T, W), BF16),
                   jax.ShapeDtypeStruct((T, LANES), F32)],
        compiler_params=_cparams(),
    )(jt, ht, it, qn, qr, kn, kr, v, o, do, lse)


def _pad_cols(w, n):
    return jnp.pad(w, ((0, 0), (0, n - w.shape[1])))


def _local_step(x, positions, w, target, late=None, reduce_early=None, reduce_last=None):
    w = dict(w)
    Bn, S, D = x.shape
    T = Bn * S
    depth = w['ln_mix_g'].shape[0]
    alpha = (2 * depth) ** 0.25
    d_inner = w['ssd_norm_g'].shape[1]
    n_heads_ssd = d_inner // SSD_HEAD_DIM
    G = SSD_N_GROUPS
    gn = G * SSD_D_STATE
    conv_dim = d_inner + 2 * gn
    hpg = n_heads_ssd // G
    nh = D // MLA_NOPE
    kv_rank = w['kv_norm_g'].shape[0]
    q_rank = w['q_norm_g'].shape[1]
    H = w['ffn_conv_b'].shape[1] // 2
    scale = (MLA_NOPE + MLA_ROPE) ** -0.5
    assert S % SSD_CHUNK == 0 and hpg % 2 == 0 and SSD_D_STATE == LANES and n_heads_ssd <= LANES

    X = x.reshape(T, D)
    tgt = target.reshape(T, D)

    inv_freq = 1.0 / (ROPE_THETA ** (jnp.arange(0, MLA_ROPE, 2, dtype=F32) / MLA_ROPE))
    ang = positions.reshape(T).astype(F32)[:, None] * inv_freq
    cos, sin = jnp.cos(ang), jnp.sin(ang)
    zpad = jnp.zeros((T, LANES - MLA_ROPE), F32)
    cc = jnp.concatenate([cos, cos, zpad], axis=1)
    ss = jnp.concatenate([-sin, sin, zpad], axis=1)

    w_in = w['ssd_in_proj'][0]
    w_z = w_in[:, :d_inner]
    w_xbc = w_in[:, d_inner:d_inner + conv_dim]
    w_dt = _pad_cols(w_in[:, d_inner + conv_dim:], LANES)
    ssd_cw = w['ssd_conv_w'][0]
    ssd_cb = w['ssd_conv_b']
    dt_bias = _pad_cols(w['ssd_dt_bias'], LANES)
    a_log = _pad_cols(w['ssd_A_log'], LANES)
    d_lane = jnp.repeat(w['ssd_D'], SSD_HEAD_DIM, axis=1)
    ssd_ng = w['ssd_norm_g']

    Xb = X.astype(BF16)
    z = _mm(Xb, w_z, name='ssd_in_z')
    xbc_pre = _mm(Xb, w_xbc, name='ssd_in_xbc')
    dt_pre = _mm(Xb, w_dt, name='ssd_in_dt')
    xbc, xbc_conv = _ssd_conv_fwd(xbc_pre, ssd_cw, ssd_cb, S, name='ssd_conv')
    y_scan, states, gathered = _scan_fwd(xbc, dt_pre, dt_bias, a_log, S, d_inner, name='ssd_scan',
                                         gather=late[0] if late else ())
    if late:
        w.update(late[1](gathered))

    w_out = w['ssd_out_proj'][0]
    w_kvc = w['kv_down_proj'][:, :kv_rank]
    w_kvr = _pad_cols(w['kv_down_proj'][:, kv_rank:], LANES)
    kv_g = w['kv_norm_g'].reshape(1, kv_rank)
    w_uk = w['kv_up_k']
    w_uv = w['kv_up_v']
    w_qd = w['q_down_proj'][0]
    q_g = w['q_norm_g']
    qu = w['q_up_proj'][0].reshape(q_rank, nh, MLA_NOPE + MLA_ROPE)
    w_qn = qu[:, :, :MLA_NOPE].reshape(q_rank, nh * LANES)
    w_qr = jnp.pad(qu[:, :, MLA_NOPE:], ((0, 0), (0, 0), (0, LANES - MLA_ROPE))).reshape(q_rank, nh * LANES)
    w_ao = w['attn_out_proj'][0]

    def ffn_parts(i):
        return dict(wu=w['ffn_up'][i], cw=w['ffn_conv_w'][i], cb=w['ffn_conv_b'][i:i + 1], wd=w['ffn_down'][i])

    def ln(name, i):
        return w[name][i:i + 1]

    def ffn_fwd(hb, i):
        f = ffn_parts(i)
        u = _mm(hb, f['wu'], name=f'ffn{i}_up')
        a = _ffn_act_fwd(u, f['cw'], f['cb'], S, name=f'ffn{i}_act')
        return _mm(a, f['wd'], name=f'ffn{i}_down'), (u, a)

    yn = _gate_fwd(y_scan, xbc, z, d_lane, ssd_ng, d_inner, name='ssd_gate')
    mix0 = _mm(yn, w_out, name='ssd_out')
    h1, s1, h1b = _ln_fwd(X, mix0, ln('ln_mix_g', 0), ln('ln_mix_b', 0), alpha, name='ln_mix0')
    ff0, ffn0_saved = ffn_fwd(h1b, 0)
    h2, s2, h2b = _ln_fwd(h1, ff0, ln('ln_ffn_g', 0), ln('ln_ffn_b', 0), alpha, name='ln_ffn0')

    ckv = _mm(h2b, w_kvc, name='kv_down_c')
    kr_pre = _mm(h2b, w_kvr, name='kv_down_r')
    ckvn = _rms_fwd(ckv, kv_g, name='kv_norm')
    kr = _rope(kr_pre, cc, ss, transpose=False, name='k_rope')
    kn = _mm(ckvn, w_uk, out_dtype=BF16, name='kv_up_k')
    v = _mm(ckvn, w_uv, out_dtype=BF16, name='kv_up_v')
    cq_pre = _mm(h2b, w_qd, name='q_down')
    cq = _rms_fwd(cq_pre, q_g, name='q_norm')
    qn = _mm(cq, w_qn, out_dtype=BF16, name='q_up_n')
    qr_pre = _mm(cq, w_qr, name='q_up_r')
    qr = _rope(qr_pre, cc, ss, transpose=False, name='q_rope')
    o, ob, lse = _attention_fwd(qn, qr, kn, kr, v, S, scale, name='attn_fwd')
    mix1 = _mm(ob, w_ao, name='attn_out')
    h3, s3, h3b = _ln_fwd(h2, mix1, ln('ln_mix_g', 1), ln('ln_mix_b', 1), alpha, name='ln_mix1')
    ff1, ffn1_saved = ffn_fwd(h3b, 1)
    h4, s4, _ = _ln_fwd(h3, ff1, ln('ln_ffn_g', 1), ln('ln_ffn_b', 1), alpha, name='ln_ffn1')
    sq, dh4 = _loss_head(h4, tgt, name='loss_head')

    g = {}

    def ffn_bwd(ds, dsb, hb_in, saved, i):
        f = ffn_parts(i)
        u, a = saved
        d_wd = _mm(a, dsb, ta=True, name=f'ffn{i}_down_dw')
        da = _mm(dsb, f['wd'], tb=True, out_dtype=BF16, name=f'ffn{i}_down_dx')
        dug, duv, dwg, dwv, dbg, dbv = _ffn_act_bwd(u, da, f['cw'], f['cb'], S, name=f'ffn{i}_act_bwd')
        d_wu = _mm(hb_in, dug, ta=True, name=f'ffn{i}_up_g_dw', into=lax.empty((D, 2 * H), F32))
        d_wu = _mm(hb_in, duv, ta=True, name=f'ffn{i}_up_v_dw', into=d_wu, into_col=H)
        dh = _mm(dug, f['wu'], tb=True, add=ds, add_scale=alpha, name=f'ffn{i}_up_g_dx')
        dh = _mm(duv, f['wu'], tb=True, add=dh, name=f'ffn{i}_up_v_dx', b_koff=H)
        return dh, d_wd, d_wu, jnp.concatenate([dwg, dwv], axis=1), jnp.concatenate([dbg, dbv], axis=1)

    ds4, ds4b, dg_f1, db_f1 = _ln_bwd(dh4, s4, ln('ln_ffn_g', 1), name='ln_ffn1_bwd')
    dh3, d_wd1, d_wu1, d_fcw1, d_fcb1 = ffn_bwd(ds4, ds4b, h3b, ffn1_saved, 1)
    ds3, ds3b, dg_m1, db_m1 = _ln_bwd(dh3, s3, ln('ln_mix_g', 1), name='ln_mix1_bwd')

    g['attn_out_proj'] = _mm(ob, ds3b, ta=True, name='attn_out_dw')[None]
    do = _mm(ds3b, w_ao, tb=True, name='attn_out_dx')
    dqn, dqr = _attention_dq(qn, qr, kn, kr, v, o, do, lse, S, scale, name='attn_bwd_dq')
    dkn, dv, dkr = _attention_dkv(qn, qr, kn, kr, v, o, do, lse, S, scale, name='attn_bwd_dkv')
    dqr_pre = _rope(dqr, cc, ss, transpose=True, name='q_rope_bwd')
    dkr_pre = _rope(dkr, cc, ss, transpose=True, name='k_rope_bwd')

    d_wqn = _mm(cq, dqn, ta=True, name='q_up_n_dw').reshape(q_rank, nh, LANES)
    d_wqr = _mm(cq, dqr_pre, ta=True, name='q_up_r_dw').reshape(q_rank, nh, LANES)[:, :, :MLA_ROPE]
    g['q_up_proj'] = jnp.concatenate([d_wqn, d_wqr], axis=2).reshape(1, q_rank, nh * (MLA_NOPE + MLA_ROPE))
    dcq = _mm(dqn, w_qn, tb=True, name='q_up_n_dx')
    dcq = _mm(dqr_pre, w_qr, tb=True, add=dcq, name='q_up_r_dx')
    dcq_pre, d_qg = _rms_bwd(dcq, cq_pre, q_g, name='q_norm_bwd')
    g['q_norm_g'] = d_qg
    g['q_down_proj'] = _mm(h2b, dcq_pre, ta=True, name='q_down_dw')[None]

    g['kv_up_k'] = _mm(ckvn, dkn, ta=True, name='kv_up_k_dw')
    g['kv_up_v'] = _mm(ckvn, dv, ta=True, name='kv_up_v_dw')
    dckvn = _mm(dkn, w_uk, tb=True, name='kv_up_k_dx')
    dckvn = _mm(dv, w_uv, tb=True, add=dckvn, name='kv_up_v_dx')
    dckv, d_kvg = _rms_bwd(dckvn, ckv, kv_g, name='kv_norm_bwd')
    g['kv_norm_g'] = d_kvg.reshape(kv_rank)
    g['kv_down_proj'] = jnp.concatenate(
        [_mm(h2b, dckv, ta=True, name='kv_down_c_dw'),
         _mm(h2b, dkr_pre, ta=True, name='kv_down_r_dw')[:, :MLA_ROPE]], axis=1)

    dh2 = _mm(dcq_pre, w_qd, tb=True, add=ds3, add_scale=alpha, name='q_down_dx')
    dh2 = _mm(dckv, w_kvc, tb=True, add=dh2, name='kv_down_c_dx')
    dh2 = _mm(dkr_pre, w_kvr, tb=True, add=dh2, name='kv_down_r_dx')

    ds2, ds2b, dg_f0, db_f0 = _ln_bwd(dh2, s2, ln('ln_ffn_g', 0), name='ln_ffn0_bwd')
    dh1, d_wd0, d_wu0, d_fcw0, d_fcb0 = ffn_bwd(ds2, ds2b, h1b, ffn0_saved, 0)
    ds1, ds1b, dg_m0, db_m0 = _ln_bwd(dh1, s1, ln('ln_mix_g', 0), name='ln_mix0_bwd')

    g['ssd_out_proj'] = _mm(yn, ds1b, ta=True, name='ssd_out_dw')[None]
    dyn = _mm(ds1b, w_out, tb=True, out_dtype=BF16, name='ssd_out_dx')
    dy1, dz, d_ng, d_dl = _gate_bwd(dyn, y_scan, xbc, z, d_lane, ssd_ng, d_inner, name='ssd_gate_bwd')
    g['ffn_up'] = jnp.stack([d_wu0, d_wu1])
    g['ffn_down'] = jnp.stack([d_wd0, d_wd1])
    (dxs, dB, dC, ddt_g, da_g), early = _scan_bwd(xbc, dt_pre, dt_bias, a_log, states, dy1, d_lane, S, d_inner,
                                                  name='ssd_scan_bwd',
                                                  exchange=reduce_early(g) if reduce_early else ())
    ddt_pre, d_bias, d_alog = _dt_bwd(ddt_g, da_g, dt_pre, dt_bias, a_log, name='ssd_dt_bwd')
    dxbc_pre, d_scw, d_scb = _ssd_conv_bwd(xbc_pre, xbc_conv, [dxs, dB, dC], ssd_cw, S, name='ssd_conv_bwd')
    g['ssd_in_proj'] = jnp.concatenate(
        [_mm(Xb, dz, ta=True, name='ssd_in_z_dw'), _mm(Xb, dxbc_pre, ta=True, name='ssd_in_xbc_dw'),
         _mm(Xb, ddt_pre, ta=True, name='ssd_in_dt_dw')[:, :n_heads_ssd]], axis=1)[None]
    g['ssd_conv_w'] = d_scw[None]
    g['ssd_conv_b'] = d_scb
    g['ssd_norm_g'] = d_ng
    g['ffn_conv_w'] = jnp.stack([d_fcw0, d_fcw1])
    dx = _mm(dz, w_z, tb=True, add=ds1, add_scale=alpha, name='ssd_in_z_dx')
    dx = _mm(dxbc_pre, w_xbc, tb=True, add=dx, name='ssd_in_xbc_dx', exchange=reduce_last(g) if reduce_last else ())
    dx, last = dx if reduce_last else (dx, None)
    dx = _mm(ddt_pre, w_dt, tb=True, add=dx, name='ssd_in_dt_dx')

    g['ssd_dt_bias'] = d_bias[:, :n_heads_ssd]
    g['ssd_A_log'] = d_alog[:, :n_heads_ssd]
    g['ssd_D'] = jnp.sum(d_dl.reshape(1, n_heads_ssd, SSD_HEAD_DIM), axis=2)
    g['ffn_conv_b'] = jnp.concatenate([d_fcb0, d_fcb1], axis=0)
    g['ln_mix_g'] = jnp.concatenate([dg_m0, dg_m1], axis=0)
    g['ln_mix_b'] = jnp.concatenate([db_m0, db_m1], axis=0)
    g['ln_ffn_g'] = jnp.concatenate([dg_f0, dg_f1], axis=0)
    g['ln_ffn_b'] = jnp.concatenate([db_f0, db_f1], axis=0)
    return sq, dx.reshape(Bn, S, D), g, early, last


ANY = pl.BlockSpec(memory_space=pl.ANY)


def _all_gather(xs, name):
    n = len(xs)

    def body(*refs):
        start, forward, finish = _gather_plan(refs[:n], refs[n:2 * n], *refs[2 * n:])
        start()
        forward()
        finish()

    return pl.pallas_call(
        body, name=name, out_shape=_gather_shapes(xs), in_specs=[ANY] * n, out_specs=[ANY] * n,
        scratch_shapes=_gather_sems(n),
    )(*xs)


def _gather_shapes(xs):
    return [jax.ShapeDtypeStruct((N_DEV,) + a.shape, a.dtype) for a in xs]


def _gather_sems(n):
    return [pltpu.SemaphoreType.DMA((7 * n,)), pltpu.SemaphoreType.DMA((7 * n,)), pltpu.SemaphoreType.DMA((n,))]


def _gather_plan(x_refs, out_refs, send_sems, recv_sems, local_sems):
    n = len(x_refs)
    x, y, c = lax.axis_index("x"), lax.axis_index("y"), lax.axis_index("c")
    me, sibling = (x, y, c), (x, y, 1 - c)
    chips = [(1 - x, y), (x, 1 - y), (1 - x, 1 - y)]

    def rows(i, px, py, pc):
        return out_refs[i].at[4 * px + 2 * py + pc]

    def copy(i, k, block, to, own=False):
        return pltpu.make_async_remote_copy(
            src_ref=x_refs[i] if own else rows(i, *block), dst_ref=rows(i, *block),
            send_sem=send_sems.at[7 * i + k], recv_sem=recv_sems.at[7 * i + k],
            device_id=to, device_id_type=MESH)

    def mine():
        return [pltpu.make_async_copy(x_refs[i], rows(i, *me), local_sems.at[i]) for i in range(n)]

    def first():
        out = []
        for i in range(n):
            out.append(copy(i, 0, me, sibling, own=True))
            out += [copy(i, 1 + j, me, (*chip, c), own=True) for j, chip in enumerate(chips)]
        return out

    def passed():
        return [copy(i, 4 + j, (*chip, c), sibling) for j, chip in enumerate(chips) for i in range(n)]

    def start():
        for cp in mine() + first():
            cp.start()

    def forward():
        for j, chip in enumerate(chips):
            for i in range(n):
                copy(i, 1 + j, (*chip, c), me).wait_recv()
                copy(i, 4 + j, (*chip, c), sibling).start()

    def finish():
        for i in range(n):
            copy(i, 0, sibling, me).wait_recv()
            for j, chip in enumerate(chips):
                copy(i, 4 + j, (*chip, 1 - c), me).wait_recv()
        for cp in first() + passed():
            cp.wait_send()
        for cp in mine():
            cp.wait()

    return start, forward, finish


def _exchange_sibling(gs, name):
    n = len(gs)

    def body(*refs):
        g_refs, r_refs = refs[:n], refs[n:2 * n]
        send_sems, recv_sems = refs[2 * n:]
        x, y, c = lax.axis_index("x"), lax.axis_index("y"), lax.axis_index("c")
        copies = [pltpu.make_async_remote_copy(
            src_ref=g_refs[i].at[2 * k + (1 - c)], dst_ref=r_refs[i].at[k], send_sem=send_sems.at[4 * i + k],
            recv_sem=recv_sems.at[4 * i + k], device_id=(x, y, 1 - c), device_id_type=MESH)
            for i in range(n) for k in range(4)]
        for cp in copies:
            cp.start()
        for cp in copies:
            cp.wait()

    return pl.pallas_call(
        body, name=name, out_shape=[jax.ShapeDtypeStruct((4,) + g.shape[1:], g.dtype) for g in gs],
        in_specs=[ANY] * n, out_specs=[ANY] * n,
        scratch_shapes=[pltpu.SemaphoreType.DMA((4 * n,)), pltpu.SemaphoreType.DMA((4 * n,))],
    )(*gs)


def _chips_shapes(parts):
    return [jax.ShapeDtypeStruct((3,) + p.shape[1:], p.dtype) for p in parts]


def _chips_sems(n):
    return [pltpu.SemaphoreType.DMA((3 * n,)), pltpu.SemaphoreType.DMA((3 * n,))]


def _chips_plan(p_refs, r_refs, send_sems, recv_sems):
    n = len(p_refs)
    x, y, c = lax.axis_index("x"), lax.axis_index("y"), lax.axis_index("c")
    chips = [(1 - x, y), (x, 1 - y), (1 - x, 1 - y)]

    def copies():
        return [pltpu.make_async_remote_copy(
            src_ref=p_refs[i].at[2 * cx + cy], dst_ref=r_refs[i].at[j], send_sem=send_sems.at[3 * i + j],
            recv_sem=recv_sems.at[3 * i + j], device_id=(cx, cy, c), device_id_type=MESH)
            for i in range(n) for j, (cx, cy) in enumerate(chips)]

    def start():
        for cp in copies():
            cp.start()

    def finish():
        for cp in copies():
            cp.wait()

    return start, finish


def _row_tile(rows, cols):
    want = max(8, (256 * 1024) // cols)
    for t in range(min(rows, want) // 8 * 8, 7, -8):
        if rows % t == 0:
            return t
    return rows


def _add_sibling(gfull, r1, core, name):
    _, rows, lanes = gfull.shape
    tr = _row_tile(rows, lanes)

    def body(c_ref, g_ref, r_ref, o_ref):
        o_ref[...] = g_ref[...] + r_ref[...]

    return pl.pallas_call(
        body, name=name,
        grid_spec=pltpu.PrefetchScalarGridSpec(
            num_scalar_prefetch=1, grid=(4, rows // tr),
            in_specs=[pl.BlockSpec((1, tr, lanes), lambda k, r, c_ref: (2 * k + c_ref[0], r, 0)),
                      pl.BlockSpec((1, tr, lanes), lambda k, r, c_ref: (k, r, 0))],
            out_specs=pl.BlockSpec((1, tr, lanes), lambda k, r, c_ref: (k, r, 0))),
        out_shape=jax.ShapeDtypeStruct((4, rows, lanes), F32), compiler_params=_cparams(),
    )(core, gfull, r1)


def _adam_math(wv, gv, mv, vv):
    m = ADAM_B1 * mv + (1.0 - ADAM_B1) * gv
    v = ADAM_B2 * vv + (1.0 - ADAM_B2) * (gv * gv)
    m_hat = m / (1.0 - ADAM_B1 ** ADAM_STEP)
    v_hat = v / (1.0 - ADAM_B2 ** ADAM_STEP)
    delta = -ADAM_LR * (m_hat / (jnp.sqrt(v_hat) + ADAM_EPS) + ADAM_WD * wv)
    return delta, m, v


def _adam_sharded(part, r2, chip, wts, m, v, name):
    rows, lanes = wts.shape
    tr = _row_tile(rows, lanes)

    def body(k_ref, p_ref, r_ref, w_ref, m_ref, v_ref, g_ref, d_ref, mo_ref, vo_ref):
        gv = p_ref[0] + r_ref[0] + r_ref[1] + r_ref[2]
        delta, mn, vn = _adam_math(w_ref[...], gv, m_ref[...], v_ref[...])
        g_ref[...] = gv
        d_ref[...] = delta
        mo_ref[...] = mn
        vo_ref[...] = vn

    flat = pl.BlockSpec((tr, lanes), lambda r, k_ref: (r, 0))
    sh = jax.ShapeDtypeStruct((rows, lanes), F32)
    return pl.pallas_call(
        body, name=name,
        grid_spec=pltpu.PrefetchScalarGridSpec(
            num_scalar_prefetch=1, grid=(rows // tr,),
            in_specs=[pl.BlockSpec((1, tr, lanes), lambda r, k_ref: (k_ref[0], r, 0)),
                      pl.BlockSpec((3, tr, lanes), lambda r, k_ref: (0, r, 0)), flat, flat, flat],
            out_specs=[flat, flat, flat, flat]),
        out_shape=[sh, sh, sh, sh], compiler_params=_cparams(),
    )(chip, part, r2, wts, m, v)


def _adam_replicated(gall, wts, m, v, name):
    rows, lanes = wts.shape

    def body(ga_ref, w_ref, m_ref, v_ref, g_ref, d_ref, mo_ref, vo_ref):
        gv = ga_ref[0]
        for k in range(1, N_DEV):
            gv = gv + ga_ref[k]
        delta, mn, vn = _adam_math(w_ref[...], gv, m_ref[...], v_ref[...])
        g_ref[...] = gv
        d_ref[...] = delta
        mo_ref[...] = mn
        vo_ref[...] = vn

    sh = jax.ShapeDtypeStruct((rows, lanes), F32)
    return pl.pallas_call(body, name=name, out_shape=[sh, sh, sh, sh], compiler_params=_cparams())(gall, wts, m, v)


def _pack(arrs, dtype, row_mult):
    flat = jnp.concatenate([a.reshape(-1).astype(dtype) for a in arrs])
    n = flat.shape[0]
    unit = LANES * row_mult
    total = -(-n // unit) * unit
    return jnp.pad(flat, (0, total - n)).reshape(total // LANES, LANES)


def _unpack(flat2d, shapes):
    flat = flat2d.reshape(-1)
    out, off = [], 0
    for shp in shapes:
        n = math.prod(shp)
        out.append(flat[off:off + n].reshape(shp))
        off += n
    return out


def _unpack_gathered(gathered, shapes, axes):
    flat = gathered.reshape(N_DEV, -1)
    out, off = [], 0
    for shp, ax in zip(shapes, axes):
        n = math.prod(shp)
        seg = flat[:, off:off + n].reshape((N_DEV,) + tuple(shp))
        out.append(jnp.concatenate([seg[i] for i in range(N_DEV)], axis=ax))
        off += n
    return out


def _pack_chunks(fulls, axes, row_mult):
    per_dev = []
    for full, ax in zip(fulls, axes):
        parts = jnp.stack(jnp.split(full, N_DEV, axis=ax))
        per_dev.append(parts.reshape(N_DEV, -1))
    flat = jnp.concatenate(per_dev, axis=1)
    n = flat.shape[1]
    unit = LANES * row_mult
    total = -(-n // unit) * unit
    return jnp.pad(flat, ((0, 0), (0, total - n))).reshape(N_DEV, total // LANES, LANES)


def kernel(x, positions, ssd_in_proj, ssd_conv_w, ssd_conv_b, ssd_dt_bias, ssd_A_log, ssd_D, ssd_norm_g, ssd_out_proj, kv_down_proj, kv_norm_g, kv_up_k, kv_up_v, q_down_proj, q_norm_g, q_up_proj, attn_out_proj, ffn_up, ffn_conv_w, ffn_conv_b, ffn_down, ln_mix_g, ln_mix_b, ln_ffn_g, ln_ffn_b, loss_target, m_ssd_in_proj, m_ssd_conv_w, m_ssd_conv_b, m_ssd_dt_bias, m_ssd_A_log, m_ssd_D, m_ssd_norm_g, m_ssd_out_proj, m_kv_down_proj, m_kv_norm_g, m_kv_up_k, m_kv_up_v, m_q_down_proj, m_q_norm_g, m_q_up_proj, m_attn_out_proj, m_ffn_up, m_ffn_conv_w, m_ffn_conv_b, m_ffn_down, m_ln_mix_g, m_ln_mix_b, m_ln_ffn_g, m_ln_ffn_b, v_ssd_in_proj, v_ssd_conv_w, v_ssd_conv_b, v_ssd_dt_bias, v_ssd_A_log, v_ssd_D, v_ssd_norm_g, v_ssd_out_proj, v_kv_down_proj, v_kv_norm_g, v_kv_up_k, v_kv_up_v, v_q_down_proj, v_q_norm_g, v_q_up_proj, v_attn_out_proj, v_ffn_up, v_ffn_conv_w, v_ffn_conv_b, v_ffn_down, v_ln_mix_g, v_ln_mix_b, v_ln_ffn_g, v_ln_ffn_b):
    given = dict(locals())
    wl = {n: given[n] for n in WEIGHTS}
    ml = {n: given['m_' + n] for n in WEIGHTS}
    vl = {n: given['v_' + n] for n in WEIGHTS}
    sharded_axis = dict(SHARDED)
    big = [n for n, _ in SHARDED if n not in SHARDED_F32]
    small = [n for n, _ in SHARDED if n in SHARDED_F32]

    def as2d(a):
        return a.reshape(-1, a.shape[-1])

    def to_full(gathered, n):
        shp, ax = wl[n].shape, sharded_axis[n]
        moved = jnp.moveaxis(gathered.reshape((N_DEV,) + shp), 0, ax)
        return moved.reshape(shp[:ax] + (N_DEV * shp[ax],) + shp[ax + 1:])

    def to_chunks(full_grad, n):
        shp, ax = wl[n].shape, sharded_axis[n]
        split = full_grad.reshape(shp[:ax] + (N_DEV, shp[ax]) + shp[ax + 1:])
        return jnp.moveaxis(split, ax, 0).reshape((N_DEV,) + as2d(wl[n]).shape)

    first = [n for n in big if n in GATHERED_FIRST]
    late = [n for n in big if n not in GATHERED_FIRST]
    full = {n: wl[n] for n in REPLICATED}
    gathered = _all_gather([as2d(wl[n]).astype(BF16) for n in first] + [_pack([wl[n] for n in small], F32, 8)],
                           name='gather_weights')
    for n, gat in zip(first, gathered[:-1]):
        full[n] = to_full(gat, n)
    full.update(zip(small, _unpack_gathered(gathered[-1], [wl[n].shape for n in small],
                                            [sharded_axis[n] for n in small])))

    cx, cy, cc = lax.axis_index("x"), lax.axis_index("y"), lax.axis_index("c")
    core = jnp.reshape(cc, (1,)).astype(jnp.int32)
    chip = jnp.reshape(2 * cx + cy, (1,)).astype(jnp.int32)
    early_names = [n for n in big if n not in REDUCED_LAST]
    last_names = [n for n in big if n in REDUCED_LAST]
    parts = {}

    def to_sibling(chunked, names, tag):
        r1 = _exchange_sibling(chunked, name=f'grads_to_sibling_{tag}')
        return [_add_sibling(gf, r, core, name=f'grads_add_sibling_{n}') for n, gf, r in zip(names, chunked, r1)]

    def reduce_early(grads):
        parts.update(zip(early_names, to_sibling([to_chunks(grads[n], n) for n in early_names], early_names, 'early')))
        return [parts[n] for n in early_names]

    def reduce_last(grads):
        chunked = [to_chunks(grads[n], n) for n in last_names]
        chunked.append(_pack_chunks([grads[n] for n in small], [sharded_axis[n] for n in small], 8))
        parts.update(zip(last_names + ['small'], to_sibling(chunked, last_names + ['small'], 'last')))
        return [parts[n] for n in last_names + ['small']]

    sq, grad_x, grads, r2_early, r2_last = _local_step(
        x, positions, full, loss_target,
        late=([as2d(wl[n]).astype(BF16) for n in late], lambda gats: {n: to_full(g, n) for n, g in zip(late, gats)}),
        reduce_early=reduce_early, reduce_last=reduce_last)
    loss = lax.psum(0.5 * jnp.sum(sq) / x.shape[-1], ("x", "y", "c"))

    r2 = dict(zip(early_names, r2_early))
    r2.update(zip(last_names + ['small'], r2_last))
    res = {}
    kinds = ('grad', 'delta', 'new_m', 'new_v')
    for n in big:
        outs = _adam_sharded(parts[n], r2[n], chip, as2d(wl[n]), as2d(ml[n]), as2d(vl[n]), name=f'adamw_{n}')
        for kind, a in zip(kinds, outs):
            res[kind, n] = a.reshape(wl[n].shape)
    outs = _adam_sharded(parts['small'], r2['small'], chip, _pack([wl[n] for n in small], F32, 8),
                         _pack([ml[n] for n in small], F32, 8), _pack([vl[n] for n in small], F32, 8),
                         name='adamw_small_sharded')
    for kind, packed in zip(kinds, outs):
        for n, a in zip(small, _unpack(packed, [wl[n].shape for n in small])):
            res[kind, n] = a

    rep = list(REPLICATED)
    rshapes = [wl[n].shape for n in rep]
    gall, = _all_gather([_pack([grads[n] for n in rep], F32, 8)], name='gather_replicated_grads')
    outs = _adam_replicated(gall, _pack([wl[n] for n in rep], F32, 8), _pack([ml[n] for n in rep], F32, 8),
                            _pack([vl[n] for n in rep], F32, 8), name='adamw_replicated')
    for kind, packed in zip(('grad', 'delta', 'new_m', 'new_v'), outs):
        for n, a in zip(rep, _unpack(packed, rshapes)):
            res[kind, n] = a

    return (loss, grad_x, *[res['grad', n] for n in WEIGHTS], *[res['delta', n] for n in WEIGHTS],
            *[res['new_m', n] for n in WEIGHTS], *[res['new_v', n] for n in WEIGHTS])
```

```python
import functools
import math

import jax
import jax.numpy as jnp
from jax import lax
from jax.experimental import pallas as pl
from jax.experimental.pallas import tpu as pltpu

F32 = jnp.float32
BF16 = jnp.bfloat16
MESH = pl.DeviceIdType.MESH

N_DEV = 8
LANES = 128
MM_TILES = (1024, 1408, 896, 512, 384, 256, 128)
MM_VMEM_BUDGET = 38 * 1024 * 1024
VMEM_LIMIT = 56 * 1024 * 1024
LN_EPS = 1e-5
RMS_EPS = 1e-6
SSD_HEAD_DIM = 64
SSD_N_GROUPS = 8
SSD_D_STATE = 128
SSD_CHUNK = 128
MLA_NOPE = 128
MLA_ROPE = 64
MLA_V = 128
ROPE_THETA = 10000.0
ADAM_LR = 0.001
ADAM_B1 = 0.9
ADAM_B2 = 0.999
ADAM_EPS = 1e-08
ADAM_WD = 0.01
ADAM_STEP = 10
NEG_BIG = -1e30

SHARDED = (('ssd_in_proj', 2), ('ssd_conv_w', 2), ('ssd_conv_b', 1), ('ssd_norm_g', 1), ('ssd_out_proj', 1),
           ('kv_down_proj', 0), ('kv_up_k', 1), ('kv_up_v', 1), ('q_down_proj', 1), ('q_up_proj', 2),
           ('attn_out_proj', 1), ('ffn_up', 2), ('ffn_conv_w', 2), ('ffn_down', 1))
SHARDED_F32 = ('ssd_conv_w', 'ssd_conv_b', 'ssd_norm_g', 'ffn_conv_w')
GATHERED_FIRST = ('ssd_in_proj',)
REDUCED_LAST = ('ssd_in_proj',)
REPLICATED = ('ssd_dt_bias', 'ssd_A_log', 'ssd_D', 'kv_norm_g', 'q_norm_g', 'ffn_conv_b',
              'ln_mix_g', 'ln_mix_b', 'ln_ffn_g', 'ln_ffn_b')
WEIGHTS = ('ssd_in_proj', 'ssd_conv_w', 'ssd_conv_b', 'ssd_dt_bias', 'ssd_A_log', 'ssd_D', 'ssd_norm_g',
           'ssd_out_proj', 'kv_down_proj', 'kv_norm_g', 'kv_up_k', 'kv_up_v', 'q_down_proj', 'q_norm_g',
           'q_up_proj', 'attn_out_proj', 'ffn_up', 'ffn_conv_w', 'ffn_conv_b', 'ffn_down', 'ln_mix_g',
           'ln_mix_b', 'ln_ffn_g', 'ln_ffn_b')


def _cparams():
    return pltpu.CompilerParams(vmem_limit_bytes=VMEM_LIMIT)


def _tile(n, cands):
    for c in cands:
        if n % c == 0:
            return c
    return n


def _sigmoid(x):
    return 1.0 / (1.0 + jnp.exp(-x))


def _iota(shape, axis):
    return lax.broadcasted_iota(jnp.int32, shape, axis)


def _mm(a, b, *, ta=False, tb=False, add=None, add_scale=1.0, out_dtype=F32, name, b_koff=0, into=None, into_col=0,
        exchange=()):
    if ta:
        K, M = a.shape
    else:
        M, K = a.shape
    if tb:
        N, Kb = b.shape
    else:
        Kb, N = b.shape
    assert b_koff + K <= Kb, (a.shape, b.shape, ta, tb, b_koff)
    tm = _tile(M, MM_TILES)
    tn = _tile(N, MM_TILES)
    tk = K if K <= MM_TILES[0] and b_koff == 0 and Kb == K else _tile(K, MM_TILES)
    nk = K // tk
    assert b_koff % tk == 0 and (into is None or (into.shape[0] == M and into.dtype == out_dtype))
    kb = b_koff // tk

    def vmem_estimate(tm_, tn_):
        ins = 2 * (tm_ * tk * a.dtype.itemsize + tk * tn_ * b.dtype.itemsize)
        outs = tm_ * tn_ * (2 * jnp.dtype(out_dtype).itemsize + 4 + (4 if nk > 1 else 0))
        return ins + outs + (2 * tm_ * tn_ * add.dtype.itemsize if add is not None else 0)

    while vmem_estimate(tm, tn) > MM_VMEM_BUDGET:
        can_m, can_n = tm % (2 * LANES) == 0, tn % (2 * LANES) == 0
        if can_m and (tm >= tn or not can_n):
            tm //= 2
        elif can_n:
            tn //= 2
        else:
            break

    assert into_col % tn == 0
    jb = into_col // tn

    nx = len(exchange)
    n_in = (add is not None) + (into is not None)
    grid = (M // tm, N // tn, nk)

    def body(a_ref, b_ref, *rest):
        add_ref = rest[0] if add is not None else None
        o_ref = rest[n_in + nx]
        acc = rest[n_in + 2 * nx + 1] if nk > 1 else None
        k = pl.program_id(2)
        if nx:
            start, finish_exchange = _chips_plan(rest[n_in:n_in + nx], rest[n_in + nx + 1:n_in + 2 * nx + 1],
                                                 *rest[len(rest) - 2:])
            t, total = _grid_step(grid)
            pl.when(t == 0)(start)

        if ta:
            av = a_ref[...].astype(BF16).T
        else:
            av = a_ref[...].astype(BF16)
        bv = b_ref[...].astype(BF16)
        dn = (((1,), (1 if tb else 0,)), ((), ()))
        part = lax.dot_general(av, bv, dn, preferred_element_type=F32)

        def finish(r):
            if add is not None:
                r = r + add_scale * add_ref[...].astype(F32)
            o_ref[...] = r.astype(out_dtype)

        if nk == 1:
            finish(part)
        else:
            @pl.when(k == 0)
            def _():
                acc[...] = part

            @pl.when(k > 0)
            def _():
                acc[...] += part

            @pl.when(k == nk - 1)
            def _():
                finish(acc[...])

        if nx:
            pl.when(t == total - 1)(finish_exchange)

    a_spec = (pl.BlockSpec((tk, tm), lambda i, j, k: (k, i)) if ta
              else pl.BlockSpec((tm, tk), lambda i, j, k: (i, k)))
    b_spec = (pl.BlockSpec((tn, tk), lambda i, j, k: (j, k + kb)) if tb
              else pl.BlockSpec((tk, tn), lambda i, j, k: (k + kb, j)))
    in_specs = [a_spec, b_spec]
    args = [a, b]
    if add is not None:
        in_specs.append(pl.BlockSpec((tm, tn), lambda i, j, k: (i, j)))
        args.append(add)
    aliases = {}
    if into is not None:
        aliases = {len(args): 0}
        in_specs.append(pl.BlockSpec(memory_space=pl.ANY))
        args.append(into)
    any_spec = pl.BlockSpec(memory_space=pl.ANY)
    outs = pl.pallas_call(
        body, name=name, grid=grid,
        in_specs=in_specs + [any_spec] * nx,
        out_specs=[pl.BlockSpec((tm, tn), lambda i, j, k: (i, j + jb))] + [any_spec] * nx,
        out_shape=[jax.ShapeDtypeStruct((M, N) if into is None else into.shape, out_dtype)] + _chips_shapes(exchange),
        scratch_shapes=([pltpu.VMEM((tm, tn), F32)] if nk > 1 else []) + (_chips_sems(nx) if nx else []),
        input_output_aliases=aliases, compiler_params=_cparams(),
    )(*args, *exchange)
    return (outs[0], list(outs[1:])) if nx else outs[0]


def _ln_fwd(h, mix, g, b, alpha, name):
    T, D = h.shape
    tm = _tile(T, (256, 128))

    def body(h_ref, m_ref, g_ref, b_ref, y_ref, s_ref, yb_ref):
        s = alpha * h_ref[...] + m_ref[...]
        mu = jnp.mean(s, axis=1, keepdims=True)
        xc = s - mu
        var = jnp.mean(xc * xc, axis=1, keepdims=True)
        y = xc * lax.rsqrt(var + LN_EPS) * g_ref[...] + b_ref[...]
        y_ref[...] = y
        s_ref[...] = s
        yb_ref[...] = y.astype(BF16)

    row = pl.BlockSpec((tm, D), lambda i: (i, 0))
    vec = pl.BlockSpec((1, D), lambda i: (0, 0))
    return pl.pallas_call(
        body, name=name, grid=(T // tm,), in_specs=[row, row, vec, vec], out_specs=[row, row, row],
        out_shape=[jax.ShapeDtypeStruct((T, D), F32)] * 2 + [jax.ShapeDtypeStruct((T, D), BF16)],
        compiler_params=_cparams(),
    )(h, mix, g, b)


def _ln_bwd(dy, s, g, name):
    T, D = s.shape
    tm = _tile(T, (256, 128))

    def body(dy_ref, s_ref, g_ref, ds_ref, dsb_ref, dg_ref, db_ref):
        @pl.when(pl.program_id(0) == 0)
        def _():
            dg_ref[...] = jnp.zeros_like(dg_ref)
            db_ref[...] = jnp.zeros_like(db_ref)

        sv = s_ref[...]
        dyv = dy_ref[...]
        mu = jnp.mean(sv, axis=1, keepdims=True)
        xc = sv - mu
        rstd = lax.rsqrt(jnp.mean(xc * xc, axis=1, keepdims=True) + LN_EPS)
        xhat = xc * rstd
        dxh = dyv * g_ref[...]
        m1 = jnp.mean(dxh, axis=1, keepdims=True)
        m2 = jnp.mean(dxh * xhat, axis=1, keepdims=True)
        ds = rstd * (dxh - m1 - xhat * m2)
        ds_ref[...] = ds
        dsb_ref[...] = ds.astype(BF16)
        dg_ref[...] += jnp.sum(dyv * xhat, axis=0, keepdims=True)
        db_ref[...] += jnp.sum(dyv, axis=0, keepdims=True)

    row = pl.BlockSpec((tm, D), lambda i: (i, 0))
    vec = pl.BlockSpec((1, D), lambda i: (0, 0))
    return pl.pallas_call(
        body, name=name, grid=(T // tm,), in_specs=[row, row, vec], out_specs=[row, row, vec, vec],
        out_shape=[jax.ShapeDtypeStruct((T, D), F32), jax.ShapeDtypeStruct((T, D), BF16),
                   jax.ShapeDtypeStruct((1, D), F32), jax.ShapeDtypeStruct((1, D), F32)],
        compiler_params=_cparams(),
    )(dy, s, g)


def _loss_head(y, target, name):
    T, D = y.shape
    tm = _tile(T, (256, 128))

    def body(y_ref, t_ref, sq_ref, dy_ref):
        @pl.when(pl.program_id(0) == 0)
        def _():
            sq_ref[...] = jnp.zeros_like(sq_ref)

        e = y_ref[...] - t_ref[...]
        sq_ref[...] += jnp.sum(e * e, axis=0, keepdims=True)
        dy_ref[...] = e * (1.0 / D)

    row = pl.BlockSpec((tm, D), lambda i: (i, 0))
    vec = pl.BlockSpec((1, D), lambda i: (0, 0))
    return pl.pallas_call(
        body, name=name, grid=(T // tm,), in_specs=[row, row], out_specs=[vec, row],
        out_shape=[jax.ShapeDtypeStruct((1, D), F32), jax.ShapeDtypeStruct((T, D), F32)],
        compiler_params=_cparams(),
    )(y, target)


def _shift_down(x, j):
    if j == 0:
        return x
    return jnp.where(_iota(x.shape, 0) >= j, pltpu.roll(x, j, 0), 0.0)


def _shift_up(x, j):
    if j == 0:
        return x
    n = x.shape[0]
    return jnp.where(_iota(x.shape, 0) < n - j, pltpu.roll(x, n - j, 0), 0.0)


def _conv_val(x, w_ref, b_ref):
    width = w_ref.shape[0]
    y = b_ref[...] + w_ref[width - 1:width, :] * x
    for j in range(1, width):
        y = y + w_ref[width - 1 - j:width - j, :] * _shift_down(x, j)
    return y


def _conv_bwd_val(x, dc, w_ref, dw_ref, db_ref):
    width = w_ref.shape[0]
    dx = w_ref[width - 1:width, :] * dc
    dw_ref[width - 1:width, :] += jnp.sum(dc * x, axis=0, keepdims=True)
    for j in range(1, width):
        sj = _shift_up(dc, j)
        dx = dx + w_ref[width - 1 - j:width - j, :] * sj
        dw_ref[width - 1 - j:width - j, :] += jnp.sum(sj * x, axis=0, keepdims=True)
    db_ref[...] += jnp.sum(dc, axis=0, keepdims=True)
    return dx


def _ffn_specs(H, S, width):
    cb = _tile(H, (2 * LANES, LANES))
    nC = H // cb
    return dict(
        g=pl.BlockSpec((S, cb), lambda j, b: (b, j)), v=pl.BlockSpec((S, cb), lambda j, b: (b, j + nC)),
        wg=pl.BlockSpec((width, cb), lambda j, b: (0, j)), wv=pl.BlockSpec((width, cb), lambda j, b: (0, j + nC)),
        bg=pl.BlockSpec((1, cb), lambda j, b: (0, j)), bv=pl.BlockSpec((1, cb), lambda j, b: (0, j + nC))), nC


def _ffn_act_fwd(u, cw, cb, S, name):
    T, H2 = u.shape
    H = H2 // 2
    sp, nC = _ffn_specs(H, S, cw.shape[0])

    def body(ug_ref, uv_ref, wg_ref, wv_ref, bg_ref, bv_ref, a_ref):
        g = _conv_val(ug_ref[...], wg_ref, bg_ref)
        v = _conv_val(uv_ref[...], wv_ref, bv_ref)
        a_ref[...] = (g * _sigmoid(g) * v).astype(BF16)

    return pl.pallas_call(
        body, name=name, grid=(nC, T // S),
        in_specs=[sp['g'], sp['v'], sp['wg'], sp['wv'], sp['bg'], sp['bv']], out_specs=sp['g'],
        out_shape=jax.ShapeDtypeStruct((T, H), BF16), compiler_params=_cparams(),
    )(u, u, cw, cw, cb, cb)


def _ffn_act_bwd(u, da, cw, cb, S, name):
    T, H2 = u.shape
    H = H2 // 2
    width = cw.shape[0]
    sp, nC = _ffn_specs(H, S, width)

    def body(ug_ref, uv_ref, da_ref, wg_ref, wv_ref, bg_ref, bv_ref,
             dug_ref, duv_ref, dwg_ref, dwv_ref, dbg_ref, dbv_ref):
        @pl.when(pl.program_id(1) == 0)
        def _():
            for r in (dwg_ref, dwv_ref, dbg_ref, dbv_ref):
                r[...] = jnp.zeros_like(r)

        xg = [_shift_down(ug_ref[...], j) for j in range(width)]
        xv = [_shift_down(uv_ref[...], j) for j in range(width)]

        def conv(xs, w_ref, b_ref):
            y = b_ref[...]
            for j in range(width):
                y = y + w_ref[width - 1 - j:width - j, :] * xs[j]
            return y

        def conv_bwd(xs, dc, w_ref, dw_ref, db_ref):
            dx = w_ref[width - 1:width, :] * dc
            for j in range(1, width):
                dx = dx + w_ref[width - 1 - j:width - j, :] * _shift_up(dc, j)
            for j in range(width):
                dw_ref[width - 1 - j:width - j, :] += jnp.sum(dc * xs[j], axis=0, keepdims=True)
            db_ref[...] += jnp.sum(dc, axis=0, keepdims=True)
            return dx

        g = conv(xg, wg_ref, bg_ref)
        v = conv(xv, wv_ref, bv_ref)
        dav = da_ref[...].astype(F32)
        sg = _sigmoid(g)
        dg = dav * v * sg * (1.0 + g * (1.0 - sg))
        dv = dav * g * sg
        dug_ref[...] = conv_bwd(xg, dg, wg_ref, dwg_ref, dbg_ref).astype(BF16)
        duv_ref[...] = conv_bwd(xv, dv, wv_ref, dwv_ref, dbv_ref).astype(BF16)

    act = jax.ShapeDtypeStruct((T, H), BF16)
    wsh = jax.ShapeDtypeStruct((width, H), F32)
    bsh = jax.ShapeDtypeStruct((1, H), F32)
    return pl.pallas_call(
        body, name=name, grid=(nC, T // S),
        in_specs=[sp['g'], sp['v'], sp['g'], sp['wg'], sp['wv'], sp['bg'], sp['bv']],
        out_specs=[sp['g'], sp['g'], sp['wg'], sp['wg'], sp['bg'], sp['bg']],
        out_shape=[act, act, wsh, wsh, bsh, bsh], compiler_params=_cparams(),
    )(u, u, da, cw, cw, cb, cb)


def _ssd_conv_fwd(x, cw, cb, S, name):
    T, C = x.shape
    Cb = _tile(C, (256, 128))
    width = cw.shape[0]

    def body(x_ref, w_ref, b_ref, y_ref, c_ref):
        c = _conv_val(x_ref[...], w_ref, b_ref)
        c_ref[...] = c
        y_ref[...] = c * _sigmoid(c)

    blk = pl.BlockSpec((S, Cb), lambda j, b: (b, j))
    wblk = pl.BlockSpec((width, Cb), lambda j, b: (0, j))
    bblk = pl.BlockSpec((1, Cb), lambda j, b: (0, j))
    return pl.pallas_call(
        body, name=name, grid=(C // Cb, T // S), in_specs=[blk, wblk, bblk], out_specs=[blk, blk],
        out_shape=[jax.ShapeDtypeStruct((T, C), F32)] * 2, compiler_params=_cparams(),
    )(x, cw, cb)


def _ssd_conv_bwd(x, c, dys, cw, S, name):
    T, C = x.shape
    Cb = _tile(C, (256, 128))
    width = cw.shape[0]
    assert all(d.shape[1] % Cb == 0 for d in dys) and sum(d.shape[1] for d in dys) == C

    def part(dy, j0, dx_prev, k):
        n = dy.shape[1] // Cb

        def body(x_ref, c_ref, dy_ref, w_ref, prev_ref, dx_ref, dw_ref, db_ref):
            @pl.when(pl.program_id(1) == 0)
            def _():
                dw_ref[...] = jnp.zeros_like(dw_ref)
                db_ref[...] = jnp.zeros_like(db_ref)

            cval = c_ref[...]
            sc = _sigmoid(cval)
            dc = dy_ref[...] * sc * (1.0 + cval * (1.0 - sc))
            dx_ref[...] = _conv_bwd_val(x_ref[...], dc, w_ref, dw_ref, db_ref).astype(BF16)

        wide = pl.BlockSpec((S, Cb), lambda j, b: (b, j + j0))
        return pl.pallas_call(
            body, name=f'{name}_{k}', grid=(n, T // S),
            in_specs=[wide, wide, pl.BlockSpec((S, Cb), lambda j, b: (b, j)),
                      pl.BlockSpec((width, Cb), lambda j, b: (0, j + j0)), pl.BlockSpec(memory_space=pl.ANY)],
            out_specs=[wide, pl.BlockSpec((width, Cb), lambda j, b: (0, j)), pl.BlockSpec((1, Cb), lambda j, b: (0, j))],
            out_shape=[jax.ShapeDtypeStruct((T, C), BF16), jax.ShapeDtypeStruct((width, n * Cb), F32),
                       jax.ShapeDtypeStruct((1, n * Cb), F32)],
            input_output_aliases={4: 0}, compiler_params=_cparams(),
        )(x, c, dy, cw, dx_prev)

    dx = lax.empty((T, C), BF16)
    dws, dbs, j0 = [], [], 0
    for k, dy in enumerate(dys):
        dx, dw, db = part(dy, j0, dx, k)
        dws.append(dw)
        dbs.append(db)
        j0 += dy.shape[1] // Cb
    return dx, jnp.concatenate(dws, axis=1), jnp.concatenate(dbs, axis=1)


def _softplus(x):
    e = jnp.exp(-jnp.abs(x))
    return jnp.maximum(x, 0.0) + jnp.where(e < 1e-4, e * (1.0 - 0.5 * e), jnp.log(1.0 + e))


def _cumsum_rows(x):
    n = x.shape[0]
    row = _iota(x.shape, 0)
    k = 1
    while k < n:
        x = x + jnp.where(row >= k, pltpu.roll(x, k, 0), 0.0)
        k *= 2
    return x


def _rev_cumsum_rows(x):
    n = x.shape[0]
    row = _iota(x.shape, 0)
    k = 1
    while k < n:
        x = x + jnp.where(row < n - k, pltpu.roll(x, n - k, 0), 0.0)
        k *= 2
    return x


def _col(x, lane_ids, h):
    return jnp.sum(jnp.where(lane_ids == h, x, 0.0), axis=1, keepdims=True)


def _bdot(a, b, dims):
    return lax.dot_general(a.astype(BF16), b.astype(BF16), (dims, ((), ())), preferred_element_type=F32)


NN = ((1,), (0,))
NT = ((1,), (1,))


SCAN_GROUPS = 4


def _scan_specs(nc, d_inner, hpg):
    L = SSD_CHUNK
    G = SSD_N_GROUPS
    GP = SCAN_GROUPS
    gw = GP * hpg * SSD_HEAD_DIM
    bw = GP * LANES
    assert G % GP == 0 and d_inner % bw == 0
    nb = d_inner // bw
    return dict(
        xs=pl.BlockSpec((L, gw), lambda b, q, c: (b * nc + c, q)),
        B=pl.BlockSpec((L, bw), lambda b, q, c: (b * nc + c, nb + q)),
        C=pl.BlockSpec((L, bw), lambda b, q, c: (b * nc + c, nb + G // GP + q)),
        dt=pl.BlockSpec((L, LANES), lambda b, q, c: (b * nc + c, 0)),
        vec=pl.BlockSpec((1, LANES), lambda b, q, c: (0, 0)),
        gvec=pl.BlockSpec((1, gw), lambda b, q, c: (0, q)),
        grp=pl.BlockSpec((L, bw), lambda b, q, c: (b * nc + c, q)),
        st=pl.BlockSpec((1, GP * hpg // 2, SSD_D_STATE, LANES), lambda b, q, c: (b * nc + c, q, 0, 0)),
    )


def _grid_step(dims):
    t, total = 0, 1
    for ax, d in enumerate(dims):
        t = t * d + pl.program_id(ax)
        total *= d
    return t, total


def _scan_fwd(xbc, dt_pre, dt_bias, a_log, S, d_inner, name, gather=()):
    T = xbc.shape[0]
    Bn = T // S
    L = SSD_CHUNK
    G = SSD_N_GROUPS
    nc = S // L
    hpg = d_inner // SSD_HEAD_DIM // G
    pairs = hpg // 2
    sp = _scan_specs(nc, d_inner, hpg)
    GP = SCAN_GROUPS
    ng = len(gather)

    def body(*refs):
        xs_ref, b_ref, c_ref, dtp_ref, bias_ref, alog_ref = refs[:6]
        y_ref, st_ref = refs[6 + ng:8 + ng]
        state = refs[8 + 2 * ng]
        g = pl.program_id(1)

        if ng:
            start, forward, finish = _gather_plan(refs[6:6 + ng], refs[8 + ng:8 + 2 * ng], *refs[9 + 2 * ng:])
            t, total = _grid_step((Bn, G // GP, nc))
            pl.when(t == 0)(start)
            pl.when(t == total // 2)(forward)

        @pl.when(pl.program_id(2) == 0)
        def _():
            state[...] = jnp.zeros_like(state)

        dt = _softplus(dtp_ref[...] + bias_ref[...])
        cum = _cumsum_rows(dt * (-jnp.exp(alog_ref[...])))
        tril = _iota((L, L), 1) <= _iota((L, L), 0)
        lane = _iota((L, LANES), 1)
        lane1 = _iota((1, LANES), 1)
        last = _iota((L, 1), 0) == L - 1
        for gi, p in [(gi, p) for gi in range(GP) for p in range(pairs)]:
            if p == 0:
                Bm = b_ref[:, gi * LANES:(gi + 1) * LANES]
                Cm = c_ref[:, gi * LANES:(gi + 1) * LANES]
                Gm = _bdot(Cm, Bm, NT)
                BmT = Bm.T
            sp_ = gi * pairs + p
            cols = pl.ds(sp_ * LANES, LANES)
            S_in = state[sp_]
            st_ref[0, sp_] = S_in
            x_pair = xs_ref[:, cols]
            y_pair = jnp.zeros((L, LANES), F32)
            z_pair = jnp.zeros((L, LANES), F32)
            e_pair = jnp.zeros((L, LANES), F32)
            el_pair = jnp.zeros((1, LANES), F32)
            for k in range(2):
                h = (g * GP + gi) * hpg + 2 * p + k
                cum_col = _col(cum, lane, h)
                dt_col = _col(dt, lane, h)
                cb = jnp.broadcast_to(cum_col, (L, L))
                decay = jnp.exp(jnp.where(tril, cb - cb.T, NEG_BIG))
                mk = (lane >= SSD_HEAD_DIM) if k else (lane < SSD_HEAD_DIM)
                xd = jnp.where(mk, x_pair * dt_col, 0.0)
                y_pair = y_pair + _bdot(Gm * decay, xd, NN)
                cum_l = jnp.sum(jnp.where(last, cum_col, 0.0), axis=0, keepdims=True)
                e_pair = jnp.where(mk, jnp.exp(cum_col), e_pair)
                z_pair = z_pair + xd * jnp.exp(cum_l - cum_col)
                el_pair = jnp.where((lane1 >= SSD_HEAD_DIM) if k else (lane1 < SSD_HEAD_DIM), jnp.exp(cum_l), el_pair)
            y_pair = y_pair + e_pair * _bdot(Cm, S_in, NN)
            state[sp_] = S_in * el_pair + _bdot(BmT, z_pair, NN)
            y_ref[:, cols] = y_pair

        if ng:
            pl.when(t == total - 1)(finish)

    outs = pl.pallas_call(
        body, name=name, grid=(Bn, G // GP, nc),
        in_specs=[sp['xs'], sp['B'], sp['C'], sp['dt'], sp['vec'], sp['vec']] + [ANY] * ng,
        out_specs=[sp['xs'], sp['st']] + [ANY] * ng,
        out_shape=[jax.ShapeDtypeStruct((T, d_inner), F32),
                   jax.ShapeDtypeStruct((Bn * nc, G * pairs, SSD_D_STATE, LANES), F32)] + _gather_shapes(gather),
        scratch_shapes=[pltpu.VMEM((GP * pairs, SSD_D_STATE, LANES), F32)] + (_gather_sems(ng) if ng else []),
        compiler_params=_cparams(),
    )(xbc, xbc, xbc, dt_pre, dt_bias, a_log, *gather)
    return outs[0], outs[1], list(outs[2:])


def _scan_bwd(xbc, dt_pre, dt_bias, a_log, states, dy, d_lane, S, d_inner, name, exchange=()):
    T = xbc.shape[0]
    Bn = T // S
    L = SSD_CHUNK
    G = SSD_N_GROUPS
    nc = S // L
    hpg = d_inner // SSD_HEAD_DIM // G
    pairs = hpg // 2
    sp = _scan_specs(nc, d_inner, hpg)
    GP = SCAN_GROUPS
    nx = len(exchange)

    def rev(spec):
        im = spec.index_map
        return pl.BlockSpec(spec.block_shape, lambda b, g, c: im(b, g, nc - 1 - c))

    def body(*refs):
        xs_ref, b_ref, c_ref, dtp_ref, bias_ref, alog_ref, st_ref, dy_ref, dl_ref = refs[:9]
        dxs_ref, db_ref, dc_ref, ddt_ref, da_ref = refs[9 + nx:14 + nx]
        dstate = refs[14 + 2 * nx]
        g = pl.program_id(1)

        if nx:
            start, finish = _chips_plan(refs[9:9 + nx], refs[14 + nx:14 + 2 * nx], *refs[15 + 2 * nx:])
            t, total = _grid_step((Bn, G // GP, nc))
            pl.when(t == 0)(start)

        @pl.when(pl.program_id(2) == 0)
        def _():
            dstate[...] = jnp.zeros_like(dstate)

        dt = _softplus(dtp_ref[...] + bias_ref[...])
        cum = _cumsum_rows(dt * (-jnp.exp(alog_ref[...])))
        tril = _iota((L, L), 1) <= _iota((L, L), 0)
        lane = _iota((L, LANES), 1)
        lane1 = _iota((1, LANES), 1)
        last = _iota((L, 1), 0) == L - 1
        for gi, p in [(gi, p) for gi in range(GP) for p in range(pairs)]:
            gcols = pl.ds(gi * LANES, LANES)
            if p == 0:
                Bm = b_ref[:, gcols]
                Cm = c_ref[:, gcols]
                Gm = _bdot(Cm, Bm, NT)
                CmT = Cm.T
                dG = jnp.zeros((L, L), F32)
                dB = jnp.zeros((L, LANES), F32)
                dC = jnp.zeros((L, LANES), F32)
                dcum = jnp.zeros((L, LANES), F32)
                ddt = jnp.zeros((L, LANES), F32)
            sp_ = gi * pairs + p
            cols = pl.ds(sp_ * LANES, LANES)
            S_in = st_ref[0, sp_]
            dS = dstate[sp_]
            x_pair = xs_ref[:, cols]
            dy_pair = dy_ref[:, cols]
            Q = _bdot(Cm, S_in, NN)
            dZ = _bdot(Bm, dS, NN)
            prod = dS * S_in
            e_pair = jnp.zeros((L, LANES), F32)
            z_pair = jnp.zeros((L, LANES), F32)
            el_pair = jnp.zeros((1, LANES), F32)
            dx_pair = dl_ref[:, cols] * dy_pair
            for k in range(2):
                h = (g * GP + gi) * hpg + 2 * p + k
                cum_col = _col(cum, lane, h)
                dt_col = _col(dt, lane, h)
                cb = jnp.broadcast_to(cum_col, (L, L))
                decay = jnp.exp(jnp.where(tril, cb - cb.T, NEG_BIG))
                mk = (lane >= SSD_HEAD_DIM) if k else (lane < SSD_HEAD_DIM)
                mk1 = (lane1 >= SSD_HEAD_DIM) if k else (lane1 < SSD_HEAD_DIM)
                xd = jnp.where(mk, x_pair * dt_col, 0.0)
                dy_k = jnp.where(mk, dy_pair, 0.0)
                Wm = Gm * decay
                dWm = jnp.where(tril, _bdot(dy_k, xd, NT), 0.0)
                dxd = _bdot(Wm.T, dy_k, NN)
                dseg = dWm * Wm
                dG = dG + dWm * decay
                dcum_col = (jnp.sum(dseg, axis=1, keepdims=True)
                            - jnp.sum(dseg.T, axis=1, keepdims=True))
                cum_l = jnp.sum(jnp.where(last, cum_col, 0.0), axis=0, keepdims=True)
                e_col = jnp.exp(cum_col)
                w_col = jnp.exp(cum_l - cum_col)
                el = jnp.exp(cum_l)
                dcum_col = dcum_col + jnp.sum(dy_k * Q, axis=1, keepdims=True) * e_col
                de_e = jnp.sum(dZ * xd, axis=1, keepdims=True) * w_col
                dcum_col = dcum_col - de_e
                dcum_l = (jnp.sum(de_e, axis=0, keepdims=True)
                          + el * jnp.sum(jnp.sum(jnp.where(mk, prod, 0.0), axis=1, keepdims=True),
                                         axis=0, keepdims=True))
                dcum_col = dcum_col + jnp.where(last, dcum_l, 0.0)
                dxd = dxd + jnp.where(mk, dZ * w_col, 0.0)
                dx_pair = dx_pair + dxd * dt_col
                ddt = jnp.where(lane == h, jnp.sum(dxd * x_pair, axis=1, keepdims=True), ddt)
                dcum = jnp.where(lane == h, dcum_col, dcum)
                e_pair = jnp.where(mk, e_col, e_pair)
                z_pair = z_pair + xd * w_col
                el_pair = jnp.where(mk1, el, el_pair)
            dQ = e_pair * dy_pair
            dC = dC + _bdot(dQ, S_in, NT)
            dB = dB + _bdot(z_pair, dS, NT)
            dstate[sp_] = dS * el_pair + _bdot(CmT, dQ, NN)
            dxs_ref[:, cols] = dx_pair
            if p == pairs - 1:
                dc_ref[:, gcols] = dC + _bdot(dG, Bm, NN)
                db_ref[:, gcols] = dB + _bdot(dG.T, Cm, NN)
                ddt_ref[:, gcols] = ddt
                da_ref[:, gcols] = _rev_cumsum_rows(dcum)

        if nx:
            pl.when(t == total - 1)(finish)

    grp = jax.ShapeDtypeStruct((T, G * LANES), F32)
    outs = pl.pallas_call(
        body, name=name, grid=(Bn, G // GP, nc),
        in_specs=[rev(sp['xs']), rev(sp['B']), rev(sp['C']), rev(sp['dt']), sp['vec'], sp['vec'],
                  rev(sp['st']), rev(sp['xs']), sp['gvec']] + [ANY] * nx,
        out_specs=[rev(sp['xs']), rev(sp['grp']), rev(sp['grp']), rev(sp['grp']), rev(sp['grp'])] + [ANY] * nx,
        out_shape=[jax.ShapeDtypeStruct((T, d_inner), F32), grp, grp, grp, grp] + _chips_shapes(exchange),
        scratch_shapes=[pltpu.VMEM((GP * pairs, SSD_D_STATE, LANES), F32)] + (_chips_sems(nx) if nx else []),
        compiler_params=_cparams(),
    )(xbc, xbc, xbc, dt_pre, dt_bias, a_log, states, dy, d_lane, *exchange)
    return outs[:5], list(outs[5:])


def _dt_bwd(ddt_g, da_g, dt_pre, dt_bias, a_log, name):
    T = dt_pre.shape[0]
    G = SSD_N_GROUPS
    tm = _tile(T, (512, 256, 128))

    def body(ddt_ref, da_ref, dtp_ref, bias_ref, alog_ref, dx_ref, dbias_ref, dal_ref):
        @pl.when(pl.program_id(0) == 0)
        def _():
            dbias_ref[...] = jnp.zeros_like(dbias_ref)
            dal_ref[...] = jnp.zeros_like(dal_ref)

        ddt = ddt_ref[:, 0:LANES]
        da = da_ref[:, 0:LANES]
        for gi in range(1, G):
            ddt = ddt + ddt_ref[:, gi * LANES:(gi + 1) * LANES]
            da = da + da_ref[:, gi * LANES:(gi + 1) * LANES]
        xv = dtp_ref[...] + bias_ref[...]
        A = -jnp.exp(alog_ref[...])
        dx = (ddt + da * A) * _sigmoid(xv)
        dx_ref[...] = dx.astype(BF16)
        dbias_ref[...] += jnp.sum(dx, axis=0, keepdims=True)
        dal_ref[...] += jnp.sum(da * _softplus(xv), axis=0, keepdims=True) * A

    wide = pl.BlockSpec((tm, G * LANES), lambda i: (i, 0))
    row = pl.BlockSpec((tm, LANES), lambda i: (i, 0))
    vec = pl.BlockSpec((1, LANES), lambda i: (0, 0))
    vsh = jax.ShapeDtypeStruct((1, LANES), F32)
    return pl.pallas_call(
        body, name=name, grid=(T // tm,), in_specs=[wide, wide, row, vec, vec], out_specs=[row, vec, vec],
        out_shape=[jax.ShapeDtypeStruct((T, LANES), BF16), vsh, vsh], compiler_params=_cparams(),
    )(ddt_g, da_g, dt_pre, dt_bias, a_log)


def _gate_fwd(y, xbc, z, d_lane, ng, d_inner, name):
    T = y.shape[0]
    G = SSD_N_GROUPS
    gw = d_inner // G
    tm = _tile(T, (512, 256, 128))

    def body(y_ref, x_ref, z_ref, dl_ref, ng_ref, o_ref):
        zv = z_ref[...]
        y2 = (y_ref[...] + dl_ref[...] * x_ref[...]) * (zv * _sigmoid(zv))
        r = lax.rsqrt(jnp.mean(y2 * y2, axis=1, keepdims=True) + LN_EPS)
        o_ref[...] = (y2 * r * ng_ref[...]).astype(BF16)

    blk = pl.BlockSpec((tm, gw), lambda g, i: (i, g))
    vec = pl.BlockSpec((1, gw), lambda g, i: (0, g))
    return pl.pallas_call(
        body, name=name, grid=(G, T // tm), in_specs=[blk, blk, blk, vec, vec], out_specs=blk,
        out_shape=jax.ShapeDtypeStruct((T, d_inner), BF16), compiler_params=_cparams(),
    )(y, xbc, z, d_lane, ng)


def _gate_bwd(dyn, y, xbc, z, d_lane, ng, d_inner, name):
    T = y.shape[0]
    G = SSD_N_GROUPS
    gw = d_inner // G
    tm = _tile(T, (512, 256, 128))

    def body(dyn_ref, y_ref, x_ref, z_ref, dl_ref, ng_ref, dy_ref, dz_ref, dng_ref, ddl_ref):
        @pl.when(pl.program_id(1) == 0)
        def _():
            dng_ref[...] = jnp.zeros_like(dng_ref)
            ddl_ref[...] = jnp.zeros_like(ddl_ref)

        zv = z_ref[...]
        xv = x_ref[...]
        sz = _sigmoid(zv)
        y1 = y_ref[...] + dl_ref[...] * xv
        y2 = y1 * (zv * sz)
        r = lax.rsqrt(jnp.mean(y2 * y2, axis=1, keepdims=True) + LN_EPS)
        dn = dyn_ref[...].astype(F32)
        dng_ref[...] += jnp.sum(dn * y2 * r, axis=0, keepdims=True)
        do = dn * ng_ref[...]
        dy2 = r * do - y2 * (r * r * r) * jnp.mean(do * y2, axis=1, keepdims=True)
        dz_ref[...] = (dy2 * y1 * sz * (1.0 + zv * (1.0 - sz))).astype(BF16)
        dy1 = dy2 * (zv * sz)
        dy_ref[...] = dy1
        ddl_ref[...] += jnp.sum(dy1 * xv, axis=0, keepdims=True)

    blk = pl.BlockSpec((tm, gw), lambda g, i: (i, g))
    vec = pl.BlockSpec((1, gw), lambda g, i: (0, g))
    vsh = jax.ShapeDtypeStruct((1, d_inner), F32)
    return pl.pallas_call(
        body, name=name, grid=(G, T // tm), in_specs=[blk, blk, blk, blk, vec, vec],
        out_specs=[blk, blk, vec, vec],
        out_shape=[jax.ShapeDtypeStruct((T, d_inner), F32), jax.ShapeDtypeStruct((T, d_inner), BF16), vsh, vsh],
        compiler_params=_cparams(),
    )(dyn, y, xbc, z, d_lane, ng)


def _rms_fwd(x, g, name):
    T, R = x.shape
    tm = _tile(T, (512, 256, 128))

    def body(x_ref, g_ref, y_ref):
        xv = x_ref[...]
        y = xv * lax.rsqrt(jnp.mean(xv * xv, axis=1, keepdims=True) + RMS_EPS) * g_ref[...]
        y_ref[...] = y.astype(BF16)

    row = pl.BlockSpec((tm, R), lambda i: (i, 0))
    vec = pl.BlockSpec((1, R), lambda i: (0, 0))
    return pl.pallas_call(
        body, name=name, grid=(T // tm,), in_specs=[row, vec], out_specs=row,
        out_shape=jax.ShapeDtypeStruct((T, R), BF16), compiler_params=_cparams(),
    )(x, g)


def _rms_bwd(dy, x, g, name):
    T, R = x.shape
    tm = _tile(T, (512, 256, 128))

    def body(dy_ref, x_ref, g_ref, dx_ref, dg_ref):
        @pl.when(pl.program_id(0) == 0)
        def _():
            dg_ref[...] = jnp.zeros_like(dg_ref)

        xv = x_ref[...]
        dyv = dy_ref[...]
        r = lax.rsqrt(jnp.mean(xv * xv, axis=1, keepdims=True) + RMS_EPS)
        dg_ref[...] += jnp.sum(dyv * xv * r, axis=0, keepdims=True)
        do = dyv * g_ref[...]
        dx = r * do - xv * (r * r * r) * jnp.mean(do * xv, axis=1, keepdims=True)
        dx_ref[...] = dx.astype(BF16)

    row = pl.BlockSpec((tm, R), lambda i: (i, 0))
    vec = pl.BlockSpec((1, R), lambda i: (0, 0))
    return pl.pallas_call(
        body, name=name, grid=(T // tm,), in_specs=[row, row, vec], out_specs=[row, vec],
        out_shape=[jax.ShapeDtypeStruct((T, R), BF16), jax.ShapeDtypeStruct((1, R), F32)],
        compiler_params=_cparams(),
    )(dy, x, g)


def _swap_halves(x):
    half = MLA_ROPE // 2
    return jnp.where(_iota(x.shape, 1) < half, pltpu.roll(x, LANES - half, 1), pltpu.roll(x, half, 1))


def _rope(x, cc, ss, *, transpose, sum_heads=False, name):
    T, W = x.shape
    nh = W // LANES
    tm = _tile(T, (512, 256, 128))
    n_out = 1 if sum_heads else nh

    def body(x_ref, cc_ref, ss_ref, o_ref):
        ccv = cc_ref[...]
        ssv = ss_ref[...]
        if sum_heads:
            xv = x_ref[:, 0:LANES]
            for hh in range(1, nh):
                xv = xv + x_ref[:, hh * LANES:(hh + 1) * LANES]
            heads = [xv]
        else:
            heads = [x_ref[:, hh * LANES:(hh + 1) * LANES] for hh in range(nh)]
        for hh, xv in enumerate(heads):
            if transpose:
                r = xv * ccv + _swap_halves(xv * ssv)
            else:
                r = xv * ccv + _swap_halves(xv) * ssv
            r = jnp.where(_iota(r.shape, 1) < MLA_ROPE, r, 0.0)
            o_ref[:, hh * LANES:(hh + 1) * LANES] = r.astype(BF16)

    return pl.pallas_call(
        body, name=name, grid=(T // tm,),
        in_specs=[pl.BlockSpec((tm, W), lambda i: (i, 0)), pl.BlockSpec((tm, LANES), lambda i: (i, 0)),
                  pl.BlockSpec((tm, LANES), lambda i: (i, 0))],
        out_specs=pl.BlockSpec((tm, n_out * LANES), lambda i: (i, 0)),
        out_shape=jax.ShapeDtypeStruct((T, n_out * LANES), BF16), compiler_params=_cparams(),
    )(x, cc, ss)


ATT_BLOCK = 512
ATT_SUB = 256
LOG2E = 1.4426950408889634


def _att_geometry(S):
    tb = _tile(S, (ATT_BLOCK, 256))
    return tb, S // tb, tb // ATT_SUB


def _heads_per_step(nh):
    return 2 if nh % 2 == 0 else 1


def _att_scores(a, b, diag, kv_major, off):
    s = _bdot(a, b, NT)
    if diag:
        q_ax, k_ax = (1, 0) if kv_major else (0, 1)
        s = jnp.where(_iota(s.shape, k_ax) <= _iota(s.shape, q_ax) + off, s, NEG_BIG)
    return s


def _attention_fwd(qn, qr, kn, kr, v, S, scale, name):
    T, W = qn.shape
    nh = W // LANES
    Bn = T // S
    tb, n, C = _att_geometry(S)
    pairs = [(i, j) for i in range(n) for j in range(i + 1)]
    it = jnp.asarray([p[0] for p in pairs], jnp.int32)
    jt = jnp.asarray([p[1] for p in pairs], jnp.int32)
    c2 = scale * LOG2E

    HP = _heads_per_step(nh)

    def body(it_ref, jt_ref, qn_ref, qr_ref, kn_ref, kr_ref, v_ref, o_ref, ob_ref, lse_ref, m_sc, l_sc, acc):
        p = pl.program_id(2)
        i = it_ref[p]
        j = jt_ref[p]

        @pl.when(j == 0)
        def _():
            m_sc[...] = jnp.full_like(m_sc, NEG_BIG)
            l_sc[...] = jnp.zeros_like(l_sc)
            acc[...] = jnp.zeros_like(acc)

        def step(diag):
            for hh in range(HP):
                hs = pl.ds(hh * LANES, LANES)
                qc = jnp.concatenate([qn_ref[:, hs], qr_ref[:, hs]], axis=1)
                kc = jnp.concatenate([kn_ref[:, hs], kr_ref[...]], axis=1)
                st = _att_scores(kc, qc, diag, True, 0)
                m_old = m_sc[hh]
                m_new = jnp.maximum(m_old, jnp.max(st, axis=0, keepdims=True))
                pt = jnp.exp2((st - m_new) * c2)
                corr = jnp.exp2((m_old - m_new) * c2)
                l_sc[hh] = corr * l_sc[hh] + jnp.sum(pt, axis=0, keepdims=True)
                acc[hh] = corr * acc[hh] + _bdot(v_ref[:, hs].T, pt, NN)
                m_sc[hh] = m_new

        @pl.when(j < i)
        def _():
            step(False)

        @pl.when(j == i)
        def _():
            step(True)
            for hh in range(HP):
                hs = pl.ds(hh * LANES, LANES)
                ov = (acc[hh] / l_sc[hh]).T
                o_ref[:, hs] = ov
                ob_ref[:, hs] = ov.astype(BF16)
                lse_row = m_sc[hh] * scale + jnp.log(l_sc[hh])
                lse_ref[:, hs] = jnp.broadcast_to(lse_row, (LANES, tb)).T

    qspec = pl.BlockSpec((tb, HP * LANES), lambda b, h, p, it_, jt_: (b * n + it_[p], h))
    kspec = pl.BlockSpec((tb, HP * LANES), lambda b, h, p, it_, jt_: (b * n + jt_[p], h))
    krspec = pl.BlockSpec((tb, LANES), lambda b, h, p, it_, jt_: (b * n + jt_[p], 0))
    return pl.pallas_call(
        body, name=name,
        grid_spec=pltpu.PrefetchScalarGridSpec(
            num_scalar_prefetch=2, grid=(Bn, nh // HP, len(pairs)),
            in_specs=[qspec, qspec, kspec, krspec, kspec], out_specs=[qspec, qspec, qspec],
            scratch_shapes=[pltpu.VMEM((HP, 1, tb), F32), pltpu.VMEM((HP, 1, tb), F32),
                            pltpu.VMEM((HP, LANES, tb), F32)]),
        out_shape=[jax.ShapeDtypeStruct((T, W), F32), jax.ShapeDtypeStruct((T, W), BF16),
                   jax.ShapeDtypeStruct((T, W), F32)],
        compiler_params=_cparams(),
    )(it, jt, qn, qr, kn, kr, v)


def _attention_dq(qn, qr, kn, kr, v, o, do, lse, S, scale, name):
    T, W = qn.shape
    nh = W // LANES
    Bn = T // S
    tb, n, C = _att_geometry(S)
    pairs = [(i, j) for i in range(n) for j in range(i + 1)]
    it = jnp.asarray([p[0] for p in pairs], jnp.int32)
    jt = jnp.asarray([p[1] for p in pairs], jnp.int32)
    c2 = scale * LOG2E
    HP = _heads_per_step(nh)

    def body(it_ref, jt_ref, qn_ref, qr_ref, kn_ref, kr_ref, v_ref, o_ref, do_ref, lse_ref,
             dqn_ref, dqr_ref, acc, di_sc, lse_sc):
        p = pl.program_id(2)
        i = it_ref[p]
        j = jt_ref[p]

        @pl.when(j == 0)
        def _():
            acc[...] = jnp.zeros_like(acc)
            for hh in range(HP):
                hs = pl.ds(hh * LANES, LANES)
                di_sc[hh] = jnp.sum(do_ref[:, hs] * o_ref[:, hs], axis=1, keepdims=True)
                lse_sc[hh] = jnp.max(lse_ref[:, hs], axis=1, keepdims=True) * LOG2E

        def step(diag):
            for hh in range(HP):
                hs = pl.ds(hh * LANES, LANES)
                qc = jnp.concatenate([qn_ref[:, hs], qr_ref[:, hs]], axis=1)
                for c in range(C):
                    r0 = c * ATT_SUB if diag else 0
                    rows = pl.ds(r0, tb - r0)
                    ks = pl.ds(c * ATT_SUB, ATT_SUB)
                    kc = jnp.concatenate([kn_ref[ks, hs], kr_ref[ks, :]], axis=1)
                    s = _att_scores(qc[r0:], kc, diag, False, 0)
                    pr = jnp.exp2(s * c2 - lse_sc[hh, rows, :])
                    dp = _bdot(do_ref[rows, hs], v_ref[ks, hs], NT)
                    ds = pr * (dp - di_sc[hh, rows, :]) * scale
                    acc[hh, rows, :] += _bdot(ds, kc, NN)

        @pl.when(j < i)
        def _():
            step(False)

        @pl.when(j == i)
        def _():
            step(True)
            for hh in range(HP):
                hs = pl.ds(hh * LANES, LANES)
                dqn_ref[:, hs] = acc[hh, :, 0:LANES].astype(BF16)
                dqr_ref[:, hs] = acc[hh, :, LANES:2 * LANES]

    qspec = pl.BlockSpec((tb, HP * LANES), lambda b, h, p, it_, jt_: (b * n + it_[p], h))
    kspec = pl.BlockSpec((tb, HP * LANES), lambda b, h, p, it_, jt_: (b * n + jt_[p], h))
    krspec = pl.BlockSpec((tb, LANES), lambda b, h, p, it_, jt_: (b * n + jt_[p], 0))
    return pl.pallas_call(
        body, name=name,
        grid_spec=pltpu.PrefetchScalarGridSpec(
            num_scalar_prefetch=2, grid=(Bn, nh // HP, len(pairs)),
            in_specs=[qspec, qspec, kspec, krspec, kspec, qspec, qspec, qspec], out_specs=[qspec, qspec],
            scratch_shapes=[pltpu.VMEM((HP, tb, 2 * LANES), F32), pltpu.VMEM((HP, tb, 1), F32),
                            pltpu.VMEM((HP, tb, 1), F32)]),
        out_shape=[jax.ShapeDtypeStruct((T, W), BF16), jax.ShapeDtypeStruct((T, W), F32)],
        compiler_params=_cparams(),
    )(it, jt, qn, qr, kn, kr, v, o, do, lse)


def _attention_dkv(qn, qr, kn, kr, v, o, do, lse, S, scale, name):
    T, W = qn.shape
    nh = W // LANES
    Bn = T // S
    tb, n, C = _att_geometry(S)
    HP = _heads_per_step(nh)
    triples = [(j, h, i) for j in range(n) for h in range(nh // HP) for i in range(j, n)]
    jt = jnp.asarray([t[0] for t in triples], jnp.int32)
    ht = jnp.asarray([t[1] for t in triples], jnp.int32)
    it = jnp.asarray([t[2] for t in triples], jnp.int32)
    c2 = scale * LOG2E

    def as_row(col):
        return jnp.broadcast_to(col, (ATT_SUB, LANES)).T[0:1, :]

    def body(jt_ref, ht_ref, it_ref, qn_ref, qr_ref, kn_ref, kr_ref, v_ref, o_ref, do_ref, lse_ref,
             dkn_ref, dv_ref, dkr_ref, akc, av):
        p = pl.program_id(1)
        j = jt_ref[p]
        h = ht_ref[p]
        i = it_ref[p]

        @pl.when(jnp.logical_and(h == 0, i == j))
        def _():
            dkr_ref[...] = jnp.zeros_like(dkr_ref)

        @pl.when(i == j)
        def _():
            akc[...] = jnp.zeros_like(akc)
            av[...] = jnp.zeros_like(av)

        def step(diag):
            for hh in range(HP):
                hs = pl.ds(hh * LANES, LANES)
                kc = jnp.concatenate([kn_ref[:, hs], kr_ref[...]], axis=1)
                for c in range(C):
                    rows = pl.ds(c * ATT_SUB, ATT_SUB)
                    k1 = (c + 1) * ATT_SUB if diag else tb
                    ks = pl.ds(0, k1)
                    qc = jnp.concatenate([qn_ref[rows, hs], qr_ref[rows, hs]], axis=1)
                    st = _att_scores(kc[:k1], qc, diag, True, c * ATT_SUB)
                    dov = do_ref[rows, hs]
                    lse_row = as_row(jnp.max(lse_ref[rows, hs], axis=1, keepdims=True)) * LOG2E
                    di_row = as_row(jnp.sum(dov * o_ref[rows, hs], axis=1, keepdims=True))
                    pt = jnp.exp2(st * c2 - lse_row)
                    dst = pt * (_bdot(v_ref[ks, hs], dov, NT) - di_row) * scale
                    av[ks, hs] += _bdot(pt, dov, NN)
                    akc[hh, ks, :] += _bdot(dst, qc, NN)

        @pl.when(i > j)
        def _():
            step(False)

        @pl.when(i == j)
        def _():
            step(True)

        @pl.when(i == n - 1)
        def _():
            for hh in range(HP):
                dkn_ref[:, pl.ds(hh * LANES, LANES)] = akc[hh, :, 0:LANES].astype(BF16)
                dkr_ref[...] += akc[hh, :, LANES:2 * LANES]
            dv_ref[...] = av[...].astype(BF16)

    qspec = pl.BlockSpec((tb, HP * LANES), lambda b, p, jt_, ht_, it_: (b * n + it_[p], ht_[p]))
    kspec = pl.BlockSpec((tb, HP * LANES), lambda b, p, jt_, ht_, it_: (b * n + jt_[p], ht_[p]))
    krspec = pl.BlockSpec((tb, LANES), lambda b, p, jt_, ht_, it_: (b * n + jt_[p], 0))
    return pl.pallas_call(
        body, name=name,
        grid_spec=pltpu.PrefetchScalarGridSpec(
            num_scalar_prefetch=3, grid=(Bn, len(triples)),
            in_specs=[qspec, qspec, kspec, krspec, kspec, qspec, qspec, qspec],
            out_specs=[kspec, kspec, krspec],
            scratch_shapes=[pltpu.VMEM((HP, tb, 2 * LANES), F32), pltpu.VMEM((tb, HP * LANES), F32)]),
        out_shape=[jax.ShapeDtypeStruct((T, W), BF16), jax.ShapeDtypeStruct((T, W), BF16),
                   jax.ShapeDtypeStruct((T, LANES), F32)],
        compiler_params=_cparams(),
    )(jt, ht, it, qn, qr, kn, kr, v, o, do, lse)


def _pad_cols(w, n):
    return jnp.pad(w, ((0, 0), (0, n - w.shape[1])))


def _local_step(x, positions, w, target, late=None, reduce_early=None, reduce_last=None):
    w = dict(w)
    Bn, S, D = x.shape
    T = Bn * S
    depth = w['ln_mix_g'].shape[0]
    alpha = (2 * depth) ** 0.25
    d_inner = w['ssd_norm_g'].shape[1]
    n_heads_ssd = d_inner // SSD_HEAD_DIM
    G = SSD_N_GROUPS
    gn = G * SSD_D_STATE
    conv_dim = d_inner + 2 * gn
    hpg = n_heads_ssd // G
    nh = D // MLA_NOPE
    kv_rank = w['kv_norm_g'].shape[0]
    q_rank = w['q_norm_g'].shape[1]
    H = w['ffn_conv_b'].shape[1] // 2
    scale = (MLA_NOPE + MLA_ROPE) ** -0.5
    assert S % SSD_CHUNK == 0 and hpg % 2 == 0 and SSD_D_STATE == LANES and n_heads_ssd <= LANES

    X = x.reshape(T, D)
    tgt = target.reshape(T, D)

    inv_freq = 1.0 / (ROPE_THETA ** (jnp.arange(0, MLA_ROPE, 2, dtype=F32) / MLA_ROPE))
    ang = positions.reshape(T).astype(F32)[:, None] * inv_freq
    cos, sin = jnp.cos(ang), jnp.sin(ang)
    zpad = jnp.zeros((T, LANES - MLA_ROPE), F32)
    cc = jnp.concatenate([cos, cos, zpad], axis=1)
    ss = jnp.concatenate([-sin, sin, zpad], axis=1)

    w_in = w['ssd_in_proj'][0]
    w_z = w_in[:, :d_inner]
    w_xbc = w_in[:, d_inner:d_inner + conv_dim]
    w_dt = _pad_cols(w_in[:, d_inner + conv_dim:], LANES)
    ssd_cw = w['ssd_conv_w'][0]
    ssd_cb = w['ssd_conv_b']
    dt_bias = _pad_cols(w['ssd_dt_bias'], LANES)
    a_log = _pad_cols(w['ssd_A_log'], LANES)
    d_lane = jnp.repeat(w['ssd_D'], SSD_HEAD_DIM, axis=1)
    ssd_ng = w['ssd_norm_g']

    Xb = X.astype(BF16)
    z = _mm(Xb, w_z, name='ssd_in_z')
    xbc_pre = _mm(Xb, w_xbc, name='ssd_in_xbc')
    dt_pre = _mm(Xb, w_dt, name='ssd_in_dt')
    xbc, xbc_conv = _ssd_conv_fwd(xbc_pre, ssd_cw, ssd_cb, S, name='ssd_conv')
    y_scan, states, gathered = _scan_fwd(xbc, dt_pre, dt_bias, a_log, S, d_inner, name='ssd_scan',
                                         gather=late[0] if late else ())
    if late:
        w.update(late[1](gathered))

    w_out = w['ssd_out_proj'][0]
    w_kvc = w['kv_down_proj'][:, :kv_rank]
    w_kvr = _pad_cols(w['kv_down_proj'][:, kv_rank:], LANES)
    kv_g = w['kv_norm_g'].reshape(1, kv_rank)
    w_uk = w['kv_up_k']
    w_uv = w['kv_up_v']
    w_qd = w['q_down_proj'][0]
    q_g = w['q_norm_g']
    qu = w['q_up_proj'][0].reshape(q_rank, nh, MLA_NOPE + MLA_ROPE)
    w_qn = qu[:, :, :MLA_NOPE].reshape(q_rank, nh * LANES)
    w_qr = jnp.pad(qu[:, :, MLA_NOPE:], ((0, 0), (0, 0), (0, LANES - MLA_ROPE))).reshape(q_rank, nh * LANES)
    w_ao = w['attn_out_proj'][0]

    def ffn_parts(i):
        return dict(wu=w['ffn_up'][i], cw=w['ffn_conv_w'][i], cb=w['ffn_conv_b'][i:i + 1], wd=w['ffn_down'][i])

    def ln(name, i):
        return w[name][i:i + 1]

    def ffn_fwd(hb, i):
        f = ffn_parts(i)
        u = _mm(hb, f['wu'], name=f'ffn{i}_up')
        a = _ffn_act_fwd(u, f['cw'], f['cb'], S, name=f'ffn{i}_act')
        return _mm(a, f['wd'], name=f'ffn{i}_down'), (u, a)

    yn = _gate_fwd(y_scan, xbc, z, d_lane, ssd_ng, d_inner, name='ssd_gate')
    mix0 = _mm(yn, w_out, name='ssd_out')
    h1, s1, h1b = _ln_fwd(X, mix0, ln('ln_mix_g', 0), ln('ln_mix_b', 0), alpha, name='ln_mix0')
    ff0, ffn0_saved = ffn_fwd(h1b, 0)
    h2, s2, h2b = _ln_fwd(h1, ff0, ln('ln_ffn_g', 0), ln('ln_ffn_b', 0), alpha, name='ln_ffn0')

    ckv = _mm(h2b, w_kvc, name='kv_down_c')
    kr_pre = _mm(h2b, w_kvr, name='kv_down_r')
    ckvn = _rms_fwd(ckv, kv_g, name='kv_norm')
    kr = _rope(kr_pre, cc, ss, transpose=False, name='k_rope')
    kn = _mm(ckvn, w_uk, out_dtype=BF16, name='kv_up_k')
    v = _mm(ckvn, w_uv, out_dtype=BF16, name='kv_up_v')
    cq_pre = _mm(h2b, w_qd, name='q_down')
    cq = _rms_fwd(cq_pre, q_g, name='q_norm')
    qn = _mm(cq, w_qn, out_dtype=BF16, name='q_up_n')
    qr_pre = _mm(cq, w_qr, name='q_up_r')
    qr = _rope(qr_pre, cc, ss, transpose=False, name='q_rope')
    o, ob, lse = _attention_fwd(qn, qr, kn, kr, v, S, scale, name='attn_fwd')
    mix1 = _mm(ob, w_ao, name='attn_out')
    h3, s3, h3b = _ln_fwd(h2, mix1, ln('ln_mix_g', 1), ln('ln_mix_b', 1), alpha, name='ln_mix1')
    ff1, ffn1_saved = ffn_fwd(h3b, 1)
    h4, s4, _ = _ln_fwd(h3, ff1, ln('ln_ffn_g', 1), ln('ln_ffn_b', 1), alpha, name='ln_ffn1')
    sq, dh4 = _loss_head(h4, tgt, name='loss_head')

    g = {}

    def ffn_bwd(ds, dsb, hb_in, saved, i):
        f = ffn_parts(i)
        u, a = saved
        d_wd = _mm(a, dsb, ta=True, name=f'ffn{i}_down_dw')
        da = _mm(dsb, f['wd'], tb=True, out_dtype=BF16, name=f'ffn{i}_down_dx')
        dug, duv, dwg, dwv, dbg, dbv = _ffn_act_bwd(u, da, f['cw'], f['cb'], S, name=f'ffn{i}_act_bwd')
        d_wu = _mm(hb_in, dug, ta=True, name=f'ffn{i}_up_g_dw', into=lax.empty((D, 2 * H), F32))
        d_wu = _mm(hb_in, duv, ta=True, name=f'ffn{i}_up_v_dw', into=d_wu, into_col=H)
        dh = _mm(dug, f['wu'], tb=True, add=ds, add_scale=alpha, name=f'ffn{i}_up_g_dx')
        dh = _mm(duv, f['wu'], tb=True, add=dh, name=f'ffn{i}_up_v_dx', b_koff=H)
        return dh, d_wd, d_wu, jnp.concatenate([dwg, dwv], axis=1), jnp.concatenate([dbg, dbv], axis=1)

    ds4, ds4b, dg_f1, db_f1 = _ln_bwd(dh4, s4, ln('ln_ffn_g', 1), name='ln_ffn1_bwd')
    dh3, d_wd1, d_wu1, d_fcw1, d_fcb1 = ffn_bwd(ds4, ds4b, h3b, ffn1_saved, 1)
    ds3, ds3b, dg_m1, db_m1 = _ln_bwd(dh3, s3, ln('ln_mix_g', 1), name='ln_mix1_bwd')

    g['attn_out_proj'] = _mm(ob, ds3b, ta=True, name='attn_out_dw')[None]
    do = _mm(ds3b, w_ao, tb=True, name='attn_out_dx')
    dqn, dqr = _attention_dq(qn, qr, kn, kr, v, o, do, lse, S, scale, name='attn_bwd_dq')
    dkn, dv, dkr = _attention_dkv(qn, qr, kn, kr, v, o, do, lse, S, scale, name='attn_bwd_dkv')
    dqr_pre = _rope(dqr, cc, ss, transpose=True, name='q_rope_bwd')
    dkr_pre = _rope(dkr, cc, ss, transpose=True, name='k_rope_bwd')

    d_wqn = _mm(cq, dqn, ta=True, name='q_up_n_dw').reshape(q_rank, nh, LANES)
    d_wqr = _mm(cq, dqr_pre, ta=True, name='q_up_r_dw').reshape(q_rank, nh, LANES)[:, :, :MLA_ROPE]
    g['q_up_proj'] = jnp.concatenate([d_wqn, d_wqr], axis=2).reshape(1, q_rank, nh * (MLA_NOPE + MLA_ROPE))
    dcq = _mm(dqn, w_qn, tb=True, name='q_up_n_dx')
    dcq = _mm(dqr_pre, w_qr, tb=True, add=dcq, name='q_up_r_dx')
    dcq_pre, d_qg = _rms_bwd(dcq, cq_pre, q_g, name='q_norm_bwd')
    g['q_norm_g'] = d_qg
    g['q_down_proj'] = _mm(h2b, dcq_pre, ta=True, name='q_down_dw')[None]

    g['kv_up_k'] = _mm(ckvn, dkn, ta=True, name='kv_up_k_dw')
    g['kv_up_v'] = _mm(ckvn, dv, ta=True, name='kv_up_v_dw')
    dckvn = _mm(dkn, w_uk, tb=True, name='kv_up_k_dx')
    dckvn = _mm(dv, w_uv, tb=True, add=dckvn, name='kv_up_v_dx')
    dckv, d_kvg = _rms_bwd(dckvn, ckv, kv_g, name='kv_norm_bwd')
    g['kv_norm_g'] = d_kvg.reshape(kv_rank)
    g['kv_down_proj'] = jnp.concatenate(
        [_mm(h2b, dckv, ta=True, name='kv_down_c_dw'),
         _mm(h2b, dkr_pre, ta=True, name='kv_down_r_dw')[:, :MLA_ROPE]], axis=1)

    dh2 = _mm(dcq_pre, w_qd, tb=True, add=ds3, add_scale=alpha, name='q_down_dx')
    dh2 = _mm(dckv, w_kvc, tb=True, add=dh2, name='kv_down_c_dx')
    dh2 = _mm(dkr_pre, w_kvr, tb=True, add=dh2, name='kv_down_r_dx')

    ds2, ds2b, dg_f0, db_f0 = _ln_bwd(dh2, s2, ln('ln_ffn_g', 0), name='ln_ffn0_bwd')
    dh1, d_wd0, d_wu0, d_fcw0, d_fcb0 = ffn_bwd(ds2, ds2b, h1b, ffn0_saved, 0)
    ds1, ds1b, dg_m0, db_m0 = _ln_bwd(dh1, s1, ln('ln_mix_g', 0), name='ln_mix0_bwd')

    g['ssd_out_proj'] = _mm(yn, ds1b, ta=True, name='ssd_out_dw')[None]
    dyn = _mm(ds1b, w_out, tb=True, out_dtype=BF16, name='ssd_out_dx')
    dy1, dz, d_ng, d_dl = _gate_bwd(dyn, y_scan, xbc, z, d_lane, ssd_ng, d_inner, name='ssd_gate_bwd')
    g['ffn_up'] = jnp.stack([d_wu0, d_wu1])
    g['ffn_down'] = jnp.stack([d_wd0, d_wd1])
    (dxs, dB, dC, ddt_g, da_g), early = _scan_bwd(xbc, dt_pre, dt_bias, a_log, states, dy1, d_lane, S, d_inner,
                                                  name='ssd_scan_bwd',
                                                  exchange=reduce_early(g) if reduce_early else ())
    ddt_pre, d_bias, d_alog = _dt_bwd(ddt_g, da_g, dt_pre, dt_bias, a_log, name='ssd_dt_bwd')
    dxbc_pre, d_scw, d_scb = _ssd_conv_bwd(xbc_pre, xbc_conv, [dxs, dB, dC], ssd_cw, S, name='ssd_conv_bwd')
    g['ssd_in_proj'] = jnp.concatenate(
        [_mm(Xb, dz, ta=True, name='ssd_in_z_dw'), _mm(Xb, dxbc_pre, ta=True, name='ssd_in_xbc_dw'),
         _mm(Xb, ddt_pre, ta=True, name='ssd_in_dt_dw')[:, :n_heads_ssd]], axis=1)[None]
    g['ssd_conv_w'] = d_scw[None]
    g['ssd_conv_b'] = d_scb
    g['ssd_norm_g'] = d_ng
    g['ffn_conv_w'] = jnp.stack([d_fcw0, d_fcw1])
    dx = _mm(dz, w_z, tb=True, add=ds1, add_scale=alpha, name='ssd_in_z_dx')
    dx = _mm(dxbc_pre, w_xbc, tb=True, add=dx, name='ssd_in_xbc_dx', exchange=reduce_last(g) if reduce_last else ())
    dx, last = dx if reduce_last else (dx, None)
    dx = _mm(ddt_pre, w_dt, tb=True, add=dx, name='ssd_in_dt_dx')

    g['ssd_dt_bias'] = d_bias[:, :n_heads_ssd]
    g['ssd_A_log'] = d_alog[:, :n_heads_ssd]
    g['ssd_D'] = jnp.sum(d_dl.reshape(1, n_heads_ssd, SSD_HEAD_DIM), axis=2)
    g['ffn_conv_b'] = jnp.concatenate([d_fcb0, d_fcb1], axis=0)
    g['ln_mix_g'] = jnp.concatenate([dg_m0, dg_m1], axis=0)
    g['ln_mix_b'] = jnp.concatenate([db_m0, db_m1], axis=0)
    g['ln_ffn_g'] = jnp.concatenate([dg_f0, dg_f1], axis=0)
    g['ln_ffn_b'] = jnp.concatenate([db_f0, db_f1], axis=0)
    return sq, dx.reshape(Bn, S, D), g, early, last


ANY = pl.BlockSpec(memory_space=pl.ANY)


def _all_gather(xs, name):
    n = len(xs)

    def body(*refs):
        start, forward, finish = _gather_plan(refs[:n], refs[n:2 * n], *refs[2 * n:])
        start()
        forward()
        finish()

    return pl.pallas_call(
        body, name=name, out_shape=_gather_shapes(xs), in_specs=[ANY] * n, out_specs=[ANY] * n,
        scratch_shapes=_gather_sems(n),
    )(*xs)


def _gather_shapes(xs):
    return [jax.ShapeDtypeStruct((N_DEV,) + a.shape, a.dtype) for a in xs]


def _gather_sems(n):
    return [pltpu.SemaphoreType.DMA((7 * n,)), pltpu.SemaphoreType.DMA((7 * n,)), pltpu.SemaphoreType.DMA((n,))]


def _gather_plan(x_refs, out_refs, send_sems, recv_sems, local_sems):
    n = len(x_refs)
    x, y, c = lax.axis_index("x"), lax.axis_index("y"), lax.axis_index("c")
    me, sibling = (x, y, c), (x, y, 1 - c)
    chips = [(1 - x, y), (x, 1 - y), (1 - x, 1 - y)]

    def rows(i, px, py, pc):
        return out_refs[i].at[4 * px + 2 * py + pc]

    def copy(i, k, block, to, own=False):
        return pltpu.make_async_remote_copy(
            src_ref=x_refs[i] if own else rows(i, *block), dst_ref=rows(i, *block),
            send_sem=send_sems.at[7 * i + k], recv_sem=recv_sems.at[7 * i + k],
            device_id=to, device_id_type=MESH)

    def mine():
        return [pltpu.make_async_copy(x_refs[i], rows(i, *me), local_sems.at[i]) for i in range(n)]

    def first():
        out = []
        for i in range(n):
            out.append(copy(i, 0, me, sibling, own=True))
            out += [copy(i, 1 + j, me, (*chip, c), own=True) for j, chip in enumerate(chips)]
        return out

    def passed():
        return [copy(i, 4 + j, (*chip, c), sibling) for j, chip in enumerate(chips) for i in range(n)]

    def start():
        for cp in mine() + first():
            cp.start()

    def forward():
        for j, chip in enumerate(chips):
            for i in range(n):
                copy(i, 1 + j, (*chip, c), me).wait_recv()
                copy(i, 4 + j, (*chip, c), sibling).start()

    def finish():
        for i in range(n):
            copy(i, 0, sibling, me).wait_recv()
            for j, chip in enumerate(chips):
                copy(i, 4 + j, (*chip, 1 - c), me).wait_recv()
        for cp in first() + passed():
            cp.wait_send()
        for cp in mine():
            cp.wait()

    return start, forward, finish


def _exchange_sibling(gs, name):
    n = len(gs)

    def body(*refs):
        g_refs, r_refs = refs[:n], refs[n:2 * n]
        send_sems, recv_sems = refs[2 * n:]
        x, y, c = lax.axis_index("x"), lax.axis_index("y"), lax.axis_index("c")
        copies = [pltpu.make_async_remote_copy(
            src_ref=g_refs[i].at[2 * k + (1 - c)], dst_ref=r_refs[i].at[k], send_sem=send_sems.at[4 * i + k],
            recv_sem=recv_sems.at[4 * i + k], device_id=(x, y, 1 - c), device_id_type=MESH)
            for i in range(n) for k in range(4)]
        for cp in copies:
            cp.start()
        for cp in copies:
            cp.wait()

    return pl.pallas_call(
        body, name=name, out_shape=[jax.ShapeDtypeStruct((4,) + g.shape[1:], g.dtype) for g in gs],
        in_specs=[ANY] * n, out_specs=[ANY] * n,
        scratch_shapes=[pltpu.SemaphoreType.DMA((4 * n,)), pltpu.SemaphoreType.DMA((4 * n,))],
    )(*gs)


def _chips_shapes(parts):
    return [jax.ShapeDtypeStruct((3,) + p.shape[1:], p.dtype) for p in parts]


def _chips_sems(n):
    return [pltpu.SemaphoreType.DMA((3 * n,)), pltpu.SemaphoreType.DMA((3 * n,))]


def _chips_plan(p_refs, r_refs, send_sems, recv_sems):
    n = len(p_refs)
    x, y, c = lax.axis_index("x"), lax.axis_index("y"), lax.axis_index("c")
    chips = [(1 - x, y), (x, 1 - y), (1 - x, 1 - y)]

    def copies():
        return [pltpu.make_async_remote_copy(
            src_ref=p_refs[i].at[2 * cx + cy], dst_ref=r_refs[i].at[j], send_sem=send_sems.at[3 * i + j],
            recv_sem=recv_sems.at[3 * i + j], device_id=(cx, cy, c), device_id_type=MESH)
            for i in range(n) for j, (cx, cy) in enumerate(chips)]

    def start():
        for cp in copies():
            cp.start()

    def finish():
        for cp in copies():
            cp.wait()

    return start, finish


def _row_tile(rows, cols):
    want = max(8, (256 * 1024) // cols)
    for t in range(min(rows, want) // 8 * 8, 7, -8):
        if rows % t == 0:
            return t
    return rows


def _add_sibling(gfull, r1, core, name):
    _, rows, lanes = gfull.shape
    tr = _row_tile(rows, lanes)

    def body(c_ref, g_ref, r_ref, o_ref):
        o_ref[...] = g_ref[...] + r_ref[...]

    return pl.pallas_call(
        body, name=name,
        grid_spec=pltpu.PrefetchScalarGridSpec(
            num_scalar_prefetch=1, grid=(4, rows // tr),
            in_specs=[pl.BlockSpec((1, tr, lanes), lambda k, r, c_ref: (2 * k + c_ref[0], r, 0)),
                      pl.BlockSpec((1, tr, lanes), lambda k, r, c_ref: (k, r, 0))],
            out_specs=pl.BlockSpec((1, tr, lanes), lambda k, r, c_ref: (k, r, 0))),
        out_shape=jax.ShapeDtypeStruct((4, rows, lanes), F32), compiler_params=_cparams(),
    )(core, gfull, r1)


def _adam_math(wv, gv, mv, vv):
    m = ADAM_B1 * mv + (1.0 - ADAM_B1) * gv
    v = ADAM_B2 * vv + (1.0 - ADAM_B2) * (gv * gv)
    m_hat = m / (1.0 - ADAM_B1 ** ADAM_STEP)
    v_hat = v / (1.0 - ADAM_B2 ** ADAM_STEP)
    delta = -ADAM_LR * (m_hat / (jnp.sqrt(v_hat) + ADAM_EPS) + ADAM_WD * wv)
    return delta, m, v


def _adam_sharded(part, r2, chip, wts, m, v, name):
    rows, lanes = wts.shape
    tr = _row_tile(rows, lanes)

    def body(k_ref, p_ref, r_ref, w_ref, m_ref, v_ref, g_ref, d_ref, mo_ref, vo_ref):
        gv = p_ref[0] + r_ref[0] + r_ref[1] + r_ref[2]
        delta, mn, vn = _adam_math(w_ref[...], gv, m_ref[...], v_ref[...])
        g_ref[...] = gv
        d_ref[...] = delta
        mo_ref[...] = mn
        vo_ref[...] = vn

    flat = pl.BlockSpec((tr, lanes), lambda r, k_ref: (r, 0))
    sh = jax.ShapeDtypeStruct((rows, lanes), F32)
    return pl.pallas_call(
        body, name=name,
        grid_spec=pltpu.PrefetchScalarGridSpec(
            num_scalar_prefetch=1, grid=(rows // tr,),
            in_specs=[pl.BlockSpec((1, tr, lanes), lambda r, k_ref: (k_ref[0], r, 0)),
                      pl.BlockSpec((3, tr, lanes), lambda r, k_ref: (0, r, 0)), flat, flat, flat],
            out_specs=[flat, flat, flat, flat]),
        out_shape=[sh, sh, sh, sh], compiler_params=_cparams(),
    )(chip, part, r2, wts, m, v)


def _adam_replicated(gall, wts, m, v, name):
    rows, lanes = wts.shape

    def body(ga_ref, w_ref, m_ref, v_ref, g_ref, d_ref, mo_ref, vo_ref):
        gv = ga_ref[0]
        for k in range(1, N_DEV):
            gv = gv + ga_ref[k]
        delta, mn, vn = _adam_math(w_ref[...], gv, m_ref[...], v_ref[...])
        g_ref[...] = gv
        d_ref[...] = delta
        mo_ref[...] = mn
        vo_ref[...] = vn

    sh = jax.ShapeDtypeStruct((rows, lanes), F32)
    return pl.pallas_call(body, name=name, out_shape=[sh, sh, sh, sh], compiler_params=_cparams())(gall, wts, m, v)


def _pack(arrs, dtype, row_mult):
    flat = jnp.concatenate([a.reshape(-1).astype(dtype) for a in arrs])
    n = flat.shape[0]
    unit = LANES * row_mult
    total = -(-n // unit) * unit
    return jnp.pad(flat, (0, total - n)).reshape(total // LANES, LANES)


def _unpack(flat2d, shapes):
    flat = flat2d.reshape(-1)
    out, off = [], 0
    for shp in shapes:
        n = math.prod(shp)
        out.append(flat[off:off + n].reshape(shp))
        off += n
    return out


def _unpack_gathered(gathered, shapes, axes):
    flat = gathered.reshape(N_DEV, -1)
    out, off = [], 0
    for shp, ax in zip(shapes, axes):
        n = math.prod(shp)
        seg = flat[:, off:off + n].reshape((N_DEV,) + tuple(shp))
        out.append(jnp.concatenate([seg[i] for i in range(N_DEV)], axis=ax))
        off += n
    return out


def _pack_chunks(fulls, axes, row_mult):
    per_dev = []
    for full, ax in zip(fulls, axes):
        parts = jnp.stack(jnp.split(full, N_DEV, axis=ax))
        per_dev.append(parts.reshape(N_DEV, -1))
    flat = jnp.concatenate(per_dev, axis=1)
    n = flat.shape[1]
    unit = LANES * row_mult
    total = -(-n // unit) * unit
    return jnp.pad(flat, ((0, 0), (0, total - n))).reshape(N_DEV, total // LANES, LANES)


def kernel(x, positions, ssd_in_proj, ssd_conv_w, ssd_conv_b, ssd_dt_bias, ssd_A_log, ssd_D, ssd_norm_g, ssd_out_proj, kv_down_proj, kv_norm_g, kv_up_k, kv_up_v, q_down_proj, q_norm_g, q_up_proj, attn_out_proj, ffn_up, ffn_conv_w, ffn_conv_b, ffn_down, ln_mix_g, ln_mix_b, ln_ffn_g, ln_ffn_b, loss_target, m_ssd_in_proj, m_ssd_conv_w, m_ssd_conv_b, m_ssd_dt_bias, m_ssd_A_log, m_ssd_D, m_ssd_norm_g, m_ssd_out_proj, m_kv_down_proj, m_kv_norm_g, m_kv_up_k, m_kv_up_v, m_q_down_proj, m_q_norm_g, m_q_up_proj, m_attn_out_proj, m_ffn_up, m_ffn_conv_w, m_ffn_conv_b, m_ffn_down, m_ln_mix_g, m_ln_mix_b, m_ln_ffn_g, m_ln_ffn_b, v_ssd_in_proj, v_ssd_conv_w, v_ssd_conv_b, v_ssd_dt_bias, v_ssd_A_log, v_ssd_D, v_ssd_norm_g, v_ssd_out_proj, v_kv_down_proj, v_kv_norm_g, v_kv_up_k, v_kv_up_v, v_q_down_proj, v_q_norm_g, v_q_up_proj, v_attn_out_proj, v_ffn_up, v_ffn_conv_w, v_ffn_conv_b, v_ffn_down, v_ln_mix_g, v_ln_mix_b, v_ln_ffn_g, v_ln_ffn_b):
    given = dict(locals())
    wl = {n: given[n] for n in WEIGHTS}
    ml = {n: given['m_' + n] for n in WEIGHTS}
    vl = {n: given['v_' + n] for n in WEIGHTS}
    sharded_axis = dict(SHARDED)
    big = [n for n, _ in SHARDED if n not in SHARDED_F32]
    small = [n for n, _ in SHARDED if n in SHARDED_F32]

    def as2d(a):
        return a.reshape(-1, a.shape[-1])

    def to_full(gathered, n):
        shp, ax = wl[n].shape, sharded_axis[n]
        moved = jnp.moveaxis(gathered.reshape((N_DEV,) + shp), 0, ax)
        return moved.reshape(shp[:ax] + (N_DEV * shp[ax],) + shp[ax + 1:])

    def to_chunks(full_grad, n):
        shp, ax = wl[n].shape, sharded_axis[n]
        split = full_grad.reshape(shp[:ax] + (N_DEV, shp[ax]) + shp[ax + 1:])
        return jnp.moveaxis(split, ax, 0).reshape((N_DEV,) + as2d(wl[n]).shape)

    first = [n for n in big if n in GATHERED_FIRST]
    late = [n for n in big if n not in GATHERED_FIRST]
    full = {n: wl[n] for n in REPLICATED}
    gathered = _all_gather([as2d(wl[n]).astype(BF16) for n in first] + [_pack([wl[n] for n in small], F32, 8)],
                           name='gather_weights')
    for n, gat in zip(first, gathered[:-1]):
        full[n] = to_full(gat, n)
    full.update(zip(small, _unpack_gathered(gathered[-1], [wl[n].shape for n in small],
                                            [sharded_axis[n] for n in small])))

    cx, cy, cc = lax.axis_index("x"), lax.axis_index("y"), lax.axis_index("c")
    core = jnp.reshape(cc, (1,)).astype(jnp.int32)
    chip = jnp.reshape(2 * cx + cy, (1,)).astype(jnp.int32)
    early_names = [n for n in big if n not in REDUCED_LAST]
    last_names = [n for n in big if n in REDUCED_LAST]
    parts = {}

    def to_sibling(chunked, names, tag):
        r1 = _exchange_sibling(chunked, name=f'grads_to_sibling_{tag}')
        return [_add_sibling(gf, r, core, name=f'grads_add_sibling_{n}') for n, gf, r in zip(names, chunked, r1)]

    def reduce_early(grads):
        parts.update(zip(early_names, to_sibling([to_chunks(grads[n], n) for n in early_names], early_names, 'early')))
        return [parts[n] for n in early_names]

    def reduce_last(grads):
        chunked = [to_chunks(grads[n], n) for n in last_names]
        chunked.append(_pack_chunks([grads[n] for n in small], [sharded_axis[n] for n in small], 8))
        parts.update(zip(last_names + ['small'], to_sibling(chunked, last_names + ['small'], 'last')))
        return [parts[n] for n in last_names + ['small']]

    sq, grad_x, grads, r2_early, r2_last = _local_step(
        x, positions, full, loss_target,
        late=([as2d(wl[n]).astype(BF16) for n in late], lambda gats: {n: to_full(g, n) for n, g in zip(late, gats)}),
        reduce_early=reduce_early, reduce_last=reduce_last)
    loss = lax.psum(0.5 * jnp.sum(sq) / x.shape[-1], ("x", "y", "c"))

    r2 = dict(zip(early_names, r2_early))
    r2.update(zip(last_names + ['small'], r2_last))
    res = {}
    kinds = ('grad', 'delta', 'new_m', 'new_v')
    for n in big:
        outs = _adam_sharded(parts[n], r2[n], chip, as2d(wl[n]), as2d(ml[n]), as2d(vl[n]), name=f'adamw_{n}')
        for kind, a in zip(kinds, outs):
            res[kind, n] = a.reshape(wl[n].shape)
    outs = _adam_sharded(parts['small'], r2['small'], chip, _pack([wl[n] for n in small], F32, 8),
                         _pack([ml[n] for n in small], F32, 8), _pack([vl[n] for n in small], F32, 8),
                         name='adamw_small_sharded')
    for kind, packed in zip(kinds, outs):
        for n, a in zip(small, _unpack(packed, [wl[n].shape for n in small])):
            res[kind, n] = a

    rep = list(REPLICATED)
    rshapes = [wl[n].shape for n in rep]
    gall, = _all_gather([_pack([grads[n] for n in rep], F32, 8)], name='gather_replicated_grads')
    outs = _adam_replicated(gall, _pack([wl[n] for n in rep], F32, 8), _pack([ml[n] for n in rep], F32, 8),
                            _pack([vl[n] for n in rep], F32, 8), name='adamw_replicated')
    for kind, packed in zip(('grad', 'delta', 'new_m', 'new_v'), outs):
        for n, a in zip(rep, _unpack(packed, rshapes)):
            res[kind, n] = a

    return (loss, grad_x, *[res['grad', n] for n in WEIGHTS], *[res['delta', n] for n in WEIGHTS],
            *[res['new_m', n] for n in WEIGHTS], *[res['new_v', n] for n in WEIGHTS])
```

```python
import functools
import math

import jax
import jax.numpy as jnp
from jax import lax
from jax.experimental import pallas as pl
from jax.experimental.pallas import tpu as pltpu

F32 = jnp.float32
BF16 = jnp.bfloat16
MESH = pl.DeviceIdType.MESH

N_DEV = 8
LANES = 128
MM_TILES = (1024, 1408, 896, 512, 384, 256, 128)
MM_VMEM_BUDGET = 38 * 1024 * 1024
VMEM_LIMIT = 56 * 1024 * 1024
LN_EPS = 1e-5
RMS_EPS = 1e-6
SSD_HEAD_DIM = 64
SSD_N_GROUPS = 8
SSD_D_STATE = 128
SSD_CHUNK = 128
MLA_NOPE = 128
MLA_ROPE = 64
MLA_V = 128
ROPE_THETA = 10000.0
ADAM_LR = 0.001
ADAM_B1 = 0.9
ADAM_B2 = 0.999
ADAM_EPS = 1e-08
ADAM_WD = 0.01
ADAM_STEP = 10
NEG_BIG = -1e30

SHARDED = (('ssd_in_proj', 2), ('ssd_conv_w', 2), ('ssd_conv_b', 1), ('ssd_norm_g', 1), ('ssd_out_proj', 1),
           ('kv_down_proj', 0), ('kv_up_k', 1), ('kv_up_v', 1), ('q_down_proj', 1), ('q_up_proj', 2),
           ('attn_out_proj', 1), ('ffn_up', 2), ('ffn_conv_w', 2), ('ffn_down', 1))
SHARDED_F32 = ('ssd_conv_w', 'ssd_conv_b', 'ssd_norm_g', 'ffn_conv_w')
GATHERED_FIRST = ('ssd_in_proj',)
REDUCED_LAST = ('ssd_in_proj',)
REPLICATED = ('ssd_dt_bias', 'ssd_A_log', 'ssd_D', 'kv_norm_g', 'q_norm_g', 'ffn_conv_b',
              'ln_mix_g', 'ln_mix_b', 'ln_ffn_g', 'ln_ffn_b')
WEIGHTS = ('ssd_in_proj', 'ssd_conv_w', 'ssd_conv_b', 'ssd_dt_bias', 'ssd_A_log', 'ssd_D', 'ssd_norm_g',
           'ssd_out_proj', 'kv_down_proj', 'kv_norm_g', 'kv_up_k', 'kv_up_v', 'q_down_proj', 'q_norm_g',
           'q_up_proj', 'attn_out_proj', 'ffn_up', 'ffn_conv_w', 'ffn_conv_b', 'ffn_down', 'ln_mix_g',
           'ln_mix_b', 'ln_ffn_g', 'ln_ffn_b')


def _cparams():
    return pltpu.CompilerParams(vmem_limit_bytes=VMEM_LIMIT)


def _tile(n, cands):
    for c in cands:
        if n % c == 0:
            return c
    return n


def _sigmoid(x):
    return 1.0 / (1.0 + jnp.exp(-x))


def _iota(shape, axis):
    return lax.broadcasted_iota(jnp.int32, shape, axis)


def _mm(a, b, *, ta=False, tb=False, add=None, add_scale=1.0, out_dtype=F32, name, b_koff=0, into=None, into_col=0,
        exchange=()):
    if ta:
        K, M = a.shape
    else:
        M, K = a.shape
    if tb:
        N, Kb = b.shape
    else:
        Kb, N = b.shape
    assert b_koff + K <= Kb, (a.shape, b.shape, ta, tb, b_koff)
    tm = _tile(M, MM_TILES)
    tn = _tile(N, MM_TILES)
    tk = K if K <= MM_TILES[0] and b_koff == 0 and Kb == K else _tile(K, MM_TILES)
    nk = K // tk
    assert b_koff % tk == 0 and (into is None or (into.shape[0] == M and into.dtype == out_dtype))
    kb = b_koff // tk

    def vmem_estimate(tm_, tn_):
        ins = 2 * (tm_ * tk * a.dtype.itemsize + tk * tn_ * b.dtype.itemsize)
        outs = tm_ * tn_ * (2 * jnp.dtype(out_dtype).itemsize + 4 + (4 if nk > 1 else 0))
        return ins + outs + (2 * tm_ * tn_ * add.dtype.itemsize if add is not None else 0)

    while vmem_estimate(tm, tn) > MM_VMEM_BUDGET:
        can_m, can_n = tm % (2 * LANES) == 0, tn % (2 * LANES) == 0
        if can_m and (tm >= tn or not can_n):
            tm //= 2
        elif can_n:
            tn //= 2
        else:
            break

    assert into_col % tn == 0
    jb = into_col // tn

    nx = len(exchange)
    n_in = (add is not None) + (into is not None)
    grid = (M // tm, N // tn, nk)

    def body(a_ref, b_ref, *rest):
        add_ref = rest[0] if add is not None else None
        o_ref = rest[n_in + nx]
        acc = rest[n_in + 2 * nx + 1] if nk > 1 else None
        k = pl.program_id(2)
        if nx:
            start, finish_exchange = _chips_plan(rest[n_in:n_in + nx], rest[n_in + nx + 1:n_in + 2 * nx + 1],
                                                 *rest[len(rest) - 2:])
            t, total = _grid_step(grid)
            pl.when(t == 0)(start)

        if ta:
            av = a_ref[...].astype(BF16).T
        else:
            av = a_ref[...].astype(BF16)
        bv = b_ref[...].astype(BF16)
        dn = (((1,), (1 if tb else 0,)), ((), ()))
        part = lax.dot_general(av, bv, dn, preferred_element_type=F32)

        def finish(r):
            if add is not None:
                r = r + add_scale * add_ref[...].astype(F32)
            o_ref[...] = r.astype(out_dtype)

        if nk == 1:
            finish(part)
        else:
            @pl.when(k == 0)
            def _():
                acc[...] = part

            @pl.when(k > 0)
            def _():
                acc[...] += part

            @pl.when(k == nk - 1)
            def _():
                finish(acc[...])

        if nx:
            pl.when(t == total - 1)(finish_exchange)

    a_spec = (pl.BlockSpec((tk, tm), lambda i, j, k: (k, i)) if ta
              else pl.BlockSpec((tm, tk), lambda i, j, k: (i, k)))
    b_spec = (pl.BlockSpec((tn, tk), lambda i, j, k: (j, k + kb)) if tb
              else pl.BlockSpec((tk, tn), lambda i, j, k: (k + kb, j)))
    in_specs = [a_spec, b_spec]
    args = [a, b]
    if add is not None:
        in_specs.append(pl.BlockSpec((tm, tn), lambda i, j, k: (i, j)))
        args.append(add)
    aliases = {}
    if into is not None:
        aliases = {len(args): 0}
        in_specs.append(pl.BlockSpec(memory_space=pl.ANY))
        args.append(into)
    any_spec = pl.BlockSpec(memory_space=pl.ANY)
    outs = pl.pallas_call(
        body, name=name, grid=grid,
        in_specs=in_specs + [any_spec] * nx,
        out_specs=[pl.BlockSpec((tm, tn), lambda i, j, k: (i, j + jb))] + [any_spec] * nx,
        out_shape=[jax.ShapeDtypeStruct((M, N) if into is None else into.shape, out_dtype)] + _chips_shapes(exchange),
        scratch_shapes=([pltpu.VMEM((tm, tn), F32)] if nk > 1 else []) + (_chips_sems(nx) if nx else []),
        input_output_aliases=aliases, compiler_params=_cparams(),
    )(*args, *exchange)
    return (outs[0], list(outs[1:])) if nx else outs[0]


def _ln_fwd(h, mix, g, b, alpha, name):
    T, D = h.shape
    tm = _tile(T, (256, 128))

    def body(h_ref, m_ref, g_ref, b_ref, y_ref, s_ref, yb_ref):
        s = alpha * h_ref[...] + m_ref[...]
        mu = jnp.mean(s, axis=1, keepdims=True)
        xc = s - mu
        var = jnp.mean(xc * xc, axis=1, keepdims=True)
        y = xc * lax.rsqrt(var + LN_EPS) * g_ref[...] + b_ref[...]
        y_ref[...] = y
        s_ref[...] = s
        yb_ref[...] = y.astype(BF16)

    row = pl.BlockSpec((tm, D), lambda i: (i, 0))
    vec = pl.BlockSpec((1, D), lambda i: (0, 0))
    return pl.pallas_call(
        body, name=name, grid=(T // tm,), in_specs=[row, row, vec, vec], out_specs=[row, row, row],
        out_shape=[jax.ShapeDtypeStruct((T, D), F32)] * 2 + [jax.ShapeDtypeStruct((T, D), BF16)],
        compiler_params=_cparams(),
    )(h, mix, g, b)


def _ln_bwd(dy, s, g, name):
    T, D = s.shape
    tm = _tile(T, (256, 128))

    def body(dy_ref, s_ref, g_ref, ds_ref, dsb_ref, dg_ref, db_ref):
        @pl.when(pl.program_id(0) == 0)
        def _():
            dg_ref[...] = jnp.zeros_like(dg_ref)
            db_ref[...] = jnp.zeros_like(db_ref)

        sv = s_ref[...]
        dyv = dy_ref[...]
        mu = jnp.mean(sv, axis=1, keepdims=True)
        xc = sv - mu
        rstd = lax.rsqrt(jnp.mean(xc * xc, axis=1, keepdims=True) + LN_EPS)
        xhat = xc * rstd
        dxh = dyv * g_ref[...]
        m1 = jnp.mean(dxh, axis=1, keepdims=True)
        m2 = jnp.mean(dxh * xhat, axis=1, keepdims=True)
        ds = rstd * (dxh - m1 - xhat * m2)
        ds_ref[...] = ds
        dsb_ref[...] = ds.astype(BF16)
        dg_ref[...] += jnp.sum(dyv * xhat, axis=0, keepdims=True)
        db_ref[...] += jnp.sum(dyv, axis=0, keepdims=True)

    row = pl.BlockSpec((tm, D), lambda i: (i, 0))
    vec = pl.BlockSpec((1, D), lambda i: (0, 0))
    return pl.pallas_call(
        body, name=name, grid=(T // tm,), in_specs=[row, row, vec], out_specs=[row, row, vec, vec],
        out_shape=[jax.ShapeDtypeStruct((T, D), F32), jax.ShapeDtypeStruct((T, D), BF16),
                   jax.ShapeDtypeStruct((1, D), F32), jax.ShapeDtypeStruct((1, D), F32)],
        compiler_params=_cparams(),
    )(dy, s, g)


def _loss_head(y, target, name):
    T, D = y.shape
    tm = _tile(T, (256, 128))

    def body(y_ref, t_ref, sq_ref, dy_ref):
        @pl.when(pl.program_id(0) == 0)
        def _():
            sq_ref[...] = jnp.zeros_like(sq_ref)

        e = y_ref[...] - t_ref[...]
        sq_ref[...] += jnp.sum(e * e, axis=0, keepdims=True)
        dy_ref[...] = e * (1.0 / D)

    row = pl.BlockSpec((tm, D), lambda i: (i, 0))
    vec = pl.BlockSpec((1, D), lambda i: (0, 0))
    return pl.pallas_call(
        body, name=name, grid=(T // tm,), in_specs=[row, row], out_specs=[vec, row],
        out_shape=[jax.ShapeDtypeStruct((1, D), F32), jax.ShapeDtypeStruct((T, D), F32)],
        compiler_params=_cparams(),
    )(y, target)


def _shift_down(x, j):
    if j == 0:
        return x
    return jnp.where(_iota(x.shape, 0) >= j, pltpu.roll(x, j, 0), 0.0)


def _shift_up(x, j):
    if j == 0:
        return x
    n = x.shape[0]
    return jnp.where(_iota(x.shape, 0) < n - j, pltpu.roll(x, n - j, 0), 0.0)


def _conv_val(x, w_ref, b_ref):
    width = w_ref.shape[0]
    y = b_ref[...] + w_ref[width - 1:width, :] * x
    for j in range(1, width):
        y = y + w_ref[width - 1 - j:width - j, :] * _shift_down(x, j)
    return y


def _conv_bwd_val(x, dc, w_ref, dw_ref, db_ref):
    width = w_ref.shape[0]
    dx = w_ref[width - 1:width, :] * dc
    dw_ref[width - 1:width, :] += jnp.sum(dc * x, axis=0, keepdims=True)
    for j in range(1, width):
        sj = _shift_up(dc, j)
        dx = dx + w_ref[width - 1 - j:width - j, :] * sj
        dw_ref[width - 1 - j:width - j, :] += jnp.sum(sj * x, axis=0, keepdims=True)
    db_ref[...] += jnp.sum(dc, axis=0, keepdims=True)
    return dx


def _ffn_specs(H, S, width, cb):
    cb = _tile(H, (cb, LANES))
    nC = H // cb
    return dict(
        g=pl.BlockSpec((S, cb), lambda j, b: (b, j)), v=pl.BlockSpec((S, cb), lambda j, b: (b, j + nC)),
        wg=pl.BlockSpec((width, cb), lambda j, b: (0, j)), wv=pl.BlockSpec((width, cb), lambda j, b: (0, j + nC)),
        bg=pl.BlockSpec((1, cb), lambda j, b: (0, j)), bv=pl.BlockSpec((1, cb), lambda j, b: (0, j + nC))), nC


def _ffn_act_fwd(u, cw, cb, S, name):
    T, H2 = u.shape
    H = H2 // 2
    sp, nC = _ffn_specs(H, S, cw.shape[0], 2 * LANES)

    def body(ug_ref, uv_ref, wg_ref, wv_ref, bg_ref, bv_ref, a_ref):
        g = _conv_val(ug_ref[...], wg_ref, bg_ref)
        v = _conv_val(uv_ref[...], wv_ref, bv_ref)
        a_ref[...] = (g * _sigmoid(g) * v).astype(BF16)

    return pl.pallas_call(
        body, name=name, grid=(nC, T // S),
        in_specs=[sp['g'], sp['v'], sp['wg'], sp['wv'], sp['bg'], sp['bv']], out_specs=sp['g'],
        out_shape=jax.ShapeDtypeStruct((T, H), BF16), compiler_params=_cparams(),
    )(u, u, cw, cw, cb, cb)


def _ffn_act_bwd(u, da, cw, cb, S, name):
    T, H2 = u.shape
    H = H2 // 2
    width = cw.shape[0]
    sp, nC = _ffn_specs(H, S, width, LANES)

    def body(ug_ref, uv_ref, da_ref, wg_ref, wv_ref, bg_ref, bv_ref,
             dug_ref, duv_ref, dwg_ref, dwv_ref, dbg_ref, dbv_ref):
        @pl.when(pl.program_id(1) == 0)
        def _():
            for r in (dwg_ref, dwv_ref, dbg_ref, dbv_ref):
                r[...] = jnp.zeros_like(r)

        xg = [_shift_down(ug_ref[...], j) for j in range(width)]
        xv = [_shift_down(uv_ref[...], j) for j in range(width)]

        def conv(xs, w_ref, b_ref):
            y = b_ref[...]
            for j in range(width):
                y = y + w_ref[width - 1 - j:width - j, :] * xs[j]
            return y

        def conv_bwd(xs, dc, w_ref, dw_ref, db_ref):
            dx = w_ref[width - 1:width, :] * dc
            for j in range(1, width):
                dx = dx + w_ref[width - 1 - j:width - j, :] * _shift_up(dc, j)
            for j in range(width):
                dw_ref[width - 1 - j:width - j, :] += jnp.sum(dc * xs[j], axis=0, keepdims=True)
            db_ref[...] += jnp.sum(dc, axis=0, keepdims=True)
            return dx

        g = conv(xg, wg_ref, bg_ref)
        v = conv(xv, wv_ref, bv_ref)
        dav = da_ref[...].astype(F32)
        sg = _sigmoid(g)
        dg = dav * v * sg * (1.0 + g * (1.0 - sg))
        dv = dav * g * sg
        dug_ref[...] = conv_bwd(xg, dg, wg_ref, dwg_ref, dbg_ref).astype(BF16)
        duv_ref[...] = conv_bwd(xv, dv, wv_ref, dwv_ref, dbv_ref).astype(BF16)

    act = jax.ShapeDtypeStruct((T, H), BF16)
    wsh = jax.ShapeDtypeStruct((width, H), F32)
    bsh = jax.ShapeDtypeStruct((1, H), F32)
    return pl.pallas_call(
        body, name=name, grid=(nC, T // S),
        in_specs=[sp['g'], sp['v'], sp['g'], sp['wg'], sp['wv'], sp['bg'], sp['bv']],
        out_specs=[sp['g'], sp['g'], sp['wg'], sp['wg'], sp['bg'], sp['bg']],
        out_shape=[act, act, wsh, wsh, bsh, bsh], compiler_params=_cparams(),
    )(u, u, da, cw, cw, cb, cb)


def _ssd_conv_fwd(x, cw, cb, S, name):
    T, C = x.shape
    Cb = _tile(C, (256, 128))
    width = cw.shape[0]

    def body(x_ref, w_ref, b_ref, y_ref, c_ref):
        c = _conv_val(x_ref[...], w_ref, b_ref)
        c_ref[...] = c
        y_ref[...] = c * _sigmoid(c)

    blk = pl.BlockSpec((S, Cb), lambda j, b: (b, j))
    wblk = pl.BlockSpec((width, Cb), lambda j, b: (0, j))
    bblk = pl.BlockSpec((1, Cb), lambda j, b: (0, j))
    return pl.pallas_call(
        body, name=name, grid=(C // Cb, T // S), in_specs=[blk, wblk, bblk], out_specs=[blk, blk],
        out_shape=[jax.ShapeDtypeStruct((T, C), F32)] * 2, compiler_params=_cparams(),
    )(x, cw, cb)


def _ssd_conv_bwd(x, c, dys, cw, S, name):
    T, C = x.shape
    Cb = _tile(C, (256, 128))
    width = cw.shape[0]
    assert all(d.shape[1] % Cb == 0 for d in dys) and sum(d.shape[1] for d in dys) == C

    def part(dy, j0, dx_prev, k):
        n = dy.shape[1] // Cb

        def body(x_ref, c_ref, dy_ref, w_ref, prev_ref, dx_ref, dw_ref, db_ref):
            @pl.when(pl.program_id(1) == 0)
            def _():
                dw_ref[...] = jnp.zeros_like(dw_ref)
                db_ref[...] = jnp.zeros_like(db_ref)

            cval = c_ref[...]
            sc = _sigmoid(cval)
            dc = dy_ref[...] * sc * (1.0 + cval * (1.0 - sc))
            dx_ref[...] = _conv_bwd_val(x_ref[...], dc, w_ref, dw_ref, db_ref).astype(BF16)

        wide = pl.BlockSpec((S, Cb), lambda j, b: (b, j + j0))
        return pl.pallas_call(
            body, name=f'{name}_{k}', grid=(n, T // S),
            in_specs=[wide, wide, pl.BlockSpec((S, Cb), lambda j, b: (b, j)),
                      pl.BlockSpec((width, Cb), lambda j, b: (0, j + j0)), pl.BlockSpec(memory_space=pl.ANY)],
            out_specs=[wide, pl.BlockSpec((width, Cb), lambda j, b: (0, j)), pl.BlockSpec((1, Cb), lambda j, b: (0, j))],
            out_shape=[jax.ShapeDtypeStruct((T, C), BF16), jax.ShapeDtypeStruct((width, n * Cb), F32),
                       jax.ShapeDtypeStruct((1, n * Cb), F32)],
            input_output_aliases={4: 0}, compiler_params=_cparams(),
        )(x, c, dy, cw, dx_prev)

    dx = lax.empty((T, C), BF16)
    dws, dbs, j0 = [], [], 0
    for k, dy in enumerate(dys):
        dx, dw, db = part(dy, j0, dx, k)
        dws.append(dw)
        dbs.append(db)
        j0 += dy.shape[1] // Cb
    return dx, jnp.concatenate(dws, axis=1), jnp.concatenate(dbs, axis=1)


def _softplus(x):
    e = jnp.exp(-jnp.abs(x))
    return jnp.maximum(x, 0.0) + jnp.where(e < 1e-4, e * (1.0 - 0.5 * e), jnp.log(1.0 + e))


def _cumsum_rows(x):
    n = x.shape[0]
    row = _iota(x.shape, 0)
    k = 1
    while k < n:
        x = x + jnp.where(row >= k, pltpu.roll(x, k, 0), 0.0)
        k *= 2
    return x


def _rev_cumsum_rows(x):
    n = x.shape[0]
    row = _iota(x.shape, 0)
    k = 1
    while k < n:
        x = x + jnp.where(row < n - k, pltpu.roll(x, n - k, 0), 0.0)
        k *= 2
    return x


def _col(x, lane_ids, h):
    return jnp.sum(jnp.where(lane_ids == h, x, 0.0), axis=1, keepdims=True)


def _bdot(a, b, dims):
    return lax.dot_general(a.astype(BF16), b.astype(BF16), (dims, ((), ())), preferred_element_type=F32)


NN = ((1,), (0,))
NT = ((1,), (1,))


def _ssd_dt_prep(dt_pre, dt_bias, a_log, n_heads, name):
    T = dt_pre.shape[0]
    L = SSD_CHUNK

    def body(dtp_ref, bias_ref, alog_ref, dtb_ref, cumb_ref):
        dt = _softplus(dtp_ref[...] + bias_ref[...])
        cum = _cumsum_rows(dt * (-jnp.exp(alog_ref[...])))
        lane = _iota((L, LANES), 1)
        for h in range(n_heads):
            hcols = pl.ds(h * LANES, LANES)
            dtb_ref[:, hcols] = jnp.broadcast_to(_col(dt, lane, h), (L, LANES))
            cumb_ref[:, hcols] = jnp.broadcast_to(_col(cum, lane, h), (L, LANES))

    row = pl.BlockSpec((L, LANES), lambda i: (i, 0))
    vec = pl.BlockSpec((1, LANES), lambda i: (0, 0))
    wide = pl.BlockSpec((L, n_heads * LANES), lambda i: (i, 0))
    return pl.pallas_call(
        body, name=name, grid=(T // L,), in_specs=[row, vec, vec], out_specs=[wide, wide],
        out_shape=[jax.ShapeDtypeStruct((T, n_heads * LANES), F32)] * 2, compiler_params=_cparams(),
    )(dt_pre, dt_bias, a_log)


SCAN_GROUPS_FWD = 4
SCAN_GROUPS_BWD = 2


def _scan_specs(nc, d_inner, hpg, GP):
    L = SSD_CHUNK
    G = SSD_N_GROUPS
    gw = GP * hpg * SSD_HEAD_DIM
    bw = GP * LANES
    assert G % GP == 0 and d_inner % bw == 0
    nb = d_inner // bw
    return dict(
        xs=pl.BlockSpec((L, gw), lambda b, q, c: (b * nc + c, q)),
        B=pl.BlockSpec((L, bw), lambda b, q, c: (b * nc + c, nb + q)),
        C=pl.BlockSpec((L, bw), lambda b, q, c: (b * nc + c, nb + G // GP + q)),
        heads=pl.BlockSpec((L, GP * hpg * LANES), lambda b, q, c: (b * nc + c, q)),
        gvec=pl.BlockSpec((1, gw), lambda b, q, c: (0, q)),
        grp=pl.BlockSpec((L, bw), lambda b, q, c: (b * nc + c, q)),
        st=pl.BlockSpec((1, GP * hpg // 2, SSD_D_STATE, LANES), lambda b, q, c: (b * nc + c, q, 0, 0)),
    )


def _grid_step(dims):
    t, total = 0, 1
    for ax, d in enumerate(dims):
        t = t * d + pl.program_id(ax)
        total *= d
    return t, total


def _scan_fwd(xbc, dtb, cumb, S, d_inner, name, gather=()):
    T = xbc.shape[0]
    Bn = T // S
    L = SSD_CHUNK
    G = SSD_N_GROUPS
    nc = S // L
    hpg = d_inner // SSD_HEAD_DIM // G
    pairs = hpg // 2
    GP = SCAN_GROUPS_FWD
    sp = _scan_specs(nc, d_inner, hpg, GP)
    ng = len(gather)

    def body(*refs):
        xs_ref, b_ref, c_ref, dtb_ref, cumb_ref = refs[:5]
        y_ref, st_ref = refs[5 + ng:7 + ng]
        state = refs[7 + 2 * ng]

        if ng:
            start, forward, finish = _gather_plan(refs[5:5 + ng], refs[7 + ng:7 + 2 * ng], *refs[8 + 2 * ng:])
            t, total = _grid_step((Bn, G // GP, nc))
            pl.when(t == 0)(start)
            pl.when(t == total // 2)(forward)

        @pl.when(pl.program_id(2) == 0)
        def _():
            state[...] = jnp.zeros_like(state)

        tril = _iota((L, L), 1) <= _iota((L, L), 0)
        lane = _iota((L, LANES), 1)
        lane1 = _iota((1, LANES), 1)
        last = _iota((L, 1), 0) == L - 1
        for gi, p in [(gi, p) for gi in range(GP) for p in range(pairs)]:
            if p == 0:
                Bm = b_ref[:, gi * LANES:(gi + 1) * LANES]
                Cm = c_ref[:, gi * LANES:(gi + 1) * LANES]
                Gm = _bdot(Cm, Bm, NT)
                BmT = Bm.T
            sp_ = gi * pairs + p
            cols = pl.ds(sp_ * LANES, LANES)
            S_in = state[sp_]
            st_ref[0, sp_] = S_in
            x_pair = xs_ref[:, cols]
            y_pair = jnp.zeros((L, LANES), F32)
            z_pair = jnp.zeros((L, LANES), F32)
            e_pair = jnp.zeros((L, LANES), F32)
            el_pair = jnp.zeros((1, LANES), F32)
            for k in range(2):
                hcols = pl.ds((gi * hpg + 2 * p + k) * LANES, LANES)
                cum_col = cumb_ref[:, hcols]
                dt_col = dtb_ref[:, hcols]
                decay = jnp.exp(jnp.where(tril, cum_col - cum_col.T, NEG_BIG))
                mk = (lane >= SSD_HEAD_DIM) if k else (lane < SSD_HEAD_DIM)
                xd = jnp.where(mk, x_pair * dt_col, 0.0)
                y_pair = y_pair + _bdot(Gm * decay, xd, NN)
                cum_l = jnp.sum(jnp.where(last, cum_col, 0.0), axis=0, keepdims=True)
                e_pair = jnp.where(mk, jnp.exp(cum_col), e_pair)
                z_pair = z_pair + xd * jnp.exp(cum_l - cum_col)
                el_pair = jnp.where((lane1 >= SSD_HEAD_DIM) if k else (lane1 < SSD_HEAD_DIM), jnp.exp(cum_l), el_pair)
            y_pair = y_pair + e_pair * _bdot(Cm, S_in, NN)
            state[sp_] = S_in * el_pair + _bdot(BmT, z_pair, NN)
            y_ref[:, cols] = y_pair

        if ng:
            pl.when(t == total - 1)(finish)

    outs = pl.pallas_call(
        body, name=name, grid=(Bn, G // GP, nc),
        in_specs=[sp['xs'], sp['B'], sp['C'], sp['heads'], sp['heads']] + [ANY] * ng,
        out_specs=[sp['xs'], sp['st']] + [ANY] * ng,
        out_shape=[jax.ShapeDtypeStruct((T, d_inner), F32),
                   jax.ShapeDtypeStruct((Bn * nc, G * pairs, SSD_D_STATE, LANES), F32)] + _gather_shapes(gather),
        scratch_shapes=[pltpu.VMEM((GP * pairs, SSD_D_STATE, LANES), F32)] + (_gather_sems(ng) if ng else []),
        compiler_params=_cparams(),
    )(xbc, xbc, xbc, dtb, cumb, *gather)
    return outs[0], outs[1], list(outs[2:])


def _scan_bwd(xbc, dtb, cumb, states, dy, d_lane, S, d_inner, name, exchange=()):
    T = xbc.shape[0]
    Bn = T // S
    L = SSD_CHUNK
    G = SSD_N_GROUPS
    nc = S // L
    hpg = d_inner // SSD_HEAD_DIM // G
    pairs = hpg // 2
    GP = SCAN_GROUPS_BWD
    sp = _scan_specs(nc, d_inner, hpg, GP)
    nx = len(exchange)

    def rev(spec):
        im = spec.index_map
        return pl.BlockSpec(spec.block_shape, lambda b, g, c: im(b, g, nc - 1 - c))

    def body(*refs):
        xs_ref, b_ref, c_ref, dtb_ref, cumb_ref, st_ref, dy_ref, dl_ref = refs[:8]
        dxs_ref, db_ref, dc_ref, ddt_ref, da_ref = refs[8 + nx:13 + nx]
        dstate = refs[13 + 2 * nx]
        g = pl.program_id(1)

        if nx:
            start, finish = _chips_plan(refs[8:8 + nx], refs[13 + nx:13 + 2 * nx], *refs[14 + 2 * nx:])
            t, total = _grid_step((Bn, G // GP, nc))
            pl.when(t == 0)(start)

        @pl.when(pl.program_id(2) == 0)
        def _():
            dstate[...] = jnp.zeros_like(dstate)

        tril = _iota((L, L), 1) <= _iota((L, L), 0)
        lane = _iota((L, LANES), 1)
        lane1 = _iota((1, LANES), 1)
        last = _iota((L, 1), 0) == L - 1
        for gi, p in [(gi, p) for gi in range(GP) for p in range(pairs)]:
            gcols = pl.ds(gi * LANES, LANES)
            if p == 0:
                Bm = b_ref[:, gcols]
                Cm = c_ref[:, gcols]
                Gm = _bdot(Cm, Bm, NT)
                CmT = Cm.T
                dG = jnp.zeros((L, L), F32)
                dB = jnp.zeros((L, LANES), F32)
                dC = jnp.zeros((L, LANES), F32)
                dcum = jnp.zeros((L, LANES), F32)
                ddt = jnp.zeros((L, LANES), F32)
            sp_ = gi * pairs + p
            cols = pl.ds(sp_ * LANES, LANES)
            S_in = st_ref[0, sp_]
            dS = dstate[sp_]
            x_pair = xs_ref[:, cols]
            dy_pair = dy_ref[:, cols]
            Q = _bdot(Cm, S_in, NN)
            dZ = _bdot(Bm, dS, NN)
            prod = dS * S_in
            e_pair = jnp.zeros((L, LANES), F32)
            z_pair = jnp.zeros((L, LANES), F32)
            el_pair = jnp.zeros((1, LANES), F32)
            dx_pair = dl_ref[:, cols] * dy_pair
            for k in range(2):
                h = (g * GP + gi) * hpg + 2 * p + k
                hcols = pl.ds((gi * hpg + 2 * p + k) * LANES, LANES)
                cum_col = cumb_ref[:, hcols]
                dt_col = dtb_ref[:, hcols]
                decay = jnp.exp(jnp.where(tril, cum_col - cum_col.T, NEG_BIG))
                mk = (lane >= SSD_HEAD_DIM) if k else (lane < SSD_HEAD_DIM)
                mk1 = (lane1 >= SSD_HEAD_DIM) if k else (lane1 < SSD_HEAD_DIM)
                xd = jnp.where(mk, x_pair * dt_col, 0.0)
                dy_k = jnp.where(mk, dy_pair, 0.0)
                Wm = Gm * decay
                dWm = jnp.where(tril, _bdot(dy_k, xd, NT), 0.0)
                dxd = _bdot(Wm.T, dy_k, NN)
                dseg = dWm * Wm
                dG = dG + dWm * decay
                dcum_col = (jnp.sum(dseg, axis=1, keepdims=True)
                            - jnp.sum(dseg.T, axis=1, keepdims=True))
                cum_l = jnp.sum(jnp.where(last, cum_col, 0.0), axis=0, keepdims=True)
                e_col = jnp.exp(cum_col)
                w_col = jnp.exp(cum_l - cum_col)
                el = jnp.exp(cum_l)
                dcum_col = dcum_col + jnp.sum(dy_k * Q, axis=1, keepdims=True) * e_col
                de_e = jnp.sum(dZ * xd, axis=1, keepdims=True) * w_col
                dcum_col = dcum_col - de_e
                dcum_l = (jnp.sum(de_e, axis=0, keepdims=True)
                          + el * jnp.sum(jnp.sum(jnp.where(mk, prod, 0.0), axis=1, keepdims=True),
                                         axis=0, keepdims=True))
                dcum_col = dcum_col + jnp.where(last, dcum_l, 0.0)
                dxd = dxd + jnp.where(mk, dZ * w_col, 0.0)
                dx_pair = dx_pair + dxd * dt_col
                ddt = jnp.where(lane == h, jnp.sum(dxd * x_pair, axis=1, keepdims=True), ddt)
                dcum = jnp.where(lane == h, dcum_col, dcum)
                e_pair = jnp.where(mk, e_col, e_pair)
                z_pair = z_pair + xd * w_col
                el_pair = jnp.where(mk1, el, el_pair)
            dQ = e_pair * dy_pair
            dC = dC + _bdot(dQ, S_in, NT)
            dB = dB + _bdot(z_pair, dS, NT)
            dstate[sp_] = dS * el_pair + _bdot(CmT, dQ, NN)
            dxs_ref[:, cols] = dx_pair
            if p == pairs - 1:
                dc_ref[:, gcols] = dC + _bdot(dG, Bm, NN)
                db_ref[:, gcols] = dB + _bdot(dG.T, Cm, NN)
                ddt_ref[:, gcols] = ddt
                da_ref[:, gcols] = _rev_cumsum_rows(dcum)

        if nx:
            pl.when(t == total - 1)(finish)

    grp = jax.ShapeDtypeStruct((T, G * LANES), F32)
    outs = pl.pallas_call(
        body, name=name, grid=(Bn, G // GP, nc),
        in_specs=[rev(sp['xs']), rev(sp['B']), rev(sp['C']), rev(sp['heads']), rev(sp['heads']),
                  rev(sp['st']), rev(sp['xs']), sp['gvec']] + [ANY] * nx,
        out_specs=[rev(sp['xs']), rev(sp['grp']), rev(sp['grp']), rev(sp['grp']), rev(sp['grp'])] + [ANY] * nx,
        out_shape=[jax.ShapeDtypeStruct((T, d_inner), F32), grp, grp, grp, grp] + _chips_shapes(exchange),
        scratch_shapes=[pltpu.VMEM((GP * pairs, SSD_D_STATE, LANES), F32)] + (_chips_sems(nx) if nx else []),
        compiler_params=_cparams(),
    )(xbc, xbc, xbc, dtb, cumb, states, dy, d_lane, *exchange)
    return outs[:5], list(outs[5:])


def _dt_bwd(ddt_g, da_g, dt_pre, dt_bias, a_log, name):
    T = dt_pre.shape[0]
    G = SSD_N_GROUPS
    tm = _tile(T, (512, 256, 128))

    def body(ddt_ref, da_ref, dtp_ref, bias_ref, alog_ref, dx_ref, dbias_ref, dal_ref):
        @pl.when(pl.program_id(0) == 0)
        def _():
            dbias_ref[...] = jnp.zeros_like(dbias_ref)
            dal_ref[...] = jnp.zeros_like(dal_ref)

        ddt = ddt_ref[:, 0:LANES]
        da = da_ref[:, 0:LANES]
        for gi in range(1, G):
            ddt = ddt + ddt_ref[:, gi * LANES:(gi + 1) * LANES]
            da = da + da_ref[:, gi * LANES:(gi + 1) * LANES]
        xv = dtp_ref[...] + bias_ref[...]
        A = -jnp.exp(alog_ref[...])
        dx = (ddt + da * A) * _sigmoid(xv)
        dx_ref[...] = dx.astype(BF16)
        dbias_ref[...] += jnp.sum(dx, axis=0, keepdims=True)
        dal_ref[...] += jnp.sum(da * _softplus(xv), axis=0, keepdims=True) * A

    wide = pl.BlockSpec((tm, G * LANES), lambda i: (i, 0))
    row = pl.BlockSpec((tm, LANES), lambda i: (i, 0))
    vec = pl.BlockSpec((1, LANES), lambda i: (0, 0))
    vsh = jax.ShapeDtypeStruct((1, LANES), F32)
    return pl.pallas_call(
        body, name=name, grid=(T // tm,), in_specs=[wide, wide, row, vec, vec], out_specs=[row, vec, vec],
        out_shape=[jax.ShapeDtypeStruct((T, LANES), BF16), vsh, vsh], compiler_params=_cparams(),
    )(ddt_g, da_g, dt_pre, dt_bias, a_log)


def _gate_fwd(y, xbc, z, d_lane, ng, d_inner, name):
    T = y.shape[0]
    G = SSD_N_GROUPS
    gw = d_inner // G
    tm = _tile(T, (512, 256, 128))

    def body(y_ref, x_ref, z_ref, dl_ref, ng_ref, o_ref):
        zv = z_ref[...]
        y2 = (y_ref[...] + dl_ref[...] * x_ref[...]) * (zv * _sigmoid(zv))
        r = lax.rsqrt(jnp.mean(y2 * y2, axis=1, keepdims=True) + LN_EPS)
        o_ref[...] = (y2 * r * ng_ref[...]).astype(BF16)

    blk = pl.BlockSpec((tm, gw), lambda g, i: (i, g))
    vec = pl.BlockSpec((1, gw), lambda g, i: (0, g))
    return pl.pallas_call(
        body, name=name, grid=(G, T // tm), in_specs=[blk, blk, blk, vec, vec], out_specs=blk,
        out_shape=jax.ShapeDtypeStruct((T, d_inner), BF16), compiler_params=_cparams(),
    )(y, xbc, z, d_lane, ng)


def _gate_bwd(dyn, y, xbc, z, d_lane, ng, d_inner, name):
    T = y.shape[0]
    G = SSD_N_GROUPS
    gw = d_inner // G
    tm = _tile(T, (512, 256, 128))

    def body(dyn_ref, y_ref, x_ref, z_ref, dl_ref, ng_ref, dy_ref, dz_ref, dng_ref, ddl_ref):
        @pl.when(pl.program_id(1) == 0)
        def _():
            dng_ref[...] = jnp.zeros_like(dng_ref)
            ddl_ref[...] = jnp.zeros_like(ddl_ref)

        zv = z_ref[...]
        xv = x_ref[...]
        sz = _sigmoid(zv)
        y1 = y_ref[...] + dl_ref[...] * xv
        y2 = y1 * (zv * sz)
        r = lax.rsqrt(jnp.mean(y2 * y2, axis=1, keepdims=True) + LN_EPS)
        dn = dyn_ref[...].astype(F32)
        dng_ref[...] += jnp.sum(dn * y2 * r, axis=0, keepdims=True)
        do = dn * ng_ref[...]
        dy2 = r * do - y2 * (r * r * r) * jnp.mean(do * y2, axis=1, keepdims=True)
        dz_ref[...] = (dy2 * y1 * sz * (1.0 + zv * (1.0 - sz))).astype(BF16)
        dy1 = dy2 * (zv * sz)
        dy_ref[...] = dy1
        ddl_ref[...] += jnp.sum(dy1 * xv, axis=0, keepdims=True)

    blk = pl.BlockSpec((tm, gw), lambda g, i: (i, g))
    vec = pl.BlockSpec((1, gw), lambda g, i: (0, g))
    vsh = jax.ShapeDtypeStruct((1, d_inner), F32)
    return pl.pallas_call(
        body, name=name, grid=(G, T // tm), in_specs=[blk, blk, blk, blk, vec, vec],
        out_specs=[blk, blk, vec, vec],
        out_shape=[jax.ShapeDtypeStruct((T, d_inner), F32), jax.ShapeDtypeStruct((T, d_inner), BF16), vsh, vsh],
        compiler_params=_cparams(),
    )(dyn, y, xbc, z, d_lane, ng)


def _rms_fwd(x, g, name):
    T, R = x.shape
    tm = _tile(T, (512, 256, 128))

    def body(x_ref, g_ref, y_ref):
        xv = x_ref[...]
        y = xv * lax.rsqrt(jnp.mean(xv * xv, axis=1, keepdims=True) + RMS_EPS) * g_ref[...]
        y_ref[...] = y.astype(BF16)

    row = pl.BlockSpec((tm, R), lambda i: (i, 0))
    vec = pl.BlockSpec((1, R), lambda i: (0, 0))
    return pl.pallas_call(
        body, name=name, grid=(T // tm,), in_specs=[row, vec], out_specs=row,
        out_shape=jax.ShapeDtypeStruct((T, R), BF16), compiler_params=_cparams(),
    )(x, g)


def _rms_bwd(dy, x, g, name):
    T, R = x.shape
    tm = _tile(T, (512, 256, 128))

    def body(dy_ref, x_ref, g_ref, dx_ref, dg_ref):
        @pl.when(pl.program_id(0) == 0)
        def _():
            dg_ref[...] = jnp.zeros_like(dg_ref)

        xv = x_ref[...]
        dyv = dy_ref[...]
        r = lax.rsqrt(jnp.mean(xv * xv, axis=1, keepdims=True) + RMS_EPS)
        dg_ref[...] += jnp.sum(dyv * xv * r, axis=0, keepdims=True)
        do = dyv * g_ref[...]
        dx = r * do - xv * (r * r * r) * jnp.mean(do * xv, axis=1, keepdims=True)
        dx_ref[...] = dx.astype(BF16)

    row = pl.BlockSpec((tm, R), lambda i: (i, 0))
    vec = pl.BlockSpec((1, R), lambda i: (0, 0))
    return pl.pallas_call(
        body, name=name, grid=(T // tm,), in_specs=[row, row, vec], out_specs=[row, vec],
        out_shape=[jax.ShapeDtypeStruct((T, R), BF16), jax.ShapeDtypeStruct((1, R), F32)],
        compiler_params=_cparams(),
    )(dy, x, g)


def _swap_halves(x):
    half = MLA_ROPE // 2
    return jnp.where(_iota(x.shape, 1) < half, pltpu.roll(x, LANES - half, 1), pltpu.roll(x, half, 1))


def _rope(x, cc, ss, *, transpose, sum_heads=False, name):
    T, W = x.shape
    nh = W // LANES
    tm = _tile(T, (512, 256, 128))
    n_out = 1 if sum_heads else nh

    def body(x_ref, cc_ref, ss_ref, o_ref):
        ccv = cc_ref[...]
        ssv = ss_ref[...]
        if sum_heads:
            xv = x_ref[:, 0:LANES]
            for hh in range(1, nh):
                xv = xv + x_ref[:, hh * LANES:(hh + 1) * LANES]
            heads = [xv]
        else:
            heads = [x_ref[:, hh * LANES:(hh + 1) * LANES] for hh in range(nh)]
        for hh, xv in enumerate(heads):
            if transpose:
                r = xv * ccv + _swap_halves(xv * ssv)
            else:
                r = xv * ccv + _swap_halves(xv) * ssv
            r = jnp.where(_iota(r.shape, 1) < MLA_ROPE, r, 0.0)
            o_ref[:, hh * LANES:(hh + 1) * LANES] = r.astype(BF16)

    return pl.pallas_call(
        body, name=name, grid=(T // tm,),
        in_specs=[pl.BlockSpec((tm, W), lambda i: (i, 0)), pl.BlockSpec((tm, LANES), lambda i: (i, 0)),
                  pl.BlockSpec((tm, LANES), lambda i: (i, 0))],
        out_specs=pl.BlockSpec((tm, n_out * LANES), lambda i: (i, 0)),
        out_shape=jax.ShapeDtypeStruct((T, n_out * LANES), BF16), compiler_params=_cparams(),
    )(x, cc, ss)


ATT_BLOCK = 512
ATT_SUB = 256
LOG2E = 1.4426950408889634


def _att_geometry(S):
    tb = _tile(S, (ATT_BLOCK, 256))
    return tb, S // tb, tb // ATT_SUB


def _heads_per_step(nh):
    return 2 if nh % 2 == 0 else 1


def _att_scores(a, b, diag, kv_major, off):
    s = _bdot(a, b, NT)
    if diag:
        q_ax, k_ax = (1, 0) if kv_major else (0, 1)
        s = jnp.where(_iota(s.shape, k_ax) <= _iota(s.shape, q_ax) + off, s, NEG_BIG)
    return s


def _attention_fwd(qn, qr, kn, kr, v, S, scale, name):
    T, W = qn.shape
    nh = W // LANES
    Bn = T // S
    tb, n, C = _att_geometry(S)
    pairs = [(i, j) for i in range(n) for j in range(i + 1)]
    it = jnp.asarray([p[0] for p in pairs], jnp.int32)
    jt = jnp.asarray([p[1] for p in pairs], jnp.int32)
    c2 = scale * LOG2E

    HP = _heads_per_step(nh)

    def body(it_ref, jt_ref, qn_ref, qr_ref, kn_ref, kr_ref, v_ref, o_ref, ob_ref, lse_ref, m_sc, l_sc, acc):
        p = pl.program_id(2)
        i = it_ref[p]
        j = jt_ref[p]

        @pl.when(j == 0)
        def _():
            m_sc[...] = jnp.full_like(m_sc, NEG_BIG)
            l_sc[...] = jnp.zeros_like(l_sc)
            acc[...] = jnp.zeros_like(acc)

        def step(diag):
            for hh in range(HP):
                hs = pl.ds(hh * LANES, LANES)
                qc = jnp.concatenate([qn_ref[:, hs], qr_ref[:, hs]], axis=1)
                kc = jnp.concatenate([kn_ref[:, hs], kr_ref[...]], axis=1)
                st = _att_scores(kc, qc, diag, True, 0)
                m_old = m_sc[hh]
                m_new = jnp.maximum(m_old, jnp.max(st, axis=0, keepdims=True))
                pt = jnp.exp2((st - m_new) * c2)
                corr = jnp.exp2((m_old - m_new) * c2)
                l_sc[hh] = corr * l_sc[hh] + jnp.sum(pt, axis=0, keepdims=True)
                acc[hh] = corr * acc[hh] + _bdot(v_ref[:, hs].T, pt, NN)
                m_sc[hh] = m_new

        @pl.when(j < i)
        def _():
            step(False)

        @pl.when(j == i)
        def _():
            step(True)
            for hh in range(HP):
                hs = pl.ds(hh * LANES, LANES)
                ov = (acc[hh] / l_sc[hh]).T
                o_ref[:, hs] = ov
                ob_ref[:, hs] = ov.astype(BF16)
                lse_row = m_sc[hh] * scale + jnp.log(l_sc[hh])
                lse_ref[:, hs] = jnp.broadcast_to(lse_row, (LANES, tb)).T

    qspec = pl.BlockSpec((tb, HP * LANES), lambda b, h, p, it_, jt_: (b * n + it_[p], h))
    kspec = pl.BlockSpec((tb, HP * LANES), lambda b, h, p, it_, jt_: (b * n + jt_[p], h))
    krspec = pl.BlockSpec((tb, LANES), lambda b, h, p, it_, jt_: (b * n + jt_[p], 0))
    return pl.pallas_call(
        body, name=name,
        grid_spec=pltpu.PrefetchScalarGridSpec(
            num_scalar_prefetch=2, grid=(Bn, nh // HP, len(pairs)),
            in_specs=[qspec, qspec, kspec, krspec, kspec], out_specs=[qspec, qspec, qspec],
            scratch_shapes=[pltpu.VMEM((HP, 1, tb), F32), pltpu.VMEM((HP, 1, tb), F32),
                            pltpu.VMEM((HP, LANES, tb), F32)]),
        out_shape=[jax.ShapeDtypeStruct((T, W), F32), jax.ShapeDtypeStruct((T, W), BF16),
                   jax.ShapeDtypeStruct((T, W), F32)],
        compiler_params=_cparams(),
    )(it, jt, qn, qr, kn, kr, v)


def _attention_dq(qn, qr, kn, kr, v, o, do, lse, S, scale, name):
    T, W = qn.shape
    nh = W // LANES
    Bn = T // S
    tb, n, C = _att_geometry(S)
    pairs = [(i, j) for i in range(n) for j in range(i + 1)]
    it = jnp.asarray([p[0] for p in pairs], jnp.int32)
    jt = jnp.asarray([p[1] for p in pairs], jnp.int32)
    c2 = scale * LOG2E
    HP = _heads_per_step(nh)

    def body(it_ref, jt_ref, qn_ref, qr_ref, kn_ref, kr_ref, v_ref, o_ref, do_ref, lse_ref,
             dqn_ref, dqr_ref, acc, di_sc, lse_sc):
        p = pl.program_id(2)
        i = it_ref[p]
        j = jt_ref[p]

        @pl.when(j == 0)
        def _():
            acc[...] = jnp.zeros_like(acc)
            for hh in range(HP):
                hs = pl.ds(hh * LANES, LANES)
                di_sc[hh] = jnp.sum(do_ref[:, hs] * o_ref[:, hs], axis=1, keepdims=True)
                lse_sc[hh] = jnp.max(lse_ref[:, hs], axis=1, keepdims=True) * LOG2E

        def step(diag):
            for hh in range(HP):
                hs = pl.ds(hh * LANES, LANES)
                qc = jnp.concatenate([qn_ref[:, hs], qr_ref[:, hs]], axis=1)
                for c in range(C):
                    r0 = c * ATT_SUB if diag else 0
                    rows = pl.ds(r0, tb - r0)
                    ks = pl.ds(c * ATT_SUB, ATT_SUB)
                    kc = jnp.concatenate([kn_ref[ks, hs], kr_ref[ks, :]], axis=1)
                    s = _att_scores(qc[r0:], kc, diag, False, 0)
                    pr = jnp.exp2(s * c2 - lse_sc[hh, rows, :])
                    dp = _bdot(do_ref[rows, hs], v_ref[ks, hs], NT)
                    ds = pr * (dp - di_sc[hh, rows, :]) * scale
                    acc[hh, rows, :] += _bdot(ds, kc, NN)

        @pl.when(j < i)
        def _():
            step(False)

        @pl.when(j == i)
        def _():
            step(True)
            for hh in range(HP):
                hs = pl.ds(hh * LANES, LANES)
                dqn_ref[:, hs] = acc[hh, :, 0:LANES].astype(BF16)
                dqr_ref[:, hs] = acc[hh, :, LANES:2 * LANES]

    qspec = pl.BlockSpec((tb, HP * LANES), lambda b, h, p, it_, jt_: (b * n + it_[p], h))
    kspec = pl.BlockSpec((tb, HP * LANES), lambda b, h, p, it_, jt_: (b * n + jt_[p], h))
    krspec = pl.BlockSpec((tb, LANES), lambda b, h, p, it_, jt_: (b * n + jt_[p], 0))
    return pl.pallas_call(
        body, name=name,
        grid_spec=pltpu.PrefetchScalarGridSpec(
            num_scalar_prefetch=2, grid=(Bn, nh // HP, len(pairs)),
            in_specs=[qspec, qspec, kspec, krspec, kspec, qspec, qspec, qspec], out_specs=[qspec, qspec],
            scratch_shapes=[pltpu.VMEM((HP, tb, 2 * LANES), F32), pltpu.VMEM((HP, tb, 1), F32),
                            pltpu.VMEM((HP, tb, 1), F32)]),
        out_shape=[jax.ShapeDtypeStruct((T, W), BF16), jax.ShapeDtypeStruct((T, W), F32)],
        compiler_params=_cparams(),
    )(it, jt, qn, qr, kn, kr, v, o, do, lse)


def _attention_dkv(qn, qr, kn, kr, v, o, do, lse, S, scale, name):
    T, W = qn.shape
    nh = W // LANES
    Bn = T // S
    tb, n, C = _att_geometry(S)
    HP = _heads_per_step(nh)
    triples = [(j, h, i) for j in range(n) for h in range(nh // HP) for i in range(j, n)]
    jt = jnp.asarray([t[0] for t in triples], jnp.int32)
    ht = jnp.asarray([t[1] for t in triples], jnp.int32)
    it = jnp.asarray([t[2] for t in triples], jnp.int32)
    c2 = scale * LOG2E

    def as_row(col):
        return jnp.broadcast_to(col, (ATT_SUB, LANES)).T[0:1, :]

    def body(jt_ref, ht_ref, it_ref, qn_ref, qr_ref, kn_ref, kr_ref, v_ref, o_ref, do_ref, lse_ref,
             dkn_ref, dv_ref, dkr_ref, akc, av):
        p = pl.program_id(1)
        j = jt_ref[p]
        h = ht_ref[p]
        i = it_ref[p]

        @pl.when(jnp.logical_and(h == 0, i == j))
        def _():
            dkr_ref[...] = jnp.zeros_like(dkr_ref)

        @pl.when(i == j)
        def _():
            akc[...] = jnp.zeros_like(akc)
            av[...] = jnp.zeros_like(av)

        def step(diag):
            for hh in range(HP):
                hs = pl.ds(hh * LANES, LANES)
                kc = jnp.concatenate([kn_ref[:, hs], kr_ref[...]], axis=1)
                for c in range(C):
                    rows = pl.ds(c * ATT_SUB, ATT_SUB)
                    k1 = (c + 1) * ATT_SUB if diag else tb
                    ks = pl.ds(0, k1)
                    qc = jnp.concatenate([qn_ref[rows, hs], qr_ref[rows, hs]], axis=1)
                    st = _att_scores(kc[:k1], qc, diag, True, c * ATT_SUB)
                    dov = do_ref[rows, hs]
                    lse_row = as_row(jnp.max(lse_ref[rows, hs], axis=1, keepdims=True)) * LOG2E
                    di_row = as_row(jnp.sum(dov * o_ref[rows, hs], axis=1, keepdims=True))
                    pt = jnp.exp2(st * c2 - lse_row)
                    dst = pt * (_bdot(v_ref[ks, hs], dov, NT) - di_row) * scale
                    av[ks, hs] += _bdot(pt, dov, NN)
                    akc[hh, ks, :] += _bdot(dst, qc, NN)

        @pl.when(i > j)
        def _():
            step(False)

        @pl.when(i == j)
        def _():
            step(True)

        @pl.when(i == n - 1)
        def _():
            for hh in range(HP):
                dkn_ref[:, pl.ds(hh * LANES, LANES)] = akc[hh, :, 0:LANES].astype(BF16)
                dkr_ref[...] += akc[hh, :, LANES:2 * LANES]
            dv_ref[...] = av[...].astype(BF16)

    qspec = pl.BlockSpec((tb, HP * LANES), lambda b, p, jt_, ht_, it_: (b * n + it_[p], ht_[p]))
    kspec = pl.BlockSpec((tb, HP * LANES), lambda b, p, jt_, ht_, it_: (b * n + jt_[p], ht_[p]))
    krspec = pl.BlockSpec((tb, LANES), lambda b, p, jt_, ht_, it_: (b * n + jt_[p], 0))
    return pl.pallas_call(
        body, name=name,
        grid_spec=pltpu.PrefetchScalarGridSpec(
            num_scalar_prefetch=3, grid=(Bn, len(triples)),
            in_specs=[qspec, qspec, kspec, krspec, kspec, qspec, qspec, qspec],
            out_specs=[kspec, kspec, krspec],
            scratch_shapes=[pltpu.VMEM((HP, tb, 2 * LANES), F32), pltpu.VMEM((tb, HP * LANES), F32)]),
        out_shape=[jax.ShapeDtypeStruct((T, W), BF16), jax.ShapeDtypeStruct((T, W), BF16),
                   jax.ShapeDtypeStruct((T, LANES), F32)],
        compiler_params=_cparams(),
    )(jt, ht, it, qn, qr, kn, kr, v, o, do, lse)


def _pad_cols(w, n):
    return jnp.pad(w, ((0, 0), (0, n - w.shape[1])))


def _local_step(x, positions, w, target, late=None, reduce_early=None, reduce_last=None):
    w = dict(w)
    Bn, S, D = x.shape
    T = Bn * S
    depth = w['ln_mix_g'].shape[0]
    alpha = (2 * depth) ** 0.25
    d_inner = w['ssd_norm_g'].shape[1]
    n_heads_ssd = d_inner // SSD_HEAD_DIM
    G = SSD_N_GROUPS
    gn = G * SSD_D_STATE
    conv_dim = d_inner + 2 * gn
    hpg = n_heads_ssd // G
    nh = D // MLA_NOPE
    kv_rank = w['kv_norm_g'].shape[0]
    q_rank = w['q_norm_g'].shape[1]
    H = w['ffn_conv_b'].shape[1] // 2
    scale = (MLA_NOPE + MLA_ROPE) ** -0.5
    assert S % SSD_CHUNK == 0 and hpg % 2 == 0 and SSD_D_STATE == LANES and n_heads_ssd <= LANES

    X = x.reshape(T, D)
    tgt = target.reshape(T, D)

    inv_freq = 1.0 / (ROPE_THETA ** (jnp.arange(0, MLA_ROPE, 2, dtype=F32) / MLA_ROPE))
    ang = positions.reshape(T).astype(F32)[:, None] * inv_freq
    cos, sin = jnp.cos(ang), jnp.sin(ang)
    zpad = jnp.zeros((T, LANES - MLA_ROPE), F32)
    cc = jnp.concatenate([cos, cos, zpad], axis=1)
    ss = jnp.concatenate([-sin, sin, zpad], axis=1)

    w_in = w['ssd_in_proj'][0]
    w_z = w_in[:, :d_inner]
    w_xbc = w_in[:, d_inner:d_inner + conv_dim]
    w_dt = _pad_cols(w_in[:, d_inner + conv_dim:], LANES)
    ssd_cw = w['ssd_conv_w'][0]
    ssd_cb = w['ssd_conv_b']
    dt_bias = _pad_cols(w['ssd_dt_bias'], LANES)
    a_log = _pad_cols(w['ssd_A_log'], LANES)
    d_lane = jnp.repeat(w['ssd_D'], SSD_HEAD_DIM, axis=1)
    ssd_ng = w['ssd_norm_g']

    Xb = X.astype(BF16)
    z = _mm(Xb, w_z, name='ssd_in_z')
    xbc_pre = _mm(Xb, w_xbc, name='ssd_in_xbc')
    dt_pre = _mm(Xb, w_dt, name='ssd_in_dt')
    xbc, xbc_conv = _ssd_conv_fwd(xbc_pre, ssd_cw, ssd_cb, S, name='ssd_conv')
    dtb, cumb = _ssd_dt_prep(dt_pre, dt_bias, a_log, n_heads_ssd, name='ssd_dt_prep')
    y_scan, states, gathered = _scan_fwd(xbc, dtb, cumb, S, d_inner, name='ssd_scan',
                                         gather=late[0] if late else ())
    if late:
        w.update(late[1](gathered))

    w_out = w['ssd_out_proj'][0]
    w_kvc = w['kv_down_proj'][:, :kv_rank]
    w_kvr = _pad_cols(w['kv_down_proj'][:, kv_rank:], LANES)
    kv_g = w['kv_norm_g'].reshape(1, kv_rank)
    w_uk = w['kv_up_k']
    w_uv = w['kv_up_v']
    w_qd = w['q_down_proj'][0]
    q_g = w['q_norm_g']
    qu = w['q_up_proj'][0].reshape(q_rank, nh, MLA_NOPE + MLA_ROPE)
    w_qn = qu[:, :, :MLA_NOPE].reshape(q_rank, nh * LANES)
    w_qr = jnp.pad(qu[:, :, MLA_NOPE:], ((0, 0), (0, 0), (0, LANES - MLA_ROPE))).reshape(q_rank, nh * LANES)
    w_ao = w['attn_out_proj'][0]

    def ffn_parts(i):
        return dict(wu=w['ffn_up'][i], cw=w['ffn_conv_w'][i], cb=w['ffn_conv_b'][i:i + 1], wd=w['ffn_down'][i])

    def ln(name, i):
        return w[name][i:i + 1]

    def ffn_fwd(hb, i):
        f = ffn_parts(i)
        u = _mm(hb, f['wu'], name=f'ffn{i}_up')
        a = _ffn_act_fwd(u, f['cw'], f['cb'], S, name=f'ffn{i}_act')
        return _mm(a, f['wd'], name=f'ffn{i}_down'), (u, a)

    yn = _gate_fwd(y_scan, xbc, z, d_lane, ssd_ng, d_inner, name='ssd_gate')
    mix0 = _mm(yn, w_out, name='ssd_out')
    h1, s1, h1b = _ln_fwd(X, mix0, ln('ln_mix_g', 0), ln('ln_mix_b', 0), alpha, name='ln_mix0')
    ff0, ffn0_saved = ffn_fwd(h1b, 0)
    h2, s2, h2b = _ln_fwd(h1, ff0, ln('ln_ffn_g', 0), ln('ln_ffn_b', 0), alpha, name='ln_ffn0')

    ckv = _mm(h2b, w_kvc, name='kv_down_c')
    kr_pre = _mm(h2b, w_kvr, name='kv_down_r')
    ckvn = _rms_fwd(ckv, kv_g, name='kv_norm')
    kr = _rope(kr_pre, cc, ss, transpose=False, name='k_rope')
    kn = _mm(ckvn, w_uk, out_dtype=BF16, name='kv_up_k')
    v = _mm(ckvn, w_uv, out_dtype=BF16, name='kv_up_v')
    cq_pre = _mm(h2b, w_qd, name='q_down')
    cq = _rms_fwd(cq_pre, q_g, name='q_norm')
    qn = _mm(cq, w_qn, out_dtype=BF16, name='q_up_n')
    qr_pre = _mm(cq, w_qr, name='q_up_r')
    qr = _rope(qr_pre, cc, ss, transpose=False, name='q_rope')
    o, ob, lse = _attention_fwd(qn, qr, kn, kr, v, S, scale, name='attn_fwd')
    mix1 = _mm(ob, w_ao, name='attn_out')
    h3, s3, h3b = _ln_fwd(h2, mix1, ln('ln_mix_g', 1), ln('ln_mix_b', 1), alpha, name='ln_mix1')
    ff1, ffn1_saved = ffn_fwd(h3b, 1)
    h4, s4, _ = _ln_fwd(h3, ff1, ln('ln_ffn_g', 1), ln('ln_ffn_b', 1), alpha, name='ln_ffn1')
    sq, dh4 = _loss_head(h4, tgt, name='loss_head')

    g = {}

    def ffn_bwd(ds, dsb, hb_in, saved, i):
        f = ffn_parts(i)
        u, a = saved
        d_wd = _mm(a, dsb, ta=True, name=f'ffn{i}_down_dw')
        da = _mm(dsb, f['wd'], tb=True, out_dtype=BF16, name=f'ffn{i}_down_dx')
        dug, duv, dwg, dwv, dbg, dbv = _ffn_act_bwd(u, da, f['cw'], f['cb'], S, name=f'ffn{i}_act_bwd')
        d_wu = _mm(hb_in, dug, ta=True, name=f'ffn{i}_up_g_dw', into=lax.empty((D, 2 * H), F32))
        d_wu = _mm(hb_in, duv, ta=True, name=f'ffn{i}_up_v_dw', into=d_wu, into_col=H)
        dh = _mm(dug, f['wu'], tb=True, add=ds, add_scale=alpha, name=f'ffn{i}_up_g_dx')
        dh = _mm(duv, f['wu'], tb=True, add=dh, name=f'ffn{i}_up_v_dx', b_koff=H)
        return dh, d_wd, d_wu, jnp.concatenate([dwg, dwv], axis=1), jnp.concatenate([dbg, dbv], axis=1)

    ds4, ds4b, dg_f1, db_f1 = _ln_bwd(dh4, s4, ln('ln_ffn_g', 1), name='ln_ffn1_bwd')
    dh3, d_wd1, d_wu1, d_fcw1, d_fcb1 = ffn_bwd(ds4, ds4b, h3b, ffn1_saved, 1)
    ds3, ds3b, dg_m1, db_m1 = _ln_bwd(dh3, s3, ln('ln_mix_g', 1), name='ln_mix1_bwd')

    g['attn_out_proj'] = _mm(ob, ds3b, ta=True, name='attn_out_dw')[None]
    do = _mm(ds3b, w_ao, tb=True, name='attn_out_dx')
    dqn, dqr = _attention_dq(qn, qr, kn, kr, v, o, do, lse, S, scale, name='attn_bwd_dq')
    dkn, dv, dkr = _attention_dkv(qn, qr, kn, kr, v, o, do, lse, S, scale, name='attn_bwd_dkv')
    dqr_pre = _rope(dqr, cc, ss, transpose=True, name='q_rope_bwd')
    dkr_pre = _rope(dkr, cc, ss, transpose=True, name='k_rope_bwd')

    d_wqn = _mm(cq, dqn, ta=True, name='q_up_n_dw').reshape(q_rank, nh, LANES)
    d_wqr = _mm(cq, dqr_pre, ta=True, name='q_up_r_dw').reshape(q_rank, nh, LANES)[:, :, :MLA_ROPE]
    g['q_up_proj'] = jnp.concatenate([d_wqn, d_wqr], axis=2).reshape(1, q_rank, nh * (MLA_NOPE + MLA_ROPE))
    dcq = _mm(dqn, w_qn, tb=True, name='q_up_n_dx')
    dcq = _mm(dqr_pre, w_qr, tb=True, add=dcq, name='q_up_r_dx')
    dcq_pre, d_qg = _rms_bwd(dcq, cq_pre, q_g, name='q_norm_bwd')
    g['q_norm_g'] = d_qg
    g['q_down_proj'] = _mm(h2b, dcq_pre, ta=True, name='q_down_dw')[None]

    g['kv_up_k'] = _mm(ckvn, dkn, ta=True, name='kv_up_k_dw')
    g['kv_up_v'] = _mm(ckvn, dv, ta=True, name='kv_up_v_dw')
    dckvn = _mm(dkn, w_uk, tb=True, name='kv_up_k_dx')
    dckvn = _mm(dv, w_uv, tb=True, add=dckvn, name='kv_up_v_dx')
    dckv, d_kvg = _rms_bwd(dckvn, ckv, kv_g, name='kv_norm_bwd')
    g['kv_norm_g'] = d_kvg.reshape(kv_rank)
    g['kv_down_proj'] = jnp.concatenate(
        [_mm(h2b, dckv, ta=True, name='kv_down_c_dw'),
         _mm(h2b, dkr_pre, ta=True, name='kv_down_r_dw')[:, :MLA_ROPE]], axis=1)

    dh2 = _mm(dcq_pre, w_qd, tb=True, add=ds3, add_scale=alpha, name='q_down_dx')
    dh2 = _mm(dckv, w_kvc, tb=True, add=dh2, name='kv_down_c_dx')
    dh2 = _mm(dkr_pre, w_kvr, tb=True, add=dh2, name='kv_down_r_dx')

    ds2, ds2b, dg_f0, db_f0 = _ln_bwd(dh2, s2, ln('ln_ffn_g', 0), name='ln_ffn0_bwd')
    dh1, d_wd0, d_wu0, d_fcw0, d_fcb0 = ffn_bwd(ds2, ds2b, h1b, ffn0_saved, 0)
    ds1, ds1b, dg_m0, db_m0 = _ln_bwd(dh1, s1, ln('ln_mix_g', 0), name='ln_mix0_bwd')

    g['ssd_out_proj'] = _mm(yn, ds1b, ta=True, name='ssd_out_dw')[None]
    dyn = _mm(ds1b, w_out, tb=True, out_dtype=BF16, name='ssd_out_dx')
    dy1, dz, d_ng, d_dl = _gate_bwd(dyn, y_scan, xbc, z, d_lane, ssd_ng, d_inner, name='ssd_gate_bwd')
    g['ffn_up'] = jnp.stack([d_wu0, d_wu1])
    g['ffn_down'] = jnp.stack([d_wd0, d_wd1])
    (dxs, dB, dC, ddt_g, da_g), early = _scan_bwd(xbc, dtb, cumb, states, dy1, d_lane, S, d_inner,
                                                  name='ssd_scan_bwd',
                                                  exchange=reduce_early(g) if reduce_early else ())
    ddt_pre, d_bias, d_alog = _dt_bwd(ddt_g, da_g, dt_pre, dt_bias, a_log, name='ssd_dt_bwd')
    dxbc_pre, d_scw, d_scb = _ssd_conv_bwd(xbc_pre, xbc_conv, [dxs, dB, dC], ssd_cw, S, name='ssd_conv_bwd')
    g['ssd_in_proj'] = jnp.concatenate(
        [_mm(Xb, dz, ta=True, name='ssd_in_z_dw'), _mm(Xb, dxbc_pre, ta=True, name='ssd_in_xbc_dw'),
         _mm(Xb, ddt_pre, ta=True, name='ssd_in_dt_dw')[:, :n_heads_ssd]], axis=1)[None]
    g['ssd_conv_w'] = d_scw[None]
    g['ssd_conv_b'] = d_scb
    g['ssd_norm_g'] = d_ng
    g['ffn_conv_w'] = jnp.stack([d_fcw0, d_fcw1])
    dx = _mm(dz, w_z, tb=True, add=ds1, add_scale=alpha, name='ssd_in_z_dx')
    dx = _mm(dxbc_pre, w_xbc, tb=True, add=dx, name='ssd_in_xbc_dx', exchange=reduce_last(g) if reduce_last else ())
    dx, last = dx if reduce_last else (dx, None)
    dx = _mm(ddt_pre, w_dt, tb=True, add=dx, name='ssd_in_dt_dx')

    g['ssd_dt_bias'] = d_bias[:, :n_heads_ssd]
    g['ssd_A_log'] = d_alog[:, :n_heads_ssd]
    g['ssd_D'] = jnp.sum(d_dl.reshape(1, n_heads_ssd, SSD_HEAD_DIM), axis=2)
    g['ffn_conv_b'] = jnp.concatenate([d_fcb0, d_fcb1], axis=0)
    g['ln_mix_g'] = jnp.concatenate([dg_m0, dg_m1], axis=0)
    g['ln_mix_b'] = jnp.concatenate([db_m0, db_m1], axis=0)
    g['ln_ffn_g'] = jnp.concatenate([dg_f0, dg_f1], axis=0)
    g['ln_ffn_b'] = jnp.concatenate([db_f0, db_f1], axis=0)
    return sq, dx.reshape(Bn, S, D), g, early, last


ANY = pl.BlockSpec(memory_space=pl.ANY)


def _all_gather(xs, name):
    n = len(xs)

    def body(*refs):
        start, forward, finish = _gather_plan(refs[:n], refs[n:2 * n], *refs[2 * n:])
        start()
        forward()
        finish()

    return pl.pallas_call(
        body, name=name, out_shape=_gather_shapes(xs), in_specs=[ANY] * n, out_specs=[ANY] * n,
        scratch_shapes=_gather_sems(n),
    )(*xs)


def _gather_shapes(xs):
    return [jax.ShapeDtypeStruct((N_DEV,) + a.shape, a.dtype) for a in xs]


def _gather_sems(n):
    return [pltpu.SemaphoreType.DMA((7 * n,)), pltpu.SemaphoreType.DMA((7 * n,)), pltpu.SemaphoreType.DMA((n,))]


def _gather_plan(x_refs, out_refs, send_sems, recv_sems, local_sems):
    n = len(x_refs)
    x, y, c = lax.axis_index("x"), lax.axis_index("y"), lax.axis_index("c")
    me, sibling = (x, y, c), (x, y, 1 - c)
    chips = [(1 - x, y), (x, 1 - y), (1 - x, 1 - y)]

    def rows(i, px, py, pc):
        return out_refs[i].at[4 * px + 2 * py + pc]

    def copy(i, k, block, to, own=False):
        return pltpu.make_async_remote_copy(
            src_ref=x_refs[i] if own else rows(i, *block), dst_ref=rows(i, *block),
            send_sem=send_sems.at[7 * i + k], recv_sem=recv_sems.at[7 * i + k],
            device_id=to, device_id_type=MESH)

    def mine():
        return [pltpu.make_async_copy(x_refs[i], rows(i, *me), local_sems.at[i]) for i in range(n)]

    def first():
        out = []
        for i in range(n):
            out.append(copy(i, 0, me, sibling, own=True))
            out += [copy(i, 1 + j, me, (*chip, c), own=True) for j, chip in enumerate(chips)]
        return out

    def passed():
        return [copy(i, 4 + j, (*chip, c), sibling) for j, chip in enumerate(chips) for i in range(n)]

    def start():
        for cp in mine() + first():
            cp.start()

    def forward():
        for j, chip in enumerate(chips):
            for i in range(n):
                copy(i, 1 + j, (*chip, c), me).wait_recv()
                copy(i, 4 + j, (*chip, c), sibling).start()

    def finish():
        for i in range(n):
            copy(i, 0, sibling, me).wait_recv()
            for j, chip in enumerate(chips):
                copy(i, 4 + j, (*chip, 1 - c), me).wait_recv()
        for cp in first() + passed():
            cp.wait_send()
        for cp in mine():
            cp.wait()

    return start, forward, finish


def _exchange_sibling(gs, name):
    n = len(gs)

    def body(*refs):
        g_refs, r_refs = refs[:n], refs[n:2 * n]
        send_sems, recv_sems = refs[2 * n:]
        x, y, c = lax.axis_index("x"), lax.axis_index("y"), lax.axis_index("c")
        copies = [pltpu.make_async_remote_copy(
            src_ref=g_refs[i].at[2 * k + (1 - c)], dst_ref=r_refs[i].at[k], send_sem=send_sems.at[4 * i + k],
            recv_sem=recv_sems.at[4 * i + k], device_id=(x, y, 1 - c), device_id_type=MESH)
            for i in range(n) for k in range(4)]
        for cp in copies:
            cp.start()
        for cp in copies:
            cp.wait()

    return pl.pallas_call(
        body, name=name, out_shape=[jax.ShapeDtypeStruct((4,) + g.shape[1:], g.dtype) for g in gs],
        in_specs=[ANY] * n, out_specs=[ANY] * n,
        scratch_shapes=[pltpu.SemaphoreType.DMA((4 * n,)), pltpu.SemaphoreType.DMA((4 * n,))],
    )(*gs)


def _chips_shapes(parts):
    return [jax.ShapeDtypeStruct((3,) + p.shape[1:], p.dtype) for p in parts]


def _chips_sems(n):
    return [pltpu.SemaphoreType.DMA((3 * n,)), pltpu.SemaphoreType.DMA((3 * n,))]


def _chips_plan(p_refs, r_refs, send_sems, recv_sems):
    n = len(p_refs)
    x, y, c = lax.axis_index("x"), lax.axis_index("y"), lax.axis_index("c")
    chips = [(1 - x, y), (x, 1 - y), (1 - x, 1 - y)]

    def copies():
        return [pltpu.make_async_remote_copy(
            src_ref=p_refs[i].at[2 * cx + cy], dst_ref=r_refs[i].at[j], send_sem=send_sems.at[3 * i + j],
            recv_sem=recv_sems.at[3 * i + j], device_id=(cx, cy, c), device_id_type=MESH)
            for i in range(n) for j, (cx, cy) in enumerate(chips)]

    def start():
        for cp in copies():
            cp.start()

    def finish():
        for cp in copies():
            cp.wait()

    return start, finish


def _row_tile(rows, cols):
    want = max(8, (256 * 1024) // cols)
    for t in range(min(rows, want) // 8 * 8, 7, -8):
        if rows % t == 0:
            return t
    return rows


def _add_sibling(gfull, r1, core, name):
    _, rows, lanes = gfull.shape
    tr = _row_tile(rows, lanes)

    def body(c_ref, g_ref, r_ref, o_ref):
        o_ref[...] = g_ref[...] + r_ref[...]

    return pl.pallas_call(
        body, name=name,
        grid_spec=pltpu.PrefetchScalarGridSpec(
            num_scalar_prefetch=1, grid=(4, rows // tr),
            in_specs=[pl.BlockSpec((1, tr, lanes), lambda k, r, c_ref: (2 * k + c_ref[0], r, 0)),
                      pl.BlockSpec((1, tr, lanes), lambda k, r, c_ref: (k, r, 0))],
            out_specs=pl.BlockSpec((1, tr, lanes), lambda k, r, c_ref: (k, r, 0))),
        out_shape=jax.ShapeDtypeStruct((4, rows, lanes), F32), compiler_params=_cparams(),
    )(core, gfull, r1)


def _adam_math(wv, gv, mv, vv):
    m = ADAM_B1 * mv + (1.0 - ADAM_B1) * gv
    v = ADAM_B2 * vv + (1.0 - ADAM_B2) * (gv * gv)
    m_hat = m / (1.0 - ADAM_B1 ** ADAM_STEP)
    v_hat = v / (1.0 - ADAM_B2 ** ADAM_STEP)
    delta = -ADAM_LR * (m_hat / (jnp.sqrt(v_hat) + ADAM_EPS) + ADAM_WD * wv)
    return delta, m, v


def _adam_sharded(part, r2, chip, wts, m, v, name):
    rows, lanes = wts.shape
    tr = _row_tile(rows, lanes)

    def body(k_ref, p_ref, r_ref, w_ref, m_ref, v_ref, g_ref, d_ref, mo_ref, vo_ref):
        gv = p_ref[0] + r_ref[0] + r_ref[1] + r_ref[2]
        delta, mn, vn = _adam_math(w_ref[...], gv, m_ref[...], v_ref[...])
        g_ref[...] = gv
        d_ref[...] = delta
        mo_ref[...] = mn
        vo_ref[...] = vn

    flat = pl.BlockSpec((tr, lanes), lambda r, k_ref: (r, 0))
    sh = jax.ShapeDtypeStruct((rows, lanes), F32)
    return pl.pallas_call(
        body, name=name,
        grid_spec=pltpu.PrefetchScalarGridSpec(
            num_scalar_prefetch=1, grid=(rows // tr,),
            in_specs=[pl.BlockSpec((1, tr, lanes), lambda r, k_ref: (k_ref[0], r, 0)),
                      pl.BlockSpec((3, tr, lanes), lambda r, k_ref: (0, r, 0)), flat, flat, flat],
            out_specs=[flat, flat, flat, flat]),
        out_shape=[sh, sh, sh, sh], compiler_params=_cparams(),
    )(chip, part, r2, wts, m, v)


def _adam_replicated(gall, wts, m, v, name):
    rows, lanes = wts.shape

    def body(ga_ref, w_ref, m_ref, v_ref, g_ref, d_ref, mo_ref, vo_ref):
        gv = ga_ref[0]
        for k in range(1, N_DEV):
            gv = gv + ga_ref[k]
        delta, mn, vn = _adam_math(w_ref[...], gv, m_ref[...], v_ref[...])
        g_ref[...] = gv
        d_ref[...] = delta
        mo_ref[...] = mn
        vo_ref[...] = vn

    sh = jax.ShapeDtypeStruct((rows, lanes), F32)
    return pl.pallas_call(body, name=name, out_shape=[sh, sh, sh, sh], compiler_params=_cparams())(gall, wts, m, v)


def _pack(arrs, dtype, row_mult):
    flat = jnp.concatenate([a.reshape(-1).astype(dtype) for a in arrs])
    n = flat.shape[0]
    unit = LANES * row_mult
    total = -(-n // unit) * unit
    return jnp.pad(flat, (0, total - n)).reshape(total // LANES, LANES)


def _unpack(flat2d, shapes):
    flat = flat2d.reshape(-1)
    out, off = [], 0
    for shp in shapes:
        n = math.prod(shp)
        out.append(flat[off:off + n].reshape(shp))
        off += n
    return out


def _unpack_gathered(gathered, shapes, axes):
    flat = gathered.reshape(N_DEV, -1)
    out, off = [], 0
    for shp, ax in zip(shapes, axes):
        n = math.prod(shp)
        seg = flat[:, off:off + n].reshape((N_DEV,) + tuple(shp))
        out.append(jnp.concatenate([seg[i] for i in range(N_DEV)], axis=ax))
        off += n
    return out


def _pack_chunks(fulls, axes, row_mult):
    per_dev = []
    for full, ax in zip(fulls, axes):
        parts = jnp.stack(jnp.split(full, N_DEV, axis=ax))
        per_dev.append(parts.reshape(N_DEV, -1))
    flat = jnp.concatenate(per_dev, axis=1)
    n = flat.shape[1]
    unit = LANES * row_mult
    total = -(-n // unit) * unit
    return jnp.pad(flat, ((0, 0), (0, total - n))).reshape(N_DEV, total // LANES, LANES)


def kernel(x, positions, ssd_in_proj, ssd_conv_w, ssd_conv_b, ssd_dt_bias, ssd_A_log, ssd_D, ssd_norm_g, ssd_out_proj, kv_down_proj, kv_norm_g, kv_up_k, kv_up_v, q_down_proj, q_norm_g, q_up_proj, attn_out_proj, ffn_up, ffn_conv_w, ffn_conv_b, ffn_down, ln_mix_g, ln_mix_b, ln_ffn_g, ln_ffn_b, loss_target, m_ssd_in_proj, m_ssd_conv_w, m_ssd_conv_b, m_ssd_dt_bias, m_ssd_A_log, m_ssd_D, m_ssd_norm_g, m_ssd_out_proj, m_kv_down_proj, m_kv_norm_g, m_kv_up_k, m_kv_up_v, m_q_down_proj, m_q_norm_g, m_q_up_proj, m_attn_out_proj, m_ffn_up, m_ffn_conv_w, m_ffn_conv_b, m_ffn_down, m_ln_mix_g, m_ln_mix_b, m_ln_ffn_g, m_ln_ffn_b, v_ssd_in_proj, v_ssd_conv_w, v_ssd_conv_b, v_ssd_dt_bias, v_ssd_A_log, v_ssd_D, v_ssd_norm_g, v_ssd_out_proj, v_kv_down_proj, v_kv_norm_g, v_kv_up_k, v_kv_up_v, v_q_down_proj, v_q_norm_g, v_q_up_proj, v_attn_out_proj, v_ffn_up, v_ffn_conv_w, v_ffn_conv_b, v_ffn_down, v_ln_mix_g, v_ln_mix_b, v_ln_ffn_g, v_ln_ffn_b):
    given = dict(locals())
    wl = {n: given[n] for n in WEIGHTS}
    ml = {n: given['m_' + n] for n in WEIGHTS}
    vl = {n: given['v_' + n] for n in WEIGHTS}
    sharded_axis = dict(SHARDED)
    big = [n for n, _ in SHARDED if n not in SHARDED_F32]
    small = [n for n, _ in SHARDED if n in SHARDED_F32]

    def as2d(a):
        return a.reshape(-1, a.shape[-1])

    def to_full(gathered, n):
        shp, ax = wl[n].shape, sharded_axis[n]
        moved = jnp.moveaxis(gathered.reshape((N_DEV,) + shp), 0, ax)
        return moved.reshape(shp[:ax] + (N_DEV * shp[ax],) + shp[ax + 1:])

    def to_chunks(full_grad, n):
        shp, ax = wl[n].shape, sharded_axis[n]
        split = full_grad.reshape(shp[:ax] + (N_DEV, shp[ax]) + shp[ax + 1:])
        return jnp.moveaxis(split, ax, 0).reshape((N_DEV,) + as2d(wl[n]).shape)

    first = [n for n in big if n in GATHERED_FIRST]
    late = [n for n in big if n not in GATHERED_FIRST]
    full = {n: wl[n] for n in REPLICATED}
    gathered = _all_gather([as2d(wl[n]).astype(BF16) for n in first] + [_pack([wl[n] for n in small], F32, 8)],
                           name='gather_weights')
    for n, gat in zip(first, gathered[:-1]):
        full[n] = to_full(gat, n)
    full.update(zip(small, _unpack_gathered(gathered[-1], [wl[n].shape for n in small],
                                            [sharded_axis[n] for n in small])))

    cx, cy, cc = lax.axis_index("x"), lax.axis_index("y"), lax.axis_index("c")
    core = jnp.reshape(cc, (1,)).astype(jnp.int32)
    chip = jnp.reshape(2 * cx + cy, (1,)).astype(jnp.int32)
    early_names = [n for n in big if n not in REDUCED_LAST]
    last_names = [n for n in big if n in REDUCED_LAST]
    parts = {}

    def to_sibling(chunked, names, tag):
        r1 = _exchange_sibling(chunked, name=f'grads_to_sibling_{tag}')
        return [_add_sibling(gf, r, core, name=f'grads_add_sibling_{n}') for n, gf, r in zip(names, chunked, r1)]

    def reduce_early(grads):
        parts.update(zip(early_names, to_sibling([to_chunks(grads[n], n) for n in early_names], early_names, 'early')))
        return [parts[n] for n in early_names]

    def reduce_last(grads):
        chunked = [to_chunks(grads[n], n) for n in last_names]
        chunked.append(_pack_chunks([grads[n] for n in small], [sharded_axis[n] for n in small], 8))
        parts.update(zip(last_names + ['small'], to_sibling(chunked, last_names + ['small'], 'last')))
        return [parts[n] for n in last_names + ['small']]

    sq, grad_x, grads, r2_early, r2_last = _local_step(
        x, positions, full, loss_target,
        late=([as2d(wl[n]).astype(BF16) for n in late], lambda gats: {n: to_full(g, n) for n, g in zip(late, gats)}),
        reduce_early=reduce_early, reduce_last=reduce_last)
    loss = lax.psum(0.5 * jnp.sum(sq) / x.shape[-1], ("x", "y", "c"))

    r2 = dict(zip(early_names, r2_early))
    r2.update(zip(last_names + ['small'], r2_last))
    res = {}
    kinds = ('grad', 'delta', 'new_m', 'new_v')
    for n in big:
        outs = _adam_sharded(parts[n], r2[n], chip, as2d(wl[n]), as2d(ml[n]), as2d(vl[n]), name=f'adamw_{n}')
        for kind, a in zip(kinds, outs):
            res[kind, n] = a.reshape(wl[n].shape)
    outs = _adam_sharded(parts['small'], r2['small'], chip, _pack([wl[n] for n in small], F32, 8),
                         _pack([ml[n] for n in small], F32, 8), _pack([vl[n] for n in small], F32, 8),
                         name='adamw_small_sharded')
    for kind, packed in zip(kinds, outs):
        for n, a in zip(small, _unpack(packed, [wl[n].shape for n in small])):
            res[kind, n] = a

    rep = list(REPLICATED)
    rshapes = [wl[n].shape for n in rep]
    gall, = _all_gather([_pack([grads[n] for n in rep], F32, 8)], name='gather_replicated_grads')
    outs = _adam_replicated(gall, _pack([wl[n] for n in rep], F32, 8), _pack([ml[n] for n in rep], F32, 8),
                            _pack([vl[n] for n in rep], F32, 8), name='adamw_replicated')
    for kind, packed in zip(('grad', 'delta', 'new_m', 'new_v'), outs):
        for n, a in zip(rep, _unpack(packed, rshapes)):
            res[kind, n] = a

    return (loss, grad_x, *[res['grad', n] for n in WEIGHTS], *[res['delta', n] for n in WEIGHTS],
            *[res['new_m', n] for n in WEIGHTS], *[res['new_v', n] for n in WEIGHTS])
```

```python
import functools
import math

import jax
import jax.numpy as jnp
from jax import lax
from jax.experimental import pallas as pl
from jax.experimental.pallas import tpu as pltpu

F32 = jnp.float32
BF16 = jnp.bfloat16
MESH = pl.DeviceIdType.MESH

N_DEV = 8
LANES = 128
MM_TILES = (1024, 1408, 896, 512, 384, 256, 128)
MM_VMEM_BUDGET = 38 * 1024 * 1024
MM_VMEM_LIMIT = 56 * 1024 * 1024
VMEM_LIMIT = 32 * 1024 * 1024
LN_EPS = 1e-5
RMS_EPS = 1e-6
SSD_HEAD_DIM = 64
SSD_N_GROUPS = 8
SSD_D_STATE = 128
SSD_CHUNK = 128
MLA_NOPE = 128
MLA_ROPE = 64
MLA_V = 128
ROPE_THETA = 10000.0
ADAM_LR = 0.001
ADAM_B1 = 0.9
ADAM_B2 = 0.999
ADAM_EPS = 1e-08
ADAM_WD = 0.01
ADAM_STEP = 10
NEG_BIG = -1e30

SHARDED = (('ssd_in_proj', 2), ('ssd_conv_w', 2), ('ssd_conv_b', 1), ('ssd_norm_g', 1), ('ssd_out_proj', 1),
           ('kv_down_proj', 0), ('kv_up_k', 1), ('kv_up_v', 1), ('q_down_proj', 1), ('q_up_proj', 2),
           ('attn_out_proj', 1), ('ffn_up', 2), ('ffn_conv_w', 2), ('ffn_down', 1))
SHARDED_F32 = ('ssd_conv_w', 'ssd_conv_b', 'ssd_norm_g', 'ffn_conv_w')
GATHERED_FIRST = ('ssd_in_proj',)
REDUCED_LAST = ('ssd_in_proj',)
REPLICATED = ('ssd_dt_bias', 'ssd_A_log', 'ssd_D', 'kv_norm_g', 'q_norm_g', 'ffn_conv_b',
              'ln_mix_g', 'ln_mix_b', 'ln_ffn_g', 'ln_ffn_b')
WEIGHTS = ('ssd_in_proj', 'ssd_conv_w', 'ssd_conv_b', 'ssd_dt_bias', 'ssd_A_log', 'ssd_D', 'ssd_norm_g',
           'ssd_out_proj', 'kv_down_proj', 'kv_norm_g', 'kv_up_k', 'kv_up_v', 'q_down_proj', 'q_norm_g',
           'q_up_proj', 'attn_out_proj', 'ffn_up', 'ffn_conv_w', 'ffn_conv_b', 'ffn_down', 'ln_mix_g',
           'ln_mix_b', 'ln_ffn_g', 'ln_ffn_b')


def _cparams(limit=None):
    return pltpu.CompilerParams(vmem_limit_bytes=VMEM_LIMIT if limit is None else limit)


def _tile(n, cands):
    for c in cands:
        if n % c == 0:
            return c
    return n


def _sigmoid(x):
    return 1.0 / (1.0 + jnp.exp(-x))


def _iota(shape, axis):
    return lax.broadcasted_iota(jnp.int32, shape, axis)


def _mm(a, b, *, ta=False, tb=False, add=None, add_scale=1.0, out_dtype=F32, name, b_koff=0, into=None, into_col=0,
        exchange=()):
    if ta:
        K, M = a.shape
    else:
        M, K = a.shape
    if tb:
        N, Kb = b.shape
    else:
        Kb, N = b.shape
    assert b_koff + K <= Kb, (a.shape, b.shape, ta, tb, b_koff)
    tm = _tile(M, MM_TILES)
    tn = _tile(N, MM_TILES)
    tk = K if K <= MM_TILES[0] and b_koff == 0 and Kb == K else _tile(K, MM_TILES)
    nk = K // tk
    assert b_koff % tk == 0 and (into is None or (into.shape[0] == M and into.dtype == out_dtype))
    kb = b_koff // tk

    def vmem_estimate(tm_, tn_):
        ins = 2 * (tm_ * tk * a.dtype.itemsize + tk * tn_ * b.dtype.itemsize)
        outs = tm_ * tn_ * (2 * jnp.dtype(out_dtype).itemsize + 4 + (4 if nk > 1 else 0))
        return ins + outs + (2 * tm_ * tn_ * add.dtype.itemsize if add is not None else 0)

    while vmem_estimate(tm, tn) > MM_VMEM_BUDGET:
        can_m, can_n = tm % (2 * LANES) == 0, tn % (2 * LANES) == 0
        if can_m and (tm >= tn or not can_n):
            tm //= 2
        elif can_n:
            tn //= 2
        else:
            break

    assert into_col % tn == 0
    jb = into_col // tn

    nx = len(exchange)
    n_in = (add is not None) + (into is not None)
    grid = (M // tm, N // tn, nk)

    def body(a_ref, b_ref, *rest):
        add_ref = rest[0] if add is not None else None
        o_ref = rest[n_in + nx]
        acc = rest[n_in + 2 * nx + 1] if nk > 1 else None
        k = pl.program_id(2)
        if nx:
            start, finish_exchange = _chips_plan(rest[n_in:n_in + nx], rest[n_in + nx + 1:n_in + 2 * nx + 1],
                                                 *rest[len(rest) - 2:])
            t, total = _grid_step(grid)
            pl.when(t == 0)(start)

        if ta:
            av = a_ref[...].astype(BF16).T
        else:
            av = a_ref[...].astype(BF16)
        bv = b_ref[...].astype(BF16)
        dn = (((1,), (1 if tb else 0,)), ((), ()))
        part = lax.dot_general(av, bv, dn, preferred_element_type=F32)

        def finish(r):
            if add is not None:
                r = r + add_scale * add_ref[...].astype(F32)
            o_ref[...] = r.astype(out_dtype)

        if nk == 1:
            finish(part)
        else:
            @pl.when(k == 0)
            def _():
                acc[...] = part

            @pl.when(k > 0)
            def _():
                acc[...] += part

            @pl.when(k == nk - 1)
            def _():
                finish(acc[...])

        if nx:
            pl.when(t == total - 1)(finish_exchange)

    a_spec = (pl.BlockSpec((tk, tm), lambda i, j, k: (k, i)) if ta
              else pl.BlockSpec((tm, tk), lambda i, j, k: (i, k)))
    b_spec = (pl.BlockSpec((tn, tk), lambda i, j, k: (j, k + kb)) if tb
              else pl.BlockSpec((tk, tn), lambda i, j, k: (k + kb, j)))
    in_specs = [a_spec, b_spec]
    args = [a, b]
    if add is not None:
        in_specs.append(pl.BlockSpec((tm, tn), lambda i, j, k: (i, j)))
        args.append(add)
    aliases = {}
    if into is not None:
        aliases = {len(args): 0}
        in_specs.append(pl.BlockSpec(memory_space=pl.ANY))
        args.append(into)
    any_spec = pl.BlockSpec(memory_space=pl.ANY)
    outs = pl.pallas_call(
        body, name=name, grid=grid,
        in_specs=in_specs + [any_spec] * nx,
        out_specs=[pl.BlockSpec((tm, tn), lambda i, j, k: (i, j + jb))] + [any_spec] * nx,
        out_shape=[jax.ShapeDtypeStruct((M, N) if into is None else into.shape, out_dtype)] + _chips_shapes(exchange),
        scratch_shapes=([pltpu.VMEM((tm, tn), F32)] if nk > 1 else []) + (_chips_sems(nx) if nx else []),
        input_output_aliases=aliases, compiler_params=_cparams(MM_VMEM_LIMIT),
    )(*args, *exchange)
    return (outs[0], list(outs[1:])) if nx else outs[0]


def _ln_fwd(h, mix, g, b, alpha, name):
    T, D = h.shape
    tm = _tile(T, (256, 128))

    def body(h_ref, m_ref, g_ref, b_ref, y_ref, s_ref, yb_ref):
        s = alpha * h_ref[...] + m_ref[...]
        mu = jnp.mean(s, axis=1, keepdims=True)
        xc = s - mu
        var = jnp.mean(xc * xc, axis=1, keepdims=True)
        y = xc * lax.rsqrt(var + LN_EPS) * g_ref[...] + b_ref[...]
        y_ref[...] = y
        s_ref[...] = s
        yb_ref[...] = y.astype(BF16)

    row = pl.BlockSpec((tm, D), lambda i: (i, 0))
    vec = pl.BlockSpec((1, D), lambda i: (0, 0))
    return pl.pallas_call(
        body, name=name, grid=(T // tm,), in_specs=[row, row, vec, vec], out_specs=[row, row, row],
        out_shape=[jax.ShapeDtypeStruct((T, D), F32)] * 2 + [jax.ShapeDtypeStruct((T, D), BF16)],
        compiler_params=_cparams(),
    )(h, mix, g, b)


def _ln_bwd(dy, s, g, name):
    T, D = s.shape
    tm = _tile(T, (256, 128))

    def body(dy_ref, s_ref, g_ref, ds_ref, dsb_ref, dg_ref, db_ref):
        @pl.when(pl.program_id(0) == 0)
        def _():
            dg_ref[...] = jnp.zeros_like(dg_ref)
            db_ref[...] = jnp.zeros_like(db_ref)

        sv = s_ref[...]
        dyv = dy_ref[...]
        mu = jnp.mean(sv, axis=1, keepdims=True)
        xc = sv - mu
        rstd = lax.rsqrt(jnp.mean(xc * xc, axis=1, keepdims=True) + LN_EPS)
        xhat = xc * rstd
        dxh = dyv * g_ref[...]
        m1 = jnp.mean(dxh, axis=1, keepdims=True)
        m2 = jnp.mean(dxh * xhat, axis=1, keepdims=True)
        ds = rstd * (dxh - m1 - xhat * m2)
        ds_ref[...] = ds
        dsb_ref[...] = ds.astype(BF16)
        dg_ref[...] += jnp.sum(dyv * xhat, axis=0, keepdims=True)
        db_ref[...] += jnp.sum(dyv, axis=0, keepdims=True)

    row = pl.BlockSpec((tm, D), lambda i: (i, 0))
    vec = pl.BlockSpec((1, D), lambda i: (0, 0))
    return pl.pallas_call(
        body, name=name, grid=(T // tm,), in_specs=[row, row, vec], out_specs=[row, row, vec, vec],
        out_shape=[jax.ShapeDtypeStruct((T, D), F32), jax.ShapeDtypeStruct((T, D), BF16),
                   jax.ShapeDtypeStruct((1, D), F32), jax.ShapeDtypeStruct((1, D), F32)],
        compiler_params=_cparams(),
    )(dy, s, g)


def _loss_head(y, target, name):
    T, D = y.shape
    tm = _tile(T, (256, 128))

    def body(y_ref, t_ref, sq_ref, dy_ref):
        @pl.when(pl.program_id(0) == 0)
        def _():
            sq_ref[...] = jnp.zeros_like(sq_ref)

        e = y_ref[...] - t_ref[...]
        sq_ref[...] += jnp.sum(e * e, axis=0, keepdims=True)
        dy_ref[...] = e * (1.0 / D)

    row = pl.BlockSpec((tm, D), lambda i: (i, 0))
    vec = pl.BlockSpec((1, D), lambda i: (0, 0))
    return pl.pallas_call(
        body, name=name, grid=(T // tm,), in_specs=[row, row], out_specs=[vec, row],
        out_shape=[jax.ShapeDtypeStruct((1, D), F32), jax.ShapeDtypeStruct((T, D), F32)],
        compiler_params=_cparams(),
    )(y, target)


def _shift_down(x, j):
    if j == 0:
        return x
    return jnp.where(_iota(x.shape, 0) >= j, pltpu.roll(x, j, 0), 0.0)


def _shift_up(x, j):
    if j == 0:
        return x
    n = x.shape[0]
    return jnp.where(_iota(x.shape, 0) < n - j, pltpu.roll(x, n - j, 0), 0.0)


def _conv_val(x, w_ref, b_ref):
    width = w_ref.shape[0]
    y = b_ref[...] + w_ref[width - 1:width, :] * x
    for j in range(1, width):
        y = y + w_ref[width - 1 - j:width - j, :] * _shift_down(x, j)
    return y


def _conv_bwd_val(x, dc, w_ref, dw_ref, db_ref):
    width = w_ref.shape[0]
    dx = w_ref[width - 1:width, :] * dc
    dw_ref[width - 1:width, :] += jnp.sum(dc * x, axis=0, keepdims=True)
    for j in range(1, width):
        sj = _shift_up(dc, j)
        dx = dx + w_ref[width - 1 - j:width - j, :] * sj
        dw_ref[width - 1 - j:width - j, :] += jnp.sum(sj * x, axis=0, keepdims=True)
    db_ref[...] += jnp.sum(dc, axis=0, keepdims=True)
    return dx


def _ffn_specs(H, S, width, cb):
    cb = _tile(H, (cb, LANES))
    nC = H // cb
    return dict(
        g=pl.BlockSpec((S, cb), lambda j, b: (b, j)), v=pl.BlockSpec((S, cb), lambda j, b: (b, j + nC)),
        wg=pl.BlockSpec((width, cb), lambda j, b: (0, j)), wv=pl.BlockSpec((width, cb), lambda j, b: (0, j + nC)),
        bg=pl.BlockSpec((1, cb), lambda j, b: (0, j)), bv=pl.BlockSpec((1, cb), lambda j, b: (0, j + nC))), nC


def _ffn_act_fwd(u, cw, cb, S, name):
    T, H2 = u.shape
    H = H2 // 2
    sp, nC = _ffn_specs(H, S, cw.shape[0], 2 * LANES)

    def body(ug_ref, uv_ref, wg_ref, wv_ref, bg_ref, bv_ref, a_ref):
        g = _conv_val(ug_ref[...], wg_ref, bg_ref)
        v = _conv_val(uv_ref[...], wv_ref, bv_ref)
        a_ref[...] = (g * _sigmoid(g) * v).astype(BF16)

    return pl.pallas_call(
        body, name=name, grid=(nC, T // S),
        in_specs=[sp['g'], sp['v'], sp['wg'], sp['wv'], sp['bg'], sp['bv']], out_specs=sp['g'],
        out_shape=jax.ShapeDtypeStruct((T, H), BF16), compiler_params=_cparams(),
    )(u, u, cw, cw, cb, cb)


def _ffn_act_bwd(u, da, cw, cb, S, name):
    T, H2 = u.shape
    H = H2 // 2
    width = cw.shape[0]
    sp, nC = _ffn_specs(H, S, width, LANES)

    def body(ug_ref, uv_ref, da_ref, wg_ref, wv_ref, bg_ref, bv_ref,
             dug_ref, duv_ref, dwg_ref, dwv_ref, dbg_ref, dbv_ref):
        @pl.when(pl.program_id(1) == 0)
        def _():
            for r in (dwg_ref, dwv_ref, dbg_ref, dbv_ref):
                r[...] = jnp.zeros_like(r)

        def conv_bwd(x, dc, w_ref, dw_ref, db_ref):
            dx = w_ref[width - 1:width, :] * dc
            for j in range(1, width):
                dx = dx + w_ref[width - 1 - j:width - j, :] * _shift_up(dc, j)
            for j in range(width):
                dw_ref[width - 1 - j:width - j, :] += jnp.sum(dc * _shift_down(x, j), axis=0, keepdims=True)
            db_ref[...] += jnp.sum(dc, axis=0, keepdims=True)
            return dx

        xg = ug_ref[...]
        xv = uv_ref[...]
        g = _conv_val(xg, wg_ref, bg_ref)
        v = _conv_val(xv, wv_ref, bv_ref)
        dav = da_ref[...].astype(F32)
        sg = _sigmoid(g)
        dg = dav * v * sg * (1.0 + g * (1.0 - sg))
        dv = dav * g * sg
        dug_ref[...] = conv_bwd(xg, dg, wg_ref, dwg_ref, dbg_ref).astype(BF16)
        duv_ref[...] = conv_bwd(xv, dv, wv_ref, dwv_ref, dbv_ref).astype(BF16)

    act = jax.ShapeDtypeStruct((T, H), BF16)
    wsh = jax.ShapeDtypeStruct((width, H), F32)
    bsh = jax.ShapeDtypeStruct((1, H), F32)
    return pl.pallas_call(
        body, name=name, grid=(nC, T // S),
        in_specs=[sp['g'], sp['v'], sp['g'], sp['wg'], sp['wv'], sp['bg'], sp['bv']],
        out_specs=[sp['g'], sp['g'], sp['wg'], sp['wg'], sp['bg'], sp['bg']],
        out_shape=[act, act, wsh, wsh, bsh, bsh], compiler_params=_cparams(),
    )(u, u, da, cw, cw, cb, cb)


def _ssd_conv_fwd(x, cw, cb, S, name):
    T, C = x.shape
    Cb = _tile(C, (256, 128))
    width = cw.shape[0]

    def body(x_ref, w_ref, b_ref, y_ref, c_ref):
        c = _conv_val(x_ref[...], w_ref, b_ref)
        c_ref[...] = c
        y_ref[...] = c * _sigmoid(c)

    blk = pl.BlockSpec((S, Cb), lambda j, b: (b, j))
    wblk = pl.BlockSpec((width, Cb), lambda j, b: (0, j))
    bblk = pl.BlockSpec((1, Cb), lambda j, b: (0, j))
    return pl.pallas_call(
        body, name=name, grid=(C // Cb, T // S), in_specs=[blk, wblk, bblk], out_specs=[blk, blk],
        out_shape=[jax.ShapeDtypeStruct((T, C), F32)] * 2, compiler_params=_cparams(),
    )(x, cw, cb)


def _ssd_conv_bwd(x, c, dys, cw, S, name):
    T, C = x.shape
    Cb = _tile(C, (256, 128))
    width = cw.shape[0]
    assert all(d.shape[1] % Cb == 0 for d in dys) and sum(d.shape[1] for d in dys) == C

    def part(dy, j0, dx_prev, k):
        n = dy.shape[1] // Cb

        def body(x_ref, c_ref, dy_ref, w_ref, prev_ref, dx_ref, dw_ref, db_ref):
            @pl.when(pl.program_id(1) == 0)
            def _():
                dw_ref[...] = jnp.zeros_like(dw_ref)
                db_ref[...] = jnp.zeros_like(db_ref)

            cval = c_ref[...]
            sc = _sigmoid(cval)
            dc = dy_ref[...] * sc * (1.0 + cval * (1.0 - sc))
            dx_ref[...] = _conv_bwd_val(x_ref[...], dc, w_ref, dw_ref, db_ref).astype(BF16)

        wide = pl.BlockSpec((S, Cb), lambda j, b: (b, j + j0))
        return pl.pallas_call(
            body, name=f'{name}_{k}', grid=(n, T // S),
            in_specs=[wide, wide, pl.BlockSpec((S, Cb), lambda j, b: (b, j)),
                      pl.BlockSpec((width, Cb), lambda j, b: (0, j + j0)), pl.BlockSpec(memory_space=pl.ANY)],
            out_specs=[wide, pl.BlockSpec((width, Cb), lambda j, b: (0, j)), pl.BlockSpec((1, Cb), lambda j, b: (0, j))],
            out_shape=[jax.ShapeDtypeStruct((T, C), BF16), jax.ShapeDtypeStruct((width, n * Cb), F32),
                       jax.ShapeDtypeStruct((1, n * Cb), F32)],
            input_output_aliases={4: 0}, compiler_params=_cparams(),
        )(x, c, dy, cw, dx_prev)

    dx = lax.empty((T, C), BF16)
    dws, dbs, j0 = [], [], 0
    for k, dy in enumerate(dys):
        dx, dw, db = part(dy, j0, dx, k)
        dws.append(dw)
        dbs.append(db)
        j0 += dy.shape[1] // Cb
    return dx, jnp.concatenate(dws, axis=1), jnp.concatenate(dbs, axis=1)


def _softplus(x):
    e = jnp.exp(-jnp.abs(x))
    return jnp.maximum(x, 0.0) + jnp.where(e < 1e-4, e * (1.0 - 0.5 * e), jnp.log(1.0 + e))


def _cumsum_rows(x):
    n = x.shape[0]
    row = _iota(x.shape, 0)
    k = 1
    while k < n:
        x = x + jnp.where(row >= k, pltpu.roll(x, k, 0), 0.0)
        k *= 2
    return x


def _rev_cumsum_rows(x):
    n = x.shape[0]
    row = _iota(x.shape, 0)
    k = 1
    while k < n:
        x = x + jnp.where(row < n - k, pltpu.roll(x, n - k, 0), 0.0)
        k *= 2
    return x


def _col(x, lane_ids, h):
    return jnp.sum(jnp.where(lane_ids == h, x, 0.0), axis=1, keepdims=True)


def _bdot(a, b, dims):
    return lax.dot_general(a.astype(BF16), b.astype(BF16), (dims, ((), ())), preferred_element_type=F32)


NN = ((1,), (0,))
NT = ((1,), (1,))


def _ssd_dt_prep(dt_pre, dt_bias, a_log, n_heads, name):
    T = dt_pre.shape[0]
    L = SSD_CHUNK

    def body(dtp_ref, bias_ref, alog_ref, dtb_ref, cumb_ref):
        dt = _softplus(dtp_ref[...] + bias_ref[...])
        cum = _cumsum_rows(dt * (-jnp.exp(alog_ref[...])))
        lane = _iota((L, LANES), 1)
        for h in range(n_heads):
            hcols = pl.ds(h * LANES, LANES)
            dtb_ref[:, hcols] = jnp.broadcast_to(_col(dt, lane, h), (L, LANES))
            cumb_ref[:, hcols] = jnp.broadcast_to(_col(cum, lane, h), (L, LANES))

    row = pl.BlockSpec((L, LANES), lambda i: (i, 0))
    vec = pl.BlockSpec((1, LANES), lambda i: (0, 0))
    wide = pl.BlockSpec((L, n_heads * LANES), lambda i: (i, 0))
    return pl.pallas_call(
        body, name=name, grid=(T // L,), in_specs=[row, vec, vec], out_specs=[wide, wide],
        out_shape=[jax.ShapeDtypeStruct((T, n_heads * LANES), F32)] * 2, compiler_params=_cparams(),
    )(dt_pre, dt_bias, a_log)


SCAN_GROUPS_FWD = 4
SCAN_GROUPS_BWD = 2


def _scan_specs(nc, d_inner, hpg, GP):
    L = SSD_CHUNK
    G = SSD_N_GROUPS
    gw = GP * hpg * SSD_HEAD_DIM
    bw = GP * LANES
    assert G % GP == 0 and d_inner % bw == 0
    nb = d_inner // bw
    return dict(
        xs=pl.BlockSpec((L, gw), lambda b, q, c: (b * nc + c, q)),
        B=pl.BlockSpec((L, bw), lambda b, q, c: (b * nc + c, nb + q)),
        C=pl.BlockSpec((L, bw), lambda b, q, c: (b * nc + c, nb + G // GP + q)),
        heads=pl.BlockSpec((L, GP * hpg * LANES), lambda b, q, c: (b * nc + c, q)),
        gvec=pl.BlockSpec((1, gw), lambda b, q, c: (0, q)),
        grp=pl.BlockSpec((L, bw), lambda b, q, c: (b * nc + c, q)),
        st=pl.BlockSpec((1, GP * hpg // 2, SSD_D_STATE, LANES), lambda b, q, c: (b * nc + c, q, 0, 0)),
    )


def _grid_step(dims):
    t, total = 0, 1
    for ax, d in enumerate(dims):
        t = t * d + pl.program_id(ax)
        total *= d
    return t, total


def _scan_fwd(xbc, dtb, cumb, S, d_inner, name, gather=()):
    T = xbc.shape[0]
    Bn = T // S
    L = SSD_CHUNK
    G = SSD_N_GROUPS
    nc = S // L
    hpg = d_inner // SSD_HEAD_DIM // G
    pairs = hpg // 2
    GP = SCAN_GROUPS_FWD
    sp = _scan_specs(nc, d_inner, hpg, GP)
    ng = len(gather)

    def body(*refs):
        xs_ref, b_ref, c_ref, dtb_ref, cumb_ref = refs[:5]
        y_ref, st_ref = refs[5 + ng:7 + ng]
        state = refs[7 + 2 * ng]

        if ng:
            start, forward, finish = _gather_plan(refs[5:5 + ng], refs[7 + ng:7 + 2 * ng], *refs[8 + 2 * ng:])
            t, total = _grid_step((Bn, G // GP, nc))
            pl.when(t == 0)(start)
            pl.when(t == total // 2)(forward)

        @pl.when(pl.program_id(2) == 0)
        def _():
            state[...] = jnp.zeros_like(state)

        tril = _iota((L, L), 1) <= _iota((L, L), 0)
        lane = _iota((L, LANES), 1)
        lane1 = _iota((1, LANES), 1)
        last = _iota((L, 1), 0) == L - 1
        for gi, p in [(gi, p) for gi in range(GP) for p in range(pairs)]:
            if p == 0:
                Bm = b_ref[:, gi * LANES:(gi + 1) * LANES]
                Cm = c_ref[:, gi * LANES:(gi + 1) * LANES]
                Gm = _bdot(Cm, Bm, NT)
                BmT = Bm.T
            sp_ = gi * pairs + p
            cols = pl.ds(sp_ * LANES, LANES)
            S_in = state[sp_]
            st_ref[0, sp_] = S_in
            x_pair = xs_ref[:, cols]
            y_pair = jnp.zeros((L, LANES), F32)
            z_pair = jnp.zeros((L, LANES), F32)
            e_pair = jnp.zeros((L, LANES), F32)
            el_pair = jnp.zeros((1, LANES), F32)
            for k in range(2):
                hcols = pl.ds((gi * hpg + 2 * p + k) * LANES, LANES)
                cum_col = cumb_ref[:, hcols]
                dt_col = dtb_ref[:, hcols]
                decay = jnp.exp(jnp.where(tril, cum_col - cum_col.T, NEG_BIG))
                mk = (lane >= SSD_HEAD_DIM) if k else (lane < SSD_HEAD_DIM)
                xd = jnp.where(mk, x_pair * dt_col, 0.0)
                y_pair = y_pair + _bdot(Gm * decay, xd, NN)
                cum_l = jnp.sum(jnp.where(last, cum_col, 0.0), axis=0, keepdims=True)
                e_pair = jnp.where(mk, jnp.exp(cum_col), e_pair)
                z_pair = z_pair + xd * jnp.exp(cum_l - cum_col)
                el_pair = jnp.where((lane1 >= SSD_HEAD_DIM) if k else (lane1 < SSD_HEAD_DIM), jnp.exp(cum_l), el_pair)
            y_pair = y_pair + e_pair * _bdot(Cm, S_in, NN)
            state[sp_] = S_in * el_pair + _bdot(BmT, z_pair, NN)
            y_ref[:, cols] = y_pair

        if ng:
            pl.when(t == total - 1)(finish)

    outs = pl.pallas_call(
        body, name=name, grid=(Bn, G // GP, nc),
        in_specs=[sp['xs'], sp['B'], sp['C'], sp['heads'], sp['heads']] + [ANY] * ng,
        out_specs=[sp['xs'], sp['st']] + [ANY] * ng,
        out_shape=[jax.ShapeDtypeStruct((T, d_inner), F32),
                   jax.ShapeDtypeStruct((Bn * nc, G * pairs, SSD_D_STATE, LANES), F32)] + _gather_shapes(gather),
        scratch_shapes=[pltpu.VMEM((GP * pairs, SSD_D_STATE, LANES), F32)] + (_gather_sems(ng) if ng else []),
        compiler_params=_cparams(),
    )(xbc, xbc, xbc, dtb, cumb, *gather)
    return outs[0], outs[1], list(outs[2:])


def _scan_bwd(xbc, dtb, cumb, states, dy, d_lane, S, d_inner, name, exchange=()):
    T = xbc.shape[0]
    Bn = T // S
    L = SSD_CHUNK
    G = SSD_N_GROUPS
    nc = S // L
    hpg = d_inner // SSD_HEAD_DIM // G
    pairs = hpg // 2
    GP = SCAN_GROUPS_BWD
    sp = _scan_specs(nc, d_inner, hpg, GP)
    nx = len(exchange)

    def rev(spec):
        im = spec.index_map
        return pl.BlockSpec(spec.block_shape, lambda b, g, c: im(b, g, nc - 1 - c))

    def body(*refs):
        xs_ref, b_ref, c_ref, dtb_ref, cumb_ref, st_ref, dy_ref, dl_ref = refs[:8]
        dxs_ref, db_ref, dc_ref, ddt_ref, da_ref = refs[8 + nx:13 + nx]
        dstate = refs[13 + 2 * nx]
        g = pl.program_id(1)

        if nx:
            start, finish = _chips_plan(refs[8:8 + nx], refs[13 + nx:13 + 2 * nx], *refs[14 + 2 * nx:])
            t, total = _grid_step((Bn, G // GP, nc))
            pl.when(t == 0)(start)

        @pl.when(pl.program_id(2) == 0)
        def _():
            dstate[...] = jnp.zeros_like(dstate)

        tril = _iota((L, L), 1) <= _iota((L, L), 0)
        lane = _iota((L, LANES), 1)
        lane1 = _iota((1, LANES), 1)
        last = _iota((L, 1), 0) == L - 1
        for gi, p in [(gi, p) for gi in range(GP) for p in range(pairs)]:
            gcols = pl.ds(gi * LANES, LANES)
            if p == 0:
                Bm = b_ref[:, gcols]
                Cm = c_ref[:, gcols]
                Gm = _bdot(Cm, Bm, NT)
                CmT = Cm.T
                dG = jnp.zeros((L, L), F32)
                dB = jnp.zeros((L, LANES), F32)
                dC = jnp.zeros((L, LANES), F32)
                dcum = jnp.zeros((L, LANES), F32)
                ddt = jnp.zeros((L, LANES), F32)
            sp_ = gi * pairs + p
            cols = pl.ds(sp_ * LANES, LANES)
            S_in = st_ref[0, sp_]
            dS = dstate[sp_]
            x_pair = xs_ref[:, cols]
            dy_pair = dy_ref[:, cols]
            Q = _bdot(Cm, S_in, NN)
            dZ = _bdot(Bm, dS, NN)
            prod = dS * S_in
            e_pair = jnp.zeros((L, LANES), F32)
            z_pair = jnp.zeros((L, LANES), F32)
            el_pair = jnp.zeros((1, LANES), F32)
            dx_pair = dl_ref[:, cols] * dy_pair
            for k in range(2):
                h = (g * GP + gi) * hpg + 2 * p + k
                hcols = pl.ds((gi * hpg + 2 * p + k) * LANES, LANES)
                cum_col = cumb_ref[:, hcols]
                dt_col = dtb_ref[:, hcols]
                decay = jnp.exp(jnp.where(tril, cum_col - cum_col.T, NEG_BIG))
                mk = (lane >= SSD_HEAD_DIM) if k else (lane < SSD_HEAD_DIM)
                mk1 = (lane1 >= SSD_HEAD_DIM) if k else (lane1 < SSD_HEAD_DIM)
                xd = jnp.where(mk, x_pair * dt_col, 0.0)
                dy_k = jnp.where(mk, dy_pair, 0.0)
                Wm = Gm * decay
                dWm = jnp.where(tril, _bdot(dy_k, xd, NT), 0.0)
                dxd = _bdot(Wm.T, dy_k, NN)
                dseg = dWm * Wm
                dG = dG + dWm * decay
                dcum_col = (jnp.sum(dseg, axis=1, keepdims=True)
                            - jnp.sum(dseg.T, axis=1, keepdims=True))
                cum_l = jnp.sum(jnp.where(last, cum_col, 0.0), axis=0, keepdims=True)
                e_col = jnp.exp(cum_col)
                w_col = jnp.exp(cum_l - cum_col)
                el = jnp.exp(cum_l)
                dcum_col = dcum_col + jnp.sum(dy_k * Q, axis=1, keepdims=True) * e_col
                de_e = jnp.sum(dZ * xd, axis=1, keepdims=True) * w_col
                dcum_col = dcum_col - de_e
                dcum_l = (jnp.sum(de_e, axis=0, keepdims=True)
                          + el * jnp.sum(jnp.sum(jnp.where(mk, prod, 0.0), axis=1, keepdims=True),
                                         axis=0, keepdims=True))
                dcum_col = dcum_col + jnp.where(last, dcum_l, 0.0)
                dxd = dxd + jnp.where(mk, dZ * w_col, 0.0)
                dx_pair = dx_pair + dxd * dt_col
                ddt = jnp.where(lane == h, jnp.sum(dxd * x_pair, axis=1, keepdims=True), ddt)
                dcum = jnp.where(lane == h, dcum_col, dcum)
                e_pair = jnp.where(mk, e_col, e_pair)
                z_pair = z_pair + xd * w_col
                el_pair = jnp.where(mk1, el, el_pair)
            dQ = e_pair * dy_pair
            dC = dC + _bdot(dQ, S_in, NT)
            dB = dB + _bdot(z_pair, dS, NT)
            dstate[sp_] = dS * el_pair + _bdot(CmT, dQ, NN)
            dxs_ref[:, cols] = dx_pair
            if p == pairs - 1:
                dc_ref[:, gcols] = dC + _bdot(dG, Bm, NN)
                db_ref[:, gcols] = dB + _bdot(dG.T, Cm, NN)
                ddt_ref[:, gcols] = ddt
                da_ref[:, gcols] = _rev_cumsum_rows(dcum)

        if nx:
            pl.when(t == total - 1)(finish)

    grp = jax.ShapeDtypeStruct((T, G * LANES), F32)
    outs = pl.pallas_call(
        body, name=name, grid=(Bn, G // GP, nc),
        in_specs=[rev(sp['xs']), rev(sp['B']), rev(sp['C']), rev(sp['heads']), rev(sp['heads']),
                  rev(sp['st']), rev(sp['xs']), sp['gvec']] + [ANY] * nx,
        out_specs=[rev(sp['xs']), rev(sp['grp']), rev(sp['grp']), rev(sp['grp']), rev(sp['grp'])] + [ANY] * nx,
        out_shape=[jax.ShapeDtypeStruct((T, d_inner), F32), grp, grp, grp, grp] + _chips_shapes(exchange),
        scratch_shapes=[pltpu.VMEM((GP * pairs, SSD_D_STATE, LANES), F32)] + (_chips_sems(nx) if nx else []),
        compiler_params=_cparams(),
    )(xbc, xbc, xbc, dtb, cumb, states, dy, d_lane, *exchange)
    return outs[:5], list(outs[5:])


def _dt_bwd(ddt_g, da_g, dt_pre, dt_bias, a_log, name):
    T = dt_pre.shape[0]
    G = SSD_N_GROUPS
    tm = _tile(T, (512, 256, 128))

    def body(ddt_ref, da_ref, dtp_ref, bias_ref, alog_ref, dx_ref, dbias_ref, dal_ref):
        @pl.when(pl.program_id(0) == 0)
        def _():
            dbias_ref[...] = jnp.zeros_like(dbias_ref)
            dal_ref[...] = jnp.zeros_like(dal_ref)

        ddt = ddt_ref[:, 0:LANES]
        da = da_ref[:, 0:LANES]
        for gi in range(1, G):
            ddt = ddt + ddt_ref[:, gi * LANES:(gi + 1) * LANES]
            da = da + da_ref[:, gi * LANES:(gi + 1) * LANES]
        xv = dtp_ref[...] + bias_ref[...]
        A = -jnp.exp(alog_ref[...])
        dx = (ddt + da * A) * _sigmoid(xv)
        dx_ref[...] = dx.astype(BF16)
        dbias_ref[...] += jnp.sum(dx, axis=0, keepdims=True)
        dal_ref[...] += jnp.sum(da * _softplus(xv), axis=0, keepdims=True) * A

    wide = pl.BlockSpec((tm, G * LANES), lambda i: (i, 0))
    row = pl.BlockSpec((tm, LANES), lambda i: (i, 0))
    vec = pl.BlockSpec((1, LANES), lambda i: (0, 0))
    vsh = jax.ShapeDtypeStruct((1, LANES), F32)
    return pl.pallas_call(
        body, name=name, grid=(T // tm,), in_specs=[wide, wide, row, vec, vec], out_specs=[row, vec, vec],
        out_shape=[jax.ShapeDtypeStruct((T, LANES), BF16), vsh, vsh], compiler_params=_cparams(),
    )(ddt_g, da_g, dt_pre, dt_bias, a_log)


def _gate_fwd(y, xbc, z, d_lane, ng, d_inner, name):
    T = y.shape[0]
    G = SSD_N_GROUPS
    gw = d_inner // G
    tm = _tile(T, (512, 256, 128))

    def body(y_ref, x_ref, z_ref, dl_ref, ng_ref, o_ref):
        zv = z_ref[...]
        y2 = (y_ref[...] + dl_ref[...] * x_ref[...]) * (zv * _sigmoid(zv))
        r = lax.rsqrt(jnp.mean(y2 * y2, axis=1, keepdims=True) + LN_EPS)
        o_ref[...] = (y2 * r * ng_ref[...]).astype(BF16)

    blk = pl.BlockSpec((tm, gw), lambda g, i: (i, g))
    vec = pl.BlockSpec((1, gw), lambda g, i: (0, g))
    return pl.pallas_call(
        body, name=name, grid=(G, T // tm), in_specs=[blk, blk, blk, vec, vec], out_specs=blk,
        out_shape=jax.ShapeDtypeStruct((T, d_inner), BF16), compiler_params=_cparams(),
    )(y, xbc, z, d_lane, ng)


def _gate_bwd(dyn, y, xbc, z, d_lane, ng, d_inner, name):
    T = y.shape[0]
    G = SSD_N_GROUPS
    gw = d_inner // G
    tm = _tile(T, (512, 256, 128))

    def body(dyn_ref, y_ref, x_ref, z_ref, dl_ref, ng_ref, dy_ref, dz_ref, dng_ref, ddl_ref):
        @pl.when(pl.program_id(1) == 0)
        def _():
            dng_ref[...] = jnp.zeros_like(dng_ref)
            ddl_ref[...] = jnp.zeros_like(ddl_ref)

        zv = z_ref[...]
        xv = x_ref[...]
        sz = _sigmoid(zv)
        y1 = y_ref[...] + dl_ref[...] * xv
        y2 = y1 * (zv * sz)
        r = lax.rsqrt(jnp.mean(y2 * y2, axis=1, keepdims=True) + LN_EPS)
        dn = dyn_ref[...].astype(F32)
        dng_ref[...] += jnp.sum(dn * y2 * r, axis=0, keepdims=True)
        do = dn * ng_ref[...]
        dy2 = r * do - y2 * (r * r * r) * jnp.mean(do * y2, axis=1, keepdims=True)
        dz_ref[...] = (dy2 * y1 * sz * (1.0 + zv * (1.0 - sz))).astype(BF16)
        dy1 = dy2 * (zv * sz)
        dy_ref[...] = dy1
        ddl_ref[...] += jnp.sum(dy1 * xv, axis=0, keepdims=True)

    blk = pl.BlockSpec((tm, gw), lambda g, i: (i, g))
    vec = pl.BlockSpec((1, gw), lambda g, i: (0, g))
    vsh = jax.ShapeDtypeStruct((1, d_inner), F32)
    return pl.pallas_call(
        body, name=name, grid=(G, T // tm), in_specs=[blk, blk, blk, blk, vec, vec],
        out_specs=[blk, blk, vec, vec],
        out_shape=[jax.ShapeDtypeStruct((T, d_inner), F32), jax.ShapeDtypeStruct((T, d_inner), BF16), vsh, vsh],
        compiler_params=_cparams(),
    )(dyn, y, xbc, z, d_lane, ng)


def _rms_fwd(x, g, name):
    T, R = x.shape
    tm = _tile(T, (512, 256, 128))

    def body(x_ref, g_ref, y_ref):
        xv = x_ref[...]
        y = xv * lax.rsqrt(jnp.mean(xv * xv, axis=1, keepdims=True) + RMS_EPS) * g_ref[...]
        y_ref[...] = y.astype(BF16)

    row = pl.BlockSpec((tm, R), lambda i: (i, 0))
    vec = pl.BlockSpec((1, R), lambda i: (0, 0))
    return pl.pallas_call(
        body, name=name, grid=(T // tm,), in_specs=[row, vec], out_specs=row,
        out_shape=jax.ShapeDtypeStruct((T, R), BF16), compiler_params=_cparams(),
    )(x, g)


def _rms_bwd(dy, x, g, name):
    T, R = x.shape
    tm = _tile(T, (512, 256, 128))

    def body(dy_ref, x_ref, g_ref, dx_ref, dg_ref):
        @pl.when(pl.program_id(0) == 0)
        def _():
            dg_ref[...] = jnp.zeros_like(dg_ref)

        xv = x_ref[...]
        dyv = dy_ref[...]
        r = lax.rsqrt(jnp.mean(xv * xv, axis=1, keepdims=True) + RMS_EPS)
        dg_ref[...] += jnp.sum(dyv * xv * r, axis=0, keepdims=True)
        do = dyv * g_ref[...]
        dx = r * do - xv * (r * r * r) * jnp.mean(do * xv, axis=1, keepdims=True)
        dx_ref[...] = dx.astype(BF16)

    row = pl.BlockSpec((tm, R), lambda i: (i, 0))
    vec = pl.BlockSpec((1, R), lambda i: (0, 0))
    return pl.pallas_call(
        body, name=name, grid=(T // tm,), in_specs=[row, row, vec], out_specs=[row, vec],
        out_shape=[jax.ShapeDtypeStruct((T, R), BF16), jax.ShapeDtypeStruct((1, R), F32)],
        compiler_params=_cparams(),
    )(dy, x, g)


def _swap_halves(x):
    half = MLA_ROPE // 2
    return jnp.where(_iota(x.shape, 1) < half, pltpu.roll(x, LANES - half, 1), pltpu.roll(x, half, 1))


def _rope(x, cc, ss, *, transpose, sum_heads=False, name):
    T, W = x.shape
    nh = W // LANES
    tm = _tile(T, (512, 256, 128))
    n_out = 1 if sum_heads else nh

    def body(x_ref, cc_ref, ss_ref, o_ref):
        ccv = cc_ref[...]
        ssv = ss_ref[...]
        if sum_heads:
            xv = x_ref[:, 0:LANES]
            for hh in range(1, nh):
                xv = xv + x_ref[:, hh * LANES:(hh + 1) * LANES]
            heads = [xv]
        else:
            heads = [x_ref[:, hh * LANES:(hh + 1) * LANES] for hh in range(nh)]
        for hh, xv in enumerate(heads):
            if transpose:
                r = xv * ccv + _swap_halves(xv * ssv)
            else:
                r = xv * ccv + _swap_halves(xv) * ssv
            r = jnp.where(_iota(r.shape, 1) < MLA_ROPE, r, 0.0)
            o_ref[:, hh * LANES:(hh + 1) * LANES] = r.astype(BF16)

    return pl.pallas_call(
        body, name=name, grid=(T // tm,),
        in_specs=[pl.BlockSpec((tm, W), lambda i: (i, 0)), pl.BlockSpec((tm, LANES), lambda i: (i, 0)),
                  pl.BlockSpec((tm, LANES), lambda i: (i, 0))],
        out_specs=pl.BlockSpec((tm, n_out * LANES), lambda i: (i, 0)),
        out_shape=jax.ShapeDtypeStruct((T, n_out * LANES), BF16), compiler_params=_cparams(),
    )(x, cc, ss)


ATT_BLOCK = 512
ATT_SUB = 256
LOG2E = 1.4426950408889634


def _att_geometry(S):
    tb = _tile(S, (ATT_BLOCK, 256))
    return tb, S // tb, tb // ATT_SUB


def _heads_per_step(nh):
    return 2 if nh % 2 == 0 else 1


def _att_scores(a, b, diag, kv_major, off):
    s = _bdot(a, b, NT)
    if diag:
        q_ax, k_ax = (1, 0) if kv_major else (0, 1)
        s = jnp.where(_iota(s.shape, k_ax) <= _iota(s.shape, q_ax) + off, s, NEG_BIG)
    return s


def _attention_fwd(qn, qr, kn, kr, v, S, scale, name):
    T, W = qn.shape
    nh = W // LANES
    Bn = T // S
    tb, n, C = _att_geometry(S)
    pairs = [(i, j) for i in range(n) for j in range(i + 1)]
    it = jnp.asarray([p[0] for p in pairs], jnp.int32)
    jt = jnp.asarray([p[1] for p in pairs], jnp.int32)
    c2 = scale * LOG2E

    HP = _heads_per_step(nh)

    def body(it_ref, jt_ref, qn_ref, qr_ref, kn_ref, kr_ref, v_ref, o_ref, ob_ref, lse_ref, m_sc, l_sc, acc):
        p = pl.program_id(2)
        i = it_ref[p]
        j = jt_ref[p]

        @pl.when(j == 0)
        def _():
            m_sc[...] = jnp.full_like(m_sc, NEG_BIG)
            l_sc[...] = jnp.zeros_like(l_sc)
            acc[...] = jnp.zeros_like(acc)

        def step(diag):
            for hh in range(HP):
                hs = pl.ds(hh * LANES, LANES)
                qc = jnp.concatenate([qn_ref[:, hs], qr_ref[:, hs]], axis=1)
                kc = jnp.concatenate([kn_ref[:, hs], kr_ref[...]], axis=1)
                st = _att_scores(kc, qc, diag, True, 0)
                m_old = m_sc[hh]
                m_new = jnp.maximum(m_old, jnp.max(st, axis=0, keepdims=True))
                pt = jnp.exp2((st - m_new) * c2)
                corr = jnp.exp2((m_old - m_new) * c2)
                l_sc[hh] = corr * l_sc[hh] + jnp.sum(pt, axis=0, keepdims=True)
                acc[hh] = corr * acc[hh] + _bdot(v_ref[:, hs].T, pt, NN)
                m_sc[hh] = m_new

        @pl.when(j < i)
        def _():
            step(False)

        @pl.when(j == i)
        def _():
            step(True)
            for hh in range(HP):
                hs = pl.ds(hh * LANES, LANES)
                ov = (acc[hh] / l_sc[hh]).T
                o_ref[:, hs] = ov
                ob_ref[:, hs] = ov.astype(BF16)
                lse_row = m_sc[hh] * scale + jnp.log(l_sc[hh])
                lse_ref[:, hs] = jnp.broadcast_to(lse_row, (LANES, tb)).T

    qspec = pl.BlockSpec((tb, HP * LANES), lambda b, h, p, it_, jt_: (b * n + it_[p], h))
    kspec = pl.BlockSpec((tb, HP * LANES), lambda b, h, p, it_, jt_: (b * n + jt_[p], h))
    krspec = pl.BlockSpec((tb, LANES), lambda b, h, p, it_, jt_: (b * n + jt_[p], 0))
    return pl.pallas_call(
        body, name=name,
        grid_spec=pltpu.PrefetchScalarGridSpec(
            num_scalar_prefetch=2, grid=(Bn, nh // HP, len(pairs)),
            in_specs=[qspec, qspec, kspec, krspec, kspec], out_specs=[qspec, qspec, qspec],
            scratch_shapes=[pltpu.VMEM((HP, 1, tb), F32), pltpu.VMEM((HP, 1, tb), F32),
                            pltpu.VMEM((HP, LANES, tb), F32)]),
        out_shape=[jax.ShapeDtypeStruct((T, W), F32), jax.ShapeDtypeStruct((T, W), BF16),
                   jax.ShapeDtypeStruct((T, W), F32)],
        compiler_params=_cparams(),
    )(it, jt, qn, qr, kn, kr, v)


def _attention_dq(qn, qr, kn, kr, v, o, do, lse, S, scale, name):
    T, W = qn.shape
    nh = W // LANES
    Bn = T // S
    tb, n, C = _att_geometry(S)
    pairs = [(i, j) for i in range(n) for j in range(i + 1)]
    it = jnp.asarray([p[0] for p in pairs], jnp.int32)
    jt = jnp.asarray([p[1] for p in pairs], jnp.int32)
    c2 = scale * LOG2E
    HP = _heads_per_step(nh)

    def body(it_ref, jt_ref, qn_ref, qr_ref, kn_ref, kr_ref, v_ref, o_ref, do_ref, lse_ref,
             dqn_ref, dqr_ref, acc, di_sc, lse_sc):
        p = pl.program_id(2)
        i = it_ref[p]
        j = jt_ref[p]

        @pl.when(j == 0)
        def _():
            acc[...] = jnp.zeros_like(acc)
            for hh in range(HP):
                hs = pl.ds(hh * LANES, LANES)
                di_sc[hh] = jnp.sum(do_ref[:, hs] * o_ref[:, hs], axis=1, keepdims=True)
                lse_sc[hh] = jnp.max(lse_ref[:, hs], axis=1, keepdims=True) * LOG2E

        def step(diag):
            for hh in range(HP):
                hs = pl.ds(hh * LANES, LANES)
                qc = jnp.concatenate([qn_ref[:, hs], qr_ref[:, hs]], axis=1)
                for c in range(C):
                    r0 = c * ATT_SUB if diag else 0
                    rows = pl.ds(r0, tb - r0)
                    ks = pl.ds(c * ATT_SUB, ATT_SUB)
                    kc = jnp.concatenate([kn_ref[ks, hs], kr_ref[ks, :]], axis=1)
                    s = _att_scores(qc[r0:], kc, diag, False, 0)
                    pr = jnp.exp2(s * c2 - lse_sc[hh, rows, :])
                    dp = _bdot(do_ref[rows, hs], v_ref[ks, hs], NT)
                    ds = pr * (dp - di_sc[hh, rows, :]) * scale
                    acc[hh, rows, :] += _bdot(ds, kc, NN)

        @pl.when(j < i)
        def _():
            step(False)

        @pl.when(j == i)
        def _():
            step(True)
            for hh in range(HP):
                hs = pl.ds(hh * LANES, LANES)
                dqn_ref[:, hs] = acc[hh, :, 0:LANES].astype(BF16)
                dqr_ref[:, hs] = acc[hh, :, LANES:2 * LANES]

    qspec = pl.BlockSpec((tb, HP * LANES), lambda b, h, p, it_, jt_: (b * n + it_[p], h))
    kspec = pl.BlockSpec((tb, HP * LANES), lambda b, h, p, it_, jt_: (b * n + jt_[p], h))
    krspec = pl.BlockSpec((tb, LANES), lambda b, h, p, it_, jt_: (b * n + jt_[p], 0))
    return pl.pallas_call(
        body, name=name,
        grid_spec=pltpu.PrefetchScalarGridSpec(
            num_scalar_prefetch=2, grid=(Bn, nh // HP, len(pairs)),
            in_specs=[qspec, qspec, kspec, krspec, kspec, qspec, qspec, qspec], out_specs=[qspec, qspec],
            scratch_shapes=[pltpu.VMEM((HP, tb, 2 * LANES), F32), pltpu.VMEM((HP, tb, 1), F32),
                            pltpu.VMEM((HP, tb, 1), F32)]),
        out_shape=[jax.ShapeDtypeStruct((T, W), BF16), jax.ShapeDtypeStruct((T, W), F32)],
        compiler_params=_cparams(),
    )(it, jt, qn, qr, kn, kr, v, o, do, lse)


def _attention_dkv(qn, qr, kn, kr, v, o, do, lse, S, scale, name):
    T, W = qn.shape
    nh = W // LANES
    Bn = T // S
    tb, n, C = _att_geometry(S)
    HP = _heads_per_step(nh)
    triples = [(j, h, i) for j in range(n) for h in range(nh // HP) for i in range(j, n)]
    jt = jnp.asarray([t[0] for t in triples], jnp.int32)
    ht = jnp.asarray([t[1] for t in triples], jnp.int32)
    it = jnp.asarray([t[2] for t in triples], jnp.int32)
    c2 = scale * LOG2E

    def as_row(col):
        return jnp.broadcast_to(col, (ATT_SUB, LANES)).T[0:1, :]

    def body(jt_ref, ht_ref, it_ref, qn_ref, qr_ref, kn_ref, kr_ref, v_ref, o_ref, do_ref, lse_ref,
             dkn_ref, dv_ref, dkr_ref, akc, av):
        p = pl.program_id(1)
        j = jt_ref[p]
        h = ht_ref[p]
        i = it_ref[p]

        @pl.when(jnp.logical_and(h == 0, i == j))
        def _():
            dkr_ref[...] = jnp.zeros_like(dkr_ref)

        @pl.when(i == j)
        def _():
            akc[...] = jnp.zeros_like(akc)
            av[...] = jnp.zeros_like(av)

        def step(diag):
            for hh in range(HP):
                hs = pl.ds(hh * LANES, LANES)
                kc = jnp.concatenate([kn_ref[:, hs], kr_ref[...]], axis=1)
                for c in range(C):
                    rows = pl.ds(c * ATT_SUB, ATT_SUB)
                    k1 = (c + 1) * ATT_SUB if diag else tb
                    ks = pl.ds(0, k1)
                    qc = jnp.concatenate([qn_ref[rows, hs], qr_ref[rows, hs]], axis=1)
                    st = _att_scores(kc[:k1], qc, diag, True, c * ATT_SUB)
                    dov = do_ref[rows, hs]
                    lse_row = as_row(jnp.max(lse_ref[rows, hs], axis=1, keepdims=True)) * LOG2E
                    di_row = as_row(jnp.sum(dov * o_ref[rows, hs], axis=1, keepdims=True))
                    pt = jnp.exp2(st * c2 - lse_row)
                    dst = pt * (_bdot(v_ref[ks, hs], dov, NT) - di_row) * scale
                    av[ks, hs] += _bdot(pt, dov, NN)
                    akc[hh, ks, :] += _bdot(dst, qc, NN)

        @pl.when(i > j)
        def _():
            step(False)

        @pl.when(i == j)
        def _():
            step(True)

        @pl.when(i == n - 1)
        def _():
            for hh in range(HP):
                dkn_ref[:, pl.ds(hh * LANES, LANES)] = akc[hh, :, 0:LANES].astype(BF16)
                dkr_ref[...] += akc[hh, :, LANES:2 * LANES]
            dv_ref[...] = av[...].astype(BF16)

    qspec = pl.BlockSpec((tb, HP * LANES), lambda b, p, jt_, ht_, it_: (b * n + it_[p], ht_[p]))
    kspec = pl.BlockSpec((tb, HP * LANES), lambda b, p, jt_, ht_, it_: (b * n + jt_[p], ht_[p]))
    krspec = pl.BlockSpec((tb, LANES), lambda b, p, jt_, ht_, it_: (b * n + jt_[p], 0))
    return pl.pallas_call(
        body, name=name,
        grid_spec=pltpu.PrefetchScalarGridSpec(
            num_scalar_prefetch=3, grid=(Bn, len(triples)),
            in_specs=[qspec, qspec, kspec, krspec, kspec, qspec, qspec, qspec],
            out_specs=[kspec, kspec, krspec],
            scratch_shapes=[pltpu.VMEM((HP, tb, 2 * LANES), F32), pltpu.VMEM((tb, HP * LANES), F32)]),
        out_shape=[jax.ShapeDtypeStruct((T, W), BF16), jax.ShapeDtypeStruct((T, W), BF16),
                   jax.ShapeDtypeStruct((T, LANES), F32)],
        compiler_params=_cparams(),
    )(jt, ht, it, qn, qr, kn, kr, v, o, do, lse)


def _pad_cols(w, n):
    return jnp.pad(w, ((0, 0), (0, n - w.shape[1])))


def _local_step(x, positions, w, target, late=None, reduce_early=None, reduce_last=None):
    w = dict(w)
    Bn, S, D = x.shape
    T = Bn * S
    depth = w['ln_mix_g'].shape[0]
    alpha = (2 * depth) ** 0.25
    d_inner = w['ssd_norm_g'].shape[1]
    n_heads_ssd = d_inner // SSD_HEAD_DIM
    G = SSD_N_GROUPS
    gn = G * SSD_D_STATE
    conv_dim = d_inner + 2 * gn
    hpg = n_heads_ssd // G
    nh = D // MLA_NOPE
    kv_rank = w['kv_norm_g'].shape[0]
    q_rank = w['q_norm_g'].shape[1]
    H = w['ffn_conv_b'].shape[1] // 2
    scale = (MLA_NOPE + MLA_ROPE) ** -0.5
    assert S % SSD_CHUNK == 0 and hpg % 2 == 0 and SSD_D_STATE == LANES and n_heads_ssd <= LANES

    X = x.reshape(T, D)
    tgt = target.reshape(T, D)

    inv_freq = 1.0 / (ROPE_THETA ** (jnp.arange(0, MLA_ROPE, 2, dtype=F32) / MLA_ROPE))
    ang = positions.reshape(T).astype(F32)[:, None] * inv_freq
    cos, sin = jnp.cos(ang), jnp.sin(ang)
    zpad = jnp.zeros((T, LANES - MLA_ROPE), F32)
    cc = jnp.concatenate([cos, cos, zpad], axis=1)
    ss = jnp.concatenate([-sin, sin, zpad], axis=1)

    w_in = w['ssd_in_proj'][0]
    w_z = w_in[:, :d_inner]
    w_xbc = w_in[:, d_inner:d_inner + conv_dim]
    w_dt = _pad_cols(w_in[:, d_inner + conv_dim:], LANES)
    ssd_cw = w['ssd_conv_w'][0]
    ssd_cb = w['ssd_conv_b']
    dt_bias = _pad_cols(w['ssd_dt_bias'], LANES)
    a_log = _pad_cols(w['ssd_A_log'], LANES)
    d_lane = jnp.repeat(w['ssd_D'], SSD_HEAD_DIM, axis=1)
    ssd_ng = w['ssd_norm_g']

    Xb = X.astype(BF16)
    z = _mm(Xb, w_z, name='ssd_in_z')
    xbc_pre = _mm(Xb, w_xbc, name='ssd_in_xbc')
    dt_pre = _mm(Xb, w_dt, name='ssd_in_dt')
    xbc, xbc_conv = _ssd_conv_fwd(xbc_pre, ssd_cw, ssd_cb, S, name='ssd_conv')
    dtb, cumb = _ssd_dt_prep(dt_pre, dt_bias, a_log, n_heads_ssd, name='ssd_dt_prep')
    y_scan, states, gathered = _scan_fwd(xbc, dtb, cumb, S, d_inner, name='ssd_scan',
                                         gather=late[0] if late else ())
    if late:
        w.update(late[1](gathered))

    w_out = w['ssd_out_proj'][0]
    w_kvc = w['kv_down_proj'][:, :kv_rank]
    w_kvr = _pad_cols(w['kv_down_proj'][:, kv_rank:], LANES)
    kv_g = w['kv_norm_g'].reshape(1, kv_rank)
    w_uk = w['kv_up_k']
    w_uv = w['kv_up_v']
    w_qd = w['q_down_proj'][0]
    q_g = w['q_norm_g']
    qu = w['q_up_proj'][0].reshape(q_rank, nh, MLA_NOPE + MLA_ROPE)
    w_qn = qu[:, :, :MLA_NOPE].reshape(q_rank, nh * LANES)
    w_qr = jnp.pad(qu[:, :, MLA_NOPE:], ((0, 0), (0, 0), (0, LANES - MLA_ROPE))).reshape(q_rank, nh * LANES)
    w_ao = w['attn_out_proj'][0]

    def ffn_parts(i):
        return dict(wu=w['ffn_up'][i], cw=w['ffn_conv_w'][i], cb=w['ffn_conv_b'][i:i + 1], wd=w['ffn_down'][i])

    def ln(name, i):
        return w[name][i:i + 1]

    def ffn_fwd(hb, i):
        f = ffn_parts(i)
        u = _mm(hb, f['wu'], name=f'ffn{i}_up')
        a = _ffn_act_fwd(u, f['cw'], f['cb'], S, name=f'ffn{i}_act')
        return _mm(a, f['wd'], name=f'ffn{i}_down'), (u, a)

    yn = _gate_fwd(y_scan, xbc, z, d_lane, ssd_ng, d_inner, name='ssd_gate')
    mix0 = _mm(yn, w_out, name='ssd_out')
    h1, s1, h1b = _ln_fwd(X, mix0, ln('ln_mix_g', 0), ln('ln_mix_b', 0), alpha, name='ln_mix0')
    ff0, ffn0_saved = ffn_fwd(h1b, 0)
    h2, s2, h2b = _ln_fwd(h1, ff0, ln('ln_ffn_g', 0), ln('ln_ffn_b', 0), alpha, name='ln_ffn0')

    ckv = _mm(h2b, w_kvc, name='kv_down_c')
    kr_pre = _mm(h2b, w_kvr, name='kv_down_r')
    ckvn = _rms_fwd(ckv, kv_g, name='kv_norm')
    kr = _rope(kr_pre, cc, ss, transpose=False, name='k_rope')
    kn = _mm(ckvn, w_uk, out_dtype=BF16, name='kv_up_k')
    v = _mm(ckvn, w_uv, out_dtype=BF16, name='kv_up_v')
    cq_pre = _mm(h2b, w_qd, name='q_down')
    cq = _rms_fwd(cq_pre, q_g, name='q_norm')
    qn = _mm(cq, w_qn, out_dtype=BF16, name='q_up_n')
    qr_pre = _mm(cq, w_qr, name='q_up_r')
    qr = _rope(qr_pre, cc, ss, transpose=False, name='q_rope')
    o, ob, lse = _attention_fwd(qn, qr, kn, kr, v, S, scale, name='attn_fwd')
    mix1 = _mm(ob, w_ao, name='attn_out')
    h3, s3, h3b = _ln_fwd(h2, mix1, ln('ln_mix_g', 1), ln('ln_mix_b', 1), alpha, name='ln_mix1')
    ff1, ffn1_saved = ffn_fwd(h3b, 1)
    h4, s4, _ = _ln_fwd(h3, ff1, ln('ln_ffn_g', 1), ln('ln_ffn_b', 1), alpha, name='ln_ffn1')
    sq, dh4 = _loss_head(h4, tgt, name='loss_head')

    g = {}

    def ffn_bwd(ds, dsb, hb_in, saved, i):
        f = ffn_parts(i)
        u, a = saved
        d_wd = _mm(a, dsb, ta=True, name=f'ffn{i}_down_dw')
        da = _mm(dsb, f['wd'], tb=True, out_dtype=BF16, name=f'ffn{i}_down_dx')
        dug, duv, dwg, dwv, dbg, dbv = _ffn_act_bwd(u, da, f['cw'], f['cb'], S, name=f'ffn{i}_act_bwd')
        d_wu = _mm(hb_in, dug, ta=True, name=f'ffn{i}_up_g_dw', into=lax.empty((D, 2 * H), F32))
        d_wu = _mm(hb_in, duv, ta=True, name=f'ffn{i}_up_v_dw', into=d_wu, into_col=H)
        dh = _mm(dug, f['wu'], tb=True, add=ds, add_scale=alpha, name=f'ffn{i}_up_g_dx')
        dh = _mm(duv, f['wu'], tb=True, add=dh, name=f'ffn{i}_up_v_dx', b_koff=H)
        return dh, d_wd, d_wu, jnp.concatenate([dwg, dwv], axis=1), jnp.concatenate([dbg, dbv], axis=1)

    ds4, ds4b, dg_f1, db_f1 = _ln_bwd(dh4, s4, ln('ln_ffn_g', 1), name='ln_ffn1_bwd')
    dh3, d_wd1, d_wu1, d_fcw1, d_fcb1 = ffn_bwd(ds4, ds4b, h3b, ffn1_saved, 1)
    ds3, ds3b, dg_m1, db_m1 = _ln_bwd(dh3, s3, ln('ln_mix_g', 1), name='ln_mix1_bwd')

    g['attn_out_proj'] = _mm(ob, ds3b, ta=True, name='attn_out_dw')[None]
    do = _mm(ds3b, w_ao, tb=True, name='attn_out_dx')
    dqn, dqr = _attention_dq(qn, qr, kn, kr, v, o, do, lse, S, scale, name='attn_bwd_dq')
    dkn, dv, dkr = _attention_dkv(qn, qr, kn, kr, v, o, do, lse, S, scale, name='attn_bwd_dkv')
    dqr_pre = _rope(dqr, cc, ss, transpose=True, name='q_rope_bwd')
    dkr_pre = _rope(dkr, cc, ss, transpose=True, name='k_rope_bwd')

    d_wqn = _mm(cq, dqn, ta=True, name='q_up_n_dw').reshape(q_rank, nh, LANES)
    d_wqr = _mm(cq, dqr_pre, ta=True, name='q_up_r_dw').reshape(q_rank, nh, LANES)[:, :, :MLA_ROPE]
    g['q_up_proj'] = jnp.concatenate([d_wqn, d_wqr], axis=2).reshape(1, q_rank, nh * (MLA_NOPE + MLA_ROPE))
    dcq = _mm(dqn, w_qn, tb=True, name='q_up_n_dx')
    dcq = _mm(dqr_pre, w_qr, tb=True, add=dcq, name='q_up_r_dx')
    dcq_pre, d_qg = _rms_bwd(dcq, cq_pre, q_g, name='q_norm_bwd')
    g['q_norm_g'] = d_qg
    g['q_down_proj'] = _mm(h2b, dcq_pre, ta=True, name='q_down_dw')[None]

    g['kv_up_k'] = _mm(ckvn, dkn, ta=True, name='kv_up_k_dw')
    g['kv_up_v'] = _mm(ckvn, dv, ta=True, name='kv_up_v_dw')
    dckvn = _mm(dkn, w_uk, tb=True, name='kv_up_k_dx')
    dckvn = _mm(dv, w_uv, tb=True, add=dckvn, name='kv_up_v_dx')
    dckv, d_kvg = _rms_bwd(dckvn, ckv, kv_g, name='kv_norm_bwd')
    g['kv_norm_g'] = d_kvg.reshape(kv_rank)
    g['kv_down_proj'] = jnp.concatenate(
        [_mm(h2b, dckv, ta=True, name='kv_down_c_dw'),
         _mm(h2b, dkr_pre, ta=True, name='kv_down_r_dw')[:, :MLA_ROPE]], axis=1)

    dh2 = _mm(dcq_pre, w_qd, tb=True, add=ds3, add_scale=alpha, name='q_down_dx')
    dh2 = _mm(dckv, w_kvc, tb=True, add=dh2, name='kv_down_c_dx')
    dh2 = _mm(dkr_pre, w_kvr, tb=True, add=dh2, name='kv_down_r_dx')

    ds2, ds2b, dg_f0, db_f0 = _ln_bwd(dh2, s2, ln('ln_ffn_g', 0), name='ln_ffn0_bwd')
    dh1, d_wd0, d_wu0, d_fcw0, d_fcb0 = ffn_bwd(ds2, ds2b, h1b, ffn0_saved, 0)
    ds1, ds1b, dg_m0, db_m0 = _ln_bwd(dh1, s1, ln('ln_mix_g', 0), name='ln_mix0_bwd')

    g['ssd_out_proj'] = _mm(yn, ds1b, ta=True, name='ssd_out_dw')[None]
    dyn = _mm(ds1b, w_out, tb=True, out_dtype=BF16, name='ssd_out_dx')
    dy1, dz, d_ng, d_dl = _gate_bwd(dyn, y_scan, xbc, z, d_lane, ssd_ng, d_inner, name='ssd_gate_bwd')
    g['ffn_up'] = jnp.stack([d_wu0, d_wu1])
    g['ffn_down'] = jnp.stack([d_wd0, d_wd1])
    (dxs, dB, dC, ddt_g, da_g), early = _scan_bwd(xbc, dtb, cumb, states, dy1, d_lane, S, d_inner,
                                                  name='ssd_scan_bwd',
                                                  exchange=reduce_early(g) if reduce_early else ())
    ddt_pre, d_bias, d_alog = _dt_bwd(ddt_g, da_g, dt_pre, dt_bias, a_log, name='ssd_dt_bwd')
    dxbc_pre, d_scw, d_scb = _ssd_conv_bwd(xbc_pre, xbc_conv, [dxs, dB, dC], ssd_cw, S, name='ssd_conv_bwd')
    g['ssd_in_proj'] = jnp.concatenate(
        [_mm(Xb, dz, ta=True, name='ssd_in_z_dw'), _mm(Xb, dxbc_pre, ta=True, name='ssd_in_xbc_dw'),
         _mm(Xb, ddt_pre, ta=True, name='ssd_in_dt_dw')[:, :n_heads_ssd]], axis=1)[None]
    g['ssd_conv_w'] = d_scw[None]
    g['ssd_conv_b'] = d_scb
    g['ssd_norm_g'] = d_ng
    g['ffn_conv_w'] = jnp.stack([d_fcw0, d_fcw1])
    dx = _mm(dz, w_z, tb=True, add=ds1, add_scale=alpha, name='ssd_in_z_dx')
    dx = _mm(dxbc_pre, w_xbc, tb=True, add=dx, name='ssd_in_xbc_dx', exchange=reduce_last(g) if reduce_last else ())
    dx, last = dx if reduce_last else (dx, None)
    dx = _mm(ddt_pre, w_dt, tb=True, add=dx, name='ssd_in_dt_dx')

    g['ssd_dt_bias'] = d_bias[:, :n_heads_ssd]
    g['ssd_A_log'] = d_alog[:, :n_heads_ssd]
    g['ssd_D'] = jnp.sum(d_dl.reshape(1, n_heads_ssd, SSD_HEAD_DIM), axis=2)
    g['ffn_conv_b'] = jnp.concatenate([d_fcb0, d_fcb1], axis=0)
    g['ln_mix_g'] = jnp.concatenate([dg_m0, dg_m1], axis=0)
    g['ln_mix_b'] = jnp.concatenate([db_m0, db_m1], axis=0)
    g['ln_ffn_g'] = jnp.concatenate([dg_f0, dg_f1], axis=0)
    g['ln_ffn_b'] = jnp.concatenate([db_f0, db_f1], axis=0)
    return sq, dx.reshape(Bn, S, D), g, early, last


ANY = pl.BlockSpec(memory_space=pl.ANY)


def _all_gather(xs, name):
    n = len(xs)

    def body(*refs):
        start, forward, finish = _gather_plan(refs[:n], refs[n:2 * n], *refs[2 * n:])
        start()
        forward()
        finish()

    return pl.pallas_call(
        body, name=name, out_shape=_gather_shapes(xs), in_specs=[ANY] * n, out_specs=[ANY] * n,
        scratch_shapes=_gather_sems(n),
    )(*xs)


def _gather_shapes(xs):
    return [jax.ShapeDtypeStruct((N_DEV,) + a.shape, a.dtype) for a in xs]


def _gather_sems(n):
    return [pltpu.SemaphoreType.DMA((7 * n,)), pltpu.SemaphoreType.DMA((7 * n,)), pltpu.SemaphoreType.DMA((n,))]


def _gather_plan(x_refs, out_refs, send_sems, recv_sems, local_sems):
    n = len(x_refs)
    x, y, c = lax.axis_index("x"), lax.axis_index("y"), lax.axis_index("c")
    me, sibling = (x, y, c), (x, y, 1 - c)
    chips = [(1 - x, y), (x, 1 - y), (1 - x, 1 - y)]

    def rows(i, px, py, pc):
        return out_refs[i].at[4 * px + 2 * py + pc]

    def copy(i, k, block, to, own=False):
        return pltpu.make_async_remote_copy(
            src_ref=x_refs[i] if own else rows(i, *block), dst_ref=rows(i, *block),
            send_sem=send_sems.at[7 * i + k], recv_sem=recv_sems.at[7 * i + k],
            device_id=to, device_id_type=MESH)

    def mine():
        return [pltpu.make_async_copy(x_refs[i], rows(i, *me), local_sems.at[i]) for i in range(n)]

    def first():
        out = []
        for i in range(n):
            out.append(copy(i, 0, me, sibling, own=True))
            out += [copy(i, 1 + j, me, (*chip, c), own=True) for j, chip in enumerate(chips)]
        return out

    def passed():
        return [copy(i, 4 + j, (*chip, c), sibling) for j, chip in enumerate(chips) for i in range(n)]

    def start():
        for cp in mine() + first():
            cp.start()

    def forward():
        for j, chip in enumerate(chips):
            for i in range(n):
                copy(i, 1 + j, (*chip, c), me).wait_recv()
                copy(i, 4 + j, (*chip, c), sibling).start()

    def finish():
        for i in range(n):
            copy(i, 0, sibling, me).wait_recv()
            for j, chip in enumerate(chips):
                copy(i, 4 + j, (*chip, 1 - c), me).wait_recv()
        for cp in first() + passed():
            cp.wait_send()
        for cp in mine():
            cp.wait()

    return start, forward, finish


def _exchange_sibling(gs, name):
    n = len(gs)

    def body(*refs):
        g_refs, r_refs = refs[:n], refs[n:2 * n]
        send_sems, recv_sems = refs[2 * n:]
        x, y, c = lax.axis_index("x"), lax.axis_index("y"), lax.axis_index("c")
        copies = [pltpu.make_async_remote_copy(
            src_ref=g_refs[i].at[2 * k + (1 - c)], dst_ref=r_refs[i].at[k], send_sem=send_sems.at[4 * i + k],
            recv_sem=recv_sems.at[4 * i + k], device_id=(x, y, 1 - c), device_id_type=MESH)
            for i in range(n) for k in range(4)]
        for cp in copies:
            cp.start()
        for cp in copies:
            cp.wait()

    return pl.pallas_call(
        body, name=name, out_shape=[jax.ShapeDtypeStruct((4,) + g.shape[1:], g.dtype) for g in gs],
        in_specs=[ANY] * n, out_specs=[ANY] * n,
        scratch_shapes=[pltpu.SemaphoreType.DMA((4 * n,)), pltpu.SemaphoreType.DMA((4 * n,))],
    )(*gs)


def _chips_shapes(parts):
    return [jax.ShapeDtypeStruct((3,) + p.shape[1:], p.dtype) for p in parts]


def _chips_sems(n):
    return [pltpu.SemaphoreType.DMA((3 * n,)), pltpu.SemaphoreType.DMA((3 * n,))]


def _chips_plan(p_refs, r_refs, send_sems, recv_sems):
    n = len(p_refs)
    x, y, c = lax.axis_index("x"), lax.axis_index("y"), lax.axis_index("c")
    chips = [(1 - x, y), (x, 1 - y), (1 - x, 1 - y)]

    def copies():
        return [pltpu.make_async_remote_copy(
            src_ref=p_refs[i].at[2 * cx + cy], dst_ref=r_refs[i].at[j], send_sem=send_sems.at[3 * i + j],
            recv_sem=recv_sems.at[3 * i + j], device_id=(cx, cy, c), device_id_type=MESH)
            for i in range(n) for j, (cx, cy) in enumerate(chips)]

    def start():
        for cp in copies():
            cp.start()

    def finish():
        for cp in copies():
            cp.wait()

    return start, finish


def _row_tile(rows, cols):
    want = max(8, (256 * 1024) // cols)
    for t in range(min(rows, want) // 8 * 8, 7, -8):
        if rows % t == 0:
            return t
    return rows


def _add_sibling(gfull, r1, core, name):
    _, rows, lanes = gfull.shape
    tr = _row_tile(rows, lanes)

    def body(c_ref, g_ref, r_ref, o_ref):
        o_ref[...] = g_ref[...] + r_ref[...]

    return pl.pallas_call(
        body, name=name,
        grid_spec=pltpu.PrefetchScalarGridSpec(
            num_scalar_prefetch=1, grid=(4, rows // tr),
            in_specs=[pl.BlockSpec((1, tr, lanes), lambda k, r, c_ref: (2 * k + c_ref[0], r, 0)),
                      pl.BlockSpec((1, tr, lanes), lambda k, r, c_ref: (k, r, 0))],
            out_specs=pl.BlockSpec((1, tr, lanes), lambda k, r, c_ref: (k, r, 0))),
        out_shape=jax.ShapeDtypeStruct((4, rows, lanes), F32), compiler_params=_cparams(),
    )(core, gfull, r1)


def _adam_math(wv, gv, mv, vv):
    m = ADAM_B1 * mv + (1.0 - ADAM_B1) * gv
    v = ADAM_B2 * vv + (1.0 - ADAM_B2) * (gv * gv)
    m_hat = m / (1.0 - ADAM_B1 ** ADAM_STEP)
    v_hat = v / (1.0 - ADAM_B2 ** ADAM_STEP)
    delta = -ADAM_LR * (m_hat / (jnp.sqrt(v_hat) + ADAM_EPS) + ADAM_WD * wv)
    return delta, m, v


def _adam_sharded(part, r2, chip, wts, m, v, name):
    rows, lanes = wts.shape
    tr = _row_tile(rows, lanes)

    def body(k_ref, p_ref, r_ref, w_ref, m_ref, v_ref, g_ref, d_ref, mo_ref, vo_ref):
        gv = p_ref[0] + r_ref[0] + r_ref[1] + r_ref[2]
        delta, mn, vn = _adam_math(w_ref[...], gv, m_ref[...], v_ref[...])
        g_ref[...] = gv
        d_ref[...] = delta
        mo_ref[...] = mn
        vo_ref[...] = vn

    flat = pl.BlockSpec((tr, lanes), lambda r, k_ref: (r, 0))
    sh = jax.ShapeDtypeStruct((rows, lanes), F32)
    return pl.pallas_call(
        body, name=name,
        grid_spec=pltpu.PrefetchScalarGridSpec(
            num_scalar_prefetch=1, grid=(rows // tr,),
            in_specs=[pl.BlockSpec((1, tr, lanes), lambda r, k_ref: (k_ref[0], r, 0)),
                      pl.BlockSpec((3, tr, lanes), lambda r, k_ref: (0, r, 0)), flat, flat, flat],
            out_specs=[flat, flat, flat, flat]),
        out_shape=[sh, sh, sh, sh], compiler_params=_cparams(),
    )(chip, part, r2, wts, m, v)


def _adam_replicated(gall, wts, m, v, name):
    rows, lanes = wts.shape

    def body(ga_ref, w_ref, m_ref, v_ref, g_ref, d_ref, mo_ref, vo_ref):
        gv = ga_ref[0]
        for k in range(1, N_DEV):
            gv = gv + ga_ref[k]
        delta, mn, vn = _adam_math(w_ref[...], gv, m_ref[...], v_ref[...])
        g_ref[...] = gv
        d_ref[...] = delta
        mo_ref[...] = mn
        vo_ref[...] = vn

    sh = jax.ShapeDtypeStruct((rows, lanes), F32)
    return pl.pallas_call(body, name=name, out_shape=[sh, sh, sh, sh], compiler_params=_cparams())(gall, wts, m, v)


def _pack(arrs, dtype, row_mult):
    flat = jnp.concatenate([a.reshape(-1).astype(dtype) for a in arrs])
    n = flat.shape[0]
    unit = LANES * row_mult
    total = -(-n // unit) * unit
    return jnp.pad(flat, (0, total - n)).reshape(total // LANES, LANES)


def _unpack(flat2d, shapes):
    flat = flat2d.reshape(-1)
    out, off = [], 0
    for shp in shapes:
        n = math.prod(shp)
        out.append(flat[off:off + n].reshape(shp))
        off += n
    return out


def _unpack_gathered(gathered, shapes, axes):
    flat = gathered.reshape(N_DEV, -1)
    out, off = [], 0
    for shp, ax in zip(shapes, axes):
        n = math.prod(shp)
        seg = flat[:, off:off + n].reshape((N_DEV,) + tuple(shp))
        out.append(jnp.concatenate([seg[i] for i in range(N_DEV)], axis=ax))
        off += n
    return out


def _pack_chunks(fulls, axes, row_mult):
    per_dev = []
    for full, ax in zip(fulls, axes):
        parts = jnp.stack(jnp.split(full, N_DEV, axis=ax))
        per_dev.append(parts.reshape(N_DEV, -1))
    flat = jnp.concatenate(per_dev, axis=1)
    n = flat.shape[1]
    unit = LANES * row_mult
    total = -(-n // unit) * unit
    return jnp.pad(flat, ((0, 0), (0, total - n))).reshape(N_DEV, total // LANES, LANES)


def kernel(x, positions, ssd_in_proj, ssd_conv_w, ssd_conv_b, ssd_dt_bias, ssd_A_log, ssd_D, ssd_norm_g, ssd_out_proj, kv_down_proj, kv_norm_g, kv_up_k, kv_up_v, q_down_proj, q_norm_g, q_up_proj, attn_out_proj, ffn_up, ffn_conv_w, ffn_conv_b, ffn_down, ln_mix_g, ln_mix_b, ln_ffn_g, ln_ffn_b, loss_target, m_ssd_in_proj, m_ssd_conv_w, m_ssd_conv_b, m_ssd_dt_bias, m_ssd_A_log, m_ssd_D, m_ssd_norm_g, m_ssd_out_proj, m_kv_down_proj, m_kv_norm_g, m_kv_up_k, m_kv_up_v, m_q_down_proj, m_q_norm_g, m_q_up_proj, m_attn_out_proj, m_ffn_up, m_ffn_conv_w, m_ffn_conv_b, m_ffn_down, m_ln_mix_g, m_ln_mix_b, m_ln_ffn_g, m_ln_ffn_b, v_ssd_in_proj, v_ssd_conv_w, v_ssd_conv_b, v_ssd_dt_bias, v_ssd_A_log, v_ssd_D, v_ssd_norm_g, v_ssd_out_proj, v_kv_down_proj, v_kv_norm_g, v_kv_up_k, v_kv_up_v, v_q_down_proj, v_q_norm_g, v_q_up_proj, v_attn_out_proj, v_ffn_up, v_ffn_conv_w, v_ffn_conv_b, v_ffn_down, v_ln_mix_g, v_ln_mix_b, v_ln_ffn_g, v_ln_ffn_b):
    given = dict(locals())
    wl = {n: given[n] for n in WEIGHTS}
    ml = {n: given['m_' + n] for n in WEIGHTS}
    vl = {n: given['v_' + n] for n in WEIGHTS}
    sharded_axis = dict(SHARDED)
    big = [n for n, _ in SHARDED if n not in SHARDED_F32]
    small = [n for n, _ in SHARDED if n in SHARDED_F32]

    def as2d(a):
        return a.reshape(-1, a.shape[-1])

    def to_full(gathered, n):
        shp, ax = wl[n].shape, sharded_axis[n]
        moved = jnp.moveaxis(gathered.reshape((N_DEV,) + shp), 0, ax)
        return moved.reshape(shp[:ax] + (N_DEV * shp[ax],) + shp[ax + 1:])

    def to_chunks(full_grad, n):
        shp, ax = wl[n].shape, sharded_axis[n]
        split = full_grad.reshape(shp[:ax] + (N_DEV, shp[ax]) + shp[ax + 1:])
        return jnp.moveaxis(split, ax, 0).reshape((N_DEV,) + as2d(wl[n]).shape)

    first = [n for n in big if n in GATHERED_FIRST]
    late = [n for n in big if n not in GATHERED_FIRST]
    full = {n: wl[n] for n in REPLICATED}
    gathered = _all_gather([as2d(wl[n]).astype(BF16) for n in first] + [_pack([wl[n] for n in small], F32, 8)],
                           name='gather_weights')
    for n, gat in zip(first, gathered[:-1]):
        full[n] = to_full(gat, n)
    full.update(zip(small, _unpack_gathered(gathered[-1], [wl[n].shape for n in small],
                                            [sharded_axis[n] for n in small])))

    cx, cy, cc = lax.axis_index("x"), lax.axis_index("y"), lax.axis_index("c")
    core = jnp.reshape(cc, (1,)).astype(jnp.int32)
    chip = jnp.reshape(2 * cx + cy, (1,)).astype(jnp.int32)
    early_names = [n for n in big if n not in REDUCED_LAST]
    last_names = [n for n in big if n in REDUCED_LAST]
    parts = {}

    def to_sibling(chunked, names, tag):
        r1 = _exchange_sibling(chunked, name=f'grads_to_sibling_{tag}')
        return [_add_sibling(gf, r, core, name=f'grads_add_sibling_{n}') for n, gf, r in zip(names, chunked, r1)]

    def reduce_early(grads):
        parts.update(zip(early_names, to_sibling([to_chunks(grads[n], n) for n in early_names], early_names, 'early')))
        return [parts[n] for n in early_names]

    def reduce_last(grads):
        chunked = [to_chunks(grads[n], n) for n in last_names]
        chunked.append(_pack_chunks([grads[n] for n in small], [sharded_axis[n] for n in small], 8))
        parts.update(zip(last_names + ['small'], to_sibling(chunked, last_names + ['small'], 'last')))
        return [parts[n] for n in last_names + ['small']]

    sq, grad_x, grads, r2_early, r2_last = _local_step(
        x, positions, full, loss_target,
        late=([as2d(wl[n]).astype(BF16) for n in late], lambda gats: {n: to_full(g, n) for n, g in zip(late, gats)}),
        reduce_early=reduce_early, reduce_last=reduce_last)
    loss = lax.psum(0.5 * jnp.sum(sq) / x.shape[-1], ("x", "y", "c"))

    r2 = dict(zip(early_names, r2_early))
    r2.update(zip(last_names + ['small'], r2_last))
    res = {}
    kinds = ('grad', 'delta', 'new_m', 'new_v')
    for n in big:
        outs = _adam_sharded(parts[n], r2[n], chip, as2d(wl[n]), as2d(ml[n]), as2d(vl[n]), name=f'adamw_{n}')
        for kind, a in zip(kinds, outs):
            res[kind, n] = a.reshape(wl[n].shape)
    outs = _adam_sharded(parts['small'], r2['small'], chip, _pack([wl[n] for n in small], F32, 8),
                         _pack([ml[n] for n in small], F32, 8), _pack([vl[n] for n in small], F32, 8),
                         name='adamw_small_sharded')
    for kind, packed in zip(kinds, outs):
        for n, a in zip(small, _unpack(packed, [wl[n].shape for n in small])):
            res[kind, n] = a

    rep = list(REPLICATED)
    rshapes = [wl[n].shape for n in rep]
    gall, = _all_gather([_pack([grads[n] for n in rep], F32, 8)], name='gather_replicated_grads')
    outs = _adam_replicated(gall, _pack([wl[n] for n in rep], F32, 8), _pack([ml[n] for n in rep], F32, 8),
                            _pack([vl[n] for n in rep], F32, 8), name='adamw_replicated')
    for kind, packed in zip(('grad', 'delta', 'new_m', 'new_v'), outs):
        for n, a in zip(rep, _unpack(packed, rshapes)):
            res[kind, n] = a

    return (loss, grad_x, *[res['grad', n] for n in WEIGHTS], *[res['delta', n] for n in WEIGHTS],
            *[res['new_m', n] for n in WEIGHTS], *[res['new_v', n] for n in WEIGHTS])
```

```python
import functools
import math

import jax
import jax.numpy as jnp
from jax import lax
from jax.experimental import pallas as pl
from jax.experimental.pallas import tpu as pltpu

F32 = jnp.float32
BF16 = jnp.bfloat16
MESH = pl.DeviceIdType.MESH

N_DEV = 8
LANES = 128
MM_TILES = (1024, 1408, 896, 512, 384, 256, 128)
MM_VMEM_BUDGET = 38 * 1024 * 1024
MM_VMEM_LIMIT = 56 * 1024 * 1024
VMEM_LIMIT = 32 * 1024 * 1024
LN_EPS = 1e-5
RMS_EPS = 1e-6
SSD_HEAD_DIM = 64
SSD_N_GROUPS = 8
SSD_D_STATE = 128
SSD_CHUNK = 128
MLA_NOPE = 128
MLA_ROPE = 64
MLA_V = 128
ROPE_THETA = 10000.0
ADAM_LR = 0.001
ADAM_B1 = 0.9
ADAM_B2 = 0.999
ADAM_EPS = 1e-08
ADAM_WD = 0.01
ADAM_STEP = 10
NEG_BIG = -1e30

SHARDED = (('ssd_in_proj', 2), ('ssd_conv_w', 2), ('ssd_conv_b', 1), ('ssd_norm_g', 1), ('ssd_out_proj', 1),
           ('kv_down_proj', 0), ('kv_up_k', 1), ('kv_up_v', 1), ('q_down_proj', 1), ('q_up_proj', 2),
           ('attn_out_proj', 1), ('ffn_up', 2), ('ffn_conv_w', 2), ('ffn_down', 1))
SHARDED_F32 = ('ssd_conv_w', 'ssd_conv_b', 'ssd_norm_g', 'ffn_conv_w')
GATHERED_FIRST = ('ssd_in_proj',)
REDUCED_LAST = ('ssd_in_proj',)
REPLICATED = ('ssd_dt_bias', 'ssd_A_log', 'ssd_D', 'kv_norm_g', 'q_norm_g', 'ffn_conv_b',
              'ln_mix_g', 'ln_mix_b', 'ln_ffn_g', 'ln_ffn_b')
WEIGHTS = ('ssd_in_proj', 'ssd_conv_w', 'ssd_conv_b', 'ssd_dt_bias', 'ssd_A_log', 'ssd_D', 'ssd_norm_g',
           'ssd_out_proj', 'kv_down_proj', 'kv_norm_g', 'kv_up_k', 'kv_up_v', 'q_down_proj', 'q_norm_g',
           'q_up_proj', 'attn_out_proj', 'ffn_up', 'ffn_conv_w', 'ffn_conv_b', 'ffn_down', 'ln_mix_g',
           'ln_mix_b', 'ln_ffn_g', 'ln_ffn_b')


def _cparams(limit=None):
    return pltpu.CompilerParams(vmem_limit_bytes=VMEM_LIMIT if limit is None else limit)


def _tile(n, cands):
    for c in cands:
        if n % c == 0:
            return c
    return n


def _sigmoid(x):
    return 1.0 / (1.0 + jnp.exp(-x))


def _iota(shape, axis):
    return lax.broadcasted_iota(jnp.int32, shape, axis)


def _mm(a, b, *, ta=False, tb=False, add=None, add_scale=1.0, out_dtype=F32, name, b_koff=0, into=None, into_col=0,
        exchange=()):
    if ta:
        K, M = a.shape
    else:
        M, K = a.shape
    if tb:
        N, Kb = b.shape
    else:
        Kb, N = b.shape
    assert b_koff + K <= Kb, (a.shape, b.shape, ta, tb, b_koff)
    tm = _tile(M, MM_TILES)
    tn = _tile(N, MM_TILES)
    tk = K if K <= MM_TILES[0] and b_koff == 0 and Kb == K else _tile(K, MM_TILES)
    nk = K // tk
    assert b_koff % tk == 0 and (into is None or (into.shape[0] == M and into.dtype == out_dtype))
    kb = b_koff // tk

    def vmem_estimate(tm_, tn_):
        ins = 2 * (tm_ * tk * a.dtype.itemsize + tk * tn_ * b.dtype.itemsize)
        outs = tm_ * tn_ * (2 * jnp.dtype(out_dtype).itemsize + 4 + (4 if nk > 1 else 0))
        return ins + outs + (2 * tm_ * tn_ * add.dtype.itemsize if add is not None else 0)

    while vmem_estimate(tm, tn) > MM_VMEM_BUDGET:
        can_m, can_n = tm % (2 * LANES) == 0, tn % (2 * LANES) == 0
        if can_m and (tm >= tn or not can_n):
            tm //= 2
        elif can_n:
            tn //= 2
        else:
            break

    assert into_col % tn == 0
    jb = into_col // tn

    nx = len(exchange)
    n_in = (add is not None) + (into is not None)
    grid = (M // tm, N // tn, nk)

    def body(a_ref, b_ref, *rest):
        add_ref = rest[0] if add is not None else None
        o_ref = rest[n_in + nx]
        acc = rest[n_in + 2 * nx + 1] if nk > 1 else None
        k = pl.program_id(2)
        if nx:
            start, finish_exchange = _chips_plan(rest[n_in:n_in + nx], rest[n_in + nx + 1:n_in + 2 * nx + 1],
                                                 *rest[len(rest) - 2:])
            t, total = _grid_step(grid)
            pl.when(t == 0)(start)

        if ta:
            av = a_ref[...].astype(BF16).T
        else:
            av = a_ref[...].astype(BF16)
        bv = b_ref[...].astype(BF16)
        dn = (((1,), (1 if tb else 0,)), ((), ()))
        part = lax.dot_general(av, bv, dn, preferred_element_type=F32)

        def finish(r):
            if add is not None:
                r = r + add_scale * add_ref[...].astype(F32)
            o_ref[...] = r.astype(out_dtype)

        if nk == 1:
            finish(part)
        else:
            @pl.when(k == 0)
            def _():
                acc[...] = part

            @pl.when(k > 0)
            def _():
                acc[...] += part

            @pl.when(k == nk - 1)
            def _():
                finish(acc[...])

        if nx:
            pl.when(t == total - 1)(finish_exchange)

    a_spec = (pl.BlockSpec((tk, tm), lambda i, j, k: (k, i)) if ta
              else pl.BlockSpec((tm, tk), lambda i, j, k: (i, k)))
    b_spec = (pl.BlockSpec((tn, tk), lambda i, j, k: (j, k + kb)) if tb
              else pl.BlockSpec((tk, tn), lambda i, j, k: (k + kb, j)))
    in_specs = [a_spec, b_spec]
    args = [a, b]
    if add is not None:
        in_specs.append(pl.BlockSpec((tm, tn), lambda i, j, k: (i, j)))
        args.append(add)
    aliases = {}
    if into is not None:
        aliases = {len(args): 0}
        in_specs.append(pl.BlockSpec(memory_space=pl.ANY))
        args.append(into)
    any_spec = pl.BlockSpec(memory_space=pl.ANY)
    outs = pl.pallas_call(
        body, name=name, grid=grid,
        in_specs=in_specs + [any_spec] * nx,
        out_specs=[pl.BlockSpec((tm, tn), lambda i, j, k: (i, j + jb))] + [any_spec] * nx,
        out_shape=[jax.ShapeDtypeStruct((M, N) if into is None else into.shape, out_dtype)] + _chips_shapes(exchange),
        scratch_shapes=([pltpu.VMEM((tm, tn), F32)] if nk > 1 else []) + (_chips_sems(nx) if nx else []),
        input_output_aliases=aliases, compiler_params=_cparams(MM_VMEM_LIMIT),
    )(*args, *exchange)
    return (outs[0], list(outs[1:])) if nx else outs[0]


def _ln_fwd(h, mix, g, b, alpha, name):
    T, D = h.shape
    tm = _tile(T, (512, 256, 128))

    def body(h_ref, m_ref, g_ref, b_ref, y_ref, s_ref, yb_ref):
        s = alpha * h_ref[...] + m_ref[...]
        mu = jnp.mean(s, axis=1, keepdims=True)
        xc = s - mu
        var = jnp.mean(xc * xc, axis=1, keepdims=True)
        y = xc * lax.rsqrt(var + LN_EPS) * g_ref[...] + b_ref[...]
        y_ref[...] = y
        s_ref[...] = s
        yb_ref[...] = y.astype(BF16)

    row = pl.BlockSpec((tm, D), lambda i: (i, 0))
    vec = pl.BlockSpec((1, D), lambda i: (0, 0))
    return pl.pallas_call(
        body, name=name, grid=(T // tm,), in_specs=[row, row, vec, vec], out_specs=[row, row, row],
        out_shape=[jax.ShapeDtypeStruct((T, D), F32)] * 2 + [jax.ShapeDtypeStruct((T, D), BF16)],
        compiler_params=_cparams(),
    )(h, mix, g, b)


def _ln_bwd(dy, s, g, name):
    T, D = s.shape
    tm = _tile(T, (512, 256, 128))

    def body(dy_ref, s_ref, g_ref, ds_ref, dsb_ref, dg_ref, db_ref):
        @pl.when(pl.program_id(0) == 0)
        def _():
            dg_ref[...] = jnp.zeros_like(dg_ref)
            db_ref[...] = jnp.zeros_like(db_ref)

        sv = s_ref[...]
        dyv = dy_ref[...]
        mu = jnp.mean(sv, axis=1, keepdims=True)
        xc = sv - mu
        rstd = lax.rsqrt(jnp.mean(xc * xc, axis=1, keepdims=True) + LN_EPS)
        xhat = xc * rstd
        dxh = dyv * g_ref[...]
        m1 = jnp.mean(dxh, axis=1, keepdims=True)
        m2 = jnp.mean(dxh * xhat, axis=1, keepdims=True)
        ds = rstd * (dxh - m1 - xhat * m2)
        ds_ref[...] = ds
        dsb_ref[...] = ds.astype(BF16)
        dg_ref[...] += jnp.sum(dyv * xhat, axis=0, keepdims=True)
        db_ref[...] += jnp.sum(dyv, axis=0, keepdims=True)

    row = pl.BlockSpec((tm, D), lambda i: (i, 0))
    vec = pl.BlockSpec((1, D), lambda i: (0, 0))
    return pl.pallas_call(
        body, name=name, grid=(T // tm,), in_specs=[row, row, vec], out_specs=[row, row, vec, vec],
        out_shape=[jax.ShapeDtypeStruct((T, D), F32), jax.ShapeDtypeStruct((T, D), BF16),
                   jax.ShapeDtypeStruct((1, D), F32), jax.ShapeDtypeStruct((1, D), F32)],
        compiler_params=_cparams(),
    )(dy, s, g)


def _loss_head(y, target, name):
    T, D = y.shape
    tm = _tile(T, (512, 256, 128))

    def body(y_ref, t_ref, sq_ref, dy_ref):
        @pl.when(pl.program_id(0) == 0)
        def _():
            sq_ref[...] = jnp.zeros_like(sq_ref)

        e = y_ref[...] - t_ref[...]
        sq_ref[...] += jnp.sum(e * e, axis=0, keepdims=True)
        dy_ref[...] = e * (1.0 / D)

    row = pl.BlockSpec((tm, D), lambda i: (i, 0))
    vec = pl.BlockSpec((1, D), lambda i: (0, 0))
    return pl.pallas_call(
        body, name=name, grid=(T // tm,), in_specs=[row, row], out_specs=[vec, row],
        out_shape=[jax.ShapeDtypeStruct((1, D), F32), jax.ShapeDtypeStruct((T, D), F32)],
        compiler_params=_cparams(),
    )(y, target)


def _shift_down(x, j):
    if j == 0:
        return x
    return jnp.where(_iota(x.shape, 0) >= j, pltpu.roll(x, j, 0), 0.0)


def _shift_up(x, j):
    if j == 0:
        return x
    n = x.shape[0]
    return jnp.where(_iota(x.shape, 0) < n - j, pltpu.roll(x, n - j, 0), 0.0)


def _conv_val(x, w_ref, b_ref):
    width = w_ref.shape[0]
    y = b_ref[...] + w_ref[width - 1:width, :] * x
    for j in range(1, width):
        y = y + w_ref[width - 1 - j:width - j, :] * _shift_down(x, j)
    return y


def _conv_bwd_val(x, dc, w_ref, dw_ref, db_ref):
    width = w_ref.shape[0]
    dx = w_ref[width - 1:width, :] * dc
    dw_ref[width - 1:width, :] += jnp.sum(dc * x, axis=0, keepdims=True)
    for j in range(1, width):
        sj = _shift_up(dc, j)
        dx = dx + w_ref[width - 1 - j:width - j, :] * sj
        dw_ref[width - 1 - j:width - j, :] += jnp.sum(sj * x, axis=0, keepdims=True)
    db_ref[...] += jnp.sum(dc, axis=0, keepdims=True)
    return dx


def _ffn_specs(H, S, width, cb):
    cb = _tile(H, (cb, LANES))
    nC = H // cb
    return dict(
        g=pl.BlockSpec((S, cb), lambda j, b: (b, j)), v=pl.BlockSpec((S, cb), lambda j, b: (b, j + nC)),
        wg=pl.BlockSpec((width, cb), lambda j, b: (0, j)), wv=pl.BlockSpec((width, cb), lambda j, b: (0, j + nC)),
        bg=pl.BlockSpec((1, cb), lambda j, b: (0, j)), bv=pl.BlockSpec((1, cb), lambda j, b: (0, j + nC))), nC


def _ffn_act_fwd(u, cw, cb, S, name):
    T, H2 = u.shape
    H = H2 // 2
    sp, nC = _ffn_specs(H, S, cw.shape[0], 2 * LANES)

    def body(ug_ref, uv_ref, wg_ref, wv_ref, bg_ref, bv_ref, a_ref):
        g = _conv_val(ug_ref[...], wg_ref, bg_ref)
        v = _conv_val(uv_ref[...], wv_ref, bv_ref)
        a_ref[...] = (g * _sigmoid(g) * v).astype(BF16)

    return pl.pallas_call(
        body, name=name, grid=(nC, T // S),
        in_specs=[sp['g'], sp['v'], sp['wg'], sp['wv'], sp['bg'], sp['bv']], out_specs=sp['g'],
        out_shape=jax.ShapeDtypeStruct((T, H), BF16), compiler_params=_cparams(),
    )(u, u, cw, cw, cb, cb)


def _ffn_act_bwd(u, da, cw, cb, S, name):
    T, H2 = u.shape
    H = H2 // 2
    width = cw.shape[0]
    sp, nC = _ffn_specs(H, S, width, LANES)

    def body(ug_ref, uv_ref, da_ref, wg_ref, wv_ref, bg_ref, bv_ref,
             dug_ref, duv_ref, dwg_ref, dwv_ref, dbg_ref, dbv_ref):
        @pl.when(pl.program_id(1) == 0)
        def _():
            for r in (dwg_ref, dwv_ref, dbg_ref, dbv_ref):
                r[...] = jnp.zeros_like(r)

        def conv_bwd(x, dc, w_ref, dw_ref, db_ref):
            dx = w_ref[width - 1:width, :] * dc
            for j in range(1, width):
                dx = dx + w_ref[width - 1 - j:width - j, :] * _shift_up(dc, j)
            for j in range(width):
                dw_ref[width - 1 - j:width - j, :] += jnp.sum(dc * _shift_down(x, j), axis=0, keepdims=True)
            db_ref[...] += jnp.sum(dc, axis=0, keepdims=True)
            return dx

        xg = ug_ref[...]
        xv = uv_ref[...]
        g = _conv_val(xg, wg_ref, bg_ref)
        v = _conv_val(xv, wv_ref, bv_ref)
        dav = da_ref[...].astype(F32)
        sg = _sigmoid(g)
        dg = dav * v * sg * (1.0 + g * (1.0 - sg))
        dv = dav * g * sg
        dug_ref[...] = conv_bwd(xg, dg, wg_ref, dwg_ref, dbg_ref).astype(BF16)
        duv_ref[...] = conv_bwd(xv, dv, wv_ref, dwv_ref, dbv_ref).astype(BF16)

    act = jax.ShapeDtypeStruct((T, H), BF16)
    wsh = jax.ShapeDtypeStruct((width, H), F32)
    bsh = jax.ShapeDtypeStruct((1, H), F32)
    return pl.pallas_call(
        body, name=name, grid=(nC, T // S),
        in_specs=[sp['g'], sp['v'], sp['g'], sp['wg'], sp['wv'], sp['bg'], sp['bv']],
        out_specs=[sp['g'], sp['g'], sp['wg'], sp['wg'], sp['bg'], sp['bg']],
        out_shape=[act, act, wsh, wsh, bsh, bsh], compiler_params=_cparams(),
    )(u, u, da, cw, cw, cb, cb)


def _ssd_conv_fwd(x, cw, cb, S, name):
    T, C = x.shape
    Cb = _tile(C, (256, 128))
    width = cw.shape[0]

    def body(x_ref, w_ref, b_ref, y_ref, c_ref):
        c = _conv_val(x_ref[...], w_ref, b_ref)
        c_ref[...] = c
        y_ref[...] = c * _sigmoid(c)

    blk = pl.BlockSpec((S, Cb), lambda j, b: (b, j))
    wblk = pl.BlockSpec((width, Cb), lambda j, b: (0, j))
    bblk = pl.BlockSpec((1, Cb), lambda j, b: (0, j))
    return pl.pallas_call(
        body, name=name, grid=(C // Cb, T // S), in_specs=[blk, wblk, bblk], out_specs=[blk, blk],
        out_shape=[jax.ShapeDtypeStruct((T, C), F32)] * 2, compiler_params=_cparams(),
    )(x, cw, cb)


def _ssd_conv_bwd(x, c, dys, cw, S, name):
    T, C = x.shape
    Cb = _tile(C, (256, 128))
    width = cw.shape[0]
    assert all(d.shape[1] % Cb == 0 for d in dys) and sum(d.shape[1] for d in dys) == C

    def part(dy, j0, dx_prev, k):
        n = dy.shape[1] // Cb

        def body(x_ref, c_ref, dy_ref, w_ref, prev_ref, dx_ref, dw_ref, db_ref):
            @pl.when(pl.program_id(1) == 0)
            def _():
                dw_ref[...] = jnp.zeros_like(dw_ref)
                db_ref[...] = jnp.zeros_like(db_ref)

            cval = c_ref[...]
            sc = _sigmoid(cval)
            dc = dy_ref[...] * sc * (1.0 + cval * (1.0 - sc))
            dx_ref[...] = _conv_bwd_val(x_ref[...], dc, w_ref, dw_ref, db_ref).astype(BF16)

        wide = pl.BlockSpec((S, Cb), lambda j, b: (b, j + j0))
        return pl.pallas_call(
            body, name=f'{name}_{k}', grid=(n, T // S),
            in_specs=[wide, wide, pl.BlockSpec((S, Cb), lambda j, b: (b, j)),
                      pl.BlockSpec((width, Cb), lambda j, b: (0, j + j0)), pl.BlockSpec(memory_space=pl.ANY)],
            out_specs=[wide, pl.BlockSpec((width, Cb), lambda j, b: (0, j)), pl.BlockSpec((1, Cb), lambda j, b: (0, j))],
            out_shape=[jax.ShapeDtypeStruct((T, C), BF16), jax.ShapeDtypeStruct((width, n * Cb), F32),
                       jax.ShapeDtypeStruct((1, n * Cb), F32)],
            input_output_aliases={4: 0}, compiler_params=_cparams(),
        )(x, c, dy, cw, dx_prev)

    dx = lax.empty((T, C), BF16)
    dws, dbs, j0 = [], [], 0
    for k, dy in enumerate(dys):
        dx, dw, db = part(dy, j0, dx, k)
        dws.append(dw)
        dbs.append(db)
        j0 += dy.shape[1] // Cb
    return dx, jnp.concatenate(dws, axis=1), jnp.concatenate(dbs, axis=1)


def _softplus(x):
    e = jnp.exp(-jnp.abs(x))
    return jnp.maximum(x, 0.0) + jnp.where(e < 1e-4, e * (1.0 - 0.5 * e), jnp.log(1.0 + e))


def _cumsum_rows(x):
    n = x.shape[0]
    row = _iota(x.shape, 0)
    k = 1
    while k < n:
        x = x + jnp.where(row >= k, pltpu.roll(x, k, 0), 0.0)
        k *= 2
    return x


def _rev_cumsum_rows(x):
    n = x.shape[0]
    row = _iota(x.shape, 0)
    k = 1
    while k < n:
        x = x + jnp.where(row < n - k, pltpu.roll(x, n - k, 0), 0.0)
        k *= 2
    return x


def _col(x, lane_ids, h):
    return jnp.sum(jnp.where(lane_ids == h, x, 0.0), axis=1, keepdims=True)


def _bdot(a, b, dims):
    return lax.dot_general(a.astype(BF16), b.astype(BF16), (dims, ((), ())), preferred_element_type=F32)


NN = ((1,), (0,))
NT = ((1,), (1,))


def _ssd_dt_prep(dt_pre, dt_bias, a_log, n_heads, name):
    T = dt_pre.shape[0]
    L = SSD_CHUNK

    def body(dtp_ref, bias_ref, alog_ref, dtb_ref, cumb_ref):
        dt = _softplus(dtp_ref[...] + bias_ref[...])
        cum = _cumsum_rows(dt * (-jnp.exp(alog_ref[...])))
        lane = _iota((L, LANES), 1)
        for h in range(n_heads):
            hcols = pl.ds(h * LANES, LANES)
            dtb_ref[:, hcols] = jnp.broadcast_to(_col(dt, lane, h), (L, LANES))
            cumb_ref[:, hcols] = jnp.broadcast_to(_col(cum, lane, h), (L, LANES))

    row = pl.BlockSpec((L, LANES), lambda i: (i, 0))
    vec = pl.BlockSpec((1, LANES), lambda i: (0, 0))
    wide = pl.BlockSpec((L, n_heads * LANES), lambda i: (i, 0))
    return pl.pallas_call(
        body, name=name, grid=(T // L,), in_specs=[row, vec, vec], out_specs=[wide, wide],
        out_shape=[jax.ShapeDtypeStruct((T, n_heads * LANES), F32)] * 2, compiler_params=_cparams(),
    )(dt_pre, dt_bias, a_log)


SCAN_GROUPS_FWD = 4
SCAN_GROUPS_BWD = 2


def _scan_specs(nc, d_inner, hpg, GP):
    L = SSD_CHUNK
    G = SSD_N_GROUPS
    gw = GP * hpg * SSD_HEAD_DIM
    bw = GP * LANES
    assert G % GP == 0 and d_inner % bw == 0
    nb = d_inner // bw
    return dict(
        xs=pl.BlockSpec((L, gw), lambda b, q, c: (b * nc + c, q)),
        B=pl.BlockSpec((L, bw), lambda b, q, c: (b * nc + c, nb + q)),
        C=pl.BlockSpec((L, bw), lambda b, q, c: (b * nc + c, nb + G // GP + q)),
        heads=pl.BlockSpec((L, GP * hpg * LANES), lambda b, q, c: (b * nc + c, q)),
        gvec=pl.BlockSpec((1, gw), lambda b, q, c: (0, q)),
        grp=pl.BlockSpec((L, bw), lambda b, q, c: (b * nc + c, q)),
        st=pl.BlockSpec((1, GP * hpg // 2, SSD_D_STATE, LANES), lambda b, q, c: (b * nc + c, q, 0, 0)),
    )


def _grid_step(dims):
    t, total = 0, 1
    for ax, d in enumerate(dims):
        t = t * d + pl.program_id(ax)
        total *= d
    return t, total


def _scan_fwd(xbc, dtb, cumb, S, d_inner, name, gather=()):
    T = xbc.shape[0]
    Bn = T // S
    L = SSD_CHUNK
    G = SSD_N_GROUPS
    nc = S // L
    hpg = d_inner // SSD_HEAD_DIM // G
    pairs = hpg // 2
    GP = SCAN_GROUPS_FWD
    sp = _scan_specs(nc, d_inner, hpg, GP)
    ng = len(gather)

    def body(*refs):
        xs_ref, b_ref, c_ref, dtb_ref, cumb_ref = refs[:5]
        y_ref, st_ref = refs[5 + ng:7 + ng]
        state = refs[7 + 2 * ng]

        if ng:
            start, forward, finish = _gather_plan(refs[5:5 + ng], refs[7 + ng:7 + 2 * ng], *refs[8 + 2 * ng:])
            t, total = _grid_step((Bn, G // GP, nc))
            pl.when(t == 0)(start)
            pl.when(t == total // 2)(forward)

        @pl.when(pl.program_id(2) == 0)
        def _():
            state[...] = jnp.zeros_like(state)

        tril = _iota((L, L), 1) <= _iota((L, L), 0)
        lane = _iota((L, LANES), 1)
        lane1 = _iota((1, LANES), 1)
        last = _iota((L, 1), 0) == L - 1
        for gi, p in [(gi, p) for gi in range(GP) for p in range(pairs)]:
            if p == 0:
                Bm = b_ref[:, gi * LANES:(gi + 1) * LANES]
                Cm = c_ref[:, gi * LANES:(gi + 1) * LANES]
                Gm = _bdot(Cm, Bm, NT)
                BmT = Bm.T
            sp_ = gi * pairs + p
            cols = pl.ds(sp_ * LANES, LANES)
            S_in = state[sp_]
            st_ref[0, sp_] = S_in
            x_pair = xs_ref[:, cols]
            y_pair = jnp.zeros((L, LANES), F32)
            z_pair = jnp.zeros((L, LANES), F32)
            e_pair = jnp.zeros((L, LANES), F32)
            el_pair = jnp.zeros((1, LANES), F32)
            for k in range(2):
                hcols = pl.ds((gi * hpg + 2 * p + k) * LANES, LANES)
                cum_col = cumb_ref[:, hcols]
                dt_col = dtb_ref[:, hcols]
                decay = jnp.exp(jnp.where(tril, cum_col - cum_col.T, NEG_BIG))
                mk = (lane >= SSD_HEAD_DIM) if k else (lane < SSD_HEAD_DIM)
                xd = jnp.where(mk, x_pair * dt_col, 0.0)
                y_pair = y_pair + _bdot(Gm * decay, xd, NN)
                cum_l = jnp.sum(jnp.where(last, cum_col, 0.0), axis=0, keepdims=True)
                e_pair = jnp.where(mk, jnp.exp(cum_col), e_pair)
                z_pair = z_pair + xd * jnp.exp(cum_l - cum_col)
                el_pair = jnp.where((lane1 >= SSD_HEAD_DIM) if k else (lane1 < SSD_HEAD_DIM), jnp.exp(cum_l), el_pair)
            y_pair = y_pair + e_pair * _bdot(Cm, S_in, NN)
            state[sp_] = S_in * el_pair + _bdot(BmT, z_pair, NN)
            y_ref[:, cols] = y_pair

        if ng:
            pl.when(t == total - 1)(finish)

    outs = pl.pallas_call(
        body, name=name, grid=(Bn, G // GP, nc),
        in_specs=[sp['xs'], sp['B'], sp['C'], sp['heads'], sp['heads']] + [ANY] * ng,
        out_specs=[sp['xs'], sp['st']] + [ANY] * ng,
        out_shape=[jax.ShapeDtypeStruct((T, d_inner), F32),
                   jax.ShapeDtypeStruct((Bn * nc, G * pairs, SSD_D_STATE, LANES), F32)] + _gather_shapes(gather),
        scratch_shapes=[pltpu.VMEM((GP * pairs, SSD_D_STATE, LANES), F32)] + (_gather_sems(ng) if ng else []),
        compiler_params=_cparams(),
    )(xbc, xbc, xbc, dtb, cumb, *gather)
    return outs[0], outs[1], list(outs[2:])


def _scan_bwd(xbc, dtb, cumb, states, dy, d_lane, S, d_inner, name, exchange=()):
    T = xbc.shape[0]
    Bn = T // S
    L = SSD_CHUNK
    G = SSD_N_GROUPS
    nc = S // L
    hpg = d_inner // SSD_HEAD_DIM // G
    pairs = hpg // 2
    GP = SCAN_GROUPS_BWD
    sp = _scan_specs(nc, d_inner, hpg, GP)
    nx = len(exchange)

    def rev(spec):
        im = spec.index_map
        return pl.BlockSpec(spec.block_shape, lambda b, g, c: im(b, g, nc - 1 - c))

    def body(*refs):
        xs_ref, b_ref, c_ref, dtb_ref, cumb_ref, st_ref, dy_ref, dl_ref = refs[:8]
        dxs_ref, db_ref, dc_ref, ddt_ref, da_ref = refs[8 + nx:13 + nx]
        dstate = refs[13 + 2 * nx]
        g = pl.program_id(1)

        if nx:
            start, finish = _chips_plan(refs[8:8 + nx], refs[13 + nx:13 + 2 * nx], *refs[14 + 2 * nx:])
            t, total = _grid_step((Bn, G // GP, nc))
            pl.when(t == 0)(start)

        @pl.when(pl.program_id(2) == 0)
        def _():
            dstate[...] = jnp.zeros_like(dstate)

        tril = _iota((L, L), 1) <= _iota((L, L), 0)
        lane = _iota((L, LANES), 1)
        lane1 = _iota((1, LANES), 1)
        last = _iota((L, 1), 0) == L - 1
        for gi, p in [(gi, p) for gi in range(GP) for p in range(pairs)]:
            gcols = pl.ds(gi * LANES, LANES)
            if p == 0:
                Bm = b_ref[:, gcols]
                Cm = c_ref[:, gcols]
                Gm = _bdot(Cm, Bm, NT)
                CmT = Cm.T
                dG = jnp.zeros((L, L), F32)
                dB = jnp.zeros((L, LANES), F32)
                dC = jnp.zeros((L, LANES), F32)
                dcum = jnp.zeros((L, LANES), F32)
                ddt = jnp.zeros((L, LANES), F32)
            sp_ = gi * pairs + p
            cols = pl.ds(sp_ * LANES, LANES)
            S_in = st_ref[0, sp_]
            dS = dstate[sp_]
            x_pair = xs_ref[:, cols]
            dy_pair = dy_ref[:, cols]
            Q = _bdot(Cm, S_in, NN)
            dZ = _bdot(Bm, dS, NN)
            prod = dS * S_in
            e_pair = jnp.zeros((L, LANES), F32)
            z_pair = jnp.zeros((L, LANES), F32)
            el_pair = jnp.zeros((1, LANES), F32)
            dx_pair = dl_ref[:, cols] * dy_pair
            for k in range(2):
                h = (g * GP + gi) * hpg + 2 * p + k
                hcols = pl.ds((gi * hpg + 2 * p + k) * LANES, LANES)
                cum_col = cumb_ref[:, hcols]
                dt_col = dtb_ref[:, hcols]
                decay = jnp.exp(jnp.where(tril, cum_col - cum_col.T, NEG_BIG))
                mk = (lane >= SSD_HEAD_DIM) if k else (lane < SSD_HEAD_DIM)
                mk1 = (lane1 >= SSD_HEAD_DIM) if k else (lane1 < SSD_HEAD_DIM)
                xd = jnp.where(mk, x_pair * dt_col, 0.0)
                dy_k = jnp.where(mk, dy_pair, 0.0)
                Wm = Gm * decay
                dWm = jnp.where(tril, _bdot(dy_k, xd, NT), 0.0)
                dxd = _bdot(Wm.T, dy_k, NN)
                dseg = dWm * Wm
                dG = dG + dWm * decay
                dcum_col = (jnp.sum(dseg, axis=1, keepdims=True)
                            - jnp.sum(dseg.T, axis=1, keepdims=True))
                cum_l = jnp.sum(jnp.where(last, cum_col, 0.0), axis=0, keepdims=True)
                e_col = jnp.exp(cum_col)
                w_col = jnp.exp(cum_l - cum_col)
                el = jnp.exp(cum_l)
                dcum_col = dcum_col + jnp.sum(dy_k * Q, axis=1, keepdims=True) * e_col
                de_e = jnp.sum(dZ * xd, axis=1, keepdims=True) * w_col
                dcum_col = dcum_col - de_e
                dcum_l = (jnp.sum(de_e, axis=0, keepdims=True)
                          + el * jnp.sum(jnp.sum(jnp.where(mk, prod, 0.0), axis=1, keepdims=True),
                                         axis=0, keepdims=True))
                dcum_col = dcum_col + jnp.where(last, dcum_l, 0.0)
                dxd = dxd + jnp.where(mk, dZ * w_col, 0.0)
                dx_pair = dx_pair + dxd * dt_col
                ddt = jnp.where(lane == h, jnp.sum(dxd * x_pair, axis=1, keepdims=True), ddt)
                dcum = jnp.where(lane == h, dcum_col, dcum)
                e_pair = jnp.where(mk, e_col, e_pair)
                z_pair = z_pair + xd * w_col
                el_pair = jnp.where(mk1, el, el_pair)
            dQ = e_pair * dy_pair
            dC = dC + _bdot(dQ, S_in, NT)
            dB = dB + _bdot(z_pair, dS, NT)
            dstate[sp_] = dS * el_pair + _bdot(CmT, dQ, NN)
            dxs_ref[:, cols] = dx_pair
            if p == pairs - 1:
                dc_ref[:, gcols] = dC + _bdot(dG, Bm, NN)
                db_ref[:, gcols] = dB + _bdot(dG.T, Cm, NN)
                ddt_ref[:, gcols] = ddt
                da_ref[:, gcols] = _rev_cumsum_rows(dcum)

        if nx:
            pl.when(t == total - 1)(finish)

    grp = jax.ShapeDtypeStruct((T, G * LANES), F32)
    outs = pl.pallas_call(
        body, name=name, grid=(Bn, G // GP, nc),
        in_specs=[rev(sp['xs']), rev(sp['B']), rev(sp['C']), rev(sp['heads']), rev(sp['heads']),
                  rev(sp['st']), rev(sp['xs']), sp['gvec']] + [ANY] * nx,
        out_specs=[rev(sp['xs']), rev(sp['grp']), rev(sp['grp']), rev(sp['grp']), rev(sp['grp'])] + [ANY] * nx,
        out_shape=[jax.ShapeDtypeStruct((T, d_inner), F32), grp, grp, grp, grp] + _chips_shapes(exchange),
        scratch_shapes=[pltpu.VMEM((GP * pairs, SSD_D_STATE, LANES), F32)] + (_chips_sems(nx) if nx else []),
        compiler_params=_cparams(),
    )(xbc, xbc, xbc, dtb, cumb, states, dy, d_lane, *exchange)
    return outs[:5], list(outs[5:])


def _dt_bwd(ddt_g, da_g, dt_pre, dt_bias, a_log, name):
    T = dt_pre.shape[0]
    G = SSD_N_GROUPS
    tm = _tile(T, (512, 256, 128))

    def body(ddt_ref, da_ref, dtp_ref, bias_ref, alog_ref, dx_ref, dbias_ref, dal_ref):
        @pl.when(pl.program_id(0) == 0)
        def _():
            dbias_ref[...] = jnp.zeros_like(dbias_ref)
            dal_ref[...] = jnp.zeros_like(dal_ref)

        ddt = ddt_ref[:, 0:LANES]
        da = da_ref[:, 0:LANES]
        for gi in range(1, G):
            ddt = ddt + ddt_ref[:, gi * LANES:(gi + 1) * LANES]
            da = da + da_ref[:, gi * LANES:(gi + 1) * LANES]
        xv = dtp_ref[...] + bias_ref[...]
        A = -jnp.exp(alog_ref[...])
        dx = (ddt + da * A) * _sigmoid(xv)
        dx_ref[...] = dx.astype(BF16)
        dbias_ref[...] += jnp.sum(dx, axis=0, keepdims=True)
        dal_ref[...] += jnp.sum(da * _softplus(xv), axis=0, keepdims=True) * A

    wide = pl.BlockSpec((tm, G * LANES), lambda i: (i, 0))
    row = pl.BlockSpec((tm, LANES), lambda i: (i, 0))
    vec = pl.BlockSpec((1, LANES), lambda i: (0, 0))
    vsh = jax.ShapeDtypeStruct((1, LANES), F32)
    return pl.pallas_call(
        body, name=name, grid=(T // tm,), in_specs=[wide, wide, row, vec, vec], out_specs=[row, vec, vec],
        out_shape=[jax.ShapeDtypeStruct((T, LANES), BF16), vsh, vsh], compiler_params=_cparams(),
    )(ddt_g, da_g, dt_pre, dt_bias, a_log)


def _gate_fwd(y, xbc, z, d_lane, ng, d_inner, name):
    T = y.shape[0]
    G = SSD_N_GROUPS
    gw = d_inner // G
    tm = _tile(T, (1024, 512, 256, 128))

    def body(y_ref, x_ref, z_ref, dl_ref, ng_ref, o_ref):
        zv = z_ref[...]
        y2 = (y_ref[...] + dl_ref[...] * x_ref[...]) * (zv * _sigmoid(zv))
        r = lax.rsqrt(jnp.mean(y2 * y2, axis=1, keepdims=True) + LN_EPS)
        o_ref[...] = (y2 * r * ng_ref[...]).astype(BF16)

    blk = pl.BlockSpec((tm, gw), lambda g, i: (i, g))
    vec = pl.BlockSpec((1, gw), lambda g, i: (0, g))
    return pl.pallas_call(
        body, name=name, grid=(G, T // tm), in_specs=[blk, blk, blk, vec, vec], out_specs=blk,
        out_shape=jax.ShapeDtypeStruct((T, d_inner), BF16), compiler_params=_cparams(),
    )(y, xbc, z, d_lane, ng)


def _gate_bwd(dyn, y, xbc, z, d_lane, ng, d_inner, name, exchange=()):
    T = y.shape[0]
    G = SSD_N_GROUPS
    gw = d_inner // G
    tm = _tile(T, (1024, 512, 256, 128))
    nx = len(exchange)

    def body(*refs):
        dyn_ref, y_ref, x_ref, z_ref, dl_ref, ng_ref = refs[:6]
        dy_ref, dz_ref, dng_ref, ddl_ref = refs[6 + nx:10 + nx]
        if nx:
            start, finish = _sibling_plan(refs[6:6 + nx], refs[10 + nx:10 + 2 * nx], *refs[10 + 2 * nx:])
            t, total = _grid_step((G, T // tm))
            pl.when(t == 0)(start)

        @pl.when(pl.program_id(1) == 0)
        def _():
            dng_ref[...] = jnp.zeros_like(dng_ref)
            ddl_ref[...] = jnp.zeros_like(ddl_ref)

        zv = z_ref[...]
        xv = x_ref[...]
        sz = _sigmoid(zv)
        y1 = y_ref[...] + dl_ref[...] * xv
        y2 = y1 * (zv * sz)
        r = lax.rsqrt(jnp.mean(y2 * y2, axis=1, keepdims=True) + LN_EPS)
        dn = dyn_ref[...].astype(F32)
        dng_ref[...] += jnp.sum(dn * y2 * r, axis=0, keepdims=True)
        do = dn * ng_ref[...]
        dy2 = r * do - y2 * (r * r * r) * jnp.mean(do * y2, axis=1, keepdims=True)
        dz_ref[...] = (dy2 * y1 * sz * (1.0 + zv * (1.0 - sz))).astype(BF16)
        dy1 = dy2 * (zv * sz)
        dy_ref[...] = dy1
        ddl_ref[...] += jnp.sum(dy1 * xv, axis=0, keepdims=True)
        if nx:
            pl.when(t == total - 1)(finish)

    blk = pl.BlockSpec((tm, gw), lambda g, i: (i, g))
    vec = pl.BlockSpec((1, gw), lambda g, i: (0, g))
    vsh = jax.ShapeDtypeStruct((1, d_inner), F32)
    outs = pl.pallas_call(
        body, name=name, grid=(G, T // tm), in_specs=[blk, blk, blk, blk, vec, vec] + [ANY] * nx,
        out_specs=[blk, blk, vec, vec] + [ANY] * nx,
        out_shape=[jax.ShapeDtypeStruct((T, d_inner), F32), jax.ShapeDtypeStruct((T, d_inner), BF16), vsh, vsh]
        + _sibling_shapes(exchange),
        scratch_shapes=_sibling_sems(nx) if nx else [], compiler_params=_cparams(),
    )(dyn, y, xbc, z, d_lane, ng, *exchange)
    return outs[:4], list(outs[4:])


def _rms_fwd(x, g, name):
    T, R = x.shape
    tm = _tile(T, (512, 256, 128))

    def body(x_ref, g_ref, y_ref):
        xv = x_ref[...]
        y = xv * lax.rsqrt(jnp.mean(xv * xv, axis=1, keepdims=True) + RMS_EPS) * g_ref[...]
        y_ref[...] = y.astype(BF16)

    row = pl.BlockSpec((tm, R), lambda i: (i, 0))
    vec = pl.BlockSpec((1, R), lambda i: (0, 0))
    return pl.pallas_call(
        body, name=name, grid=(T // tm,), in_specs=[row, vec], out_specs=row,
        out_shape=jax.ShapeDtypeStruct((T, R), BF16), compiler_params=_cparams(),
    )(x, g)


def _rms_bwd(dy, x, g, name):
    T, R = x.shape
    tm = _tile(T, (512, 256, 128))

    def body(dy_ref, x_ref, g_ref, dx_ref, dg_ref):
        @pl.when(pl.program_id(0) == 0)
        def _():
            dg_ref[...] = jnp.zeros_like(dg_ref)

        xv = x_ref[...]
        dyv = dy_ref[...]
        r = lax.rsqrt(jnp.mean(xv * xv, axis=1, keepdims=True) + RMS_EPS)
        dg_ref[...] += jnp.sum(dyv * xv * r, axis=0, keepdims=True)
        do = dyv * g_ref[...]
        dx = r * do - xv * (r * r * r) * jnp.mean(do * xv, axis=1, keepdims=True)
        dx_ref[...] = dx.astype(BF16)

    row = pl.BlockSpec((tm, R), lambda i: (i, 0))
    vec = pl.BlockSpec((1, R), lambda i: (0, 0))
    return pl.pallas_call(
        body, name=name, grid=(T // tm,), in_specs=[row, row, vec], out_specs=[row, vec],
        out_shape=[jax.ShapeDtypeStruct((T, R), BF16), jax.ShapeDtypeStruct((1, R), F32)],
        compiler_params=_cparams(),
    )(dy, x, g)


def _swap_halves(x):
    half = MLA_ROPE // 2
    return jnp.where(_iota(x.shape, 1) < half, pltpu.roll(x, LANES - half, 1), pltpu.roll(x, half, 1))


def _rope(x, cc, ss, *, transpose, sum_heads=False, name):
    T, W = x.shape
    nh = W // LANES
    tm = _tile(T, (512, 256, 128))
    n_out = 1 if sum_heads else nh

    def body(x_ref, cc_ref, ss_ref, o_ref):
        ccv = cc_ref[...]
        ssv = ss_ref[...]
        if sum_heads:
            xv = x_ref[:, 0:LANES]
            for hh in range(1, nh):
                xv = xv + x_ref[:, hh * LANES:(hh + 1) * LANES]
            heads = [xv]
        else:
            heads = [x_ref[:, hh * LANES:(hh + 1) * LANES] for hh in range(nh)]
        for hh, xv in enumerate(heads):
            if transpose:
                r = xv * ccv + _swap_halves(xv * ssv)
            else:
                r = xv * ccv + _swap_halves(xv) * ssv
            r = jnp.where(_iota(r.shape, 1) < MLA_ROPE, r, 0.0)
            o_ref[:, hh * LANES:(hh + 1) * LANES] = r.astype(BF16)

    return pl.pallas_call(
        body, name=name, grid=(T // tm,),
        in_specs=[pl.BlockSpec((tm, W), lambda i: (i, 0)), pl.BlockSpec((tm, LANES), lambda i: (i, 0)),
                  pl.BlockSpec((tm, LANES), lambda i: (i, 0))],
        out_specs=pl.BlockSpec((tm, n_out * LANES), lambda i: (i, 0)),
        out_shape=jax.ShapeDtypeStruct((T, n_out * LANES), BF16), compiler_params=_cparams(),
    )(x, cc, ss)


ATT_BLOCK = 512
ATT_SUB = 256
LOG2E = 1.4426950408889634


def _att_geometry(S):
    tb = _tile(S, (ATT_BLOCK, 256))
    return tb, S // tb, tb // ATT_SUB


def _heads_per_step(nh):
    return 2 if nh % 2 == 0 else 1


def _att_scores(a, b, diag, kv_major, off):
    s = _bdot(a, b, NT)
    if diag:
        q_ax, k_ax = (1, 0) if kv_major else (0, 1)
        s = jnp.where(_iota(s.shape, k_ax) <= _iota(s.shape, q_ax) + off, s, NEG_BIG)
    return s


def _attention_fwd(qn, qr, kn, kr, v, S, scale, name):
    T, W = qn.shape
    nh = W // LANES
    Bn = T // S
    tb, n, C = _att_geometry(S)
    pairs = [(i, j) for i in range(n) for j in range(i + 1)]
    it = jnp.asarray([p[0] for p in pairs], jnp.int32)
    jt = jnp.asarray([p[1] for p in pairs], jnp.int32)
    c2 = scale * LOG2E

    HP = _heads_per_step(nh)

    def body(it_ref, jt_ref, qn_ref, qr_ref, kn_ref, kr_ref, v_ref, o_ref, ob_ref, lse_ref, m_sc, l_sc, acc):
        p = pl.program_id(2)
        i = it_ref[p]
        j = jt_ref[p]

        @pl.when(j == 0)
        def _():
            m_sc[...] = jnp.full_like(m_sc, NEG_BIG)
            l_sc[...] = jnp.zeros_like(l_sc)
            acc[...] = jnp.zeros_like(acc)

        def step(diag):
            for hh in range(HP):
                hs = pl.ds(hh * LANES, LANES)
                qc = jnp.concatenate([qn_ref[:, hs], qr_ref[:, hs]], axis=1)
                kc = jnp.concatenate([kn_ref[:, hs], kr_ref[...]], axis=1)
                st = _att_scores(kc, qc, diag, True, 0)
                m_old = m_sc[hh]
                m_new = jnp.maximum(m_old, jnp.max(st, axis=0, keepdims=True))
                pt = jnp.exp2((st - m_new) * c2)
                corr = jnp.exp2((m_old - m_new) * c2)
                l_sc[hh] = corr * l_sc[hh] + jnp.sum(pt, axis=0, keepdims=True)
                acc[hh] = corr * acc[hh] + _bdot(v_ref[:, hs].T, pt, NN)
                m_sc[hh] = m_new

        @pl.when(j < i)
        def _():
            step(False)

        @pl.when(j == i)
        def _():
            step(True)
            for hh in range(HP):
                hs = pl.ds(hh * LANES, LANES)
                ov = (acc[hh] / l_sc[hh]).T
                o_ref[:, hs] = ov
                ob_ref[:, hs] = ov.astype(BF16)
                lse_row = m_sc[hh] * scale + jnp.log(l_sc[hh])
                lse_ref[:, hs] = jnp.broadcast_to(lse_row, (LANES, tb)).T

    qspec = pl.BlockSpec((tb, HP * LANES), lambda b, h, p, it_, jt_: (b * n + it_[p], h))
    kspec = pl.BlockSpec((tb, HP * LANES), lambda b, h, p, it_, jt_: (b * n + jt_[p], h))
    krspec = pl.BlockSpec((tb, LANES), lambda b, h, p, it_, jt_: (b * n + jt_[p], 0))
    return pl.pallas_call(
        body, name=name,
        grid_spec=pltpu.PrefetchScalarGridSpec(
            num_scalar_prefetch=2, grid=(Bn, nh // HP, len(pairs)),
            in_specs=[qspec, qspec, kspec, krspec, kspec], out_specs=[qspec, qspec, qspec],
            scratch_shapes=[pltpu.VMEM((HP, 1, tb), F32), pltpu.VMEM((HP, 1, tb), F32),
                            pltpu.VMEM((HP, LANES, tb), F32)]),
        out_shape=[jax.ShapeDtypeStruct((T, W), F32), jax.ShapeDtypeStruct((T, W), BF16),
                   jax.ShapeDtypeStruct((T, W), F32)],
        compiler_params=_cparams(),
    )(it, jt, qn, qr, kn, kr, v)


def _attention_dq(qn, qr, kn, kr, v, o, do, lse, S, scale, name):
    T, W = qn.shape
    nh = W // LANES
    Bn = T // S
    tb, n, C = _att_geometry(S)
    pairs = [(i, j) for i in range(n) for j in range(i + 1)]
    it = jnp.asarray([p[0] for p in pairs], jnp.int32)
    jt = jnp.asarray([p[1] for p in pairs], jnp.int32)
    c2 = scale * LOG2E
    HP = _heads_per_step(nh)

    def body(it_ref, jt_ref, qn_ref, qr_ref, kn_ref, kr_ref, v_ref, o_ref, do_ref, lse_ref,
             dqn_ref, dqr_ref, acc, di_sc, lse_sc):
        p = pl.program_id(2)
        i = it_ref[p]
        j = jt_ref[p]

        @pl.when(j == 0)
        def _():
            acc[...] = jnp.zeros_like(acc)
            for hh in range(HP):
                hs = pl.ds(hh * LANES, LANES)
                di_sc[hh] = jnp.sum(do_ref[:, hs] * o_ref[:, hs], axis=1, keepdims=True)
                lse_sc[hh] = jnp.max(lse_ref[:, hs], axis=1, keepdims=True) * LOG2E

        def step(diag):
            for hh in range(HP):
                hs = pl.ds(hh * LANES, LANES)
                qc = jnp.concatenate([qn_ref[:, hs], qr_ref[:, hs]], axis=1)
                for c in range(C):
                    r0 = c * ATT_SUB if diag else 0
                    rows = pl.ds(r0, tb - r0)
                    ks = pl.ds(c * ATT_SUB, ATT_SUB)
                    kc = jnp.concatenate([kn_ref[ks, hs], kr_ref[ks, :]], axis=1)
                    s = _att_scores(qc[r0:], kc, diag, False, 0)
                    pr = jnp.exp2(s * c2 - lse_sc[hh, rows, :])
                    dp = _bdot(do_ref[rows, hs], v_ref[ks, hs], NT)
                    ds = pr * (dp - di_sc[hh, rows, :]) * scale
                    acc[hh, rows, :] += _bdot(ds, kc, NN)

        @pl.when(j < i)
        def _():
            step(False)

        @pl.when(j == i)
        def _():
            step(True)
            for hh in range(HP):
                hs = pl.ds(hh * LANES, LANES)
                dqn_ref[:, hs] = acc[hh, :, 0:LANES].astype(BF16)
                dqr_ref[:, hs] = acc[hh, :, LANES:2 * LANES]

    qspec = pl.BlockSpec((tb, HP * LANES), lambda b, h, p, it_, jt_: (b * n + it_[p], h))
    kspec = pl.BlockSpec((tb, HP * LANES), lambda b, h, p, it_, jt_: (b * n + jt_[p], h))
    krspec = pl.BlockSpec((tb, LANES), lambda b, h, p, it_, jt_: (b * n + jt_[p], 0))
    return pl.pallas_call(
        body, name=name,
        grid_spec=pltpu.PrefetchScalarGridSpec(
            num_scalar_prefetch=2, grid=(Bn, nh // HP, len(pairs)),
            in_specs=[qspec, qspec, kspec, krspec, kspec, qspec, qspec, qspec], out_specs=[qspec, qspec],
            scratch_shapes=[pltpu.VMEM((HP, tb, 2 * LANES), F32), pltpu.VMEM((HP, tb, 1), F32),
                            pltpu.VMEM((HP, tb, 1), F32)]),
        out_shape=[jax.ShapeDtypeStruct((T, W), BF16), jax.ShapeDtypeStruct((T, W), F32)],
        compiler_params=_cparams(),
    )(it, jt, qn, qr, kn, kr, v, o, do, lse)


def _attention_dkv(qn, qr, kn, kr, v, o, do, lse, S, scale, name):
    T, W = qn.shape
    nh = W // LANES
    Bn = T // S
    tb, n, C = _att_geometry(S)
    HP = _heads_per_step(nh)
    triples = [(j, h, i) for j in range(n) for h in range(nh // HP) for i in range(j, n)]
    jt = jnp.asarray([t[0] for t in triples], jnp.int32)
    ht = jnp.asarray([t[1] for t in triples], jnp.int32)
    it = jnp.asarray([t[2] for t in triples], jnp.int32)
    c2 = scale * LOG2E

    def as_row(col):
        return jnp.broadcast_to(col, (ATT_SUB, LANES)).T[0:1, :]

    def body(jt_ref, ht_ref, it_ref, qn_ref, qr_ref, kn_ref, kr_ref, v_ref, o_ref, do_ref, lse_ref,
             dkn_ref, dv_ref, dkr_ref, akc, av):
        p = pl.program_id(1)
        j = jt_ref[p]
        h = ht_ref[p]
        i = it_ref[p]

        @pl.when(jnp.logical_and(h == 0, i == j))
        def _():
            dkr_ref[...] = jnp.zeros_like(dkr_ref)

        @pl.when(i == j)
        def _():
            akc[...] = jnp.zeros_like(akc)
            av[...] = jnp.zeros_like(av)

        def step(diag):
            for hh in range(HP):
                hs = pl.ds(hh * LANES, LANES)
                kc = jnp.concatenate([kn_ref[:, hs], kr_ref[...]], axis=1)
                for c in range(C):
                    rows = pl.ds(c * ATT_SUB, ATT_SUB)
                    k1 = (c + 1) * ATT_SUB if diag else tb
                    ks = pl.ds(0, k1)
                    qc = jnp.concatenate([qn_ref[rows, hs], qr_ref[rows, hs]], axis=1)
                    st = _att_scores(kc[:k1], qc, diag, True, c * ATT_SUB)
                    dov = do_ref[rows, hs]
                    lse_row = as_row(jnp.max(lse_ref[rows, hs], axis=1, keepdims=True)) * LOG2E
                    di_row = as_row(jnp.sum(dov * o_ref[rows, hs], axis=1, keepdims=True))
                    pt = jnp.exp2(st * c2 - lse_row)
                    dst = pt * (_bdot(v_ref[ks, hs], dov, NT) - di_row) * scale
                    av[ks, hs] += _bdot(pt, dov, NN)
                    akc[hh, ks, :] += _bdot(dst, qc, NN)

        @pl.when(i > j)
        def _():
            step(False)

        @pl.when(i == j)
        def _():
            step(True)

        @pl.when(i == n - 1)
        def _():
            for hh in range(HP):
                dkn_ref[:, pl.ds(hh * LANES, LANES)] = akc[hh, :, 0:LANES].astype(BF16)
                dkr_ref[...] += akc[hh, :, LANES:2 * LANES]
            dv_ref[...] = av[...].astype(BF16)

    qspec = pl.BlockSpec((tb, HP * LANES), lambda b, p, jt_, ht_, it_: (b * n + it_[p], ht_[p]))
    kspec = pl.BlockSpec((tb, HP * LANES), lambda b, p, jt_, ht_, it_: (b * n + jt_[p], ht_[p]))
    krspec = pl.BlockSpec((tb, LANES), lambda b, p, jt_, ht_, it_: (b * n + jt_[p], 0))
    return pl.pallas_call(
        body, name=name,
        grid_spec=pltpu.PrefetchScalarGridSpec(
            num_scalar_prefetch=3, grid=(Bn, len(triples)),
            in_specs=[qspec, qspec, kspec, krspec, kspec, qspec, qspec, qspec],
            out_specs=[kspec, kspec, krspec],
            scratch_shapes=[pltpu.VMEM((HP, tb, 2 * LANES), F32), pltpu.VMEM((tb, HP * LANES), F32)]),
        out_shape=[jax.ShapeDtypeStruct((T, W), BF16), jax.ShapeDtypeStruct((T, W), BF16),
                   jax.ShapeDtypeStruct((T, LANES), F32)],
        compiler_params=_cparams(),
    )(jt, ht, it, qn, qr, kn, kr, v, o, do, lse)


def _pad_cols(w, n):
    return jnp.pad(w, ((0, 0), (0, n - w.shape[1])))


def _local_step(x, positions, w, target, late=None, reduce_early=None, reduce_last=None):
    w = dict(w)
    Bn, S, D = x.shape
    T = Bn * S
    depth = w['ln_mix_g'].shape[0]
    alpha = (2 * depth) ** 0.25
    d_inner = w['ssd_norm_g'].shape[1]
    n_heads_ssd = d_inner // SSD_HEAD_DIM
    G = SSD_N_GROUPS
    gn = G * SSD_D_STATE
    conv_dim = d_inner + 2 * gn
    hpg = n_heads_ssd // G
    nh = D // MLA_NOPE
    kv_rank = w['kv_norm_g'].shape[0]
    q_rank = w['q_norm_g'].shape[1]
    H = w['ffn_conv_b'].shape[1] // 2
    scale = (MLA_NOPE + MLA_ROPE) ** -0.5
    assert S % SSD_CHUNK == 0 and hpg % 2 == 0 and SSD_D_STATE == LANES and n_heads_ssd <= LANES

    X = x.reshape(T, D)
    tgt = target.reshape(T, D)

    inv_freq = 1.0 / (ROPE_THETA ** (jnp.arange(0, MLA_ROPE, 2, dtype=F32) / MLA_ROPE))
    ang = positions.reshape(T).astype(F32)[:, None] * inv_freq
    cos, sin = jnp.cos(ang), jnp.sin(ang)
    zpad = jnp.zeros((T, LANES - MLA_ROPE), F32)
    cc = jnp.concatenate([cos, cos, zpad], axis=1)
    ss = jnp.concatenate([-sin, sin, zpad], axis=1)

    w_in = w['ssd_in_proj'][0]
    w_z = w_in[:, :d_inner]
    w_xbc = w_in[:, d_inner:d_inner + conv_dim]
    w_dt = _pad_cols(w_in[:, d_inner + conv_dim:], LANES)
    ssd_cw = w['ssd_conv_w'][0]
    ssd_cb = w['ssd_conv_b']
    dt_bias = _pad_cols(w['ssd_dt_bias'], LANES)
    a_log = _pad_cols(w['ssd_A_log'], LANES)
    d_lane = jnp.repeat(w['ssd_D'], SSD_HEAD_DIM, axis=1)
    ssd_ng = w['ssd_norm_g']

    Xb = X.astype(BF16)
    z = _mm(Xb, w_z, name='ssd_in_z')
    xbc_pre = _mm(Xb, w_xbc, name='ssd_in_xbc')
    dt_pre = _mm(Xb, w_dt, name='ssd_in_dt')
    xbc, xbc_conv = _ssd_conv_fwd(xbc_pre, ssd_cw, ssd_cb, S, name='ssd_conv')
    dtb, cumb = _ssd_dt_prep(dt_pre, dt_bias, a_log, n_heads_ssd, name='ssd_dt_prep')
    y_scan, states, gathered = _scan_fwd(xbc, dtb, cumb, S, d_inner, name='ssd_scan',
                                         gather=late[0] if late else ())
    if late:
        w.update(late[1](gathered))

    w_out = w['ssd_out_proj'][0]
    w_kvc = w['kv_down_proj'][:, :kv_rank]
    w_kvr = _pad_cols(w['kv_down_proj'][:, kv_rank:], LANES)
    kv_g = w['kv_norm_g'].reshape(1, kv_rank)
    w_uk = w['kv_up_k']
    w_uv = w['kv_up_v']
    w_qd = w['q_down_proj'][0]
    q_g = w['q_norm_g']
    qu = w['q_up_proj'][0].reshape(q_rank, nh, MLA_NOPE + MLA_ROPE)
    w_qn = qu[:, :, :MLA_NOPE].reshape(q_rank, nh * LANES)
    w_qr = jnp.pad(qu[:, :, MLA_NOPE:], ((0, 0), (0, 0), (0, LANES - MLA_ROPE))).reshape(q_rank, nh * LANES)
    w_ao = w['attn_out_proj'][0]

    def ffn_parts(i):
        return dict(wu=w['ffn_up'][i], cw=w['ffn_conv_w'][i], cb=w['ffn_conv_b'][i:i + 1], wd=w['ffn_down'][i])

    def ln(name, i):
        return w[name][i:i + 1]

    def ffn_fwd(hb, i):
        f = ffn_parts(i)
        u = _mm(hb, f['wu'], name=f'ffn{i}_up')
        a = _ffn_act_fwd(u, f['cw'], f['cb'], S, name=f'ffn{i}_act')
        return _mm(a, f['wd'], name=f'ffn{i}_down'), (u, a)

    yn = _gate_fwd(y_scan, xbc, z, d_lane, ssd_ng, d_inner, name='ssd_gate')
    mix0 = _mm(yn, w_out, name='ssd_out')
    h1, s1, h1b = _ln_fwd(X, mix0, ln('ln_mix_g', 0), ln('ln_mix_b', 0), alpha, name='ln_mix0')
    ff0, ffn0_saved = ffn_fwd(h1b, 0)
    h2, s2, h2b = _ln_fwd(h1, ff0, ln('ln_ffn_g', 0), ln('ln_ffn_b', 0), alpha, name='ln_ffn0')

    ckv = _mm(h2b, w_kvc, name='kv_down_c')
    kr_pre = _mm(h2b, w_kvr, name='kv_down_r')
    ckvn = _rms_fwd(ckv, kv_g, name='kv_norm')
    kr = _rope(kr_pre, cc, ss, transpose=False, name='k_rope')
    kn = _mm(ckvn, w_uk, out_dtype=BF16, name='kv_up_k')
    v = _mm(ckvn, w_uv, out_dtype=BF16, name='kv_up_v')
    cq_pre = _mm(h2b, w_qd, name='q_down')
    cq = _rms_fwd(cq_pre, q_g, name='q_norm')
    qn = _mm(cq, w_qn, out_dtype=BF16, name='q_up_n')
    qr_pre = _mm(cq, w_qr, name='q_up_r')
    qr = _rope(qr_pre, cc, ss, transpose=False, name='q_rope')
    o, ob, lse = _attention_fwd(qn, qr, kn, kr, v, S, scale, name='attn_fwd')
    mix1 = _mm(ob, w_ao, name='attn_out')
    h3, s3, h3b = _ln_fwd(h2, mix1, ln('ln_mix_g', 1), ln('ln_mix_b', 1), alpha, name='ln_mix1')
    ff1, ffn1_saved = ffn_fwd(h3b, 1)
    h4, s4, _ = _ln_fwd(h3, ff1, ln('ln_ffn_g', 1), ln('ln_ffn_b', 1), alpha, name='ln_ffn1')
    sq, dh4 = _loss_head(h4, tgt, name='loss_head')

    g = {}

    def ffn_bwd(ds, dsb, hb_in, saved, i):
        f = ffn_parts(i)
        u, a = saved
        d_wd = _mm(a, dsb, ta=True, name=f'ffn{i}_down_dw')
        da = _mm(dsb, f['wd'], tb=True, out_dtype=BF16, name=f'ffn{i}_down_dx')
        dug, duv, dwg, dwv, dbg, dbv = _ffn_act_bwd(u, da, f['cw'], f['cb'], S, name=f'ffn{i}_act_bwd')
        d_wu = _mm(hb_in, dug, ta=True, name=f'ffn{i}_up_g_dw', into=lax.empty((D, 2 * H), F32))
        d_wu = _mm(hb_in, duv, ta=True, name=f'ffn{i}_up_v_dw', into=d_wu, into_col=H)
        dh = _mm(dug, f['wu'], tb=True, add=ds, add_scale=alpha, name=f'ffn{i}_up_g_dx')
        dh = _mm(duv, f['wu'], tb=True, add=dh, name=f'ffn{i}_up_v_dx', b_koff=H)
        return dh, d_wd, d_wu, jnp.concatenate([dwg, dwv], axis=1), jnp.concatenate([dbg, dbv], axis=1)

    ds4, ds4b, dg_f1, db_f1 = _ln_bwd(dh4, s4, ln('ln_ffn_g', 1), name='ln_ffn1_bwd')
    dh3, d_wd1, d_wu1, d_fcw1, d_fcb1 = ffn_bwd(ds4, ds4b, h3b, ffn1_saved, 1)
    ds3, ds3b, dg_m1, db_m1 = _ln_bwd(dh3, s3, ln('ln_mix_g', 1), name='ln_mix1_bwd')

    g['attn_out_proj'] = _mm(ob, ds3b, ta=True, name='attn_out_dw')[None]
    do = _mm(ds3b, w_ao, tb=True, name='attn_out_dx')
    dqn, dqr = _attention_dq(qn, qr, kn, kr, v, o, do, lse, S, scale, name='attn_bwd_dq')
    dkn, dv, dkr = _attention_dkv(qn, qr, kn, kr, v, o, do, lse, S, scale, name='attn_bwd_dkv')
    dqr_pre = _rope(dqr, cc, ss, transpose=True, name='q_rope_bwd')
    dkr_pre = _rope(dkr, cc, ss, transpose=True, name='k_rope_bwd')

    d_wqn = _mm(cq, dqn, ta=True, name='q_up_n_dw').reshape(q_rank, nh, LANES)
    d_wqr = _mm(cq, dqr_pre, ta=True, name='q_up_r_dw').reshape(q_rank, nh, LANES)[:, :, :MLA_ROPE]
    g['q_up_proj'] = jnp.concatenate([d_wqn, d_wqr], axis=2).reshape(1, q_rank, nh * (MLA_NOPE + MLA_ROPE))
    dcq = _mm(dqn, w_qn, tb=True, name='q_up_n_dx')
    dcq = _mm(dqr_pre, w_qr, tb=True, add=dcq, name='q_up_r_dx')
    dcq_pre, d_qg = _rms_bwd(dcq, cq_pre, q_g, name='q_norm_bwd')
    g['q_norm_g'] = d_qg
    g['q_down_proj'] = _mm(h2b, dcq_pre, ta=True, name='q_down_dw')[None]

    g['kv_up_k'] = _mm(ckvn, dkn, ta=True, name='kv_up_k_dw')
    g['kv_up_v'] = _mm(ckvn, dv, ta=True, name='kv_up_v_dw')
    dckvn = _mm(dkn, w_uk, tb=True, name='kv_up_k_dx')
    dckvn = _mm(dv, w_uv, tb=True, add=dckvn, name='kv_up_v_dx')
    dckv, d_kvg = _rms_bwd(dckvn, ckv, kv_g, name='kv_norm_bwd')
    g['kv_norm_g'] = d_kvg.reshape(kv_rank)
    g['kv_down_proj'] = jnp.concatenate(
        [_mm(h2b, dckv, ta=True, name='kv_down_c_dw'),
         _mm(h2b, dkr_pre, ta=True, name='kv_down_r_dw')[:, :MLA_ROPE]], axis=1)

    dh2 = _mm(dcq_pre, w_qd, tb=True, add=ds3, add_scale=alpha, name='q_down_dx')
    dh2 = _mm(dckv, w_kvc, tb=True, add=dh2, name='kv_down_c_dx')
    dh2 = _mm(dkr_pre, w_kvr, tb=True, add=dh2, name='kv_down_r_dx')

    ds2, ds2b, dg_f0, db_f0 = _ln_bwd(dh2, s2, ln('ln_ffn_g', 0), name='ln_ffn0_bwd')
    dh1, d_wd0, d_wu0, d_fcw0, d_fcb0 = ffn_bwd(ds2, ds2b, h1b, ffn0_saved, 0)
    ds1, ds1b, dg_m0, db_m0 = _ln_bwd(dh1, s1, ln('ln_mix_g', 0), name='ln_mix0_bwd')

    g['ssd_out_proj'] = _mm(yn, ds1b, ta=True, name='ssd_out_dw')[None]
    dyn = _mm(ds1b, w_out, tb=True, out_dtype=BF16, name='ssd_out_dx')
    g['ffn_up'] = jnp.stack([d_wu0, d_wu1])
    g['ffn_down'] = jnp.stack([d_wd0, d_wd1])
    early_blocks = reduce_early[0](g) if reduce_early else ()
    (dy1, dz, d_ng, d_dl), from_sibling = _gate_bwd(dyn, y_scan, xbc, z, d_lane, ssd_ng, d_inner,
                                                    name='ssd_gate_bwd', exchange=early_blocks)
    (dxs, dB, dC, ddt_g, da_g), early = _scan_bwd(
        xbc, dtb, cumb, states, dy1, d_lane, S, d_inner, name='ssd_scan_bwd',
        exchange=reduce_early[1](early_blocks, from_sibling) if reduce_early else ())
    ddt_pre, d_bias, d_alog = _dt_bwd(ddt_g, da_g, dt_pre, dt_bias, a_log, name='ssd_dt_bwd')
    dxbc_pre, d_scw, d_scb = _ssd_conv_bwd(xbc_pre, xbc_conv, [dxs, dB, dC], ssd_cw, S, name='ssd_conv_bwd')
    g['ssd_in_proj'] = jnp.concatenate(
        [_mm(Xb, dz, ta=True, name='ssd_in_z_dw'), _mm(Xb, dxbc_pre, ta=True, name='ssd_in_xbc_dw'),
         _mm(Xb, ddt_pre, ta=True, name='ssd_in_dt_dw')[:, :n_heads_ssd]], axis=1)[None]
    g['ssd_conv_w'] = d_scw[None]
    g['ssd_conv_b'] = d_scb
    g['ssd_norm_g'] = d_ng
    g['ffn_conv_w'] = jnp.stack([d_fcw0, d_fcw1])
    dx = _mm(dz, w_z, tb=True, add=ds1, add_scale=alpha, name='ssd_in_z_dx')
    dx = _mm(dxbc_pre, w_xbc, tb=True, add=dx, name='ssd_in_xbc_dx', exchange=reduce_last(g) if reduce_last else ())
    dx, last = dx if reduce_last else (dx, None)
    dx = _mm(ddt_pre, w_dt, tb=True, add=dx, name='ssd_in_dt_dx')

    g['ssd_dt_bias'] = d_bias[:, :n_heads_ssd]
    g['ssd_A_log'] = d_alog[:, :n_heads_ssd]
    g['ssd_D'] = jnp.sum(d_dl.reshape(1, n_heads_ssd, SSD_HEAD_DIM), axis=2)
    g['ffn_conv_b'] = jnp.concatenate([d_fcb0, d_fcb1], axis=0)
    g['ln_mix_g'] = jnp.concatenate([dg_m0, dg_m1], axis=0)
    g['ln_mix_b'] = jnp.concatenate([db_m0, db_m1], axis=0)
    g['ln_ffn_g'] = jnp.concatenate([dg_f0, dg_f1], axis=0)
    g['ln_ffn_b'] = jnp.concatenate([db_f0, db_f1], axis=0)
    return sq, dx.reshape(Bn, S, D), g, early, last


ANY = pl.BlockSpec(memory_space=pl.ANY)


def _all_gather(xs, name):
    n = len(xs)

    def body(*refs):
        start, forward, finish = _gather_plan(refs[:n], refs[n:2 * n], *refs[2 * n:])
        start()
        forward()
        finish()

    return pl.pallas_call(
        body, name=name, out_shape=_gather_shapes(xs), in_specs=[ANY] * n, out_specs=[ANY] * n,
        scratch_shapes=_gather_sems(n),
    )(*xs)


def _gather_shapes(xs):
    return [jax.ShapeDtypeStruct((N_DEV,) + a.shape, a.dtype) for a in xs]


def _gather_sems(n):
    return [pltpu.SemaphoreType.DMA((7 * n,)), pltpu.SemaphoreType.DMA((7 * n,)), pltpu.SemaphoreType.DMA((n,))]


def _gather_plan(x_refs, out_refs, send_sems, recv_sems, local_sems):
    n = len(x_refs)
    x, y, c = lax.axis_index("x"), lax.axis_index("y"), lax.axis_index("c")
    me, sibling = (x, y, c), (x, y, 1 - c)
    chips = [(1 - x, y), (x, 1 - y), (1 - x, 1 - y)]

    def rows(i, px, py, pc):
        return out_refs[i].at[4 * px + 2 * py + pc]

    def copy(i, k, block, to, own=False):
        return pltpu.make_async_remote_copy(
            src_ref=x_refs[i] if own else rows(i, *block), dst_ref=rows(i, *block),
            send_sem=send_sems.at[7 * i + k], recv_sem=recv_sems.at[7 * i + k],
            device_id=to, device_id_type=MESH)

    def mine():
        return [pltpu.make_async_copy(x_refs[i], rows(i, *me), local_sems.at[i]) for i in range(n)]

    def first():
        out = []
        for i in range(n):
            out.append(copy(i, 0, me, sibling, own=True))
            out += [copy(i, 1 + j, me, (*chip, c), own=True) for j, chip in enumerate(chips)]
        return out

    def passed():
        return [copy(i, 4 + j, (*chip, c), sibling) for j, chip in enumerate(chips) for i in range(n)]

    def start():
        for cp in mine() + first():
            cp.start()

    def forward():
        for j, chip in enumerate(chips):
            for i in range(n):
                copy(i, 1 + j, (*chip, c), me).wait_recv()
                copy(i, 4 + j, (*chip, c), sibling).start()

    def finish():
        for i in range(n):
            copy(i, 0, sibling, me).wait_recv()
            for j, chip in enumerate(chips):
                copy(i, 4 + j, (*chip, 1 - c), me).wait_recv()
        for cp in first() + passed():
            cp.wait_send()
        for cp in mine():
            cp.wait()

    return start, forward, finish


def _exchange_sibling(gs, name):
    n = len(gs)

    def body(*refs):
        start, finish = _sibling_plan(refs[:n], refs[n:2 * n], *refs[2 * n:])
        start()
        finish()

    return pl.pallas_call(
        body, name=name, out_shape=_sibling_shapes(gs), in_specs=[ANY] * n, out_specs=[ANY] * n,
        scratch_shapes=_sibling_sems(n),
    )(*gs)


def _sibling_shapes(gs):
    return [jax.ShapeDtypeStruct((4,) + g.shape[1:], g.dtype) for g in gs]


def _sibling_sems(n):
    return [pltpu.SemaphoreType.DMA((4 * n,)), pltpu.SemaphoreType.DMA((4 * n,))]


def _sibling_plan(g_refs, r_refs, send_sems, recv_sems):
    n = len(g_refs)
    x, y, c = lax.axis_index("x"), lax.axis_index("y"), lax.axis_index("c")

    def copies():
        return [pltpu.make_async_remote_copy(
            src_ref=g_refs[i].at[2 * k + (1 - c)], dst_ref=r_refs[i].at[k], send_sem=send_sems.at[4 * i + k],
            recv_sem=recv_sems.at[4 * i + k], device_id=(x, y, 1 - c), device_id_type=MESH)
            for i in range(n) for k in range(4)]

    def start():
        for cp in copies():
            cp.start()

    def finish():
        for cp in copies():
            cp.wait()

    return start, finish


def _chips_shapes(parts):
    return [jax.ShapeDtypeStruct((3,) + p.shape[1:], p.dtype) for p in parts]


def _chips_sems(n):
    return [pltpu.SemaphoreType.DMA((3 * n,)), pltpu.SemaphoreType.DMA((3 * n,))]


def _chips_plan(p_refs, r_refs, send_sems, recv_sems):
    n = len(p_refs)
    x, y, c = lax.axis_index("x"), lax.axis_index("y"), lax.axis_index("c")
    chips = [(1 - x, y), (x, 1 - y), (1 - x, 1 - y)]

    def copies():
        return [pltpu.make_async_remote_copy(
            src_ref=p_refs[i].at[2 * cx + cy], dst_ref=r_refs[i].at[j], send_sem=send_sems.at[3 * i + j],
            recv_sem=recv_sems.at[3 * i + j], device_id=(cx, cy, c), device_id_type=MESH)
            for i in range(n) for j, (cx, cy) in enumerate(chips)]

    def start():
        for cp in copies():
            cp.start()

    def finish():
        for cp in copies():
            cp.wait()

    return start, finish


def _row_tile(rows, cols):
    want = max(8, (256 * 1024) // cols)
    for t in range(min(rows, want) // 8 * 8, 7, -8):
        if rows % t == 0:
            return t
    return rows


def _add_sibling(gfull, r1, core, name):
    _, rows, lanes = gfull.shape
    tr = _row_tile(rows, lanes)

    def body(c_ref, g_ref, r_ref, o_ref):
        o_ref[...] = g_ref[...] + r_ref[...]

    return pl.pallas_call(
        body, name=name,
        grid_spec=pltpu.PrefetchScalarGridSpec(
            num_scalar_prefetch=1, grid=(4, rows // tr),
            in_specs=[pl.BlockSpec((1, tr, lanes), lambda k, r, c_ref: (2 * k + c_ref[0], r, 0)),
                      pl.BlockSpec((1, tr, lanes), lambda k, r, c_ref: (k, r, 0))],
            out_specs=pl.BlockSpec((1, tr, lanes), lambda k, r, c_ref: (k, r, 0))),
        out_shape=jax.ShapeDtypeStruct((4, rows, lanes), F32), compiler_params=_cparams(),
    )(core, gfull, r1)


def _adam_math(wv, gv, mv, vv):
    m = ADAM_B1 * mv + (1.0 - ADAM_B1) * gv
    v = ADAM_B2 * vv + (1.0 - ADAM_B2) * (gv * gv)
    m_hat = m / (1.0 - ADAM_B1 ** ADAM_STEP)
    v_hat = v / (1.0 - ADAM_B2 ** ADAM_STEP)
    delta = -ADAM_LR * (m_hat / (jnp.sqrt(v_hat) + ADAM_EPS) + ADAM_WD * wv)
    return delta, m, v


def _adam_sharded(part, r2, chip, wts, m, v, name):
    rows, lanes = wts.shape
    tr = _row_tile(rows, lanes)

    def body(k_ref, p_ref, r_ref, w_ref, m_ref, v_ref, g_ref, d_ref, mo_ref, vo_ref):
        gv = p_ref[0] + r_ref[0] + r_ref[1] + r_ref[2]
        delta, mn, vn = _adam_math(w_ref[...], gv, m_ref[...], v_ref[...])
        g_ref[...] = gv
        d_ref[...] = delta
        mo_ref[...] = mn
        vo_ref[...] = vn

    flat = pl.BlockSpec((tr, lanes), lambda r, k_ref: (r, 0))
    sh = jax.ShapeDtypeStruct((rows, lanes), F32)
    return pl.pallas_call(
        body, name=name,
        grid_spec=pltpu.PrefetchScalarGridSpec(
            num_scalar_prefetch=1, grid=(rows // tr,),
            in_specs=[pl.BlockSpec((1, tr, lanes), lambda r, k_ref: (k_ref[0], r, 0)),
                      pl.BlockSpec((3, tr, lanes), lambda r, k_ref: (0, r, 0)), flat, flat, flat],
            out_specs=[flat, flat, flat, flat]),
        out_shape=[sh, sh, sh, sh], compiler_params=_cparams(),
    )(chip, part, r2, wts, m, v)


def _adam_replicated(gall, wts, m, v, name):
    rows, lanes = wts.shape

    def body(ga_ref, w_ref, m_ref, v_ref, g_ref, d_ref, mo_ref, vo_ref):
        gv = ga_ref[0]
        for k in range(1, N_DEV):
            gv = gv + ga_ref[k]
        delta, mn, vn = _adam_math(w_ref[...], gv, m_ref[...], v_ref[...])
        g_ref[...] = gv
        d_ref[...] = delta
        mo_ref[...] = mn
        vo_ref[...] = vn

    sh = jax.ShapeDtypeStruct((rows, lanes), F32)
    return pl.pallas_call(body, name=name, out_shape=[sh, sh, sh, sh], compiler_params=_cparams())(gall, wts, m, v)


def _pack(arrs, dtype, row_mult):
    flat = jnp.concatenate([a.reshape(-1).astype(dtype) for a in arrs])
    n = flat.shape[0]
    unit = LANES * row_mult
    total = -(-n // unit) * unit
    return jnp.pad(flat, (0, total - n)).reshape(total // LANES, LANES)


def _unpack(flat2d, shapes):
    flat = flat2d.reshape(-1)
    out, off = [], 0
    for shp in shapes:
        n = math.prod(shp)
        out.append(flat[off:off + n].reshape(shp))
        off += n
    return out


def _unpack_gathered(gathered, shapes, axes):
    flat = gathered.reshape(N_DEV, -1)
    out, off = [], 0
    for shp, ax in zip(shapes, axes):
        n = math.prod(shp)
        seg = flat[:, off:off + n].reshape((N_DEV,) + tuple(shp))
        out.append(jnp.concatenate([seg[i] for i in range(N_DEV)], axis=ax))
        off += n
    return out


def _pack_chunks(fulls, axes, row_mult):
    per_dev = []
    for full, ax in zip(fulls, axes):
        parts = jnp.stack(jnp.split(full, N_DEV, axis=ax))
        per_dev.append(parts.reshape(N_DEV, -1))
    flat = jnp.concatenate(per_dev, axis=1)
    n = flat.shape[1]
    unit = LANES * row_mult
    total = -(-n // unit) * unit
    return jnp.pad(flat, ((0, 0), (0, total - n))).reshape(N_DEV, total // LANES, LANES)


def kernel(x, positions, ssd_in_proj, ssd_conv_w, ssd_conv_b, ssd_dt_bias, ssd_A_log, ssd_D, ssd_norm_g, ssd_out_proj, kv_down_proj, kv_norm_g, kv_up_k, kv_up_v, q_down_proj, q_norm_g, q_up_proj, attn_out_proj, ffn_up, ffn_conv_w, ffn_conv_b, ffn_down, ln_mix_g, ln_mix_b, ln_ffn_g, ln_ffn_b, loss_target, m_ssd_in_proj, m_ssd_conv_w, m_ssd_conv_b, m_ssd_dt_bias, m_ssd_A_log, m_ssd_D, m_ssd_norm_g, m_ssd_out_proj, m_kv_down_proj, m_kv_norm_g, m_kv_up_k, m_kv_up_v, m_q_down_proj, m_q_norm_g, m_q_up_proj, m_attn_out_proj, m_ffn_up, m_ffn_conv_w, m_ffn_conv_b, m_ffn_down, m_ln_mix_g, m_ln_mix_b, m_ln_ffn_g, m_ln_ffn_b, v_ssd_in_proj, v_ssd_conv_w, v_ssd_conv_b, v_ssd_dt_bias, v_ssd_A_log, v_ssd_D, v_ssd_norm_g, v_ssd_out_proj, v_kv_down_proj, v_kv_norm_g, v_kv_up_k, v_kv_up_v, v_q_down_proj, v_q_norm_g, v_q_up_proj, v_attn_out_proj, v_ffn_up, v_ffn_conv_w, v_ffn_conv_b, v_ffn_down, v_ln_mix_g, v_ln_mix_b, v_ln_ffn_g, v_ln_ffn_b):
    given = dict(locals())
    wl = {n: given[n] for n in WEIGHTS}
    ml = {n: given['m_' + n] for n in WEIGHTS}
    vl = {n: given['v_' + n] for n in WEIGHTS}
    sharded_axis = dict(SHARDED)
    big = [n for n, _ in SHARDED if n not in SHARDED_F32]
    small = [n for n, _ in SHARDED if n in SHARDED_F32]

    def as2d(a):
        return a.reshape(-1, a.shape[-1])

    def to_full(gathered, n):
        shp, ax = wl[n].shape, sharded_axis[n]
        moved = jnp.moveaxis(gathered.reshape((N_DEV,) + shp), 0, ax)
        return moved.reshape(shp[:ax] + (N_DEV * shp[ax],) + shp[ax + 1:])

    def to_chunks(full_grad, n):
        shp, ax = wl[n].shape, sharded_axis[n]
        split = full_grad.reshape(shp[:ax] + (N_DEV, shp[ax]) + shp[ax + 1:])
        return jnp.moveaxis(split, ax, 0).reshape((N_DEV,) + as2d(wl[n]).shape)

    first = [n for n in big if n in GATHERED_FIRST]
    late = [n for n in big if n not in GATHERED_FIRST]
    full = {n: wl[n] for n in REPLICATED}
    gathered = _all_gather([as2d(wl[n]).astype(BF16) for n in first] + [_pack([wl[n] for n in small], F32, 8)],
                           name='gather_weights')
    for n, gat in zip(first, gathered[:-1]):
        full[n] = to_full(gat, n)
    full.update(zip(small, _unpack_gathered(gathered[-1], [wl[n].shape for n in small],
                                            [sharded_axis[n] for n in small])))

    cx, cy, cc = lax.axis_index("x"), lax.axis_index("y"), lax.axis_index("c")
    core = jnp.reshape(cc, (1,)).astype(jnp.int32)
    chip = jnp.reshape(2 * cx + cy, (1,)).astype(jnp.int32)
    early_names = [n for n in big if n not in REDUCED_LAST]
    last_names = [n for n in big if n in REDUCED_LAST]
    parts = {}

    def add_sibling(names, chunked, received):
        parts.update((n, _add_sibling(gf, r, core, name=f'grads_add_sibling_{n}'))
                     for n, gf, r in zip(names, chunked, received))
        return [parts[n] for n in names]

    def early_blocks(grads):
        return [to_chunks(grads[n], n) for n in early_names]

    def early_parts(chunked, received):
        return add_sibling(early_names, chunked, received)

    def reduce_last(grads):
        chunked = [to_chunks(grads[n], n) for n in last_names]
        chunked.append(_pack_chunks([grads[n] for n in small], [sharded_axis[n] for n in small], 8))
        return add_sibling(last_names + ['small'], chunked, _exchange_sibling(chunked, name='grads_to_sibling_last'))

    sq, grad_x, grads, r2_early, r2_last = _local_step(
        x, positions, full, loss_target,
        late=([as2d(wl[n]).astype(BF16) for n in late], lambda gats: {n: to_full(g, n) for n, g in zip(late, gats)}),
        reduce_early=(early_blocks, early_parts), reduce_last=reduce_last)
    loss = lax.psum(0.5 * jnp.sum(sq) / x.shape[-1], ("x", "y", "c"))

    r2 = dict(zip(early_names, r2_early))
    r2.update(zip(last_names + ['small'], r2_last))
    res = {}
    kinds = ('grad', 'delta', 'new_m', 'new_v')
    for n in big:
        outs = _adam_sharded(parts[n], r2[n], chip, as2d(wl[n]), as2d(ml[n]), as2d(vl[n]), name=f'adamw_{n}')
        for kind, a in zip(kinds, outs):
            res[kind, n] = a.reshape(wl[n].shape)
    outs = _adam_sharded(parts['small'], r2['small'], chip, _pack([wl[n] for n in small], F32, 8),
                         _pack([ml[n] for n in small], F32, 8), _pack([vl[n] for n in small], F32, 8),
                         name='adamw_small_sharded')
    for kind, packed in zip(kinds, outs):
        for n, a in zip(small, _unpack(packed, [wl[n].shape for n in small])):
            res[kind, n] = a

    rep = list(REPLICATED)
    rshapes = [wl[n].shape for n in rep]
    gall, = _all_gather([_pack([grads[n] for n in rep], F32, 8)], name='gather_replicated_grads')
    outs = _adam_replicated(gall, _pack([wl[n] for n in rep], F32, 8), _pack([ml[n] for n in rep], F32, 8),
                            _pack([vl[n] for n in rep], F32, 8), name='adamw_replicated')
    for kind, packed in zip(('grad', 'delta', 'new_m', 'new_v'), outs):
        for n, a in zip(rep, _unpack(packed, rshapes)):
            res[kind, n] = a

    return (loss, grad_x, *[res['grad', n] for n in WEIGHTS], *[res['delta', n] for n in WEIGHTS],
            *[res['new_m', n] for n in WEIGHTS], *[res['new_v', n] for n in WEIGHTS])
```

```python
import math

import jax
import jax.numpy as jnp
from jax import lax
from jax.experimental import pallas as pl
from jax.experimental.pallas import tpu as pltpu

F32 = jnp.float32
BF16 = jnp.bfloat16
MESH = pl.DeviceIdType.MESH

N_DEV = 8
LANES = 128
MM_TILES = (1024, 1408, 896, 512, 384, 256, 128)
MM_VMEM_BUDGET = 38 * 1024 * 1024
MM_VMEM_LIMIT = 56 * 1024 * 1024
VMEM_LIMIT = 32 * 1024 * 1024
LN_EPS = 1e-5
RMS_EPS = 1e-6
SSD_HEAD_DIM = 64
SSD_N_GROUPS = 8
SSD_D_STATE = 128
SSD_CHUNK = 128
MLA_NOPE = 128
MLA_ROPE = 64
ROPE_THETA = 10000.0
ADAM_LR = 0.001
ADAM_B1 = 0.9
ADAM_B2 = 0.999
ADAM_EPS = 1e-08
ADAM_WD = 0.01
ADAM_STEP = 10
NEG_BIG = -1e30

SHARDED = (('ssd_in_proj', 2), ('ssd_conv_w', 2), ('ssd_conv_b', 1), ('ssd_norm_g', 1), ('ssd_out_proj', 1),
           ('kv_down_proj', 0), ('kv_up_k', 1), ('kv_up_v', 1), ('q_down_proj', 1), ('q_up_proj', 2),
           ('attn_out_proj', 1), ('ffn_up', 2), ('ffn_conv_w', 2), ('ffn_down', 1))
SHARDED_F32 = ('ssd_conv_w', 'ssd_conv_b', 'ssd_norm_g', 'ffn_conv_w')
GATHERED_FIRST = ('ssd_in_proj',)
REDUCED_LAST = ('ssd_in_proj',)
REPLICATED = ('ssd_dt_bias', 'ssd_A_log', 'ssd_D', 'kv_norm_g', 'q_norm_g', 'ffn_conv_b',
              'ln_mix_g', 'ln_mix_b', 'ln_ffn_g', 'ln_ffn_b')
WEIGHTS = ('ssd_in_proj', 'ssd_conv_w', 'ssd_conv_b', 'ssd_dt_bias', 'ssd_A_log', 'ssd_D', 'ssd_norm_g',
           'ssd_out_proj', 'kv_down_proj', 'kv_norm_g', 'kv_up_k', 'kv_up_v', 'q_down_proj', 'q_norm_g',
           'q_up_proj', 'attn_out_proj', 'ffn_up', 'ffn_conv_w', 'ffn_conv_b', 'ffn_down', 'ln_mix_g',
           'ln_mix_b', 'ln_ffn_g', 'ln_ffn_b')


def _cparams(limit=None):
    return pltpu.CompilerParams(vmem_limit_bytes=VMEM_LIMIT if limit is None else limit)


def _tile(n, cands):
    for c in cands:
        if n % c == 0:
            return c
    return n


def _sigmoid(x):
    return 1.0 / (1.0 + jnp.exp(-x))


def _iota(shape, axis):
    return lax.broadcasted_iota(jnp.int32, shape, axis)


def _mm(a, b, *, ta=False, tb=False, add=None, add_scale=1.0, out_dtype=F32, name, b_koff=0, into=None, into_col=0,
        exchange=()):
    if ta:
        K, M = a.shape
    else:
        M, K = a.shape
    if tb:
        N, Kb = b.shape
    else:
        Kb, N = b.shape
    assert b_koff + K <= Kb, (a.shape, b.shape, ta, tb, b_koff)
    tm = _tile(M, MM_TILES)
    tn = _tile(N, MM_TILES)
    tk = K if K <= MM_TILES[0] and b_koff == 0 and Kb == K else _tile(K, MM_TILES)
    nk = K // tk
    assert b_koff % tk == 0 and (into is None or (into.shape[0] == M and into.dtype == out_dtype))
    kb = b_koff // tk

    def vmem_estimate(tm_, tn_):
        ins = 2 * (tm_ * tk * a.dtype.itemsize + tk * tn_ * b.dtype.itemsize)
        outs = tm_ * tn_ * (2 * jnp.dtype(out_dtype).itemsize + 4 + (4 if nk > 1 else 0))
        return ins + outs + (2 * tm_ * tn_ * add.dtype.itemsize if add is not None else 0)

    while vmem_estimate(tm, tn) > MM_VMEM_BUDGET:
        can_m, can_n = tm % (2 * LANES) == 0, tn % (2 * LANES) == 0
        if can_m and (tm >= tn or not can_n):
            tm //= 2
        elif can_n:
            tn //= 2
        else:
            break

    assert into_col % tn == 0
    jb = into_col // tn

    nx = len(exchange)
    n_in = (add is not None) + (into is not None)
    grid = (M // tm, N // tn, nk)

    def body(a_ref, b_ref, *rest):
        add_ref = rest[0] if add is not None else None
        o_ref = rest[n_in + nx]
        acc = rest[n_in + 2 * nx + 1] if nk > 1 else None
        k = pl.program_id(2)
        if nx:
            start, finish_exchange = _chips_plan(rest[n_in:n_in + nx], rest[n_in + nx + 1:n_in + 2 * nx + 1],
                                                 *rest[len(rest) - 2:])
            t, total = _grid_step(grid)
            pl.when(t == 0)(start)

        if ta:
            av = a_ref[...].astype(BF16).T
        else:
            av = a_ref[...].astype(BF16)
        bv = b_ref[...].astype(BF16)
        dn = (((1,), (1 if tb else 0,)), ((), ()))
        part = lax.dot_general(av, bv, dn, preferred_element_type=F32)

        def finish(r):
            if add is not None:
                r = r + add_scale * add_ref[...].astype(F32)
            o_ref[...] = r.astype(out_dtype)

        if nk == 1:
            finish(part)
        else:
            @pl.when(k == 0)
            def _():
                acc[...] = part

            @pl.when(k > 0)
            def _():
                acc[...] += part

            @pl.when(k == nk - 1)
            def _():
                finish(acc[...])

        if nx:
            pl.when(t == total - 1)(finish_exchange)

    a_spec = (pl.BlockSpec((tk, tm), lambda i, j, k: (k, i)) if ta
              else pl.BlockSpec((tm, tk), lambda i, j, k: (i, k)))
    b_spec = (pl.BlockSpec((tn, tk), lambda i, j, k: (j, k + kb)) if tb
              else pl.BlockSpec((tk, tn), lambda i, j, k: (k + kb, j)))
    in_specs = [a_spec, b_spec]
    args = [a, b]
    if add is not None:
        in_specs.append(pl.BlockSpec((tm, tn), lambda i, j, k: (i, j)))
        args.append(add)
    aliases = {}
    if into is not None:
        aliases = {len(args): 0}
        in_specs.append(pl.BlockSpec(memory_space=pl.ANY))
        args.append(into)
    any_spec = pl.BlockSpec(memory_space=pl.ANY)
    outs = pl.pallas_call(
        body, name=name, grid=grid,
        in_specs=in_specs + [any_spec] * nx,
        out_specs=[pl.BlockSpec((tm, tn), lambda i, j, k: (i, j + jb))] + [any_spec] * nx,
        out_shape=[jax.ShapeDtypeStruct((M, N) if into is None else into.shape, out_dtype)] + _chips_shapes(exchange),
        scratch_shapes=([pltpu.VMEM((tm, tn), F32)] if nk > 1 else []) + (_chips_sems(nx) if nx else []),
        input_output_aliases=aliases, compiler_params=_cparams(MM_VMEM_LIMIT),
    )(*args, *exchange)
    return (outs[0], list(outs[1:])) if nx else outs[0]


def _ln_fwd(h, mix, g, b, alpha, name):
    T, D = h.shape
    tm = _tile(T, (512, 256, 128))

    def body(h_ref, m_ref, g_ref, b_ref, y_ref, s_ref, yb_ref):
        s = alpha * h_ref[...] + m_ref[...]
        mu = jnp.mean(s, axis=1, keepdims=True)
        xc = s - mu
        var = jnp.mean(xc * xc, axis=1, keepdims=True)
        y = xc * lax.rsqrt(var + LN_EPS) * g_ref[...] + b_ref[...]
        y_ref[...] = y
        s_ref[...] = s
        yb_ref[...] = y.astype(BF16)

    row = pl.BlockSpec((tm, D), lambda i: (i, 0))
    vec = pl.BlockSpec((1, D), lambda i: (0, 0))
    return pl.pallas_call(
        body, name=name, grid=(T // tm,), in_specs=[row, row, vec, vec], out_specs=[row, row, row],
        out_shape=[jax.ShapeDtypeStruct((T, D), F32)] * 2 + [jax.ShapeDtypeStruct((T, D), BF16)],
        compiler_params=_cparams(),
    )(h, mix, g, b)


def _ln_bwd(dy, s, g, name):
    T, D = s.shape
    tm = _tile(T, (512, 256, 128))

    def body(dy_ref, s_ref, g_ref, ds_ref, dsb_ref, dg_ref, db_ref):
        @pl.when(pl.program_id(0) == 0)
        def _():
            dg_ref[...] = jnp.zeros_like(dg_ref)
            db_ref[...] = jnp.zeros_like(db_ref)

        sv = s_ref[...]
        dyv = dy_ref[...]
        mu = jnp.mean(sv, axis=1, keepdims=True)
        xc = sv - mu
        rstd = lax.rsqrt(jnp.mean(xc * xc, axis=1, keepdims=True) + LN_EPS)
        xhat = xc * rstd
        dxh = dyv * g_ref[...]
        m1 = jnp.mean(dxh, axis=1, keepdims=True)
        m2 = jnp.mean(dxh * xhat, axis=1, keepdims=True)
        ds = rstd * (dxh - m1 - xhat * m2)
        ds_ref[...] = ds
        dsb_ref[...] = ds.astype(BF16)
        dg_ref[...] += jnp.sum(dyv * xhat, axis=0, keepdims=True)
        db_ref[...] += jnp.sum(dyv, axis=0, keepdims=True)

    row = pl.BlockSpec((tm, D), lambda i: (i, 0))
    vec = pl.BlockSpec((1, D), lambda i: (0, 0))
    return pl.pallas_call(
        body, name=name, grid=(T // tm,), in_specs=[row, row, vec], out_specs=[row, row, vec, vec],
        out_shape=[jax.ShapeDtypeStruct((T, D), F32), jax.ShapeDtypeStruct((T, D), BF16),
                   jax.ShapeDtypeStruct((1, D), F32), jax.ShapeDtypeStruct((1, D), F32)],
        compiler_params=_cparams(),
    )(dy, s, g)


def _loss_head(y, target, name):
    T, D = y.shape
    tm = _tile(T, (512, 256, 128))

    def body(y_ref, t_ref, sq_ref, dy_ref):
        @pl.when(pl.program_id(0) == 0)
        def _():
            sq_ref[...] = jnp.zeros_like(sq_ref)

        e = y_ref[...] - t_ref[...]
        sq_ref[...] += jnp.sum(e * e, axis=0, keepdims=True)
        dy_ref[...] = e * (1.0 / D)

    row = pl.BlockSpec((tm, D), lambda i: (i, 0))
    vec = pl.BlockSpec((1, D), lambda i: (0, 0))
    return pl.pallas_call(
        body, name=name, grid=(T // tm,), in_specs=[row, row], out_specs=[vec, row],
        out_shape=[jax.ShapeDtypeStruct((1, D), F32), jax.ShapeDtypeStruct((T, D), F32)],
        compiler_params=_cparams(),
    )(y, target)


def _shift_down(x, j):
    if j == 0:
        return x
    return jnp.where(_iota(x.shape, 0) >= j, pltpu.roll(x, j, 0), 0.0)


def _shift_up(x, j):
    if j == 0:
        return x
    n = x.shape[0]
    return jnp.where(_iota(x.shape, 0) < n - j, pltpu.roll(x, n - j, 0), 0.0)


def _conv_val(x, w_ref, b_ref):
    width = w_ref.shape[0]
    y = b_ref[...] + w_ref[width - 1:width, :] * x
    for j in range(1, width):
        y = y + w_ref[width - 1 - j:width - j, :] * _shift_down(x, j)
    return y


def _conv_bwd_val(x, dc, w_ref, dw_ref, db_ref):
    width = w_ref.shape[0]
    dx = w_ref[width - 1:width, :] * dc
    dw_ref[width - 1:width, :] += jnp.sum(dc * x, axis=0, keepdims=True)
    for j in range(1, width):
        sj = _shift_up(dc, j)
        dx = dx + w_ref[width - 1 - j:width - j, :] * sj
        dw_ref[width - 1 - j:width - j, :] += jnp.sum(sj * x, axis=0, keepdims=True)
    db_ref[...] += jnp.sum(dc, axis=0, keepdims=True)
    return dx


def _ffn_specs(H, S, width, cb):
    cb = _tile(H, (cb, LANES))
    nC = H // cb
    return dict(
        g=pl.BlockSpec((S, cb), lambda j, b: (b, j)), v=pl.BlockSpec((S, cb), lambda j, b: (b, j + nC)),
        wg=pl.BlockSpec((width, cb), lambda j, b: (0, j)), wv=pl.BlockSpec((width, cb), lambda j, b: (0, j + nC)),
        bg=pl.BlockSpec((1, cb), lambda j, b: (0, j)), bv=pl.BlockSpec((1, cb), lambda j, b: (0, j + nC))), nC


def _ffn_act_fwd(u, cw, cb, S, name):
    T, H2 = u.shape
    H = H2 // 2
    sp, nC = _ffn_specs(H, S, cw.shape[0], 2 * LANES)

    def body(ug_ref, uv_ref, wg_ref, wv_ref, bg_ref, bv_ref, a_ref):
        g = _conv_val(ug_ref[...], wg_ref, bg_ref)
        v = _conv_val(uv_ref[...], wv_ref, bv_ref)
        a_ref[...] = (g * _sigmoid(g) * v).astype(BF16)

    return pl.pallas_call(
        body, name=name, grid=(nC, T // S),
        in_specs=[sp['g'], sp['v'], sp['wg'], sp['wv'], sp['bg'], sp['bv']], out_specs=sp['g'],
        out_shape=jax.ShapeDtypeStruct((T, H), BF16), compiler_params=_cparams(),
    )(u, u, cw, cw, cb, cb)


def _ffn_act_bwd(u, da, cw, cb, S, name):
    T, H2 = u.shape
    H = H2 // 2
    width = cw.shape[0]
    sp, nC = _ffn_specs(H, S, width, LANES)

    def body(ug_ref, uv_ref, da_ref, wg_ref, wv_ref, bg_ref, bv_ref,
             dug_ref, duv_ref, dwg_ref, dwv_ref, dbg_ref, dbv_ref):
        @pl.when(pl.program_id(1) == 0)
        def _():
            for r in (dwg_ref, dwv_ref, dbg_ref, dbv_ref):
                r[...] = jnp.zeros_like(r)

        def conv_bwd(x, dc, w_ref, dw_ref, db_ref):
            dx = w_ref[width - 1:width, :] * dc
            for j in range(1, width):
                dx = dx + w_ref[width - 1 - j:width - j, :] * _shift_up(dc, j)
            for j in range(width):
                dw_ref[width - 1 - j:width - j, :] += jnp.sum(dc * _shift_down(x, j), axis=0, keepdims=True)
            db_ref[...] += jnp.sum(dc, axis=0, keepdims=True)
            return dx

        xg = ug_ref[...]
        xv = uv_ref[...]
        g = _conv_val(xg, wg_ref, bg_ref)
        v = _conv_val(xv, wv_ref, bv_ref)
        dav = da_ref[...].astype(F32)
        sg = _sigmoid(g)
        dg = dav * v * sg * (1.0 + g * (1.0 - sg))
        dv = dav * g * sg
        dug_ref[...] = conv_bwd(xg, dg, wg_ref, dwg_ref, dbg_ref).astype(BF16)
        duv_ref[...] = conv_bwd(xv, dv, wv_ref, dwv_ref, dbv_ref).astype(BF16)

    act = jax.ShapeDtypeStruct((T, H), BF16)
    wsh = jax.ShapeDtypeStruct((width, H), F32)
    bsh = jax.ShapeDtypeStruct((1, H), F32)
    return pl.pallas_call(
        body, name=name, grid=(nC, T // S),
        in_specs=[sp['g'], sp['v'], sp['g'], sp['wg'], sp['wv'], sp['bg'], sp['bv']],
        out_specs=[sp['g'], sp['g'], sp['wg'], sp['wg'], sp['bg'], sp['bg']],
        out_shape=[act, act, wsh, wsh, bsh, bsh], compiler_params=_cparams(),
    )(u, u, da, cw, cw, cb, cb)


def _ssd_conv_fwd(x, cw, cb, S, name):
    T, C = x.shape
    Cb = _tile(C, (256, 128))
    width = cw.shape[0]

    def body(x_ref, w_ref, b_ref, y_ref, c_ref):
        c = _conv_val(x_ref[...], w_ref, b_ref)
        c_ref[...] = c
        y_ref[...] = c * _sigmoid(c)

    blk = pl.BlockSpec((S, Cb), lambda j, b: (b, j))
    wblk = pl.BlockSpec((width, Cb), lambda j, b: (0, j))
    bblk = pl.BlockSpec((1, Cb), lambda j, b: (0, j))
    return pl.pallas_call(
        body, name=name, grid=(C // Cb, T // S), in_specs=[blk, wblk, bblk], out_specs=[blk, blk],
        out_shape=[jax.ShapeDtypeStruct((T, C), F32)] * 2, compiler_params=_cparams(),
    )(x, cw, cb)


def _ssd_conv_bwd(x, c, dys, cw, S, name):
    T, C = x.shape
    Cb = _tile(C, (256, 128))
    width = cw.shape[0]
    assert all(d.shape[1] % Cb == 0 for d in dys) and sum(d.shape[1] for d in dys) == C

    def part(dy, j0, dx_prev, k):
        n = dy.shape[1] // Cb

        def body(x_ref, c_ref, dy_ref, w_ref, prev_ref, dx_ref, dw_ref, db_ref):
            @pl.when(pl.program_id(1) == 0)
            def _():
                dw_ref[...] = jnp.zeros_like(dw_ref)
                db_ref[...] = jnp.zeros_like(db_ref)

            cval = c_ref[...]
            sc = _sigmoid(cval)
            dc = dy_ref[...] * sc * (1.0 + cval * (1.0 - sc))
            dx_ref[...] = _conv_bwd_val(x_ref[...], dc, w_ref, dw_ref, db_ref).astype(BF16)

        wide = pl.BlockSpec((S, Cb), lambda j, b: (b, j + j0))
        return pl.pallas_call(
            body, name=f'{name}_{k}', grid=(n, T // S),
            in_specs=[wide, wide, pl.BlockSpec((S, Cb), lambda j, b: (b, j)),
                      pl.BlockSpec((width, Cb), lambda j, b: (0, j + j0)), pl.BlockSpec(memory_space=pl.ANY)],
            out_specs=[wide, pl.BlockSpec((width, Cb), lambda j, b: (0, j)), pl.BlockSpec((1, Cb), lambda j, b: (0, j))],
            out_shape=[jax.ShapeDtypeStruct((T, C), BF16), jax.ShapeDtypeStruct((width, n * Cb), F32),
                       jax.ShapeDtypeStruct((1, n * Cb), F32)],
            input_output_aliases={4: 0}, compiler_params=_cparams(),
        )(x, c, dy, cw, dx_prev)

    dx = lax.empty((T, C), BF16)
    dws, dbs, j0 = [], [], 0
    for k, dy in enumerate(dys):
        dx, dw, db = part(dy, j0, dx, k)
        dws.append(dw)
        dbs.append(db)
        j0 += dy.shape[1] // Cb
    return dx, jnp.concatenate(dws, axis=1), jnp.concatenate(dbs, axis=1)


def _softplus(x):
    e = jnp.exp(-jnp.abs(x))
    return jnp.maximum(x, 0.0) + jnp.where(e < 1e-4, e * (1.0 - 0.5 * e), jnp.log(1.0 + e))


def _cumsum_rows(x):
    n = x.shape[0]
    row = _iota(x.shape, 0)
    k = 1
    while k < n:
        x = x + jnp.where(row >= k, pltpu.roll(x, k, 0), 0.0)
        k *= 2
    return x


def _rev_cumsum_rows(x):
    n = x.shape[0]
    row = _iota(x.shape, 0)
    k = 1
    while k < n:
        x = x + jnp.where(row < n - k, pltpu.roll(x, n - k, 0), 0.0)
        k *= 2
    return x


def _col(x, lane_ids, h):
    return jnp.sum(jnp.where(lane_ids == h, x, 0.0), axis=1, keepdims=True)


def _bdot(a, b, dims):
    return lax.dot_general(a.astype(BF16), b.astype(BF16), (dims, ((), ())), preferred_element_type=F32)


NN = ((1,), (0,))
NT = ((1,), (1,))


def _ssd_dt_prep(dt_pre, dt_bias, a_log, n_heads, name):
    T = dt_pre.shape[0]
    L = SSD_CHUNK

    def body(dtp_ref, bias_ref, alog_ref, dtb_ref, cumb_ref):
        dt = _softplus(dtp_ref[...] + bias_ref[...])
        cum = _cumsum_rows(dt * (-jnp.exp(alog_ref[...])))
        lane = _iota((L, LANES), 1)
        for h in range(n_heads):
            hcols = pl.ds(h * LANES, LANES)
            dtb_ref[:, hcols] = jnp.broadcast_to(_col(dt, lane, h), (L, LANES))
            cumb_ref[:, hcols] = jnp.broadcast_to(_col(cum, lane, h), (L, LANES))

    row = pl.BlockSpec((L, LANES), lambda i: (i, 0))
    vec = pl.BlockSpec((1, LANES), lambda i: (0, 0))
    wide = pl.BlockSpec((L, n_heads * LANES), lambda i: (i, 0))
    return pl.pallas_call(
        body, name=name, grid=(T // L,), in_specs=[row, vec, vec], out_specs=[wide, wide],
        out_shape=[jax.ShapeDtypeStruct((T, n_heads * LANES), F32)] * 2, compiler_params=_cparams(),
    )(dt_pre, dt_bias, a_log)


SCAN_GROUPS_FWD = 8
SCAN_GROUPS_BWD = 4


def _scan_specs(nc, d_inner, hpg, GP):
    L = SSD_CHUNK
    G = SSD_N_GROUPS
    gw = GP * hpg * SSD_HEAD_DIM
    bw = GP * LANES
    assert G % GP == 0 and d_inner % bw == 0
    nb = d_inner // bw
    return dict(
        xs=pl.BlockSpec((L, gw), lambda b, q, c: (b * nc + c, q)),
        B=pl.BlockSpec((L, bw), lambda b, q, c: (b * nc + c, nb + q)),
        C=pl.BlockSpec((L, bw), lambda b, q, c: (b * nc + c, nb + G // GP + q)),
        heads=pl.BlockSpec((L, GP * hpg * LANES), lambda b, q, c: (b * nc + c, q)),
        gvec=pl.BlockSpec((1, gw), lambda b, q, c: (0, q)),
        grp=pl.BlockSpec((L, bw), lambda b, q, c: (b * nc + c, q)),
        st=pl.BlockSpec((1, GP * hpg // 2, SSD_D_STATE, LANES), lambda b, q, c: (b * nc + c, q, 0, 0)),
    )


def _grid_step(dims):
    t, total = 0, 1
    for ax, d in enumerate(dims):
        t = t * d + pl.program_id(ax)
        total *= d
    return t, total


def _scan_fwd(xbc, dtb, cumb, S, d_inner, name, gather=()):
    T = xbc.shape[0]
    Bn = T // S
    L = SSD_CHUNK
    G = SSD_N_GROUPS
    nc = S // L
    hpg = d_inner // SSD_HEAD_DIM // G
    pairs = hpg // 2
    GP = SCAN_GROUPS_FWD
    sp = _scan_specs(nc, d_inner, hpg, GP)
    ng = len(gather)

    def body(*refs):
        xs_ref, b_ref, c_ref, dtb_ref, cumb_ref = refs[:5]
        y_ref, st_ref = refs[5 + ng:7 + ng]
        state = refs[7 + 2 * ng]

        if ng:
            start, forward, finish = _gather_plan(refs[5:5 + ng], refs[7 + ng:7 + 2 * ng], *refs[8 + 2 * ng:])
            t, total = _grid_step((Bn, G // GP, nc))
            pl.when(t == 0)(start)
            pl.when(t == total // 2)(forward)

        @pl.when(pl.program_id(2) == 0)
        def _():
            state[...] = jnp.zeros_like(state)

        tril = _iota((L, L), 1) <= _iota((L, L), 0)
        lane = _iota((L, LANES), 1)
        lane1 = _iota((1, LANES), 1)
        last = _iota((L, 1), 0) == L - 1
        for gi, p in [(gi, p) for gi in range(GP) for p in range(pairs)]:
            if p == 0:
                Bm = b_ref[:, gi * LANES:(gi + 1) * LANES]
                Cm = c_ref[:, gi * LANES:(gi + 1) * LANES]
                Gm = _bdot(Cm, Bm, NT)
                BmT = Bm.T
            sp_ = gi * pairs + p
            cols = pl.ds(sp_ * LANES, LANES)
            S_in = state[sp_]
            st_ref[0, sp_] = S_in
            x_pair = xs_ref[:, cols]
            y_pair = jnp.zeros((L, LANES), F32)
            z_pair = jnp.zeros((L, LANES), F32)
            e_pair = jnp.zeros((L, LANES), F32)
            el_pair = jnp.zeros((1, LANES), F32)
            for k in range(2):
                hcols = pl.ds((gi * hpg + 2 * p + k) * LANES, LANES)
                cum_col = cumb_ref[:, hcols]
                dt_col = dtb_ref[:, hcols]
                decay = jnp.exp(jnp.where(tril, cum_col - cum_col.T, NEG_BIG))
                mk = (lane >= SSD_HEAD_DIM) if k else (lane < SSD_HEAD_DIM)
                xd = jnp.where(mk, x_pair * dt_col, 0.0)
                y_pair = y_pair + _bdot(Gm * decay, xd, NN)
                cum_l = jnp.sum(jnp.where(last, cum_col, 0.0), axis=0, keepdims=True)
                e_pair = jnp.where(mk, jnp.exp(cum_col), e_pair)
                z_pair = z_pair + xd * jnp.exp(cum_l - cum_col)
                el_pair = jnp.where((lane1 >= SSD_HEAD_DIM) if k else (lane1 < SSD_HEAD_DIM), jnp.exp(cum_l), el_pair)
            y_pair = y_pair + e_pair * _bdot(Cm, S_in, NN)
            state[sp_] = S_in * el_pair + _bdot(BmT, z_pair, NN)
            y_ref[:, cols] = y_pair

        if ng:
            pl.when(t == total - 1)(finish)

    outs = pl.pallas_call(
        body, name=name, grid=(Bn, G // GP, nc),
        in_specs=[sp['xs'], sp['B'], sp['C'], sp['heads'], sp['heads']] + [ANY] * ng,
        out_specs=[sp['xs'], sp['st']] + [ANY] * ng,
        out_shape=[jax.ShapeDtypeStruct((T, d_inner), F32),
                   jax.ShapeDtypeStruct((Bn * nc, G * pairs, SSD_D_STATE, LANES), F32)] + _gather_shapes(gather),
        scratch_shapes=[pltpu.VMEM((GP * pairs, SSD_D_STATE, LANES), F32)] + (_gather_sems(ng) if ng else []),
        compiler_params=_cparams(),
    )(xbc, xbc, xbc, dtb, cumb, *gather)
    return outs[0], outs[1], list(outs[2:])


def _scan_bwd(xbc, dtb, cumb, states, dy, d_lane, S, d_inner, name, exchange=()):
    T = xbc.shape[0]
    Bn = T // S
    L = SSD_CHUNK
    G = SSD_N_GROUPS
    nc = S // L
    hpg = d_inner // SSD_HEAD_DIM // G
    pairs = hpg // 2
    GP = SCAN_GROUPS_BWD
    sp = _scan_specs(nc, d_inner, hpg, GP)
    nx = len(exchange)

    def rev(spec):
        im = spec.index_map
        return pl.BlockSpec(spec.block_shape, lambda b, g, c: im(b, g, nc - 1 - c))

    def body(*refs):
        xs_ref, b_ref, c_ref, dtb_ref, cumb_ref, st_ref, dy_ref, dl_ref = refs[:8]
        dxs_ref, db_ref, dc_ref, ddt_ref, da_ref = refs[8 + nx:13 + nx]
        dstate = refs[13 + 2 * nx]
        g = pl.program_id(1)

        if nx:
            start, finish = _chips_plan(refs[8:8 + nx], refs[13 + nx:13 + 2 * nx], *refs[14 + 2 * nx:])
            t, total = _grid_step((Bn, G // GP, nc))
            pl.when(t == 0)(start)

        @pl.when(pl.program_id(2) == 0)
        def _():
            dstate[...] = jnp.zeros_like(dstate)

        tril = _iota((L, L), 1) <= _iota((L, L), 0)
        lane = _iota((L, LANES), 1)
        lane1 = _iota((1, LANES), 1)
        last = _iota((L, 1), 0) == L - 1
        for gi, p in [(gi, p) for gi in range(GP) for p in range(pairs)]:
            gcols = pl.ds(gi * LANES, LANES)
            if p == 0:
                Bm = b_ref[:, gcols]
                Cm = c_ref[:, gcols]
                Gm = _bdot(Cm, Bm, NT)
                CmT = Cm.T
                dG = jnp.zeros((L, L), F32)
                dB = jnp.zeros((L, LANES), F32)
                dC = jnp.zeros((L, LANES), F32)
                dcum = jnp.zeros((L, LANES), F32)
                ddt = jnp.zeros((L, LANES), F32)
            sp_ = gi * pairs + p
            cols = pl.ds(sp_ * LANES, LANES)
            S_in = st_ref[0, sp_]
            dS = dstate[sp_]
            x_pair = xs_ref[:, cols]
            dy_pair = dy_ref[:, cols]
            Q = _bdot(Cm, S_in, NN)
            dZ = _bdot(Bm, dS, NN)
            prod = dS * S_in
            e_pair = jnp.zeros((L, LANES), F32)
            z_pair = jnp.zeros((L, LANES), F32)
            el_pair = jnp.zeros((1, LANES), F32)
            dx_pair = dl_ref[:, cols] * dy_pair
            for k in range(2):
                h = (g * GP + gi) * hpg + 2 * p + k
                hcols = pl.ds((gi * hpg + 2 * p + k) * LANES, LANES)
                cum_col = cumb_ref[:, hcols]
                dt_col = dtb_ref[:, hcols]
                decay = jnp.exp(jnp.where(tril, cum_col - cum_col.T, NEG_BIG))
                mk = (lane >= SSD_HEAD_DIM) if k else (lane < SSD_HEAD_DIM)
                mk1 = (lane1 >= SSD_HEAD_DIM) if k else (lane1 < SSD_HEAD_DIM)
                xd = jnp.where(mk, x_pair * dt_col, 0.0)
                dy_k = jnp.where(mk, dy_pair, 0.0)
                Wm = Gm * decay
                dWm = jnp.where(tril, _bdot(dy_k, xd, NT), 0.0)
                dxd = _bdot(Wm.T, dy_k, NN)
                dseg = dWm * Wm
                dG = dG + dWm * decay
                dcum_col = (jnp.sum(dseg, axis=1, keepdims=True)
                            - jnp.sum(dseg.T, axis=1, keepdims=True))
                cum_l = jnp.sum(jnp.where(last, cum_col, 0.0), axis=0, keepdims=True)
                e_col = jnp.exp(cum_col)
                w_col = jnp.exp(cum_l - cum_col)
                el = jnp.exp(cum_l)
                dcum_col = dcum_col + jnp.sum(dy_k * Q, axis=1, keepdims=True) * e_col
                de_e = jnp.sum(dZ * xd, axis=1, keepdims=True) * w_col
                dcum_col = dcum_col - de_e
                dcum_l = (jnp.sum(de_e, axis=0, keepdims=True)
                          + el * jnp.sum(jnp.sum(jnp.where(mk, prod, 0.0), axis=1, keepdims=True),
                                         axis=0, keepdims=True))
                dcum_col = dcum_col + jnp.where(last, dcum_l, 0.0)
                dxd = dxd + jnp.where(mk, dZ * w_col, 0.0)
                dx_pair = dx_pair + dxd * dt_col
                ddt = jnp.where(lane == h, jnp.sum(dxd * x_pair, axis=1, keepdims=True), ddt)
                dcum = jnp.where(lane == h, dcum_col, dcum)
                e_pair = jnp.where(mk, e_col, e_pair)
                z_pair = z_pair + xd * w_col
                el_pair = jnp.where(mk1, el, el_pair)
            dQ = e_pair * dy_pair
            dC = dC + _bdot(dQ, S_in, NT)
            dB = dB + _bdot(z_pair, dS, NT)
            dstate[sp_] = dS * el_pair + _bdot(CmT, dQ, NN)
            dxs_ref[:, cols] = dx_pair
            if p == pairs - 1:
                dc_ref[:, gcols] = dC + _bdot(dG, Bm, NN)
                db_ref[:, gcols] = dB + _bdot(dG.T, Cm, NN)
                ddt_ref[:, gcols] = ddt
                da_ref[:, gcols] = _rev_cumsum_rows(dcum)

        if nx:
            pl.when(t == total - 1)(finish)

    grp = jax.ShapeDtypeStruct((T, G * LANES), F32)
    outs = pl.pallas_call(
        body, name=name, grid=(Bn, G // GP, nc),
        in_specs=[rev(sp['xs']), rev(sp['B']), rev(sp['C']), rev(sp['heads']), rev(sp['heads']),
                  rev(sp['st']), rev(sp['xs']), sp['gvec']] + [ANY] * nx,
        out_specs=[rev(sp['xs']), rev(sp['grp']), rev(sp['grp']), rev(sp['grp']), rev(sp['grp'])] + [ANY] * nx,
        out_shape=[jax.ShapeDtypeStruct((T, d_inner), F32), grp, grp, grp, grp] + _chips_shapes(exchange),
        scratch_shapes=[pltpu.VMEM((GP * pairs, SSD_D_STATE, LANES), F32)] + (_chips_sems(nx) if nx else []),
        compiler_params=_cparams(),
    )(xbc, xbc, xbc, dtb, cumb, states, dy, d_lane, *exchange)
    return outs[:5], list(outs[5:])


def _dt_bwd(ddt_g, da_g, dt_pre, dt_bias, a_log, name):
    T = dt_pre.shape[0]
    G = SSD_N_GROUPS
    tm = _tile(T, (512, 256, 128))

    def body(ddt_ref, da_ref, dtp_ref, bias_ref, alog_ref, dx_ref, dbias_ref, dal_ref):
        @pl.when(pl.program_id(0) == 0)
        def _():
            dbias_ref[...] = jnp.zeros_like(dbias_ref)
            dal_ref[...] = jnp.zeros_like(dal_ref)

        ddt = ddt_ref[:, 0:LANES]
        da = da_ref[:, 0:LANES]
        for gi in range(1, G):
            ddt = ddt + ddt_ref[:, gi * LANES:(gi + 1) * LANES]
            da = da + da_ref[:, gi * LANES:(gi + 1) * LANES]
        xv = dtp_ref[...] + bias_ref[...]
        A = -jnp.exp(alog_ref[...])
        dx = (ddt + da * A) * _sigmoid(xv)
        dx_ref[...] = dx.astype(BF16)
        dbias_ref[...] += jnp.sum(dx, axis=0, keepdims=True)
        dal_ref[...] += jnp.sum(da * _softplus(xv), axis=0, keepdims=True) * A

    wide = pl.BlockSpec((tm, G * LANES), lambda i: (i, 0))
    row = pl.BlockSpec((tm, LANES), lambda i: (i, 0))
    vec = pl.BlockSpec((1, LANES), lambda i: (0, 0))
    vsh = jax.ShapeDtypeStruct((1, LANES), F32)
    return pl.pallas_call(
        body, name=name, grid=(T // tm,), in_specs=[wide, wide, row, vec, vec], out_specs=[row, vec, vec],
        out_shape=[jax.ShapeDtypeStruct((T, LANES), BF16), vsh, vsh], compiler_params=_cparams(),
    )(ddt_g, da_g, dt_pre, dt_bias, a_log)


def _gate_fwd(y, xbc, z, d_lane, ng, d_inner, name):
    T = y.shape[0]
    G = SSD_N_GROUPS
    gw = d_inner // G
    tm = _tile(T, (1024, 512, 256, 128))

    def body(y_ref, x_ref, z_ref, dl_ref, ng_ref, o_ref):
        zv = z_ref[...]
        y2 = (y_ref[...] + dl_ref[...] * x_ref[...]) * (zv * _sigmoid(zv))
        r = lax.rsqrt(jnp.mean(y2 * y2, axis=1, keepdims=True) + LN_EPS)
        o_ref[...] = (y2 * r * ng_ref[...]).astype(BF16)

    blk = pl.BlockSpec((tm, gw), lambda g, i: (i, g))
    vec = pl.BlockSpec((1, gw), lambda g, i: (0, g))
    return pl.pallas_call(
        body, name=name, grid=(G, T // tm), in_specs=[blk, blk, blk, vec, vec], out_specs=blk,
        out_shape=jax.ShapeDtypeStruct((T, d_inner), BF16), compiler_params=_cparams(),
    )(y, xbc, z, d_lane, ng)


def _gate_bwd(dyn, y, xbc, z, d_lane, ng, d_inner, name, exchange=()):
    T = y.shape[0]
    G = SSD_N_GROUPS
    gw = d_inner // G
    tm = _tile(T, (1024, 512, 256, 128))
    nx = len(exchange)

    def body(*refs):
        dyn_ref, y_ref, x_ref, z_ref, dl_ref, ng_ref = refs[:6]
        dy_ref, dz_ref, dng_ref, ddl_ref = refs[6 + nx:10 + nx]
        if nx:
            start, finish = _sibling_plan(refs[6:6 + nx], refs[10 + nx:10 + 2 * nx], *refs[10 + 2 * nx:])
            t, total = _grid_step((G, T // tm))
            pl.when(t == 0)(start)

        @pl.when(pl.program_id(1) == 0)
        def _():
            dng_ref[...] = jnp.zeros_like(dng_ref)
            ddl_ref[...] = jnp.zeros_like(ddl_ref)

        zv = z_ref[...]
        xv = x_ref[...]
        sz = _sigmoid(zv)
        y1 = y_ref[...] + dl_ref[...] * xv
        y2 = y1 * (zv * sz)
        r = lax.rsqrt(jnp.mean(y2 * y2, axis=1, keepdims=True) + LN_EPS)
        dn = dyn_ref[...].astype(F32)
        dng_ref[...] += jnp.sum(dn * y2 * r, axis=0, keepdims=True)
        do = dn * ng_ref[...]
        dy2 = r * do - y2 * (r * r * r) * jnp.mean(do * y2, axis=1, keepdims=True)
        dz_ref[...] = (dy2 * y1 * sz * (1.0 + zv * (1.0 - sz))).astype(BF16)
        dy1 = dy2 * (zv * sz)
        dy_ref[...] = dy1
        ddl_ref[...] += jnp.sum(dy1 * xv, axis=0, keepdims=True)
        if nx:
            pl.when(t == total - 1)(finish)

    blk = pl.BlockSpec((tm, gw), lambda g, i: (i, g))
    vec = pl.BlockSpec((1, gw), lambda g, i: (0, g))
    vsh = jax.ShapeDtypeStruct((1, d_inner), F32)
    outs = pl.pallas_call(
        body, name=name, grid=(G, T // tm), in_specs=[blk, blk, blk, blk, vec, vec] + [ANY] * nx,
        out_specs=[blk, blk, vec, vec] + [ANY] * nx,
        out_shape=[jax.ShapeDtypeStruct((T, d_inner), F32), jax.ShapeDtypeStruct((T, d_inner), BF16), vsh, vsh]
        + _sibling_shapes(exchange),
        scratch_shapes=_sibling_sems(nx) if nx else [], compiler_params=_cparams(),
    )(dyn, y, xbc, z, d_lane, ng, *exchange)
    return outs[:4], list(outs[4:])


def _rms_fwd(x, g, name):
    T, R = x.shape
    tm = _tile(T, (512, 256, 128))

    def body(x_ref, g_ref, y_ref):
        xv = x_ref[...]
        y = xv * lax.rsqrt(jnp.mean(xv * xv, axis=1, keepdims=True) + RMS_EPS) * g_ref[...]
        y_ref[...] = y.astype(BF16)

    row = pl.BlockSpec((tm, R), lambda i: (i, 0))
    vec = pl.BlockSpec((1, R), lambda i: (0, 0))
    return pl.pallas_call(
        body, name=name, grid=(T // tm,), in_specs=[row, vec], out_specs=row,
        out_shape=jax.ShapeDtypeStruct((T, R), BF16), compiler_params=_cparams(),
    )(x, g)


def _rms_bwd(dy, x, g, name):
    T, R = x.shape
    tm = _tile(T, (512, 256, 128))

    def body(dy_ref, x_ref, g_ref, dx_ref, dg_ref):
        @pl.when(pl.program_id(0) == 0)
        def _():
            dg_ref[...] = jnp.zeros_like(dg_ref)

        xv = x_ref[...]
        dyv = dy_ref[...]
        r = lax.rsqrt(jnp.mean(xv * xv, axis=1, keepdims=True) + RMS_EPS)
        dg_ref[...] += jnp.sum(dyv * xv * r, axis=0, keepdims=True)
        do = dyv * g_ref[...]
        dx = r * do - xv * (r * r * r) * jnp.mean(do * xv, axis=1, keepdims=True)
        dx_ref[...] = dx.astype(BF16)

    row = pl.BlockSpec((tm, R), lambda i: (i, 0))
    vec = pl.BlockSpec((1, R), lambda i: (0, 0))
    return pl.pallas_call(
        body, name=name, grid=(T // tm,), in_specs=[row, row, vec], out_specs=[row, vec],
        out_shape=[jax.ShapeDtypeStruct((T, R), BF16), jax.ShapeDtypeStruct((1, R), F32)],
        compiler_params=_cparams(),
    )(dy, x, g)


def _swap_halves(x):
    half = MLA_ROPE // 2
    return jnp.where(_iota(x.shape, 1) < half, pltpu.roll(x, LANES - half, 1), pltpu.roll(x, half, 1))


def _rope(x, cc, ss, *, transpose, sum_heads=False, name):
    T, W = x.shape
    nh = W // LANES
    tm = _tile(T, (512, 256, 128))
    n_out = 1 if sum_heads else nh

    def body(x_ref, cc_ref, ss_ref, o_ref):
        ccv = cc_ref[...]
        ssv = ss_ref[...]
        if sum_heads:
            xv = x_ref[:, 0:LANES]
            for hh in range(1, nh):
                xv = xv + x_ref[:, hh * LANES:(hh + 1) * LANES]
            heads = [xv]
        else:
            heads = [x_ref[:, hh * LANES:(hh + 1) * LANES] for hh in range(nh)]
        for hh, xv in enumerate(heads):
            if transpose:
                r = xv * ccv + _swap_halves(xv * ssv)
            else:
                r = xv * ccv + _swap_halves(xv) * ssv
            r = jnp.where(_iota(r.shape, 1) < MLA_ROPE, r, 0.0)
            o_ref[:, hh * LANES:(hh + 1) * LANES] = r.astype(BF16)

    return pl.pallas_call(
        body, name=name, grid=(T // tm,),
        in_specs=[pl.BlockSpec((tm, W), lambda i: (i, 0)), pl.BlockSpec((tm, LANES), lambda i: (i, 0)),
                  pl.BlockSpec((tm, LANES), lambda i: (i, 0))],
        out_specs=pl.BlockSpec((tm, n_out * LANES), lambda i: (i, 0)),
        out_shape=jax.ShapeDtypeStruct((T, n_out * LANES), BF16), compiler_params=_cparams(),
    )(x, cc, ss)


ATT_BLOCK = 512
ATT_SUB = 256
LOG2E = 1.4426950408889634


def _att_geometry(S):
    tb = _tile(S, (ATT_BLOCK, 256))
    return tb, S // tb, tb // ATT_SUB


def _heads_per_step(nh):
    return 2 if nh % 2 == 0 else 1


def _att_scores(a, b, diag, kv_major, off):
    s = _bdot(a, b, NT)
    if diag:
        q_ax, k_ax = (1, 0) if kv_major else (0, 1)
        s = jnp.where(_iota(s.shape, k_ax) <= _iota(s.shape, q_ax) + off, s, NEG_BIG)
    return s


def _attention_fwd(qn, qr, kn, kr, v, S, scale, name):
    T, W = qn.shape
    nh = W // LANES
    Bn = T // S
    tb, n, C = _att_geometry(S)
    pairs = [(i, j) for i in range(n) for j in range(i + 1)]
    it = jnp.asarray([p[0] for p in pairs], jnp.int32)
    jt = jnp.asarray([p[1] for p in pairs], jnp.int32)
    c2 = scale * LOG2E

    HP = _heads_per_step(nh)

    def body(it_ref, jt_ref, qn_ref, qr_ref, kn_ref, kr_ref, v_ref, o_ref, ob_ref, lse_ref, m_sc, l_sc, acc):
        p = pl.program_id(2)
        i = it_ref[p]
        j = jt_ref[p]

        @pl.when(j == 0)
        def _():
            m_sc[...] = jnp.full_like(m_sc, NEG_BIG)
            l_sc[...] = jnp.zeros_like(l_sc)
            acc[...] = jnp.zeros_like(acc)

        def step(diag):
            for hh in range(HP):
                hs = pl.ds(hh * LANES, LANES)
                qc = jnp.concatenate([qn_ref[:, hs], qr_ref[:, hs]], axis=1)
                kc = jnp.concatenate([kn_ref[:, hs], kr_ref[...]], axis=1)
                st = _att_scores(kc, qc, diag, True, 0)
                m_old = m_sc[hh]
                m_new = jnp.maximum(m_old, jnp.max(st, axis=0, keepdims=True))
                pt = jnp.exp2((st - m_new) * c2)
                corr = jnp.exp2((m_old - m_new) * c2)
                l_sc[hh] = corr * l_sc[hh] + jnp.sum(pt, axis=0, keepdims=True)
                acc[hh] = corr * acc[hh] + _bdot(v_ref[:, hs].T, pt, NN)
                m_sc[hh] = m_new

        @pl.when(j < i)
        def _():
            step(False)

        @pl.when(j == i)
        def _():
            step(True)
            for hh in range(HP):
                hs = pl.ds(hh * LANES, LANES)
                ov = (acc[hh] / l_sc[hh]).T
                o_ref[:, hs] = ov
                ob_ref[:, hs] = ov.astype(BF16)
                lse_row = m_sc[hh] * scale + jnp.log(l_sc[hh])
                lse_ref[:, hs] = jnp.broadcast_to(lse_row, (LANES, tb)).T

    qspec = pl.BlockSpec((tb, HP * LANES), lambda b, h, p, it_, jt_: (b * n + it_[p], h))
    kspec = pl.BlockSpec((tb, HP * LANES), lambda b, h, p, it_, jt_: (b * n + jt_[p], h))
    krspec = pl.BlockSpec((tb, LANES), lambda b, h, p, it_, jt_: (b * n + jt_[p], 0))
    return pl.pallas_call(
        body, name=name,
        grid_spec=pltpu.PrefetchScalarGridSpec(
            num_scalar_prefetch=2, grid=(Bn, nh // HP, len(pairs)),
            in_specs=[qspec, qspec, kspec, krspec, kspec], out_specs=[qspec, qspec, qspec],
            scratch_shapes=[pltpu.VMEM((HP, 1, tb), F32), pltpu.VMEM((HP, 1, tb), F32),
                            pltpu.VMEM((HP, LANES, tb), F32)]),
        out_shape=[jax.ShapeDtypeStruct((T, W), F32), jax.ShapeDtypeStruct((T, W), BF16),
                   jax.ShapeDtypeStruct((T, W), F32)],
        compiler_params=_cparams(),
    )(it, jt, qn, qr, kn, kr, v)


def _attention_dq(qn, qr, kn, kr, v, o, do, lse, S, scale, name):
    T, W = qn.shape
    nh = W // LANES
    Bn = T // S
    tb, n, C = _att_geometry(S)
    pairs = [(i, j) for i in range(n) for j in range(i + 1)]
    it = jnp.asarray([p[0] for p in pairs], jnp.int32)
    jt = jnp.asarray([p[1] for p in pairs], jnp.int32)
    c2 = scale * LOG2E
    HP = _heads_per_step(nh)

    def body(it_ref, jt_ref, qn_ref, qr_ref, kn_ref, kr_ref, v_ref, o_ref, do_ref, lse_ref,
             dqn_ref, dqr_ref, acc, di_sc, lse_sc):
        p = pl.program_id(2)
        i = it_ref[p]
        j = jt_ref[p]

        @pl.when(j == 0)
        def _():
            acc[...] = jnp.zeros_like(acc)
            for hh in range(HP):
                hs = pl.ds(hh * LANES, LANES)
                di_sc[hh] = jnp.sum(do_ref[:, hs] * o_ref[:, hs], axis=1, keepdims=True)
                lse_sc[hh] = jnp.max(lse_ref[:, hs], axis=1, keepdims=True) * LOG2E

        def step(diag):
            for hh in range(HP):
                hs = pl.ds(hh * LANES, LANES)
                qc = jnp.concatenate([qn_ref[:, hs], qr_ref[:, hs]], axis=1)
                for c in range(C):
                    r0 = c * ATT_SUB if diag else 0
                    rows = pl.ds(r0, tb - r0)
                    ks = pl.ds(c * ATT_SUB, ATT_SUB)
                    kc = jnp.concatenate([kn_ref[ks, hs], kr_ref[ks, :]], axis=1)
                    s = _att_scores(qc[r0:], kc, diag, False, 0)
                    pr = jnp.exp2(s * c2 - lse_sc[hh, rows, :])
                    dp = _bdot(do_ref[rows, hs], v_ref[ks, hs], NT)
                    ds = pr * (dp - di_sc[hh, rows, :]) * scale
                    acc[hh, rows, :] += _bdot(ds, kc, NN)

        @pl.when(j < i)
        def _():
            step(False)

        @pl.when(j == i)
        def _():
            step(True)
            for hh in range(HP):
                hs = pl.ds(hh * LANES, LANES)
                dqn_ref[:, hs] = acc[hh, :, 0:LANES].astype(BF16)
                dqr_ref[:, hs] = acc[hh, :, LANES:2 * LANES]

    qspec = pl.BlockSpec((tb, HP * LANES), lambda b, h, p, it_, jt_: (b * n + it_[p], h))
    kspec = pl.BlockSpec((tb, HP * LANES), lambda b, h, p, it_, jt_: (b * n + jt_[p], h))
    krspec = pl.BlockSpec((tb, LANES), lambda b, h, p, it_, jt_: (b * n + jt_[p], 0))
    return pl.pallas_call(
        body, name=name,
        grid_spec=pltpu.PrefetchScalarGridSpec(
            num_scalar_prefetch=2, grid=(Bn, nh // HP, len(pairs)),
            in_specs=[qspec, qspec, kspec, krspec, kspec, qspec, qspec, qspec], out_specs=[qspec, qspec],
            scratch_shapes=[pltpu.VMEM((HP, tb, 2 * LANES), F32), pltpu.VMEM((HP, tb, 1), F32),
                            pltpu.VMEM((HP, tb, 1), F32)]),
        out_shape=[jax.ShapeDtypeStruct((T, W), BF16), jax.ShapeDtypeStruct((T, W), F32)],
        compiler_params=_cparams(),
    )(it, jt, qn, qr, kn, kr, v, o, do, lse)


def _attention_dkv(qn, qr, kn, kr, v, o, do, lse, S, scale, name):
    T, W = qn.shape
    nh = W // LANES
    Bn = T // S
    tb, n, C = _att_geometry(S)
    HP = _heads_per_step(nh)
    triples = [(j, h, i) for j in range(n) for h in range(nh // HP) for i in range(j, n)]
    jt = jnp.asarray([t[0] for t in triples], jnp.int32)
    ht = jnp.asarray([t[1] for t in triples], jnp.int32)
    it = jnp.asarray([t[2] for t in triples], jnp.int32)
    c2 = scale * LOG2E

    def as_row(col):
        return jnp.broadcast_to(col, (ATT_SUB, LANES)).T[0:1, :]

    def body(jt_ref, ht_ref, it_ref, qn_ref, qr_ref, kn_ref, kr_ref, v_ref, o_ref, do_ref, lse_ref,
             dkn_ref, dv_ref, dkr_ref, akc, av):
        p = pl.program_id(1)
        j = jt_ref[p]
        h = ht_ref[p]
        i = it_ref[p]

        @pl.when(jnp.logical_and(h == 0, i == j))
        def _():
            dkr_ref[...] = jnp.zeros_like(dkr_ref)

        @pl.when(i == j)
        def _():
            akc[...] = jnp.zeros_like(akc)
            av[...] = jnp.zeros_like(av)

        def step(diag):
            for hh in range(HP):
                hs = pl.ds(hh * LANES, LANES)
                kc = jnp.concatenate([kn_ref[:, hs], kr_ref[...]], axis=1)
                for c in range(C):
                    rows = pl.ds(c * ATT_SUB, ATT_SUB)
                    k1 = (c + 1) * ATT_SUB if diag else tb
                    ks = pl.ds(0, k1)
                    qc = jnp.concatenate([qn_ref[rows, hs], qr_ref[rows, hs]], axis=1)
                    st = _att_scores(kc[:k1], qc, diag, True, c * ATT_SUB)
                    dov = do_ref[rows, hs]
                    lse_row = as_row(jnp.max(lse_ref[rows, hs], axis=1, keepdims=True)) * LOG2E
                    di_row = as_row(jnp.sum(dov * o_ref[rows, hs], axis=1, keepdims=True))
                    pt = jnp.exp2(st * c2 - lse_row)
                    dst = pt * (_bdot(v_ref[ks, hs], dov, NT) - di_row) * scale
                    av[ks, hs] += _bdot(pt, dov, NN)
                    akc[hh, ks, :] += _bdot(dst, qc, NN)

        @pl.when(i > j)
        def _():
            step(False)

        @pl.when(i == j)
        def _():
            step(True)

        @pl.when(i == n - 1)
        def _():
            for hh in range(HP):
                dkn_ref[:, pl.ds(hh * LANES, LANES)] = akc[hh, :, 0:LANES].astype(BF16)
                dkr_ref[...] += akc[hh, :, LANES:2 * LANES]
            dv_ref[...] = av[...].astype(BF16)

    qspec = pl.BlockSpec((tb, HP * LANES), lambda b, p, jt_, ht_, it_: (b * n + it_[p], ht_[p]))
    kspec = pl.BlockSpec((tb, HP * LANES), lambda b, p, jt_, ht_, it_: (b * n + jt_[p], ht_[p]))
    krspec = pl.BlockSpec((tb, LANES), lambda b, p, jt_, ht_, it_: (b * n + jt_[p], 0))
    return pl.pallas_call(
        body, name=name,
        grid_spec=pltpu.PrefetchScalarGridSpec(
            num_scalar_prefetch=3, grid=(Bn, len(triples)),
            in_specs=[qspec, qspec, kspec, krspec, kspec, qspec, qspec, qspec],
            out_specs=[kspec, kspec, krspec],
            scratch_shapes=[pltpu.VMEM((HP, tb, 2 * LANES), F32), pltpu.VMEM((tb, HP * LANES), F32)]),
        out_shape=[jax.ShapeDtypeStruct((T, W), BF16), jax.ShapeDtypeStruct((T, W), BF16),
                   jax.ShapeDtypeStruct((T, LANES), F32)],
        compiler_params=_cparams(),
    )(jt, ht, it, qn, qr, kn, kr, v, o, do, lse)


def _pad_cols(w, n):
    return jnp.pad(w, ((0, 0), (0, n - w.shape[1])))


def _local_step(x, positions, w, target, late=None, reduce_early=None, reduce_last=None):
    w = dict(w)
    Bn, S, D = x.shape
    T = Bn * S
    depth = w['ln_mix_g'].shape[0]
    alpha = (2 * depth) ** 0.25
    d_inner = w['ssd_norm_g'].shape[1]
    n_heads_ssd = d_inner // SSD_HEAD_DIM
    G = SSD_N_GROUPS
    gn = G * SSD_D_STATE
    conv_dim = d_inner + 2 * gn
    hpg = n_heads_ssd // G
    nh = D // MLA_NOPE
    kv_rank = w['kv_norm_g'].shape[0]
    q_rank = w['q_norm_g'].shape[1]
    H = w['ffn_conv_b'].shape[1] // 2
    scale = (MLA_NOPE + MLA_ROPE) ** -0.5
    assert S % SSD_CHUNK == 0 and hpg % 2 == 0 and SSD_D_STATE == LANES and n_heads_ssd <= LANES

    X = x.reshape(T, D)
    tgt = target.reshape(T, D)

    inv_freq = 1.0 / (ROPE_THETA ** (jnp.arange(0, MLA_ROPE, 2, dtype=F32) / MLA_ROPE))
    ang = positions.reshape(T).astype(F32)[:, None] * inv_freq
    cos, sin = jnp.cos(ang), jnp.sin(ang)
    zpad = jnp.zeros((T, LANES - MLA_ROPE), F32)
    cc = jnp.concatenate([cos, cos, zpad], axis=1)
    ss = jnp.concatenate([-sin, sin, zpad], axis=1)

    w_in = w['ssd_in_proj'][0]
    w_z = w_in[:, :d_inner]
    w_xbc = w_in[:, d_inner:d_inner + conv_dim]
    w_dt = _pad_cols(w_in[:, d_inner + conv_dim:], LANES)
    ssd_cw = w['ssd_conv_w'][0]
    ssd_cb = w['ssd_conv_b']
    dt_bias = _pad_cols(w['ssd_dt_bias'], LANES)
    a_log = _pad_cols(w['ssd_A_log'], LANES)
    d_lane = jnp.repeat(w['ssd_D'], SSD_HEAD_DIM, axis=1)
    ssd_ng = w['ssd_norm_g']

    Xb = X.astype(BF16)
    z = _mm(Xb, w_z, name='ssd_in_z')
    xbc_pre = _mm(Xb, w_xbc, name='ssd_in_xbc')
    dt_pre = _mm(Xb, w_dt, name='ssd_in_dt')
    xbc, xbc_conv = _ssd_conv_fwd(xbc_pre, ssd_cw, ssd_cb, S, name='ssd_conv')
    dtb, cumb = _ssd_dt_prep(dt_pre, dt_bias, a_log, n_heads_ssd, name='ssd_dt_prep')
    y_scan, states, gathered = _scan_fwd(xbc, dtb, cumb, S, d_inner, name='ssd_scan',
                                         gather=late[0] if late else ())
    if late:
        w.update(late[1](gathered))

    w_out = w['ssd_out_proj'][0]
    w_kvc = w['kv_down_proj'][:, :kv_rank]
    w_kvr = _pad_cols(w['kv_down_proj'][:, kv_rank:], LANES)
    kv_g = w['kv_norm_g'].reshape(1, kv_rank)
    w_uk = w['kv_up_k']
    w_uv = w['kv_up_v']
    w_qd = w['q_down_proj'][0]
    q_g = w['q_norm_g']
    qu = w['q_up_proj'][0].reshape(q_rank, nh, MLA_NOPE + MLA_ROPE)
    w_qn = qu[:, :, :MLA_NOPE].reshape(q_rank, nh * LANES)
    w_qr = jnp.pad(qu[:, :, MLA_NOPE:], ((0, 0), (0, 0), (0, LANES - MLA_ROPE))).reshape(q_rank, nh * LANES)
    w_ao = w['attn_out_proj'][0]

    def ffn_parts(i):
        return dict(wu=w['ffn_up'][i], cw=w['ffn_conv_w'][i], cb=w['ffn_conv_b'][i:i + 1], wd=w['ffn_down'][i])

    def ln(name, i):
        return w[name][i:i + 1]

    def ffn_fwd(hb, i):
        f = ffn_parts(i)
        u = _mm(hb, f['wu'], name=f'ffn{i}_up')
        a = _ffn_act_fwd(u, f['cw'], f['cb'], S, name=f'ffn{i}_act')
        return _mm(a, f['wd'], name=f'ffn{i}_down'), (u, a)

    yn = _gate_fwd(y_scan, xbc, z, d_lane, ssd_ng, d_inner, name='ssd_gate')
    mix0 = _mm(yn, w_out, name='ssd_out')
    h1, s1, h1b = _ln_fwd(X, mix0, ln('ln_mix_g', 0), ln('ln_mix_b', 0), alpha, name='ln_mix0')
    ff0, ffn0_saved = ffn_fwd(h1b, 0)
    h2, s2, h2b = _ln_fwd(h1, ff0, ln('ln_ffn_g', 0), ln('ln_ffn_b', 0), alpha, name='ln_ffn0')

    ckv = _mm(h2b, w_kvc, name='kv_down_c')
    kr_pre = _mm(h2b, w_kvr, name='kv_down_r')
    ckvn = _rms_fwd(ckv, kv_g, name='kv_norm')
    kr = _rope(kr_pre, cc, ss, transpose=False, name='k_rope')
    kn = _mm(ckvn, w_uk, out_dtype=BF16, name='kv_up_k')
    v = _mm(ckvn, w_uv, out_dtype=BF16, name='kv_up_v')
    cq_pre = _mm(h2b, w_qd, name='q_down')
    cq = _rms_fwd(cq_pre, q_g, name='q_norm')
    qn = _mm(cq, w_qn, out_dtype=BF16, name='q_up_n')
    qr_pre = _mm(cq, w_qr, name='q_up_r')
    qr = _rope(qr_pre, cc, ss, transpose=False, name='q_rope')
    o, ob, lse = _attention_fwd(qn, qr, kn, kr, v, S, scale, name='attn_fwd')
    mix1 = _mm(ob, w_ao, name='attn_out')
    h3, s3, h3b = _ln_fwd(h2, mix1, ln('ln_mix_g', 1), ln('ln_mix_b', 1), alpha, name='ln_mix1')
    ff1, ffn1_saved = ffn_fwd(h3b, 1)
    h4, s4, _ = _ln_fwd(h3, ff1, ln('ln_ffn_g', 1), ln('ln_ffn_b', 1), alpha, name='ln_ffn1')
    sq, dh4 = _loss_head(h4, tgt, name='loss_head')

    g = {}

    def ffn_bwd(ds, dsb, hb_in, saved, i):
        f = ffn_parts(i)
        u, a = saved
        d_wd = _mm(a, dsb, ta=True, name=f'ffn{i}_down_dw')
        da = _mm(dsb, f['wd'], tb=True, out_dtype=BF16, name=f'ffn{i}_down_dx')
        dug, duv, dwg, dwv, dbg, dbv = _ffn_act_bwd(u, da, f['cw'], f['cb'], S, name=f'ffn{i}_act_bwd')
        d_wu = _mm(hb_in, dug, ta=True, name=f'ffn{i}_up_g_dw', into=lax.empty((D, 2 * H), F32))
        d_wu = _mm(hb_in, duv, ta=True, name=f'ffn{i}_up_v_dw', into=d_wu, into_col=H)
        dh = _mm(dug, f['wu'], tb=True, add=ds, add_scale=alpha, name=f'ffn{i}_up_g_dx')
        dh = _mm(duv, f['wu'], tb=True, add=dh, name=f'ffn{i}_up_v_dx', b_koff=H)
        return dh, d_wd, d_wu, jnp.concatenate([dwg, dwv], axis=1), jnp.concatenate([dbg, dbv], axis=1)

    ds4, ds4b, dg_f1, db_f1 = _ln_bwd(dh4, s4, ln('ln_ffn_g', 1), name='ln_ffn1_bwd')
    dh3, d_wd1, d_wu1, d_fcw1, d_fcb1 = ffn_bwd(ds4, ds4b, h3b, ffn1_saved, 1)
    ds3, ds3b, dg_m1, db_m1 = _ln_bwd(dh3, s3, ln('ln_mix_g', 1), name='ln_mix1_bwd')

    g['attn_out_proj'] = _mm(ob, ds3b, ta=True, name='attn_out_dw')[None]
    do = _mm(ds3b, w_ao, tb=True, name='attn_out_dx')
    dqn, dqr = _attention_dq(qn, qr, kn, kr, v, o, do, lse, S, scale, name='attn_bwd_dq')
    dkn, dv, dkr = _attention_dkv(qn, qr, kn, kr, v, o, do, lse, S, scale, name='attn_bwd_dkv')
    dqr_pre = _rope(dqr, cc, ss, transpose=True, name='q_rope_bwd')
    dkr_pre = _rope(dkr, cc, ss, transpose=True, name='k_rope_bwd')

    d_wqn = _mm(cq, dqn, ta=True, name='q_up_n_dw').reshape(q_rank, nh, LANES)
    d_wqr = _mm(cq, dqr_pre, ta=True, name='q_up_r_dw').reshape(q_rank, nh, LANES)[:, :, :MLA_ROPE]
    g['q_up_proj'] = jnp.concatenate([d_wqn, d_wqr], axis=2).reshape(1, q_rank, nh * (MLA_NOPE + MLA_ROPE))
    dcq = _mm(dqn, w_qn, tb=True, name='q_up_n_dx')
    dcq = _mm(dqr_pre, w_qr, tb=True, add=dcq, name='q_up_r_dx')
    dcq_pre, d_qg = _rms_bwd(dcq, cq_pre, q_g, name='q_norm_bwd')
    g['q_norm_g'] = d_qg
    g['q_down_proj'] = _mm(h2b, dcq_pre, ta=True, name='q_down_dw')[None]

    g['kv_up_k'] = _mm(ckvn, dkn, ta=True, name='kv_up_k_dw')
    g['kv_up_v'] = _mm(ckvn, dv, ta=True, name='kv_up_v_dw')
    dckvn = _mm(dkn, w_uk, tb=True, name='kv_up_k_dx')
    dckvn = _mm(dv, w_uv, tb=True, add=dckvn, name='kv_up_v_dx')
    dckv, d_kvg = _rms_bwd(dckvn, ckv, kv_g, name='kv_norm_bwd')
    g['kv_norm_g'] = d_kvg.reshape(kv_rank)
    g['kv_down_proj'] = jnp.concatenate(
        [_mm(h2b, dckv, ta=True, name='kv_down_c_dw'),
         _mm(h2b, dkr_pre, ta=True, name='kv_down_r_dw')[:, :MLA_ROPE]], axis=1)

    dh2 = _mm(dcq_pre, w_qd, tb=True, add=ds3, add_scale=alpha, name='q_down_dx')
    dh2 = _mm(dckv, w_kvc, tb=True, add=dh2, name='kv_down_c_dx')
    dh2 = _mm(dkr_pre, w_kvr, tb=True, add=dh2, name='kv_down_r_dx')

    ds2, ds2b, dg_f0, db_f0 = _ln_bwd(dh2, s2, ln('ln_ffn_g', 0), name='ln_ffn0_bwd')
    dh1, d_wd0, d_wu0, d_fcw0, d_fcb0 = ffn_bwd(ds2, ds2b, h1b, ffn0_saved, 0)
    ds1, ds1b, dg_m0, db_m0 = _ln_bwd(dh1, s1, ln('ln_mix_g', 0), name='ln_mix0_bwd')

    g['ssd_out_proj'] = _mm(yn, ds1b, ta=True, name='ssd_out_dw')[None]
    dyn = _mm(ds1b, w_out, tb=True, out_dtype=BF16, name='ssd_out_dx')
    g['ffn_up'] = jnp.stack([d_wu0, d_wu1])
    g['ffn_down'] = jnp.stack([d_wd0, d_wd1])
    early_blocks = reduce_early[0](g) if reduce_early else ()
    (dy1, dz, d_ng, d_dl), from_sibling = _gate_bwd(dyn, y_scan, xbc, z, d_lane, ssd_ng, d_inner,
                                                    name='ssd_gate_bwd', exchange=early_blocks)
    (dxs, dB, dC, ddt_g, da_g), early = _scan_bwd(
        xbc, dtb, cumb, states, dy1, d_lane, S, d_inner, name='ssd_scan_bwd',
        exchange=reduce_early[1](early_blocks, from_sibling) if reduce_early else ())
    ddt_pre, d_bias, d_alog = _dt_bwd(ddt_g, da_g, dt_pre, dt_bias, a_log, name='ssd_dt_bwd')
    dxbc_pre, d_scw, d_scb = _ssd_conv_bwd(xbc_pre, xbc_conv, [dxs, dB, dC], ssd_cw, S, name='ssd_conv_bwd')
    g['ssd_in_proj'] = jnp.concatenate(
        [_mm(Xb, dz, ta=True, name='ssd_in_z_dw'), _mm(Xb, dxbc_pre, ta=True, name='ssd_in_xbc_dw'),
         _mm(Xb, ddt_pre, ta=True, name='ssd_in_dt_dw')[:, :n_heads_ssd]], axis=1)[None]
    g['ssd_conv_w'] = d_scw[None]
    g['ssd_conv_b'] = d_scb
    g['ssd_norm_g'] = d_ng
    g['ffn_conv_w'] = jnp.stack([d_fcw0, d_fcw1])
    dx = _mm(dz, w_z, tb=True, add=ds1, add_scale=alpha, name='ssd_in_z_dx')
    dx = _mm(dxbc_pre, w_xbc, tb=True, add=dx, name='ssd_in_xbc_dx', exchange=reduce_last(g) if reduce_last else ())
    dx, last = dx if reduce_last else (dx, None)
    dx = _mm(ddt_pre, w_dt, tb=True, add=dx, name='ssd_in_dt_dx')

    g['ssd_dt_bias'] = d_bias[:, :n_heads_ssd]
    g['ssd_A_log'] = d_alog[:, :n_heads_ssd]
    g['ssd_D'] = jnp.sum(d_dl.reshape(1, n_heads_ssd, SSD_HEAD_DIM), axis=2)
    g['ffn_conv_b'] = jnp.concatenate([d_fcb0, d_fcb1], axis=0)
    g['ln_mix_g'] = jnp.concatenate([dg_m0, dg_m1], axis=0)
    g['ln_mix_b'] = jnp.concatenate([db_m0, db_m1], axis=0)
    g['ln_ffn_g'] = jnp.concatenate([dg_f0, dg_f1], axis=0)
    g['ln_ffn_b'] = jnp.concatenate([db_f0, db_f1], axis=0)
    return sq, dx.reshape(Bn, S, D), g, early, last


ANY = pl.BlockSpec(memory_space=pl.ANY)


def _all_gather(xs, name):
    n = len(xs)

    def body(*refs):
        start, forward, finish = _gather_plan(refs[:n], refs[n:2 * n], *refs[2 * n:])
        start()
        forward()
        finish()

    return pl.pallas_call(
        body, name=name, out_shape=_gather_shapes(xs), in_specs=[ANY] * n, out_specs=[ANY] * n,
        scratch_shapes=_gather_sems(n),
    )(*xs)


def _gather_shapes(xs):
    return [jax.ShapeDtypeStruct((N_DEV,) + a.shape, a.dtype) for a in xs]


def _gather_sems(n):
    return [pltpu.SemaphoreType.DMA((7 * n,)), pltpu.SemaphoreType.DMA((7 * n,)), pltpu.SemaphoreType.DMA((n,))]


def _gather_plan(x_refs, out_refs, send_sems, recv_sems, local_sems):
    n = len(x_refs)
    x, y, c = lax.axis_index("x"), lax.axis_index("y"), lax.axis_index("c")
    me, sibling = (x, y, c), (x, y, 1 - c)
    chips = [(1 - x, y), (x, 1 - y), (1 - x, 1 - y)]

    def rows(i, px, py, pc):
        return out_refs[i].at[4 * px + 2 * py + pc]

    def copy(i, k, block, to, own=False):
        return pltpu.make_async_remote_copy(
            src_ref=x_refs[i] if own else rows(i, *block), dst_ref=rows(i, *block),
            send_sem=send_sems.at[7 * i + k], recv_sem=recv_sems.at[7 * i + k],
            device_id=to, device_id_type=MESH)

    def mine():
        return [pltpu.make_async_copy(x_refs[i], rows(i, *me), local_sems.at[i]) for i in range(n)]

    def first():
        out = []
        for i in range(n):
            out.append(copy(i, 0, me, sibling, own=True))
            out += [copy(i, 1 + j, me, (*chip, c), own=True) for j, chip in enumerate(chips)]
        return out

    def passed():
        return [copy(i, 4 + j, (*chip, c), sibling) for j, chip in enumerate(chips) for i in range(n)]

    def start():
        for cp in mine() + first():
            cp.start()

    def forward():
        for j, chip in enumerate(chips):
            for i in range(n):
                copy(i, 1 + j, (*chip, c), me).wait_recv()
                copy(i, 4 + j, (*chip, c), sibling).start()

    def finish():
        for i in range(n):
            copy(i, 0, sibling, me).wait_recv()
            for j, chip in enumerate(chips):
                copy(i, 4 + j, (*chip, 1 - c), me).wait_recv()
        for cp in first() + passed():
            cp.wait_send()
        for cp in mine():
            cp.wait()

    return start, forward, finish


def _exchange_sibling(gs, name):
    n = len(gs)

    def body(*refs):
        start, finish = _sibling_plan(refs[:n], refs[n:2 * n], *refs[2 * n:])
        start()
        finish()

    return pl.pallas_call(
        body, name=name, out_shape=_sibling_shapes(gs), in_specs=[ANY] * n, out_specs=[ANY] * n,
        scratch_shapes=_sibling_sems(n),
    )(*gs)


def _sibling_shapes(gs):
    return [jax.ShapeDtypeStruct((4,) + g.shape[1:], g.dtype) for g in gs]


def _sibling_sems(n):
    return [pltpu.SemaphoreType.DMA((4 * n,)), pltpu.SemaphoreType.DMA((4 * n,))]


def _sibling_plan(g_refs, r_refs, send_sems, recv_sems):
    n = len(g_refs)
    x, y, c = lax.axis_index("x"), lax.axis_index("y"), lax.axis_index("c")

    def copies():
        return [pltpu.make_async_remote_copy(
            src_ref=g_refs[i].at[2 * k + (1 - c)], dst_ref=r_refs[i].at[k], send_sem=send_sems.at[4 * i + k],
            recv_sem=recv_sems.at[4 * i + k], device_id=(x, y, 1 - c), device_id_type=MESH)
            for i in range(n) for k in range(4)]

    def start():
        for cp in copies():
            cp.start()

    def finish():
        for cp in copies():
            cp.wait()

    return start, finish


def _chips_shapes(parts):
    return [jax.ShapeDtypeStruct((3,) + p.shape[1:], p.dtype) for p in parts]


def _chips_sems(n):
    return [pltpu.SemaphoreType.DMA((3 * n,)), pltpu.SemaphoreType.DMA((3 * n,))]


def _chips_plan(p_refs, r_refs, send_sems, recv_sems):
    n = len(p_refs)
    x, y, c = lax.axis_index("x"), lax.axis_index("y"), lax.axis_index("c")
    chips = [(1 - x, y), (x, 1 - y), (1 - x, 1 - y)]

    def copies():
        return [pltpu.make_async_remote_copy(
            src_ref=p_refs[i].at[2 * cx + cy], dst_ref=r_refs[i].at[j], send_sem=send_sems.at[3 * i + j],
            recv_sem=recv_sems.at[3 * i + j], device_id=(cx, cy, c), device_id_type=MESH)
            for i in range(n) for j, (cx, cy) in enumerate(chips)]

    def start():
        for cp in copies():
            cp.start()

    def finish():
        for cp in copies():
            cp.wait()

    return start, finish


def _row_tile(rows, cols):
    want = max(8, (256 * 1024) // cols)
    for t in range(min(rows, want) // 8 * 8, 7, -8):
        if rows % t == 0:
            return t
    return rows


def _add_sibling(gfull, r1, core, name):
    _, rows, lanes = gfull.shape
    tr = _row_tile(rows, lanes)

    def body(c_ref, g_ref, r_ref, o_ref):
        o_ref[...] = g_ref[...] + r_ref[...]

    return pl.pallas_call(
        body, name=name,
        grid_spec=pltpu.PrefetchScalarGridSpec(
            num_scalar_prefetch=1, grid=(4, rows // tr),
            in_specs=[pl.BlockSpec((1, tr, lanes), lambda k, r, c_ref: (2 * k + c_ref[0], r, 0)),
                      pl.BlockSpec((1, tr, lanes), lambda k, r, c_ref: (k, r, 0))],
            out_specs=pl.BlockSpec((1, tr, lanes), lambda k, r, c_ref: (k, r, 0))),
        out_shape=jax.ShapeDtypeStruct((4, rows, lanes), F32), compiler_params=_cparams(),
    )(core, gfull, r1)


def _adam_math(wv, gv, mv, vv):
    m = ADAM_B1 * mv + (1.0 - ADAM_B1) * gv
    v = ADAM_B2 * vv + (1.0 - ADAM_B2) * (gv * gv)
    m_hat = m / (1.0 - ADAM_B1 ** ADAM_STEP)
    v_hat = v / (1.0 - ADAM_B2 ** ADAM_STEP)
    delta = -ADAM_LR * (m_hat / (jnp.sqrt(v_hat) + ADAM_EPS) + ADAM_WD * wv)
    return delta, m, v


def _adam_sharded(part, r2, chip, wts, m, v, name):
    rows, lanes = wts.shape
    tr = _row_tile(rows, lanes)

    def body(k_ref, p_ref, r_ref, w_ref, m_ref, v_ref, g_ref, d_ref, mo_ref, vo_ref):
        gv = p_ref[0] + r_ref[0] + r_ref[1] + r_ref[2]
        delta, mn, vn = _adam_math(w_ref[...], gv, m_ref[...], v_ref[...])
        g_ref[...] = gv
        d_ref[...] = delta
        mo_ref[...] = mn
        vo_ref[...] = vn

    flat = pl.BlockSpec((tr, lanes), lambda r, k_ref: (r, 0))
    sh = jax.ShapeDtypeStruct((rows, lanes), F32)
    return pl.pallas_call(
        body, name=name,
        grid_spec=pltpu.PrefetchScalarGridSpec(
            num_scalar_prefetch=1, grid=(rows // tr,),
            in_specs=[pl.BlockSpec((1, tr, lanes), lambda r, k_ref: (k_ref[0], r, 0)),
                      pl.BlockSpec((3, tr, lanes), lambda r, k_ref: (0, r, 0)), flat, flat, flat],
            out_specs=[flat, flat, flat, flat]),
        out_shape=[sh, sh, sh, sh], compiler_params=_cparams(),
    )(chip, part, r2, wts, m, v)


def _adam_replicated(gall, wts, m, v, name):
    rows, lanes = wts.shape

    def body(ga_ref, w_ref, m_ref, v_ref, g_ref, d_ref, mo_ref, vo_ref):
        gv = ga_ref[0]
        for k in range(1, N_DEV):
            gv = gv + ga_ref[k]
        delta, mn, vn = _adam_math(w_ref[...], gv, m_ref[...], v_ref[...])
        g_ref[...] = gv
        d_ref[...] = delta
        mo_ref[...] = mn
        vo_ref[...] = vn

    sh = jax.ShapeDtypeStruct((rows, lanes), F32)
    return pl.pallas_call(body, name=name, out_shape=[sh, sh, sh, sh], compiler_params=_cparams())(gall, wts, m, v)


def _pack(arrs, dtype, row_mult):
    flat = jnp.concatenate([a.reshape(-1).astype(dtype) for a in arrs])
    n = flat.shape[0]
    unit = LANES * row_mult
    total = -(-n // unit) * unit
    return jnp.pad(flat, (0, total - n)).reshape(total // LANES, LANES)


def _unpack(flat2d, shapes):
    flat = flat2d.reshape(-1)
    out, off = [], 0
    for shp in shapes:
        n = math.prod(shp)
        out.append(flat[off:off + n].reshape(shp))
        off += n
    return out


def _unpack_gathered(gathered, shapes, axes):
    flat = gathered.reshape(N_DEV, -1)
    out, off = [], 0
    for shp, ax in zip(shapes, axes):
        n = math.prod(shp)
        seg = flat[:, off:off + n].reshape((N_DEV,) + tuple(shp))
        out.append(jnp.concatenate([seg[i] for i in range(N_DEV)], axis=ax))
        off += n
    return out


def _pack_chunks(fulls, axes, row_mult):
    per_dev = []
    for full, ax in zip(fulls, axes):
        parts = jnp.stack(jnp.split(full, N_DEV, axis=ax))
        per_dev.append(parts.reshape(N_DEV, -1))
    flat = jnp.concatenate(per_dev, axis=1)
    n = flat.shape[1]
    unit = LANES * row_mult
    total = -(-n // unit) * unit
    return jnp.pad(flat, ((0, 0), (0, total - n))).reshape(N_DEV, total // LANES, LANES)


def kernel(x, positions, ssd_in_proj, ssd_conv_w, ssd_conv_b, ssd_dt_bias, ssd_A_log, ssd_D, ssd_norm_g, ssd_out_proj, kv_down_proj, kv_norm_g, kv_up_k, kv_up_v, q_down_proj, q_norm_g, q_up_proj, attn_out_proj, ffn_up, ffn_conv_w, ffn_conv_b, ffn_down, ln_mix_g, ln_mix_b, ln_ffn_g, ln_ffn_b, loss_target, m_ssd_in_proj, m_ssd_conv_w, m_ssd_conv_b, m_ssd_dt_bias, m_ssd_A_log, m_ssd_D, m_ssd_norm_g, m_ssd_out_proj, m_kv_down_proj, m_kv_norm_g, m_kv_up_k, m_kv_up_v, m_q_down_proj, m_q_norm_g, m_q_up_proj, m_attn_out_proj, m_ffn_up, m_ffn_conv_w, m_ffn_conv_b, m_ffn_down, m_ln_mix_g, m_ln_mix_b, m_ln_ffn_g, m_ln_ffn_b, v_ssd_in_proj, v_ssd_conv_w, v_ssd_conv_b, v_ssd_dt_bias, v_ssd_A_log, v_ssd_D, v_ssd_norm_g, v_ssd_out_proj, v_kv_down_proj, v_kv_norm_g, v_kv_up_k, v_kv_up_v, v_q_down_proj, v_q_norm_g, v_q_up_proj, v_attn_out_proj, v_ffn_up, v_ffn_conv_w, v_ffn_conv_b, v_ffn_down, v_ln_mix_g, v_ln_mix_b, v_ln_ffn_g, v_ln_ffn_b):
    given = dict(locals())
    wl = {n: given[n] for n in WEIGHTS}
    ml = {n: given['m_' + n] for n in WEIGHTS}
    vl = {n: given['v_' + n] for n in WEIGHTS}
    sharded_axis = dict(SHARDED)
    big = [n for n, _ in SHARDED if n not in SHARDED_F32]
    small = [n for n, _ in SHARDED if n in SHARDED_F32]

    def as2d(a):
        return a.reshape(-1, a.shape[-1])

    def to_full(gathered, n):
        shp, ax = wl[n].shape, sharded_axis[n]
        moved = jnp.moveaxis(gathered.reshape((N_DEV,) + shp), 0, ax)
        return moved.reshape(shp[:ax] + (N_DEV * shp[ax],) + shp[ax + 1:])

    def to_chunks(full_grad, n):
        shp, ax = wl[n].shape, sharded_axis[n]
        split = full_grad.reshape(shp[:ax] + (N_DEV, shp[ax]) + shp[ax + 1:])
        return jnp.moveaxis(split, ax, 0).reshape((N_DEV,) + as2d(wl[n]).shape)

    first = [n for n in big if n in GATHERED_FIRST]
    late = [n for n in big if n not in GATHERED_FIRST]
    full = {n: wl[n] for n in REPLICATED}
    gathered = _all_gather([as2d(wl[n]).astype(BF16) for n in first] + [_pack([wl[n] for n in small], F32, 8)],
                           name='gather_weights')
    for n, gat in zip(first, gathered[:-1]):
        full[n] = to_full(gat, n)
    full.update(zip(small, _unpack_gathered(gathered[-1], [wl[n].shape for n in small],
                                            [sharded_axis[n] for n in small])))

    cx, cy, cc = lax.axis_index("x"), lax.axis_index("y"), lax.axis_index("c")
    core = jnp.reshape(cc, (1,)).astype(jnp.int32)
    chip = jnp.reshape(2 * cx + cy, (1,)).astype(jnp.int32)
    early_names = [n for n in big if n not in REDUCED_LAST]
    last_names = [n for n in big if n in REDUCED_LAST]
    parts = {}

    def add_sibling(names, chunked, received):
        parts.update((n, _add_sibling(gf, r, core, name=f'grads_add_sibling_{n}'))
                     for n, gf, r in zip(names, chunked, received))
        return [parts[n] for n in names]

    def early_blocks(grads):
        return [to_chunks(grads[n], n) for n in early_names]

    def early_parts(chunked, received):
        return add_sibling(early_names, chunked, received)

    def reduce_last(grads):
        chunked = [to_chunks(grads[n], n) for n in last_names]
        chunked.append(_pack_chunks([grads[n] for n in small], [sharded_axis[n] for n in small], 8))
        return add_sibling(last_names + ['small'], chunked, _exchange_sibling(chunked, name='grads_to_sibling_last'))

    sq, grad_x, grads, r2_early, r2_last = _local_step(
        x, positions, full, loss_target,
        late=([as2d(wl[n]).astype(BF16) for n in late], lambda gats: {n: to_full(g, n) for n, g in zip(late, gats)}),
        reduce_early=(early_blocks, early_parts), reduce_last=reduce_last)
    loss = lax.psum(0.5 * jnp.sum(sq) / x.shape[-1], ("x", "y", "c"))

    r2 = dict(zip(early_names, r2_early))
    r2.update(zip(last_names + ['small'], r2_last))
    res = {}
    kinds = ('grad', 'delta', 'new_m', 'new_v')
    for n in big:
        outs = _adam_sharded(parts[n], r2[n], chip, as2d(wl[n]), as2d(ml[n]), as2d(vl[n]), name=f'adamw_{n}')
        for kind, a in zip(kinds, outs):
            res[kind, n] = a.reshape(wl[n].shape)
    outs = _adam_sharded(parts['small'], r2['small'], chip, _pack([wl[n] for n in small], F32, 8),
                         _pack([ml[n] for n in small], F32, 8), _pack([vl[n] for n in small], F32, 8),
                         name='adamw_small_sharded')
    for kind, packed in zip(kinds, outs):
        for n, a in zip(small, _unpack(packed, [wl[n].shape for n in small])):
            res[kind, n] = a

    rep = list(REPLICATED)
    rshapes = [wl[n].shape for n in rep]
    gall, = _all_gather([_pack([grads[n] for n in rep], F32, 8)], name='gather_replicated_grads')
    outs = _adam_replicated(gall, _pack([wl[n] for n in rep], F32, 8), _pack([ml[n] for n in rep], F32, 8),
                            _pack([vl[n] for n in rep], F32, 8), name='adamw_replicated')
    for kind, packed in zip(('grad', 'delta', 'new_m', 'new_v'), outs):
        for n, a in zip(rep, _unpack(packed, rshapes)):
            res[kind, n] = a

    return (loss, grad_x, *[res['grad', n] for n in WEIGHTS], *[res['delta', n] for n in WEIGHTS],
            *[res['new_m', n] for n in WEIGHTS], *[res['new_v', n] for n in WEIGHTS])
```

```python
import math

import jax
import jax.numpy as jnp
from jax import lax
from jax.experimental import pallas as pl
from jax.experimental.pallas import tpu as pltpu

F32 = jnp.float32
BF16 = jnp.bfloat16
MESH = pl.DeviceIdType.MESH

N_DEV = 8
LANES = 128
MM_TILES = (1024, 1408, 896, 512, 384, 256, 128)
MM_VMEM_BUDGET = 38 * 1024 * 1024
MM_VMEM_LIMIT = 56 * 1024 * 1024
VMEM_LIMIT = 32 * 1024 * 1024
LN_EPS = 1e-5
RMS_EPS = 1e-6
SSD_HEAD_DIM = 64
SSD_N_GROUPS = 8
SSD_D_STATE = 128
SSD_CHUNK = 128
MLA_NOPE = 128
MLA_ROPE = 64
ROPE_THETA = 10000.0
ADAM_LR = 0.001
ADAM_B1 = 0.9
ADAM_B2 = 0.999
ADAM_EPS = 1e-08
ADAM_WD = 0.01
ADAM_STEP = 10
NEG_BIG = -1e30

SHARDED = (('ssd_in_proj', 2), ('ssd_conv_w', 2), ('ssd_conv_b', 1), ('ssd_norm_g', 1), ('ssd_out_proj', 1),
           ('kv_down_proj', 0), ('kv_up_k', 1), ('kv_up_v', 1), ('q_down_proj', 1), ('q_up_proj', 2),
           ('attn_out_proj', 1), ('ffn_up', 2), ('ffn_conv_w', 2), ('ffn_down', 1))
SHARDED_F32 = ('ssd_conv_w', 'ssd_conv_b', 'ssd_norm_g', 'ffn_conv_w')
GATHERED_FIRST = ('ssd_in_proj',)
REDUCED_LAST = ('ssd_in_proj',)
REPLICATED = ('ssd_dt_bias', 'ssd_A_log', 'ssd_D', 'kv_norm_g', 'q_norm_g', 'ffn_conv_b',
              'ln_mix_g', 'ln_mix_b', 'ln_ffn_g', 'ln_ffn_b')
WEIGHTS = ('ssd_in_proj', 'ssd_conv_w', 'ssd_conv_b', 'ssd_dt_bias', 'ssd_A_log', 'ssd_D', 'ssd_norm_g',
           'ssd_out_proj', 'kv_down_proj', 'kv_norm_g', 'kv_up_k', 'kv_up_v', 'q_down_proj', 'q_norm_g',
           'q_up_proj', 'attn_out_proj', 'ffn_up', 'ffn_conv_w', 'ffn_conv_b', 'ffn_down', 'ln_mix_g',
           'ln_mix_b', 'ln_ffn_g', 'ln_ffn_b')


def _cparams(limit=None):
    return pltpu.CompilerParams(vmem_limit_bytes=VMEM_LIMIT if limit is None else limit)


def _tile(n, cands):
    for c in cands:
        if n % c == 0:
            return c
    return n


def _sigmoid(x):
    return 1.0 / (1.0 + jnp.exp(-x))


def _iota(shape, axis):
    return lax.broadcasted_iota(jnp.int32, shape, axis)


def _mm(a, b, *, ta=False, tb=False, add=None, add_scale=1.0, out_dtype=F32, name, b_koff=0, into=None, into_col=0,
        exchange=()):
    if ta:
        K, M = a.shape
    else:
        M, K = a.shape
    if tb:
        N, Kb = b.shape
    else:
        Kb, N = b.shape
    assert b_koff + K <= Kb, (a.shape, b.shape, ta, tb, b_koff)
    tm = _tile(M, MM_TILES)
    tn = _tile(N, MM_TILES)
    tk = K if K <= MM_TILES[0] and b_koff == 0 and Kb == K else _tile(K, MM_TILES)
    nk = K // tk
    assert b_koff % tk == 0 and (into is None or (into.shape[0] == M and into.dtype == out_dtype))
    kb = b_koff // tk

    def vmem_estimate(tm_, tn_):
        ins = 2 * (tm_ * tk * a.dtype.itemsize + tk * tn_ * b.dtype.itemsize)
        outs = tm_ * tn_ * (2 * jnp.dtype(out_dtype).itemsize + 4 + (4 if nk > 1 else 0))
        return ins + outs + (2 * tm_ * tn_ * add.dtype.itemsize if add is not None else 0)

    while vmem_estimate(tm, tn) > MM_VMEM_BUDGET:
        can_m, can_n = tm % (2 * LANES) == 0, tn % (2 * LANES) == 0
        if can_m and (tm >= tn or not can_n):
            tm //= 2
        elif can_n:
            tn //= 2
        else:
            break

    assert into_col % tn == 0
    jb = into_col // tn

    nx = len(exchange)
    n_in = (add is not None) + (into is not None)
    grid = (M // tm, N // tn, nk)

    def body(a_ref, b_ref, *rest):
        add_ref = rest[0] if add is not None else None
        o_ref = rest[n_in + nx]
        acc = rest[n_in + 2 * nx + 1] if nk > 1 else None
        k = pl.program_id(2)
        if nx:
            start, finish_exchange = _chips_plan(rest[n_in:n_in + nx], rest[n_in + nx + 1:n_in + 2 * nx + 1],
                                                 *rest[len(rest) - 2:])
            t, total = _grid_step(grid)
            pl.when(t == 0)(start)

        if ta:
            av = a_ref[...].astype(BF16).T
        else:
            av = a_ref[...].astype(BF16)
        bv = b_ref[...].astype(BF16)
        dn = (((1,), (1 if tb else 0,)), ((), ()))
        part = lax.dot_general(av, bv, dn, preferred_element_type=F32)

        def finish(r):
            if add is not None:
                r = r + add_scale * add_ref[...].astype(F32)
            o_ref[...] = r.astype(out_dtype)

        if nk == 1:
            finish(part)
        else:
            @pl.when(k == 0)
            def _():
                acc[...] = part

            @pl.when(k > 0)
            def _():
                acc[...] += part

            @pl.when(k == nk - 1)
            def _():
                finish(acc[...])

        if nx:
            pl.when(t == total - 1)(finish_exchange)

    a_spec = (pl.BlockSpec((tk, tm), lambda i, j, k: (k, i)) if ta
              else pl.BlockSpec((tm, tk), lambda i, j, k: (i, k)))
    b_spec = (pl.BlockSpec((tn, tk), lambda i, j, k: (j, k + kb)) if tb
              else pl.BlockSpec((tk, tn), lambda i, j, k: (k + kb, j)))
    in_specs = [a_spec, b_spec]
    args = [a, b]
    if add is not None:
        in_specs.append(pl.BlockSpec((tm, tn), lambda i, j, k: (i, j)))
        args.append(add)
    aliases = {}
    if into is not None:
        aliases = {len(args): 0}
        in_specs.append(pl.BlockSpec(memory_space=pl.ANY))
        args.append(into)
    any_spec = pl.BlockSpec(memory_space=pl.ANY)
    outs = pl.pallas_call(
        body, name=name, grid=grid,
        in_specs=in_specs + [any_spec] * nx,
        out_specs=[pl.BlockSpec((tm, tn), lambda i, j, k: (i, j + jb))] + [any_spec] * nx,
        out_shape=[jax.ShapeDtypeStruct((M, N) if into is None else into.shape, out_dtype)] + _chips_shapes(exchange),
        scratch_shapes=([pltpu.VMEM((tm, tn), F32)] if nk > 1 else []) + (_chips_sems(nx) if nx else []),
        input_output_aliases=aliases, compiler_params=_cparams(MM_VMEM_LIMIT),
    )(*args, *exchange)
    return (outs[0], list(outs[1:])) if nx else outs[0]


def _ln_fwd(h, mix, g, b, alpha, name):
    T, D = h.shape
    tm = _tile(T, (512, 256, 128))

    def body(h_ref, m_ref, g_ref, b_ref, y_ref, s_ref, yb_ref):
        s = alpha * h_ref[...] + m_ref[...]
        mu = jnp.mean(s, axis=1, keepdims=True)
        xc = s - mu
        var = jnp.mean(xc * xc, axis=1, keepdims=True)
        y = xc * lax.rsqrt(var + LN_EPS) * g_ref[...] + b_ref[...]
        y_ref[...] = y
        s_ref[...] = s
        yb_ref[...] = y.astype(BF16)

    row = pl.BlockSpec((tm, D), lambda i: (i, 0))
    vec = pl.BlockSpec((1, D), lambda i: (0, 0))
    return pl.pallas_call(
        body, name=name, grid=(T // tm,), in_specs=[row, row, vec, vec], out_specs=[row, row, row],
        out_shape=[jax.ShapeDtypeStruct((T, D), F32)] * 2 + [jax.ShapeDtypeStruct((T, D), BF16)],
        compiler_params=_cparams(),
    )(h, mix, g, b)


def _ln_bwd(dy, s, g, name):
    T, D = s.shape
    tm = _tile(T, (512, 256, 128))

    def body(dy_ref, s_ref, g_ref, ds_ref, dsb_ref, dg_ref, db_ref):
        @pl.when(pl.program_id(0) == 0)
        def _():
            dg_ref[...] = jnp.zeros_like(dg_ref)
            db_ref[...] = jnp.zeros_like(db_ref)

        sv = s_ref[...]
        dyv = dy_ref[...]
        mu = jnp.mean(sv, axis=1, keepdims=True)
        xc = sv - mu
        rstd = lax.rsqrt(jnp.mean(xc * xc, axis=1, keepdims=True) + LN_EPS)
        xhat = xc * rstd
        dxh = dyv * g_ref[...]
        m1 = jnp.mean(dxh, axis=1, keepdims=True)
        m2 = jnp.mean(dxh * xhat, axis=1, keepdims=True)
        ds = rstd * (dxh - m1 - xhat * m2)
        ds_ref[...] = ds
        dsb_ref[...] = ds.astype(BF16)
        dg_ref[...] += jnp.sum(dyv * xhat, axis=0, keepdims=True)
        db_ref[...] += jnp.sum(dyv, axis=0, keepdims=True)

    row = pl.BlockSpec((tm, D), lambda i: (i, 0))
    vec = pl.BlockSpec((1, D), lambda i: (0, 0))
    return pl.pallas_call(
        body, name=name, grid=(T // tm,), in_specs=[row, row, vec], out_specs=[row, row, vec, vec],
        out_shape=[jax.ShapeDtypeStruct((T, D), F32), jax.ShapeDtypeStruct((T, D), BF16),
                   jax.ShapeDtypeStruct((1, D), F32), jax.ShapeDtypeStruct((1, D), F32)],
        compiler_params=_cparams(),
    )(dy, s, g)


def _loss_head(y, target, name):
    T, D = y.shape
    tm = _tile(T, (512, 256, 128))

    def body(y_ref, t_ref, sq_ref, dy_ref):
        @pl.when(pl.program_id(0) == 0)
        def _():
            sq_ref[...] = jnp.zeros_like(sq_ref)

        e = y_ref[...] - t_ref[...]
        sq_ref[...] += jnp.sum(e * e, axis=0, keepdims=True)
        dy_ref[...] = e * (1.0 / D)

    row = pl.BlockSpec((tm, D), lambda i: (i, 0))
    vec = pl.BlockSpec((1, D), lambda i: (0, 0))
    return pl.pallas_call(
        body, name=name, grid=(T // tm,), in_specs=[row, row], out_specs=[vec, row],
        out_shape=[jax.ShapeDtypeStruct((1, D), F32), jax.ShapeDtypeStruct((T, D), F32)],
        compiler_params=_cparams(),
    )(y, target)


def _shift_down(x, j):
    if j == 0:
        return x
    return jnp.where(_iota(x.shape, 0) >= j, pltpu.roll(x, j, 0), 0.0)


def _shift_up(x, j):
    if j == 0:
        return x
    n = x.shape[0]
    return jnp.where(_iota(x.shape, 0) < n - j, pltpu.roll(x, n - j, 0), 0.0)


def _conv_val(x, w_ref, b_ref):
    width = w_ref.shape[0]
    y = b_ref[...] + w_ref[width - 1:width, :] * x
    for j in range(1, width):
        y = y + w_ref[width - 1 - j:width - j, :] * _shift_down(x, j)
    return y


def _conv_bwd_val(x, dc, w_ref, dw_ref, db_ref):
    width = w_ref.shape[0]
    dx = w_ref[width - 1:width, :] * dc
    dw_ref[width - 1:width, :] += jnp.sum(dc * x, axis=0, keepdims=True)
    for j in range(1, width):
        sj = _shift_up(dc, j)
        dx = dx + w_ref[width - 1 - j:width - j, :] * sj
        dw_ref[width - 1 - j:width - j, :] += jnp.sum(sj * x, axis=0, keepdims=True)
    db_ref[...] += jnp.sum(dc, axis=0, keepdims=True)
    return dx


def _ffn_specs(H, S, width, cb):
    cb = _tile(H, (cb, LANES))
    nC = H // cb
    return dict(
        g=pl.BlockSpec((S, cb), lambda j, b: (b, j)), v=pl.BlockSpec((S, cb), lambda j, b: (b, j + nC)),
        wg=pl.BlockSpec((width, cb), lambda j, b: (0, j)), wv=pl.BlockSpec((width, cb), lambda j, b: (0, j + nC)),
        bg=pl.BlockSpec((1, cb), lambda j, b: (0, j)), bv=pl.BlockSpec((1, cb), lambda j, b: (0, j + nC))), nC


def _ffn_act_fwd(u, cw, cb, S, name):
    T, H2 = u.shape
    H = H2 // 2
    sp, nC = _ffn_specs(H, S, cw.shape[0], 2 * LANES)

    def body(ug_ref, uv_ref, wg_ref, wv_ref, bg_ref, bv_ref, a_ref):
        g = _conv_val(ug_ref[...], wg_ref, bg_ref)
        v = _conv_val(uv_ref[...], wv_ref, bv_ref)
        a_ref[...] = (g * _sigmoid(g) * v).astype(BF16)

    return pl.pallas_call(
        body, name=name, grid=(nC, T // S),
        in_specs=[sp['g'], sp['v'], sp['wg'], sp['wv'], sp['bg'], sp['bv']], out_specs=sp['g'],
        out_shape=jax.ShapeDtypeStruct((T, H), BF16), compiler_params=_cparams(),
    )(u, u, cw, cw, cb, cb)


def _ffn_act_bwd(u, da, cw, cb, S, name):
    T, H2 = u.shape
    H = H2 // 2
    width = cw.shape[0]
    sp, nC = _ffn_specs(H, S, width, LANES)

    def body(ug_ref, uv_ref, da_ref, wg_ref, wv_ref, bg_ref, bv_ref,
             dug_ref, duv_ref, dwg_ref, dwv_ref, dbg_ref, dbv_ref):
        @pl.when(pl.program_id(1) == 0)
        def _():
            for r in (dwg_ref, dwv_ref, dbg_ref, dbv_ref):
                r[...] = jnp.zeros_like(r)

        def conv_bwd(x, dc, w_ref, dw_ref, db_ref):
            dx = w_ref[width - 1:width, :] * dc
            for j in range(1, width):
                dx = dx + w_ref[width - 1 - j:width - j, :] * _shift_up(dc, j)
            for j in range(width):
                dw_ref[width - 1 - j:width - j, :] += jnp.sum(dc * _shift_down(x, j), axis=0, keepdims=True)
            db_ref[...] += jnp.sum(dc, axis=0, keepdims=True)
            return dx

        xg = ug_ref[...]
        xv = uv_ref[...]
        g = _conv_val(xg, wg_ref, bg_ref)
        v = _conv_val(xv, wv_ref, bv_ref)
        dav = da_ref[...].astype(F32)
        sg = _sigmoid(g)
        dg = dav * v * sg * (1.0 + g * (1.0 - sg))
        dv = dav * g * sg
        dug_ref[...] = conv_bwd(xg, dg, wg_ref, dwg_ref, dbg_ref).astype(BF16)
        duv_ref[...] = conv_bwd(xv, dv, wv_ref, dwv_ref, dbv_ref).astype(BF16)

    act = jax.ShapeDtypeStruct((T, H), BF16)
    wsh = jax.ShapeDtypeStruct((width, H), F32)
    bsh = jax.ShapeDtypeStruct((1, H), F32)
    return pl.pallas_call(
        body, name=name, grid=(nC, T // S),
        in_specs=[sp['g'], sp['v'], sp['g'], sp['wg'], sp['wv'], sp['bg'], sp['bv']],
        out_specs=[sp['g'], sp['g'], sp['wg'], sp['wg'], sp['bg'], sp['bg']],
        out_shape=[act, act, wsh, wsh, bsh, bsh], compiler_params=_cparams(),
    )(u, u, da, cw, cw, cb, cb)


def _ssd_conv_fwd(x, cw, cb, S, name):
    T, C = x.shape
    Cb = _tile(C, (256, 128))
    width = cw.shape[0]

    def body(x_ref, w_ref, b_ref, y_ref, c_ref):
        c = _conv_val(x_ref[...], w_ref, b_ref)
        c_ref[...] = c
        y_ref[...] = c * _sigmoid(c)

    blk = pl.BlockSpec((S, Cb), lambda j, b: (b, j))
    wblk = pl.BlockSpec((width, Cb), lambda j, b: (0, j))
    bblk = pl.BlockSpec((1, Cb), lambda j, b: (0, j))
    return pl.pallas_call(
        body, name=name, grid=(C // Cb, T // S), in_specs=[blk, wblk, bblk], out_specs=[blk, blk],
        out_shape=[jax.ShapeDtypeStruct((T, C), F32)] * 2, compiler_params=_cparams(),
    )(x, cw, cb)


def _ssd_conv_bwd(x, c, dys, cw, S, name):
    T, C = x.shape
    Cb = _tile(C, (256, 128))
    width = cw.shape[0]
    assert all(d.shape[1] % Cb == 0 for d in dys) and sum(d.shape[1] for d in dys) == C

    def part(dy, j0, dx_prev, k):
        n = dy.shape[1] // Cb

        def body(x_ref, c_ref, dy_ref, w_ref, prev_ref, dx_ref, dw_ref, db_ref):
            @pl.when(pl.program_id(1) == 0)
            def _():
                dw_ref[...] = jnp.zeros_like(dw_ref)
                db_ref[...] = jnp.zeros_like(db_ref)

            cval = c_ref[...]
            sc = _sigmoid(cval)
            dc = dy_ref[...] * sc * (1.0 + cval * (1.0 - sc))
            dx_ref[...] = _conv_bwd_val(x_ref[...], dc, w_ref, dw_ref, db_ref).astype(BF16)

        wide = pl.BlockSpec((S, Cb), lambda j, b: (b, j + j0))
        return pl.pallas_call(
            body, name=f'{name}_{k}', grid=(n, T // S),
            in_specs=[wide, wide, pl.BlockSpec((S, Cb), lambda j, b: (b, j)),
                      pl.BlockSpec((width, Cb), lambda j, b: (0, j + j0)), pl.BlockSpec(memory_space=pl.ANY)],
            out_specs=[wide, pl.BlockSpec((width, Cb), lambda j, b: (0, j)), pl.BlockSpec((1, Cb), lambda j, b: (0, j))],
            out_shape=[jax.ShapeDtypeStruct((T, C), BF16), jax.ShapeDtypeStruct((width, n * Cb), F32),
                       jax.ShapeDtypeStruct((1, n * Cb), F32)],
            input_output_aliases={4: 0}, compiler_params=_cparams(),
        )(x, c, dy, cw, dx_prev)

    dx = lax.empty((T, C), BF16)
    dws, dbs, j0 = [], [], 0
    for k, dy in enumerate(dys):
        dx, dw, db = part(dy, j0, dx, k)
        dws.append(dw)
        dbs.append(db)
        j0 += dy.shape[1] // Cb
    return dx, jnp.concatenate(dws, axis=1), jnp.concatenate(dbs, axis=1)


def _softplus(x):
    e = jnp.exp(-jnp.abs(x))
    return jnp.maximum(x, 0.0) + jnp.where(e < 1e-4, e * (1.0 - 0.5 * e), jnp.log(1.0 + e))


def _cumsum_rows(x):
    n = x.shape[0]
    row = _iota(x.shape, 0)
    k = 1
    while k < n:
        x = x + jnp.where(row >= k, pltpu.roll(x, k, 0), 0.0)
        k *= 2
    return x


def _rev_cumsum_rows(x):
    n = x.shape[0]
    row = _iota(x.shape, 0)
    k = 1
    while k < n:
        x = x + jnp.where(row < n - k, pltpu.roll(x, n - k, 0), 0.0)
        k *= 2
    return x


def _col(x, lane_ids, h):
    return jnp.sum(jnp.where(lane_ids == h, x, 0.0), axis=1, keepdims=True)


def _bdot(a, b, dims):
    return lax.dot_general(a.astype(BF16), b.astype(BF16), (dims, ((), ())), preferred_element_type=F32)


NN = ((1,), (0,))
NT = ((1,), (1,))


def _ssd_dt_prep(dt_pre, dt_bias, a_log, n_heads, name):
    T = dt_pre.shape[0]
    L = SSD_CHUNK

    def body(dtp_ref, bias_ref, alog_ref, dtb_ref, cumb_ref):
        dt = _softplus(dtp_ref[...] + bias_ref[...])
        cum = _cumsum_rows(dt * (-jnp.exp(alog_ref[...])))
        lane = _iota((L, LANES), 1)
        for h in range(n_heads):
            hcols = pl.ds(h * LANES, LANES)
            dtb_ref[:, hcols] = jnp.broadcast_to(_col(dt, lane, h), (L, LANES))
            cumb_ref[:, hcols] = jnp.broadcast_to(_col(cum, lane, h), (L, LANES))

    row = pl.BlockSpec((L, LANES), lambda i: (i, 0))
    vec = pl.BlockSpec((1, LANES), lambda i: (0, 0))
    wide = pl.BlockSpec((L, n_heads * LANES), lambda i: (i, 0))
    return pl.pallas_call(
        body, name=name, grid=(T // L,), in_specs=[row, vec, vec], out_specs=[wide, wide],
        out_shape=[jax.ShapeDtypeStruct((T, n_heads * LANES), F32)] * 2, compiler_params=_cparams(),
    )(dt_pre, dt_bias, a_log)


SCAN_GROUPS_FWD = 8
SCAN_GROUPS_BWD = 4


def _scan_specs(nc, d_inner, hpg, GP):
    L = SSD_CHUNK
    G = SSD_N_GROUPS
    gw = GP * hpg * SSD_HEAD_DIM
    bw = GP * LANES
    assert G % GP == 0 and d_inner % bw == 0
    nb = d_inner // bw
    return dict(
        xs=pl.BlockSpec((L, gw), lambda b, q, c: (b * nc + c, q)),
        B=pl.BlockSpec((L, bw), lambda b, q, c: (b * nc + c, nb + q)),
        C=pl.BlockSpec((L, bw), lambda b, q, c: (b * nc + c, nb + G // GP + q)),
        heads=pl.BlockSpec((L, GP * hpg * LANES), lambda b, q, c: (b * nc + c, q)),
        gvec=pl.BlockSpec((1, gw), lambda b, q, c: (0, q)),
        grp=pl.BlockSpec((L, bw), lambda b, q, c: (b * nc + c, q)),
        st=pl.BlockSpec((1, GP * hpg // 2, SSD_D_STATE, LANES), lambda b, q, c: (b * nc + c, q, 0, 0)),
    )


def _grid_step(dims):
    t, total = 0, 1
    for ax, d in enumerate(dims):
        t = t * d + pl.program_id(ax)
        total *= d
    return t, total


def _scan_fwd(xbc, dtb, cumb, S, d_inner, name, gather=()):
    T = xbc.shape[0]
    Bn = T // S
    L = SSD_CHUNK
    G = SSD_N_GROUPS
    nc = S // L
    hpg = d_inner // SSD_HEAD_DIM // G
    pairs = hpg // 2
    GP = SCAN_GROUPS_FWD
    sp = _scan_specs(nc, d_inner, hpg, GP)
    ng = len(gather)

    def body(*refs):
        xs_ref, b_ref, c_ref, dtb_ref, cumb_ref = refs[:5]
        y_ref, st_ref = refs[5 + ng:7 + ng]
        state = refs[7 + 2 * ng]

        if ng:
            start, forward, finish = _gather_plan(refs[5:5 + ng], refs[7 + ng:7 + 2 * ng], *refs[8 + 2 * ng:])
            t, total = _grid_step((Bn, G // GP, nc))
            pl.when(t == 0)(start)
            pl.when(t == total // 2)(forward)

        @pl.when(pl.program_id(2) == 0)
        def _():
            state[...] = jnp.zeros_like(state)

        tril = _iota((L, L), 1) <= _iota((L, L), 0)
        lane = _iota((L, LANES), 1)
        lane1 = _iota((1, LANES), 1)
        last = _iota((L, 1), 0) == L - 1
        for gi, p in [(gi, p) for gi in range(GP) for p in range(pairs)]:
            if p == 0:
                Bm = b_ref[:, gi * LANES:(gi + 1) * LANES]
                Cm = c_ref[:, gi * LANES:(gi + 1) * LANES]
                Gm = _bdot(Cm, Bm, NT)
                BmT = Bm.T
            sp_ = gi * pairs + p
            cols = pl.ds(sp_ * LANES, LANES)
            S_in = state[sp_]
            st_ref[0, sp_] = S_in
            x_pair = xs_ref[:, cols]
            z_pair = jnp.zeros((L, LANES), F32)
            e_pair = jnp.zeros((L, LANES), F32)
            el_pair = jnp.zeros((1, LANES), F32)
            wms, xds = [], []
            for k in range(2):
                hcols = pl.ds((gi * hpg + 2 * p + k) * LANES, LANES)
                cum_col = cumb_ref[:, hcols]
                dt_col = dtb_ref[:, hcols]
                decay = jnp.exp(jnp.where(tril, cum_col - cum_col.T, NEG_BIG))
                mk = (lane >= SSD_HEAD_DIM) if k else (lane < SSD_HEAD_DIM)
                xd = jnp.where(mk, x_pair * dt_col, 0.0)
                wms.append((Gm * decay).astype(BF16))
                xds.append(xd.astype(BF16))
                cum_l = jnp.sum(jnp.where(last, cum_col, 0.0), axis=0, keepdims=True)
                e_pair = jnp.where(mk, jnp.exp(cum_col), e_pair)
                z_pair = z_pair + xd * jnp.exp(cum_l - cum_col)
                el_pair = jnp.where((lane1 >= SSD_HEAD_DIM) if k else (lane1 < SSD_HEAD_DIM), jnp.exp(cum_l), el_pair)
            y_pair = _bdot(jnp.concatenate(wms, axis=1), jnp.concatenate(xds, axis=0), NN)
            y_pair = y_pair + e_pair * _bdot(Cm, S_in, NN)
            state[sp_] = S_in * el_pair + _bdot(BmT, z_pair, NN)
            y_ref[:, cols] = y_pair

        if ng:
            pl.when(t == total - 1)(finish)

    outs = pl.pallas_call(
        body, name=name, grid=(Bn, G // GP, nc),
        in_specs=[sp['xs'], sp['B'], sp['C'], sp['heads'], sp['heads']] + [ANY] * ng,
        out_specs=[sp['xs'], sp['st']] + [ANY] * ng,
        out_shape=[jax.ShapeDtypeStruct((T, d_inner), F32),
                   jax.ShapeDtypeStruct((Bn * nc, G * pairs, SSD_D_STATE, LANES), F32)] + _gather_shapes(gather),
        scratch_shapes=[pltpu.VMEM((GP * pairs, SSD_D_STATE, LANES), F32)] + (_gather_sems(ng) if ng else []),
        compiler_params=_cparams(),
    )(xbc, xbc, xbc, dtb, cumb, *gather)
    return outs[0], outs[1], list(outs[2:])


def _scan_bwd(xbc, dtb, cumb, states, dy, d_lane, S, d_inner, name, exchange=()):
    T = xbc.shape[0]
    Bn = T // S
    L = SSD_CHUNK
    G = SSD_N_GROUPS
    nc = S // L
    hpg = d_inner // SSD_HEAD_DIM // G
    pairs = hpg // 2
    GP = SCAN_GROUPS_BWD
    sp = _scan_specs(nc, d_inner, hpg, GP)
    nx = len(exchange)

    def rev(spec):
        im = spec.index_map
        return pl.BlockSpec(spec.block_shape, lambda b, g, c: im(b, g, nc - 1 - c))

    def body(*refs):
        xs_ref, b_ref, c_ref, dtb_ref, cumb_ref, st_ref, dy_ref, dl_ref = refs[:8]
        dxs_ref, db_ref, dc_ref, ddt_ref, da_ref = refs[8 + nx:13 + nx]
        dstate = refs[13 + 2 * nx]
        g = pl.program_id(1)

        if nx:
            start, finish = _chips_plan(refs[8:8 + nx], refs[13 + nx:13 + 2 * nx], *refs[14 + 2 * nx:])
            t, total = _grid_step((Bn, G // GP, nc))
            pl.when(t == 0)(start)

        @pl.when(pl.program_id(2) == 0)
        def _():
            dstate[...] = jnp.zeros_like(dstate)

        tril = _iota((L, L), 1) <= _iota((L, L), 0)
        lane = _iota((L, LANES), 1)
        lane1 = _iota((1, LANES), 1)
        last = _iota((L, 1), 0) == L - 1
        for gi, p in [(gi, p) for gi in range(GP) for p in range(pairs)]:
            gcols = pl.ds(gi * LANES, LANES)
            if p == 0:
                Bm = b_ref[:, gcols]
                Cm = c_ref[:, gcols]
                Gm = _bdot(Cm, Bm, NT)
                CmT = Cm.T
                dG = jnp.zeros((L, L), F32)
                dB = jnp.zeros((L, LANES), F32)
                dC = jnp.zeros((L, LANES), F32)
                dcum = jnp.zeros((L, LANES), F32)
                ddt = jnp.zeros((L, LANES), F32)
            sp_ = gi * pairs + p
            cols = pl.ds(sp_ * LANES, LANES)
            S_in = st_ref[0, sp_]
            dS = dstate[sp_]
            x_pair = xs_ref[:, cols]
            dy_pair = dy_ref[:, cols]
            Q = _bdot(Cm, S_in, NN)
            dZ = _bdot(Bm, dS, NN)
            prod = dS * S_in
            e_pair = jnp.zeros((L, LANES), F32)
            z_pair = jnp.zeros((L, LANES), F32)
            el_pair = jnp.zeros((1, LANES), F32)
            dx_pair = dl_ref[:, cols] * dy_pair
            heads = []
            for k in range(2):
                hcols = pl.ds((gi * hpg + 2 * p + k) * LANES, LANES)
                cum_col = cumb_ref[:, hcols]
                dt_col = dtb_ref[:, hcols]
                decay = jnp.exp(jnp.where(tril, cum_col - cum_col.T, NEG_BIG))
                mk = (lane >= SSD_HEAD_DIM) if k else (lane < SSD_HEAD_DIM)
                heads.append((cum_col, dt_col, decay, mk, jnp.where(mk, x_pair * dt_col, 0.0),
                              jnp.where(mk, dy_pair, 0.0), Gm * decay))
            dy_both = jnp.concatenate([hd[5] for hd in heads], axis=0).astype(BF16)
            dWm_both = _bdot(dy_both, heads[0][4] + heads[1][4], NT)
            dxd_pair = _bdot(jnp.concatenate([hd[6].T for hd in heads], axis=1), dy_both, NN)
            for k, (cum_col, dt_col, decay, mk, xd, dy_k, Wm) in enumerate(heads):
                h = (g * GP + gi) * hpg + 2 * p + k
                mk1 = (lane1 >= SSD_HEAD_DIM) if k else (lane1 < SSD_HEAD_DIM)
                dWm = jnp.where(tril, dWm_both[k * L:(k + 1) * L], 0.0)
                dxd = jnp.where(mk, dxd_pair, 0.0)
                dseg = dWm * Wm
                dG = dG + dWm * decay
                dcum_col = (jnp.sum(dseg, axis=1, keepdims=True)
                            - jnp.sum(dseg.T, axis=1, keepdims=True))
                cum_l = jnp.sum(jnp.where(last, cum_col, 0.0), axis=0, keepdims=True)
                e_col = jnp.exp(cum_col)
                w_col = jnp.exp(cum_l - cum_col)
                el = jnp.exp(cum_l)
                dcum_col = dcum_col + jnp.sum(dy_k * Q, axis=1, keepdims=True) * e_col
                de_e = jnp.sum(dZ * xd, axis=1, keepdims=True) * w_col
                dcum_col = dcum_col - de_e
                dcum_l = (jnp.sum(de_e, axis=0, keepdims=True)
                          + el * jnp.sum(jnp.sum(jnp.where(mk, prod, 0.0), axis=1, keepdims=True),
                                         axis=0, keepdims=True))
                dcum_col = dcum_col + jnp.where(last, dcum_l, 0.0)
                dxd = dxd + jnp.where(mk, dZ * w_col, 0.0)
                dx_pair = dx_pair + dxd * dt_col
                ddt = jnp.where(lane == h, jnp.sum(dxd * x_pair, axis=1, keepdims=True), ddt)
                dcum = jnp.where(lane == h, dcum_col, dcum)
                e_pair = jnp.where(mk, e_col, e_pair)
                z_pair = z_pair + xd * w_col
                el_pair = jnp.where(mk1, el, el_pair)
            dQ = e_pair * dy_pair
            dC = dC + _bdot(dQ, S_in, NT)
            dB = dB + _bdot(z_pair, dS, NT)
            dstate[sp_] = dS * el_pair + _bdot(CmT, dQ, NN)
            dxs_ref[:, cols] = dx_pair
            if p == pairs - 1:
                dc_ref[:, gcols] = dC + _bdot(dG, Bm, NN)
                db_ref[:, gcols] = dB + _bdot(dG.T, Cm, NN)
                ddt_ref[:, gcols] = ddt
                da_ref[:, gcols] = _rev_cumsum_rows(dcum)

        if nx:
            pl.when(t == total - 1)(finish)

    grp = jax.ShapeDtypeStruct((T, G * LANES), F32)
    outs = pl.pallas_call(
        body, name=name, grid=(Bn, G // GP, nc),
        in_specs=[rev(sp['xs']), rev(sp['B']), rev(sp['C']), rev(sp['heads']), rev(sp['heads']),
                  rev(sp['st']), rev(sp['xs']), sp['gvec']] + [ANY] * nx,
        out_specs=[rev(sp['xs']), rev(sp['grp']), rev(sp['grp']), rev(sp['grp']), rev(sp['grp'])] + [ANY] * nx,
        out_shape=[jax.ShapeDtypeStruct((T, d_inner), F32), grp, grp, grp, grp] + _chips_shapes(exchange),
        scratch_shapes=[pltpu.VMEM((GP * pairs, SSD_D_STATE, LANES), F32)] + (_chips_sems(nx) if nx else []),
        compiler_params=_cparams(),
    )(xbc, xbc, xbc, dtb, cumb, states, dy, d_lane, *exchange)
    return outs[:5], list(outs[5:])


def _dt_bwd(ddt_g, da_g, dt_pre, dt_bias, a_log, name):
    T = dt_pre.shape[0]
    G = SSD_N_GROUPS
    tm = _tile(T, (512, 256, 128))

    def body(ddt_ref, da_ref, dtp_ref, bias_ref, alog_ref, dx_ref, dbias_ref, dal_ref):
        @pl.when(pl.program_id(0) == 0)
        def _():
            dbias_ref[...] = jnp.zeros_like(dbias_ref)
            dal_ref[...] = jnp.zeros_like(dal_ref)

        ddt = ddt_ref[:, 0:LANES]
        da = da_ref[:, 0:LANES]
        for gi in range(1, G):
            ddt = ddt + ddt_ref[:, gi * LANES:(gi + 1) * LANES]
            da = da + da_ref[:, gi * LANES:(gi + 1) * LANES]
        xv = dtp_ref[...] + bias_ref[...]
        A = -jnp.exp(alog_ref[...])
        dx = (ddt + da * A) * _sigmoid(xv)
        dx_ref[...] = dx.astype(BF16)
        dbias_ref[...] += jnp.sum(dx, axis=0, keepdims=True)
        dal_ref[...] += jnp.sum(da * _softplus(xv), axis=0, keepdims=True) * A

    wide = pl.BlockSpec((tm, G * LANES), lambda i: (i, 0))
    row = pl.BlockSpec((tm, LANES), lambda i: (i, 0))
    vec = pl.BlockSpec((1, LANES), lambda i: (0, 0))
    vsh = jax.ShapeDtypeStruct((1, LANES), F32)
    return pl.pallas_call(
        body, name=name, grid=(T // tm,), in_specs=[wide, wide, row, vec, vec], out_specs=[row, vec, vec],
        out_shape=[jax.ShapeDtypeStruct((T, LANES), BF16), vsh, vsh], compiler_params=_cparams(),
    )(ddt_g, da_g, dt_pre, dt_bias, a_log)


def _gate_fwd(y, xbc, z, d_lane, ng, d_inner, name):
    T = y.shape[0]
    G = SSD_N_GROUPS
    gw = d_inner // G
    tm = _tile(T, (1024, 512, 256, 128))

    def body(y_ref, x_ref, z_ref, dl_ref, ng_ref, o_ref):
        zv = z_ref[...]
        y2 = (y_ref[...] + dl_ref[...] * x_ref[...]) * (zv * _sigmoid(zv))
        r = lax.rsqrt(jnp.mean(y2 * y2, axis=1, keepdims=True) + LN_EPS)
        o_ref[...] = (y2 * r * ng_ref[...]).astype(BF16)

    blk = pl.BlockSpec((tm, gw), lambda g, i: (i, g))
    vec = pl.BlockSpec((1, gw), lambda g, i: (0, g))
    return pl.pallas_call(
        body, name=name, grid=(G, T // tm), in_specs=[blk, blk, blk, vec, vec], out_specs=blk,
        out_shape=jax.ShapeDtypeStruct((T, d_inner), BF16), compiler_params=_cparams(),
    )(y, xbc, z, d_lane, ng)


def _gate_bwd(dyn, y, xbc, z, d_lane, ng, d_inner, name, exchange=()):
    T = y.shape[0]
    G = SSD_N_GROUPS
    gw = d_inner // G
    tm = _tile(T, (1024, 512, 256, 128))
    nx = len(exchange)

    def body(*refs):
        dyn_ref, y_ref, x_ref, z_ref, dl_ref, ng_ref = refs[:6]
        dy_ref, dz_ref, dng_ref, ddl_ref = refs[6 + nx:10 + nx]
        if nx:
            start, finish = _sibling_plan(refs[6:6 + nx], refs[10 + nx:10 + 2 * nx], *refs[10 + 2 * nx:])
            t, total = _grid_step((G, T // tm))
            pl.when(t == 0)(start)

        @pl.when(pl.program_id(1) == 0)
        def _():
            dng_ref[...] = jnp.zeros_like(dng_ref)
            ddl_ref[...] = jnp.zeros_like(ddl_ref)

        zv = z_ref[...]
        xv = x_ref[...]
        sz = _sigmoid(zv)
        y1 = y_ref[...] + dl_ref[...] * xv
        y2 = y1 * (zv * sz)
        r = lax.rsqrt(jnp.mean(y2 * y2, axis=1, keepdims=True) + LN_EPS)
        dn = dyn_ref[...].astype(F32)
        dng_ref[...] += jnp.sum(dn * y2 * r, axis=0, keepdims=True)
        do = dn * ng_ref[...]
        dy2 = r * do - y2 * (r * r * r) * jnp.mean(do * y2, axis=1, keepdims=True)
        dz_ref[...] = (dy2 * y1 * sz * (1.0 + zv * (1.0 - sz))).astype(BF16)
        dy1 = dy2 * (zv * sz)
        dy_ref[...] = dy1
        ddl_ref[...] += jnp.sum(dy1 * xv, axis=0, keepdims=True)
        if nx:
            pl.when(t == total - 1)(finish)

    blk = pl.BlockSpec((tm, gw), lambda g, i: (i, g))
    vec = pl.BlockSpec((1, gw), lambda g, i: (0, g))
    vsh = jax.ShapeDtypeStruct((1, d_inner), F32)
    outs = pl.pallas_call(
        body, name=name, grid=(G, T // tm), in_specs=[blk, blk, blk, blk, vec, vec] + [ANY] * nx,
        out_specs=[blk, blk, vec, vec] + [ANY] * nx,
        out_shape=[jax.ShapeDtypeStruct((T, d_inner), F32), jax.ShapeDtypeStruct((T, d_inner), BF16), vsh, vsh]
        + _sibling_shapes(exchange),
        scratch_shapes=_sibling_sems(nx) if nx else [], compiler_params=_cparams(),
    )(dyn, y, xbc, z, d_lane, ng, *exchange)
    return outs[:4], list(outs[4:])


def _rms_fwd(x, g, name):
    T, R = x.shape
    tm = _tile(T, (512, 256, 128))

    def body(x_ref, g_ref, y_ref):
        xv = x_ref[...]
        y = xv * lax.rsqrt(jnp.mean(xv * xv, axis=1, keepdims=True) + RMS_EPS) * g_ref[...]
        y_ref[...] = y.astype(BF16)

    row = pl.BlockSpec((tm, R), lambda i: (i, 0))
    vec = pl.BlockSpec((1, R), lambda i: (0, 0))
    return pl.pallas_call(
        body, name=name, grid=(T // tm,), in_specs=[row, vec], out_specs=row,
        out_shape=jax.ShapeDtypeStruct((T, R), BF16), compiler_params=_cparams(),
    )(x, g)


def _rms_bwd(dy, x, g, name):
    T, R = x.shape
    tm = _tile(T, (512, 256, 128))

    def body(dy_ref, x_ref, g_ref, dx_ref, dg_ref):
        @pl.when(pl.program_id(0) == 0)
        def _():
            dg_ref[...] = jnp.zeros_like(dg_ref)

        xv = x_ref[...]
        dyv = dy_ref[...]
        r = lax.rsqrt(jnp.mean(xv * xv, axis=1, keepdims=True) + RMS_EPS)
        dg_ref[...] += jnp.sum(dyv * xv * r, axis=0, keepdims=True)
        do = dyv * g_ref[...]
        dx = r * do - xv * (r * r * r) * jnp.mean(do * xv, axis=1, keepdims=True)
        dx_ref[...] = dx.astype(BF16)

    row = pl.BlockSpec((tm, R), lambda i: (i, 0))
    vec = pl.BlockSpec((1, R), lambda i: (0, 0))
    return pl.pallas_call(
        body, name=name, grid=(T // tm,), in_specs=[row, row, vec], out_specs=[row, vec],
        out_shape=[jax.ShapeDtypeStruct((T, R), BF16), jax.ShapeDtypeStruct((1, R), F32)],
        compiler_params=_cparams(),
    )(dy, x, g)


def _swap_halves(x):
    half = MLA_ROPE // 2
    return jnp.where(_iota(x.shape, 1) < half, pltpu.roll(x, LANES - half, 1), pltpu.roll(x, half, 1))


def _rope(x, cc, ss, *, transpose, sum_heads=False, name):
    T, W = x.shape
    nh = W // LANES
    tm = _tile(T, (512, 256, 128))
    n_out = 1 if sum_heads else nh

    def body(x_ref, cc_ref, ss_ref, o_ref):
        ccv = cc_ref[...]
        ssv = ss_ref[...]
        if sum_heads:
            xv = x_ref[:, 0:LANES]
            for hh in range(1, nh):
                xv = xv + x_ref[:, hh * LANES:(hh + 1) * LANES]
            heads = [xv]
        else:
            heads = [x_ref[:, hh * LANES:(hh + 1) * LANES] for hh in range(nh)]
        for hh, xv in enumerate(heads):
            if transpose:
                r = xv * ccv + _swap_halves(xv * ssv)
            else:
                r = xv * ccv + _swap_halves(xv) * ssv
            r = jnp.where(_iota(r.shape, 1) < MLA_ROPE, r, 0.0)
            o_ref[:, hh * LANES:(hh + 1) * LANES] = r.astype(BF16)

    return pl.pallas_call(
        body, name=name, grid=(T // tm,),
        in_specs=[pl.BlockSpec((tm, W), lambda i: (i, 0)), pl.BlockSpec((tm, LANES), lambda i: (i, 0)),
                  pl.BlockSpec((tm, LANES), lambda i: (i, 0))],
        out_specs=pl.BlockSpec((tm, n_out * LANES), lambda i: (i, 0)),
        out_shape=jax.ShapeDtypeStruct((T, n_out * LANES), BF16), compiler_params=_cparams(),
    )(x, cc, ss)


ATT_BLOCK = 512
ATT_SUB = 256
LOG2E = 1.4426950408889634


def _att_geometry(S):
    tb = _tile(S, (ATT_BLOCK, 256))
    return tb, S // tb, tb // ATT_SUB


def _heads_per_step(nh):
    return 2 if nh % 2 == 0 else 1


def _att_scores(a, b, diag, kv_major, off):
    s = _bdot(a, b, NT)
    if diag:
        q_ax, k_ax = (1, 0) if kv_major else (0, 1)
        s = jnp.where(_iota(s.shape, k_ax) <= _iota(s.shape, q_ax) + off, s, NEG_BIG)
    return s


def _attention_fwd(qn, qr, kn, kr, v, S, scale, name):
    T, W = qn.shape
    nh = W // LANES
    Bn = T // S
    tb, n, C = _att_geometry(S)
    pairs = [(i, j) for i in range(n) for j in range(i + 1)]
    it = jnp.asarray([p[0] for p in pairs], jnp.int32)
    jt = jnp.asarray([p[1] for p in pairs], jnp.int32)
    c2 = scale * LOG2E

    HP = _heads_per_step(nh)

    def body(it_ref, jt_ref, qn_ref, qr_ref, kn_ref, kr_ref, v_ref, o_ref, ob_ref, lse_ref, m_sc, l_sc, acc):
        p = pl.program_id(2)
        i = it_ref[p]
        j = jt_ref[p]

        @pl.when(j == 0)
        def _():
            m_sc[...] = jnp.full_like(m_sc, NEG_BIG)
            l_sc[...] = jnp.zeros_like(l_sc)
            acc[...] = jnp.zeros_like(acc)

        def step(diag):
            for hh in range(HP):
                hs = pl.ds(hh * LANES, LANES)
                qc = jnp.concatenate([qn_ref[:, hs], qr_ref[:, hs]], axis=1)
                kc = jnp.concatenate([kn_ref[:, hs], kr_ref[...]], axis=1)
                st = _att_scores(kc, qc, diag, True, 0)
                m_old = m_sc[hh]
                m_new = jnp.maximum(m_old, jnp.max(st, axis=0, keepdims=True))
                pt = jnp.exp2((st - m_new) * c2)
                corr = jnp.exp2((m_old - m_new) * c2)
                l_sc[hh] = corr * l_sc[hh] + jnp.sum(pt, axis=0, keepdims=True)
                acc[hh] = corr * acc[hh] + _bdot(v_ref[:, hs].T, pt, NN)
                m_sc[hh] = m_new

        @pl.when(j < i)
        def _():
            step(False)

        @pl.when(j == i)
        def _():
            step(True)
            for hh in range(HP):
                hs = pl.ds(hh * LANES, LANES)
                ov = (acc[hh] / l_sc[hh]).T
                o_ref[:, hs] = ov
                ob_ref[:, hs] = ov.astype(BF16)
                lse_row = m_sc[hh] * scale + jnp.log(l_sc[hh])
                lse_ref[:, hs] = jnp.broadcast_to(lse_row, (LANES, tb)).T

    qspec = pl.BlockSpec((tb, HP * LANES), lambda b, h, p, it_, jt_: (b * n + it_[p], h))
    kspec = pl.BlockSpec((tb, HP * LANES), lambda b, h, p, it_, jt_: (b * n + jt_[p], h))
    krspec = pl.BlockSpec((tb, LANES), lambda b, h, p, it_, jt_: (b * n + jt_[p], 0))
    return pl.pallas_call(
        body, name=name,
        grid_spec=pltpu.PrefetchScalarGridSpec(
            num_scalar_prefetch=2, grid=(Bn, nh // HP, len(pairs)),
            in_specs=[qspec, qspec, kspec, krspec, kspec], out_specs=[qspec, qspec, qspec],
            scratch_shapes=[pltpu.VMEM((HP, 1, tb), F32), pltpu.VMEM((HP, 1, tb), F32),
                            pltpu.VMEM((HP, LANES, tb), F32)]),
        out_shape=[jax.ShapeDtypeStruct((T, W), F32), jax.ShapeDtypeStruct((T, W), BF16),
                   jax.ShapeDtypeStruct((T, W), F32)],
        compiler_params=_cparams(),
    )(it, jt, qn, qr, kn, kr, v)


def _attention_dq(qn, qr, kn, kr, v, o, do, lse, S, scale, name):
    T, W = qn.shape
    nh = W // LANES
    Bn = T // S
    tb, n, C = _att_geometry(S)
    pairs = [(i, j) for i in range(n) for j in range(i + 1)]
    it = jnp.asarray([p[0] for p in pairs], jnp.int32)
    jt = jnp.asarray([p[1] for p in pairs], jnp.int32)
    c2 = scale * LOG2E
    HP = _heads_per_step(nh)

    def body(it_ref, jt_ref, qn_ref, qr_ref, kn_ref, kr_ref, v_ref, o_ref, do_ref, lse_ref,
             dqn_ref, dqr_ref, acc, di_sc, lse_sc):
        p = pl.program_id(2)
        i = it_ref[p]
        j = jt_ref[p]

        @pl.when(j == 0)
        def _():
            acc[...] = jnp.zeros_like(acc)
            for hh in range(HP):
                hs = pl.ds(hh * LANES, LANES)
                di_sc[hh] = jnp.sum(do_ref[:, hs] * o_ref[:, hs], axis=1, keepdims=True)
                lse_sc[hh] = jnp.max(lse_ref[:, hs], axis=1, keepdims=True) * LOG2E

        def step(diag):
            for hh in range(HP):
                hs = pl.ds(hh * LANES, LANES)
                qc = jnp.concatenate([qn_ref[:, hs], qr_ref[:, hs]], axis=1)
                for c in range(C):
                    r0 = c * ATT_SUB if diag else 0
                    rows = pl.ds(r0, tb - r0)
                    ks = pl.ds(c * ATT_SUB, ATT_SUB)
                    kc = jnp.concatenate([kn_ref[ks, hs], kr_ref[ks, :]], axis=1)
                    s = _att_scores(qc[r0:], kc, diag, False, 0)
                    pr = jnp.exp2(s * c2 - lse_sc[hh, rows, :])
                    dp = _bdot(do_ref[rows, hs], v_ref[ks, hs], NT)
                    ds = pr * (dp - di_sc[hh, rows, :]) * scale
                    acc[hh, rows, :] += _bdot(ds, kc, NN)

        @pl.when(j < i)
        def _():
            step(False)

        @pl.when(j == i)
        def _():
            step(True)
            for hh in range(HP):
                hs = pl.ds(hh * LANES, LANES)
                dqn_ref[:, hs] = acc[hh, :, 0:LANES].astype(BF16)
                dqr_ref[:, hs] = acc[hh, :, LANES:2 * LANES]

    qspec = pl.BlockSpec((tb, HP * LANES), lambda b, h, p, it_, jt_: (b * n + it_[p], h))
    kspec = pl.BlockSpec((tb, HP * LANES), lambda b, h, p, it_, jt_: (b * n + jt_[p], h))
    krspec = pl.BlockSpec((tb, LANES), lambda b, h, p, it_, jt_: (b * n + jt_[p], 0))
    return pl.pallas_call(
        body, name=name,
        grid_spec=pltpu.PrefetchScalarGridSpec(
            num_scalar_prefetch=2, grid=(Bn, nh // HP, len(pairs)),
            in_specs=[qspec, qspec, kspec, krspec, kspec, qspec, qspec, qspec], out_specs=[qspec, qspec],
            scratch_shapes=[pltpu.VMEM((HP, tb, 2 * LANES), F32), pltpu.VMEM((HP, tb, 1), F32),
                            pltpu.VMEM((HP, tb, 1), F32)]),
        out_shape=[jax.ShapeDtypeStruct((T, W), BF16), jax.ShapeDtypeStruct((T, W), F32)],
        compiler_params=_cparams(),
    )(it, jt, qn, qr, kn, kr, v, o, do, lse)


def _attention_dkv(qn, qr, kn, kr, v, o, do, lse, S, scale, name):
    T, W = qn.shape
    nh = W // LANES
    Bn = T // S
    tb, n, C = _att_geometry(S)
    HP = _heads_per_step(nh)
    triples = [(j, h, i) for j in range(n) for h in range(nh // HP) for i in range(j, n)]
    jt = jnp.asarray([t[0] for t in triples], jnp.int32)
    ht = jnp.asarray([t[1] for t in triples], jnp.int32)
    it = jnp.asarray([t[2] for t in triples], jnp.int32)
    c2 = scale * LOG2E

    def as_row(col):
        return jnp.broadcast_to(col, (ATT_SUB, LANES)).T[0:1, :]

    def body(jt_ref, ht_ref, it_ref, qn_ref, qr_ref, kn_ref, kr_ref, v_ref, o_ref, do_ref, lse_ref,
             dkn_ref, dv_ref, dkr_ref, akc, av):
        p = pl.program_id(1)
        j = jt_ref[p]
        h = ht_ref[p]
        i = it_ref[p]

        @pl.when(jnp.logical_and(h == 0, i == j))
        def _():
            dkr_ref[...] = jnp.zeros_like(dkr_ref)

        @pl.when(i == j)
        def _():
            akc[...] = jnp.zeros_like(akc)
            av[...] = jnp.zeros_like(av)

        def step(diag):
            for hh in range(HP):
                hs = pl.ds(hh * LANES, LANES)
                kc = jnp.concatenate([kn_ref[:, hs], kr_ref[...]], axis=1)
                for c in range(C):
                    rows = pl.ds(c * ATT_SUB, ATT_SUB)
                    k1 = (c + 1) * ATT_SUB if diag else tb
                    ks = pl.ds(0, k1)
                    qc = jnp.concatenate([qn_ref[rows, hs], qr_ref[rows, hs]], axis=1)
                    st = _att_scores(kc[:k1], qc, diag, True, c * ATT_SUB)
                    dov = do_ref[rows, hs]
                    lse_row = as_row(jnp.max(lse_ref[rows, hs], axis=1, keepdims=True)) * LOG2E
                    di_row = as_row(jnp.sum(dov * o_ref[rows, hs], axis=1, keepdims=True))
                    pt = jnp.exp2(st * c2 - lse_row)
                    dst = pt * (_bdot(v_ref[ks, hs], dov, NT) - di_row) * scale
                    av[ks, hs] += _bdot(pt, dov, NN)
                    akc[hh, ks, :] += _bdot(dst, qc, NN)

        @pl.when(i > j)
        def _():
            step(False)

        @pl.when(i == j)
        def _():
            step(True)

        @pl.when(i == n - 1)
        def _():
            for hh in range(HP):
                dkn_ref[:, pl.ds(hh * LANES, LANES)] = akc[hh, :, 0:LANES].astype(BF16)
                dkr_ref[...] += akc[hh, :, LANES:2 * LANES]
            dv_ref[...] = av[...].astype(BF16)

    qspec = pl.BlockSpec((tb, HP * LANES), lambda b, p, jt_, ht_, it_: (b * n + it_[p], ht_[p]))
    kspec = pl.BlockSpec((tb, HP * LANES), lambda b, p, jt_, ht_, it_: (b * n + jt_[p], ht_[p]))
    krspec = pl.BlockSpec((tb, LANES), lambda b, p, jt_, ht_, it_: (b * n + jt_[p], 0))
    return pl.pallas_call(
        body, name=name,
        grid_spec=pltpu.PrefetchScalarGridSpec(
            num_scalar_prefetch=3, grid=(Bn, len(triples)),
            in_specs=[qspec, qspec, kspec, krspec, kspec, qspec, qspec, qspec],
            out_specs=[kspec, kspec, krspec],
            scratch_shapes=[pltpu.VMEM((HP, tb, 2 * LANES), F32), pltpu.VMEM((tb, HP * LANES), F32)]),
        out_shape=[jax.ShapeDtypeStruct((T, W), BF16), jax.ShapeDtypeStruct((T, W), BF16),
                   jax.ShapeDtypeStruct((T, LANES), F32)],
        compiler_params=_cparams(),
    )(jt, ht, it, qn, qr, kn, kr, v, o, do, lse)


def _pad_cols(w, n):
    return jnp.pad(w, ((0, 0), (0, n - w.shape[1])))


def _local_step(x, positions, w, target, late=None, reduce_early=None, reduce_last=None):
    w = dict(w)
    Bn, S, D = x.shape
    T = Bn * S
    depth = w['ln_mix_g'].shape[0]
    alpha = (2 * depth) ** 0.25
    d_inner = w['ssd_norm_g'].shape[1]
    n_heads_ssd = d_inner // SSD_HEAD_DIM
    G = SSD_N_GROUPS
    gn = G * SSD_D_STATE
    conv_dim = d_inner + 2 * gn
    hpg = n_heads_ssd // G
    nh = D // MLA_NOPE
    kv_rank = w['kv_norm_g'].shape[0]
    q_rank = w['q_norm_g'].shape[1]
    H = w['ffn_conv_b'].shape[1] // 2
    scale = (MLA_NOPE + MLA_ROPE) ** -0.5
    assert S % SSD_CHUNK == 0 and hpg % 2 == 0 and SSD_D_STATE == LANES and n_heads_ssd <= LANES

    X = x.reshape(T, D)
    tgt = target.reshape(T, D)

    inv_freq = 1.0 / (ROPE_THETA ** (jnp.arange(0, MLA_ROPE, 2, dtype=F32) / MLA_ROPE))
    ang = positions.reshape(T).astype(F32)[:, None] * inv_freq
    cos, sin = jnp.cos(ang), jnp.sin(ang)
    zpad = jnp.zeros((T, LANES - MLA_ROPE), F32)
    cc = jnp.concatenate([cos, cos, zpad], axis=1)
    ss = jnp.concatenate([-sin, sin, zpad], axis=1)

    w_in = w['ssd_in_proj'][0]
    w_z = w_in[:, :d_inner]
    w_xbc = w_in[:, d_inner:d_inner + conv_dim]
    w_dt = _pad_cols(w_in[:, d_inner + conv_dim:], LANES)
    ssd_cw = w['ssd_conv_w'][0]
    ssd_cb = w['ssd_conv_b']
    dt_bias = _pad_cols(w['ssd_dt_bias'], LANES)
    a_log = _pad_cols(w['ssd_A_log'], LANES)
    d_lane = jnp.repeat(w['ssd_D'], SSD_HEAD_DIM, axis=1)
    ssd_ng = w['ssd_norm_g']

    Xb = X.astype(BF16)
    z = _mm(Xb, w_z, name='ssd_in_z')
    xbc_pre = _mm(Xb, w_xbc, name='ssd_in_xbc')
    dt_pre = _mm(Xb, w_dt, name='ssd_in_dt')
    xbc, xbc_conv = _ssd_conv_fwd(xbc_pre, ssd_cw, ssd_cb, S, name='ssd_conv')
    dtb, cumb = _ssd_dt_prep(dt_pre, dt_bias, a_log, n_heads_ssd, name='ssd_dt_prep')
    y_scan, states, gathered = _scan_fwd(xbc, dtb, cumb, S, d_inner, name='ssd_scan',
                                         gather=late[0] if late else ())
    if late:
        w.update(late[1](gathered))

    w_out = w['ssd_out_proj'][0]
    w_kvc = w['kv_down_proj'][:, :kv_rank]
    w_kvr = _pad_cols(w['kv_down_proj'][:, kv_rank:], LANES)
    kv_g = w['kv_norm_g'].reshape(1, kv_rank)
    w_uk = w['kv_up_k']
    w_uv = w['kv_up_v']
    w_qd = w['q_down_proj'][0]
    q_g = w['q_norm_g']
    qu = w['q_up_proj'][0].reshape(q_rank, nh, MLA_NOPE + MLA_ROPE)
    w_qn = qu[:, :, :MLA_NOPE].reshape(q_rank, nh * LANES)
    w_qr = jnp.pad(qu[:, :, MLA_NOPE:], ((0, 0), (0, 0), (0, LANES - MLA_ROPE))).reshape(q_rank, nh * LANES)
    w_ao = w['attn_out_proj'][0]

    def ffn_parts(i):
        return dict(wu=w['ffn_up'][i], cw=w['ffn_conv_w'][i], cb=w['ffn_conv_b'][i:i + 1], wd=w['ffn_down'][i])

    def ln(name, i):
        return w[name][i:i + 1]

    def ffn_fwd(hb, i):
        f = ffn_parts(i)
        u = _mm(hb, f['wu'], name=f'ffn{i}_up')
        a = _ffn_act_fwd(u, f['cw'], f['cb'], S, name=f'ffn{i}_act')
        return _mm(a, f['wd'], name=f'ffn{i}_down'), (u, a)

    yn = _gate_fwd(y_scan, xbc, z, d_lane, ssd_ng, d_inner, name='ssd_gate')
    mix0 = _mm(yn, w_out, name='ssd_out')
    h1, s1, h1b = _ln_fwd(X, mix0, ln('ln_mix_g', 0), ln('ln_mix_b', 0), alpha, name='ln_mix0')
    ff0, ffn0_saved = ffn_fwd(h1b, 0)
    h2, s2, h2b = _ln_fwd(h1, ff0, ln('ln_ffn_g', 0), ln('ln_ffn_b', 0), alpha, name='ln_ffn0')

    ckv = _mm(h2b, w_kvc, name='kv_down_c')
    kr_pre = _mm(h2b, w_kvr, name='kv_down_r')
    ckvn = _rms_fwd(ckv, kv_g, name='kv_norm')
    kr = _rope(kr_pre, cc, ss, transpose=False, name='k_rope')
    kn = _mm(ckvn, w_uk, out_dtype=BF16, name='kv_up_k')
    v = _mm(ckvn, w_uv, out_dtype=BF16, name='kv_up_v')
    cq_pre = _mm(h2b, w_qd, name='q_down')
    cq = _rms_fwd(cq_pre, q_g, name='q_norm')
    qn = _mm(cq, w_qn, out_dtype=BF16, name='q_up_n')
    qr_pre = _mm(cq, w_qr, name='q_up_r')
    qr = _rope(qr_pre, cc, ss, transpose=False, name='q_rope')
    o, ob, lse = _attention_fwd(qn, qr, kn, kr, v, S, scale, name='attn_fwd')
    mix1 = _mm(ob, w_ao, name='attn_out')
    h3, s3, h3b = _ln_fwd(h2, mix1, ln('ln_mix_g', 1), ln('ln_mix_b', 1), alpha, name='ln_mix1')
    ff1, ffn1_saved = ffn_fwd(h3b, 1)
    h4, s4, _ = _ln_fwd(h3, ff1, ln('ln_ffn_g', 1), ln('ln_ffn_b', 1), alpha, name='ln_ffn1')
    sq, dh4 = _loss_head(h4, tgt, name='loss_head')

    g = {}

    def ffn_bwd(ds, dsb, hb_in, saved, i):
        f = ffn_parts(i)
        u, a = saved
        d_wd = _mm(a, dsb, ta=True, name=f'ffn{i}_down_dw')
        da = _mm(dsb, f['wd'], tb=True, out_dtype=BF16, name=f'ffn{i}_down_dx')
        dug, duv, dwg, dwv, dbg, dbv = _ffn_act_bwd(u, da, f['cw'], f['cb'], S, name=f'ffn{i}_act_bwd')
        d_wu = _mm(hb_in, dug, ta=True, name=f'ffn{i}_up_g_dw', into=lax.empty((D, 2 * H), F32))
        d_wu = _mm(hb_in, duv, ta=True, name=f'ffn{i}_up_v_dw', into=d_wu, into_col=H)
        dh = _mm(dug, f['wu'], tb=True, add=ds, add_scale=alpha, name=f'ffn{i}_up_g_dx')
        dh = _mm(duv, f['wu'], tb=True, add=dh, name=f'ffn{i}_up_v_dx', b_koff=H)
        return dh, d_wd, d_wu, jnp.concatenate([dwg, dwv], axis=1), jnp.concatenate([dbg, dbv], axis=1)

    ds4, ds4b, dg_f1, db_f1 = _ln_bwd(dh4, s4, ln('ln_ffn_g', 1), name='ln_ffn1_bwd')
    dh3, d_wd1, d_wu1, d_fcw1, d_fcb1 = ffn_bwd(ds4, ds4b, h3b, ffn1_saved, 1)
    ds3, ds3b, dg_m1, db_m1 = _ln_bwd(dh3, s3, ln('ln_mix_g', 1), name='ln_mix1_bwd')

    g['attn_out_proj'] = _mm(ob, ds3b, ta=True, name='attn_out_dw')[None]
    do = _mm(ds3b, w_ao, tb=True, name='attn_out_dx')
    dqn, dqr = _attention_dq(qn, qr, kn, kr, v, o, do, lse, S, scale, name='attn_bwd_dq')
    dkn, dv, dkr = _attention_dkv(qn, qr, kn, kr, v, o, do, lse, S, scale, name='attn_bwd_dkv')
    dqr_pre = _rope(dqr, cc, ss, transpose=True, name='q_rope_bwd')
    dkr_pre = _rope(dkr, cc, ss, transpose=True, name='k_rope_bwd')

    d_wqn = _mm(cq, dqn, ta=True, name='q_up_n_dw').reshape(q_rank, nh, LANES)
    d_wqr = _mm(cq, dqr_pre, ta=True, name='q_up_r_dw').reshape(q_rank, nh, LANES)[:, :, :MLA_ROPE]
    g['q_up_proj'] = jnp.concatenate([d_wqn, d_wqr], axis=2).reshape(1, q_rank, nh * (MLA_NOPE + MLA_ROPE))
    dcq = _mm(dqn, w_qn, tb=True, name='q_up_n_dx')
    dcq = _mm(dqr_pre, w_qr, tb=True, add=dcq, name='q_up_r_dx')
    dcq_pre, d_qg = _rms_bwd(dcq, cq_pre, q_g, name='q_norm_bwd')
    g['q_norm_g'] = d_qg
    g['q_down_proj'] = _mm(h2b, dcq_pre, ta=True, name='q_down_dw')[None]

    g['kv_up_k'] = _mm(ckvn, dkn, ta=True, name='kv_up_k_dw')
    g['kv_up_v'] = _mm(ckvn, dv, ta=True, name='kv_up_v_dw')
    dckvn = _mm(dkn, w_uk, tb=True, name='kv_up_k_dx')
    dckvn = _mm(dv, w_uv, tb=True, add=dckvn, name='kv_up_v_dx')
    dckv, d_kvg = _rms_bwd(dckvn, ckv, kv_g, name='kv_norm_bwd')
    g['kv_norm_g'] = d_kvg.reshape(kv_rank)
    g['kv_down_proj'] = jnp.concatenate(
        [_mm(h2b, dckv, ta=True, name='kv_down_c_dw'),
         _mm(h2b, dkr_pre, ta=True, name='kv_down_r_dw')[:, :MLA_ROPE]], axis=1)

    dh2 = _mm(dcq_pre, w_qd, tb=True, add=ds3, add_scale=alpha, name='q_down_dx')
    dh2 = _mm(dckv, w_kvc, tb=True, add=dh2, name='kv_down_c_dx')
    dh2 = _mm(dkr_pre, w_kvr, tb=True, add=dh2, name='kv_down_r_dx')

    ds2, ds2b, dg_f0, db_f0 = _ln_bwd(dh2, s2, ln('ln_ffn_g', 0), name='ln_ffn0_bwd')
    dh1, d_wd0, d_wu0, d_fcw0, d_fcb0 = ffn_bwd(ds2, ds2b, h1b, ffn0_saved, 0)
    ds1, ds1b, dg_m0, db_m0 = _ln_bwd(dh1, s1, ln('ln_mix_g', 0), name='ln_mix0_bwd')

    g['ssd_out_proj'] = _mm(yn, ds1b, ta=True, name='ssd_out_dw')[None]
    dyn = _mm(ds1b, w_out, tb=True, out_dtype=BF16, name='ssd_out_dx')
    g['ffn_up'] = jnp.stack([d_wu0, d_wu1])
    g['ffn_down'] = jnp.stack([d_wd0, d_wd1])
    early_blocks = reduce_early[0](g) if reduce_early else ()
    (dy1, dz, d_ng, d_dl), from_sibling = _gate_bwd(dyn, y_scan, xbc, z, d_lane, ssd_ng, d_inner,
                                                    name='ssd_gate_bwd', exchange=early_blocks)
    (dxs, dB, dC, ddt_g, da_g), early = _scan_bwd(
        xbc, dtb, cumb, states, dy1, d_lane, S, d_inner, name='ssd_scan_bwd',
        exchange=reduce_early[1](early_blocks, from_sibling) if reduce_early else ())
    ddt_pre, d_bias, d_alog = _dt_bwd(ddt_g, da_g, dt_pre, dt_bias, a_log, name='ssd_dt_bwd')
    dxbc_pre, d_scw, d_scb = _ssd_conv_bwd(xbc_pre, xbc_conv, [dxs, dB, dC], ssd_cw, S, name='ssd_conv_bwd')
    g['ssd_in_proj'] = jnp.concatenate(
        [_mm(Xb, dz, ta=True, name='ssd_in_z_dw'), _mm(Xb, dxbc_pre, ta=True, name='ssd_in_xbc_dw'),
         _mm(Xb, ddt_pre, ta=True, name='ssd_in_dt_dw')[:, :n_heads_ssd]], axis=1)[None]
    g['ssd_conv_w'] = d_scw[None]
    g['ssd_conv_b'] = d_scb
    g['ssd_norm_g'] = d_ng
    g['ffn_conv_w'] = jnp.stack([d_fcw0, d_fcw1])
    dx = _mm(dz, w_z, tb=True, add=ds1, add_scale=alpha, name='ssd_in_z_dx')
    dx = _mm(dxbc_pre, w_xbc, tb=True, add=dx, name='ssd_in_xbc_dx', exchange=reduce_last(g) if reduce_last else ())
    dx, last = dx if reduce_last else (dx, None)
    dx = _mm(ddt_pre, w_dt, tb=True, add=dx, name='ssd_in_dt_dx')

    g['ssd_dt_bias'] = d_bias[:, :n_heads_ssd]
    g['ssd_A_log'] = d_alog[:, :n_heads_ssd]
    g['ssd_D'] = jnp.sum(d_dl.reshape(1, n_heads_ssd, SSD_HEAD_DIM), axis=2)
    g['ffn_conv_b'] = jnp.concatenate([d_fcb0, d_fcb1], axis=0)
    g['ln_mix_g'] = jnp.concatenate([dg_m0, dg_m1], axis=0)
    g['ln_mix_b'] = jnp.concatenate([db_m0, db_m1], axis=0)
    g['ln_ffn_g'] = jnp.concatenate([dg_f0, dg_f1], axis=0)
    g['ln_ffn_b'] = jnp.concatenate([db_f0, db_f1], axis=0)
    return sq, dx.reshape(Bn, S, D), g, early, last


ANY = pl.BlockSpec(memory_space=pl.ANY)


def _all_gather(xs, name):
    n = len(xs)

    def body(*refs):
        start, forward, finish = _gather_plan(refs[:n], refs[n:2 * n], *refs[2 * n:])
        start()
        forward()
        finish()

    return pl.pallas_call(
        body, name=name, out_shape=_gather_shapes(xs), in_specs=[ANY] * n, out_specs=[ANY] * n,
        scratch_shapes=_gather_sems(n),
    )(*xs)


def _gather_shapes(xs):
    return [jax.ShapeDtypeStruct((N_DEV,) + a.shape, a.dtype) for a in xs]


def _gather_sems(n):
    return [pltpu.SemaphoreType.DMA((7 * n,)), pltpu.SemaphoreType.DMA((7 * n,)), pltpu.SemaphoreType.DMA((n,))]


def _gather_plan(x_refs, out_refs, send_sems, recv_sems, local_sems):
    n = len(x_refs)
    x, y, c = lax.axis_index("x"), lax.axis_index("y"), lax.axis_index("c")
    me, sibling = (x, y, c), (x, y, 1 - c)
    chips = [(1 - x, y), (x, 1 - y), (1 - x, 1 - y)]

    def rows(i, px, py, pc):
        return out_refs[i].at[4 * px + 2 * py + pc]

    def copy(i, k, block, to, own=False):
        return pltpu.make_async_remote_copy(
            src_ref=x_refs[i] if own else rows(i, *block), dst_ref=rows(i, *block),
            send_sem=send_sems.at[7 * i + k], recv_sem=recv_sems.at[7 * i + k],
            device_id=to, device_id_type=MESH)

    def mine():
        return [pltpu.make_async_copy(x_refs[i], rows(i, *me), local_sems.at[i]) for i in range(n)]

    def first():
        out = []
        for i in range(n):
            out.append(copy(i, 0, me, sibling, own=True))
            out += [copy(i, 1 + j, me, (*chip, c), own=True) for j, chip in enumerate(chips)]
        return out

    def passed():
        return [copy(i, 4 + j, (*chip, c), sibling) for j, chip in enumerate(chips) for i in range(n)]

    def start():
        for cp in mine() + first():
            cp.start()

    def forward():
        for j, chip in enumerate(chips):
            for i in range(n):
                copy(i, 1 + j, (*chip, c), me).wait_recv()
                copy(i, 4 + j, (*chip, c), sibling).start()

    def finish():
        for i in range(n):
            copy(i, 0, sibling, me).wait_recv()
            for j, chip in enumerate(chips):
                copy(i, 4 + j, (*chip, 1 - c), me).wait_recv()
        for cp in first() + passed():
            cp.wait_send()
        for cp in mine():
            cp.wait()

    return start, forward, finish


def _exchange_sibling(gs, name):
    n = len(gs)

    def body(*refs):
        start, finish = _sibling_plan(refs[:n], refs[n:2 * n], *refs[2 * n:])
        start()
        finish()

    return pl.pallas_call(
        body, name=name, out_shape=_sibling_shapes(gs), in_specs=[ANY] * n, out_specs=[ANY] * n,
        scratch_shapes=_sibling_sems(n),
    )(*gs)


def _sibling_shapes(gs):
    return [jax.ShapeDtypeStruct((4,) + g.shape[1:], g.dtype) for g in gs]


def _sibling_sems(n):
    return [pltpu.SemaphoreType.DMA((4 * n,)), pltpu.SemaphoreType.DMA((4 * n,))]


def _sibling_plan(g_refs, r_refs, send_sems, recv_sems):
    n = len(g_refs)
    x, y, c = lax.axis_index("x"), lax.axis_index("y"), lax.axis_index("c")

    def copies():
        return [pltpu.make_async_remote_copy(
            src_ref=g_refs[i].at[2 * k + (1 - c)], dst_ref=r_refs[i].at[k], send_sem=send_sems.at[4 * i + k],
            recv_sem=recv_sems.at[4 * i + k], device_id=(x, y, 1 - c), device_id_type=MESH)
            for i in range(n) for k in range(4)]

    def start():
        for cp in copies():
            cp.start()

    def finish():
        for cp in copies():
            cp.wait()

    return start, finish


def _chips_shapes(parts):
    return [jax.ShapeDtypeStruct((3,) + p.shape[1:], p.dtype) for p in parts]


def _chips_sems(n):
    return [pltpu.SemaphoreType.DMA((3 * n,)), pltpu.SemaphoreType.DMA((3 * n,))]


def _chips_plan(p_refs, r_refs, send_sems, recv_sems):
    n = len(p_refs)
    x, y, c = lax.axis_index("x"), lax.axis_index("y"), lax.axis_index("c")
    chips = [(1 - x, y), (x, 1 - y), (1 - x, 1 - y)]

    def copies():
        return [pltpu.make_async_remote_copy(
            src_ref=p_refs[i].at[2 * cx + cy], dst_ref=r_refs[i].at[j], send_sem=send_sems.at[3 * i + j],
            recv_sem=recv_sems.at[3 * i + j], device_id=(cx, cy, c), device_id_type=MESH)
            for i in range(n) for j, (cx, cy) in enumerate(chips)]

    def start():
        for cp in copies():
            cp.start()

    def finish():
        for cp in copies():
            cp.wait()

    return start, finish


def _row_tile(rows, cols):
    want = max(8, (256 * 1024) // cols)
    for t in range(min(rows, want) // 8 * 8, 7, -8):
        if rows % t == 0:
            return t
    return rows


def _add_sibling(gfull, r1, core, name):
    _, rows, lanes = gfull.shape
    tr = _row_tile(rows, lanes)

    def body(c_ref, g_ref, r_ref, o_ref):
        o_ref[...] = g_ref[...] + r_ref[...]

    return pl.pallas_call(
        body, name=name,
        grid_spec=pltpu.PrefetchScalarGridSpec(
            num_scalar_prefetch=1, grid=(4, rows // tr),
            in_specs=[pl.BlockSpec((1, tr, lanes), lambda k, r, c_ref: (2 * k + c_ref[0], r, 0)),
                      pl.BlockSpec((1, tr, lanes), lambda k, r, c_ref: (k, r, 0))],
            out_specs=pl.BlockSpec((1, tr, lanes), lambda k, r, c_ref: (k, r, 0))),
        out_shape=jax.ShapeDtypeStruct((4, rows, lanes), F32), compiler_params=_cparams(),
    )(core, gfull, r1)


def _adam_math(wv, gv, mv, vv):
    m = ADAM_B1 * mv + (1.0 - ADAM_B1) * gv
    v = ADAM_B2 * vv + (1.0 - ADAM_B2) * (gv * gv)
    m_hat = m / (1.0 - ADAM_B1 ** ADAM_STEP)
    v_hat = v / (1.0 - ADAM_B2 ** ADAM_STEP)
    delta = -ADAM_LR * (m_hat / (jnp.sqrt(v_hat) + ADAM_EPS) + ADAM_WD * wv)
    return delta, m, v


def _adam_sharded(part, r2, chip, wts, m, v, name):
    rows, lanes = wts.shape
    tr = _row_tile(rows, lanes)

    def body(k_ref, p_ref, r_ref, w_ref, m_ref, v_ref, g_ref, d_ref, mo_ref, vo_ref):
        gv = p_ref[0] + r_ref[0] + r_ref[1] + r_ref[2]
        delta, mn, vn = _adam_math(w_ref[...], gv, m_ref[...], v_ref[...])
        g_ref[...] = gv
        d_ref[...] = delta
        mo_ref[...] = mn
        vo_ref[...] = vn

    flat = pl.BlockSpec((tr, lanes), lambda r, k_ref: (r, 0))
    sh = jax.ShapeDtypeStruct((rows, lanes), F32)
    return pl.pallas_call(
        body, name=name,
        grid_spec=pltpu.PrefetchScalarGridSpec(
            num_scalar_prefetch=1, grid=(rows // tr,),
            in_specs=[pl.BlockSpec((1, tr, lanes), lambda r, k_ref: (k_ref[0], r, 0)),
                      pl.BlockSpec((3, tr, lanes), lambda r, k_ref: (0, r, 0)), flat, flat, flat],
            out_specs=[flat, flat, flat, flat]),
        out_shape=[sh, sh, sh, sh], compiler_params=_cparams(),
    )(chip, part, r2, wts, m, v)


def _adam_replicated(gall, wts, m, v, name):
    rows, lanes = wts.shape

    def body(ga_ref, w_ref, m_ref, v_ref, g_ref, d_ref, mo_ref, vo_ref):
        gv = ga_ref[0]
        for k in range(1, N_DEV):
            gv = gv + ga_ref[k]
        delta, mn, vn = _adam_math(w_ref[...], gv, m_ref[...], v_ref[...])
        g_ref[...] = gv
        d_ref[...] = delta
        mo_ref[...] = mn
        vo_ref[...] = vn

    sh = jax.ShapeDtypeStruct((rows, lanes), F32)
    return pl.pallas_call(body, name=name, out_shape=[sh, sh, sh, sh], compiler_params=_cparams())(gall, wts, m, v)


def _pack(arrs, dtype, row_mult):
    flat = jnp.concatenate([a.reshape(-1).astype(dtype) for a in arrs])
    n = flat.shape[0]
    unit = LANES * row_mult
    total = -(-n // unit) * unit
    return jnp.pad(flat, (0, total - n)).reshape(total // LANES, LANES)


def _unpack(flat2d, shapes):
    flat = flat2d.reshape(-1)
    out, off = [], 0
    for shp in shapes:
        n = math.prod(shp)
        out.append(flat[off:off + n].reshape(shp))
        off += n
    return out


def _unpack_gathered(gathered, shapes, axes):
    flat = gathered.reshape(N_DEV, -1)
    out, off = [], 0
    for shp, ax in zip(shapes, axes):
        n = math.prod(shp)
        seg = flat[:, off:off + n].reshape((N_DEV,) + tuple(shp))
        out.append(jnp.concatenate([seg[i] for i in range(N_DEV)], axis=ax))
        off += n
    return out


def _pack_chunks(fulls, axes, row_mult):
    per_dev = []
    for full, ax in zip(fulls, axes):
        parts = jnp.stack(jnp.split(full, N_DEV, axis=ax))
        per_dev.append(parts.reshape(N_DEV, -1))
    flat = jnp.concatenate(per_dev, axis=1)
    n = flat.shape[1]
    unit = LANES * row_mult
    total = -(-n // unit) * unit
    return jnp.pad(flat, ((0, 0), (0, total - n))).reshape(N_DEV, total // LANES, LANES)


def kernel(x, positions, ssd_in_proj, ssd_conv_w, ssd_conv_b, ssd_dt_bias, ssd_A_log, ssd_D, ssd_norm_g, ssd_out_proj, kv_down_proj, kv_norm_g, kv_up_k, kv_up_v, q_down_proj, q_norm_g, q_up_proj, attn_out_proj, ffn_up, ffn_conv_w, ffn_conv_b, ffn_down, ln_mix_g, ln_mix_b, ln_ffn_g, ln_ffn_b, loss_target, m_ssd_in_proj, m_ssd_conv_w, m_ssd_conv_b, m_ssd_dt_bias, m_ssd_A_log, m_ssd_D, m_ssd_norm_g, m_ssd_out_proj, m_kv_down_proj, m_kv_norm_g, m_kv_up_k, m_kv_up_v, m_q_down_proj, m_q_norm_g, m_q_up_proj, m_attn_out_proj, m_ffn_up, m_ffn_conv_w, m_ffn_conv_b, m_ffn_down, m_ln_mix_g, m_ln_mix_b, m_ln_ffn_g, m_ln_ffn_b, v_ssd_in_proj, v_ssd_conv_w, v_ssd_conv_b, v_ssd_dt_bias, v_ssd_A_log, v_ssd_D, v_ssd_norm_g, v_ssd_out_proj, v_kv_down_proj, v_kv_norm_g, v_kv_up_k, v_kv_up_v, v_q_down_proj, v_q_norm_g, v_q_up_proj, v_attn_out_proj, v_ffn_up, v_ffn_conv_w, v_ffn_conv_b, v_ffn_down, v_ln_mix_g, v_ln_mix_b, v_ln_ffn_g, v_ln_ffn_b):
    given = dict(locals())
    wl = {n: given[n] for n in WEIGHTS}
    ml = {n: given['m_' + n] for n in WEIGHTS}
    vl = {n: given['v_' + n] for n in WEIGHTS}
    sharded_axis = dict(SHARDED)
    big = [n for n, _ in SHARDED if n not in SHARDED_F32]
    small = [n for n, _ in SHARDED if n in SHARDED_F32]

    def as2d(a):
        return a.reshape(-1, a.shape[-1])

    def to_full(gathered, n):
        shp, ax = wl[n].shape, sharded_axis[n]
        moved = jnp.moveaxis(gathered.reshape((N_DEV,) + shp), 0, ax)
        return moved.reshape(shp[:ax] + (N_DEV * shp[ax],) + shp[ax + 1:])

    def to_chunks(full_grad, n):
        shp, ax = wl[n].shape, sharded_axis[n]
        split = full_grad.reshape(shp[:ax] + (N_DEV, shp[ax]) + shp[ax + 1:])
        return jnp.moveaxis(split, ax, 0).reshape((N_DEV,) + as2d(wl[n]).shape)

    first = [n for n in big if n in GATHERED_FIRST]
    late = [n for n in big if n not in GATHERED_FIRST]
    full = {n: wl[n] for n in REPLICATED}
    gathered = _all_gather([as2d(wl[n]).astype(BF16) for n in first] + [_pack([wl[n] for n in small], F32, 8)],
                           name='gather_weights')
    for n, gat in zip(first, gathered[:-1]):
        full[n] = to_full(gat, n)
    full.update(zip(small, _unpack_gathered(gathered[-1], [wl[n].shape for n in small],
                                            [sharded_axis[n] for n in small])))

    cx, cy, cc = lax.axis_index("x"), lax.axis_index("y"), lax.axis_index("c")
    core = jnp.reshape(cc, (1,)).astype(jnp.int32)
    chip = jnp.reshape(2 * cx + cy, (1,)).astype(jnp.int32)
    early_names = [n for n in big if n not in REDUCED_LAST]
    last_names = [n for n in big if n in REDUCED_LAST]
    parts = {}

    def add_sibling(names, chunked, received):
        parts.update((n, _add_sibling(gf, r, core, name=f'grads_add_sibling_{n}'))
                     for n, gf, r in zip(names, chunked, received))
        return [parts[n] for n in names]

    def early_blocks(grads):
        return [to_chunks(grads[n], n) for n in early_names]

    def early_parts(chunked, received):
        return add_sibling(early_names, chunked, received)

    def reduce_last(grads):
        chunked = [to_chunks(grads[n], n) for n in last_names]
        chunked.append(_pack_chunks([grads[n] for n in small], [sharded_axis[n] for n in small], 8))
        return add_sibling(last_names + ['small'], chunked, _exchange_sibling(chunked, name='grads_to_sibling_last'))

    sq, grad_x, grads, r2_early, r2_last = _local_step(
        x, positions, full, loss_target,
        late=([as2d(wl[n]).astype(BF16) for n in late], lambda gats: {n: to_full(g, n) for n, g in zip(late, gats)}),
        reduce_early=(early_blocks, early_parts), reduce_last=reduce_last)
    loss = lax.psum(0.5 * jnp.sum(sq) / x.shape[-1], ("x", "y", "c"))

    r2 = dict(zip(early_names, r2_early))
    r2.update(zip(last_names + ['small'], r2_last))
    res = {}
    kinds = ('grad', 'delta', 'new_m', 'new_v')
    for n in big:
        outs = _adam_sharded(parts[n], r2[n], chip, as2d(wl[n]), as2d(ml[n]), as2d(vl[n]), name=f'adamw_{n}')
        for kind, a in zip(kinds, outs):
            res[kind, n] = a.reshape(wl[n].shape)
    outs = _adam_sharded(parts['small'], r2['small'], chip, _pack([wl[n] for n in small], F32, 8),
                         _pack([ml[n] for n in small], F32, 8), _pack([vl[n] for n in small], F32, 8),
                         name='adamw_small_sharded')
    for kind, packed in zip(kinds, outs):
        for n, a in zip(small, _unpack(packed, [wl[n].shape for n in small])):
            res[kind, n] = a

    rep = list(REPLICATED)
    rshapes = [wl[n].shape for n in rep]
    gall, = _all_gather([_pack([grads[n] for n in rep], F32, 8)], name='gather_replicated_grads')
    outs = _adam_replicated(gall, _pack([wl[n] for n in rep], F32, 8), _pack([ml[n] for n in rep], F32, 8),
                            _pack([vl[n] for n in rep], F32, 8), name='adamw_replicated')
    for kind, packed in zip(('grad', 'delta', 'new_m', 'new_v'), outs):
        for n, a in zip(rep, _unpack(packed, rshapes)):
            res[kind, n] = a

    return (loss, grad_x, *[res['grad', n] for n in WEIGHTS], *[res['delta', n] for n in WEIGHTS],
            *[res['new_m', n] for n in WEIGHTS], *[res['new_v', n] for n in WEIGHTS])
```

```python
import math

import jax
import jax.numpy as jnp
from jax import lax
from jax.experimental import pallas as pl
from jax.experimental.pallas import tpu as pltpu

F32 = jnp.float32
BF16 = jnp.bfloat16
MESH = pl.DeviceIdType.MESH

N_DEV = 8
LANES = 128
MM_TILES = (1024, 1408, 896, 512, 384, 256, 128)
MM_VMEM_BUDGET = 38 * 1024 * 1024
MM_VMEM_LIMIT = 56 * 1024 * 1024
VMEM_LIMIT = 32 * 1024 * 1024
LN_EPS = 1e-5
RMS_EPS = 1e-6
SSD_HEAD_DIM = 64
SSD_N_GROUPS = 8
SSD_D_STATE = 128
SSD_CHUNK = 128
MLA_NOPE = 128
MLA_ROPE = 64
ROPE_THETA = 10000.0
ADAM_LR = 0.001
ADAM_B1 = 0.9
ADAM_B2 = 0.999
ADAM_EPS = 1e-08
ADAM_WD = 0.01
ADAM_STEP = 10
NEG_BIG = -1e30

SHARDED = (('ssd_in_proj', 2), ('ssd_conv_w', 2), ('ssd_conv_b', 1), ('ssd_norm_g', 1), ('ssd_out_proj', 1),
           ('kv_down_proj', 0), ('kv_up_k', 1), ('kv_up_v', 1), ('q_down_proj', 1), ('q_up_proj', 2),
           ('attn_out_proj', 1), ('ffn_up', 2), ('ffn_conv_w', 2), ('ffn_down', 1))
SHARDED_F32 = ('ssd_conv_w', 'ssd_conv_b', 'ssd_norm_g', 'ffn_conv_w')
GATHERED_FIRST = ('ssd_in_proj',)
REDUCED_LAST = ('ssd_in_proj',)
REPLICATED = ('ssd_dt_bias', 'ssd_A_log', 'ssd_D', 'kv_norm_g', 'q_norm_g', 'ffn_conv_b',
              'ln_mix_g', 'ln_mix_b', 'ln_ffn_g', 'ln_ffn_b')
WEIGHTS = ('ssd_in_proj', 'ssd_conv_w', 'ssd_conv_b', 'ssd_dt_bias', 'ssd_A_log', 'ssd_D', 'ssd_norm_g',
           'ssd_out_proj', 'kv_down_proj', 'kv_norm_g', 'kv_up_k', 'kv_up_v', 'q_down_proj', 'q_norm_g',
           'q_up_proj', 'attn_out_proj', 'ffn_up', 'ffn_conv_w', 'ffn_conv_b', 'ffn_down', 'ln_mix_g',
           'ln_mix_b', 'ln_ffn_g', 'ln_ffn_b')


def _cparams(limit=None):
    return pltpu.CompilerParams(vmem_limit_bytes=VMEM_LIMIT if limit is None else limit)


def _tile(n, cands):
    for c in cands:
        if n % c == 0:
            return c
    return n


def _sigmoid(x):
    return 1.0 / (1.0 + jnp.exp(-x))


def _iota(shape, axis):
    return lax.broadcasted_iota(jnp.int32, shape, axis)


def _mm(a, b, *, ta=False, tb=False, add=None, add_scale=1.0, out_dtype=F32, name, b_koff=0, into=None, into_col=0,
        exchange=()):
    if ta:
        K, M = a.shape
    else:
        M, K = a.shape
    if tb:
        N, Kb = b.shape
    else:
        Kb, N = b.shape
    assert b_koff + K <= Kb, (a.shape, b.shape, ta, tb, b_koff)
    tm = _tile(M, MM_TILES)
    tn = _tile(N, MM_TILES)
    tk = K if K <= MM_TILES[0] and b_koff == 0 and Kb == K else _tile(K, MM_TILES)
    nk = K // tk
    assert b_koff % tk == 0 and (into is None or (into.shape[0] == M and into.dtype == out_dtype))
    kb = b_koff // tk

    def vmem_estimate(tm_, tn_):
        ins = 2 * (tm_ * tk * a.dtype.itemsize + tk * tn_ * b.dtype.itemsize)
        outs = tm_ * tn_ * (2 * jnp.dtype(out_dtype).itemsize + 4 + (4 if nk > 1 else 0))
        return ins + outs + (2 * tm_ * tn_ * add.dtype.itemsize if add is not None else 0)

    while vmem_estimate(tm, tn) > MM_VMEM_BUDGET:
        can_m, can_n = tm % (2 * LANES) == 0, tn % (2 * LANES) == 0
        if can_m and (tm >= tn or not can_n):
            tm //= 2
        elif can_n:
            tn //= 2
        else:
            break

    assert into_col % tn == 0
    jb = into_col // tn

    nx = len(exchange)
    n_in = (add is not None) + (into is not None)
    grid = (M // tm, N // tn, nk)

    def body(a_ref, b_ref, *rest):
        add_ref = rest[0] if add is not None else None
        o_ref = rest[n_in + nx]
        acc = rest[n_in + 2 * nx + 1] if nk > 1 else None
        k = pl.program_id(2)
        if nx:
            start, finish_exchange = _chips_plan(rest[n_in:n_in + nx], rest[n_in + nx + 1:n_in + 2 * nx + 1],
                                                 *rest[len(rest) - 2:])
            t, total = _grid_step(grid)
            pl.when(t == 0)(start)

        if ta:
            av = a_ref[...].astype(BF16).T
        else:
            av = a_ref[...].astype(BF16)
        bv = b_ref[...].astype(BF16)
        dn = (((1,), (1 if tb else 0,)), ((), ()))
        part = lax.dot_general(av, bv, dn, preferred_element_type=F32)

        def finish(r):
            if add is not None:
                r = r + add_scale * add_ref[...].astype(F32)
            o_ref[...] = r.astype(out_dtype)

        if nk == 1:
            finish(part)
        else:
            @pl.when(k == 0)
            def _():
                acc[...] = part

            @pl.when(k > 0)
            def _():
                acc[...] += part

            @pl.when(k == nk - 1)
            def _():
                finish(acc[...])

        if nx:
            pl.when(t == total - 1)(finish_exchange)

    a_spec = (pl.BlockSpec((tk, tm), lambda i, j, k: (k, i)) if ta
              else pl.BlockSpec((tm, tk), lambda i, j, k: (i, k)))
    b_spec = (pl.BlockSpec((tn, tk), lambda i, j, k: (j, k + kb)) if tb
              else pl.BlockSpec((tk, tn), lambda i, j, k: (k + kb, j)))
    in_specs = [a_spec, b_spec]
    args = [a, b]
    if add is not None:
        in_specs.append(pl.BlockSpec((tm, tn), lambda i, j, k: (i, j)))
        args.append(add)
    aliases = {}
    if into is not None:
        aliases = {len(args): 0}
        in_specs.append(pl.BlockSpec(memory_space=pl.ANY))
        args.append(into)
    any_spec = pl.BlockSpec(memory_space=pl.ANY)
    outs = pl.pallas_call(
        body, name=name, grid=grid,
        in_specs=in_specs + [any_spec] * nx,
        out_specs=[pl.BlockSpec((tm, tn), lambda i, j, k: (i, j + jb))] + [any_spec] * nx,
        out_shape=[jax.ShapeDtypeStruct((M, N) if into is None else into.shape, out_dtype)] + _chips_shapes(exchange),
        scratch_shapes=([pltpu.VMEM((tm, tn), F32)] if nk > 1 else []) + (_chips_sems(nx) if nx else []),
        input_output_aliases=aliases, compiler_params=_cparams(MM_VMEM_LIMIT),
    )(*args, *exchange)
    return (outs[0], list(outs[1:])) if nx else outs[0]


def _ln_fwd(h, mix, g, b, alpha, name):
    T, D = h.shape
    tm = _tile(T, (512, 256, 128))

    def body(h_ref, m_ref, g_ref, b_ref, y_ref, s_ref, yb_ref):
        s = alpha * h_ref[...] + m_ref[...]
        mu = jnp.mean(s, axis=1, keepdims=True)
        xc = s - mu
        var = jnp.mean(xc * xc, axis=1, keepdims=True)
        y = xc * lax.rsqrt(var + LN_EPS) * g_ref[...] + b_ref[...]
        y_ref[...] = y
        s_ref[...] = s
        yb_ref[...] = y.astype(BF16)

    row = pl.BlockSpec((tm, D), lambda i: (i, 0))
    vec = pl.BlockSpec((1, D), lambda i: (0, 0))
    return pl.pallas_call(
        body, name=name, grid=(T // tm,), in_specs=[row, row, vec, vec], out_specs=[row, row, row],
        out_shape=[jax.ShapeDtypeStruct((T, D), F32)] * 2 + [jax.ShapeDtypeStruct((T, D), BF16)],
        compiler_params=_cparams(),
    )(h, mix, g, b)


def _ln_bwd(dy, s, g, name):
    T, D = s.shape
    tm = _tile(T, (512, 256, 128))

    def body(dy_ref, s_ref, g_ref, ds_ref, dsb_ref, dg_ref, db_ref):
        @pl.when(pl.program_id(0) == 0)
        def _():
            dg_ref[...] = jnp.zeros_like(dg_ref)
            db_ref[...] = jnp.zeros_like(db_ref)

        sv = s_ref[...]
        dyv = dy_ref[...]
        mu = jnp.mean(sv, axis=1, keepdims=True)
        xc = sv - mu
        rstd = lax.rsqrt(jnp.mean(xc * xc, axis=1, keepdims=True) + LN_EPS)
        xhat = xc * rstd
        dxh = dyv * g_ref[...]
        m1 = jnp.mean(dxh, axis=1, keepdims=True)
        m2 = jnp.mean(dxh * xhat, axis=1, keepdims=True)
        ds = rstd * (dxh - m1 - xhat * m2)
        ds_ref[...] = ds
        dsb_ref[...] = ds.astype(BF16)
        dg_ref[...] += jnp.sum(dyv * xhat, axis=0, keepdims=True)
        db_ref[...] += jnp.sum(dyv, axis=0, keepdims=True)

    row = pl.BlockSpec((tm, D), lambda i: (i, 0))
    vec = pl.BlockSpec((1, D), lambda i: (0, 0))
    return pl.pallas_call(
        body, name=name, grid=(T // tm,), in_specs=[row, row, vec], out_specs=[row, row, vec, vec],
        out_shape=[jax.ShapeDtypeStruct((T, D), F32), jax.ShapeDtypeStruct((T, D), BF16),
                   jax.ShapeDtypeStruct((1, D), F32), jax.ShapeDtypeStruct((1, D), F32)],
        compiler_params=_cparams(),
    )(dy, s, g)


def _loss_head(y, target, name):
    T, D = y.shape
    tm = _tile(T, (512, 256, 128))

    def body(y_ref, t_ref, sq_ref, dy_ref):
        @pl.when(pl.program_id(0) == 0)
        def _():
            sq_ref[...] = jnp.zeros_like(sq_ref)

        e = y_ref[...] - t_ref[...]
        sq_ref[...] += jnp.sum(e * e, axis=0, keepdims=True)
        dy_ref[...] = e * (1.0 / D)

    row = pl.BlockSpec((tm, D), lambda i: (i, 0))
    vec = pl.BlockSpec((1, D), lambda i: (0, 0))
    return pl.pallas_call(
        body, name=name, grid=(T // tm,), in_specs=[row, row], out_specs=[vec, row],
        out_shape=[jax.ShapeDtypeStruct((1, D), F32), jax.ShapeDtypeStruct((T, D), F32)],
        compiler_params=_cparams(),
    )(y, target)


def _shift_down(x, j):
    if j == 0:
        return x
    return jnp.where(_iota(x.shape, 0) >= j, pltpu.roll(x, j, 0), 0.0)


def _shift_up(x, j):
    if j == 0:
        return x
    n = x.shape[0]
    return jnp.where(_iota(x.shape, 0) < n - j, pltpu.roll(x, n - j, 0), 0.0)


def _conv_val(x, w_ref, b_ref):
    width = w_ref.shape[0]
    y = b_ref[...] + w_ref[width - 1:width, :] * x
    for j in range(1, width):
        y = y + w_ref[width - 1 - j:width - j, :] * _shift_down(x, j)
    return y


def _conv_bwd_val(x, dc, w_ref, dw_ref, db_ref):
    width = w_ref.shape[0]
    dx = w_ref[width - 1:width, :] * dc
    dw_ref[width - 1:width, :] += jnp.sum(dc * x, axis=0, keepdims=True)
    for j in range(1, width):
        sj = _shift_up(dc, j)
        dx = dx + w_ref[width - 1 - j:width - j, :] * sj
        dw_ref[width - 1 - j:width - j, :] += jnp.sum(sj * x, axis=0, keepdims=True)
    db_ref[...] += jnp.sum(dc, axis=0, keepdims=True)
    return dx


def _ffn_specs(H, S, width, cb):
    cb = _tile(H, (cb, LANES))
    nC = H // cb
    return dict(
        g=pl.BlockSpec((S, cb), lambda j, b: (b, j)), v=pl.BlockSpec((S, cb), lambda j, b: (b, j + nC)),
        wg=pl.BlockSpec((width, cb), lambda j, b: (0, j)), wv=pl.BlockSpec((width, cb), lambda j, b: (0, j + nC)),
        bg=pl.BlockSpec((1, cb), lambda j, b: (0, j)), bv=pl.BlockSpec((1, cb), lambda j, b: (0, j + nC))), nC


def _ffn_act_fwd(u, cw, cb, S, name):
    T, H2 = u.shape
    H = H2 // 2
    sp, nC = _ffn_specs(H, S, cw.shape[0], 2 * LANES)

    def body(ug_ref, uv_ref, wg_ref, wv_ref, bg_ref, bv_ref, a_ref):
        g = _conv_val(ug_ref[...], wg_ref, bg_ref)
        v = _conv_val(uv_ref[...], wv_ref, bv_ref)
        a_ref[...] = (g * _sigmoid(g) * v).astype(BF16)

    return pl.pallas_call(
        body, name=name, grid=(nC, T // S),
        in_specs=[sp['g'], sp['v'], sp['wg'], sp['wv'], sp['bg'], sp['bv']], out_specs=sp['g'],
        out_shape=jax.ShapeDtypeStruct((T, H), BF16), compiler_params=_cparams(),
    )(u, u, cw, cw, cb, cb)


def _ffn_act_bwd(u, da, cw, cb, S, name):
    T, H2 = u.shape
    H = H2 // 2
    width = cw.shape[0]
    sp, nC = _ffn_specs(H, S, width, LANES)

    def body(ug_ref, uv_ref, da_ref, wg_ref, wv_ref, bg_ref, bv_ref,
             dug_ref, duv_ref, dwg_ref, dwv_ref, dbg_ref, dbv_ref):
        @pl.when(pl.program_id(1) == 0)
        def _():
            for r in (dwg_ref, dwv_ref, dbg_ref, dbv_ref):
                r[...] = jnp.zeros_like(r)

        def conv_bwd(x, dc, w_ref, dw_ref, db_ref):
            dx = w_ref[width - 1:width, :] * dc
            for j in range(1, width):
                dx = dx + w_ref[width - 1 - j:width - j, :] * _shift_up(dc, j)
            for j in range(width):
                dw_ref[width - 1 - j:width - j, :] += jnp.sum(dc * _shift_down(x, j), axis=0, keepdims=True)
            db_ref[...] += jnp.sum(dc, axis=0, keepdims=True)
            return dx

        xg = ug_ref[...]
        xv = uv_ref[...]
        g = _conv_val(xg, wg_ref, bg_ref)
        v = _conv_val(xv, wv_ref, bv_ref)
        dav = da_ref[...].astype(F32)
        sg = _sigmoid(g)
        dg = dav * v * sg * (1.0 + g * (1.0 - sg))
        dv = dav * g * sg
        dug_ref[...] = conv_bwd(xg, dg, wg_ref, dwg_ref, dbg_ref).astype(BF16)
        duv_ref[...] = conv_bwd(xv, dv, wv_ref, dwv_ref, dbv_ref).astype(BF16)

    act = jax.ShapeDtypeStruct((T, H), BF16)
    wsh = jax.ShapeDtypeStruct((width, H), F32)
    bsh = jax.ShapeDtypeStruct((1, H), F32)
    return pl.pallas_call(
        body, name=name, grid=(nC, T // S),
        in_specs=[sp['g'], sp['v'], sp['g'], sp['wg'], sp['wv'], sp['bg'], sp['bv']],
        out_specs=[sp['g'], sp['g'], sp['wg'], sp['wg'], sp['bg'], sp['bg']],
        out_shape=[act, act, wsh, wsh, bsh, bsh], compiler_params=_cparams(),
    )(u, u, da, cw, cw, cb, cb)


def _ssd_conv_fwd(x, cw, cb, S, name):
    T, C = x.shape
    Cb = _tile(C, (256, 128))
    width = cw.shape[0]

    def body(x_ref, w_ref, b_ref, y_ref, c_ref):
        c = _conv_val(x_ref[...], w_ref, b_ref)
        c_ref[...] = c
        y_ref[...] = c * _sigmoid(c)

    blk = pl.BlockSpec((S, Cb), lambda j, b: (b, j))
    wblk = pl.BlockSpec((width, Cb), lambda j, b: (0, j))
    bblk = pl.BlockSpec((1, Cb), lambda j, b: (0, j))
    return pl.pallas_call(
        body, name=name, grid=(C // Cb, T // S), in_specs=[blk, wblk, bblk], out_specs=[blk, blk],
        out_shape=[jax.ShapeDtypeStruct((T, C), F32)] * 2, compiler_params=_cparams(),
    )(x, cw, cb)


def _ssd_conv_bwd(x, c, dys, cw, S, name):
    T, C = x.shape
    Cb = _tile(C, (256, 128))
    width = cw.shape[0]
    assert all(d.shape[1] % Cb == 0 for d in dys) and sum(d.shape[1] for d in dys) == C

    def part(dy, j0, dx_prev, k):
        n = dy.shape[1] // Cb

        def body(x_ref, c_ref, dy_ref, w_ref, prev_ref, dx_ref, dw_ref, db_ref):
            @pl.when(pl.program_id(1) == 0)
            def _():
                dw_ref[...] = jnp.zeros_like(dw_ref)
                db_ref[...] = jnp.zeros_like(db_ref)

            cval = c_ref[...]
            sc = _sigmoid(cval)
            dc = dy_ref[...] * sc * (1.0 + cval * (1.0 - sc))
            dx_ref[...] = _conv_bwd_val(x_ref[...], dc, w_ref, dw_ref, db_ref).astype(BF16)

        wide = pl.BlockSpec((S, Cb), lambda j, b: (b, j + j0))
        return pl.pallas_call(
            body, name=f'{name}_{k}', grid=(n, T // S),
            in_specs=[wide, wide, pl.BlockSpec((S, Cb), lambda j, b: (b, j)),
                      pl.BlockSpec((width, Cb), lambda j, b: (0, j + j0)), pl.BlockSpec(memory_space=pl.ANY)],
            out_specs=[wide, pl.BlockSpec((width, Cb), lambda j, b: (0, j)), pl.BlockSpec((1, Cb), lambda j, b: (0, j))],
            out_shape=[jax.ShapeDtypeStruct((T, C), BF16), jax.ShapeDtypeStruct((width, n * Cb), F32),
                       jax.ShapeDtypeStruct((1, n * Cb), F32)],
            input_output_aliases={4: 0}, compiler_params=_cparams(),
        )(x, c, dy, cw, dx_prev)

    dx = lax.empty((T, C), BF16)
    dws, dbs, j0 = [], [], 0
    for k, dy in enumerate(dys):
        dx, dw, db = part(dy, j0, dx, k)
        dws.append(dw)
        dbs.append(db)
        j0 += dy.shape[1] // Cb
    return dx, jnp.concatenate(dws, axis=1), jnp.concatenate(dbs, axis=1)


def _softplus(x):
    e = jnp.exp(-jnp.abs(x))
    return jnp.maximum(x, 0.0) + jnp.where(e < 1e-4, e * (1.0 - 0.5 * e), jnp.log(1.0 + e))


def _cumsum_rows(x):
    n = x.shape[0]
    row = _iota(x.shape, 0)
    k = 1
    while k < n:
        x = x + jnp.where(row >= k, pltpu.roll(x, k, 0), 0.0)
        k *= 2
    return x


def _rev_cumsum_rows(x):
    n = x.shape[0]
    row = _iota(x.shape, 0)
    k = 1
    while k < n:
        x = x + jnp.where(row < n - k, pltpu.roll(x, n - k, 0), 0.0)
        k *= 2
    return x


def _col(x, lane_ids, h):
    return jnp.sum(jnp.where(lane_ids == h, x, 0.0), axis=1, keepdims=True)


def _bdot(a, b, dims):
    return lax.dot_general(a.astype(BF16), b.astype(BF16), (dims, ((), ())), preferred_element_type=F32)


NN = ((1,), (0,))
NT = ((1,), (1,))


def _ssd_dt_prep(dt_pre, dt_bias, a_log, n_heads, name):
    T = dt_pre.shape[0]
    L = SSD_CHUNK

    def body(dtp_ref, bias_ref, alog_ref, dtb_ref, cumb_ref):
        dt = _softplus(dtp_ref[...] + bias_ref[...])
        cum = _cumsum_rows(dt * (-jnp.exp(alog_ref[...])))
        lane = _iota((L, LANES), 1)
        for h in range(n_heads):
            hcols = pl.ds(h * LANES, LANES)
            dtb_ref[:, hcols] = jnp.broadcast_to(_col(dt, lane, h), (L, LANES))
            cumb_ref[:, hcols] = jnp.broadcast_to(_col(cum, lane, h), (L, LANES))

    row = pl.BlockSpec((L, LANES), lambda i: (i, 0))
    vec = pl.BlockSpec((1, LANES), lambda i: (0, 0))
    wide = pl.BlockSpec((L, n_heads * LANES), lambda i: (i, 0))
    return pl.pallas_call(
        body, name=name, grid=(T // L,), in_specs=[row, vec, vec], out_specs=[wide, wide],
        out_shape=[jax.ShapeDtypeStruct((T, n_heads * LANES), F32)] * 2, compiler_params=_cparams(),
    )(dt_pre, dt_bias, a_log)


SCAN_GROUPS_FWD = 8
SCAN_GROUPS_BWD = 4


def _scan_specs(nc, d_inner, hpg, GP):
    L = SSD_CHUNK
    G = SSD_N_GROUPS
    gw = GP * hpg * SSD_HEAD_DIM
    bw = GP * LANES
    assert G % GP == 0 and d_inner % bw == 0
    nb = d_inner // bw
    return dict(
        xs=pl.BlockSpec((L, gw), lambda b, q, c: (b * nc + c, q)),
        B=pl.BlockSpec((L, bw), lambda b, q, c: (b * nc + c, nb + q)),
        C=pl.BlockSpec((L, bw), lambda b, q, c: (b * nc + c, nb + G // GP + q)),
        heads=pl.BlockSpec((L, GP * hpg * LANES), lambda b, q, c: (b * nc + c, q)),
        gvec=pl.BlockSpec((1, gw), lambda b, q, c: (0, q)),
        grp=pl.BlockSpec((L, bw), lambda b, q, c: (b * nc + c, q)),
        st=pl.BlockSpec((1, GP * hpg // 2, SSD_D_STATE, LANES), lambda b, q, c: (b * nc + c, q, 0, 0)),
    )


def _grid_step(dims):
    t, total = 0, 1
    for ax, d in enumerate(dims):
        t = t * d + pl.program_id(ax)
        total *= d
    return t, total


def _scan_fwd(xbc, dtb, cumb, S, d_inner, name, gather=()):
    T = xbc.shape[0]
    Bn = T // S
    L = SSD_CHUNK
    G = SSD_N_GROUPS
    nc = S // L
    hpg = d_inner // SSD_HEAD_DIM // G
    pairs = hpg // 2
    GP = SCAN_GROUPS_FWD
    sp = _scan_specs(nc, d_inner, hpg, GP)
    ng = len(gather)

    def body(*refs):
        xs_ref, b_ref, c_ref, dtb_ref, cumb_ref = refs[:5]
        y_ref, st_ref = refs[5 + ng:7 + ng]
        state = refs[7 + 2 * ng]

        if ng:
            start, forward, finish = _gather_plan(refs[5:5 + ng], refs[7 + ng:7 + 2 * ng], *refs[8 + 2 * ng:])
            t, total = _grid_step((Bn, G // GP, nc))
            pl.when(t == 0)(start)
            pl.when(t == total // 2)(forward)

        @pl.when(pl.program_id(2) == 0)
        def _():
            state[...] = jnp.zeros_like(state)

        tril = _iota((L, L), 1) <= _iota((L, L), 0)
        lane = _iota((L, LANES), 1)
        lane1 = _iota((1, LANES), 1)
        last = _iota((L, 1), 0) == L - 1
        for gi, p in [(gi, p) for gi in range(GP) for p in range(pairs)]:
            if p == 0:
                Bm = b_ref[:, gi * LANES:(gi + 1) * LANES]
                Cm = c_ref[:, gi * LANES:(gi + 1) * LANES]
                Gm = _bdot(Cm, Bm, NT)
                S_cat = jnp.concatenate([state[gi * pairs + q] for q in range(pairs)], axis=1)
                Q_cat = _bdot(Cm, S_cat, NN)
                z_pairs, el_pairs = [], []
            sp_ = gi * pairs + p
            cols = pl.ds(sp_ * LANES, LANES)
            st_ref[0, sp_] = state[sp_]
            x_pair = xs_ref[:, cols]
            z_pair = jnp.zeros((L, LANES), F32)
            e_pair = jnp.zeros((L, LANES), F32)
            el_pair = jnp.zeros((1, LANES), F32)
            wms, xds = [], []
            for k in range(2):
                hcols = pl.ds((gi * hpg + 2 * p + k) * LANES, LANES)
                cum_col = cumb_ref[:, hcols]
                dt_col = dtb_ref[:, hcols]
                decay = jnp.exp(jnp.where(tril, cum_col - cum_col.T, NEG_BIG))
                mk = (lane >= SSD_HEAD_DIM) if k else (lane < SSD_HEAD_DIM)
                xd = jnp.where(mk, x_pair * dt_col, 0.0)
                wms.append((Gm * decay).astype(BF16))
                xds.append(xd.astype(BF16))
                cum_l = jnp.sum(jnp.where(last, cum_col, 0.0), axis=0, keepdims=True)
                e_pair = jnp.where(mk, jnp.exp(cum_col), e_pair)
                z_pair = z_pair + xd * jnp.exp(cum_l - cum_col)
                el_pair = jnp.where((lane1 >= SSD_HEAD_DIM) if k else (lane1 < SSD_HEAD_DIM), jnp.exp(cum_l), el_pair)
            y_pair = _bdot(jnp.concatenate(wms, axis=1), jnp.concatenate(xds, axis=0), NN)
            y_ref[:, cols] = y_pair + e_pair * Q_cat[:, p * LANES:(p + 1) * LANES]
            z_pairs.append(z_pair)
            el_pairs.append(el_pair)
            if p == pairs - 1:
                S_new = (S_cat * jnp.concatenate(el_pairs, axis=1)
                         + _bdot(Bm.T, jnp.concatenate(z_pairs, axis=1), NN))
                for q in range(pairs):
                    state[gi * pairs + q] = S_new[:, q * LANES:(q + 1) * LANES]

        if ng:
            pl.when(t == total - 1)(finish)

    outs = pl.pallas_call(
        body, name=name, grid=(Bn, G // GP, nc),
        in_specs=[sp['xs'], sp['B'], sp['C'], sp['heads'], sp['heads']] + [ANY] * ng,
        out_specs=[sp['xs'], sp['st']] + [ANY] * ng,
        out_shape=[jax.ShapeDtypeStruct((T, d_inner), F32),
                   jax.ShapeDtypeStruct((Bn * nc, G * pairs, SSD_D_STATE, LANES), F32)] + _gather_shapes(gather),
        scratch_shapes=[pltpu.VMEM((GP * pairs, SSD_D_STATE, LANES), F32)] + (_gather_sems(ng) if ng else []),
        compiler_params=_cparams(),
    )(xbc, xbc, xbc, dtb, cumb, *gather)
    return outs[0], outs[1], list(outs[2:])


def _scan_bwd(xbc, dtb, cumb, states, dy, d_lane, S, d_inner, name, exchange=()):
    T = xbc.shape[0]
    Bn = T // S
    L = SSD_CHUNK
    G = SSD_N_GROUPS
    nc = S // L
    hpg = d_inner // SSD_HEAD_DIM // G
    pairs = hpg // 2
    GP = SCAN_GROUPS_BWD
    sp = _scan_specs(nc, d_inner, hpg, GP)
    nx = len(exchange)

    def rev(spec):
        im = spec.index_map
        return pl.BlockSpec(spec.block_shape, lambda b, g, c: im(b, g, nc - 1 - c))

    def body(*refs):
        xs_ref, b_ref, c_ref, dtb_ref, cumb_ref, st_ref, dy_ref, dl_ref = refs[:8]
        dxs_ref, db_ref, dc_ref, ddt_ref, da_ref = refs[8 + nx:13 + nx]
        dstate = refs[13 + 2 * nx]
        g = pl.program_id(1)

        if nx:
            start, finish = _chips_plan(refs[8:8 + nx], refs[13 + nx:13 + 2 * nx], *refs[14 + 2 * nx:])
            t, total = _grid_step((Bn, G // GP, nc))
            pl.when(t == 0)(start)

        @pl.when(pl.program_id(2) == 0)
        def _():
            dstate[...] = jnp.zeros_like(dstate)

        tril = _iota((L, L), 1) <= _iota((L, L), 0)
        lane = _iota((L, LANES), 1)
        lane1 = _iota((1, LANES), 1)
        last = _iota((L, 1), 0) == L - 1
        for gi, p in [(gi, p) for gi in range(GP) for p in range(pairs)]:
            gcols = pl.ds(gi * LANES, LANES)
            if p == 0:
                Bm = b_ref[:, gcols]
                Cm = c_ref[:, gcols]
                Gm = _bdot(Cm, Bm, NT)
                dG = jnp.zeros((L, L), F32)
                dcum = jnp.zeros((L, LANES), F32)
                ddt = jnp.zeros((L, LANES), F32)
                S_cat = jnp.concatenate([st_ref[0, gi * pairs + q] for q in range(pairs)], axis=1)
                dS_cat = jnp.concatenate([dstate[gi * pairs + q] for q in range(pairs)], axis=1)
                Q_cat = _bdot(Cm, S_cat, NN)
                dZ_cat = _bdot(Bm, dS_cat, NN)
                dQs, zs, els = [], [], []
            sp_ = gi * pairs + p
            cols = pl.ds(sp_ * LANES, LANES)
            pcols = slice(p * LANES, (p + 1) * LANES)
            x_pair = xs_ref[:, cols]
            dy_pair = dy_ref[:, cols]
            Q = Q_cat[:, pcols]
            dZ = dZ_cat[:, pcols]
            prod = dS_cat[:, pcols] * S_cat[:, pcols]
            e_pair = jnp.zeros((L, LANES), F32)
            z_pair = jnp.zeros((L, LANES), F32)
            el_pair = jnp.zeros((1, LANES), F32)
            dx_pair = dl_ref[:, cols] * dy_pair
            heads = []
            for k in range(2):
                hcols = pl.ds((gi * hpg + 2 * p + k) * LANES, LANES)
                cum_col = cumb_ref[:, hcols]
                dt_col = dtb_ref[:, hcols]
                decay = jnp.exp(jnp.where(tril, cum_col - cum_col.T, NEG_BIG))
                mk = (lane >= SSD_HEAD_DIM) if k else (lane < SSD_HEAD_DIM)
                heads.append((cum_col, dt_col, decay, mk, jnp.where(mk, x_pair * dt_col, 0.0),
                              jnp.where(mk, dy_pair, 0.0), Gm * decay))
            dy_both = jnp.concatenate([hd[5] for hd in heads], axis=0).astype(BF16)
            dWm_both = _bdot(dy_both, heads[0][4] + heads[1][4], NT)
            dxd_pair = _bdot(jnp.concatenate([hd[6].T for hd in heads], axis=1), dy_both, NN)
            for k, (cum_col, dt_col, decay, mk, xd, dy_k, Wm) in enumerate(heads):
                h = (g * GP + gi) * hpg + 2 * p + k
                mk1 = (lane1 >= SSD_HEAD_DIM) if k else (lane1 < SSD_HEAD_DIM)
                dWm = jnp.where(tril, dWm_both[k * L:(k + 1) * L], 0.0)
                dxd = jnp.where(mk, dxd_pair, 0.0)
                dseg = dWm * Wm
                dG = dG + dWm * decay
                dcum_col = (jnp.sum(dseg, axis=1, keepdims=True)
                            - jnp.sum(dseg.T, axis=1, keepdims=True))
                cum_l = jnp.sum(jnp.where(last, cum_col, 0.0), axis=0, keepdims=True)
                e_col = jnp.exp(cum_col)
                w_col = jnp.exp(cum_l - cum_col)
                el = jnp.exp(cum_l)
                dcum_col = dcum_col + jnp.sum(dy_k * Q, axis=1, keepdims=True) * e_col
                de_e = jnp.sum(dZ * xd, axis=1, keepdims=True) * w_col
                dcum_col = dcum_col - de_e
                dcum_l = (jnp.sum(de_e, axis=0, keepdims=True)
                          + el * jnp.sum(jnp.sum(jnp.where(mk, prod, 0.0), axis=1, keepdims=True),
                                         axis=0, keepdims=True))
                dcum_col = dcum_col + jnp.where(last, dcum_l, 0.0)
                dxd = dxd + jnp.where(mk, dZ * w_col, 0.0)
                dx_pair = dx_pair + dxd * dt_col
                ddt = jnp.where(lane == h, jnp.sum(dxd * x_pair, axis=1, keepdims=True), ddt)
                dcum = jnp.where(lane == h, dcum_col, dcum)
                e_pair = jnp.where(mk, e_col, e_pair)
                z_pair = z_pair + xd * w_col
                el_pair = jnp.where(mk1, el, el_pair)
            dQs.append(e_pair * dy_pair)
            zs.append(z_pair)
            els.append(el_pair)
            dxs_ref[:, cols] = dx_pair
            if p == pairs - 1:
                dQ_cat = jnp.concatenate(dQs, axis=1)
                dS_new = dS_cat * jnp.concatenate(els, axis=1) + _bdot(Cm.T, dQ_cat, NN)
                for q in range(pairs):
                    dstate[gi * pairs + q] = dS_new[:, q * LANES:(q + 1) * LANES]
                dc_ref[:, gcols] = _bdot(dQ_cat, S_cat, NT) + _bdot(dG, Bm, NN)
                db_ref[:, gcols] = _bdot(jnp.concatenate(zs, axis=1), dS_cat, NT) + _bdot(dG.T, Cm, NN)
                ddt_ref[:, gcols] = ddt
                da_ref[:, gcols] = _rev_cumsum_rows(dcum)

        if nx:
            pl.when(t == total - 1)(finish)

    grp = jax.ShapeDtypeStruct((T, G * LANES), F32)
    outs = pl.pallas_call(
        body, name=name, grid=(Bn, G // GP, nc),
        in_specs=[rev(sp['xs']), rev(sp['B']), rev(sp['C']), rev(sp['heads']), rev(sp['heads']),
                  rev(sp['st']), rev(sp['xs']), sp['gvec']] + [ANY] * nx,
        out_specs=[rev(sp['xs']), rev(sp['grp']), rev(sp['grp']), rev(sp['grp']), rev(sp['grp'])] + [ANY] * nx,
        out_shape=[jax.ShapeDtypeStruct((T, d_inner), F32), grp, grp, grp, grp] + _chips_shapes(exchange),
        scratch_shapes=[pltpu.VMEM((GP * pairs, SSD_D_STATE, LANES), F32)] + (_chips_sems(nx) if nx else []),
        compiler_params=_cparams(),
    )(xbc, xbc, xbc, dtb, cumb, states, dy, d_lane, *exchange)
    return outs[:5], list(outs[5:])


def _dt_bwd(ddt_g, da_g, dt_pre, dt_bias, a_log, name):
    T = dt_pre.shape[0]
    G = SSD_N_GROUPS
    tm = _tile(T, (512, 256, 128))

    def body(ddt_ref, da_ref, dtp_ref, bias_ref, alog_ref, dx_ref, dbias_ref, dal_ref):
        @pl.when(pl.program_id(0) == 0)
        def _():
            dbias_ref[...] = jnp.zeros_like(dbias_ref)
            dal_ref[...] = jnp.zeros_like(dal_ref)

        ddt = ddt_ref[:, 0:LANES]
        da = da_ref[:, 0:LANES]
        for gi in range(1, G):
            ddt = ddt + ddt_ref[:, gi * LANES:(gi + 1) * LANES]
            da = da + da_ref[:, gi * LANES:(gi + 1) * LANES]
        xv = dtp_ref[...] + bias_ref[...]
        A = -jnp.exp(alog_ref[...])
        dx = (ddt + da * A) * _sigmoid(xv)
        dx_ref[...] = dx.astype(BF16)
        dbias_ref[...] += jnp.sum(dx, axis=0, keepdims=True)
        dal_ref[...] += jnp.sum(da * _softplus(xv), axis=0, keepdims=True) * A

    wide = pl.BlockSpec((tm, G * LANES), lambda i: (i, 0))
    row = pl.BlockSpec((tm, LANES), lambda i: (i, 0))
    vec = pl.BlockSpec((1, LANES), lambda i: (0, 0))
    vsh = jax.ShapeDtypeStruct((1, LANES), F32)
    return pl.pallas_call(
        body, name=name, grid=(T // tm,), in_specs=[wide, wide, row, vec, vec], out_specs=[row, vec, vec],
        out_shape=[jax.ShapeDtypeStruct((T, LANES), BF16), vsh, vsh], compiler_params=_cparams(),
    )(ddt_g, da_g, dt_pre, dt_bias, a_log)


def _gate_fwd(y, xbc, z, d_lane, ng, d_inner, name):
    T = y.shape[0]
    G = SSD_N_GROUPS
    gw = d_inner // G
    tm = _tile(T, (1024, 512, 256, 128))

    def body(y_ref, x_ref, z_ref, dl_ref, ng_ref, o_ref):
        zv = z_ref[...]
        y2 = (y_ref[...] + dl_ref[...] * x_ref[...]) * (zv * _sigmoid(zv))
        r = lax.rsqrt(jnp.mean(y2 * y2, axis=1, keepdims=True) + LN_EPS)
        o_ref[...] = (y2 * r * ng_ref[...]).astype(BF16)

    blk = pl.BlockSpec((tm, gw), lambda g, i: (i, g))
    vec = pl.BlockSpec((1, gw), lambda g, i: (0, g))
    return pl.pallas_call(
        body, name=name, grid=(G, T // tm), in_specs=[blk, blk, blk, vec, vec], out_specs=blk,
        out_shape=jax.ShapeDtypeStruct((T, d_inner), BF16), compiler_params=_cparams(),
    )(y, xbc, z, d_lane, ng)


def _gate_bwd(dyn, y, xbc, z, d_lane, ng, d_inner, name, exchange=()):
    T = y.shape[0]
    G = SSD_N_GROUPS
    gw = d_inner // G
    tm = _tile(T, (1024, 512, 256, 128))
    nx = len(exchange)

    def body(*refs):
        dyn_ref, y_ref, x_ref, z_ref, dl_ref, ng_ref = refs[:6]
        dy_ref, dz_ref, dng_ref, ddl_ref = refs[6 + nx:10 + nx]
        if nx:
            start, finish = _sibling_plan(refs[6:6 + nx], refs[10 + nx:10 + 2 * nx], *refs[10 + 2 * nx:])
            t, total = _grid_step((G, T // tm))
            pl.when(t == 0)(start)

        @pl.when(pl.program_id(1) == 0)
        def _():
            dng_ref[...] = jnp.zeros_like(dng_ref)
            ddl_ref[...] = jnp.zeros_like(ddl_ref)

        zv = z_ref[...]
        xv = x_ref[...]
        sz = _sigmoid(zv)
        y1 = y_ref[...] + dl_ref[...] * xv
        y2 = y1 * (zv * sz)
        r = lax.rsqrt(jnp.mean(y2 * y2, axis=1, keepdims=True) + LN_EPS)
        dn = dyn_ref[...].astype(F32)
        dng_ref[...] += jnp.sum(dn * y2 * r, axis=0, keepdims=True)
        do = dn * ng_ref[...]
        dy2 = r * do - y2 * (r * r * r) * jnp.mean(do * y2, axis=1, keepdims=True)
        dz_ref[...] = (dy2 * y1 * sz * (1.0 + zv * (1.0 - sz))).astype(BF16)
        dy1 = dy2 * (zv * sz)
        dy_ref[...] = dy1
        ddl_ref[...] += jnp.sum(dy1 * xv, axis=0, keepdims=True)
        if nx:
            pl.when(t == total - 1)(finish)

    blk = pl.BlockSpec((tm, gw), lambda g, i: (i, g))
    vec = pl.BlockSpec((1, gw), lambda g, i: (0, g))
    vsh = jax.ShapeDtypeStruct((1, d_inner), F32)
    outs = pl.pallas_call(
        body, name=name, grid=(G, T // tm), in_specs=[blk, blk, blk, blk, vec, vec] + [ANY] * nx,
        out_specs=[blk, blk, vec, vec] + [ANY] * nx,
        out_shape=[jax.ShapeDtypeStruct((T, d_inner), F32), jax.ShapeDtypeStruct((T, d_inner), BF16), vsh, vsh]
        + _sibling_shapes(exchange),
        scratch_shapes=_sibling_sems(nx) if nx else [], compiler_params=_cparams(),
    )(dyn, y, xbc, z, d_lane, ng, *exchange)
    return outs[:4], list(outs[4:])


def _rms_fwd(x, g, name):
    T, R = x.shape
    tm = _tile(T, (512, 256, 128))

    def body(x_ref, g_ref, y_ref):
        xv = x_ref[...]
        y = xv * lax.rsqrt(jnp.mean(xv * xv, axis=1, keepdims=True) + RMS_EPS) * g_ref[...]
        y_ref[...] = y.astype(BF16)

    row = pl.BlockSpec((tm, R), lambda i: (i, 0))
    vec = pl.BlockSpec((1, R), lambda i: (0, 0))
    return pl.pallas_call(
        body, name=name, grid=(T // tm,), in_specs=[row, vec], out_specs=row,
        out_shape=jax.ShapeDtypeStruct((T, R), BF16), compiler_params=_cparams(),
    )(x, g)


def _rms_bwd(dy, x, g, name):
    T, R = x.shape
    tm = _tile(T, (512, 256, 128))

    def body(dy_ref, x_ref, g_ref, dx_ref, dg_ref):
        @pl.when(pl.program_id(0) == 0)
        def _():
            dg_ref[...] = jnp.zeros_like(dg_ref)

        xv = x_ref[...]
        dyv = dy_ref[...]
        r = lax.rsqrt(jnp.mean(xv * xv, axis=1, keepdims=True) + RMS_EPS)
        dg_ref[...] += jnp.sum(dyv * xv * r, axis=0, keepdims=True)
        do = dyv * g_ref[...]
        dx = r * do - xv * (r * r * r) * jnp.mean(do * xv, axis=1, keepdims=True)
        dx_ref[...] = dx.astype(BF16)

    row = pl.BlockSpec((tm, R), lambda i: (i, 0))
    vec = pl.BlockSpec((1, R), lambda i: (0, 0))
    return pl.pallas_call(
        body, name=name, grid=(T // tm,), in_specs=[row, row, vec], out_specs=[row, vec],
        out_shape=[jax.ShapeDtypeStruct((T, R), BF16), jax.ShapeDtypeStruct((1, R), F32)],
        compiler_params=_cparams(),
    )(dy, x, g)


def _swap_halves(x):
    half = MLA_ROPE // 2
    return jnp.where(_iota(x.shape, 1) < half, pltpu.roll(x, LANES - half, 1), pltpu.roll(x, half, 1))


def _rope(x, cc, ss, *, transpose, sum_heads=False, name):
    T, W = x.shape
    nh = W // LANES
    tm = _tile(T, (512, 256, 128))
    n_out = 1 if sum_heads else nh

    def body(x_ref, cc_ref, ss_ref, o_ref):
        ccv = cc_ref[...]
        ssv = ss_ref[...]
        if sum_heads:
            xv = x_ref[:, 0:LANES]
            for hh in range(1, nh):
                xv = xv + x_ref[:, hh * LANES:(hh + 1) * LANES]
            heads = [xv]
        else:
            heads = [x_ref[:, hh * LANES:(hh + 1) * LANES] for hh in range(nh)]
        for hh, xv in enumerate(heads):
            if transpose:
                r = xv * ccv + _swap_halves(xv * ssv)
            else:
                r = xv * ccv + _swap_halves(xv) * ssv
            r = jnp.where(_iota(r.shape, 1) < MLA_ROPE, r, 0.0)
            o_ref[:, hh * LANES:(hh + 1) * LANES] = r.astype(BF16)

    return pl.pallas_call(
        body, name=name, grid=(T // tm,),
        in_specs=[pl.BlockSpec((tm, W), lambda i: (i, 0)), pl.BlockSpec((tm, LANES), lambda i: (i, 0)),
                  pl.BlockSpec((tm, LANES), lambda i: (i, 0))],
        out_specs=pl.BlockSpec((tm, n_out * LANES), lambda i: (i, 0)),
        out_shape=jax.ShapeDtypeStruct((T, n_out * LANES), BF16), compiler_params=_cparams(),
    )(x, cc, ss)


ATT_BLOCK = 512
ATT_SUB = 256
LOG2E = 1.4426950408889634


def _att_geometry(S):
    tb = _tile(S, (ATT_BLOCK, 256))
    return tb, S // tb, tb // ATT_SUB


def _heads_per_step(nh):
    return 2 if nh % 2 == 0 else 1


def _att_scores(a, b, diag, kv_major, off):
    s = _bdot(a, b, NT)
    if diag:
        q_ax, k_ax = (1, 0) if kv_major else (0, 1)
        s = jnp.where(_iota(s.shape, k_ax) <= _iota(s.shape, q_ax) + off, s, NEG_BIG)
    return s


def _attention_fwd(qn, qr, kn, kr, v, S, scale, name):
    T, W = qn.shape
    nh = W // LANES
    Bn = T // S
    tb, n, C = _att_geometry(S)
    pairs = [(i, j) for i in range(n) for j in range(i + 1)]
    it = jnp.asarray([p[0] for p in pairs], jnp.int32)
    jt = jnp.asarray([p[1] for p in pairs], jnp.int32)
    c2 = scale * LOG2E

    HP = _heads_per_step(nh)

    def body(it_ref, jt_ref, qn_ref, qr_ref, kn_ref, kr_ref, v_ref, o_ref, ob_ref, lse_ref, m_sc, l_sc, acc):
        p = pl.program_id(2)
        i = it_ref[p]
        j = jt_ref[p]

        @pl.when(j == 0)
        def _():
            m_sc[...] = jnp.full_like(m_sc, NEG_BIG)
            l_sc[...] = jnp.zeros_like(l_sc)
            acc[...] = jnp.zeros_like(acc)

        def step(diag):
            for hh in range(HP):
                hs = pl.ds(hh * LANES, LANES)
                qc = jnp.concatenate([qn_ref[:, hs], qr_ref[:, hs]], axis=1)
                kc = jnp.concatenate([kn_ref[:, hs], kr_ref[...]], axis=1)
                st = _att_scores(kc, qc, diag, True, 0)
                m_old = m_sc[hh]
                m_new = jnp.maximum(m_old, jnp.max(st, axis=0, keepdims=True))
                pt = jnp.exp2((st - m_new) * c2)
                corr = jnp.exp2((m_old - m_new) * c2)
                l_sc[hh] = corr * l_sc[hh] + jnp.sum(pt, axis=0, keepdims=True)
                acc[hh] = corr * acc[hh] + _bdot(v_ref[:, hs].T, pt, NN)
                m_sc[hh] = m_new

        @pl.when(j < i)
        def _():
            step(False)

        @pl.when(j == i)
        def _():
            step(True)
            for hh in range(HP):
                hs = pl.ds(hh * LANES, LANES)
                ov = (acc[hh] / l_sc[hh]).T
                o_ref[:, hs] = ov
                ob_ref[:, hs] = ov.astype(BF16)
                lse_row = m_sc[hh] * scale + jnp.log(l_sc[hh])
                lse_ref[:, hs] = jnp.broadcast_to(lse_row, (LANES, tb)).T

    qspec = pl.BlockSpec((tb, HP * LANES), lambda b, h, p, it_, jt_: (b * n + it_[p], h))
    kspec = pl.BlockSpec((tb, HP * LANES), lambda b, h, p, it_, jt_: (b * n + jt_[p], h))
    krspec = pl.BlockSpec((tb, LANES), lambda b, h, p, it_, jt_: (b * n + jt_[p], 0))
    return pl.pallas_call(
        body, name=name,
        grid_spec=pltpu.PrefetchScalarGridSpec(
            num_scalar_prefetch=2, grid=(Bn, nh // HP, len(pairs)),
            in_specs=[qspec, qspec, kspec, krspec, kspec], out_specs=[qspec, qspec, qspec],
            scratch_shapes=[pltpu.VMEM((HP, 1, tb), F32), pltpu.VMEM((HP, 1, tb), F32),
                            pltpu.VMEM((HP, LANES, tb), F32)]),
        out_shape=[jax.ShapeDtypeStruct((T, W), F32), jax.ShapeDtypeStruct((T, W), BF16),
                   jax.ShapeDtypeStruct((T, W), F32)],
        compiler_params=_cparams(),
    )(it, jt, qn, qr, kn, kr, v)


def _attention_dq(qn, qr, kn, kr, v, o, do, lse, S, scale, name):
    T, W = qn.shape
    nh = W // LANES
    Bn = T // S
    tb, n, C = _att_geometry(S)
    pairs = [(i, j) for i in range(n) for j in range(i + 1)]
    it = jnp.asarray([p[0] for p in pairs], jnp.int32)
    jt = jnp.asarray([p[1] for p in pairs], jnp.int32)
    c2 = scale * LOG2E
    HP = _heads_per_step(nh)

    def body(it_ref, jt_ref, qn_ref, qr_ref, kn_ref, kr_ref, v_ref, o_ref, do_ref, lse_ref,
             dqn_ref, dqr_ref, acc, di_sc, lse_sc):
        p = pl.program_id(2)
        i = it_ref[p]
        j = jt_ref[p]

        @pl.when(j == 0)
        def _():
            acc[...] = jnp.zeros_like(acc)
            for hh in range(HP):
                hs = pl.ds(hh * LANES, LANES)
                di_sc[hh] = jnp.sum(do_ref[:, hs] * o_ref[:, hs], axis=1, keepdims=True)
                lse_sc[hh] = jnp.max(lse_ref[:, hs], axis=1, keepdims=True) * LOG2E

        def step(diag):
            for hh in range(HP):
                hs = pl.ds(hh * LANES, LANES)
                qc = jnp.concatenate([qn_ref[:, hs], qr_ref[:, hs]], axis=1)
                for c in range(C):
                    r0 = c * ATT_SUB if diag else 0
                    rows = pl.ds(r0, tb - r0)
                    ks = pl.ds(c * ATT_SUB, ATT_SUB)
                    kc = jnp.concatenate([kn_ref[ks, hs], kr_ref[ks, :]], axis=1)
                    s = _att_scores(qc[r0:], kc, diag, False, 0)
                    pr = jnp.exp2(s * c2 - lse_sc[hh, rows, :])
                    dp = _bdot(do_ref[rows, hs], v_ref[ks, hs], NT)
                    ds = pr * (dp - di_sc[hh, rows, :]) * scale
                    acc[hh, rows, :] += _bdot(ds, kc, NN)

        @pl.when(j < i)
        def _():
            step(False)

        @pl.when(j == i)
        def _():
            step(True)
            for hh in range(HP):
                hs = pl.ds(hh * LANES, LANES)
                dqn_ref[:, hs] = acc[hh, :, 0:LANES].astype(BF16)
                dqr_ref[:, hs] = acc[hh, :, LANES:2 * LANES]

    qspec = pl.BlockSpec((tb, HP * LANES), lambda b, h, p, it_, jt_: (b * n + it_[p], h))
    kspec = pl.BlockSpec((tb, HP * LANES), lambda b, h, p, it_, jt_: (b * n + jt_[p], h))
    krspec = pl.BlockSpec((tb, LANES), lambda b, h, p, it_, jt_: (b * n + jt_[p], 0))
    return pl.pallas_call(
        body, name=name,
        grid_spec=pltpu.PrefetchScalarGridSpec(
            num_scalar_prefetch=2, grid=(Bn, nh // HP, len(pairs)),
            in_specs=[qspec, qspec, kspec, krspec, kspec, qspec, qspec, qspec], out_specs=[qspec, qspec],
            scratch_shapes=[pltpu.VMEM((HP, tb, 2 * LANES), F32), pltpu.VMEM((HP, tb, 1), F32),
                            pltpu.VMEM((HP, tb, 1), F32)]),
        out_shape=[jax.ShapeDtypeStruct((T, W), BF16), jax.ShapeDtypeStruct((T, W), F32)],
        compiler_params=_cparams(),
    )(it, jt, qn, qr, kn, kr, v, o, do, lse)


def _attention_dkv(qn, qr, kn, kr, v, o, do, lse, S, scale, name):
    T, W = qn.shape
    nh = W // LANES
    Bn = T // S
    tb, n, C = _att_geometry(S)
    HP = _heads_per_step(nh)
    triples = [(j, h, i) for j in range(n) for h in range(nh // HP) for i in range(j, n)]
    jt = jnp.asarray([t[0] for t in triples], jnp.int32)
    ht = jnp.asarray([t[1] for t in triples], jnp.int32)
    it = jnp.asarray([t[2] for t in triples], jnp.int32)
    c2 = scale * LOG2E

    def as_row(col):
        return jnp.broadcast_to(col, (ATT_SUB, LANES)).T[0:1, :]

    def body(jt_ref, ht_ref, it_ref, qn_ref, qr_ref, kn_ref, kr_ref, v_ref, o_ref, do_ref, lse_ref,
             dkn_ref, dv_ref, dkr_ref, akc, av):
        p = pl.program_id(1)
        j = jt_ref[p]
        h = ht_ref[p]
        i = it_ref[p]

        @pl.when(jnp.logical_and(h == 0, i == j))
        def _():
            dkr_ref[...] = jnp.zeros_like(dkr_ref)

        @pl.when(i == j)
        def _():
            akc[...] = jnp.zeros_like(akc)
            av[...] = jnp.zeros_like(av)

        def step(diag):
            for hh in range(HP):
                hs = pl.ds(hh * LANES, LANES)
                kc = jnp.concatenate([kn_ref[:, hs], kr_ref[...]], axis=1)
                for c in range(C):
                    rows = pl.ds(c * ATT_SUB, ATT_SUB)
                    k1 = (c + 1) * ATT_SUB if diag else tb
                    ks = pl.ds(0, k1)
                    qc = jnp.concatenate([qn_ref[rows, hs], qr_ref[rows, hs]], axis=1)
                    st = _att_scores(kc[:k1], qc, diag, True, c * ATT_SUB)
                    dov = do_ref[rows, hs]
                    lse_row = as_row(jnp.max(lse_ref[rows, hs], axis=1, keepdims=True)) * LOG2E
                    di_row = as_row(jnp.sum(dov * o_ref[rows, hs], axis=1, keepdims=True))
                    pt = jnp.exp2(st * c2 - lse_row)
                    dst = pt * (_bdot(v_ref[ks, hs], dov, NT) - di_row) * scale
                    av[ks, hs] += _bdot(pt, dov, NN)
                    akc[hh, ks, :] += _bdot(dst, qc, NN)

        @pl.when(i > j)
        def _():
            step(False)

        @pl.when(i == j)
        def _():
            step(True)

        @pl.when(i == n - 1)
        def _():
            for hh in range(HP):
                dkn_ref[:, pl.ds(hh * LANES, LANES)] = akc[hh, :, 0:LANES].astype(BF16)
                dkr_ref[...] += akc[hh, :, LANES:2 * LANES]
            dv_ref[...] = av[...].astype(BF16)

    qspec = pl.BlockSpec((tb, HP * LANES), lambda b, p, jt_, ht_, it_: (b * n + it_[p], ht_[p]))
    kspec = pl.BlockSpec((tb, HP * LANES), lambda b, p, jt_, ht_, it_: (b * n + jt_[p], ht_[p]))
    krspec = pl.BlockSpec((tb, LANES), lambda b, p, jt_, ht_, it_: (b * n + jt_[p], 0))
    return pl.pallas_call(
        body, name=name,
        grid_spec=pltpu.PrefetchScalarGridSpec(
            num_scalar_prefetch=3, grid=(Bn, len(triples)),
            in_specs=[qspec, qspec, kspec, krspec, kspec, qspec, qspec, qspec],
            out_specs=[kspec, kspec, krspec],
            scratch_shapes=[pltpu.VMEM((HP, tb, 2 * LANES), F32), pltpu.VMEM((tb, HP * LANES), F32)]),
        out_shape=[jax.ShapeDtypeStruct((T, W), BF16), jax.ShapeDtypeStruct((T, W), BF16),
                   jax.ShapeDtypeStruct((T, LANES), F32)],
        compiler_params=_cparams(),
    )(jt, ht, it, qn, qr, kn, kr, v, o, do, lse)


def _pad_cols(w, n):
    return jnp.pad(w, ((0, 0), (0, n - w.shape[1])))


def _local_step(x, positions, w, target, late=None, reduce_early=None, reduce_last=None):
    w = dict(w)
    Bn, S, D = x.shape
    T = Bn * S
    depth = w['ln_mix_g'].shape[0]
    alpha = (2 * depth) ** 0.25
    d_inner = w['ssd_norm_g'].shape[1]
    n_heads_ssd = d_inner // SSD_HEAD_DIM
    G = SSD_N_GROUPS
    gn = G * SSD_D_STATE
    conv_dim = d_inner + 2 * gn
    hpg = n_heads_ssd // G
    nh = D // MLA_NOPE
    kv_rank = w['kv_norm_g'].shape[0]
    q_rank = w['q_norm_g'].shape[1]
    H = w['ffn_conv_b'].shape[1] // 2
    scale = (MLA_NOPE + MLA_ROPE) ** -0.5
    assert S % SSD_CHUNK == 0 and hpg % 2 == 0 and SSD_D_STATE == LANES and n_heads_ssd <= LANES

    X = x.reshape(T, D)
    tgt = target.reshape(T, D)

    inv_freq = 1.0 / (ROPE_THETA ** (jnp.arange(0, MLA_ROPE, 2, dtype=F32) / MLA_ROPE))
    ang = positions.reshape(T).astype(F32)[:, None] * inv_freq
    cos, sin = jnp.cos(ang), jnp.sin(ang)
    zpad = jnp.zeros((T, LANES - MLA_ROPE), F32)
    cc = jnp.concatenate([cos, cos, zpad], axis=1)
    ss = jnp.concatenate([-sin, sin, zpad], axis=1)

    w_in = w['ssd_in_proj'][0]
    w_z = w_in[:, :d_inner]
    w_xbc = w_in[:, d_inner:d_inner + conv_dim]
    w_dt = _pad_cols(w_in[:, d_inner + conv_dim:], LANES)
    ssd_cw = w['ssd_conv_w'][0]
    ssd_cb = w['ssd_conv_b']
    dt_bias = _pad_cols(w['ssd_dt_bias'], LANES)
    a_log = _pad_cols(w['ssd_A_log'], LANES)
    d_lane = jnp.repeat(w['ssd_D'], SSD_HEAD_DIM, axis=1)
    ssd_ng = w['ssd_norm_g']

    Xb = X.astype(BF16)
    z = _mm(Xb, w_z, name='ssd_in_z')
    xbc_pre = _mm(Xb, w_xbc, name='ssd_in_xbc')
    dt_pre = _mm(Xb, w_dt, name='ssd_in_dt')
    xbc, xbc_conv = _ssd_conv_fwd(xbc_pre, ssd_cw, ssd_cb, S, name='ssd_conv')
    dtb, cumb = _ssd_dt_prep(dt_pre, dt_bias, a_log, n_heads_ssd, name='ssd_dt_prep')
    y_scan, states, gathered = _scan_fwd(xbc, dtb, cumb, S, d_inner, name='ssd_scan',
                                         gather=late[0] if late else ())
    if late:
        w.update(late[1](gathered))

    w_out = w['ssd_out_proj'][0]
    w_kvc = w['kv_down_proj'][:, :kv_rank]
    w_kvr = _pad_cols(w['kv_down_proj'][:, kv_rank:], LANES)
    kv_g = w['kv_norm_g'].reshape(1, kv_rank)
    w_uk = w['kv_up_k']
    w_uv = w['kv_up_v']
    w_qd = w['q_down_proj'][0]
    q_g = w['q_norm_g']
    qu = w['q_up_proj'][0].reshape(q_rank, nh, MLA_NOPE + MLA_ROPE)
    w_qn = qu[:, :, :MLA_NOPE].reshape(q_rank, nh * LANES)
    w_qr = jnp.pad(qu[:, :, MLA_NOPE:], ((0, 0), (0, 0), (0, LANES - MLA_ROPE))).reshape(q_rank, nh * LANES)
    w_ao = w['attn_out_proj'][0]

    def ffn_parts(i):
        return dict(wu=w['ffn_up'][i], cw=w['ffn_conv_w'][i], cb=w['ffn_conv_b'][i:i + 1], wd=w['ffn_down'][i])

    def ln(name, i):
        return w[name][i:i + 1]

    def ffn_fwd(hb, i):
        f = ffn_parts(i)
        u = _mm(hb, f['wu'], name=f'ffn{i}_up')
        a = _ffn_act_fwd(u, f['cw'], f['cb'], S, name=f'ffn{i}_act')
        return _mm(a, f['wd'], name=f'ffn{i}_down'), (u, a)

    yn = _gate_fwd(y_scan, xbc, z, d_lane, ssd_ng, d_inner, name='ssd_gate')
    mix0 = _mm(yn, w_out, name='ssd_out')
    h1, s1, h1b = _ln_fwd(X, mix0, ln('ln_mix_g', 0), ln('ln_mix_b', 0), alpha, name='ln_mix0')
    ff0, ffn0_saved = ffn_fwd(h1b, 0)
    h2, s2, h2b = _ln_fwd(h1, ff0, ln('ln_ffn_g', 0), ln('ln_ffn_b', 0), alpha, name='ln_ffn0')

    ckv = _mm(h2b, w_kvc, name='kv_down_c')
    kr_pre = _mm(h2b, w_kvr, name='kv_down_r')
    ckvn = _rms_fwd(ckv, kv_g, name='kv_norm')
    kr = _rope(kr_pre, cc, ss, transpose=False, name='k_rope')
    kn = _mm(ckvn, w_uk, out_dtype=BF16, name='kv_up_k')
    v = _mm(ckvn, w_uv, out_dtype=BF16, name='kv_up_v')
    cq_pre = _mm(h2b, w_qd, name='q_down')
    cq = _rms_fwd(cq_pre, q_g, name='q_norm')
    qn = _mm(cq, w_qn, out_dtype=BF16, name='q_up_n')
    qr_pre = _mm(cq, w_qr, name='q_up_r')
    qr = _rope(qr_pre, cc, ss, transpose=False, name='q_rope')
    o, ob, lse = _attention_fwd(qn, qr, kn, kr, v, S, scale, name='attn_fwd')
    mix1 = _mm(ob, w_ao, name='attn_out')
    h3, s3, h3b = _ln_fwd(h2, mix1, ln('ln_mix_g', 1), ln('ln_mix_b', 1), alpha, name='ln_mix1')
    ff1, ffn1_saved = ffn_fwd(h3b, 1)
    h4, s4, _ = _ln_fwd(h3, ff1, ln('ln_ffn_g', 1), ln('ln_ffn_b', 1), alpha, name='ln_ffn1')
    sq, dh4 = _loss_head(h4, tgt, name='loss_head')

    g = {}

    def ffn_bwd(ds, dsb, hb_in, saved, i):
        f = ffn_parts(i)
        u, a = saved
        d_wd = _mm(a, dsb, ta=True, name=f'ffn{i}_down_dw')
        da = _mm(dsb, f['wd'], tb=True, out_dtype=BF16, name=f'ffn{i}_down_dx')
        dug, duv, dwg, dwv, dbg, dbv = _ffn_act_bwd(u, da, f['cw'], f['cb'], S, name=f'ffn{i}_act_bwd')
        d_wu = _mm(hb_in, dug, ta=True, name=f'ffn{i}_up_g_dw', into=lax.empty((D, 2 * H), F32))
        d_wu = _mm(hb_in, duv, ta=True, name=f'ffn{i}_up_v_dw', into=d_wu, into_col=H)
        dh = _mm(dug, f['wu'], tb=True, add=ds, add_scale=alpha, name=f'ffn{i}_up_g_dx')
        dh = _mm(duv, f['wu'], tb=True, add=dh, name=f'ffn{i}_up_v_dx', b_koff=H)
        return dh, d_wd, d_wu, jnp.concatenate([dwg, dwv], axis=1), jnp.concatenate([dbg, dbv], axis=1)

    ds4, ds4b, dg_f1, db_f1 = _ln_bwd(dh4, s4, ln('ln_ffn_g', 1), name='ln_ffn1_bwd')
    dh3, d_wd1, d_wu1, d_fcw1, d_fcb1 = ffn_bwd(ds4, ds4b, h3b, ffn1_saved, 1)
    ds3, ds3b, dg_m1, db_m1 = _ln_bwd(dh3, s3, ln('ln_mix_g', 1), name='ln_mix1_bwd')

    g['attn_out_proj'] = _mm(ob, ds3b, ta=True, name='attn_out_dw')[None]
    do = _mm(ds3b, w_ao, tb=True, name='attn_out_dx')
    dqn, dqr = _attention_dq(qn, qr, kn, kr, v, o, do, lse, S, scale, name='attn_bwd_dq')
    dkn, dv, dkr = _attention_dkv(qn, qr, kn, kr, v, o, do, lse, S, scale, name='attn_bwd_dkv')
    dqr_pre = _rope(dqr, cc, ss, transpose=True, name='q_rope_bwd')
    dkr_pre = _rope(dkr, cc, ss, transpose=True, name='k_rope_bwd')

    d_wqn = _mm(cq, dqn, ta=True, name='q_up_n_dw').reshape(q_rank, nh, LANES)
    d_wqr = _mm(cq, dqr_pre, ta=True, name='q_up_r_dw').reshape(q_rank, nh, LANES)[:, :, :MLA_ROPE]
    g['q_up_proj'] = jnp.concatenate([d_wqn, d_wqr], axis=2).reshape(1, q_rank, nh * (MLA_NOPE + MLA_ROPE))
    dcq = _mm(dqn, w_qn, tb=True, name='q_up_n_dx')
    dcq = _mm(dqr_pre, w_qr, tb=True, add=dcq, name='q_up_r_dx')
    dcq_pre, d_qg = _rms_bwd(dcq, cq_pre, q_g, name='q_norm_bwd')
    g['q_norm_g'] = d_qg
    g['q_down_proj'] = _mm(h2b, dcq_pre, ta=True, name='q_down_dw')[None]

    g['kv_up_k'] = _mm(ckvn, dkn, ta=True, name='kv_up_k_dw')
    g['kv_up_v'] = _mm(ckvn, dv, ta=True, name='kv_up_v_dw')
    dckvn = _mm(dkn, w_uk, tb=True, name='kv_up_k_dx')
    dckvn = _mm(dv, w_uv, tb=True, add=dckvn, name='kv_up_v_dx')
    dckv, d_kvg = _rms_bwd(dckvn, ckv, kv_g, name='kv_norm_bwd')
    g['kv_norm_g'] = d_kvg.reshape(kv_rank)
    g['kv_down_proj'] = jnp.concatenate(
        [_mm(h2b, dckv, ta=True, name='kv_down_c_dw'),
         _mm(h2b, dkr_pre, ta=True, name='kv_down_r_dw')[:, :MLA_ROPE]], axis=1)

    dh2 = _mm(dcq_pre, w_qd, tb=True, add=ds3, add_scale=alpha, name='q_down_dx')
    dh2 = _mm(dckv, w_kvc, tb=True, add=dh2, name='kv_down_c_dx')
    dh2 = _mm(dkr_pre, w_kvr, tb=True, add=dh2, name='kv_down_r_dx')

    ds2, ds2b, dg_f0, db_f0 = _ln_bwd(dh2, s2, ln('ln_ffn_g', 0), name='ln_ffn0_bwd')
    dh1, d_wd0, d_wu0, d_fcw0, d_fcb0 = ffn_bwd(ds2, ds2b, h1b, ffn0_saved, 0)
    ds1, ds1b, dg_m0, db_m0 = _ln_bwd(dh1, s1, ln('ln_mix_g', 0), name='ln_mix0_bwd')

    g['ssd_out_proj'] = _mm(yn, ds1b, ta=True, name='ssd_out_dw')[None]
    dyn = _mm(ds1b, w_out, tb=True, out_dtype=BF16, name='ssd_out_dx')
    g['ffn_up'] = jnp.stack([d_wu0, d_wu1])
    g['ffn_down'] = jnp.stack([d_wd0, d_wd1])
    early_blocks = reduce_early[0](g) if reduce_early else ()
    (dy1, dz, d_ng, d_dl), from_sibling = _gate_bwd(dyn, y_scan, xbc, z, d_lane, ssd_ng, d_inner,
                                                    name='ssd_gate_bwd', exchange=early_blocks)
    (dxs, dB, dC, ddt_g, da_g), early = _scan_bwd(
        xbc, dtb, cumb, states, dy1, d_lane, S, d_inner, name='ssd_scan_bwd',
        exchange=reduce_early[1](early_blocks, from_sibling) if reduce_early else ())
    ddt_pre, d_bias, d_alog = _dt_bwd(ddt_g, da_g, dt_pre, dt_bias, a_log, name='ssd_dt_bwd')
    dxbc_pre, d_scw, d_scb = _ssd_conv_bwd(xbc_pre, xbc_conv, [dxs, dB, dC], ssd_cw, S, name='ssd_conv_bwd')
    g['ssd_in_proj'] = jnp.concatenate(
        [_mm(Xb, dz, ta=True, name='ssd_in_z_dw'), _mm(Xb, dxbc_pre, ta=True, name='ssd_in_xbc_dw'),
         _mm(Xb, ddt_pre, ta=True, name='ssd_in_dt_dw')[:, :n_heads_ssd]], axis=1)[None]
    g['ssd_conv_w'] = d_scw[None]
    g['ssd_conv_b'] = d_scb
    g['ssd_norm_g'] = d_ng
    g['ffn_conv_w'] = jnp.stack([d_fcw0, d_fcw1])
    dx = _mm(dz, w_z, tb=True, add=ds1, add_scale=alpha, name='ssd_in_z_dx')
    dx = _mm(dxbc_pre, w_xbc, tb=True, add=dx, name='ssd_in_xbc_dx', exchange=reduce_last(g) if reduce_last else ())
    dx, last = dx if reduce_last else (dx, None)
    dx = _mm(ddt_pre, w_dt, tb=True, add=dx, name='ssd_in_dt_dx')

    g['ssd_dt_bias'] = d_bias[:, :n_heads_ssd]
    g['ssd_A_log'] = d_alog[:, :n_heads_ssd]
    g['ssd_D'] = jnp.sum(d_dl.reshape(1, n_heads_ssd, SSD_HEAD_DIM), axis=2)
    g['ffn_conv_b'] = jnp.concatenate([d_fcb0, d_fcb1], axis=0)
    g['ln_mix_g'] = jnp.concatenate([dg_m0, dg_m1], axis=0)
    g['ln_mix_b'] = jnp.concatenate([db_m0, db_m1], axis=0)
    g['ln_ffn_g'] = jnp.concatenate([dg_f0, dg_f1], axis=0)
    g['ln_ffn_b'] = jnp.concatenate([db_f0, db_f1], axis=0)
    return sq, dx.reshape(Bn, S, D), g, early, last


ANY = pl.BlockSpec(memory_space=pl.ANY)


def _all_gather(xs, name):
    n = len(xs)

    def body(*refs):
        start, forward, finish = _gather_plan(refs[:n], refs[n:2 * n], *refs[2 * n:])
        start()
        forward()
        finish()

    return pl.pallas_call(
        body, name=name, out_shape=_gather_shapes(xs), in_specs=[ANY] * n, out_specs=[ANY] * n,
        scratch_shapes=_gather_sems(n),
    )(*xs)


def _gather_shapes(xs):
    return [jax.ShapeDtypeStruct((N_DEV,) + a.shape, a.dtype) for a in xs]


def _gather_sems(n):
    return [pltpu.SemaphoreType.DMA((7 * n,)), pltpu.SemaphoreType.DMA((7 * n,)), pltpu.SemaphoreType.DMA((n,))]


def _gather_plan(x_refs, out_refs, send_sems, recv_sems, local_sems):
    n = len(x_refs)
    x, y, c = lax.axis_index("x"), lax.axis_index("y"), lax.axis_index("c")
    me, sibling = (x, y, c), (x, y, 1 - c)
    chips = [(1 - x, y), (x, 1 - y), (1 - x, 1 - y)]

    def rows(i, px, py, pc):
        return out_refs[i].at[4 * px + 2 * py + pc]

    def copy(i, k, block, to, own=False):
        return pltpu.make_async_remote_copy(
            src_ref=x_refs[i] if own else rows(i, *block), dst_ref=rows(i, *block),
            send_sem=send_sems.at[7 * i + k], recv_sem=recv_sems.at[7 * i + k],
            device_id=to, device_id_type=MESH)

    def mine():
        return [pltpu.make_async_copy(x_refs[i], rows(i, *me), local_sems.at[i]) for i in range(n)]

    def first():
        out = []
        for i in range(n):
            out.append(copy(i, 0, me, sibling, own=True))
            out += [copy(i, 1 + j, me, (*chip, c), own=True) for j, chip in enumerate(chips)]
        return out

    def passed():
        return [copy(i, 4 + j, (*chip, c), sibling) for j, chip in enumerate(chips) for i in range(n)]

    def start():
        for cp in mine() + first():
            cp.start()

    def forward():
        for j, chip in enumerate(chips):
            for i in range(n):
                copy(i, 1 + j, (*chip, c), me).wait_recv()
                copy(i, 4 + j, (*chip, c), sibling).start()

    def finish():
        for i in range(n):
            copy(i, 0, sibling, me).wait_recv()
            for j, chip in enumerate(chips):
                copy(i, 4 + j, (*chip, 1 - c), me).wait_recv()
        for cp in first() + passed():
            cp.wait_send()
        for cp in mine():
            cp.wait()

    return start, forward, finish


def _exchange_sibling(gs, name):
    n = len(gs)

    def body(*refs):
        start, finish = _sibling_plan(refs[:n], refs[n:2 * n], *refs[2 * n:])
        start()
        finish()

    return pl.pallas_call(
        body, name=name, out_shape=_sibling_shapes(gs), in_specs=[ANY] * n, out_specs=[ANY] * n,
        scratch_shapes=_sibling_sems(n),
    )(*gs)


def _sibling_shapes(gs):
    return [jax.ShapeDtypeStruct((4,) + g.shape[1:], g.dtype) for g in gs]


def _sibling_sems(n):
    return [pltpu.SemaphoreType.DMA((4 * n,)), pltpu.SemaphoreType.DMA((4 * n,))]


def _sibling_plan(g_refs, r_refs, send_sems, recv_sems):
    n = len(g_refs)
    x, y, c = lax.axis_index("x"), lax.axis_index("y"), lax.axis_index("c")

    def copies():
        return [pltpu.make_async_remote_copy(
            src_ref=g_refs[i].at[2 * k + (1 - c)], dst_ref=r_refs[i].at[k], send_sem=send_sems.at[4 * i + k],
            recv_sem=recv_sems.at[4 * i + k], device_id=(x, y, 1 - c), device_id_type=MESH)
            for i in range(n) for k in range(4)]

    def start():
        for cp in copies():
            cp.start()

    def finish():
        for cp in copies():
            cp.wait()

    return start, finish


def _chips_shapes(parts):
    return [jax.ShapeDtypeStruct((3,) + p.shape[1:], p.dtype) for p in parts]


def _chips_sems(n):
    return [pltpu.SemaphoreType.DMA((3 * n,)), pltpu.SemaphoreType.DMA((3 * n,))]


def _chips_plan(p_refs, r_refs, send_sems, recv_sems):
    n = len(p_refs)
    x, y, c = lax.axis_index("x"), lax.axis_index("y"), lax.axis_index("c")
    chips = [(1 - x, y), (x, 1 - y), (1 - x, 1 - y)]

    def copies():
        return [pltpu.make_async_remote_copy(
            src_ref=p_refs[i].at[2 * cx + cy], dst_ref=r_refs[i].at[j], send_sem=send_sems.at[3 * i + j],
            recv_sem=recv_sems.at[3 * i + j], device_id=(cx, cy, c), device_id_type=MESH)
            for i in range(n) for j, (cx, cy) in enumerate(chips)]

    def start():
        for cp in copies():
            cp.start()

    def finish():
        for cp in copies():
            cp.wait()

    return start, finish


def _row_tile(rows, cols):
    want = max(8, (256 * 1024) // cols)
    for t in range(min(rows, want) // 8 * 8, 7, -8):
        if rows % t == 0:
            return t
    return rows


def _add_sibling(gfull, r1, core, name):
    _, rows, lanes = gfull.shape
    tr = _row_tile(rows, lanes)

    def body(c_ref, g_ref, r_ref, o_ref):
        o_ref[...] = g_ref[...] + r_ref[...]

    return pl.pallas_call(
        body, name=name,
        grid_spec=pltpu.PrefetchScalarGridSpec(
            num_scalar_prefetch=1, grid=(4, rows // tr),
            in_specs=[pl.BlockSpec((1, tr, lanes), lambda k, r, c_ref: (2 * k + c_ref[0], r, 0)),
                      pl.BlockSpec((1, tr, lanes), lambda k, r, c_ref: (k, r, 0))],
            out_specs=pl.BlockSpec((1, tr, lanes), lambda k, r, c_ref: (k, r, 0))),
        out_shape=jax.ShapeDtypeStruct((4, rows, lanes), F32), compiler_params=_cparams(),
    )(core, gfull, r1)


def _adam_math(wv, gv, mv, vv):
    m = ADAM_B1 * mv + (1.0 - ADAM_B1) * gv
    v = ADAM_B2 * vv + (1.0 - ADAM_B2) * (gv * gv)
    m_hat = m / (1.0 - ADAM_B1 ** ADAM_STEP)
    v_hat = v / (1.0 - ADAM_B2 ** ADAM_STEP)
    delta = -ADAM_LR * (m_hat / (jnp.sqrt(v_hat) + ADAM_EPS) + ADAM_WD * wv)
    return delta, m, v


def _adam_sharded(part, r2, chip, wts, m, v, name):
    rows, lanes = wts.shape
    tr = _row_tile(rows, lanes)

    def body(k_ref, p_ref, r_ref, w_ref, m_ref, v_ref, g_ref, d_ref, mo_ref, vo_ref):
        gv = p_ref[0] + r_ref[0] + r_ref[1] + r_ref[2]
        delta, mn, vn = _adam_math(w_ref[...], gv, m_ref[...], v_ref[...])
        g_ref[...] = gv
        d_ref[...] = delta
        mo_ref[...] = mn
        vo_ref[...] = vn

    flat = pl.BlockSpec((tr, lanes), lambda r, k_ref: (r, 0))
    sh = jax.ShapeDtypeStruct((rows, lanes), F32)
    return pl.pallas_call(
        body, name=name,
        grid_spec=pltpu.PrefetchScalarGridSpec(
            num_scalar_prefetch=1, grid=(rows // tr,),
            in_specs=[pl.BlockSpec((1, tr, lanes), lambda r, k_ref: (k_ref[0], r, 0)),
                      pl.BlockSpec((3, tr, lanes), lambda r, k_ref: (0, r, 0)), flat, flat, flat],
            out_specs=[flat, flat, flat, flat]),
        out_shape=[sh, sh, sh, sh], compiler_params=_cparams(),
    )(chip, part, r2, wts, m, v)


def _adam_replicated(gall, wts, m, v, name):
    rows, lanes = wts.shape

    def body(ga_ref, w_ref, m_ref, v_ref, g_ref, d_ref, mo_ref, vo_ref):
        gv = ga_ref[0]
        for k in range(1, N_DEV):
            gv = gv + ga_ref[k]
        delta, mn, vn = _adam_math(w_ref[...], gv, m_ref[...], v_ref[...])
        g_ref[...] = gv
        d_ref[...] = delta
        mo_ref[...] = mn
        vo_ref[...] = vn

    sh = jax.ShapeDtypeStruct((rows, lanes), F32)
    return pl.pallas_call(body, name=name, out_shape=[sh, sh, sh, sh], compiler_params=_cparams())(gall, wts, m, v)


def _pack(arrs, dtype, row_mult):
    flat = jnp.concatenate([a.reshape(-1).astype(dtype) for a in arrs])
    n = flat.shape[0]
    unit = LANES * row_mult
    total = -(-n // unit) * unit
    return jnp.pad(flat, (0, total - n)).reshape(total // LANES, LANES)


def _unpack(flat2d, shapes):
    flat = flat2d.reshape(-1)
    out, off = [], 0
    for shp in shapes:
        n = math.prod(shp)
        out.append(flat[off:off + n].reshape(shp))
        off += n
    return out


def _unpack_gathered(gathered, shapes, axes):
    flat = gathered.reshape(N_DEV, -1)
    out, off = [], 0
    for shp, ax in zip(shapes, axes):
        n = math.prod(shp)
        seg = flat[:, off:off + n].reshape((N_DEV,) + tuple(shp))
        out.append(jnp.concatenate([seg[i] for i in range(N_DEV)], axis=ax))
        off += n
    return out


def _pack_chunks(fulls, axes, row_mult):
    per_dev = []
    for full, ax in zip(fulls, axes):
        parts = jnp.stack(jnp.split(full, N_DEV, axis=ax))
        per_dev.append(parts.reshape(N_DEV, -1))
    flat = jnp.concatenate(per_dev, axis=1)
    n = flat.shape[1]
    unit = LANES * row_mult
    total = -(-n // unit) * unit
    return jnp.pad(flat, ((0, 0), (0, total - n))).reshape(N_DEV, total // LANES, LANES)


def kernel(x, positions, ssd_in_proj, ssd_conv_w, ssd_conv_b, ssd_dt_bias, ssd_A_log, ssd_D, ssd_norm_g, ssd_out_proj, kv_down_proj, kv_norm_g, kv_up_k, kv_up_v, q_down_proj, q_norm_g, q_up_proj, attn_out_proj, ffn_up, ffn_conv_w, ffn_conv_b, ffn_down, ln_mix_g, ln_mix_b, ln_ffn_g, ln_ffn_b, loss_target, m_ssd_in_proj, m_ssd_conv_w, m_ssd_conv_b, m_ssd_dt_bias, m_ssd_A_log, m_ssd_D, m_ssd_norm_g, m_ssd_out_proj, m_kv_down_proj, m_kv_norm_g, m_kv_up_k, m_kv_up_v, m_q_down_proj, m_q_norm_g, m_q_up_proj, m_attn_out_proj, m_ffn_up, m_ffn_conv_w, m_ffn_conv_b, m_ffn_down, m_ln_mix_g, m_ln_mix_b, m_ln_ffn_g, m_ln_ffn_b, v_ssd_in_proj, v_ssd_conv_w, v_ssd_conv_b, v_ssd_dt_bias, v_ssd_A_log, v_ssd_D, v_ssd_norm_g, v_ssd_out_proj, v_kv_down_proj, v_kv_norm_g, v_kv_up_k, v_kv_up_v, v_q_down_proj, v_q_norm_g, v_q_up_proj, v_attn_out_proj, v_ffn_up, v_ffn_conv_w, v_ffn_conv_b, v_ffn_down, v_ln_mix_g, v_ln_mix_b, v_ln_ffn_g, v_ln_ffn_b):
    given = dict(locals())
    wl = {n: given[n] for n in WEIGHTS}
    ml = {n: given['m_' + n] for n in WEIGHTS}
    vl = {n: given['v_' + n] for n in WEIGHTS}
    sharded_axis = dict(SHARDED)
    big = [n for n, _ in SHARDED if n not in SHARDED_F32]
    small = [n for n, _ in SHARDED if n in SHARDED_F32]

    def as2d(a):
        return a.reshape(-1, a.shape[-1])

    def to_full(gathered, n):
        shp, ax = wl[n].shape, sharded_axis[n]
        moved = jnp.moveaxis(gathered.reshape((N_DEV,) + shp), 0, ax)
        return moved.reshape(shp[:ax] + (N_DEV * shp[ax],) + shp[ax + 1:])

    def to_chunks(full_grad, n):
        shp, ax = wl[n].shape, sharded_axis[n]
        split = full_grad.reshape(shp[:ax] + (N_DEV, shp[ax]) + shp[ax + 1:])
        return jnp.moveaxis(split, ax, 0).reshape((N_DEV,) + as2d(wl[n]).shape)

    first = [n for n in big if n in GATHERED_FIRST]
    late = [n for n in big if n not in GATHERED_FIRST]
    full = {n: wl[n] for n in REPLICATED}
    gathered = _all_gather([as2d(wl[n]).astype(BF16) for n in first] + [_pack([wl[n] for n in small], F32, 8)],
                           name='gather_weights')
    for n, gat in zip(first, gathered[:-1]):
        full[n] = to_full(gat, n)
    full.update(zip(small, _unpack_gathered(gathered[-1], [wl[n].shape for n in small],
                                            [sharded_axis[n] for n in small])))

    cx, cy, cc = lax.axis_index("x"), lax.axis_index("y"), lax.axis_index("c")
    core = jnp.reshape(cc, (1,)).astype(jnp.int32)
    chip = jnp.reshape(2 * cx + cy, (1,)).astype(jnp.int32)
    early_names = [n for n in big if n not in REDUCED_LAST]
    last_names = [n for n in big if n in REDUCED_LAST]
    parts = {}

    def add_sibling(names, chunked, received):
        parts.update((n, _add_sibling(gf, r, core, name=f'grads_add_sibling_{n}'))
                     for n, gf, r in zip(names, chunked, received))
        return [parts[n] for n in names]

    def early_blocks(grads):
        return [to_chunks(grads[n], n) for n in early_names]

    def early_parts(chunked, received):
        return add_sibling(early_names, chunked, received)

    def reduce_last(grads):
        chunked = [to_chunks(grads[n], n) for n in last_names]
        chunked.append(_pack_chunks([grads[n] for n in small], [sharded_axis[n] for n in small], 8))
        return add_sibling(last_names + ['small'], chunked, _exchange_sibling(chunked, name='grads_to_sibling_last'))

    sq, grad_x, grads, r2_early, r2_last = _local_step(
        x, positions, full, loss_target,
        late=([as2d(wl[n]).astype(BF16) for n in late], lambda gats: {n: to_full(g, n) for n, g in zip(late, gats)}),
        reduce_early=(early_blocks, early_parts), reduce_last=reduce_last)
    loss = lax.psum(0.5 * jnp.sum(sq) / x.shape[-1], ("x", "y", "c"))

    r2 = dict(zip(early_names, r2_early))
    r2.update(zip(last_names + ['small'], r2_last))
    res = {}
    kinds = ('grad', 'delta', 'new_m', 'new_v')
    for n in big:
        outs = _adam_sharded(parts[n], r2[n], chip, as2d(wl[n]), as2d(ml[n]), as2d(vl[n]), name=f'adamw_{n}')
        for kind, a in zip(kinds, outs):
            res[kind, n] = a.reshape(wl[n].shape)
    outs = _adam_sharded(parts['small'], r2['small'], chip, _pack([wl[n] for n in small], F32, 8),
                         _pack([ml[n] for n in small], F32, 8), _pack([vl[n] for n in small], F32, 8),
                         name='adamw_small_sharded')
    for kind, packed in zip(kinds, outs):
        for n, a in zip(small, _unpack(packed, [wl[n].shape for n in small])):
            res[kind, n] = a

    rep = list(REPLICATED)
    rshapes = [wl[n].shape for n in rep]
    gall, = _all_gather([_pack([grads[n] for n in rep], F32, 8)], name='gather_replicated_grads')
    outs = _adam_replicated(gall, _pack([wl[n] for n in rep], F32, 8), _pack([ml[n] for n in rep], F32, 8),
                            _pack([vl[n] for n in rep], F32, 8), name='adamw_replicated')
    for kind, packed in zip(('grad', 'delta', 'new_m', 'new_v'), outs):
        for n, a in zip(rep, _unpack(packed, rshapes)):
            res[kind, n] = a

    return (loss, grad_x, *[res['grad', n] for n in WEIGHTS], *[res['delta', n] for n in WEIGHTS],
            *[res['new_m', n] for n in WEIGHTS], *[res['new_v', n] for n in WEIGHTS])
```

```python
import math

import jax
import jax.numpy as jnp
from jax import lax
from jax.experimental import pallas as pl
from jax.experimental.pallas import tpu as pltpu

F32 = jnp.float32
BF16 = jnp.bfloat16
MESH = pl.DeviceIdType.MESH

N_DEV = 8
LANES = 128
MM_TILES = (1024, 1408, 896, 512, 384, 256, 128)
MM_VMEM_BUDGET = 38 * 1024 * 1024
MM_VMEM_LIMIT = 56 * 1024 * 1024
VMEM_LIMIT = 32 * 1024 * 1024
LN_EPS = 1e-5
RMS_EPS = 1e-6
SSD_HEAD_DIM = 64
SSD_N_GROUPS = 8
SSD_D_STATE = 128
SSD_CHUNK = 128
MLA_NOPE = 128
MLA_ROPE = 64
ROPE_THETA = 10000.0
ADAM_LR = 0.001
ADAM_B1 = 0.9
ADAM_B2 = 0.999
ADAM_EPS = 1e-08
ADAM_WD = 0.01
ADAM_STEP = 10
NEG_BIG = -1e30

SHARDED = (('ssd_in_proj', 2), ('ssd_conv_w', 2), ('ssd_conv_b', 1), ('ssd_norm_g', 1), ('ssd_out_proj', 1),
           ('kv_down_proj', 0), ('kv_up_k', 1), ('kv_up_v', 1), ('q_down_proj', 1), ('q_up_proj', 2),
           ('attn_out_proj', 1), ('ffn_up', 2), ('ffn_conv_w', 2), ('ffn_down', 1))
SHARDED_F32 = ('ssd_conv_w', 'ssd_conv_b', 'ssd_norm_g', 'ffn_conv_w')
GATHERED_FIRST = ('ssd_in_proj',)
REDUCED_LAST = ('ssd_in_proj',)
REPLICATED = ('ssd_dt_bias', 'ssd_A_log', 'ssd_D', 'kv_norm_g', 'q_norm_g', 'ffn_conv_b',
              'ln_mix_g', 'ln_mix_b', 'ln_ffn_g', 'ln_ffn_b')
WEIGHTS = ('ssd_in_proj', 'ssd_conv_w', 'ssd_conv_b', 'ssd_dt_bias', 'ssd_A_log', 'ssd_D', 'ssd_norm_g',
           'ssd_out_proj', 'kv_down_proj', 'kv_norm_g', 'kv_up_k', 'kv_up_v', 'q_down_proj', 'q_norm_g',
           'q_up_proj', 'attn_out_proj', 'ffn_up', 'ffn_conv_w', 'ffn_conv_b', 'ffn_down', 'ln_mix_g',
           'ln_mix_b', 'ln_ffn_g', 'ln_ffn_b')


def _cparams(limit=None):
    return pltpu.CompilerParams(vmem_limit_bytes=VMEM_LIMIT if limit is None else limit)


def _tile(n, cands):
    for c in cands:
        if n % c == 0:
            return c
    return n


def _sigmoid(x):
    return 1.0 / (1.0 + jnp.exp(-x))


def _iota(shape, axis):
    return lax.broadcasted_iota(jnp.int32, shape, axis)


def _mm(a, b, *, ta=False, tb=False, add=None, add_scale=1.0, out_dtype=F32, name, b_koff=0, into=None, into_col=0,
        exchange=()):
    if ta:
        K, M = a.shape
    else:
        M, K = a.shape
    if tb:
        N, Kb = b.shape
    else:
        Kb, N = b.shape
    assert b_koff + K <= Kb, (a.shape, b.shape, ta, tb, b_koff)
    tm = _tile(M, MM_TILES)
    tn = _tile(N, MM_TILES)
    tk = K if K <= MM_TILES[0] and b_koff == 0 and Kb == K else _tile(K, MM_TILES)
    nk = K // tk
    assert b_koff % tk == 0 and (into is None or (into.shape[0] == M and into.dtype == out_dtype))
    kb = b_koff // tk

    def vmem_estimate(tm_, tn_):
        ins = 2 * (tm_ * tk * a.dtype.itemsize + tk * tn_ * b.dtype.itemsize)
        outs = tm_ * tn_ * (2 * jnp.dtype(out_dtype).itemsize + 4 + (4 if nk > 1 else 0))
        return ins + outs + (2 * tm_ * tn_ * add.dtype.itemsize if add is not None else 0)

    while vmem_estimate(tm, tn) > MM_VMEM_BUDGET:
        can_m, can_n = tm % (2 * LANES) == 0, tn % (2 * LANES) == 0
        if can_m and (tm >= tn or not can_n):
            tm //= 2
        elif can_n:
            tn //= 2
        else:
            break

    assert into_col % tn == 0
    jb = into_col // tn

    nx = len(exchange)
    n_in = (add is not None) + (into is not None)
    grid = (M // tm, N // tn, nk)

    def body(a_ref, b_ref, *rest):
        add_ref = rest[0] if add is not None else None
        o_ref = rest[n_in + nx]
        acc = rest[n_in + 2 * nx + 1] if nk > 1 else None
        k = pl.program_id(2)
        if nx:
            start, finish_exchange = _chips_plan(rest[n_in:n_in + nx], rest[n_in + nx + 1:n_in + 2 * nx + 1],
                                                 *rest[len(rest) - 2:])
            t, total = _grid_step(grid)
            pl.when(t == 0)(start)

        if ta:
            av = a_ref[...].astype(BF16).T
        else:
            av = a_ref[...].astype(BF16)
        bv = b_ref[...].astype(BF16)
        dn = (((1,), (1 if tb else 0,)), ((), ()))
        part = lax.dot_general(av, bv, dn, preferred_element_type=F32)

        def finish(r):
            if add is not None:
                r = r + add_scale * add_ref[...].astype(F32)
            o_ref[...] = r.astype(out_dtype)

        if nk == 1:
            finish(part)
        else:
            @pl.when(k == 0)
            def _():
                acc[...] = part

            @pl.when(k > 0)
            def _():
                acc[...] += part

            @pl.when(k == nk - 1)
            def _():
                finish(acc[...])

        if nx:
            pl.when(t == total - 1)(finish_exchange)

    a_spec = (pl.BlockSpec((tk, tm), lambda i, j, k: (k, i)) if ta
              else pl.BlockSpec((tm, tk), lambda i, j, k: (i, k)))
    b_spec = (pl.BlockSpec((tn, tk), lambda i, j, k: (j, k + kb)) if tb
              else pl.BlockSpec((tk, tn), lambda i, j, k: (k + kb, j)))
    in_specs = [a_spec, b_spec]
    args = [a, b]
    if add is not None:
        in_specs.append(pl.BlockSpec((tm, tn), lambda i, j, k: (i, j)))
        args.append(add)
    aliases = {}
    if into is not None:
        aliases = {len(args): 0}
        in_specs.append(pl.BlockSpec(memory_space=pl.ANY))
        args.append(into)
    any_spec = pl.BlockSpec(memory_space=pl.ANY)
    outs = pl.pallas_call(
        body, name=name, grid=grid,
        in_specs=in_specs + [any_spec] * nx,
        out_specs=[pl.BlockSpec((tm, tn), lambda i, j, k: (i, j + jb))] + [any_spec] * nx,
        out_shape=[jax.ShapeDtypeStruct((M, N) if into is None else into.shape, out_dtype)] + _chips_shapes(exchange),
        scratch_shapes=([pltpu.VMEM((tm, tn), F32)] if nk > 1 else []) + (_chips_sems(nx) if nx else []),
        input_output_aliases=aliases, compiler_params=_cparams(MM_VMEM_LIMIT),
    )(*args, *exchange)
    return (outs[0], list(outs[1:])) if nx else outs[0]


def _ln_fwd(h, mix, g, b, alpha, name):
    T, D = h.shape
    tm = _tile(T, (512, 256, 128))

    def body(h_ref, m_ref, g_ref, b_ref, y_ref, s_ref, yb_ref):
        s = alpha * h_ref[...] + m_ref[...]
        mu = jnp.mean(s, axis=1, keepdims=True)
        xc = s - mu
        var = jnp.mean(xc * xc, axis=1, keepdims=True)
        y = xc * lax.rsqrt(var + LN_EPS) * g_ref[...] + b_ref[...]
        y_ref[...] = y
        s_ref[...] = s
        yb_ref[...] = y.astype(BF16)

    row = pl.BlockSpec((tm, D), lambda i: (i, 0))
    vec = pl.BlockSpec((1, D), lambda i: (0, 0))
    return pl.pallas_call(
        body, name=name, grid=(T // tm,), in_specs=[row, row, vec, vec], out_specs=[row, row, row],
        out_shape=[jax.ShapeDtypeStruct((T, D), F32)] * 2 + [jax.ShapeDtypeStruct((T, D), BF16)],
        compiler_params=_cparams(),
    )(h, mix, g, b)


def _ln_bwd(dy, s, g, name):
    T, D = s.shape
    tm = _tile(T, (512, 256, 128))

    def body(dy_ref, s_ref, g_ref, ds_ref, dsb_ref, dg_ref, db_ref):
        @pl.when(pl.program_id(0) == 0)
        def _():
            dg_ref[...] = jnp.zeros_like(dg_ref)
            db_ref[...] = jnp.zeros_like(db_ref)

        sv = s_ref[...]
        dyv = dy_ref[...]
        mu = jnp.mean(sv, axis=1, keepdims=True)
        xc = sv - mu
        rstd = lax.rsqrt(jnp.mean(xc * xc, axis=1, keepdims=True) + LN_EPS)
        xhat = xc * rstd
        dxh = dyv * g_ref[...]
        m1 = jnp.mean(dxh, axis=1, keepdims=True)
        m2 = jnp.mean(dxh * xhat, axis=1, keepdims=True)
        ds = rstd * (dxh - m1 - xhat * m2)
        ds_ref[...] = ds
        dsb_ref[...] = ds.astype(BF16)
        dg_ref[...] += jnp.sum(dyv * xhat, axis=0, keepdims=True)
        db_ref[...] += jnp.sum(dyv, axis=0, keepdims=True)

    row = pl.BlockSpec((tm, D), lambda i: (i, 0))
    vec = pl.BlockSpec((1, D), lambda i: (0, 0))
    return pl.pallas_call(
        body, name=name, grid=(T // tm,), in_specs=[row, row, vec], out_specs=[row, row, vec, vec],
        out_shape=[jax.ShapeDtypeStruct((T, D), F32), jax.ShapeDtypeStruct((T, D), BF16),
                   jax.ShapeDtypeStruct((1, D), F32), jax.ShapeDtypeStruct((1, D), F32)],
        compiler_params=_cparams(),
    )(dy, s, g)


def _loss_head(y, target, name):
    T, D = y.shape
    tm = _tile(T, (512, 256, 128))

    def body(y_ref, t_ref, sq_ref, dy_ref):
        @pl.when(pl.program_id(0) == 0)
        def _():
            sq_ref[...] = jnp.zeros_like(sq_ref)

        e = y_ref[...] - t_ref[...]
        sq_ref[...] += jnp.sum(e * e, axis=0, keepdims=True)
        dy_ref[...] = e * (1.0 / D)

    row = pl.BlockSpec((tm, D), lambda i: (i, 0))
    vec = pl.BlockSpec((1, D), lambda i: (0, 0))
    return pl.pallas_call(
        body, name=name, grid=(T // tm,), in_specs=[row, row], out_specs=[vec, row],
        out_shape=[jax.ShapeDtypeStruct((1, D), F32), jax.ShapeDtypeStruct((T, D), F32)],
        compiler_params=_cparams(),
    )(y, target)


def _shift_down(x, j):
    if j == 0:
        return x
    return jnp.where(_iota(x.shape, 0) >= j, pltpu.roll(x, j, 0), 0.0)


def _shift_up(x, j):
    if j == 0:
        return x
    n = x.shape[0]
    return jnp.where(_iota(x.shape, 0) < n - j, pltpu.roll(x, n - j, 0), 0.0)


def _conv_val(x, w_ref, b_ref):
    width = w_ref.shape[0]
    y = b_ref[...] + w_ref[width - 1:width, :] * x
    for j in range(1, width):
        y = y + w_ref[width - 1 - j:width - j, :] * _shift_down(x, j)
    return y


def _conv_bwd_val(x, dc, w_ref, dw_ref, db_ref):
    width = w_ref.shape[0]
    dx = w_ref[width - 1:width, :] * dc
    dw_ref[width - 1:width, :] += jnp.sum(dc * x, axis=0, keepdims=True)
    for j in range(1, width):
        sj = _shift_up(dc, j)
        dx = dx + w_ref[width - 1 - j:width - j, :] * sj
        dw_ref[width - 1 - j:width - j, :] += jnp.sum(sj * x, axis=0, keepdims=True)
    db_ref[...] += jnp.sum(dc, axis=0, keepdims=True)
    return dx


def _ffn_specs(H, S, width, cb):
    cb = _tile(H, (cb, LANES))
    nC = H // cb
    return dict(
        g=pl.BlockSpec((S, cb), lambda j, b: (b, j)), v=pl.BlockSpec((S, cb), lambda j, b: (b, j + nC)),
        wg=pl.BlockSpec((width, cb), lambda j, b: (0, j)), wv=pl.BlockSpec((width, cb), lambda j, b: (0, j + nC)),
        bg=pl.BlockSpec((1, cb), lambda j, b: (0, j)), bv=pl.BlockSpec((1, cb), lambda j, b: (0, j + nC))), nC


def _ffn_act_fwd(u, cw, cb, S, name):
    T, H2 = u.shape
    H = H2 // 2
    sp, nC = _ffn_specs(H, S, cw.shape[0], 2 * LANES)

    def body(ug_ref, uv_ref, wg_ref, wv_ref, bg_ref, bv_ref, a_ref):
        g = _conv_val(ug_ref[...], wg_ref, bg_ref)
        v = _conv_val(uv_ref[...], wv_ref, bv_ref)
        a_ref[...] = (g * _sigmoid(g) * v).astype(BF16)

    return pl.pallas_call(
        body, name=name, grid=(nC, T // S),
        in_specs=[sp['g'], sp['v'], sp['wg'], sp['wv'], sp['bg'], sp['bv']], out_specs=sp['g'],
        out_shape=jax.ShapeDtypeStruct((T, H), BF16), compiler_params=_cparams(),
    )(u, u, cw, cw, cb, cb)


def _ffn_act_bwd(u, da, cw, cb, S, name):
    T, H2 = u.shape
    H = H2 // 2
    width = cw.shape[0]
    sp, nC = _ffn_specs(H, S, width, LANES)

    def body(ug_ref, uv_ref, da_ref, wg_ref, wv_ref, bg_ref, bv_ref,
             dug_ref, duv_ref, dwg_ref, dwv_ref, dbg_ref, dbv_ref):
        @pl.when(pl.program_id(1) == 0)
        def _():
            for r in (dwg_ref, dwv_ref, dbg_ref, dbv_ref):
                r[...] = jnp.zeros_like(r)

        def conv_bwd(x, dc, w_ref, dw_ref, db_ref):
            dx = w_ref[width - 1:width, :] * dc
            for j in range(1, width):
                dx = dx + w_ref[width - 1 - j:width - j, :] * _shift_up(dc, j)
            for j in range(width):
                dw_ref[width - 1 - j:width - j, :] += jnp.sum(dc * _shift_down(x, j), axis=0, keepdims=True)
            db_ref[...] += jnp.sum(dc, axis=0, keepdims=True)
            return dx

        xg = ug_ref[...]
        xv = uv_ref[...]
        g = _conv_val(xg, wg_ref, bg_ref)
        v = _conv_val(xv, wv_ref, bv_ref)
        dav = da_ref[...].astype(F32)
        sg = _sigmoid(g)
        dg = dav * v * sg * (1.0 + g * (1.0 - sg))
        dv = dav * g * sg
        dug_ref[...] = conv_bwd(xg, dg, wg_ref, dwg_ref, dbg_ref).astype(BF16)
        duv_ref[...] = conv_bwd(xv, dv, wv_ref, dwv_ref, dbv_ref).astype(BF16)

    act = jax.ShapeDtypeStruct((T, H), BF16)
    wsh = jax.ShapeDtypeStruct((width, H), F32)
    bsh = jax.ShapeDtypeStruct((1, H), F32)
    return pl.pallas_call(
        body, name=name, grid=(nC, T // S),
        in_specs=[sp['g'], sp['v'], sp['g'], sp['wg'], sp['wv'], sp['bg'], sp['bv']],
        out_specs=[sp['g'], sp['g'], sp['wg'], sp['wg'], sp['bg'], sp['bg']],
        out_shape=[act, act, wsh, wsh, bsh, bsh], compiler_params=_cparams(),
    )(u, u, da, cw, cw, cb, cb)


def _ssd_conv_fwd(x, cw, cb, S, name):
    T, C = x.shape
    Cb = _tile(C, (256, 128))
    width = cw.shape[0]

    def body(x_ref, w_ref, b_ref, y_ref, c_ref):
        c = _conv_val(x_ref[...], w_ref, b_ref)
        c_ref[...] = c
        y_ref[...] = c * _sigmoid(c)

    blk = pl.BlockSpec((S, Cb), lambda j, b: (b, j))
    wblk = pl.BlockSpec((width, Cb), lambda j, b: (0, j))
    bblk = pl.BlockSpec((1, Cb), lambda j, b: (0, j))
    return pl.pallas_call(
        body, name=name, grid=(C // Cb, T // S), in_specs=[blk, wblk, bblk], out_specs=[blk, blk],
        out_shape=[jax.ShapeDtypeStruct((T, C), F32)] * 2, compiler_params=_cparams(),
    )(x, cw, cb)


def _ssd_conv_bwd(x, c, dys, cw, S, name):
    T, C = x.shape
    Cb = _tile(C, (256, 128))
    width = cw.shape[0]
    assert all(d.shape[1] % Cb == 0 for d in dys) and sum(d.shape[1] for d in dys) == C

    def part(dy, j0, dx_prev, k):
        n = dy.shape[1] // Cb

        def body(x_ref, c_ref, dy_ref, w_ref, prev_ref, dx_ref, dw_ref, db_ref):
            @pl.when(pl.program_id(1) == 0)
            def _():
                dw_ref[...] = jnp.zeros_like(dw_ref)
                db_ref[...] = jnp.zeros_like(db_ref)

            cval = c_ref[...]
            sc = _sigmoid(cval)
            dc = dy_ref[...] * sc * (1.0 + cval * (1.0 - sc))
            dx_ref[...] = _conv_bwd_val(x_ref[...], dc, w_ref, dw_ref, db_ref).astype(BF16)

        wide = pl.BlockSpec((S, Cb), lambda j, b: (b, j + j0))
        return pl.pallas_call(
            body, name=f'{name}_{k}', grid=(n, T // S),
            in_specs=[wide, wide, pl.BlockSpec((S, Cb), lambda j, b: (b, j)),
                      pl.BlockSpec((width, Cb), lambda j, b: (0, j + j0)), pl.BlockSpec(memory_space=pl.ANY)],
            out_specs=[wide, pl.BlockSpec((width, Cb), lambda j, b: (0, j)), pl.BlockSpec((1, Cb), lambda j, b: (0, j))],
            out_shape=[jax.ShapeDtypeStruct((T, C), BF16), jax.ShapeDtypeStruct((width, n * Cb), F32),
                       jax.ShapeDtypeStruct((1, n * Cb), F32)],
            input_output_aliases={4: 0}, compiler_params=_cparams(),
        )(x, c, dy, cw, dx_prev)

    dx = lax.empty((T, C), BF16)
    dws, dbs, j0 = [], [], 0
    for k, dy in enumerate(dys):
        dx, dw, db = part(dy, j0, dx, k)
        dws.append(dw)
        dbs.append(db)
        j0 += dy.shape[1] // Cb
    return dx, jnp.concatenate(dws, axis=1), jnp.concatenate(dbs, axis=1)


def _softplus(x):
    e = jnp.exp(-jnp.abs(x))
    return jnp.maximum(x, 0.0) + jnp.where(e < 1e-4, e * (1.0 - 0.5 * e), jnp.log(1.0 + e))


def _cumsum_rows(x):
    n = x.shape[0]
    row = _iota(x.shape, 0)
    k = 1
    while k < n:
        x = x + jnp.where(row >= k, pltpu.roll(x, k, 0), 0.0)
        k *= 2
    return x


def _rev_cumsum_rows(x):
    n = x.shape[0]
    row = _iota(x.shape, 0)
    k = 1
    while k < n:
        x = x + jnp.where(row < n - k, pltpu.roll(x, n - k, 0), 0.0)
        k *= 2
    return x


def _col(x, lane_ids, h):
    return jnp.sum(jnp.where(lane_ids == h, x, 0.0), axis=1, keepdims=True)


def _bdot(a, b, dims):
    return lax.dot_general(a.astype(BF16), b.astype(BF16), (dims, ((), ())), preferred_element_type=F32)


NN = ((1,), (0,))
NT = ((1,), (1,))


def _ssd_dt_prep(dt_pre, dt_bias, a_log, n_heads, name):
    T = dt_pre.shape[0]
    L = SSD_CHUNK

    def body(dtp_ref, bias_ref, alog_ref, dtb_ref, cumb_ref):
        dt = _softplus(dtp_ref[...] + bias_ref[...])
        cum = _cumsum_rows(dt * (-jnp.exp(alog_ref[...])))
        lane = _iota((L, LANES), 1)
        for h in range(n_heads):
            hcols = pl.ds(h * LANES, LANES)
            dtb_ref[:, hcols] = jnp.broadcast_to(_col(dt, lane, h), (L, LANES))
            cumb_ref[:, hcols] = jnp.broadcast_to(_col(cum, lane, h), (L, LANES))

    row = pl.BlockSpec((L, LANES), lambda i: (i, 0))
    vec = pl.BlockSpec((1, LANES), lambda i: (0, 0))
    wide = pl.BlockSpec((L, n_heads * LANES), lambda i: (i, 0))
    return pl.pallas_call(
        body, name=name, grid=(T // L,), in_specs=[row, vec, vec], out_specs=[wide, wide],
        out_shape=[jax.ShapeDtypeStruct((T, n_heads * LANES), F32)] * 2, compiler_params=_cparams(),
    )(dt_pre, dt_bias, a_log)


SCAN_GROUPS_FWD = 8
SCAN_GROUPS_BWD = 4


def _scan_specs(nc, d_inner, hpg, GP):
    L = SSD_CHUNK
    G = SSD_N_GROUPS
    gw = GP * hpg * SSD_HEAD_DIM
    bw = GP * LANES
    assert G % GP == 0 and d_inner % bw == 0
    nb = d_inner // bw
    return dict(
        xs=pl.BlockSpec((L, gw), lambda b, q, c: (b * nc + c, q)),
        B=pl.BlockSpec((L, bw), lambda b, q, c: (b * nc + c, nb + q)),
        C=pl.BlockSpec((L, bw), lambda b, q, c: (b * nc + c, nb + G // GP + q)),
        heads=pl.BlockSpec((L, GP * hpg * LANES), lambda b, q, c: (b * nc + c, q)),
        gvec=pl.BlockSpec((1, gw), lambda b, q, c: (0, q)),
        grp=pl.BlockSpec((L, bw), lambda b, q, c: (b * nc + c, q)),
        st=pl.BlockSpec((1, GP * hpg // 2, SSD_D_STATE, LANES), lambda b, q, c: (b * nc + c, q, 0, 0)),
    )


def _grid_step(dims):
    t, total = 0, 1
    for ax, d in enumerate(dims):
        t = t * d + pl.program_id(ax)
        total *= d
    return t, total


def _scan_fwd(xbc, dtb, cumb, S, d_inner, name, gather=()):
    T = xbc.shape[0]
    Bn = T // S
    L = SSD_CHUNK
    G = SSD_N_GROUPS
    nc = S // L
    hpg = d_inner // SSD_HEAD_DIM // G
    pairs = hpg // 2
    GP = SCAN_GROUPS_FWD
    sp = _scan_specs(nc, d_inner, hpg, GP)
    ng = len(gather)

    def body(*refs):
        xs_ref, b_ref, c_ref, dtb_ref, cumb_ref = refs[:5]
        y_ref, st_ref = refs[5 + ng:7 + ng]
        state = refs[7 + 2 * ng]

        if ng:
            start, forward, finish = _gather_plan(refs[5:5 + ng], refs[7 + ng:7 + 2 * ng], *refs[8 + 2 * ng:])
            t, total = _grid_step((Bn, G // GP, nc))
            pl.when(t == 0)(start)
            pl.when(t == total // 2)(forward)

        @pl.when(pl.program_id(2) == 0)
        def _():
            state[...] = jnp.zeros_like(state)

        tril = _iota((L, L), 1) <= _iota((L, L), 0)
        lane = _iota((L, LANES), 1)
        lane1 = _iota((1, LANES), 1)
        last = _iota((L, 1), 0) == L - 1
        for gi, p in [(gi, p) for gi in range(GP) for p in range(pairs)]:
            if p == 0:
                Bm = b_ref[:, gi * LANES:(gi + 1) * LANES]
                Cm = c_ref[:, gi * LANES:(gi + 1) * LANES]
                Gm = _bdot(Cm, Bm, NT)
                S_cat = jnp.concatenate([state[gi * pairs + q] for q in range(pairs)], axis=1)
                Q_cat = _bdot(Cm, S_cat, NN)
                z_pairs, el_pairs = [], []
            sp_ = gi * pairs + p
            cols = pl.ds(sp_ * LANES, LANES)
            st_ref[0, sp_] = state[sp_]
            x_pair = xs_ref[:, cols]
            z_pair = jnp.zeros((L, LANES), F32)
            e_pair = jnp.zeros((L, LANES), F32)
            el_pair = jnp.zeros((1, LANES), F32)
            wms, xds = [], []
            for k in range(2):
                hcols = pl.ds((gi * hpg + 2 * p + k) * LANES, LANES)
                cum_col = cumb_ref[:, hcols]
                dt_col = dtb_ref[:, hcols]
                decay = jnp.exp(jnp.where(tril, cum_col - cum_col.T, NEG_BIG))
                mk = (lane >= SSD_HEAD_DIM) if k else (lane < SSD_HEAD_DIM)
                xd = jnp.where(mk, x_pair * dt_col, 0.0)
                wms.append((Gm * decay).astype(BF16))
                xds.append(xd.astype(BF16))
                cum_l = jnp.sum(jnp.where(last, cum_col, 0.0), axis=0, keepdims=True)
                e_pair = jnp.where(mk, jnp.exp(cum_col), e_pair)
                z_pair = z_pair + xd * jnp.exp(cum_l - cum_col)
                el_pair = jnp.where((lane1 >= SSD_HEAD_DIM) if k else (lane1 < SSD_HEAD_DIM), jnp.exp(cum_l), el_pair)
            y_pair = _bdot(jnp.concatenate(wms, axis=1), jnp.concatenate(xds, axis=0), NN)
            y_ref[:, cols] = y_pair + e_pair * Q_cat[:, p * LANES:(p + 1) * LANES]
            z_pairs.append(z_pair)
            el_pairs.append(el_pair)
            if p == pairs - 1:
                S_new = (S_cat * jnp.concatenate(el_pairs, axis=1)
                         + _bdot(Bm.T, jnp.concatenate(z_pairs, axis=1), NN))
                for q in range(pairs):
                    state[gi * pairs + q] = S_new[:, q * LANES:(q + 1) * LANES]

        if ng:
            pl.when(t == total - 1)(finish)

    outs = pl.pallas_call(
        body, name=name, grid=(Bn, G // GP, nc),
        in_specs=[sp['xs'], sp['B'], sp['C'], sp['heads'], sp['heads']] + [ANY] * ng,
        out_specs=[sp['xs'], sp['st']] + [ANY] * ng,
        out_shape=[jax.ShapeDtypeStruct((T, d_inner), F32),
                   jax.ShapeDtypeStruct((Bn * nc, G * pairs, SSD_D_STATE, LANES), F32)] + _gather_shapes(gather),
        scratch_shapes=[pltpu.VMEM((GP * pairs, SSD_D_STATE, LANES), F32)] + (_gather_sems(ng) if ng else []),
        compiler_params=_cparams(),
    )(xbc, xbc, xbc, dtb, cumb, *gather)
    return outs[0], outs[1], list(outs[2:])


def _scan_bwd(xbc, dtb, cumb, states, dy, d_lane, S, d_inner, name, exchange=()):
    T = xbc.shape[0]
    Bn = T // S
    L = SSD_CHUNK
    G = SSD_N_GROUPS
    nc = S // L
    hpg = d_inner // SSD_HEAD_DIM // G
    pairs = hpg // 2
    GP = SCAN_GROUPS_BWD
    sp = _scan_specs(nc, d_inner, hpg, GP)
    nx = len(exchange)

    def rev(spec):
        im = spec.index_map
        return pl.BlockSpec(spec.block_shape, lambda b, g, c: im(b, g, nc - 1 - c))

    def body(*refs):
        xs_ref, b_ref, c_ref, dtb_ref, cumb_ref, st_ref, dy_ref, dl_ref = refs[:8]
        dxs_ref, db_ref, dc_ref, ddt_ref, da_ref = refs[8 + nx:13 + nx]
        dstate = refs[13 + 2 * nx]
        g = pl.program_id(1)

        if nx:
            start, finish = _chips_plan(refs[8:8 + nx], refs[13 + nx:13 + 2 * nx], *refs[14 + 2 * nx:])
            t, total = _grid_step((Bn, G // GP, nc))
            pl.when(t == 0)(start)

        @pl.when(pl.program_id(2) == 0)
        def _():
            dstate[...] = jnp.zeros_like(dstate)

        tril = _iota((L, L), 1) <= _iota((L, L), 0)
        lane = _iota((L, LANES), 1)
        lane1 = _iota((1, LANES), 1)
        last = _iota((L, 1), 0) == L - 1
        for gi, p in [(gi, p) for gi in range(GP) for p in range(pairs)]:
            gcols = pl.ds(gi * LANES, LANES)
            if p == 0:
                Bm = b_ref[:, gcols]
                Cm = c_ref[:, gcols]
                Gm = _bdot(Cm, Bm, NT)
                dG = jnp.zeros((L, L), F32)
                dcum = jnp.zeros((L, LANES), F32)
                ddt = jnp.zeros((L, LANES), F32)
                S_cat = jnp.concatenate([st_ref[0, gi * pairs + q] for q in range(pairs)], axis=1)
                dS_cat = jnp.concatenate([dstate[gi * pairs + q] for q in range(pairs)], axis=1)
                Q_cat = _bdot(Cm, S_cat, NN)
                dZ_cat = _bdot(Bm, dS_cat, NN)
                dQs, zs, els = [], [], []
            sp_ = gi * pairs + p
            cols = pl.ds(sp_ * LANES, LANES)
            pcols = slice(p * LANES, (p + 1) * LANES)
            x_pair = xs_ref[:, cols]
            dy_pair = dy_ref[:, cols]
            Q = Q_cat[:, pcols]
            dZ = dZ_cat[:, pcols]
            prod = dS_cat[:, pcols] * S_cat[:, pcols]
            e_pair = jnp.zeros((L, LANES), F32)
            z_pair = jnp.zeros((L, LANES), F32)
            el_pair = jnp.zeros((1, LANES), F32)
            dx_pair = dl_ref[:, cols] * dy_pair
            heads = []
            for k in range(2):
                hcols = pl.ds((gi * hpg + 2 * p + k) * LANES, LANES)
                cum_col = cumb_ref[:, hcols]
                dt_col = dtb_ref[:, hcols]
                decay = jnp.exp(jnp.where(tril, cum_col - cum_col.T, NEG_BIG))
                mk = (lane >= SSD_HEAD_DIM) if k else (lane < SSD_HEAD_DIM)
                heads.append((cum_col, dt_col, decay, mk, jnp.where(mk, x_pair * dt_col, 0.0),
                              jnp.where(mk, dy_pair, 0.0), Gm * decay))
            dy_both = jnp.concatenate([hd[5] for hd in heads], axis=0).astype(BF16)
            dWm_both = _bdot(dy_both, heads[0][4] + heads[1][4], NT)
            dxd_pair = _bdot(jnp.concatenate([hd[6].T for hd in heads], axis=1), dy_both, NN)
            for k, (cum_col, dt_col, decay, mk, xd, dy_k, Wm) in enumerate(heads):
                h = (g * GP + gi) * hpg + 2 * p + k
                mk1 = (lane1 >= SSD_HEAD_DIM) if k else (lane1 < SSD_HEAD_DIM)
                dWm = jnp.where(tril, dWm_both[k * L:(k + 1) * L], 0.0)
                dxd = jnp.where(mk, dxd_pair, 0.0)
                dseg = dWm * Wm
                dG = dG + dWm * decay
                dcum_col = (jnp.sum(dseg, axis=1, keepdims=True)
                            - jnp.sum(dseg.T, axis=1, keepdims=True))
                cum_l = jnp.sum(jnp.where(last, cum_col, 0.0), axis=0, keepdims=True)
                e_col = jnp.exp(cum_col)
                w_col = jnp.exp(cum_l - cum_col)
                el = jnp.exp(cum_l)
                dcum_col = dcum_col + jnp.sum(dy_k * Q, axis=1, keepdims=True) * e_col
                de_e = jnp.sum(dZ * xd, axis=1, keepdims=True) * w_col
                dcum_col = dcum_col - de_e
                dcum_l = (jnp.sum(de_e, axis=0, keepdims=True)
                          + el * jnp.sum(jnp.sum(jnp.where(mk, prod, 0.0), axis=1, keepdims=True),
                                         axis=0, keepdims=True))
                dcum_col = dcum_col + jnp.where(last, dcum_l, 0.0)
                dxd = dxd + jnp.where(mk, dZ * w_col, 0.0)
                dx_pair = dx_pair + dxd * dt_col
                ddt = jnp.where(lane == h, jnp.sum(dxd * x_pair, axis=1, keepdims=True), ddt)
                dcum = jnp.where(lane == h, dcum_col, dcum)
                e_pair = jnp.where(mk, e_col, e_pair)
                z_pair = z_pair + xd * w_col
                el_pair = jnp.where(mk1, el, el_pair)
            dQs.append(e_pair * dy_pair)
            zs.append(z_pair)
            els.append(el_pair)
            dxs_ref[:, cols] = dx_pair
            if p == pairs - 1:
                dQ_cat = jnp.concatenate(dQs, axis=1)
                dS_new = dS_cat * jnp.concatenate(els, axis=1) + _bdot(Cm.T, dQ_cat, NN)
                for q in range(pairs):
                    dstate[gi * pairs + q] = dS_new[:, q * LANES:(q + 1) * LANES]
                dc_ref[:, gcols] = _bdot(dQ_cat, S_cat, NT) + _bdot(dG, Bm, NN)
                db_ref[:, gcols] = _bdot(jnp.concatenate(zs, axis=1), dS_cat, NT) + _bdot(dG.T, Cm, NN)
                ddt_ref[:, gcols] = ddt
                da_ref[:, gcols] = _rev_cumsum_rows(dcum)

        if nx:
            pl.when(t == total - 1)(finish)

    grp = jax.ShapeDtypeStruct((T, G * LANES), F32)
    outs = pl.pallas_call(
        body, name=name, grid=(Bn, G // GP, nc),
        in_specs=[rev(sp['xs']), rev(sp['B']), rev(sp['C']), rev(sp['heads']), rev(sp['heads']),
                  rev(sp['st']), rev(sp['xs']), sp['gvec']] + [ANY] * nx,
        out_specs=[rev(sp['xs']), rev(sp['grp']), rev(sp['grp']), rev(sp['grp']), rev(sp['grp'])] + [ANY] * nx,
        out_shape=[jax.ShapeDtypeStruct((T, d_inner), F32), grp, grp, grp, grp] + _chips_shapes(exchange),
        scratch_shapes=[pltpu.VMEM((GP * pairs, SSD_D_STATE, LANES), F32)] + (_chips_sems(nx) if nx else []),
        compiler_params=_cparams(),
    )(xbc, xbc, xbc, dtb, cumb, states, dy, d_lane, *exchange)
    return outs[:5], list(outs[5:])


def _dt_bwd(ddt_g, da_g, dt_pre, dt_bias, a_log, name):
    T = dt_pre.shape[0]
    G = SSD_N_GROUPS
    tm = _tile(T, (512, 256, 128))

    def body(ddt_ref, da_ref, dtp_ref, bias_ref, alog_ref, dx_ref, dbias_ref, dal_ref):
        @pl.when(pl.program_id(0) == 0)
        def _():
            dbias_ref[...] = jnp.zeros_like(dbias_ref)
            dal_ref[...] = jnp.zeros_like(dal_ref)

        ddt = ddt_ref[:, 0:LANES]
        da = da_ref[:, 0:LANES]
        for gi in range(1, G):
            ddt = ddt + ddt_ref[:, gi * LANES:(gi + 1) * LANES]
            da = da + da_ref[:, gi * LANES:(gi + 1) * LANES]
        xv = dtp_ref[...] + bias_ref[...]
        A = -jnp.exp(alog_ref[...])
        dx = (ddt + da * A) * _sigmoid(xv)
        dx_ref[...] = dx.astype(BF16)
        dbias_ref[...] += jnp.sum(dx, axis=0, keepdims=True)
        dal_ref[...] += jnp.sum(da * _softplus(xv), axis=0, keepdims=True) * A

    wide = pl.BlockSpec((tm, G * LANES), lambda i: (i, 0))
    row = pl.BlockSpec((tm, LANES), lambda i: (i, 0))
    vec = pl.BlockSpec((1, LANES), lambda i: (0, 0))
    vsh = jax.ShapeDtypeStruct((1, LANES), F32)
    return pl.pallas_call(
        body, name=name, grid=(T // tm,), in_specs=[wide, wide, row, vec, vec], out_specs=[row, vec, vec],
        out_shape=[jax.ShapeDtypeStruct((T, LANES), BF16), vsh, vsh], compiler_params=_cparams(),
    )(ddt_g, da_g, dt_pre, dt_bias, a_log)


def _gate_fwd(y, xbc, z, d_lane, ng, d_inner, name):
    T = y.shape[0]
    G = SSD_N_GROUPS
    gw = d_inner // G
    tm = _tile(T, (1024, 512, 256, 128))

    def body(y_ref, x_ref, z_ref, dl_ref, ng_ref, o_ref):
        zv = z_ref[...]
        y2 = (y_ref[...] + dl_ref[...] * x_ref[...]) * (zv * _sigmoid(zv))
        r = lax.rsqrt(jnp.mean(y2 * y2, axis=1, keepdims=True) + LN_EPS)
        o_ref[...] = (y2 * r * ng_ref[...]).astype(BF16)

    blk = pl.BlockSpec((tm, gw), lambda g, i: (i, g))
    vec = pl.BlockSpec((1, gw), lambda g, i: (0, g))
    return pl.pallas_call(
        body, name=name, grid=(G, T // tm), in_specs=[blk, blk, blk, vec, vec], out_specs=blk,
        out_shape=jax.ShapeDtypeStruct((T, d_inner), BF16), compiler_params=_cparams(),
    )(y, xbc, z, d_lane, ng)


def _gate_bwd(dyn, y, xbc, z, d_lane, ng, d_inner, name, exchange=()):
    T = y.shape[0]
    G = SSD_N_GROUPS
    gw = d_inner // G
    tm = _tile(T, (1024, 512, 256, 128))
    nx = len(exchange)

    def body(*refs):
        dyn_ref, y_ref, x_ref, z_ref, dl_ref, ng_ref = refs[:6]
        dy_ref, dz_ref, dng_ref, ddl_ref = refs[6 + nx:10 + nx]
        if nx:
            start, finish = _sibling_plan(refs[6:6 + nx], refs[10 + nx:10 + 2 * nx], *refs[10 + 2 * nx:])
            t, total = _grid_step((G, T // tm))
            pl.when(t == 0)(start)

        @pl.when(pl.program_id(1) == 0)
        def _():
            dng_ref[...] = jnp.zeros_like(dng_ref)
            ddl_ref[...] = jnp.zeros_like(ddl_ref)

        zv = z_ref[...]
        xv = x_ref[...]
        sz = _sigmoid(zv)
        y1 = y_ref[...] + dl_ref[...] * xv
        y2 = y1 * (zv * sz)
        r = lax.rsqrt(jnp.mean(y2 * y2, axis=1, keepdims=True) + LN_EPS)
        dn = dyn_ref[...].astype(F32)
        dng_ref[...] += jnp.sum(dn * y2 * r, axis=0, keepdims=True)
        do = dn * ng_ref[...]
        dy2 = r * do - y2 * (r * r * r) * jnp.mean(do * y2, axis=1, keepdims=True)
        dz_ref[...] = (dy2 * y1 * sz * (1.0 + zv * (1.0 - sz))).astype(BF16)
        dy1 = dy2 * (zv * sz)
        dy_ref[...] = dy1
        ddl_ref[...] += jnp.sum(dy1 * xv, axis=0, keepdims=True)
        if nx:
            pl.when(t == total - 1)(finish)

    blk = pl.BlockSpec((tm, gw), lambda g, i: (i, g))
    vec = pl.BlockSpec((1, gw), lambda g, i: (0, g))
    vsh = jax.ShapeDtypeStruct((1, d_inner), F32)
    outs = pl.pallas_call(
        body, name=name, grid=(G, T // tm), in_specs=[blk, blk, blk, blk, vec, vec] + [ANY] * nx,
        out_specs=[blk, blk, vec, vec] + [ANY] * nx,
        out_shape=[jax.ShapeDtypeStruct((T, d_inner), F32), jax.ShapeDtypeStruct((T, d_inner), BF16), vsh, vsh]
        + _sibling_shapes(exchange),
        scratch_shapes=_sibling_sems(nx) if nx else [], compiler_params=_cparams(),
    )(dyn, y, xbc, z, d_lane, ng, *exchange)
    return outs[:4], list(outs[4:])


def _rms_fwd(x, g, name):
    T, R = x.shape
    tm = _tile(T, (512, 256, 128))

    def body(x_ref, g_ref, y_ref):
        xv = x_ref[...]
        y = xv * lax.rsqrt(jnp.mean(xv * xv, axis=1, keepdims=True) + RMS_EPS) * g_ref[...]
        y_ref[...] = y.astype(BF16)

    row = pl.BlockSpec((tm, R), lambda i: (i, 0))
    vec = pl.BlockSpec((1, R), lambda i: (0, 0))
    return pl.pallas_call(
        body, name=name, grid=(T // tm,), in_specs=[row, vec], out_specs=row,
        out_shape=jax.ShapeDtypeStruct((T, R), BF16), compiler_params=_cparams(),
    )(x, g)


def _rms_bwd(dy, x, g, name):
    T, R = x.shape
    tm = _tile(T, (512, 256, 128))

    def body(dy_ref, x_ref, g_ref, dx_ref, dg_ref):
        @pl.when(pl.program_id(0) == 0)
        def _():
            dg_ref[...] = jnp.zeros_like(dg_ref)

        xv = x_ref[...]
        dyv = dy_ref[...]
        r = lax.rsqrt(jnp.mean(xv * xv, axis=1, keepdims=True) + RMS_EPS)
        dg_ref[...] += jnp.sum(dyv * xv * r, axis=0, keepdims=True)
        do = dyv * g_ref[...]
        dx = r * do - xv * (r * r * r) * jnp.mean(do * xv, axis=1, keepdims=True)
        dx_ref[...] = dx.astype(BF16)

    row = pl.BlockSpec((tm, R), lambda i: (i, 0))
    vec = pl.BlockSpec((1, R), lambda i: (0, 0))
    return pl.pallas_call(
        body, name=name, grid=(T // tm,), in_specs=[row, row, vec], out_specs=[row, vec],
        out_shape=[jax.ShapeDtypeStruct((T, R), BF16), jax.ShapeDtypeStruct((1, R), F32)],
        compiler_params=_cparams(),
    )(dy, x, g)


def _swap_halves(x):
    half = MLA_ROPE // 2
    return jnp.where(_iota(x.shape, 1) < half, pltpu.roll(x, LANES - half, 1), pltpu.roll(x, half, 1))


def _rope(x, cc, ss, *, transpose, sum_heads=False, name):
    T, W = x.shape
    nh = W // LANES
    tm = _tile(T, (512, 256, 128))
    n_out = 1 if sum_heads else nh

    def body(x_ref, cc_ref, ss_ref, o_ref):
        ccv = cc_ref[...]
        ssv = ss_ref[...]
        if sum_heads:
            xv = x_ref[:, 0:LANES]
            for hh in range(1, nh):
                xv = xv + x_ref[:, hh * LANES:(hh + 1) * LANES]
            heads = [xv]
        else:
            heads = [x_ref[:, hh * LANES:(hh + 1) * LANES] for hh in range(nh)]
        for hh, xv in enumerate(heads):
            if transpose:
                r = xv * ccv + _swap_halves(xv * ssv)
            else:
                r = xv * ccv + _swap_halves(xv) * ssv
            r = jnp.where(_iota(r.shape, 1) < MLA_ROPE, r, 0.0)
            o_ref[:, hh * LANES:(hh + 1) * LANES] = r.astype(BF16)

    return pl.pallas_call(
        body, name=name, grid=(T // tm,),
        in_specs=[pl.BlockSpec((tm, W), lambda i: (i, 0)), pl.BlockSpec((tm, LANES), lambda i: (i, 0)),
                  pl.BlockSpec((tm, LANES), lambda i: (i, 0))],
        out_specs=pl.BlockSpec((tm, n_out * LANES), lambda i: (i, 0)),
        out_shape=jax.ShapeDtypeStruct((T, n_out * LANES), BF16), compiler_params=_cparams(),
    )(x, cc, ss)


ATT_BLOCK = 512
ATT_SUB = 256
LOG2E = 1.4426950408889634


def _att_geometry(S):
    tb = _tile(S, (ATT_BLOCK, 256))
    return tb, S // tb, tb // ATT_SUB


def _heads_per_step(nh):
    return 4 if nh % 4 == 0 else 2 if nh % 2 == 0 else 1


def _att_scores(a, b, diag, kv_major, off):
    s = _bdot(a, b, NT)
    if diag:
        q_ax, k_ax = (1, 0) if kv_major else (0, 1)
        s = jnp.where(_iota(s.shape, k_ax) <= _iota(s.shape, q_ax) + off, s, NEG_BIG)
    return s


def _attention_fwd(qn, qr, kn, kr, v, S, scale, name):
    T, W = qn.shape
    nh = W // LANES
    Bn = T // S
    tb, n, C = _att_geometry(S)
    pairs = [(i, j) for i in range(n) for j in range(i + 1)]
    it = jnp.asarray([p[0] for p in pairs], jnp.int32)
    jt = jnp.asarray([p[1] for p in pairs], jnp.int32)
    c2 = scale * LOG2E

    HP = _heads_per_step(nh)

    def body(it_ref, jt_ref, qn_ref, qr_ref, kn_ref, kr_ref, v_ref, o_ref, ob_ref, lse_ref, m_sc, l_sc, acc):
        p = pl.program_id(2)
        i = it_ref[p]
        j = jt_ref[p]

        @pl.when(j == 0)
        def _():
            m_sc[...] = jnp.full_like(m_sc, NEG_BIG)
            l_sc[...] = jnp.zeros_like(l_sc)
            acc[...] = jnp.zeros_like(acc)

        def step(diag):
            for hh in range(HP):
                hs = pl.ds(hh * LANES, LANES)
                qc = jnp.concatenate([qn_ref[:, hs], qr_ref[:, hs]], axis=1)
                kc = jnp.concatenate([kn_ref[:, hs], kr_ref[...]], axis=1)
                st = _att_scores(kc, qc, diag, True, 0)
                m_old = m_sc[hh]
                m_new = jnp.maximum(m_old, jnp.max(st, axis=0, keepdims=True))
                pt = jnp.exp2((st - m_new) * c2)
                corr = jnp.exp2((m_old - m_new) * c2)
                l_sc[hh] = corr * l_sc[hh] + jnp.sum(pt, axis=0, keepdims=True)
                acc[hh] = corr * acc[hh] + _bdot(v_ref[:, hs].T, pt, NN)
                m_sc[hh] = m_new

        @pl.when(j < i)
        def _():
            step(False)

        @pl.when(j == i)
        def _():
            step(True)
            for hh in range(HP):
                hs = pl.ds(hh * LANES, LANES)
                ov = (acc[hh] / l_sc[hh]).T
                o_ref[:, hs] = ov
                ob_ref[:, hs] = ov.astype(BF16)
                lse_row = m_sc[hh] * scale + jnp.log(l_sc[hh])
                lse_ref[:, hs] = jnp.broadcast_to(lse_row, (LANES, tb)).T

    qspec = pl.BlockSpec((tb, HP * LANES), lambda b, h, p, it_, jt_: (b * n + it_[p], h))
    kspec = pl.BlockSpec((tb, HP * LANES), lambda b, h, p, it_, jt_: (b * n + jt_[p], h))
    krspec = pl.BlockSpec((tb, LANES), lambda b, h, p, it_, jt_: (b * n + jt_[p], 0))
    return pl.pallas_call(
        body, name=name,
        grid_spec=pltpu.PrefetchScalarGridSpec(
            num_scalar_prefetch=2, grid=(Bn, nh // HP, len(pairs)),
            in_specs=[qspec, qspec, kspec, krspec, kspec], out_specs=[qspec, qspec, qspec],
            scratch_shapes=[pltpu.VMEM((HP, 1, tb), F32), pltpu.VMEM((HP, 1, tb), F32),
                            pltpu.VMEM((HP, LANES, tb), F32)]),
        out_shape=[jax.ShapeDtypeStruct((T, W), F32), jax.ShapeDtypeStruct((T, W), BF16),
                   jax.ShapeDtypeStruct((T, W), F32)],
        compiler_params=_cparams(),
    )(it, jt, qn, qr, kn, kr, v)


def _attention_dq(qn, qr, kn, kr, v, o, do, lse, S, scale, name):
    T, W = qn.shape
    nh = W // LANES
    Bn = T // S
    tb, n, C = _att_geometry(S)
    pairs = [(i, j) for i in range(n) for j in range(i + 1)]
    it = jnp.asarray([p[0] for p in pairs], jnp.int32)
    jt = jnp.asarray([p[1] for p in pairs], jnp.int32)
    c2 = scale * LOG2E
    HP = _heads_per_step(nh)

    def body(it_ref, jt_ref, qn_ref, qr_ref, kn_ref, kr_ref, v_ref, o_ref, do_ref, lse_ref,
             dqn_ref, dqr_ref, acc, di_sc, lse_sc):
        p = pl.program_id(2)
        i = it_ref[p]
        j = jt_ref[p]

        @pl.when(j == 0)
        def _():
            acc[...] = jnp.zeros_like(acc)
            for hh in range(HP):
                hs = pl.ds(hh * LANES, LANES)
                di_sc[hh] = jnp.sum(do_ref[:, hs] * o_ref[:, hs], axis=1, keepdims=True)
                lse_sc[hh] = jnp.max(lse_ref[:, hs], axis=1, keepdims=True) * LOG2E

        def step(diag):
            for hh in range(HP):
                hs = pl.ds(hh * LANES, LANES)
                qc = jnp.concatenate([qn_ref[:, hs], qr_ref[:, hs]], axis=1)
                for c in range(C):
                    r0 = c * ATT_SUB if diag else 0
                    rows = pl.ds(r0, tb - r0)
                    ks = pl.ds(c * ATT_SUB, ATT_SUB)
                    kc = jnp.concatenate([kn_ref[ks, hs], kr_ref[ks, :]], axis=1)
                    s = _att_scores(qc[r0:], kc, diag, False, 0)
                    pr = jnp.exp2(s * c2 - lse_sc[hh, rows, :])
                    dp = _bdot(do_ref[rows, hs], v_ref[ks, hs], NT)
                    ds = pr * (dp - di_sc[hh, rows, :]) * scale
                    acc[hh, rows, :] += _bdot(ds, kc, NN)

        @pl.when(j < i)
        def _():
            step(False)

        @pl.when(j == i)
        def _():
            step(True)
            for hh in range(HP):
                hs = pl.ds(hh * LANES, LANES)
                dqn_ref[:, hs] = acc[hh, :, 0:LANES].astype(BF16)
                dqr_ref[:, hs] = acc[hh, :, LANES:2 * LANES]

    qspec = pl.BlockSpec((tb, HP * LANES), lambda b, h, p, it_, jt_: (b * n + it_[p], h))
    kspec = pl.BlockSpec((tb, HP * LANES), lambda b, h, p, it_, jt_: (b * n + jt_[p], h))
    krspec = pl.BlockSpec((tb, LANES), lambda b, h, p, it_, jt_: (b * n + jt_[p], 0))
    return pl.pallas_call(
        body, name=name,
        grid_spec=pltpu.PrefetchScalarGridSpec(
            num_scalar_prefetch=2, grid=(Bn, nh // HP, len(pairs)),
            in_specs=[qspec, qspec, kspec, krspec, kspec, qspec, qspec, qspec], out_specs=[qspec, qspec],
            scratch_shapes=[pltpu.VMEM((HP, tb, 2 * LANES), F32), pltpu.VMEM((HP, tb, 1), F32),
                            pltpu.VMEM((HP, tb, 1), F32)]),
        out_shape=[jax.ShapeDtypeStruct((T, W), BF16), jax.ShapeDtypeStruct((T, W), F32)],
        compiler_params=_cparams(),
    )(it, jt, qn, qr, kn, kr, v, o, do, lse)


def _attention_dkv(qn, qr, kn, kr, v, o, do, lse, S, scale, name):
    T, W = qn.shape
    nh = W // LANES
    Bn = T // S
    tb, n, C = _att_geometry(S)
    HP = _heads_per_step(nh)
    triples = [(j, h, i) for j in range(n) for h in range(nh // HP) for i in range(j, n)]
    jt = jnp.asarray([t[0] for t in triples], jnp.int32)
    ht = jnp.asarray([t[1] for t in triples], jnp.int32)
    it = jnp.asarray([t[2] for t in triples], jnp.int32)
    c2 = scale * LOG2E

    def as_row(col):
        return jnp.broadcast_to(col, (ATT_SUB, LANES)).T[0:1, :]

    def body(jt_ref, ht_ref, it_ref, qn_ref, qr_ref, kn_ref, kr_ref, v_ref, o_ref, do_ref, lse_ref,
             dkn_ref, dv_ref, dkr_ref, akc, av):
        p = pl.program_id(1)
        j = jt_ref[p]
        h = ht_ref[p]
        i = it_ref[p]

        @pl.when(jnp.logical_and(h == 0, i == j))
        def _():
            dkr_ref[...] = jnp.zeros_like(dkr_ref)

        @pl.when(i == j)
        def _():
            akc[...] = jnp.zeros_like(akc)
            av[...] = jnp.zeros_like(av)

        def step(diag):
            for hh in range(HP):
                hs = pl.ds(hh * LANES, LANES)
                kc = jnp.concatenate([kn_ref[:, hs], kr_ref[...]], axis=1)
                for c in range(C):
                    rows = pl.ds(c * ATT_SUB, ATT_SUB)
                    k1 = (c + 1) * ATT_SUB if diag else tb
                    ks = pl.ds(0, k1)
                    qc = jnp.concatenate([qn_ref[rows, hs], qr_ref[rows, hs]], axis=1)
                    st = _att_scores(kc[:k1], qc, diag, True, c * ATT_SUB)
                    dov = do_ref[rows, hs]
                    lse_row = as_row(jnp.max(lse_ref[rows, hs], axis=1, keepdims=True)) * LOG2E
                    di_row = as_row(jnp.sum(dov * o_ref[rows, hs], axis=1, keepdims=True))
                    pt = jnp.exp2(st * c2 - lse_row)
                    dst = pt * (_bdot(v_ref[ks, hs], dov, NT) - di_row) * scale
                    av[ks, hs] += _bdot(pt, dov, NN)
                    akc[hh, ks, :] += _bdot(dst, qc, NN)

        @pl.when(i > j)
        def _():
            step(False)

        @pl.when(i == j)
        def _():
            step(True)

        @pl.when(i == n - 1)
        def _():
            for hh in range(HP):
                dkn_ref[:, pl.ds(hh * LANES, LANES)] = akc[hh, :, 0:LANES].astype(BF16)
                dkr_ref[...] += akc[hh, :, LANES:2 * LANES]
            dv_ref[...] = av[...].astype(BF16)

    qspec = pl.BlockSpec((tb, HP * LANES), lambda b, p, jt_, ht_, it_: (b * n + it_[p], ht_[p]))
    kspec = pl.BlockSpec((tb, HP * LANES), lambda b, p, jt_, ht_, it_: (b * n + jt_[p], ht_[p]))
    krspec = pl.BlockSpec((tb, LANES), lambda b, p, jt_, ht_, it_: (b * n + jt_[p], 0))
    return pl.pallas_call(
        body, name=name,
        grid_spec=pltpu.PrefetchScalarGridSpec(
            num_scalar_prefetch=3, grid=(Bn, len(triples)),
            in_specs=[qspec, qspec, kspec, krspec, kspec, qspec, qspec, qspec],
            out_specs=[kspec, kspec, krspec],
            scratch_shapes=[pltpu.VMEM((HP, tb, 2 * LANES), F32), pltpu.VMEM((tb, HP * LANES), F32)]),
        out_shape=[jax.ShapeDtypeStruct((T, W), BF16), jax.ShapeDtypeStruct((T, W), BF16),
                   jax.ShapeDtypeStruct((T, LANES), F32)],
        compiler_params=_cparams(),
    )(jt, ht, it, qn, qr, kn, kr, v, o, do, lse)


def _pad_cols(w, n):
    return jnp.pad(w, ((0, 0), (0, n - w.shape[1])))


def _local_step(x, positions, w, target, late=None, reduce_early=None, reduce_last=None):
    w = dict(w)
    Bn, S, D = x.shape
    T = Bn * S
    depth = w['ln_mix_g'].shape[0]
    alpha = (2 * depth) ** 0.25
    d_inner = w['ssd_norm_g'].shape[1]
    n_heads_ssd = d_inner // SSD_HEAD_DIM
    G = SSD_N_GROUPS
    gn = G * SSD_D_STATE
    conv_dim = d_inner + 2 * gn
    hpg = n_heads_ssd // G
    nh = D // MLA_NOPE
    kv_rank = w['kv_norm_g'].shape[0]
    q_rank = w['q_norm_g'].shape[1]
    H = w['ffn_conv_b'].shape[1] // 2
    scale = (MLA_NOPE + MLA_ROPE) ** -0.5
    assert S % SSD_CHUNK == 0 and hpg % 2 == 0 and SSD_D_STATE == LANES and n_heads_ssd <= LANES

    X = x.reshape(T, D)
    tgt = target.reshape(T, D)

    inv_freq = 1.0 / (ROPE_THETA ** (jnp.arange(0, MLA_ROPE, 2, dtype=F32) / MLA_ROPE))
    ang = positions.reshape(T).astype(F32)[:, None] * inv_freq
    cos, sin = jnp.cos(ang), jnp.sin(ang)
    zpad = jnp.zeros((T, LANES - MLA_ROPE), F32)
    cc = jnp.concatenate([cos, cos, zpad], axis=1)
    ss = jnp.concatenate([-sin, sin, zpad], axis=1)

    w_in = w['ssd_in_proj'][0]
    w_z = w_in[:, :d_inner]
    w_xbc = w_in[:, d_inner:d_inner + conv_dim]
    w_dt = _pad_cols(w_in[:, d_inner + conv_dim:], LANES)
    ssd_cw = w['ssd_conv_w'][0]
    ssd_cb = w['ssd_conv_b']
    dt_bias = _pad_cols(w['ssd_dt_bias'], LANES)
    a_log = _pad_cols(w['ssd_A_log'], LANES)
    d_lane = jnp.repeat(w['ssd_D'], SSD_HEAD_DIM, axis=1)
    ssd_ng = w['ssd_norm_g']

    Xb = X.astype(BF16)
    z = _mm(Xb, w_z, name='ssd_in_z')
    xbc_pre = _mm(Xb, w_xbc, name='ssd_in_xbc')
    dt_pre = _mm(Xb, w_dt, name='ssd_in_dt')
    xbc, xbc_conv = _ssd_conv_fwd(xbc_pre, ssd_cw, ssd_cb, S, name='ssd_conv')
    dtb, cumb = _ssd_dt_prep(dt_pre, dt_bias, a_log, n_heads_ssd, name='ssd_dt_prep')
    y_scan, states, gathered = _scan_fwd(xbc, dtb, cumb, S, d_inner, name='ssd_scan',
                                         gather=late[0] if late else ())
    if late:
        w.update(late[1](gathered))

    w_out = w['ssd_out_proj'][0]
    w_kvc = w['kv_down_proj'][:, :kv_rank]
    w_kvr = _pad_cols(w['kv_down_proj'][:, kv_rank:], LANES)
    kv_g = w['kv_norm_g'].reshape(1, kv_rank)
    w_uk = w['kv_up_k']
    w_uv = w['kv_up_v']
    w_qd = w['q_down_proj'][0]
    q_g = w['q_norm_g']
    qu = w['q_up_proj'][0].reshape(q_rank, nh, MLA_NOPE + MLA_ROPE)
    w_qn = qu[:, :, :MLA_NOPE].reshape(q_rank, nh * LANES)
    w_qr = jnp.pad(qu[:, :, MLA_NOPE:], ((0, 0), (0, 0), (0, LANES - MLA_ROPE))).reshape(q_rank, nh * LANES)
    w_ao = w['attn_out_proj'][0]

    def ffn_parts(i):
        return dict(wu=w['ffn_up'][i], cw=w['ffn_conv_w'][i], cb=w['ffn_conv_b'][i:i + 1], wd=w['ffn_down'][i])

    def ln(name, i):
        return w[name][i:i + 1]

    def ffn_fwd(hb, i):
        f = ffn_parts(i)
        u = _mm(hb, f['wu'], name=f'ffn{i}_up')
        a = _ffn_act_fwd(u, f['cw'], f['cb'], S, name=f'ffn{i}_act')
        return _mm(a, f['wd'], name=f'ffn{i}_down'), (u, a)

    yn = _gate_fwd(y_scan, xbc, z, d_lane, ssd_ng, d_inner, name='ssd_gate')
    mix0 = _mm(yn, w_out, name='ssd_out')
    h1, s1, h1b = _ln_fwd(X, mix0, ln('ln_mix_g', 0), ln('ln_mix_b', 0), alpha, name='ln_mix0')
    ff0, ffn0_saved = ffn_fwd(h1b, 0)
    h2, s2, h2b = _ln_fwd(h1, ff0, ln('ln_ffn_g', 0), ln('ln_ffn_b', 0), alpha, name='ln_ffn0')

    ckv = _mm(h2b, w_kvc, name='kv_down_c')
    kr_pre = _mm(h2b, w_kvr, name='kv_down_r')
    ckvn = _rms_fwd(ckv, kv_g, name='kv_norm')
    kr = _rope(kr_pre, cc, ss, transpose=False, name='k_rope')
    kn = _mm(ckvn, w_uk, out_dtype=BF16, name='kv_up_k')
    v = _mm(ckvn, w_uv, out_dtype=BF16, name='kv_up_v')
    cq_pre = _mm(h2b, w_qd, name='q_down')
    cq = _rms_fwd(cq_pre, q_g, name='q_norm')
    qn = _mm(cq, w_qn, out_dtype=BF16, name='q_up_n')
    qr_pre = _mm(cq, w_qr, name='q_up_r')
    qr = _rope(qr_pre, cc, ss, transpose=False, name='q_rope')
    o, ob, lse = _attention_fwd(qn, qr, kn, kr, v, S, scale, name='attn_fwd')
    mix1 = _mm(ob, w_ao, name='attn_out')
    h3, s3, h3b = _ln_fwd(h2, mix1, ln('ln_mix_g', 1), ln('ln_mix_b', 1), alpha, name='ln_mix1')
    ff1, ffn1_saved = ffn_fwd(h3b, 1)
    h4, s4, _ = _ln_fwd(h3, ff1, ln('ln_ffn_g', 1), ln('ln_ffn_b', 1), alpha, name='ln_ffn1')
    sq, dh4 = _loss_head(h4, tgt, name='loss_head')

    g = {}

    def ffn_bwd(ds, dsb, hb_in, saved, i):
        f = ffn_parts(i)
        u, a = saved
        d_wd = _mm(a, dsb, ta=True, name=f'ffn{i}_down_dw')
        da = _mm(dsb, f['wd'], tb=True, out_dtype=BF16, name=f'ffn{i}_down_dx')
        dug, duv, dwg, dwv, dbg, dbv = _ffn_act_bwd(u, da, f['cw'], f['cb'], S, name=f'ffn{i}_act_bwd')
        d_wu = _mm(hb_in, dug, ta=True, name=f'ffn{i}_up_g_dw', into=lax.empty((D, 2 * H), F32))
        d_wu = _mm(hb_in, duv, ta=True, name=f'ffn{i}_up_v_dw', into=d_wu, into_col=H)
        dh = _mm(dug, f['wu'], tb=True, add=ds, add_scale=alpha, name=f'ffn{i}_up_g_dx')
        dh = _mm(duv, f['wu'], tb=True, add=dh, name=f'ffn{i}_up_v_dx', b_koff=H)
        return dh, d_wd, d_wu, jnp.concatenate([dwg, dwv], axis=1), jnp.concatenate([dbg, dbv], axis=1)

    ds4, ds4b, dg_f1, db_f1 = _ln_bwd(dh4, s4, ln('ln_ffn_g', 1), name='ln_ffn1_bwd')
    dh3, d_wd1, d_wu1, d_fcw1, d_fcb1 = ffn_bwd(ds4, ds4b, h3b, ffn1_saved, 1)
    ds3, ds3b, dg_m1, db_m1 = _ln_bwd(dh3, s3, ln('ln_mix_g', 1), name='ln_mix1_bwd')

    g['attn_out_proj'] = _mm(ob, ds3b, ta=True, name='attn_out_dw')[None]
    do = _mm(ds3b, w_ao, tb=True, name='attn_out_dx')
    dqn, dqr = _attention_dq(qn, qr, kn, kr, v, o, do, lse, S, scale, name='attn_bwd_dq')
    dkn, dv, dkr = _attention_dkv(qn, qr, kn, kr, v, o, do, lse, S, scale, name='attn_bwd_dkv')
    dqr_pre = _rope(dqr, cc, ss, transpose=True, name='q_rope_bwd')
    dkr_pre = _rope(dkr, cc, ss, transpose=True, name='k_rope_bwd')

    d_wqn = _mm(cq, dqn, ta=True, name='q_up_n_dw').reshape(q_rank, nh, LANES)
    d_wqr = _mm(cq, dqr_pre, ta=True, name='q_up_r_dw').reshape(q_rank, nh, LANES)[:, :, :MLA_ROPE]
    g['q_up_proj'] = jnp.concatenate([d_wqn, d_wqr], axis=2).reshape(1, q_rank, nh * (MLA_NOPE + MLA_ROPE))
    dcq = _mm(dqn, w_qn, tb=True, name='q_up_n_dx')
    dcq = _mm(dqr_pre, w_qr, tb=True, add=dcq, name='q_up_r_dx')
    dcq_pre, d_qg = _rms_bwd(dcq, cq_pre, q_g, name='q_norm_bwd')
    g['q_norm_g'] = d_qg
    g['q_down_proj'] = _mm(h2b, dcq_pre, ta=True, name='q_down_dw')[None]

    g['kv_up_k'] = _mm(ckvn, dkn, ta=True, name='kv_up_k_dw')
    g['kv_up_v'] = _mm(ckvn, dv, ta=True, name='kv_up_v_dw')
    dckvn = _mm(dkn, w_uk, tb=True, name='kv_up_k_dx')
    dckvn = _mm(dv, w_uv, tb=True, add=dckvn, name='kv_up_v_dx')
    dckv, d_kvg = _rms_bwd(dckvn, ckv, kv_g, name='kv_norm_bwd')
    g['kv_norm_g'] = d_kvg.reshape(kv_rank)
    g['kv_down_proj'] = jnp.concatenate(
        [_mm(h2b, dckv, ta=True, name='kv_down_c_dw'),
         _mm(h2b, dkr_pre, ta=True, name='kv_down_r_dw')[:, :MLA_ROPE]], axis=1)

    dh2 = _mm(dcq_pre, w_qd, tb=True, add=ds3, add_scale=alpha, name='q_down_dx')
    dh2 = _mm(dckv, w_kvc, tb=True, add=dh2, name='kv_down_c_dx')
    dh2 = _mm(dkr_pre, w_kvr, tb=True, add=dh2, name='kv_down_r_dx')

    ds2, ds2b, dg_f0, db_f0 = _ln_bwd(dh2, s2, ln('ln_ffn_g', 0), name='ln_ffn0_bwd')
    dh1, d_wd0, d_wu0, d_fcw0, d_fcb0 = ffn_bwd(ds2, ds2b, h1b, ffn0_saved, 0)
    ds1, ds1b, dg_m0, db_m0 = _ln_bwd(dh1, s1, ln('ln_mix_g', 0), name='ln_mix0_bwd')

    g['ssd_out_proj'] = _mm(yn, ds1b, ta=True, name='ssd_out_dw')[None]
    dyn = _mm(ds1b, w_out, tb=True, out_dtype=BF16, name='ssd_out_dx')
    g['ffn_up'] = jnp.stack([d_wu0, d_wu1])
    g['ffn_down'] = jnp.stack([d_wd0, d_wd1])
    early_blocks = reduce_early[0](g) if reduce_early else ()
    (dy1, dz, d_ng, d_dl), from_sibling = _gate_bwd(dyn, y_scan, xbc, z, d_lane, ssd_ng, d_inner,
                                                    name='ssd_gate_bwd', exchange=early_blocks)
    (dxs, dB, dC, ddt_g, da_g), early = _scan_bwd(
        xbc, dtb, cumb, states, dy1, d_lane, S, d_inner, name='ssd_scan_bwd',
        exchange=reduce_early[1](early_blocks, from_sibling) if reduce_early else ())
    ddt_pre, d_bias, d_alog = _dt_bwd(ddt_g, da_g, dt_pre, dt_bias, a_log, name='ssd_dt_bwd')
    dxbc_pre, d_scw, d_scb = _ssd_conv_bwd(xbc_pre, xbc_conv, [dxs, dB, dC], ssd_cw, S, name='ssd_conv_bwd')
    g['ssd_in_proj'] = jnp.concatenate(
        [_mm(Xb, dz, ta=True, name='ssd_in_z_dw'), _mm(Xb, dxbc_pre, ta=True, name='ssd_in_xbc_dw'),
         _mm(Xb, ddt_pre, ta=True, name='ssd_in_dt_dw')[:, :n_heads_ssd]], axis=1)[None]
    g['ssd_conv_w'] = d_scw[None]
    g['ssd_conv_b'] = d_scb
    g['ssd_norm_g'] = d_ng
    g['ffn_conv_w'] = jnp.stack([d_fcw0, d_fcw1])
    dx = _mm(dz, w_z, tb=True, add=ds1, add_scale=alpha, name='ssd_in_z_dx')
    dx = _mm(dxbc_pre, w_xbc, tb=True, add=dx, name='ssd_in_xbc_dx', exchange=reduce_last(g) if reduce_last else ())
    dx, last = dx if reduce_last else (dx, None)
    dx = _mm(ddt_pre, w_dt, tb=True, add=dx, name='ssd_in_dt_dx')

    g['ssd_dt_bias'] = d_bias[:, :n_heads_ssd]
    g['ssd_A_log'] = d_alog[:, :n_heads_ssd]
    g['ssd_D'] = jnp.sum(d_dl.reshape(1, n_heads_ssd, SSD_HEAD_DIM), axis=2)
    g['ffn_conv_b'] = jnp.concatenate([d_fcb0, d_fcb1], axis=0)
    g['ln_mix_g'] = jnp.concatenate([dg_m0, dg_m1], axis=0)
    g['ln_mix_b'] = jnp.concatenate([db_m0, db_m1], axis=0)
    g['ln_ffn_g'] = jnp.concatenate([dg_f0, dg_f1], axis=0)
    g['ln_ffn_b'] = jnp.concatenate([db_f0, db_f1], axis=0)
    return sq, dx.reshape(Bn, S, D), g, early, last


ANY = pl.BlockSpec(memory_space=pl.ANY)


def _all_gather(xs, name):
    n = len(xs)

    def body(*refs):
        start, forward, finish = _gather_plan(refs[:n], refs[n:2 * n], *refs[2 * n:])
        start()
        forward()
        finish()

    return pl.pallas_call(
        body, name=name, out_shape=_gather_shapes(xs), in_specs=[ANY] * n, out_specs=[ANY] * n,
        scratch_shapes=_gather_sems(n),
    )(*xs)


def _gather_shapes(xs):
    return [jax.ShapeDtypeStruct((N_DEV,) + a.shape, a.dtype) for a in xs]


def _gather_sems(n):
    return [pltpu.SemaphoreType.DMA((7 * n,)), pltpu.SemaphoreType.DMA((7 * n,)), pltpu.SemaphoreType.DMA((n,))]


def _gather_plan(x_refs, out_refs, send_sems, recv_sems, local_sems):
    n = len(x_refs)
    x, y, c = lax.axis_index("x"), lax.axis_index("y"), lax.axis_index("c")
    me, sibling = (x, y, c), (x, y, 1 - c)
    chips = [(1 - x, y), (x, 1 - y), (1 - x, 1 - y)]

    def rows(i, px, py, pc):
        return out_refs[i].at[4 * px + 2 * py + pc]

    def copy(i, k, block, to, own=False):
        return pltpu.make_async_remote_copy(
            src_ref=x_refs[i] if own else rows(i, *block), dst_ref=rows(i, *block),
            send_sem=send_sems.at[7 * i + k], recv_sem=recv_sems.at[7 * i + k],
            device_id=to, device_id_type=MESH)

    def mine():
        return [pltpu.make_async_copy(x_refs[i], rows(i, *me), local_sems.at[i]) for i in range(n)]

    def first():
        out = []
        for i in range(n):
            out.append(copy(i, 0, me, sibling, own=True))
            out += [copy(i, 1 + j, me, (*chip, c), own=True) for j, chip in enumerate(chips)]
        return out

    def passed():
        return [copy(i, 4 + j, (*chip, c), sibling) for j, chip in enumerate(chips) for i in range(n)]

    def start():
        for cp in mine() + first():
            cp.start()

    def forward():
        for j, chip in enumerate(chips):
            for i in range(n):
                copy(i, 1 + j, (*chip, c), me).wait_recv()
                copy(i, 4 + j, (*chip, c), sibling).start()

    def finish():
        for i in range(n):
            copy(i, 0, sibling, me).wait_recv()
            for j, chip in enumerate(chips):
                copy(i, 4 + j, (*chip, 1 - c), me).wait_recv()
        for cp in first() + passed():
            cp.wait_send()
        for cp in mine():
            cp.wait()

    return start, forward, finish


def _exchange_sibling(gs, name):
    n = len(gs)

    def body(*refs):
        start, finish = _sibling_plan(refs[:n], refs[n:2 * n], *refs[2 * n:])
        start()
        finish()

    return pl.pallas_call(
        body, name=name, out_shape=_sibling_shapes(gs), in_specs=[ANY] * n, out_specs=[ANY] * n,
        scratch_shapes=_sibling_sems(n),
    )(*gs)


def _sibling_shapes(gs):
    return [jax.ShapeDtypeStruct((4,) + g.shape[1:], g.dtype) for g in gs]


def _sibling_sems(n):
    return [pltpu.SemaphoreType.DMA((4 * n,)), pltpu.SemaphoreType.DMA((4 * n,))]


def _sibling_plan(g_refs, r_refs, send_sems, recv_sems):
    n = len(g_refs)
    x, y, c = lax.axis_index("x"), lax.axis_index("y"), lax.axis_index("c")

    def copies():
        return [pltpu.make_async_remote_copy(
            src_ref=g_refs[i].at[2 * k + (1 - c)], dst_ref=r_refs[i].at[k], send_sem=send_sems.at[4 * i + k],
            recv_sem=recv_sems.at[4 * i + k], device_id=(x, y, 1 - c), device_id_type=MESH)
            for i in range(n) for k in range(4)]

    def start():
        for cp in copies():
            cp.start()

    def finish():
        for cp in copies():
            cp.wait()

    return start, finish


def _chips_shapes(parts):
    return [jax.ShapeDtypeStruct((3,) + p.shape[1:], p.dtype) for p in parts]


def _chips_sems(n):
    return [pltpu.SemaphoreType.DMA((3 * n,)), pltpu.SemaphoreType.DMA((3 * n,))]


def _chips_plan(p_refs, r_refs, send_sems, recv_sems):
    n = len(p_refs)
    x, y, c = lax.axis_index("x"), lax.axis_index("y"), lax.axis_index("c")
    chips = [(1 - x, y), (x, 1 - y), (1 - x, 1 - y)]

    def copies():
        return [pltpu.make_async_remote_copy(
            src_ref=p_refs[i].at[2 * cx + cy], dst_ref=r_refs[i].at[j], send_sem=send_sems.at[3 * i + j],
            recv_sem=recv_sems.at[3 * i + j], device_id=(cx, cy, c), device_id_type=MESH)
            for i in range(n) for j, (cx, cy) in enumerate(chips)]

    def start():
        for cp in copies():
            cp.start()

    def finish():
        for cp in copies():
            cp.wait()

    return start, finish


def _row_tile(rows, cols):
    want = max(8, (256 * 1024) // cols)
    for t in range(min(rows, want) // 8 * 8, 7, -8):
        if rows % t == 0:
            return t
    return rows


def _add_sibling(gfull, r1, core, name):
    _, rows, lanes = gfull.shape
    tr = _row_tile(rows, lanes)

    def body(c_ref, g_ref, r_ref, o_ref):
        o_ref[...] = g_ref[...] + r_ref[...]

    return pl.pallas_call(
        body, name=name,
        grid_spec=pltpu.PrefetchScalarGridSpec(
            num_scalar_prefetch=1, grid=(4, rows // tr),
            in_specs=[pl.BlockSpec((1, tr, lanes), lambda k, r, c_ref: (2 * k + c_ref[0], r, 0)),
                      pl.BlockSpec((1, tr, lanes), lambda k, r, c_ref: (k, r, 0))],
            out_specs=pl.BlockSpec((1, tr, lanes), lambda k, r, c_ref: (k, r, 0))),
        out_shape=jax.ShapeDtypeStruct((4, rows, lanes), F32), compiler_params=_cparams(),
    )(core, gfull, r1)


def _adam_math(wv, gv, mv, vv):
    m = ADAM_B1 * mv + (1.0 - ADAM_B1) * gv
    v = ADAM_B2 * vv + (1.0 - ADAM_B2) * (gv * gv)
    m_hat = m / (1.0 - ADAM_B1 ** ADAM_STEP)
    v_hat = v / (1.0 - ADAM_B2 ** ADAM_STEP)
    delta = -ADAM_LR * (m_hat / (jnp.sqrt(v_hat) + ADAM_EPS) + ADAM_WD * wv)
    return delta, m, v


def _adam_sharded(part, r2, chip, wts, m, v, name):
    rows, lanes = wts.shape
    tr = _row_tile(rows, lanes)

    def body(k_ref, p_ref, r_ref, w_ref, m_ref, v_ref, g_ref, d_ref, mo_ref, vo_ref):
        gv = p_ref[0] + r_ref[0] + r_ref[1] + r_ref[2]
        delta, mn, vn = _adam_math(w_ref[...], gv, m_ref[...], v_ref[...])
        g_ref[...] = gv
        d_ref[...] = delta
        mo_ref[...] = mn
        vo_ref[...] = vn

    flat = pl.BlockSpec((tr, lanes), lambda r, k_ref: (r, 0))
    sh = jax.ShapeDtypeStruct((rows, lanes), F32)
    return pl.pallas_call(
        body, name=name,
        grid_spec=pltpu.PrefetchScalarGridSpec(
            num_scalar_prefetch=1, grid=(rows // tr,),
            in_specs=[pl.BlockSpec((1, tr, lanes), lambda r, k_ref: (k_ref[0], r, 0)),
                      pl.BlockSpec((3, tr, lanes), lambda r, k_ref: (0, r, 0)), flat, flat, flat],
            out_specs=[flat, flat, flat, flat]),
        out_shape=[sh, sh, sh, sh], compiler_params=_cparams(),
    )(chip, part, r2, wts, m, v)


def _adam_replicated(gall, wts, m, v, name):
    rows, lanes = wts.shape

    def body(ga_ref, w_ref, m_ref, v_ref, g_ref, d_ref, mo_ref, vo_ref):
        gv = ga_ref[0]
        for k in range(1, N_DEV):
            gv = gv + ga_ref[k]
        delta, mn, vn = _adam_math(w_ref[...], gv, m_ref[...], v_ref[...])
        g_ref[...] = gv
        d_ref[...] = delta
        mo_ref[...] = mn
        vo_ref[...] = vn

    sh = jax.ShapeDtypeStruct((rows, lanes), F32)
    return pl.pallas_call(body, name=name, out_shape=[sh, sh, sh, sh], compiler_params=_cparams())(gall, wts, m, v)


def _pack(arrs, dtype, row_mult):
    flat = jnp.concatenate([a.reshape(-1).astype(dtype) for a in arrs])
    n = flat.shape[0]
    unit = LANES * row_mult
    total = -(-n // unit) * unit
    return jnp.pad(flat, (0, total - n)).reshape(total // LANES, LANES)


def _unpack(flat2d, shapes):
    flat = flat2d.reshape(-1)
    out, off = [], 0
    for shp in shapes:
        n = math.prod(shp)
        out.append(flat[off:off + n].reshape(shp))
        off += n
    return out


def _unpack_gathered(gathered, shapes, axes):
    flat = gathered.reshape(N_DEV, -1)
    out, off = [], 0
    for shp, ax in zip(shapes, axes):
        n = math.prod(shp)
        seg = flat[:, off:off + n].reshape((N_DEV,) + tuple(shp))
        out.append(jnp.concatenate([seg[i] for i in range(N_DEV)], axis=ax))
        off += n
    return out


def _pack_chunks(fulls, axes, row_mult):
    per_dev = []
    for full, ax in zip(fulls, axes):
        parts = jnp.stack(jnp.split(full, N_DEV, axis=ax))
        per_dev.append(parts.reshape(N_DEV, -1))
    flat = jnp.concatenate(per_dev, axis=1)
    n = flat.shape[1]
    unit = LANES * row_mult
    total = -(-n // unit) * unit
    return jnp.pad(flat, ((0, 0), (0, total - n))).reshape(N_DEV, total // LANES, LANES)


def kernel(x, positions, ssd_in_proj, ssd_conv_w, ssd_conv_b, ssd_dt_bias, ssd_A_log, ssd_D, ssd_norm_g, ssd_out_proj, kv_down_proj, kv_norm_g, kv_up_k, kv_up_v, q_down_proj, q_norm_g, q_up_proj, attn_out_proj, ffn_up, ffn_conv_w, ffn_conv_b, ffn_down, ln_mix_g, ln_mix_b, ln_ffn_g, ln_ffn_b, loss_target, m_ssd_in_proj, m_ssd_conv_w, m_ssd_conv_b, m_ssd_dt_bias, m_ssd_A_log, m_ssd_D, m_ssd_norm_g, m_ssd_out_proj, m_kv_down_proj, m_kv_norm_g, m_kv_up_k, m_kv_up_v, m_q_down_proj, m_q_norm_g, m_q_up_proj, m_attn_out_proj, m_ffn_up, m_ffn_conv_w, m_ffn_conv_b, m_ffn_down, m_ln_mix_g, m_ln_mix_b, m_ln_ffn_g, m_ln_ffn_b, v_ssd_in_proj, v_ssd_conv_w, v_ssd_conv_b, v_ssd_dt_bias, v_ssd_A_log, v_ssd_D, v_ssd_norm_g, v_ssd_out_proj, v_kv_down_proj, v_kv_norm_g, v_kv_up_k, v_kv_up_v, v_q_down_proj, v_q_norm_g, v_q_up_proj, v_attn_out_proj, v_ffn_up, v_ffn_conv_w, v_ffn_conv_b, v_ffn_down, v_ln_mix_g, v_ln_mix_b, v_ln_ffn_g, v_ln_ffn_b):
    given = dict(locals())
    wl = {n: given[n] for n in WEIGHTS}
    ml = {n: given['m_' + n] for n in WEIGHTS}
    vl = {n: given['v_' + n] for n in WEIGHTS}
    sharded_axis = dict(SHARDED)
    big = [n for n, _ in SHARDED if n not in SHARDED_F32]
    small = [n for n, _ in SHARDED if n in SHARDED_F32]

    def as2d(a):
        return a.reshape(-1, a.shape[-1])

    def to_full(gathered, n):
        shp, ax = wl[n].shape, sharded_axis[n]
        moved = jnp.moveaxis(gathered.reshape((N_DEV,) + shp), 0, ax)
        return moved.reshape(shp[:ax] + (N_DEV * shp[ax],) + shp[ax + 1:])

    def to_chunks(full_grad, n):
        shp, ax = wl[n].shape, sharded_axis[n]
        split = full_grad.reshape(shp[:ax] + (N_DEV, shp[ax]) + shp[ax + 1:])
        return jnp.moveaxis(split, ax, 0).reshape((N_DEV,) + as2d(wl[n]).shape)

    first = [n for n in big if n in GATHERED_FIRST]
    late = [n for n in big if n not in GATHERED_FIRST]
    full = {n: wl[n] for n in REPLICATED}
    gathered = _all_gather([as2d(wl[n]).astype(BF16) for n in first] + [_pack([wl[n] for n in small], F32, 8)],
                           name='gather_weights')
    for n, gat in zip(first, gathered[:-1]):
        full[n] = to_full(gat, n)
    full.update(zip(small, _unpack_gathered(gathered[-1], [wl[n].shape for n in small],
                                            [sharded_axis[n] for n in small])))

    cx, cy, cc = lax.axis_index("x"), lax.axis_index("y"), lax.axis_index("c")
    core = jnp.reshape(cc, (1,)).astype(jnp.int32)
    chip = jnp.reshape(2 * cx + cy, (1,)).astype(jnp.int32)
    early_names = [n for n in big if n not in REDUCED_LAST]
    last_names = [n for n in big if n in REDUCED_LAST]
    parts = {}

    def add_sibling(names, chunked, received):
        parts.update((n, _add_sibling(gf, r, core, name=f'grads_add_sibling_{n}'))
                     for n, gf, r in zip(names, chunked, received))
        return [parts[n] for n in names]

    def early_blocks(grads):
        return [to_chunks(grads[n], n) for n in early_names]

    def early_parts(chunked, received):
        return add_sibling(early_names, chunked, received)

    def reduce_last(grads):
        chunked = [to_chunks(grads[n], n) for n in last_names]
        chunked.append(_pack_chunks([grads[n] for n in small], [sharded_axis[n] for n in small], 8))
        return add_sibling(last_names + ['small'], chunked, _exchange_sibling(chunked, name='grads_to_sibling_last'))

    sq, grad_x, grads, r2_early, r2_last = _local_step(
        x, positions, full, loss_target,
        late=([as2d(wl[n]).astype(BF16) for n in late], lambda gats: {n: to_full(g, n) for n, g in zip(late, gats)}),
        reduce_early=(early_blocks, early_parts), reduce_last=reduce_last)
    loss = lax.psum(0.5 * jnp.sum(sq) / x.shape[-1], ("x", "y", "c"))

    r2 = dict(zip(early_names, r2_early))
    r2.update(zip(last_names + ['small'], r2_last))
    res = {}
    kinds = ('grad', 'delta', 'new_m', 'new_v')
    for n in big:
        outs = _adam_sharded(parts[n], r2[n], chip, as2d(wl[n]), as2d(ml[n]), as2d(vl[n]), name=f'adamw_{n}')
        for kind, a in zip(kinds, outs):
            res[kind, n] = a.reshape(wl[n].shape)
    outs = _adam_sharded(parts['small'], r2['small'], chip, _pack([wl[n] for n in small], F32, 8),
                         _pack([ml[n] for n in small], F32, 8), _pack([vl[n] for n in small], F32, 8),
                         name='adamw_small_sharded')
    for kind, packed in zip(kinds, outs):
        for n, a in zip(small, _unpack(packed, [wl[n].shape for n in small])):
            res[kind, n] = a

    rep = list(REPLICATED)
    rshapes = [wl[n].shape for n in rep]
    gall, = _all_gather([_pack([grads[n] for n in rep], F32, 8)], name='gather_replicated_grads')
    outs = _adam_replicated(gall, _pack([wl[n] for n in rep], F32, 8), _pack([ml[n] for n in rep], F32, 8),
                            _pack([vl[n] for n in rep], F32, 8), name='adamw_replicated')
    for kind, packed in zip(('grad', 'delta', 'new_m', 'new_v'), outs):
        for n, a in zip(rep, _unpack(packed, rshapes)):
            res[kind, n] = a

    return (loss, grad_x, *[res['grad', n] for n in WEIGHTS], *[res['delta', n] for n in WEIGHTS],
            *[res['new_m', n] for n in WEIGHTS], *[res['new_v', n] for n in WEIGHTS])
```

```python
import math

import jax
import jax.numpy as jnp
from jax import lax
from jax.experimental import pallas as pl
from jax.experimental.pallas import tpu as pltpu

F32 = jnp.float32
BF16 = jnp.bfloat16
MESH = pl.DeviceIdType.MESH

N_DEV = 8
LANES = 128
MM_TILES = (1024, 1408, 896, 512, 384, 256, 128)
MM_VMEM_BUDGET = 38 * 1024 * 1024
MM_VMEM_LIMIT = 56 * 1024 * 1024
VMEM_LIMIT = 32 * 1024 * 1024
LN_EPS = 1e-5
RMS_EPS = 1e-6
SSD_HEAD_DIM = 64
SSD_N_GROUPS = 8
SSD_D_STATE = 128
SSD_CHUNK = 128
MLA_NOPE = 128
MLA_ROPE = 64
ROPE_THETA = 10000.0
ADAM_LR = 0.001
ADAM_B1 = 0.9
ADAM_B2 = 0.999
ADAM_EPS = 1e-08
ADAM_WD = 0.01
ADAM_STEP = 10
NEG_BIG = -1e30

SHARDED = (('ssd_in_proj', 2), ('ssd_conv_w', 2), ('ssd_conv_b', 1), ('ssd_norm_g', 1), ('ssd_out_proj', 1),
           ('kv_down_proj', 0), ('kv_up_k', 1), ('kv_up_v', 1), ('q_down_proj', 1), ('q_up_proj', 2),
           ('attn_out_proj', 1), ('ffn_up', 2), ('ffn_conv_w', 2), ('ffn_down', 1))
SHARDED_F32 = ('ssd_conv_w', 'ssd_conv_b', 'ssd_norm_g', 'ffn_conv_w')
GATHERED_FIRST = ('ssd_in_proj',)
REDUCED_LAST = ('ssd_in_proj',)
REPLICATED = ('ssd_dt_bias', 'ssd_A_log', 'ssd_D', 'kv_norm_g', 'q_norm_g', 'ffn_conv_b',
              'ln_mix_g', 'ln_mix_b', 'ln_ffn_g', 'ln_ffn_b')
WEIGHTS = ('ssd_in_proj', 'ssd_conv_w', 'ssd_conv_b', 'ssd_dt_bias', 'ssd_A_log', 'ssd_D', 'ssd_norm_g',
           'ssd_out_proj', 'kv_down_proj', 'kv_norm_g', 'kv_up_k', 'kv_up_v', 'q_down_proj', 'q_norm_g',
           'q_up_proj', 'attn_out_proj', 'ffn_up', 'ffn_conv_w', 'ffn_conv_b', 'ffn_down', 'ln_mix_g',
           'ln_mix_b', 'ln_ffn_g', 'ln_ffn_b')


def _cparams(limit=None):
    return pltpu.CompilerParams(vmem_limit_bytes=VMEM_LIMIT if limit is None else limit)


def _tile(n, cands):
    for c in cands:
        if n % c == 0:
            return c
    return n


def _sigmoid(x):
    return 1.0 / (1.0 + jnp.exp(-x))


def _iota(shape, axis):
    return lax.broadcasted_iota(jnp.int32, shape, axis)


def _mm(a, b, *, ta=False, tb=False, add=None, add_scale=1.0, out_dtype=F32, name, b_koff=0, into=None, into_col=0,
        exchange=()):
    if ta:
        K, M = a.shape
    else:
        M, K = a.shape
    if tb:
        N, Kb = b.shape
    else:
        Kb, N = b.shape
    assert b_koff + K <= Kb, (a.shape, b.shape, ta, tb, b_koff)
    tm = _tile(M, MM_TILES)
    tn = _tile(N, MM_TILES)
    tk = K if K <= MM_TILES[0] and b_koff == 0 and Kb == K else _tile(K, MM_TILES)
    nk = K // tk
    assert b_koff % tk == 0 and (into is None or (into.shape[0] == M and into.dtype == out_dtype))
    kb = b_koff // tk

    def vmem_estimate(tm_, tn_):
        ins = 2 * (tm_ * tk * a.dtype.itemsize + tk * tn_ * b.dtype.itemsize)
        outs = tm_ * tn_ * (2 * jnp.dtype(out_dtype).itemsize + 4 + (4 if nk > 1 else 0))
        return ins + outs + (2 * tm_ * tn_ * add.dtype.itemsize if add is not None else 0)

    while vmem_estimate(tm, tn) > MM_VMEM_BUDGET:
        can_m, can_n = tm % (2 * LANES) == 0, tn % (2 * LANES) == 0
        if can_m and (tm >= tn or not can_n):
            tm //= 2
        elif can_n:
            tn //= 2
        else:
            break

    assert into_col % tn == 0
    jb = into_col // tn

    nx = len(exchange)
    n_in = (add is not None) + (into is not None)
    grid = (M // tm, N // tn, nk)

    def body(a_ref, b_ref, *rest):
        add_ref = rest[0] if add is not None else None
        o_ref = rest[n_in + nx]
        acc = rest[n_in + 2 * nx + 1] if nk > 1 else None
        k = pl.program_id(2)
        if nx:
            start, finish_exchange = _chips_plan(rest[n_in:n_in + nx], rest[n_in + nx + 1:n_in + 2 * nx + 1],
                                                 *rest[len(rest) - 2:])
            t, total = _grid_step(grid)
            pl.when(t == 0)(start)

        if ta:
            av = a_ref[...].astype(BF16).T
        else:
            av = a_ref[...].astype(BF16)
        bv = b_ref[...].astype(BF16)
        dn = (((1,), (1 if tb else 0,)), ((), ()))
        part = lax.dot_general(av, bv, dn, preferred_element_type=F32)

        def finish(r):
            if add is not None:
                r = r + add_scale * add_ref[...].astype(F32)
            o_ref[...] = r.astype(out_dtype)

        if nk == 1:
            finish(part)
        else:
            @pl.when(k == 0)
            def _():
                acc[...] = part

            @pl.when(k > 0)
            def _():
                acc[...] += part

            @pl.when(k == nk - 1)
            def _():
                finish(acc[...])

        if nx:
            pl.when(t == total - 1)(finish_exchange)

    a_spec = (pl.BlockSpec((tk, tm), lambda i, j, k: (k, i)) if ta
              else pl.BlockSpec((tm, tk), lambda i, j, k: (i, k)))
    b_spec = (pl.BlockSpec((tn, tk), lambda i, j, k: (j, k + kb)) if tb
              else pl.BlockSpec((tk, tn), lambda i, j, k: (k + kb, j)))
    in_specs = [a_spec, b_spec]
    args = [a, b]
    if add is not None:
        in_specs.append(pl.BlockSpec((tm, tn), lambda i, j, k: (i, j)))
        args.append(add)
    aliases = {}
    if into is not None:
        aliases = {len(args): 0}
        in_specs.append(pl.BlockSpec(memory_space=pl.ANY))
        args.append(into)
    any_spec = pl.BlockSpec(memory_space=pl.ANY)
    outs = pl.pallas_call(
        body, name=name, grid=grid,
        in_specs=in_specs + [any_spec] * nx,
        out_specs=[pl.BlockSpec((tm, tn), lambda i, j, k: (i, j + jb))] + [any_spec] * nx,
        out_shape=[jax.ShapeDtypeStruct((M, N) if into is None else into.shape, out_dtype)] + _chips_shapes(exchange),
        scratch_shapes=([pltpu.VMEM((tm, tn), F32)] if nk > 1 else []) + (_chips_sems(nx) if nx else []),
        input_output_aliases=aliases, compiler_params=_cparams(MM_VMEM_LIMIT),
    )(*args, *exchange)
    return (outs[0], list(outs[1:])) if nx else outs[0]


def _ln_fwd(h, mix, g, b, alpha, name):
    T, D = h.shape
    tm = _tile(T, (512, 256, 128))

    def body(h_ref, m_ref, g_ref, b_ref, y_ref, s_ref, yb_ref):
        s = alpha * h_ref[...] + m_ref[...]
        mu = jnp.mean(s, axis=1, keepdims=True)
        xc = s - mu
        var = jnp.mean(xc * xc, axis=1, keepdims=True)
        y = xc * lax.rsqrt(var + LN_EPS) * g_ref[...] + b_ref[...]
        y_ref[...] = y
        s_ref[...] = s
        yb_ref[...] = y.astype(BF16)

    row = pl.BlockSpec((tm, D), lambda i: (i, 0))
    vec = pl.BlockSpec((1, D), lambda i: (0, 0))
    return pl.pallas_call(
        body, name=name, grid=(T // tm,), in_specs=[row, row, vec, vec], out_specs=[row, row, row],
        out_shape=[jax.ShapeDtypeStruct((T, D), F32)] * 2 + [jax.ShapeDtypeStruct((T, D), BF16)],
        compiler_params=_cparams(),
    )(h, mix, g, b)


def _ln_bwd(dy, s, g, name):
    T, D = s.shape
    tm = _tile(T, (512, 256, 128))

    def body(dy_ref, s_ref, g_ref, ds_ref, dsb_ref, dg_ref, db_ref):
        @pl.when(pl.program_id(0) == 0)
        def _():
            dg_ref[...] = jnp.zeros_like(dg_ref)
            db_ref[...] = jnp.zeros_like(db_ref)

        sv = s_ref[...]
        dyv = dy_ref[...]
        mu = jnp.mean(sv, axis=1, keepdims=True)
        xc = sv - mu
        rstd = lax.rsqrt(jnp.mean(xc * xc, axis=1, keepdims=True) + LN_EPS)
        xhat = xc * rstd
        dxh = dyv * g_ref[...]
        m1 = jnp.mean(dxh, axis=1, keepdims=True)
        m2 = jnp.mean(dxh * xhat, axis=1, keepdims=True)
        ds = rstd * (dxh - m1 - xhat * m2)
        ds_ref[...] = ds
        dsb_ref[...] = ds.astype(BF16)
        dg_ref[...] += jnp.sum(dyv * xhat, axis=0, keepdims=True)
        db_ref[...] += jnp.sum(dyv, axis=0, keepdims=True)

    row = pl.BlockSpec((tm, D), lambda i: (i, 0))
    vec = pl.BlockSpec((1, D), lambda i: (0, 0))
    return pl.pallas_call(
        body, name=name, grid=(T // tm,), in_specs=[row, row, vec], out_specs=[row, row, vec, vec],
        out_shape=[jax.ShapeDtypeStruct((T, D), F32), jax.ShapeDtypeStruct((T, D), BF16),
                   jax.ShapeDtypeStruct((1, D), F32), jax.ShapeDtypeStruct((1, D), F32)],
        compiler_params=_cparams(),
    )(dy, s, g)


def _loss_head(y, target, name):
    T, D = y.shape
    tm = _tile(T, (512, 256, 128))

    def body(y_ref, t_ref, sq_ref, dy_ref):
        @pl.when(pl.program_id(0) == 0)
        def _():
            sq_ref[...] = jnp.zeros_like(sq_ref)

        e = y_ref[...] - t_ref[...]
        sq_ref[...] += jnp.sum(e * e, axis=0, keepdims=True)
        dy_ref[...] = e * (1.0 / D)

    row = pl.BlockSpec((tm, D), lambda i: (i, 0))
    vec = pl.BlockSpec((1, D), lambda i: (0, 0))
    return pl.pallas_call(
        body, name=name, grid=(T // tm,), in_specs=[row, row], out_specs=[vec, row],
        out_shape=[jax.ShapeDtypeStruct((1, D), F32), jax.ShapeDtypeStruct((T, D), F32)],
        compiler_params=_cparams(),
    )(y, target)


def _shift_down(x, j):
    if j == 0:
        return x
    return jnp.where(_iota(x.shape, 0) >= j, pltpu.roll(x, j, 0), 0.0)


def _shift_up(x, j):
    if j == 0:
        return x
    n = x.shape[0]
    return jnp.where(_iota(x.shape, 0) < n - j, pltpu.roll(x, n - j, 0), 0.0)


def _conv_val(x, w_ref, b_ref):
    width = w_ref.shape[0]
    y = b_ref[...] + w_ref[width - 1:width, :] * x
    for j in range(1, width):
        y = y + w_ref[width - 1 - j:width - j, :] * _shift_down(x, j)
    return y


def _conv_bwd_val(x, dc, w_ref, dw_ref, db_ref):
    width = w_ref.shape[0]
    dx = w_ref[width - 1:width, :] * dc
    dw_ref[width - 1:width, :] += jnp.sum(dc * x, axis=0, keepdims=True)
    for j in range(1, width):
        sj = _shift_up(dc, j)
        dx = dx + w_ref[width - 1 - j:width - j, :] * sj
        dw_ref[width - 1 - j:width - j, :] += jnp.sum(sj * x, axis=0, keepdims=True)
    db_ref[...] += jnp.sum(dc, axis=0, keepdims=True)
    return dx


def _ffn_specs(H, S, width, cb):
    cb = _tile(H, (cb, LANES))
    nC = H // cb
    return dict(
        g=pl.BlockSpec((S, cb), lambda j, b: (b, j)), v=pl.BlockSpec((S, cb), lambda j, b: (b, j + nC)),
        wg=pl.BlockSpec((width, cb), lambda j, b: (0, j)), wv=pl.BlockSpec((width, cb), lambda j, b: (0, j + nC)),
        bg=pl.BlockSpec((1, cb), lambda j, b: (0, j)), bv=pl.BlockSpec((1, cb), lambda j, b: (0, j + nC))), nC


def _ffn_act_fwd(u, cw, cb, S, name):
    T, H2 = u.shape
    H = H2 // 2
    sp, nC = _ffn_specs(H, S, cw.shape[0], 2 * LANES)

    def body(ug_ref, uv_ref, wg_ref, wv_ref, bg_ref, bv_ref, a_ref):
        g = _conv_val(ug_ref[...], wg_ref, bg_ref)
        v = _conv_val(uv_ref[...], wv_ref, bv_ref)
        a_ref[...] = (g * _sigmoid(g) * v).astype(BF16)

    return pl.pallas_call(
        body, name=name, grid=(nC, T // S),
        in_specs=[sp['g'], sp['v'], sp['wg'], sp['wv'], sp['bg'], sp['bv']], out_specs=sp['g'],
        out_shape=jax.ShapeDtypeStruct((T, H), BF16), compiler_params=_cparams(),
    )(u, u, cw, cw, cb, cb)


def _ffn_act_bwd(u, da, cw, cb, S, name):
    T, H2 = u.shape
    H = H2 // 2
    width = cw.shape[0]
    sp, nC = _ffn_specs(H, S, width, LANES)

    def body(ug_ref, uv_ref, da_ref, wg_ref, wv_ref, bg_ref, bv_ref,
             dug_ref, duv_ref, dwg_ref, dwv_ref, dbg_ref, dbv_ref):
        @pl.when(pl.program_id(1) == 0)
        def _():
            for r in (dwg_ref, dwv_ref, dbg_ref, dbv_ref):
                r[...] = jnp.zeros_like(r)

        def conv_bwd(x, dc, w_ref, dw_ref, db_ref):
            dx = w_ref[width - 1:width, :] * dc
            for j in range(1, width):
                dx = dx + w_ref[width - 1 - j:width - j, :] * _shift_up(dc, j)
            for j in range(width):
                dw_ref[width - 1 - j:width - j, :] += jnp.sum(dc * _shift_down(x, j), axis=0, keepdims=True)
            db_ref[...] += jnp.sum(dc, axis=0, keepdims=True)
            return dx

        xg = ug_ref[...]
        xv = uv_ref[...]
        g = _conv_val(xg, wg_ref, bg_ref)
        v = _conv_val(xv, wv_ref, bv_ref)
        dav = da_ref[...].astype(F32)
        sg = _sigmoid(g)
        dg = dav * v * sg * (1.0 + g * (1.0 - sg))
        dv = dav * g * sg
        dug_ref[...] = conv_bwd(xg, dg, wg_ref, dwg_ref, dbg_ref).astype(BF16)
        duv_ref[...] = conv_bwd(xv, dv, wv_ref, dwv_ref, dbv_ref).astype(BF16)

    act = jax.ShapeDtypeStruct((T, H), BF16)
    wsh = jax.ShapeDtypeStruct((width, H), F32)
    bsh = jax.ShapeDtypeStruct((1, H), F32)
    return pl.pallas_call(
        body, name=name, grid=(nC, T // S),
        in_specs=[sp['g'], sp['v'], sp['g'], sp['wg'], sp['wv'], sp['bg'], sp['bv']],
        out_specs=[sp['g'], sp['g'], sp['wg'], sp['wg'], sp['bg'], sp['bg']],
        out_shape=[act, act, wsh, wsh, bsh, bsh], compiler_params=_cparams(),
    )(u, u, da, cw, cw, cb, cb)


def _ssd_conv_fwd(x, cw, cb, S, name):
    T, C = x.shape
    Cb = _tile(C, (256, 128))
    width = cw.shape[0]

    def body(x_ref, w_ref, b_ref, y_ref, c_ref):
        c = _conv_val(x_ref[...], w_ref, b_ref)
        c_ref[...] = c
        y_ref[...] = c * _sigmoid(c)

    blk = pl.BlockSpec((S, Cb), lambda j, b: (b, j))
    wblk = pl.BlockSpec((width, Cb), lambda j, b: (0, j))
    bblk = pl.BlockSpec((1, Cb), lambda j, b: (0, j))
    return pl.pallas_call(
        body, name=name, grid=(C // Cb, T // S), in_specs=[blk, wblk, bblk], out_specs=[blk, blk],
        out_shape=[jax.ShapeDtypeStruct((T, C), F32)] * 2, compiler_params=_cparams(),
    )(x, cw, cb)


def _ssd_conv_bwd(x, c, dys, cw, S, name):
    T, C = x.shape
    Cb = _tile(C, (256, 128))
    width = cw.shape[0]
    assert all(d.shape[1] % Cb == 0 for d in dys) and sum(d.shape[1] for d in dys) == C

    def part(dy, j0, dx_prev, k):
        n = dy.shape[1] // Cb

        def body(x_ref, c_ref, dy_ref, w_ref, prev_ref, dx_ref, dw_ref, db_ref):
            @pl.when(pl.program_id(1) == 0)
            def _():
                dw_ref[...] = jnp.zeros_like(dw_ref)
                db_ref[...] = jnp.zeros_like(db_ref)

            cval = c_ref[...]
            sc = _sigmoid(cval)
            dc = dy_ref[...] * sc * (1.0 + cval * (1.0 - sc))
            dx_ref[...] = _conv_bwd_val(x_ref[...], dc, w_ref, dw_ref, db_ref).astype(BF16)

        wide = pl.BlockSpec((S, Cb), lambda j, b: (b, j + j0))
        return pl.pallas_call(
            body, name=f'{name}_{k}', grid=(n, T // S),
            in_specs=[wide, wide, pl.BlockSpec((S, Cb), lambda j, b: (b, j)),
                      pl.BlockSpec((width, Cb), lambda j, b: (0, j + j0)), pl.BlockSpec(memory_space=pl.ANY)],
            out_specs=[wide, pl.BlockSpec((width, Cb), lambda j, b: (0, j)), pl.BlockSpec((1, Cb), lambda j, b: (0, j))],
            out_shape=[jax.ShapeDtypeStruct((T, C), BF16), jax.ShapeDtypeStruct((width, n * Cb), F32),
                       jax.ShapeDtypeStruct((1, n * Cb), F32)],
            input_output_aliases={4: 0}, compiler_params=_cparams(),
        )(x, c, dy, cw, dx_prev)

    dx = lax.empty((T, C), BF16)
    dws, dbs, j0 = [], [], 0
    for k, dy in enumerate(dys):
        dx, dw, db = part(dy, j0, dx, k)
        dws.append(dw)
        dbs.append(db)
        j0 += dy.shape[1] // Cb
    return dx, jnp.concatenate(dws, axis=1), jnp.concatenate(dbs, axis=1)


def _softplus(x):
    e = jnp.exp(-jnp.abs(x))
    return jnp.maximum(x, 0.0) + jnp.where(e < 1e-4, e * (1.0 - 0.5 * e), jnp.log(1.0 + e))


def _cumsum_rows(x):
    n = x.shape[0]
    row = _iota(x.shape, 0)
    k = 1
    while k < n:
        x = x + jnp.where(row >= k, pltpu.roll(x, k, 0), 0.0)
        k *= 2
    return x


def _rev_cumsum_rows(x):
    n = x.shape[0]
    row = _iota(x.shape, 0)
    k = 1
    while k < n:
        x = x + jnp.where(row < n - k, pltpu.roll(x, n - k, 0), 0.0)
        k *= 2
    return x


def _col(x, lane_ids, h):
    return jnp.sum(jnp.where(lane_ids == h, x, 0.0), axis=1, keepdims=True)


def _bdot(a, b, dims):
    return lax.dot_general(a.astype(BF16), b.astype(BF16), (dims, ((), ())), preferred_element_type=F32)


NN = ((1,), (0,))
NT = ((1,), (1,))


def _ssd_dt_prep(dt_pre, dt_bias, a_log, n_heads, name):
    T = dt_pre.shape[0]
    L = SSD_CHUNK

    def body(dtp_ref, bias_ref, alog_ref, dtb_ref, cumb_ref):
        dt = _softplus(dtp_ref[...] + bias_ref[...])
        cum = _cumsum_rows(dt * (-jnp.exp(alog_ref[...])))
        lane = _iota((L, LANES), 1)
        for h in range(n_heads):
            hcols = pl.ds(h * LANES, LANES)
            dtb_ref[:, hcols] = jnp.broadcast_to(_col(dt, lane, h), (L, LANES))
            cumb_ref[:, hcols] = jnp.broadcast_to(_col(cum, lane, h), (L, LANES))

    row = pl.BlockSpec((L, LANES), lambda i: (i, 0))
    vec = pl.BlockSpec((1, LANES), lambda i: (0, 0))
    wide = pl.BlockSpec((L, n_heads * LANES), lambda i: (i, 0))
    return pl.pallas_call(
        body, name=name, grid=(T // L,), in_specs=[row, vec, vec], out_specs=[wide, wide],
        out_shape=[jax.ShapeDtypeStruct((T, n_heads * LANES), F32)] * 2, compiler_params=_cparams(),
    )(dt_pre, dt_bias, a_log)


SCAN_GROUPS_FWD = 8
SCAN_GROUPS_BWD = 4


def _scan_specs(nc, d_inner, hpg, GP):
    L = SSD_CHUNK
    G = SSD_N_GROUPS
    gw = GP * hpg * SSD_HEAD_DIM
    bw = GP * LANES
    assert G % GP == 0 and d_inner % bw == 0
    nb = d_inner // bw
    return dict(
        xs=pl.BlockSpec((L, gw), lambda b, q, c: (b * nc + c, q)),
        B=pl.BlockSpec((L, bw), lambda b, q, c: (b * nc + c, nb + q)),
        C=pl.BlockSpec((L, bw), lambda b, q, c: (b * nc + c, nb + G // GP + q)),
        heads=pl.BlockSpec((L, GP * hpg * LANES), lambda b, q, c: (b * nc + c, q)),
        gvec=pl.BlockSpec((1, gw), lambda b, q, c: (0, q)),
        grp=pl.BlockSpec((L, bw), lambda b, q, c: (b * nc + c, q)),
        st=pl.BlockSpec((1, GP * hpg // 2, SSD_D_STATE, LANES), lambda b, q, c: (b * nc + c, q, 0, 0)),
    )


def _grid_step(dims):
    t, total = 0, 1
    for ax, d in enumerate(dims):
        t = t * d + pl.program_id(ax)
        total *= d
    return t, total


def _scan_fwd(xbc, dtb, cumb, S, d_inner, name, gather=()):
    T = xbc.shape[0]
    Bn = T // S
    L = SSD_CHUNK
    G = SSD_N_GROUPS
    nc = S // L
    hpg = d_inner // SSD_HEAD_DIM // G
    pairs = hpg // 2
    GP = SCAN_GROUPS_FWD
    sp = _scan_specs(nc, d_inner, hpg, GP)
    ng = len(gather)

    def body(*refs):
        xs_ref, b_ref, c_ref, dtb_ref, cumb_ref = refs[:5]
        y_ref, st_ref = refs[5 + ng:7 + ng]
        state = refs[7 + 2 * ng]

        if ng:
            start, forward, finish = _gather_plan(refs[5:5 + ng], refs[7 + ng:7 + 2 * ng], *refs[8 + 2 * ng:])
            t, total = _grid_step((Bn, G // GP, nc))
            pl.when(t == 0)(start)
            pl.when(t == total // 2)(forward)

        @pl.when(pl.program_id(2) == 0)
        def _():
            state[...] = jnp.zeros_like(state)

        tril = _iota((L, L), 1) <= _iota((L, L), 0)
        lane = _iota((L, LANES), 1)
        lane1 = _iota((1, LANES), 1)
        last = _iota((L, 1), 0) == L - 1
        for gi, p in [(gi, p) for gi in range(GP) for p in range(pairs)]:
            if p == 0:
                Bm = b_ref[:, gi * LANES:(gi + 1) * LANES]
                Cm = c_ref[:, gi * LANES:(gi + 1) * LANES]
                Gm = _bdot(Cm, Bm, NT)
                S_cat = jnp.concatenate([state[gi * pairs + q] for q in range(pairs)], axis=1)
                Q_cat = _bdot(Cm, S_cat, NN)
                z_pairs, el_pairs = [], []
            sp_ = gi * pairs + p
            cols = pl.ds(sp_ * LANES, LANES)
            st_ref[0, sp_] = state[sp_]
            x_pair = xs_ref[:, cols]
            z_pair = jnp.zeros((L, LANES), F32)
            e_pair = jnp.zeros((L, LANES), F32)
            el_pair = jnp.zeros((1, LANES), F32)
            wms, xds = [], []
            for k in range(2):
                hcols = pl.ds((gi * hpg + 2 * p + k) * LANES, LANES)
                cum_col = cumb_ref[:, hcols]
                dt_col = dtb_ref[:, hcols]
                decay = jnp.exp(jnp.where(tril, cum_col - cum_col.T, NEG_BIG))
                mk = (lane >= SSD_HEAD_DIM) if k else (lane < SSD_HEAD_DIM)
                xd = jnp.where(mk, x_pair * dt_col, 0.0)
                wms.append((Gm * decay).astype(BF16))
                xds.append(xd.astype(BF16))
                cum_l = jnp.sum(jnp.where(last, cum_col, 0.0), axis=0, keepdims=True)
                e_pair = jnp.where(mk, jnp.exp(cum_col), e_pair)
                z_pair = z_pair + xd * jnp.exp(cum_l - cum_col)
                el_pair = jnp.where((lane1 >= SSD_HEAD_DIM) if k else (lane1 < SSD_HEAD_DIM), jnp.exp(cum_l), el_pair)
            y_pair = _bdot(jnp.concatenate(wms, axis=1), jnp.concatenate(xds, axis=0), NN)
            y_ref[:, cols] = y_pair + e_pair * Q_cat[:, p * LANES:(p + 1) * LANES]
            z_pairs.append(z_pair)
            el_pairs.append(el_pair)
            if p == pairs - 1:
                S_new = (S_cat * jnp.concatenate(el_pairs, axis=1)
                         + _bdot(Bm.T, jnp.concatenate(z_pairs, axis=1), NN))
                for q in range(pairs):
                    state[gi * pairs + q] = S_new[:, q * LANES:(q + 1) * LANES]

        if ng:
            pl.when(t == total - 1)(finish)

    outs = pl.pallas_call(
        body, name=name, grid=(Bn, G // GP, nc),
        in_specs=[sp['xs'], sp['B'], sp['C'], sp['heads'], sp['heads']] + [ANY] * ng,
        out_specs=[sp['xs'], sp['st']] + [ANY] * ng,
        out_shape=[jax.ShapeDtypeStruct((T, d_inner), F32),
                   jax.ShapeDtypeStruct((Bn * nc, G * pairs, SSD_D_STATE, LANES), F32)] + _gather_shapes(gather),
        scratch_shapes=[pltpu.VMEM((GP * pairs, SSD_D_STATE, LANES), F32)] + (_gather_sems(ng) if ng else []),
        compiler_params=_cparams(),
    )(xbc, xbc, xbc, dtb, cumb, *gather)
    return outs[0], outs[1], list(outs[2:])


def _scan_bwd(xbc, dtb, cumb, states, dy, d_lane, S, d_inner, name, exchange=()):
    T = xbc.shape[0]
    Bn = T // S
    L = SSD_CHUNK
    G = SSD_N_GROUPS
    nc = S // L
    hpg = d_inner // SSD_HEAD_DIM // G
    pairs = hpg // 2
    GP = SCAN_GROUPS_BWD
    sp = _scan_specs(nc, d_inner, hpg, GP)
    nx = len(exchange)

    def rev(spec):
        im = spec.index_map
        return pl.BlockSpec(spec.block_shape, lambda b, g, c: im(b, g, nc - 1 - c))

    def body(*refs):
        xs_ref, b_ref, c_ref, dtb_ref, cumb_ref, st_ref, dy_ref, dl_ref = refs[:8]
        dxs_ref, db_ref, dc_ref, ddt_ref, da_ref = refs[8 + nx:13 + nx]
        dstate = refs[13 + 2 * nx]
        g = pl.program_id(1)

        if nx:
            start, finish = _chips_plan(refs[8:8 + nx], refs[13 + nx:13 + 2 * nx], *refs[14 + 2 * nx:])
            t, total = _grid_step((Bn, G // GP, nc))
            pl.when(t == 0)(start)

        @pl.when(pl.program_id(2) == 0)
        def _():
            dstate[...] = jnp.zeros_like(dstate)

        tril = _iota((L, L), 1) <= _iota((L, L), 0)
        lane = _iota((L, LANES), 1)
        lane1 = _iota((1, LANES), 1)
        last = _iota((L, 1), 0) == L - 1
        for gi, p in [(gi, p) for gi in range(GP) for p in range(pairs)]:
            gcols = pl.ds(gi * LANES, LANES)
            if p == 0:
                Bm = b_ref[:, gcols]
                Cm = c_ref[:, gcols]
                Gm = _bdot(Cm, Bm, NT)
                dG = jnp.zeros((L, L), F32)
                dcum = jnp.zeros((L, LANES), F32)
                ddt = jnp.zeros((L, LANES), F32)
                S_cat = jnp.concatenate([st_ref[0, gi * pairs + q] for q in range(pairs)], axis=1)
                dS_cat = jnp.concatenate([dstate[gi * pairs + q] for q in range(pairs)], axis=1)
                Q_cat = _bdot(Cm, S_cat, NN)
                dZ_cat = _bdot(Bm, dS_cat, NN)
                dQs, zs, els = [], [], []
            sp_ = gi * pairs + p
            cols = pl.ds(sp_ * LANES, LANES)
            pcols = slice(p * LANES, (p + 1) * LANES)
            x_pair = xs_ref[:, cols]
            dy_pair = dy_ref[:, cols]
            Q = Q_cat[:, pcols]
            dZ = dZ_cat[:, pcols]
            prod = dS_cat[:, pcols] * S_cat[:, pcols]
            e_pair = jnp.zeros((L, LANES), F32)
            z_pair = jnp.zeros((L, LANES), F32)
            el_pair = jnp.zeros((1, LANES), F32)
            dx_pair = dl_ref[:, cols] * dy_pair
            heads = []
            for k in range(2):
                hcols = pl.ds((gi * hpg + 2 * p + k) * LANES, LANES)
                cum_col = cumb_ref[:, hcols]
                dt_col = dtb_ref[:, hcols]
                decay = jnp.exp(jnp.where(tril, cum_col - cum_col.T, NEG_BIG))
                mk = (lane >= SSD_HEAD_DIM) if k else (lane < SSD_HEAD_DIM)
                heads.append((cum_col, dt_col, decay, mk, jnp.where(mk, x_pair * dt_col, 0.0),
                              jnp.where(mk, dy_pair, 0.0), Gm * decay))
            dy_both = jnp.concatenate([hd[5] for hd in heads], axis=0).astype(BF16)
            dWm_both = _bdot(dy_both, heads[0][4] + heads[1][4], NT)
            dxd_pair = _bdot(jnp.concatenate([hd[6].T for hd in heads], axis=1), dy_both, NN)
            for k, (cum_col, dt_col, decay, mk, xd, dy_k, Wm) in enumerate(heads):
                h = (g * GP + gi) * hpg + 2 * p + k
                mk1 = (lane1 >= SSD_HEAD_DIM) if k else (lane1 < SSD_HEAD_DIM)
                dWm = jnp.where(tril, dWm_both[k * L:(k + 1) * L], 0.0)
                dxd = jnp.where(mk, dxd_pair, 0.0)
                dseg = dWm * Wm
                dG = dG + dWm * decay
                dcum_col = (jnp.sum(dseg, axis=1, keepdims=True)
                            - jnp.sum(dseg.T, axis=1, keepdims=True))
                cum_l = jnp.sum(jnp.where(last, cum_col, 0.0), axis=0, keepdims=True)
                e_col = jnp.exp(cum_col)
                w_col = jnp.exp(cum_l - cum_col)
                el = jnp.exp(cum_l)
                dcum_col = dcum_col + jnp.sum(dy_k * Q, axis=1, keepdims=True) * e_col
                de_e = jnp.sum(dZ * xd, axis=1, keepdims=True) * w_col
                dcum_col = dcum_col - de_e
                dcum_l = (jnp.sum(de_e, axis=0, keepdims=True)
                          + el * jnp.sum(jnp.sum(jnp.where(mk, prod, 0.0), axis=1, keepdims=True),
                                         axis=0, keepdims=True))
                dcum_col = dcum_col + jnp.where(last, dcum_l, 0.0)
                dxd = dxd + jnp.where(mk, dZ * w_col, 0.0)
                dx_pair = dx_pair + dxd * dt_col
                ddt = jnp.where(lane == h, jnp.sum(dxd * x_pair, axis=1, keepdims=True), ddt)
                dcum = jnp.where(lane == h, dcum_col, dcum)
                e_pair = jnp.where(mk, e_col, e_pair)
                z_pair = z_pair + xd * w_col
                el_pair = jnp.where(mk1, el, el_pair)
            dQs.append(e_pair * dy_pair)
            zs.append(z_pair)
            els.append(el_pair)
            dxs_ref[:, cols] = dx_pair
            if p == pairs - 1:
                dQ_cat = jnp.concatenate(dQs, axis=1)
                dS_new = dS_cat * jnp.concatenate(els, axis=1) + _bdot(Cm.T, dQ_cat, NN)
                for q in range(pairs):
                    dstate[gi * pairs + q] = dS_new[:, q * LANES:(q + 1) * LANES]
                dc_ref[:, gcols] = _bdot(dQ_cat, S_cat, NT) + _bdot(dG, Bm, NN)
                db_ref[:, gcols] = _bdot(jnp.concatenate(zs, axis=1), dS_cat, NT) + _bdot(dG.T, Cm, NN)
                ddt_ref[:, gcols] = ddt
                da_ref[:, gcols] = _rev_cumsum_rows(dcum)

        if nx:
            pl.when(t == total - 1)(finish)

    grp = jax.ShapeDtypeStruct((T, G * LANES), F32)
    outs = pl.pallas_call(
        body, name=name, grid=(Bn, G // GP, nc),
        in_specs=[rev(sp['xs']), rev(sp['B']), rev(sp['C']), rev(sp['heads']), rev(sp['heads']),
                  rev(sp['st']), rev(sp['xs']), sp['gvec']] + [ANY] * nx,
        out_specs=[rev(sp['xs']), rev(sp['grp']), rev(sp['grp']), rev(sp['grp']), rev(sp['grp'])] + [ANY] * nx,
        out_shape=[jax.ShapeDtypeStruct((T, d_inner), F32), grp, grp, grp, grp] + _chips_shapes(exchange),
        scratch_shapes=[pltpu.VMEM((GP * pairs, SSD_D_STATE, LANES), F32)] + (_chips_sems(nx) if nx else []),
        compiler_params=_cparams(),
    )(xbc, xbc, xbc, dtb, cumb, states, dy, d_lane, *exchange)
    return outs[:5], list(outs[5:])


def _dt_bwd(ddt_g, da_g, dt_pre, dt_bias, a_log, name):
    T = dt_pre.shape[0]
    G = SSD_N_GROUPS
    tm = _tile(T, (512, 256, 128))

    def body(ddt_ref, da_ref, dtp_ref, bias_ref, alog_ref, dx_ref, dbias_ref, dal_ref):
        @pl.when(pl.program_id(0) == 0)
        def _():
            dbias_ref[...] = jnp.zeros_like(dbias_ref)
            dal_ref[...] = jnp.zeros_like(dal_ref)

        ddt = ddt_ref[:, 0:LANES]
        da = da_ref[:, 0:LANES]
        for gi in range(1, G):
            ddt = ddt + ddt_ref[:, gi * LANES:(gi + 1) * LANES]
            da = da + da_ref[:, gi * LANES:(gi + 1) * LANES]
        xv = dtp_ref[...] + bias_ref[...]
        A = -jnp.exp(alog_ref[...])
        dx = (ddt + da * A) * _sigmoid(xv)
        dx_ref[...] = dx.astype(BF16)
        dbias_ref[...] += jnp.sum(dx, axis=0, keepdims=True)
        dal_ref[...] += jnp.sum(da * _softplus(xv), axis=0, keepdims=True) * A

    wide = pl.BlockSpec((tm, G * LANES), lambda i: (i, 0))
    row = pl.BlockSpec((tm, LANES), lambda i: (i, 0))
    vec = pl.BlockSpec((1, LANES), lambda i: (0, 0))
    vsh = jax.ShapeDtypeStruct((1, LANES), F32)
    return pl.pallas_call(
        body, name=name, grid=(T // tm,), in_specs=[wide, wide, row, vec, vec], out_specs=[row, vec, vec],
        out_shape=[jax.ShapeDtypeStruct((T, LANES), BF16), vsh, vsh], compiler_params=_cparams(),
    )(ddt_g, da_g, dt_pre, dt_bias, a_log)


def _gate_fwd(y, xbc, z, d_lane, ng, d_inner, name):
    T = y.shape[0]
    G = SSD_N_GROUPS
    gw = d_inner // G
    tm = _tile(T, (1024, 512, 256, 128))

    def body(y_ref, x_ref, z_ref, dl_ref, ng_ref, o_ref):
        zv = z_ref[...]
        y2 = (y_ref[...] + dl_ref[...] * x_ref[...]) * (zv * _sigmoid(zv))
        r = lax.rsqrt(jnp.mean(y2 * y2, axis=1, keepdims=True) + LN_EPS)
        o_ref[...] = (y2 * r * ng_ref[...]).astype(BF16)

    blk = pl.BlockSpec((tm, gw), lambda g, i: (i, g))
    vec = pl.BlockSpec((1, gw), lambda g, i: (0, g))
    return pl.pallas_call(
        body, name=name, grid=(G, T // tm), in_specs=[blk, blk, blk, vec, vec], out_specs=blk,
        out_shape=jax.ShapeDtypeStruct((T, d_inner), BF16), compiler_params=_cparams(),
    )(y, xbc, z, d_lane, ng)


def _gate_bwd(dyn, y, xbc, z, d_lane, ng, d_inner, name, exchange=()):
    T = y.shape[0]
    G = SSD_N_GROUPS
    gw = d_inner // G
    tm = _tile(T, (1024, 512, 256, 128))
    nx = len(exchange)

    def body(*refs):
        dyn_ref, y_ref, x_ref, z_ref, dl_ref, ng_ref = refs[:6]
        dy_ref, dz_ref, dng_ref, ddl_ref = refs[6 + nx:10 + nx]
        if nx:
            start, finish = _sibling_plan(refs[6:6 + nx], refs[10 + nx:10 + 2 * nx], *refs[10 + 2 * nx:])
            t, total = _grid_step((G, T // tm))
            pl.when(t == 0)(start)

        @pl.when(pl.program_id(1) == 0)
        def _():
            dng_ref[...] = jnp.zeros_like(dng_ref)
            ddl_ref[...] = jnp.zeros_like(ddl_ref)

        zv = z_ref[...]
        xv = x_ref[...]
        sz = _sigmoid(zv)
        y1 = y_ref[...] + dl_ref[...] * xv
        y2 = y1 * (zv * sz)
        r = lax.rsqrt(jnp.mean(y2 * y2, axis=1, keepdims=True) + LN_EPS)
        dn = dyn_ref[...].astype(F32)
        dng_ref[...] += jnp.sum(dn * y2 * r, axis=0, keepdims=True)
        do = dn * ng_ref[...]
        dy2 = r * do - y2 * (r * r * r) * jnp.mean(do * y2, axis=1, keepdims=True)
        dz_ref[...] = (dy2 * y1 * sz * (1.0 + zv * (1.0 - sz))).astype(BF16)
        dy1 = dy2 * (zv * sz)
        dy_ref[...] = dy1
        ddl_ref[...] += jnp.sum(dy1 * xv, axis=0, keepdims=True)
        if nx:
            pl.when(t == total - 1)(finish)

    blk = pl.BlockSpec((tm, gw), lambda g, i: (i, g))
    vec = pl.BlockSpec((1, gw), lambda g, i: (0, g))
    vsh = jax.ShapeDtypeStruct((1, d_inner), F32)
    outs = pl.pallas_call(
        body, name=name, grid=(G, T // tm), in_specs=[blk, blk, blk, blk, vec, vec] + [ANY] * nx,
        out_specs=[blk, blk, vec, vec] + [ANY] * nx,
        out_shape=[jax.ShapeDtypeStruct((T, d_inner), F32), jax.ShapeDtypeStruct((T, d_inner), BF16), vsh, vsh]
        + _sibling_shapes(exchange),
        scratch_shapes=_sibling_sems(nx) if nx else [], compiler_params=_cparams(),
    )(dyn, y, xbc, z, d_lane, ng, *exchange)
    return outs[:4], list(outs[4:])


def _rms_fwd(x, g, name):
    T, R = x.shape
    tm = _tile(T, (512, 256, 128))

    def body(x_ref, g_ref, y_ref):
        xv = x_ref[...]
        y = xv * lax.rsqrt(jnp.mean(xv * xv, axis=1, keepdims=True) + RMS_EPS) * g_ref[...]
        y_ref[...] = y.astype(BF16)

    row = pl.BlockSpec((tm, R), lambda i: (i, 0))
    vec = pl.BlockSpec((1, R), lambda i: (0, 0))
    return pl.pallas_call(
        body, name=name, grid=(T // tm,), in_specs=[row, vec], out_specs=row,
        out_shape=jax.ShapeDtypeStruct((T, R), BF16), compiler_params=_cparams(),
    )(x, g)


def _rms_bwd(dy, x, g, name):
    T, R = x.shape
    tm = _tile(T, (512, 256, 128))

    def body(dy_ref, x_ref, g_ref, dx_ref, dg_ref):
        @pl.when(pl.program_id(0) == 0)
        def _():
            dg_ref[...] = jnp.zeros_like(dg_ref)

        xv = x_ref[...]
        dyv = dy_ref[...]
        r = lax.rsqrt(jnp.mean(xv * xv, axis=1, keepdims=True) + RMS_EPS)
        dg_ref[...] += jnp.sum(dyv * xv * r, axis=0, keepdims=True)
        do = dyv * g_ref[...]
        dx = r * do - xv * (r * r * r) * jnp.mean(do * xv, axis=1, keepdims=True)
        dx_ref[...] = dx.astype(BF16)

    row = pl.BlockSpec((tm, R), lambda i: (i, 0))
    vec = pl.BlockSpec((1, R), lambda i: (0, 0))
    return pl.pallas_call(
        body, name=name, grid=(T // tm,), in_specs=[row, row, vec], out_specs=[row, vec],
        out_shape=[jax.ShapeDtypeStruct((T, R), BF16), jax.ShapeDtypeStruct((1, R), F32)],
        compiler_params=_cparams(),
    )(dy, x, g)


def _swap_halves(x):
    half = MLA_ROPE // 2
    return jnp.where(_iota(x.shape, 1) < half, pltpu.roll(x, LANES - half, 1), pltpu.roll(x, half, 1))


def _rope(x, cc, ss, *, transpose, sum_heads=False, name):
    T, W = x.shape
    nh = W // LANES
    tm = _tile(T, (512, 256, 128))
    n_out = 1 if sum_heads else nh

    def body(x_ref, cc_ref, ss_ref, o_ref):
        ccv = cc_ref[...]
        ssv = ss_ref[...]
        if sum_heads:
            xv = x_ref[:, 0:LANES]
            for hh in range(1, nh):
                xv = xv + x_ref[:, hh * LANES:(hh + 1) * LANES]
            heads = [xv]
        else:
            heads = [x_ref[:, hh * LANES:(hh + 1) * LANES] for hh in range(nh)]
        for hh, xv in enumerate(heads):
            if transpose:
                r = xv * ccv + _swap_halves(xv * ssv)
            else:
                r = xv * ccv + _swap_halves(xv) * ssv
            r = jnp.where(_iota(r.shape, 1) < MLA_ROPE, r, 0.0)
            o_ref[:, hh * LANES:(hh + 1) * LANES] = r.astype(BF16)

    return pl.pallas_call(
        body, name=name, grid=(T // tm,),
        in_specs=[pl.BlockSpec((tm, W), lambda i: (i, 0)), pl.BlockSpec((tm, LANES), lambda i: (i, 0)),
                  pl.BlockSpec((tm, LANES), lambda i: (i, 0))],
        out_specs=pl.BlockSpec((tm, n_out * LANES), lambda i: (i, 0)),
        out_shape=jax.ShapeDtypeStruct((T, n_out * LANES), BF16), compiler_params=_cparams(),
    )(x, cc, ss)


ATT_BLOCK = 512
ATT_SUB = 256
ATT_SUB_KV = 512
LOG2E = 1.4426950408889634


def _att_geometry(S):
    tb = _tile(S, (ATT_BLOCK, 256))
    return tb, S // tb, tb // ATT_SUB


def _heads_per_step(nh):
    return 4 if nh % 4 == 0 else 2 if nh % 2 == 0 else 1


def _att_scores(a, b, diag, kv_major, off):
    s = _bdot(a, b, NT)
    if diag:
        q_ax, k_ax = (1, 0) if kv_major else (0, 1)
        s = jnp.where(_iota(s.shape, k_ax) <= _iota(s.shape, q_ax) + off, s, NEG_BIG)
    return s


def _attention_fwd(qn, qr, kn, kr, v, S, scale, name):
    T, W = qn.shape
    nh = W // LANES
    Bn = T // S
    tb, n, C = _att_geometry(S)
    pairs = [(i, j) for i in range(n) for j in range(i + 1)]
    it = jnp.asarray([p[0] for p in pairs], jnp.int32)
    jt = jnp.asarray([p[1] for p in pairs], jnp.int32)
    c2 = scale * LOG2E

    HP = _heads_per_step(nh)

    def body(it_ref, jt_ref, qn_ref, qr_ref, kn_ref, kr_ref, v_ref, o_ref, ob_ref, lse_ref, m_sc, l_sc, acc):
        p = pl.program_id(2)
        i = it_ref[p]
        j = jt_ref[p]

        @pl.when(j == 0)
        def _():
            m_sc[...] = jnp.full_like(m_sc, NEG_BIG)
            l_sc[...] = jnp.zeros_like(l_sc)
            acc[...] = jnp.zeros_like(acc)

        def step(diag):
            for hh in range(HP):
                hs = pl.ds(hh * LANES, LANES)
                qc = jnp.concatenate([qn_ref[:, hs], qr_ref[:, hs]], axis=1)
                kc = jnp.concatenate([kn_ref[:, hs], kr_ref[...]], axis=1)
                st = _att_scores(kc, qc, diag, True, 0)
                m_old = m_sc[hh]
                m_new = jnp.maximum(m_old, jnp.max(st, axis=0, keepdims=True))
                pt = jnp.exp2((st - m_new) * c2)
                corr = jnp.exp2((m_old - m_new) * c2)
                l_sc[hh] = corr * l_sc[hh] + jnp.sum(pt, axis=0, keepdims=True)
                acc[hh] = corr * acc[hh] + _bdot(v_ref[:, hs].T, pt, NN)
                m_sc[hh] = m_new

        @pl.when(j < i)
        def _():
            step(False)

        @pl.when(j == i)
        def _():
            step(True)
            for hh in range(HP):
                hs = pl.ds(hh * LANES, LANES)
                ov = (acc[hh] / l_sc[hh]).T
                o_ref[:, hs] = ov
                ob_ref[:, hs] = ov.astype(BF16)
                lse_row = m_sc[hh] * scale + jnp.log(l_sc[hh])
                lse_ref[:, hs] = jnp.broadcast_to(lse_row, (LANES, tb)).T

    qspec = pl.BlockSpec((tb, HP * LANES), lambda b, h, p, it_, jt_: (b * n + it_[p], h))
    kspec = pl.BlockSpec((tb, HP * LANES), lambda b, h, p, it_, jt_: (b * n + jt_[p], h))
    krspec = pl.BlockSpec((tb, LANES), lambda b, h, p, it_, jt_: (b * n + jt_[p], 0))
    return pl.pallas_call(
        body, name=name,
        grid_spec=pltpu.PrefetchScalarGridSpec(
            num_scalar_prefetch=2, grid=(Bn, nh // HP, len(pairs)),
            in_specs=[qspec, qspec, kspec, krspec, kspec], out_specs=[qspec, qspec, qspec],
            scratch_shapes=[pltpu.VMEM((HP, 1, tb), F32), pltpu.VMEM((HP, 1, tb), F32),
                            pltpu.VMEM((HP, LANES, tb), F32)]),
        out_shape=[jax.ShapeDtypeStruct((T, W), F32), jax.ShapeDtypeStruct((T, W), BF16),
                   jax.ShapeDtypeStruct((T, W), F32)],
        compiler_params=_cparams(),
    )(it, jt, qn, qr, kn, kr, v)


def _attention_dq(qn, qr, kn, kr, v, o, do, lse, S, scale, name):
    T, W = qn.shape
    nh = W // LANES
    Bn = T // S
    tb, n, C = _att_geometry(S)
    pairs = [(i, j) for i in range(n) for j in range(i + 1)]
    it = jnp.asarray([p[0] for p in pairs], jnp.int32)
    jt = jnp.asarray([p[1] for p in pairs], jnp.int32)
    c2 = scale * LOG2E
    HP = _heads_per_step(nh)

    def body(it_ref, jt_ref, qn_ref, qr_ref, kn_ref, kr_ref, v_ref, o_ref, do_ref, lse_ref,
             dqn_ref, dqr_ref, acc, di_sc, lse_sc):
        p = pl.program_id(2)
        i = it_ref[p]
        j = jt_ref[p]

        @pl.when(j == 0)
        def _():
            acc[...] = jnp.zeros_like(acc)
            for hh in range(HP):
                hs = pl.ds(hh * LANES, LANES)
                di_sc[hh] = jnp.sum(do_ref[:, hs] * o_ref[:, hs], axis=1, keepdims=True)
                lse_sc[hh] = jnp.max(lse_ref[:, hs], axis=1, keepdims=True) * LOG2E

        def step(diag):
            for hh in range(HP):
                hs = pl.ds(hh * LANES, LANES)
                qc = jnp.concatenate([qn_ref[:, hs], qr_ref[:, hs]], axis=1)
                for c in range(C):
                    r0 = c * ATT_SUB if diag else 0
                    rows = pl.ds(r0, tb - r0)
                    ks = pl.ds(c * ATT_SUB, ATT_SUB)
                    kc = jnp.concatenate([kn_ref[ks, hs], kr_ref[ks, :]], axis=1)
                    s = _att_scores(qc[r0:], kc, diag, False, 0)
                    pr = jnp.exp2(s * c2 - lse_sc[hh, rows, :])
                    dp = _bdot(do_ref[rows, hs], v_ref[ks, hs], NT)
                    ds = pr * (dp - di_sc[hh, rows, :]) * scale
                    acc[hh, rows, :] += _bdot(ds, kc, NN)

        @pl.when(j < i)
        def _():
            step(False)

        @pl.when(j == i)
        def _():
            step(True)
            for hh in range(HP):
                hs = pl.ds(hh * LANES, LANES)
                dqn_ref[:, hs] = acc[hh, :, 0:LANES].astype(BF16)
                dqr_ref[:, hs] = acc[hh, :, LANES:2 * LANES]

    qspec = pl.BlockSpec((tb, HP * LANES), lambda b, h, p, it_, jt_: (b * n + it_[p], h))
    kspec = pl.BlockSpec((tb, HP * LANES), lambda b, h, p, it_, jt_: (b * n + jt_[p], h))
    krspec = pl.BlockSpec((tb, LANES), lambda b, h, p, it_, jt_: (b * n + jt_[p], 0))
    return pl.pallas_call(
        body, name=name,
        grid_spec=pltpu.PrefetchScalarGridSpec(
            num_scalar_prefetch=2, grid=(Bn, nh // HP, len(pairs)),
            in_specs=[qspec, qspec, kspec, krspec, kspec, qspec, qspec, qspec], out_specs=[qspec, qspec],
            scratch_shapes=[pltpu.VMEM((HP, tb, 2 * LANES), F32), pltpu.VMEM((HP, tb, 1), F32),
                            pltpu.VMEM((HP, tb, 1), F32)]),
        out_shape=[jax.ShapeDtypeStruct((T, W), BF16), jax.ShapeDtypeStruct((T, W), F32)],
        compiler_params=_cparams(),
    )(it, jt, qn, qr, kn, kr, v, o, do, lse)


def _attention_dkv(qn, qr, kn, kr, v, o, do, lse, S, scale, name):
    T, W = qn.shape
    nh = W // LANES
    Bn = T // S
    tb, n, _ = _att_geometry(S)
    sub = min(ATT_SUB_KV, tb)
    C = tb // sub
    HP = _heads_per_step(nh)
    triples = [(j, h, i) for j in range(n) for h in range(nh // HP) for i in range(j, n)]
    jt = jnp.asarray([t[0] for t in triples], jnp.int32)
    ht = jnp.asarray([t[1] for t in triples], jnp.int32)
    it = jnp.asarray([t[2] for t in triples], jnp.int32)
    c2 = scale * LOG2E

    def as_row(col):
        return jnp.broadcast_to(col, (sub, LANES)).T[0:1, :]

    def body(jt_ref, ht_ref, it_ref, qn_ref, qr_ref, kn_ref, kr_ref, v_ref, o_ref, do_ref, lse_ref,
             dkn_ref, dv_ref, dkr_ref, akc, av):
        p = pl.program_id(1)
        j = jt_ref[p]
        h = ht_ref[p]
        i = it_ref[p]

        @pl.when(jnp.logical_and(h == 0, i == j))
        def _():
            dkr_ref[...] = jnp.zeros_like(dkr_ref)

        @pl.when(i == j)
        def _():
            akc[...] = jnp.zeros_like(akc)
            av[...] = jnp.zeros_like(av)

        def step(diag):
            for hh in range(HP):
                hs = pl.ds(hh * LANES, LANES)
                kc = jnp.concatenate([kn_ref[:, hs], kr_ref[...]], axis=1)
                for c in range(C):
                    rows = pl.ds(c * sub, sub)
                    k1 = (c + 1) * sub if diag else tb
                    ks = pl.ds(0, k1)
                    qc = jnp.concatenate([qn_ref[rows, hs], qr_ref[rows, hs]], axis=1)
                    st = _att_scores(kc[:k1], qc, diag, True, c * sub)
                    dov = do_ref[rows, hs]
                    lse_row = as_row(jnp.max(lse_ref[rows, hs], axis=1, keepdims=True)) * LOG2E
                    di_row = as_row(jnp.sum(dov * o_ref[rows, hs], axis=1, keepdims=True))
                    pt = jnp.exp2(st * c2 - lse_row)
                    dst = pt * (_bdot(v_ref[ks, hs], dov, NT) - di_row) * scale
                    av[ks, hs] += _bdot(pt, dov, NN)
                    akc[hh, ks, :] += _bdot(dst, qc, NN)

        @pl.when(i > j)
        def _():
            step(False)

        @pl.when(i == j)
        def _():
            step(True)

        @pl.when(i == n - 1)
        def _():
            for hh in range(HP):
                dkn_ref[:, pl.ds(hh * LANES, LANES)] = akc[hh, :, 0:LANES].astype(BF16)
                dkr_ref[...] += akc[hh, :, LANES:2 * LANES]
            dv_ref[...] = av[...].astype(BF16)

    qspec = pl.BlockSpec((tb, HP * LANES), lambda b, p, jt_, ht_, it_: (b * n + it_[p], ht_[p]))
    kspec = pl.BlockSpec((tb, HP * LANES), lambda b, p, jt_, ht_, it_: (b * n + jt_[p], ht_[p]))
    krspec = pl.BlockSpec((tb, LANES), lambda b, p, jt_, ht_, it_: (b * n + jt_[p], 0))
    return pl.pallas_call(
        body, name=name,
        grid_spec=pltpu.PrefetchScalarGridSpec(
            num_scalar_prefetch=3, grid=(Bn, len(triples)),
            in_specs=[qspec, qspec, kspec, krspec, kspec, qspec, qspec, qspec],
            out_specs=[kspec, kspec, krspec],
            scratch_shapes=[pltpu.VMEM((HP, tb, 2 * LANES), F32), pltpu.VMEM((tb, HP * LANES), F32)]),
        out_shape=[jax.ShapeDtypeStruct((T, W), BF16), jax.ShapeDtypeStruct((T, W), BF16),
                   jax.ShapeDtypeStruct((T, LANES), F32)],
        compiler_params=_cparams(),
    )(jt, ht, it, qn, qr, kn, kr, v, o, do, lse)


def _pad_cols(w, n):
    return jnp.pad(w, ((0, 0), (0, n - w.shape[1])))


def _local_step(x, positions, w, target, late=None, reduce_early=None, reduce_last=None):
    w = dict(w)
    Bn, S, D = x.shape
    T = Bn * S
    depth = w['ln_mix_g'].shape[0]
    alpha = (2 * depth) ** 0.25
    d_inner = w['ssd_norm_g'].shape[1]
    n_heads_ssd = d_inner // SSD_HEAD_DIM
    G = SSD_N_GROUPS
    gn = G * SSD_D_STATE
    conv_dim = d_inner + 2 * gn
    hpg = n_heads_ssd // G
    nh = D // MLA_NOPE
    kv_rank = w['kv_norm_g'].shape[0]
    q_rank = w['q_norm_g'].shape[1]
    H = w['ffn_conv_b'].shape[1] // 2
    scale = (MLA_NOPE + MLA_ROPE) ** -0.5
    assert S % SSD_CHUNK == 0 and hpg % 2 == 0 and SSD_D_STATE == LANES and n_heads_ssd <= LANES

    X = x.reshape(T, D)
    tgt = target.reshape(T, D)

    inv_freq = 1.0 / (ROPE_THETA ** (jnp.arange(0, MLA_ROPE, 2, dtype=F32) / MLA_ROPE))
    ang = positions.reshape(T).astype(F32)[:, None] * inv_freq
    cos, sin = jnp.cos(ang), jnp.sin(ang)
    zpad = jnp.zeros((T, LANES - MLA_ROPE), F32)
    cc = jnp.concatenate([cos, cos, zpad], axis=1)
    ss = jnp.concatenate([-sin, sin, zpad], axis=1)

    w_in = w['ssd_in_proj'][0]
    w_z = w_in[:, :d_inner]
    w_xbc = w_in[:, d_inner:d_inner + conv_dim]
    w_dt = _pad_cols(w_in[:, d_inner + conv_dim:], LANES)
    ssd_cw = w['ssd_conv_w'][0]
    ssd_cb = w['ssd_conv_b']
    dt_bias = _pad_cols(w['ssd_dt_bias'], LANES)
    a_log = _pad_cols(w['ssd_A_log'], LANES)
    d_lane = jnp.repeat(w['ssd_D'], SSD_HEAD_DIM, axis=1)
    ssd_ng = w['ssd_norm_g']

    Xb = X.astype(BF16)
    z = _mm(Xb, w_z, name='ssd_in_z')
    xbc_pre = _mm(Xb, w_xbc, name='ssd_in_xbc')
    dt_pre = _mm(Xb, w_dt, name='ssd_in_dt')
    xbc, xbc_conv = _ssd_conv_fwd(xbc_pre, ssd_cw, ssd_cb, S, name='ssd_conv')
    dtb, cumb = _ssd_dt_prep(dt_pre, dt_bias, a_log, n_heads_ssd, name='ssd_dt_prep')
    y_scan, states, gathered = _scan_fwd(xbc, dtb, cumb, S, d_inner, name='ssd_scan',
                                         gather=late[0] if late else ())
    if late:
        w.update(late[1](gathered))

    w_out = w['ssd_out_proj'][0]
    w_kvc = w['kv_down_proj'][:, :kv_rank]
    w_kvr = _pad_cols(w['kv_down_proj'][:, kv_rank:], LANES)
    kv_g = w['kv_norm_g'].reshape(1, kv_rank)
    w_uk = w['kv_up_k']
    w_uv = w['kv_up_v']
    w_qd = w['q_down_proj'][0]
    q_g = w['q_norm_g']
    qu = w['q_up_proj'][0].reshape(q_rank, nh, MLA_NOPE + MLA_ROPE)
    w_qn = qu[:, :, :MLA_NOPE].reshape(q_rank, nh * LANES)
    w_qr = jnp.pad(qu[:, :, MLA_NOPE:], ((0, 0), (0, 0), (0, LANES - MLA_ROPE))).reshape(q_rank, nh * LANES)
    w_ao = w['attn_out_proj'][0]

    def ffn_parts(i):
        return dict(wu=w['ffn_up'][i], cw=w['ffn_conv_w'][i], cb=w['ffn_conv_b'][i:i + 1], wd=w['ffn_down'][i])

    def ln(name, i):
        return w[name][i:i + 1]

    def ffn_fwd(hb, i):
        f = ffn_parts(i)
        u = _mm(hb, f['wu'], name=f'ffn{i}_up')
        a = _ffn_act_fwd(u, f['cw'], f['cb'], S, name=f'ffn{i}_act')
        return _mm(a, f['wd'], name=f'ffn{i}_down'), (u, a)

    yn = _gate_fwd(y_scan, xbc, z, d_lane, ssd_ng, d_inner, name='ssd_gate')
    mix0 = _mm(yn, w_out, name='ssd_out')
    h1, s1, h1b = _ln_fwd(X, mix0, ln('ln_mix_g', 0), ln('ln_mix_b', 0), alpha, name='ln_mix0')
    ff0, ffn0_saved = ffn_fwd(h1b, 0)
    h2, s2, h2b = _ln_fwd(h1, ff0, ln('ln_ffn_g', 0), ln('ln_ffn_b', 0), alpha, name='ln_ffn0')

    ckv = _mm(h2b, w_kvc, name='kv_down_c')
    kr_pre = _mm(h2b, w_kvr, name='kv_down_r')
    ckvn = _rms_fwd(ckv, kv_g, name='kv_norm')
    kr = _rope(kr_pre, cc, ss, transpose=False, name='k_rope')
    kn = _mm(ckvn, w_uk, out_dtype=BF16, name='kv_up_k')
    v = _mm(ckvn, w_uv, out_dtype=BF16, name='kv_up_v')
    cq_pre = _mm(h2b, w_qd, name='q_down')
    cq = _rms_fwd(cq_pre, q_g, name='q_norm')
    qn = _mm(cq, w_qn, out_dtype=BF16, name='q_up_n')
    qr_pre = _mm(cq, w_qr, name='q_up_r')
    qr = _rope(qr_pre, cc, ss, transpose=False, name='q_rope')
    o, ob, lse = _attention_fwd(qn, qr, kn, kr, v, S, scale, name='attn_fwd')
    mix1 = _mm(ob, w_ao, name='attn_out')
    h3, s3, h3b = _ln_fwd(h2, mix1, ln('ln_mix_g', 1), ln('ln_mix_b', 1), alpha, name='ln_mix1')
    ff1, ffn1_saved = ffn_fwd(h3b, 1)
    h4, s4, _ = _ln_fwd(h3, ff1, ln('ln_ffn_g', 1), ln('ln_ffn_b', 1), alpha, name='ln_ffn1')
    sq, dh4 = _loss_head(h4, tgt, name='loss_head')

    g = {}

    def ffn_bwd(ds, dsb, hb_in, saved, i):
        f = ffn_parts(i)
        u, a = saved
        d_wd = _mm(a, dsb, ta=True, name=f'ffn{i}_down_dw')
        da = _mm(dsb, f['wd'], tb=True, out_dtype=BF16, name=f'ffn{i}_down_dx')
        dug, duv, dwg, dwv, dbg, dbv = _ffn_act_bwd(u, da, f['cw'], f['cb'], S, name=f'ffn{i}_act_bwd')
        d_wu = _mm(hb_in, dug, ta=True, name=f'ffn{i}_up_g_dw', into=lax.empty((D, 2 * H), F32))
        d_wu = _mm(hb_in, duv, ta=True, name=f'ffn{i}_up_v_dw', into=d_wu, into_col=H)
        dh = _mm(dug, f['wu'], tb=True, add=ds, add_scale=alpha, name=f'ffn{i}_up_g_dx')
        dh = _mm(duv, f['wu'], tb=True, add=dh, name=f'ffn{i}_up_v_dx', b_koff=H)
        return dh, d_wd, d_wu, jnp.concatenate([dwg, dwv], axis=1), jnp.concatenate([dbg, dbv], axis=1)

    ds4, ds4b, dg_f1, db_f1 = _ln_bwd(dh4, s4, ln('ln_ffn_g', 1), name='ln_ffn1_bwd')
    dh3, d_wd1, d_wu1, d_fcw1, d_fcb1 = ffn_bwd(ds4, ds4b, h3b, ffn1_saved, 1)
    ds3, ds3b, dg_m1, db_m1 = _ln_bwd(dh3, s3, ln('ln_mix_g', 1), name='ln_mix1_bwd')

    g['attn_out_proj'] = _mm(ob, ds3b, ta=True, name='attn_out_dw')[None]
    do = _mm(ds3b, w_ao, tb=True, name='attn_out_dx')
    dqn, dqr = _attention_dq(qn, qr, kn, kr, v, o, do, lse, S, scale, name='attn_bwd_dq')
    dkn, dv, dkr = _attention_dkv(qn, qr, kn, kr, v, o, do, lse, S, scale, name='attn_bwd_dkv')
    dqr_pre = _rope(dqr, cc, ss, transpose=True, name='q_rope_bwd')
    dkr_pre = _rope(dkr, cc, ss, transpose=True, name='k_rope_bwd')

    d_wqn = _mm(cq, dqn, ta=True, name='q_up_n_dw').reshape(q_rank, nh, LANES)
    d_wqr = _mm(cq, dqr_pre, ta=True, name='q_up_r_dw').reshape(q_rank, nh, LANES)[:, :, :MLA_ROPE]
    g['q_up_proj'] = jnp.concatenate([d_wqn, d_wqr], axis=2).reshape(1, q_rank, nh * (MLA_NOPE + MLA_ROPE))
    dcq = _mm(dqn, w_qn, tb=True, name='q_up_n_dx')
    dcq = _mm(dqr_pre, w_qr, tb=True, add=dcq, name='q_up_r_dx')
    dcq_pre, d_qg = _rms_bwd(dcq, cq_pre, q_g, name='q_norm_bwd')
    g['q_norm_g'] = d_qg
    g['q_down_proj'] = _mm(h2b, dcq_pre, ta=True, name='q_down_dw')[None]

    g['kv_up_k'] = _mm(ckvn, dkn, ta=True, name='kv_up_k_dw')
    g['kv_up_v'] = _mm(ckvn, dv, ta=True, name='kv_up_v_dw')
    dckvn = _mm(dkn, w_uk, tb=True, name='kv_up_k_dx')
    dckvn = _mm(dv, w_uv, tb=True, add=dckvn, name='kv_up_v_dx')
    dckv, d_kvg = _rms_bwd(dckvn, ckv, kv_g, name='kv_norm_bwd')
    g['kv_norm_g'] = d_kvg.reshape(kv_rank)
    g['kv_down_proj'] = jnp.concatenate(
        [_mm(h2b, dckv, ta=True, name='kv_down_c_dw'),
         _mm(h2b, dkr_pre, ta=True, name='kv_down_r_dw')[:, :MLA_ROPE]], axis=1)

    dh2 = _mm(dcq_pre, w_qd, tb=True, add=ds3, add_scale=alpha, name='q_down_dx')
    dh2 = _mm(dckv, w_kvc, tb=True, add=dh2, name='kv_down_c_dx')
    dh2 = _mm(dkr_pre, w_kvr, tb=True, add=dh2, name='kv_down_r_dx')

    ds2, ds2b, dg_f0, db_f0 = _ln_bwd(dh2, s2, ln('ln_ffn_g', 0), name='ln_ffn0_bwd')
    dh1, d_wd0, d_wu0, d_fcw0, d_fcb0 = ffn_bwd(ds2, ds2b, h1b, ffn0_saved, 0)
    ds1, ds1b, dg_m0, db_m0 = _ln_bwd(dh1, s1, ln('ln_mix_g', 0), name='ln_mix0_bwd')

    g['ssd_out_proj'] = _mm(yn, ds1b, ta=True, name='ssd_out_dw')[None]
    dyn = _mm(ds1b, w_out, tb=True, out_dtype=BF16, name='ssd_out_dx')
    g['ffn_up'] = jnp.stack([d_wu0, d_wu1])
    g['ffn_down'] = jnp.stack([d_wd0, d_wd1])
    early_blocks = reduce_early[0](g) if reduce_early else ()
    (dy1, dz, d_ng, d_dl), from_sibling = _gate_bwd(dyn, y_scan, xbc, z, d_lane, ssd_ng, d_inner,
                                                    name='ssd_gate_bwd', exchange=early_blocks)
    (dxs, dB, dC, ddt_g, da_g), early = _scan_bwd(
        xbc, dtb, cumb, states, dy1, d_lane, S, d_inner, name='ssd_scan_bwd',
        exchange=reduce_early[1](early_blocks, from_sibling) if reduce_early else ())
    ddt_pre, d_bias, d_alog = _dt_bwd(ddt_g, da_g, dt_pre, dt_bias, a_log, name='ssd_dt_bwd')
    dxbc_pre, d_scw, d_scb = _ssd_conv_bwd(xbc_pre, xbc_conv, [dxs, dB, dC], ssd_cw, S, name='ssd_conv_bwd')
    g['ssd_in_proj'] = jnp.concatenate(
        [_mm(Xb, dz, ta=True, name='ssd_in_z_dw'), _mm(Xb, dxbc_pre, ta=True, name='ssd_in_xbc_dw'),
         _mm(Xb, ddt_pre, ta=True, name='ssd_in_dt_dw')[:, :n_heads_ssd]], axis=1)[None]
    g['ssd_conv_w'] = d_scw[None]
    g['ssd_conv_b'] = d_scb
    g['ssd_norm_g'] = d_ng
    g['ffn_conv_w'] = jnp.stack([d_fcw0, d_fcw1])
    dx = _mm(dz, w_z, tb=True, add=ds1, add_scale=alpha, name='ssd_in_z_dx')
    dx = _mm(dxbc_pre, w_xbc, tb=True, add=dx, name='ssd_in_xbc_dx', exchange=reduce_last(g) if reduce_last else ())
    dx, last = dx if reduce_last else (dx, None)
    dx = _mm(ddt_pre, w_dt, tb=True, add=dx, name='ssd_in_dt_dx')

    g['ssd_dt_bias'] = d_bias[:, :n_heads_ssd]
    g['ssd_A_log'] = d_alog[:, :n_heads_ssd]
    g['ssd_D'] = jnp.sum(d_dl.reshape(1, n_heads_ssd, SSD_HEAD_DIM), axis=2)
    g['ffn_conv_b'] = jnp.concatenate([d_fcb0, d_fcb1], axis=0)
    g['ln_mix_g'] = jnp.concatenate([dg_m0, dg_m1], axis=0)
    g['ln_mix_b'] = jnp.concatenate([db_m0, db_m1], axis=0)
    g['ln_ffn_g'] = jnp.concatenate([dg_f0, dg_f1], axis=0)
    g['ln_ffn_b'] = jnp.concatenate([db_f0, db_f1], axis=0)
    return sq, dx.reshape(Bn, S, D), g, early, last


ANY = pl.BlockSpec(memory_space=pl.ANY)


def _all_gather(xs, name):
    n = len(xs)

    def body(*refs):
        start, forward, finish = _gather_plan(refs[:n], refs[n:2 * n], *refs[2 * n:])
        start()
        forward()
        finish()

    return pl.pallas_call(
        body, name=name, out_shape=_gather_shapes(xs), in_specs=[ANY] * n, out_specs=[ANY] * n,
        scratch_shapes=_gather_sems(n),
    )(*xs)


def _gather_shapes(xs):
    return [jax.ShapeDtypeStruct((N_DEV,) + a.shape, a.dtype) for a in xs]


def _gather_sems(n):
    return [pltpu.SemaphoreType.DMA((7 * n,)), pltpu.SemaphoreType.DMA((7 * n,)), pltpu.SemaphoreType.DMA((n,))]


def _gather_plan(x_refs, out_refs, send_sems, recv_sems, local_sems):
    n = len(x_refs)
    x, y, c = lax.axis_index("x"), lax.axis_index("y"), lax.axis_index("c")
    me, sibling = (x, y, c), (x, y, 1 - c)
    chips = [(1 - x, y), (x, 1 - y), (1 - x, 1 - y)]

    def rows(i, px, py, pc):
        return out_refs[i].at[4 * px + 2 * py + pc]

    def copy(i, k, block, to, own=False):
        return pltpu.make_async_remote_copy(
            src_ref=x_refs[i] if own else rows(i, *block), dst_ref=rows(i, *block),
            send_sem=send_sems.at[7 * i + k], recv_sem=recv_sems.at[7 * i + k],
            device_id=to, device_id_type=MESH)

    def mine():
        return [pltpu.make_async_copy(x_refs[i], rows(i, *me), local_sems.at[i]) for i in range(n)]

    def first():
        out = []
        for i in range(n):
            out.append(copy(i, 0, me, sibling, own=True))
            out += [copy(i, 1 + j, me, (*chip, c), own=True) for j, chip in enumerate(chips)]
        return out

    def passed():
        return [copy(i, 4 + j, (*chip, c), sibling) for j, chip in enumerate(chips) for i in range(n)]

    def start():
        for cp in mine() + first():
            cp.start()

    def forward():
        for j, chip in enumerate(chips):
            for i in range(n):
                copy(i, 1 + j, (*chip, c), me).wait_recv()
                copy(i, 4 + j, (*chip, c), sibling).start()

    def finish():
        for i in range(n):
            copy(i, 0, sibling, me).wait_recv()
            for j, chip in enumerate(chips):
                copy(i, 4 + j, (*chip, 1 - c), me).wait_recv()
        for cp in first() + passed():
            cp.wait_send()
        for cp in mine():
            cp.wait()

    return start, forward, finish


def _exchange_sibling(gs, name):
    n = len(gs)

    def body(*refs):
        start, finish = _sibling_plan(refs[:n], refs[n:2 * n], *refs[2 * n:])
        start()
        finish()

    return pl.pallas_call(
        body, name=name, out_shape=_sibling_shapes(gs), in_specs=[ANY] * n, out_specs=[ANY] * n,
        scratch_shapes=_sibling_sems(n),
    )(*gs)


def _sibling_shapes(gs):
    return [jax.ShapeDtypeStruct((4,) + g.shape[1:], g.dtype) for g in gs]


def _sibling_sems(n):
    return [pltpu.SemaphoreType.DMA((4 * n,)), pltpu.SemaphoreType.DMA((4 * n,))]


def _sibling_plan(g_refs, r_refs, send_sems, recv_sems):
    n = len(g_refs)
    x, y, c = lax.axis_index("x"), lax.axis_index("y"), lax.axis_index("c")

    def copies():
        return [pltpu.make_async_remote_copy(
            src_ref=g_refs[i].at[2 * k + (1 - c)], dst_ref=r_refs[i].at[k], send_sem=send_sems.at[4 * i + k],
            recv_sem=recv_sems.at[4 * i + k], device_id=(x, y, 1 - c), device_id_type=MESH)
            for i in range(n) for k in range(4)]

    def start():
        for cp in copies():
            cp.start()

    def finish():
        for cp in copies():
            cp.wait()

    return start, finish


def _chips_shapes(parts):
    return [jax.ShapeDtypeStruct((3,) + p.shape[1:], p.dtype) for p in parts]


def _chips_sems(n):
    return [pltpu.SemaphoreType.DMA((3 * n,)), pltpu.SemaphoreType.DMA((3 * n,))]


def _chips_plan(p_refs, r_refs, send_sems, recv_sems):
    n = len(p_refs)
    x, y, c = lax.axis_index("x"), lax.axis_index("y"), lax.axis_index("c")
    chips = [(1 - x, y), (x, 1 - y), (1 - x, 1 - y)]

    def copies():
        return [pltpu.make_async_remote_copy(
            src_ref=p_refs[i].at[2 * cx + cy], dst_ref=r_refs[i].at[j], send_sem=send_sems.at[3 * i + j],
            recv_sem=recv_sems.at[3 * i + j], device_id=(cx, cy, c), device_id_type=MESH)
            for i in range(n) for j, (cx, cy) in enumerate(chips)]

    def start():
        for cp in copies():
            cp.start()

    def finish():
        for cp in copies():
            cp.wait()

    return start, finish


def _row_tile(rows, cols):
    want = max(8, (256 * 1024) // cols)
    for t in range(min(rows, want) // 8 * 8, 7, -8):
        if rows % t == 0:
            return t
    return rows


def _add_sibling(gfull, r1, core, name):
    _, rows, lanes = gfull.shape
    tr = _row_tile(rows, lanes)

    def body(c_ref, g_ref, r_ref, o_ref):
        o_ref[...] = g_ref[...] + r_ref[...]

    return pl.pallas_call(
        body, name=name,
        grid_spec=pltpu.PrefetchScalarGridSpec(
            num_scalar_prefetch=1, grid=(4, rows // tr),
            in_specs=[pl.BlockSpec((1, tr, lanes), lambda k, r, c_ref: (2 * k + c_ref[0], r, 0)),
                      pl.BlockSpec((1, tr, lanes), lambda k, r, c_ref: (k, r, 0))],
            out_specs=pl.BlockSpec((1, tr, lanes), lambda k, r, c_ref: (k, r, 0))),
        out_shape=jax.ShapeDtypeStruct((4, rows, lanes), F32), compiler_params=_cparams(),
    )(core, gfull, r1)


def _adam_math(wv, gv, mv, vv):
    m = ADAM_B1 * mv + (1.0 - ADAM_B1) * gv
    v = ADAM_B2 * vv + (1.0 - ADAM_B2) * (gv * gv)
    m_hat = m / (1.0 - ADAM_B1 ** ADAM_STEP)
    v_hat = v / (1.0 - ADAM_B2 ** ADAM_STEP)
    delta = -ADAM_LR * (m_hat / (jnp.sqrt(v_hat) + ADAM_EPS) + ADAM_WD * wv)
    return delta, m, v


def _adam_sharded(part, r2, chip, wts, m, v, name):
    rows, lanes = wts.shape
    tr = _row_tile(rows, lanes)

    def body(k_ref, p_ref, r_ref, w_ref, m_ref, v_ref, g_ref, d_ref, mo_ref, vo_ref):
        gv = p_ref[0] + r_ref[0] + r_ref[1] + r_ref[2]
        delta, mn, vn = _adam_math(w_ref[...], gv, m_ref[...], v_ref[...])
        g_ref[...] = gv
        d_ref[...] = delta
        mo_ref[...] = mn
        vo_ref[...] = vn

    flat = pl.BlockSpec((tr, lanes), lambda r, k_ref: (r, 0))
    sh = jax.ShapeDtypeStruct((rows, lanes), F32)
    return pl.pallas_call(
        body, name=name,
        grid_spec=pltpu.PrefetchScalarGridSpec(
            num_scalar_prefetch=1, grid=(rows // tr,),
            in_specs=[pl.BlockSpec((1, tr, lanes), lambda r, k_ref: (k_ref[0], r, 0)),
                      pl.BlockSpec((3, tr, lanes), lambda r, k_ref: (0, r, 0)), flat, flat, flat],
            out_specs=[flat, flat, flat, flat]),
        out_shape=[sh, sh, sh, sh], compiler_params=_cparams(),
    )(chip, part, r2, wts, m, v)


def _adam_replicated(gall, wts, m, v, name):
    rows, lanes = wts.shape

    def body(ga_ref, w_ref, m_ref, v_ref, g_ref, d_ref, mo_ref, vo_ref):
        gv = ga_ref[0]
        for k in range(1, N_DEV):
            gv = gv + ga_ref[k]
        delta, mn, vn = _adam_math(w_ref[...], gv, m_ref[...], v_ref[...])
        g_ref[...] = gv
        d_ref[...] = delta
        mo_ref[...] = mn
        vo_ref[...] = vn

    sh = jax.ShapeDtypeStruct((rows, lanes), F32)
    return pl.pallas_call(body, name=name, out_shape=[sh, sh, sh, sh], compiler_params=_cparams())(gall, wts, m, v)


def _pack(arrs, dtype, row_mult):
    flat = jnp.concatenate([a.reshape(-1).astype(dtype) for a in arrs])
    n = flat.shape[0]
    unit = LANES * row_mult
    total = -(-n // unit) * unit
    return jnp.pad(flat, (0, total - n)).reshape(total // LANES, LANES)


def _unpack(flat2d, shapes):
    flat = flat2d.reshape(-1)
    out, off = [], 0
    for shp in shapes:
        n = math.prod(shp)
        out.append(flat[off:off + n].reshape(shp))
        off += n
    return out


def _unpack_gathered(gathered, shapes, axes):
    flat = gathered.reshape(N_DEV, -1)
    out, off = [], 0
    for shp, ax in zip(shapes, axes):
        n = math.prod(shp)
        seg = flat[:, off:off + n].reshape((N_DEV,) + tuple(shp))
        out.append(jnp.concatenate([seg[i] for i in range(N_DEV)], axis=ax))
        off += n
    return out


def _pack_chunks(fulls, axes, row_mult):
    per_dev = []
    for full, ax in zip(fulls, axes):
        parts = jnp.stack(jnp.split(full, N_DEV, axis=ax))
        per_dev.append(parts.reshape(N_DEV, -1))
    flat = jnp.concatenate(per_dev, axis=1)
    n = flat.shape[1]
    unit = LANES * row_mult
    total = -(-n // unit) * unit
    return jnp.pad(flat, ((0, 0), (0, total - n))).reshape(N_DEV, total // LANES, LANES)


def kernel(x, positions, ssd_in_proj, ssd_conv_w, ssd_conv_b, ssd_dt_bias, ssd_A_log, ssd_D, ssd_norm_g, ssd_out_proj, kv_down_proj, kv_norm_g, kv_up_k, kv_up_v, q_down_proj, q_norm_g, q_up_proj, attn_out_proj, ffn_up, ffn_conv_w, ffn_conv_b, ffn_down, ln_mix_g, ln_mix_b, ln_ffn_g, ln_ffn_b, loss_target, m_ssd_in_proj, m_ssd_conv_w, m_ssd_conv_b, m_ssd_dt_bias, m_ssd_A_log, m_ssd_D, m_ssd_norm_g, m_ssd_out_proj, m_kv_down_proj, m_kv_norm_g, m_kv_up_k, m_kv_up_v, m_q_down_proj, m_q_norm_g, m_q_up_proj, m_attn_out_proj, m_ffn_up, m_ffn_conv_w, m_ffn_conv_b, m_ffn_down, m_ln_mix_g, m_ln_mix_b, m_ln_ffn_g, m_ln_ffn_b, v_ssd_in_proj, v_ssd_conv_w, v_ssd_conv_b, v_ssd_dt_bias, v_ssd_A_log, v_ssd_D, v_ssd_norm_g, v_ssd_out_proj, v_kv_down_proj, v_kv_norm_g, v_kv_up_k, v_kv_up_v, v_q_down_proj, v_q_norm_g, v_q_up_proj, v_attn_out_proj, v_ffn_up, v_ffn_conv_w, v_ffn_conv_b, v_ffn_down, v_ln_mix_g, v_ln_mix_b, v_ln_ffn_g, v_ln_ffn_b):
    given = dict(locals())
    wl = {n: given[n] for n in WEIGHTS}
    ml = {n: given['m_' + n] for n in WEIGHTS}
    vl = {n: given['v_' + n] for n in WEIGHTS}
    sharded_axis = dict(SHARDED)
    big = [n for n, _ in SHARDED if n not in SHARDED_F32]
    small = [n for n, _ in SHARDED if n in SHARDED_F32]

    def as2d(a):
        return a.reshape(-1, a.shape[-1])

    def to_full(gathered, n):
        shp, ax = wl[n].shape, sharded_axis[n]
        moved = jnp.moveaxis(gathered.reshape((N_DEV,) + shp), 0, ax)
        return moved.reshape(shp[:ax] + (N_DEV * shp[ax],) + shp[ax + 1:])

    def to_chunks(full_grad, n):
        shp, ax = wl[n].shape, sharded_axis[n]
        split = full_grad.reshape(shp[:ax] + (N_DEV, shp[ax]) + shp[ax + 1:])
        return jnp.moveaxis(split, ax, 0).reshape((N_DEV,) + as2d(wl[n]).shape)

    first = [n for n in big if n in GATHERED_FIRST]
    late = [n for n in big if n not in GATHERED_FIRST]
    full = {n: wl[n] for n in REPLICATED}
    gathered = _all_gather([as2d(wl[n]).astype(BF16) for n in first] + [_pack([wl[n] for n in small], F32, 8)],
                           name='gather_weights')
    for n, gat in zip(first, gathered[:-1]):
        full[n] = to_full(gat, n)
    full.update(zip(small, _unpack_gathered(gathered[-1], [wl[n].shape for n in small],
                                            [sharded_axis[n] for n in small])))

    cx, cy, cc = lax.axis_index("x"), lax.axis_index("y"), lax.axis_index("c")
    core = jnp.reshape(cc, (1,)).astype(jnp.int32)
    chip = jnp.reshape(2 * cx + cy, (1,)).astype(jnp.int32)
    early_names = [n for n in big if n not in REDUCED_LAST]
    last_names = [n for n in big if n in REDUCED_LAST]
    parts = {}

    def add_sibling(names, chunked, received):
        parts.update((n, _add_sibling(gf, r, core, name=f'grads_add_sibling_{n}'))
                     for n, gf, r in zip(names, chunked, received))
        return [parts[n] for n in names]

    def early_blocks(grads):
        return [to_chunks(grads[n], n) for n in early_names]

    def early_parts(chunked, received):
        return add_sibling(early_names, chunked, received)

    def reduce_last(grads):
        chunked = [to_chunks(grads[n], n) for n in last_names]
        chunked.append(_pack_chunks([grads[n] for n in small], [sharded_axis[n] for n in small], 8))
        return add_sibling(last_names + ['small'], chunked, _exchange_sibling(chunked, name='grads_to_sibling_last'))

    sq, grad_x, grads, r2_early, r2_last = _local_step(
        x, positions, full, loss_target,
        late=([as2d(wl[n]).astype(BF16) for n in late], lambda gats: {n: to_full(g, n) for n, g in zip(late, gats)}),
        reduce_early=(early_blocks, early_parts), reduce_last=reduce_last)
    loss = lax.psum(0.5 * jnp.sum(sq) / x.shape[-1], ("x", "y", "c"))

    r2 = dict(zip(early_names, r2_early))
    r2.update(zip(last_names + ['small'], r2_last))
    res = {}
    kinds = ('grad', 'delta', 'new_m', 'new_v')
    for n in big:
        outs = _adam_sharded(parts[n], r2[n], chip, as2d(wl[n]), as2d(ml[n]), as2d(vl[n]), name=f'adamw_{n}')
        for kind, a in zip(kinds, outs):
            res[kind, n] = a.reshape(wl[n].shape)
    outs = _adam_sharded(parts['small'], r2['small'], chip, _pack([wl[n] for n in small], F32, 8),
                         _pack([ml[n] for n in small], F32, 8), _pack([vl[n] for n in small], F32, 8),
                         name='adamw_small_sharded')
    for kind, packed in zip(kinds, outs):
        for n, a in zip(small, _unpack(packed, [wl[n].shape for n in small])):
            res[kind, n] = a

    rep = list(REPLICATED)
    rshapes = [wl[n].shape for n in rep]
    gall, = _all_gather([_pack([grads[n] for n in rep], F32, 8)], name='gather_replicated_grads')
    outs = _adam_replicated(gall, _pack([wl[n] for n in rep], F32, 8), _pack([ml[n] for n in rep], F32, 8),
                            _pack([vl[n] for n in rep], F32, 8), name='adamw_replicated')
    for kind, packed in zip(('grad', 'delta', 'new_m', 'new_v'), outs):
        for n, a in zip(rep, _unpack(packed, rshapes)):
            res[kind, n] = a

    return (loss, grad_x, *[res['grad', n] for n in WEIGHTS], *[res['delta', n] for n in WEIGHTS],
            *[res['new_m', n] for n in WEIGHTS], *[res['new_v', n] for n in WEIGHTS])
```

```python
import math

import jax
import jax.numpy as jnp
from jax import lax
from jax.experimental import pallas as pl
from jax.experimental.pallas import tpu as pltpu

F32 = jnp.float32
BF16 = jnp.bfloat16
MESH = pl.DeviceIdType.MESH

N_DEV = 8
LANES = 128
MM_TILES = (1024, 1408, 896, 512, 384, 256, 128)
MM_VMEM_BUDGET = 38 * 1024 * 1024
MM_VMEM_LIMIT = 56 * 1024 * 1024
VMEM_LIMIT = 32 * 1024 * 1024
LN_EPS = 1e-5
RMS_EPS = 1e-6
SSD_HEAD_DIM = 64
SSD_N_GROUPS = 8
SSD_D_STATE = 128
SSD_CHUNK = 128
MLA_NOPE = 128
MLA_ROPE = 64
ROPE_THETA = 10000.0
ADAM_LR = 0.001
ADAM_B1 = 0.9
ADAM_B2 = 0.999
ADAM_EPS = 1e-08
ADAM_WD = 0.01
ADAM_STEP = 10
NEG_BIG = -1e30

SHARDED = (('ssd_in_proj', 2), ('ssd_conv_w', 2), ('ssd_conv_b', 1), ('ssd_norm_g', 1), ('ssd_out_proj', 1),
           ('kv_down_proj', 0), ('kv_up_k', 1), ('kv_up_v', 1), ('q_down_proj', 1), ('q_up_proj', 2),
           ('attn_out_proj', 1), ('ffn_up', 2), ('ffn_conv_w', 2), ('ffn_down', 1))
SHARDED_F32 = ('ssd_conv_w', 'ssd_conv_b', 'ssd_norm_g', 'ffn_conv_w')
GATHERED_FIRST = ('ssd_in_proj',)
REDUCED_LAST = ('ssd_in_proj',)
REPLICATED = ('ssd_dt_bias', 'ssd_A_log', 'ssd_D', 'kv_norm_g', 'q_norm_g', 'ffn_conv_b',
              'ln_mix_g', 'ln_mix_b', 'ln_ffn_g', 'ln_ffn_b')
WEIGHTS = ('ssd_in_proj', 'ssd_conv_w', 'ssd_conv_b', 'ssd_dt_bias', 'ssd_A_log', 'ssd_D', 'ssd_norm_g',
           'ssd_out_proj', 'kv_down_proj', 'kv_norm_g', 'kv_up_k', 'kv_up_v', 'q_down_proj', 'q_norm_g',
           'q_up_proj', 'attn_out_proj', 'ffn_up', 'ffn_conv_w', 'ffn_conv_b', 'ffn_down', 'ln_mix_g',
           'ln_mix_b', 'ln_ffn_g', 'ln_ffn_b')


def _cparams(limit=None):
    return pltpu.CompilerParams(vmem_limit_bytes=VMEM_LIMIT if limit is None else limit)


def _tile(n, cands):
    for c in cands:
        if n % c == 0:
            return c
    return n


def _sigmoid(x):
    return 1.0 / (1.0 + jnp.exp(-x))


def _iota(shape, axis):
    return lax.broadcasted_iota(jnp.int32, shape, axis)


def _mm(a, b, *, ta=False, tb=False, add=None, add_scale=1.0, out_dtype=F32, name, b_koff=0, into=None, into_col=0,
        exchange=()):
    if ta:
        K, M = a.shape
    else:
        M, K = a.shape
    if tb:
        N, Kb = b.shape
    else:
        Kb, N = b.shape
    assert b_koff + K <= Kb, (a.shape, b.shape, ta, tb, b_koff)
    tm = _tile(M, MM_TILES)
    tn = _tile(N, MM_TILES)
    tk = K if K <= MM_TILES[0] and b_koff == 0 and Kb == K else _tile(K, MM_TILES)
    nk = K // tk
    assert b_koff % tk == 0 and (into is None or (into.shape[0] == M and into.dtype == out_dtype))
    kb = b_koff // tk

    def vmem_estimate(tm_, tn_):
        ins = 2 * (tm_ * tk * a.dtype.itemsize + tk * tn_ * b.dtype.itemsize)
        outs = tm_ * tn_ * (2 * jnp.dtype(out_dtype).itemsize + 4 + (4 if nk > 1 else 0))
        return ins + outs + (2 * tm_ * tn_ * add.dtype.itemsize if add is not None else 0)

    while vmem_estimate(tm, tn) > MM_VMEM_BUDGET:
        can_m, can_n = tm % (2 * LANES) == 0, tn % (2 * LANES) == 0
        if can_m and (tm >= tn or not can_n):
            tm //= 2
        elif can_n:
            tn //= 2
        else:
            break

    assert into_col % tn == 0
    jb = into_col // tn

    nx = len(exchange)
    n_in = (add is not None) + (into is not None)
    grid = (M // tm, N // tn, nk)

    def body(a_ref, b_ref, *rest):
        add_ref = rest[0] if add is not None else None
        o_ref = rest[n_in + nx]
        acc = rest[n_in + 2 * nx + 1] if nk > 1 else None
        k = pl.program_id(2)
        if nx:
            start, finish_exchange = _chips_plan(rest[n_in:n_in + nx], rest[n_in + nx + 1:n_in + 2 * nx + 1],
                                                 *rest[len(rest) - 2:])
            t, total = _grid_step(grid)
            pl.when(t == 0)(start)

        if ta:
            av = a_ref[...].astype(BF16).T
        else:
            av = a_ref[...].astype(BF16)
        bv = b_ref[...].astype(BF16)
        dn = (((1,), (1 if tb else 0,)), ((), ()))
        part = lax.dot_general(av, bv, dn, preferred_element_type=F32)

        def finish(r):
            if add is not None:
                r = r + add_scale * add_ref[...].astype(F32)
            o_ref[...] = r.astype(out_dtype)

        if nk == 1:
            finish(part)
        else:
            @pl.when(k == 0)
            def _():
                acc[...] = part

            @pl.when(k > 0)
            def _():
                acc[...] += part

            @pl.when(k == nk - 1)
            def _():
                finish(acc[...])

        if nx:
            pl.when(t == total - 1)(finish_exchange)

    a_spec = (pl.BlockSpec((tk, tm), lambda i, j, k: (k, i)) if ta
              else pl.BlockSpec((tm, tk), lambda i, j, k: (i, k)))
    b_spec = (pl.BlockSpec((tn, tk), lambda i, j, k: (j, k + kb)) if tb
              else pl.BlockSpec((tk, tn), lambda i, j, k: (k + kb, j)))
    in_specs = [a_spec, b_spec]
    args = [a, b]
    if add is not None:
        in_specs.append(pl.BlockSpec((tm, tn), lambda i, j, k: (i, j)))
        args.append(add)
    aliases = {}
    if into is not None:
        aliases = {len(args): 0}
        in_specs.append(pl.BlockSpec(memory_space=pl.ANY))
        args.append(into)
    any_spec = pl.BlockSpec(memory_space=pl.ANY)
    outs = pl.pallas_call(
        body, name=name, grid=grid,
        in_specs=in_specs + [any_spec] * nx,
        out_specs=[pl.BlockSpec((tm, tn), lambda i, j, k: (i, j + jb))] + [any_spec] * nx,
        out_shape=[jax.ShapeDtypeStruct((M, N) if into is None else into.shape, out_dtype)] + _chips_shapes(exchange),
        scratch_shapes=([pltpu.VMEM((tm, tn), F32)] if nk > 1 else []) + (_chips_sems(nx) if nx else []),
        input_output_aliases=aliases, compiler_params=_cparams(MM_VMEM_LIMIT),
    )(*args, *exchange)
    return (outs[0], list(outs[1:])) if nx else outs[0]


def _ln_fwd(h, mix, g, b, alpha, name):
    T, D = h.shape
    tm = _tile(T, (512, 256, 128))

    def body(h_ref, m_ref, g_ref, b_ref, y_ref, s_ref, yb_ref):
        s = alpha * h_ref[...] + m_ref[...]
        mu = jnp.mean(s, axis=1, keepdims=True)
        xc = s - mu
        var = jnp.mean(xc * xc, axis=1, keepdims=True)
        y = xc * lax.rsqrt(var + LN_EPS) * g_ref[...] + b_ref[...]
        y_ref[...] = y
        s_ref[...] = s
        yb_ref[...] = y.astype(BF16)

    row = pl.BlockSpec((tm, D), lambda i: (i, 0))
    vec = pl.BlockSpec((1, D), lambda i: (0, 0))
    return pl.pallas_call(
        body, name=name, grid=(T // tm,), in_specs=[row, row, vec, vec], out_specs=[row, row, row],
        out_shape=[jax.ShapeDtypeStruct((T, D), F32)] * 2 + [jax.ShapeDtypeStruct((T, D), BF16)],
        compiler_params=_cparams(),
    )(h, mix, g, b)


def _ln_bwd(dy, s, g, name):
    T, D = s.shape
    tm = _tile(T, (512, 256, 128))

    def body(dy_ref, s_ref, g_ref, ds_ref, dsb_ref, dg_ref, db_ref):
        @pl.when(pl.program_id(0) == 0)
        def _():
            dg_ref[...] = jnp.zeros_like(dg_ref)
            db_ref[...] = jnp.zeros_like(db_ref)

        sv = s_ref[...]
        dyv = dy_ref[...]
        mu = jnp.mean(sv, axis=1, keepdims=True)
        xc = sv - mu
        rstd = lax.rsqrt(jnp.mean(xc * xc, axis=1, keepdims=True) + LN_EPS)
        xhat = xc * rstd
        dxh = dyv * g_ref[...]
        m1 = jnp.mean(dxh, axis=1, keepdims=True)
        m2 = jnp.mean(dxh * xhat, axis=1, keepdims=True)
        ds = rstd * (dxh - m1 - xhat * m2)
        ds_ref[...] = ds
        dsb_ref[...] = ds.astype(BF16)
        dg_ref[...] += jnp.sum(dyv * xhat, axis=0, keepdims=True)
        db_ref[...] += jnp.sum(dyv, axis=0, keepdims=True)

    row = pl.BlockSpec((tm, D), lambda i: (i, 0))
    vec = pl.BlockSpec((1, D), lambda i: (0, 0))
    return pl.pallas_call(
        body, name=name, grid=(T // tm,), in_specs=[row, row, vec], out_specs=[row, row, vec, vec],
        out_shape=[jax.ShapeDtypeStruct((T, D), F32), jax.ShapeDtypeStruct((T, D), BF16),
                   jax.ShapeDtypeStruct((1, D), F32), jax.ShapeDtypeStruct((1, D), F32)],
        compiler_params=_cparams(),
    )(dy, s, g)


def _loss_head(y, target, name):
    T, D = y.shape
    tm = _tile(T, (512, 256, 128))

    def body(y_ref, t_ref, sq_ref, dy_ref):
        @pl.when(pl.program_id(0) == 0)
        def _():
            sq_ref[...] = jnp.zeros_like(sq_ref)

        e = y_ref[...] - t_ref[...]
        sq_ref[...] += jnp.sum(e * e, axis=0, keepdims=True)
        dy_ref[...] = e * (1.0 / D)

    row = pl.BlockSpec((tm, D), lambda i: (i, 0))
    vec = pl.BlockSpec((1, D), lambda i: (0, 0))
    return pl.pallas_call(
        body, name=name, grid=(T // tm,), in_specs=[row, row], out_specs=[vec, row],
        out_shape=[jax.ShapeDtypeStruct((1, D), F32), jax.ShapeDtypeStruct((T, D), F32)],
        compiler_params=_cparams(),
    )(y, target)


def _shift_down(x, j):
    if j == 0:
        return x
    return jnp.where(_iota(x.shape, 0) >= j, pltpu.roll(x, j, 0), 0.0)


def _shift_up(x, j):
    if j == 0:
        return x
    n = x.shape[0]
    return jnp.where(_iota(x.shape, 0) < n - j, pltpu.roll(x, n - j, 0), 0.0)


def _conv_val(x, w_ref, b_ref):
    width = w_ref.shape[0]
    y = b_ref[...] + w_ref[width - 1:width, :] * x
    for j in range(1, width):
        y = y + w_ref[width - 1 - j:width - j, :] * _shift_down(x, j)
    return y


def _conv_bwd_val(x, dc, w_ref, dw_ref, db_ref):
    width = w_ref.shape[0]
    dx = w_ref[width - 1:width, :] * dc
    dw_ref[width - 1:width, :] += jnp.sum(dc * x, axis=0, keepdims=True)
    for j in range(1, width):
        sj = _shift_up(dc, j)
        dx = dx + w_ref[width - 1 - j:width - j, :] * sj
        dw_ref[width - 1 - j:width - j, :] += jnp.sum(sj * x, axis=0, keepdims=True)
    db_ref[...] += jnp.sum(dc, axis=0, keepdims=True)
    return dx


def _ffn_specs(H, S, width, cb):
    cb = _tile(H, (cb, LANES))
    nC = H // cb
    return dict(
        g=pl.BlockSpec((S, cb), lambda j, b: (b, j)), v=pl.BlockSpec((S, cb), lambda j, b: (b, j + nC)),
        wg=pl.BlockSpec((width, cb), lambda j, b: (0, j)), wv=pl.BlockSpec((width, cb), lambda j, b: (0, j + nC)),
        bg=pl.BlockSpec((1, cb), lambda j, b: (0, j)), bv=pl.BlockSpec((1, cb), lambda j, b: (0, j + nC))), nC


def _ffn_act_fwd(u, cw, cb, S, name):
    T, H2 = u.shape
    H = H2 // 2
    sp, nC = _ffn_specs(H, S, cw.shape[0], 2 * LANES)

    def body(ug_ref, uv_ref, wg_ref, wv_ref, bg_ref, bv_ref, a_ref):
        g = _conv_val(ug_ref[...], wg_ref, bg_ref)
        v = _conv_val(uv_ref[...], wv_ref, bv_ref)
        a_ref[...] = (g * _sigmoid(g) * v).astype(BF16)

    return pl.pallas_call(
        body, name=name, grid=(nC, T // S),
        in_specs=[sp['g'], sp['v'], sp['wg'], sp['wv'], sp['bg'], sp['bv']], out_specs=sp['g'],
        out_shape=jax.ShapeDtypeStruct((T, H), BF16), compiler_params=_cparams(),
    )(u, u, cw, cw, cb, cb)


def _ffn_act_bwd(u, da, cw, cb, S, name):
    T, H2 = u.shape
    H = H2 // 2
    width = cw.shape[0]
    sp, nC = _ffn_specs(H, S, width, LANES)

    def body(ug_ref, uv_ref, da_ref, wg_ref, wv_ref, bg_ref, bv_ref,
             dug_ref, duv_ref, dwg_ref, dwv_ref, dbg_ref, dbv_ref):
        @pl.when(pl.program_id(1) == 0)
        def _():
            for r in (dwg_ref, dwv_ref, dbg_ref, dbv_ref):
                r[...] = jnp.zeros_like(r)

        def conv_bwd(x, dc, w_ref, dw_ref, db_ref):
            dx = w_ref[width - 1:width, :] * dc
            for j in range(1, width):
                dx = dx + w_ref[width - 1 - j:width - j, :] * _shift_up(dc, j)
            for j in range(width):
                dw_ref[width - 1 - j:width - j, :] += jnp.sum(dc * _shift_down(x, j), axis=0, keepdims=True)
            db_ref[...] += jnp.sum(dc, axis=0, keepdims=True)
            return dx

        xg = ug_ref[...]
        xv = uv_ref[...]
        g = _conv_val(xg, wg_ref, bg_ref)
        v = _conv_val(xv, wv_ref, bv_ref)
        dav = da_ref[...].astype(F32)
        sg = _sigmoid(g)
        dg = dav * v * sg * (1.0 + g * (1.0 - sg))
        dv = dav * g * sg
        dug_ref[...] = conv_bwd(xg, dg, wg_ref, dwg_ref, dbg_ref).astype(BF16)
        duv_ref[...] = conv_bwd(xv, dv, wv_ref, dwv_ref, dbv_ref).astype(BF16)

    act = jax.ShapeDtypeStruct((T, H), BF16)
    wsh = jax.ShapeDtypeStruct((width, H), F32)
    bsh = jax.ShapeDtypeStruct((1, H), F32)
    return pl.pallas_call(
        body, name=name, grid=(nC, T // S),
        in_specs=[sp['g'], sp['v'], sp['g'], sp['wg'], sp['wv'], sp['bg'], sp['bv']],
        out_specs=[sp['g'], sp['g'], sp['wg'], sp['wg'], sp['bg'], sp['bg']],
        out_shape=[act, act, wsh, wsh, bsh, bsh], compiler_params=_cparams(),
    )(u, u, da, cw, cw, cb, cb)


def _ssd_conv_fwd(x, cw, cb, S, name):
    T, C = x.shape
    Cb = _tile(C, (256, 128))
    width = cw.shape[0]

    def body(x_ref, w_ref, b_ref, y_ref, c_ref):
        c = _conv_val(x_ref[...], w_ref, b_ref)
        c_ref[...] = c
        y_ref[...] = c * _sigmoid(c)

    blk = pl.BlockSpec((S, Cb), lambda j, b: (b, j))
    wblk = pl.BlockSpec((width, Cb), lambda j, b: (0, j))
    bblk = pl.BlockSpec((1, Cb), lambda j, b: (0, j))
    return pl.pallas_call(
        body, name=name, grid=(C // Cb, T // S), in_specs=[blk, wblk, bblk], out_specs=[blk, blk],
        out_shape=[jax.ShapeDtypeStruct((T, C), F32)] * 2, compiler_params=_cparams(),
    )(x, cw, cb)


def _ssd_conv_bwd(x, c, dys, cw, S, name):
    T, C = x.shape
    Cb = _tile(C, (256, 128))
    width = cw.shape[0]
    assert all(d.shape[1] % Cb == 0 for d in dys) and sum(d.shape[1] for d in dys) == C

    def part(dy, j0, dx_prev, k):
        n = dy.shape[1] // Cb

        def body(x_ref, c_ref, dy_ref, w_ref, prev_ref, dx_ref, dw_ref, db_ref):
            @pl.when(pl.program_id(1) == 0)
            def _():
                dw_ref[...] = jnp.zeros_like(dw_ref)
                db_ref[...] = jnp.zeros_like(db_ref)

            cval = c_ref[...]
            sc = _sigmoid(cval)
            dc = dy_ref[...] * sc * (1.0 + cval * (1.0 - sc))
            dx_ref[...] = _conv_bwd_val(x_ref[...], dc, w_ref, dw_ref, db_ref).astype(BF16)

        wide = pl.BlockSpec((S, Cb), lambda j, b: (b, j + j0))
        return pl.pallas_call(
            body, name=f'{name}_{k}', grid=(n, T // S),
            in_specs=[wide, wide, pl.BlockSpec((S, Cb), lambda j, b: (b, j)),
                      pl.BlockSpec((width, Cb), lambda j, b: (0, j + j0)), pl.BlockSpec(memory_space=pl.ANY)],
            out_specs=[wide, pl.BlockSpec((width, Cb), lambda j, b: (0, j)), pl.BlockSpec((1, Cb), lambda j, b: (0, j))],
            out_shape=[jax.ShapeDtypeStruct((T, C), BF16), jax.ShapeDtypeStruct((width, n * Cb), F32),
                       jax.ShapeDtypeStruct((1, n * Cb), F32)],
            input_output_aliases={4: 0}, compiler_params=_cparams(),
        )(x, c, dy, cw, dx_prev)

    dx = lax.empty((T, C), BF16)
    dws, dbs, j0 = [], [], 0
    for k, dy in enumerate(dys):
        dx, dw, db = part(dy, j0, dx, k)
        dws.append(dw)
        dbs.append(db)
        j0 += dy.shape[1] // Cb
    return dx, jnp.concatenate(dws, axis=1), jnp.concatenate(dbs, axis=1)


def _softplus(x):
    e = jnp.exp(-jnp.abs(x))
    return jnp.maximum(x, 0.0) + jnp.where(e < 1e-4, e * (1.0 - 0.5 * e), jnp.log(1.0 + e))


def _cumsum_rows(x):
    n = x.shape[0]
    row = _iota(x.shape, 0)
    k = 1
    while k < n:
        x = x + jnp.where(row >= k, pltpu.roll(x, k, 0), 0.0)
        k *= 2
    return x


def _rev_cumsum_rows(x):
    n = x.shape[0]
    row = _iota(x.shape, 0)
    k = 1
    while k < n:
        x = x + jnp.where(row < n - k, pltpu.roll(x, n - k, 0), 0.0)
        k *= 2
    return x


def _col(x, lane_ids, h):
    return jnp.sum(jnp.where(lane_ids == h, x, 0.0), axis=1, keepdims=True)


def _bdot(a, b, dims):
    return lax.dot_general(a.astype(BF16), b.astype(BF16), (dims, ((), ())), preferred_element_type=F32)


NN = ((1,), (0,))
NT = ((1,), (1,))


def _ssd_dt_prep(dt_pre, dt_bias, a_log, n_heads, name):
    T = dt_pre.shape[0]
    L = SSD_CHUNK

    def body(dtp_ref, bias_ref, alog_ref, dtb_ref, cumb_ref):
        dt = _softplus(dtp_ref[...] + bias_ref[...])
        cum = _cumsum_rows(dt * (-jnp.exp(alog_ref[...])))
        lane = _iota((L, LANES), 1)
        for h in range(n_heads):
            hcols = pl.ds(h * LANES, LANES)
            dtb_ref[:, hcols] = jnp.broadcast_to(_col(dt, lane, h), (L, LANES))
            cumb_ref[:, hcols] = jnp.broadcast_to(_col(cum, lane, h), (L, LANES))

    row = pl.BlockSpec((L, LANES), lambda i: (i, 0))
    vec = pl.BlockSpec((1, LANES), lambda i: (0, 0))
    wide = pl.BlockSpec((L, n_heads * LANES), lambda i: (i, 0))
    return pl.pallas_call(
        body, name=name, grid=(T // L,), in_specs=[row, vec, vec], out_specs=[wide, wide],
        out_shape=[jax.ShapeDtypeStruct((T, n_heads * LANES), F32)] * 2, compiler_params=_cparams(),
    )(dt_pre, dt_bias, a_log)


SCAN_GROUPS_FWD = 8
SCAN_GROUPS_BWD = 4


def _scan_specs(nc, d_inner, hpg, GP):
    L = SSD_CHUNK
    G = SSD_N_GROUPS
    gw = GP * hpg * SSD_HEAD_DIM
    bw = GP * LANES
    assert G % GP == 0 and d_inner % bw == 0
    nb = d_inner // bw
    return dict(
        xs=pl.BlockSpec((L, gw), lambda b, q, c: (b * nc + c, q)),
        B=pl.BlockSpec((L, bw), lambda b, q, c: (b * nc + c, nb + q)),
        C=pl.BlockSpec((L, bw), lambda b, q, c: (b * nc + c, nb + G // GP + q)),
        heads=pl.BlockSpec((L, GP * hpg * LANES), lambda b, q, c: (b * nc + c, q)),
        gvec=pl.BlockSpec((1, gw), lambda b, q, c: (0, q)),
        grp=pl.BlockSpec((L, bw), lambda b, q, c: (b * nc + c, q)),
        st=pl.BlockSpec((1, GP * hpg // 2, SSD_D_STATE, LANES), lambda b, q, c: (b * nc + c, q, 0, 0)),
    )


def _grid_step(dims):
    t, total = 0, 1
    for ax, d in enumerate(dims):
        t = t * d + pl.program_id(ax)
        total *= d
    return t, total


def _scan_fwd(xbc, dtb, cumb, S, d_inner, name, gather=()):
    T = xbc.shape[0]
    Bn = T // S
    L = SSD_CHUNK
    G = SSD_N_GROUPS
    nc = S // L
    hpg = d_inner // SSD_HEAD_DIM // G
    pairs = hpg // 2
    GP = SCAN_GROUPS_FWD
    sp = _scan_specs(nc, d_inner, hpg, GP)
    ng = len(gather)

    def body(*refs):
        xs_ref, b_ref, c_ref, dtb_ref, cumb_ref = refs[:5]
        y_ref, st_ref = refs[5 + ng:7 + ng]
        state = refs[7 + 2 * ng]

        if ng:
            start, forward, finish = _gather_plan(refs[5:5 + ng], refs[7 + ng:7 + 2 * ng], *refs[8 + 2 * ng:])
            t, total = _grid_step((Bn, G // GP, nc))
            pl.when(t == 0)(start)
            pl.when(t == total // 2)(forward)

        @pl.when(pl.program_id(2) == 0)
        def _():
            state[...] = jnp.zeros_like(state)

        tril = _iota((L, L), 1) <= _iota((L, L), 0)
        lane = _iota((L, LANES), 1)
        lane1 = _iota((1, LANES), 1)
        last = _iota((L, 1), 0) == L - 1
        for gi, p in [(gi, p) for gi in range(GP) for p in range(pairs)]:
            if p == 0:
                Bm = b_ref[:, gi * LANES:(gi + 1) * LANES]
                Cm = c_ref[:, gi * LANES:(gi + 1) * LANES]
                Gm = _bdot(Cm, Bm, NT)
                S_cat = jnp.concatenate([state[gi * pairs + q] for q in range(pairs)], axis=1)
                Q_cat = _bdot(Cm, S_cat, NN)
                z_pairs, el_pairs = [], []
            sp_ = gi * pairs + p
            cols = pl.ds(sp_ * LANES, LANES)
            st_ref[0, sp_] = state[sp_]
            x_pair = xs_ref[:, cols]
            z_pair = jnp.zeros((L, LANES), F32)
            e_pair = jnp.zeros((L, LANES), F32)
            el_pair = jnp.zeros((1, LANES), F32)
            wms, xds = [], []
            for k in range(2):
                hcols = pl.ds((gi * hpg + 2 * p + k) * LANES, LANES)
                cum_col = cumb_ref[:, hcols]
                dt_col = dtb_ref[:, hcols]
                decay = jnp.exp(jnp.where(tril, cum_col - cum_col.T, NEG_BIG))
                mk = (lane >= SSD_HEAD_DIM) if k else (lane < SSD_HEAD_DIM)
                xd = jnp.where(mk, x_pair * dt_col, 0.0)
                wms.append((Gm * decay).astype(BF16))
                xds.append(xd.astype(BF16))
                cum_l = jnp.sum(jnp.where(last, cum_col, 0.0), axis=0, keepdims=True)
                e_pair = jnp.where(mk, jnp.exp(cum_col), e_pair)
                z_pair = z_pair + xd * jnp.exp(cum_l - cum_col)
                el_pair = jnp.where((lane1 >= SSD_HEAD_DIM) if k else (lane1 < SSD_HEAD_DIM), jnp.exp(cum_l), el_pair)
            y_pair = _bdot(jnp.concatenate(wms, axis=1), jnp.concatenate(xds, axis=0), NN)
            y_ref[:, cols] = y_pair + e_pair * Q_cat[:, p * LANES:(p + 1) * LANES]
            z_pairs.append(z_pair)
            el_pairs.append(el_pair)
            if p == pairs - 1:
                S_new = (S_cat * jnp.concatenate(el_pairs, axis=1)
                         + _bdot(Bm.T, jnp.concatenate(z_pairs, axis=1), NN))
                for q in range(pairs):
                    state[gi * pairs + q] = S_new[:, q * LANES:(q + 1) * LANES]

        if ng:
            pl.when(t == total - 1)(finish)

    outs = pl.pallas_call(
        body, name=name, grid=(Bn, G // GP, nc),
        in_specs=[sp['xs'], sp['B'], sp['C'], sp['heads'], sp['heads']] + [ANY] * ng,
        out_specs=[sp['xs'], sp['st']] + [ANY] * ng,
        out_shape=[jax.ShapeDtypeStruct((T, d_inner), F32),
                   jax.ShapeDtypeStruct((Bn * nc, G * pairs, SSD_D_STATE, LANES), F32)] + _gather_shapes(gather),
        scratch_shapes=[pltpu.VMEM((GP * pairs, SSD_D_STATE, LANES), F32)] + (_gather_sems(ng) if ng else []),
        compiler_params=_cparams(),
    )(xbc, xbc, xbc, dtb, cumb, *gather)
    return outs[0], outs[1], list(outs[2:])


def _scan_bwd(xbc, dtb, cumb, states, dy, d_lane, S, d_inner, name, exchange=()):
    T = xbc.shape[0]
    Bn = T // S
    L = SSD_CHUNK
    G = SSD_N_GROUPS
    nc = S // L
    hpg = d_inner // SSD_HEAD_DIM // G
    pairs = hpg // 2
    GP = SCAN_GROUPS_BWD
    sp = _scan_specs(nc, d_inner, hpg, GP)
    nx = len(exchange)

    def rev(spec):
        im = spec.index_map
        return pl.BlockSpec(spec.block_shape, lambda b, g, c: im(b, g, nc - 1 - c))

    def body(*refs):
        xs_ref, b_ref, c_ref, dtb_ref, cumb_ref, st_ref, dy_ref, dl_ref = refs[:8]
        dxs_ref, db_ref, dc_ref, ddt_ref, da_ref = refs[8 + nx:13 + nx]
        dstate = refs[13 + 2 * nx]
        g = pl.program_id(1)

        if nx:
            start, finish = _chips_plan(refs[8:8 + nx], refs[13 + nx:13 + 2 * nx], *refs[14 + 2 * nx:])
            t, total = _grid_step((Bn, G // GP, nc))
            pl.when(t == 0)(start)

        @pl.when(pl.program_id(2) == 0)
        def _():
            dstate[...] = jnp.zeros_like(dstate)

        tril = _iota((L, L), 1) <= _iota((L, L), 0)
        lane = _iota((L, LANES), 1)
        lane1 = _iota((1, LANES), 1)
        last = _iota((L, 1), 0) == L - 1
        for gi, p in [(gi, p) for gi in range(GP) for p in range(pairs)]:
            gcols = pl.ds(gi * LANES, LANES)
            if p == 0:
                Bm = b_ref[:, gcols]
                Cm = c_ref[:, gcols]
                Gm = _bdot(Cm, Bm, NT)
                dG = jnp.zeros((L, L), F32)
                dcum = jnp.zeros((L, LANES), F32)
                ddt = jnp.zeros((L, LANES), F32)
                S_cat = jnp.concatenate([st_ref[0, gi * pairs + q] for q in range(pairs)], axis=1)
                dS_cat = jnp.concatenate([dstate[gi * pairs + q] for q in range(pairs)], axis=1)
                Q_cat = _bdot(Cm, S_cat, NN)
                dZ_cat = _bdot(Bm, dS_cat, NN)
                dQs, zs, els = [], [], []
            sp_ = gi * pairs + p
            cols = pl.ds(sp_ * LANES, LANES)
            pcols = slice(p * LANES, (p + 1) * LANES)
            x_pair = xs_ref[:, cols]
            dy_pair = dy_ref[:, cols]
            Q = Q_cat[:, pcols]
            dZ = dZ_cat[:, pcols]
            prod = dS_cat[:, pcols] * S_cat[:, pcols]
            e_pair = jnp.zeros((L, LANES), F32)
            z_pair = jnp.zeros((L, LANES), F32)
            el_pair = jnp.zeros((1, LANES), F32)
            dx_pair = dl_ref[:, cols] * dy_pair
            heads = []
            for k in range(2):
                hcols = pl.ds((gi * hpg + 2 * p + k) * LANES, LANES)
                cum_col = cumb_ref[:, hcols]
                dt_col = dtb_ref[:, hcols]
                decay = jnp.exp(jnp.where(tril, cum_col - cum_col.T, NEG_BIG))
                mk = (lane >= SSD_HEAD_DIM) if k else (lane < SSD_HEAD_DIM)
                heads.append((cum_col, dt_col, decay, mk, jnp.where(mk, x_pair * dt_col, 0.0),
                              jnp.where(mk, dy_pair, 0.0), Gm * decay))
            dy_both = jnp.concatenate([hd[5] for hd in heads], axis=0).astype(BF16)
            dWm_both = _bdot(dy_both, heads[0][4] + heads[1][4], NT)
            dxd_pair = _bdot(jnp.concatenate([hd[6].T for hd in heads], axis=1), dy_both, NN)
            for k, (cum_col, dt_col, decay, mk, xd, dy_k, Wm) in enumerate(heads):
                h = (g * GP + gi) * hpg + 2 * p + k
                mk1 = (lane1 >= SSD_HEAD_DIM) if k else (lane1 < SSD_HEAD_DIM)
                dWm = jnp.where(tril, dWm_both[k * L:(k + 1) * L], 0.0)
                dxd = jnp.where(mk, dxd_pair, 0.0)
                dseg = dWm * Wm
                dG = dG + dWm * decay
                dcum_col = (jnp.sum(dseg, axis=1, keepdims=True)
                            - jnp.sum(dseg.T, axis=1, keepdims=True))
                cum_l = jnp.sum(jnp.where(last, cum_col, 0.0), axis=0, keepdims=True)
                e_col = jnp.exp(cum_col)
                w_col = jnp.exp(cum_l - cum_col)
                el = jnp.exp(cum_l)
                dcum_col = dcum_col + jnp.sum(dy_k * Q, axis=1, keepdims=True) * e_col
                de_e = jnp.sum(dZ * xd, axis=1, keepdims=True) * w_col
                dcum_col = dcum_col - de_e
                dcum_l = (jnp.sum(de_e, axis=0, keepdims=True)
                          + el * jnp.sum(jnp.sum(jnp.where(mk, prod, 0.0), axis=1, keepdims=True),
                                         axis=0, keepdims=True))
                dcum_col = dcum_col + jnp.where(last, dcum_l, 0.0)
                dxd = dxd + jnp.where(mk, dZ * w_col, 0.0)
                dx_pair = dx_pair + dxd * dt_col
                ddt = jnp.where(lane == h, jnp.sum(dxd * x_pair, axis=1, keepdims=True), ddt)
                dcum = jnp.where(lane == h, dcum_col, dcum)
                e_pair = jnp.where(mk, e_col, e_pair)
                z_pair = z_pair + xd * w_col
                el_pair = jnp.where(mk1, el, el_pair)
            dQs.append(e_pair * dy_pair)
            zs.append(z_pair)
            els.append(el_pair)
            dxs_ref[:, cols] = dx_pair
            if p == pairs - 1:
                dQ_cat = jnp.concatenate(dQs, axis=1)
                dS_new = dS_cat * jnp.concatenate(els, axis=1) + _bdot(Cm.T, dQ_cat, NN)
                for q in range(pairs):
                    dstate[gi * pairs + q] = dS_new[:, q * LANES:(q + 1) * LANES]
                dc_ref[:, gcols] = _bdot(dQ_cat, S_cat, NT) + _bdot(dG, Bm, NN)
                db_ref[:, gcols] = _bdot(jnp.concatenate(zs, axis=1), dS_cat, NT) + _bdot(dG.T, Cm, NN)
                ddt_ref[:, gcols] = ddt
                da_ref[:, gcols] = _rev_cumsum_rows(dcum)

        if nx:
            pl.when(t == total - 1)(finish)

    grp = jax.ShapeDtypeStruct((T, G * LANES), F32)
    outs = pl.pallas_call(
        body, name=name, grid=(Bn, G // GP, nc),
        in_specs=[rev(sp['xs']), rev(sp['B']), rev(sp['C']), rev(sp['heads']), rev(sp['heads']),
                  rev(sp['st']), rev(sp['xs']), sp['gvec']] + [ANY] * nx,
        out_specs=[rev(sp['xs']), rev(sp['grp']), rev(sp['grp']), rev(sp['grp']), rev(sp['grp'])] + [ANY] * nx,
        out_shape=[jax.ShapeDtypeStruct((T, d_inner), F32), grp, grp, grp, grp] + _chips_shapes(exchange),
        scratch_shapes=[pltpu.VMEM((GP * pairs, SSD_D_STATE, LANES), F32)] + (_chips_sems(nx) if nx else []),
        compiler_params=_cparams(),
    )(xbc, xbc, xbc, dtb, cumb, states, dy, d_lane, *exchange)
    return outs[:5], list(outs[5:])


def _dt_bwd(ddt_g, da_g, dt_pre, dt_bias, a_log, name):
    T = dt_pre.shape[0]
    G = SSD_N_GROUPS
    tm = _tile(T, (512, 256, 128))

    def body(ddt_ref, da_ref, dtp_ref, bias_ref, alog_ref, dx_ref, dbias_ref, dal_ref):
        @pl.when(pl.program_id(0) == 0)
        def _():
            dbias_ref[...] = jnp.zeros_like(dbias_ref)
            dal_ref[...] = jnp.zeros_like(dal_ref)

        ddt = ddt_ref[:, 0:LANES]
        da = da_ref[:, 0:LANES]
        for gi in range(1, G):
            ddt = ddt + ddt_ref[:, gi * LANES:(gi + 1) * LANES]
            da = da + da_ref[:, gi * LANES:(gi + 1) * LANES]
        xv = dtp_ref[...] + bias_ref[...]
        A = -jnp.exp(alog_ref[...])
        dx = (ddt + da * A) * _sigmoid(xv)
        dx_ref[...] = dx.astype(BF16)
        dbias_ref[...] += jnp.sum(dx, axis=0, keepdims=True)
        dal_ref[...] += jnp.sum(da * _softplus(xv), axis=0, keepdims=True) * A

    wide = pl.BlockSpec((tm, G * LANES), lambda i: (i, 0))
    row = pl.BlockSpec((tm, LANES), lambda i: (i, 0))
    vec = pl.BlockSpec((1, LANES), lambda i: (0, 0))
    vsh = jax.ShapeDtypeStruct((1, LANES), F32)
    return pl.pallas_call(
        body, name=name, grid=(T // tm,), in_specs=[wide, wide, row, vec, vec], out_specs=[row, vec, vec],
        out_shape=[jax.ShapeDtypeStruct((T, LANES), BF16), vsh, vsh], compiler_params=_cparams(),
    )(ddt_g, da_g, dt_pre, dt_bias, a_log)


def _gate_fwd(y, xbc, z, d_lane, ng, d_inner, name):
    T = y.shape[0]
    G = SSD_N_GROUPS
    gw = d_inner // G
    tm = _tile(T, (1024, 512, 256, 128))

    def body(y_ref, x_ref, z_ref, dl_ref, ng_ref, o_ref):
        zv = z_ref[...]
        y2 = (y_ref[...] + dl_ref[...] * x_ref[...]) * (zv * _sigmoid(zv))
        r = lax.rsqrt(jnp.mean(y2 * y2, axis=1, keepdims=True) + LN_EPS)
        o_ref[...] = (y2 * r * ng_ref[...]).astype(BF16)

    blk = pl.BlockSpec((tm, gw), lambda g, i: (i, g))
    vec = pl.BlockSpec((1, gw), lambda g, i: (0, g))
    return pl.pallas_call(
        body, name=name, grid=(G, T // tm), in_specs=[blk, blk, blk, vec, vec], out_specs=blk,
        out_shape=jax.ShapeDtypeStruct((T, d_inner), BF16), compiler_params=_cparams(),
    )(y, xbc, z, d_lane, ng)


def _gate_bwd(dyn, y, xbc, z, d_lane, ng, d_inner, name, exchange=()):
    T = y.shape[0]
    G = SSD_N_GROUPS
    gw = d_inner // G
    tm = _tile(T, (1024, 512, 256, 128))
    nx = len(exchange)

    def body(*refs):
        dyn_ref, y_ref, x_ref, z_ref, dl_ref, ng_ref = refs[:6]
        dy_ref, dz_ref, dng_ref, ddl_ref = refs[6 + nx:10 + nx]
        if nx:
            start, finish = _sibling_plan(refs[6:6 + nx], refs[10 + nx:10 + 2 * nx], *refs[10 + 2 * nx:])
            t, total = _grid_step((G, T // tm))
            pl.when(t == 0)(start)

        @pl.when(pl.program_id(1) == 0)
        def _():
            dng_ref[...] = jnp.zeros_like(dng_ref)
            ddl_ref[...] = jnp.zeros_like(ddl_ref)

        zv = z_ref[...]
        xv = x_ref[...]
        sz = _sigmoid(zv)
        y1 = y_ref[...] + dl_ref[...] * xv
        y2 = y1 * (zv * sz)
        r = lax.rsqrt(jnp.mean(y2 * y2, axis=1, keepdims=True) + LN_EPS)
        dn = dyn_ref[...].astype(F32)
        dng_ref[...] += jnp.sum(dn * y2 * r, axis=0, keepdims=True)
        do = dn * ng_ref[...]
        dy2 = r * do - y2 * (r * r * r) * jnp.mean(do * y2, axis=1, keepdims=True)
        dz_ref[...] = (dy2 * y1 * sz * (1.0 + zv * (1.0 - sz))).astype(BF16)
        dy1 = dy2 * (zv * sz)
        dy_ref[...] = dy1
        ddl_ref[...] += jnp.sum(dy1 * xv, axis=0, keepdims=True)
        if nx:
            pl.when(t == total - 1)(finish)

    blk = pl.BlockSpec((tm, gw), lambda g, i: (i, g))
    vec = pl.BlockSpec((1, gw), lambda g, i: (0, g))
    vsh = jax.ShapeDtypeStruct((1, d_inner), F32)
    outs = pl.pallas_call(
        body, name=name, grid=(G, T // tm), in_specs=[blk, blk, blk, blk, vec, vec] + [ANY] * nx,
        out_specs=[blk, blk, vec, vec] + [ANY] * nx,
        out_shape=[jax.ShapeDtypeStruct((T, d_inner), F32), jax.ShapeDtypeStruct((T, d_inner), BF16), vsh, vsh]
        + _sibling_shapes(exchange),
        scratch_shapes=_sibling_sems(nx) if nx else [], compiler_params=_cparams(),
    )(dyn, y, xbc, z, d_lane, ng, *exchange)
    return outs[:4], list(outs[4:])


def _rms_fwd(x, g, name):
    T, R = x.shape
    tm = _tile(T, (512, 256, 128))

    def body(x_ref, g_ref, y_ref):
        xv = x_ref[...]
        y = xv * lax.rsqrt(jnp.mean(xv * xv, axis=1, keepdims=True) + RMS_EPS) * g_ref[...]
        y_ref[...] = y.astype(BF16)

    row = pl.BlockSpec((tm, R), lambda i: (i, 0))
    vec = pl.BlockSpec((1, R), lambda i: (0, 0))
    return pl.pallas_call(
        body, name=name, grid=(T // tm,), in_specs=[row, vec], out_specs=row,
        out_shape=jax.ShapeDtypeStruct((T, R), BF16), compiler_params=_cparams(),
    )(x, g)


def _rms_bwd(dy, x, g, name):
    T, R = x.shape
    tm = _tile(T, (512, 256, 128))

    def body(dy_ref, x_ref, g_ref, dx_ref, dg_ref):
        @pl.when(pl.program_id(0) == 0)
        def _():
            dg_ref[...] = jnp.zeros_like(dg_ref)

        xv = x_ref[...]
        dyv = dy_ref[...]
        r = lax.rsqrt(jnp.mean(xv * xv, axis=1, keepdims=True) + RMS_EPS)
        dg_ref[...] += jnp.sum(dyv * xv * r, axis=0, keepdims=True)
        do = dyv * g_ref[...]
        dx = r * do - xv * (r * r * r) * jnp.mean(do * xv, axis=1, keepdims=True)
        dx_ref[...] = dx.astype(BF16)

    row = pl.BlockSpec((tm, R), lambda i: (i, 0))
    vec = pl.BlockSpec((1, R), lambda i: (0, 0))
    return pl.pallas_call(
        body, name=name, grid=(T // tm,), in_specs=[row, row, vec], out_specs=[row, vec],
        out_shape=[jax.ShapeDtypeStruct((T, R), BF16), jax.ShapeDtypeStruct((1, R), F32)],
        compiler_params=_cparams(),
    )(dy, x, g)


def _swap_halves(x):
    half = MLA_ROPE // 2
    return jnp.where(_iota(x.shape, 1) < half, pltpu.roll(x, LANES - half, 1), pltpu.roll(x, half, 1))


def _rope(x, cc, ss, *, transpose, sum_heads=False, name):
    T, W = x.shape
    nh = W // LANES
    tm = _tile(T, (512, 256, 128))
    n_out = 1 if sum_heads else nh

    def body(x_ref, cc_ref, ss_ref, o_ref):
        ccv = cc_ref[...]
        ssv = ss_ref[...]
        if sum_heads:
            xv = x_ref[:, 0:LANES]
            for hh in range(1, nh):
                xv = xv + x_ref[:, hh * LANES:(hh + 1) * LANES]
            heads = [xv]
        else:
            heads = [x_ref[:, hh * LANES:(hh + 1) * LANES] for hh in range(nh)]
        for hh, xv in enumerate(heads):
            if transpose:
                r = xv * ccv + _swap_halves(xv * ssv)
            else:
                r = xv * ccv + _swap_halves(xv) * ssv
            r = jnp.where(_iota(r.shape, 1) < MLA_ROPE, r, 0.0)
            o_ref[:, hh * LANES:(hh + 1) * LANES] = r.astype(BF16)

    return pl.pallas_call(
        body, name=name, grid=(T // tm,),
        in_specs=[pl.BlockSpec((tm, W), lambda i: (i, 0)), pl.BlockSpec((tm, LANES), lambda i: (i, 0)),
                  pl.BlockSpec((tm, LANES), lambda i: (i, 0))],
        out_specs=pl.BlockSpec((tm, n_out * LANES), lambda i: (i, 0)),
        out_shape=jax.ShapeDtypeStruct((T, n_out * LANES), BF16), compiler_params=_cparams(),
    )(x, cc, ss)


ATT_BLOCK = 512
ATT_SUB = 256
ATT_SUB_KV = 512
LOG2E = 1.4426950408889634


def _att_geometry(S):
    tb = _tile(S, (ATT_BLOCK, 256))
    return tb, S // tb, tb // ATT_SUB


def _heads_per_step(nh):
    return 4 if nh % 4 == 0 else 2 if nh % 2 == 0 else 1


def _att_scores(a, b, diag, kv_major, off):
    s = _bdot(a, b, NT)
    if diag:
        q_ax, k_ax = (1, 0) if kv_major else (0, 1)
        s = jnp.where(_iota(s.shape, k_ax) <= _iota(s.shape, q_ax) + off, s, NEG_BIG)
    return s


def _attention_fwd(qn, qr, kn, kr, v, S, scale, name):
    T, W = qn.shape
    nh = W // LANES
    Bn = T // S
    tb, n, C = _att_geometry(S)
    pairs = [(i, j) for i in range(n) for j in range(i + 1)]
    it = jnp.asarray([p[0] for p in pairs], jnp.int32)
    jt = jnp.asarray([p[1] for p in pairs], jnp.int32)
    c2 = scale * LOG2E

    HP = _heads_per_step(nh)

    def body(it_ref, jt_ref, qn_ref, qr_ref, kn_ref, kr_ref, v_ref, o_ref, ob_ref, lse_ref, m_sc, l_sc, acc):
        p = pl.program_id(2)
        i = it_ref[p]
        j = jt_ref[p]

        @pl.when(j == 0)
        def _():
            m_sc[...] = jnp.full_like(m_sc, NEG_BIG)
            l_sc[...] = jnp.zeros_like(l_sc)
            acc[...] = jnp.zeros_like(acc)

        def step(diag):
            for hh in range(HP):
                hs = pl.ds(hh * LANES, LANES)
                qc = jnp.concatenate([qn_ref[:, hs], qr_ref[:, hs]], axis=1)
                kc = jnp.concatenate([kn_ref[:, hs], kr_ref[...]], axis=1)
                st = _att_scores(kc, qc, diag, True, 0)
                m_old = m_sc[hh]
                m_new = jnp.maximum(m_old, jnp.max(st, axis=0, keepdims=True))
                pt = jnp.exp2((st - m_new) * c2)
                corr = jnp.exp2((m_old - m_new) * c2)
                l_sc[hh] = corr * l_sc[hh] + jnp.sum(pt, axis=0, keepdims=True)
                acc[hh] = corr * acc[hh] + _bdot(v_ref[:, hs].T, pt, NN)
                m_sc[hh] = m_new

        @pl.when(j < i)
        def _():
            step(False)

        @pl.when(j == i)
        def _():
            step(True)
            for hh in range(HP):
                hs = pl.ds(hh * LANES, LANES)
                ov = (acc[hh] / l_sc[hh]).T
                o_ref[:, hs] = ov
                ob_ref[:, hs] = ov.astype(BF16)
                lse_row = m_sc[hh] * scale + jnp.log(l_sc[hh])
                lse_ref[:, hs] = jnp.broadcast_to(lse_row, (LANES, tb)).T

    qspec = pl.BlockSpec((tb, HP * LANES), lambda b, h, p, it_, jt_: (b * n + it_[p], h))
    kspec = pl.BlockSpec((tb, HP * LANES), lambda b, h, p, it_, jt_: (b * n + jt_[p], h))
    krspec = pl.BlockSpec((tb, LANES), lambda b, h, p, it_, jt_: (b * n + jt_[p], 0))
    return pl.pallas_call(
        body, name=name,
        grid_spec=pltpu.PrefetchScalarGridSpec(
            num_scalar_prefetch=2, grid=(Bn, nh // HP, len(pairs)),
            in_specs=[qspec, qspec, kspec, krspec, kspec], out_specs=[qspec, qspec, qspec],
            scratch_shapes=[pltpu.VMEM((HP, 1, tb), F32), pltpu.VMEM((HP, 1, tb), F32),
                            pltpu.VMEM((HP, LANES, tb), F32)]),
        out_shape=[jax.ShapeDtypeStruct((T, W), F32), jax.ShapeDtypeStruct((T, W), BF16),
                   jax.ShapeDtypeStruct((T, W), F32)],
        compiler_params=_cparams(),
    )(it, jt, qn, qr, kn, kr, v)


def _attention_dq(qn, qr, kn, kr, v, o, do, lse, S, scale, name):
    T, W = qn.shape
    nh = W // LANES
    Bn = T // S
    tb, n, C = _att_geometry(S)
    pairs = [(i, j) for i in range(n) for j in range(i + 1)]
    it = jnp.asarray([p[0] for p in pairs], jnp.int32)
    jt = jnp.asarray([p[1] for p in pairs], jnp.int32)
    c2 = scale * LOG2E
    HP = _heads_per_step(nh)

    def body(it_ref, jt_ref, qn_ref, qr_ref, kn_ref, kr_ref, v_ref, o_ref, do_ref, lse_ref,
             dqn_ref, dqr_ref, acc, di_sc, lse_sc):
        p = pl.program_id(2)
        i = it_ref[p]
        j = jt_ref[p]

        @pl.when(j == 0)
        def _():
            acc[...] = jnp.zeros_like(acc)
            for hh in range(HP):
                hs = pl.ds(hh * LANES, LANES)
                di_sc[hh] = jnp.sum(do_ref[:, hs] * o_ref[:, hs], axis=1, keepdims=True)
                lse_sc[hh] = jnp.max(lse_ref[:, hs], axis=1, keepdims=True) * LOG2E

        def step(diag):
            for hh in range(HP):
                hs = pl.ds(hh * LANES, LANES)
                qc = jnp.concatenate([qn_ref[:, hs], qr_ref[:, hs]], axis=1)
                for c in range(C):
                    r0 = c * ATT_SUB if diag else 0
                    rows = pl.ds(r0, tb - r0)
                    ks = pl.ds(c * ATT_SUB, ATT_SUB)
                    kc = jnp.concatenate([kn_ref[ks, hs], kr_ref[ks, :]], axis=1)
                    s = _att_scores(qc[r0:], kc, diag, False, 0)
                    pr = jnp.exp2(s * c2 - lse_sc[hh, rows, :])
                    dp = _bdot(do_ref[rows, hs], v_ref[ks, hs], NT)
                    ds = pr * (dp - di_sc[hh, rows, :]) * scale
                    acc[hh, rows, :] += _bdot(ds, kc, NN)

        @pl.when(j < i)
        def _():
            step(False)

        @pl.when(j == i)
        def _():
            step(True)
            for hh in range(HP):
                hs = pl.ds(hh * LANES, LANES)
                dqn_ref[:, hs] = acc[hh, :, 0:LANES].astype(BF16)
                dqr_ref[:, hs] = acc[hh, :, LANES:2 * LANES]

    qspec = pl.BlockSpec((tb, HP * LANES), lambda b, h, p, it_, jt_: (b * n + it_[p], h))
    kspec = pl.BlockSpec((tb, HP * LANES), lambda b, h, p, it_, jt_: (b * n + jt_[p], h))
    krspec = pl.BlockSpec((tb, LANES), lambda b, h, p, it_, jt_: (b * n + jt_[p], 0))
    return pl.pallas_call(
        body, name=name,
        grid_spec=pltpu.PrefetchScalarGridSpec(
            num_scalar_prefetch=2, grid=(Bn, nh // HP, len(pairs)),
            in_specs=[qspec, qspec, kspec, krspec, kspec, qspec, qspec, qspec], out_specs=[qspec, qspec],
            scratch_shapes=[pltpu.VMEM((HP, tb, 2 * LANES), F32), pltpu.VMEM((HP, tb, 1), F32),
                            pltpu.VMEM((HP, tb, 1), F32)]),
        out_shape=[jax.ShapeDtypeStruct((T, W), BF16), jax.ShapeDtypeStruct((T, W), F32)],
        compiler_params=_cparams(),
    )(it, jt, qn, qr, kn, kr, v, o, do, lse)


def _attention_dkv(qn, qr, kn, kr, v, o, do, lse, S, scale, name):
    T, W = qn.shape
    nh = W // LANES
    Bn = T // S
    tb, n, _ = _att_geometry(S)
    sub = min(ATT_SUB_KV, tb)
    C = tb // sub
    HP = _heads_per_step(nh)
    triples = [(j, h, i) for j in range(n) for h in range(nh // HP) for i in range(j, n)]
    jt = jnp.asarray([t[0] for t in triples], jnp.int32)
    ht = jnp.asarray([t[1] for t in triples], jnp.int32)
    it = jnp.asarray([t[2] for t in triples], jnp.int32)
    c2 = scale * LOG2E

    def as_row(col):
        return jnp.broadcast_to(col, (sub, LANES)).T[0:1, :]

    def body(jt_ref, ht_ref, it_ref, qn_ref, qr_ref, kn_ref, kr_ref, v_ref, o_ref, do_ref, lse_ref,
             dkn_ref, dv_ref, dkr_ref, akc, av):
        p = pl.program_id(1)
        j = jt_ref[p]
        h = ht_ref[p]
        i = it_ref[p]

        @pl.when(jnp.logical_and(h == 0, i == j))
        def _():
            dkr_ref[...] = jnp.zeros_like(dkr_ref)

        @pl.when(i == j)
        def _():
            akc[...] = jnp.zeros_like(akc)
            av[...] = jnp.zeros_like(av)

        def step(diag):
            for hh in range(HP):
                hs = pl.ds(hh * LANES, LANES)
                kc = jnp.concatenate([kn_ref[:, hs], kr_ref[...]], axis=1)
                for c in range(C):
                    rows = pl.ds(c * sub, sub)
                    k1 = (c + 1) * sub if diag else tb
                    ks = pl.ds(0, k1)
                    qc = jnp.concatenate([qn_ref[rows, hs], qr_ref[rows, hs]], axis=1)
                    st = _att_scores(kc[:k1], qc, diag, True, c * sub)
                    dov = do_ref[rows, hs]
                    lse_row = as_row(jnp.max(lse_ref[rows, hs], axis=1, keepdims=True)) * LOG2E
                    di_row = as_row(jnp.sum(dov * o_ref[rows, hs], axis=1, keepdims=True))
                    pt = jnp.exp2(st * c2 - lse_row)
                    dst = pt * (_bdot(v_ref[ks, hs], dov, NT) - di_row) * scale
                    av[ks, hs] += _bdot(pt, dov, NN)
                    akc[hh, ks, :] += _bdot(dst, qc, NN)

        @pl.when(i > j)
        def _():
            step(False)

        @pl.when(i == j)
        def _():
            step(True)

        @pl.when(i == n - 1)
        def _():
            for hh in range(HP):
                dkn_ref[:, pl.ds(hh * LANES, LANES)] = akc[hh, :, 0:LANES].astype(BF16)
                dkr_ref[...] += akc[hh, :, LANES:2 * LANES]
            dv_ref[...] = av[...].astype(BF16)

    qspec = pl.BlockSpec((tb, HP * LANES), lambda b, p, jt_, ht_, it_: (b * n + it_[p], ht_[p]))
    kspec = pl.BlockSpec((tb, HP * LANES), lambda b, p, jt_, ht_, it_: (b * n + jt_[p], ht_[p]))
    krspec = pl.BlockSpec((tb, LANES), lambda b, p, jt_, ht_, it_: (b * n + jt_[p], 0))
    return pl.pallas_call(
        body, name=name,
        grid_spec=pltpu.PrefetchScalarGridSpec(
            num_scalar_prefetch=3, grid=(Bn, len(triples)),
            in_specs=[qspec, qspec, kspec, krspec, kspec, qspec, qspec, qspec],
            out_specs=[kspec, kspec, krspec],
            scratch_shapes=[pltpu.VMEM((HP, tb, 2 * LANES), F32), pltpu.VMEM((tb, HP * LANES), F32)]),
        out_shape=[jax.ShapeDtypeStruct((T, W), BF16), jax.ShapeDtypeStruct((T, W), BF16),
                   jax.ShapeDtypeStruct((T, LANES), F32)],
        compiler_params=_cparams(),
    )(jt, ht, it, qn, qr, kn, kr, v, o, do, lse)


def _pad_cols(w, n):
    return jnp.pad(w, ((0, 0), (0, n - w.shape[1])))


def _local_step(x, positions, w, target, late=None, reduce_early=None, reduce_last=None):
    w = dict(w)
    Bn, S, D = x.shape
    T = Bn * S
    depth = w['ln_mix_g'].shape[0]
    alpha = (2 * depth) ** 0.25
    d_inner = w['ssd_norm_g'].shape[1]
    n_heads_ssd = d_inner // SSD_HEAD_DIM
    G = SSD_N_GROUPS
    gn = G * SSD_D_STATE
    conv_dim = d_inner + 2 * gn
    hpg = n_heads_ssd // G
    nh = D // MLA_NOPE
    kv_rank = w['kv_norm_g'].shape[0]
    q_rank = w['q_norm_g'].shape[1]
    H = w['ffn_conv_b'].shape[1] // 2
    scale = (MLA_NOPE + MLA_ROPE) ** -0.5
    assert S % SSD_CHUNK == 0 and hpg % 2 == 0 and SSD_D_STATE == LANES and n_heads_ssd <= LANES

    X = x.reshape(T, D)
    tgt = target.reshape(T, D)

    inv_freq = 1.0 / (ROPE_THETA ** (jnp.arange(0, MLA_ROPE, 2, dtype=F32) / MLA_ROPE))
    ang = positions.reshape(T).astype(F32)[:, None] * inv_freq
    cos, sin = jnp.cos(ang), jnp.sin(ang)
    zpad = jnp.zeros((T, LANES - MLA_ROPE), F32)
    cc = jnp.concatenate([cos, cos, zpad], axis=1)
    ss = jnp.concatenate([-sin, sin, zpad], axis=1)

    w_in = w['ssd_in_proj'][0]
    w_z = w_in[:, :d_inner]
    w_xbc = w_in[:, d_inner:d_inner + conv_dim]
    w_dt = _pad_cols(w_in[:, d_inner + conv_dim:], LANES)
    ssd_cw = w['ssd_conv_w'][0]
    ssd_cb = w['ssd_conv_b']
    dt_bias = _pad_cols(w['ssd_dt_bias'], LANES)
    a_log = _pad_cols(w['ssd_A_log'], LANES)
    d_lane = jnp.repeat(w['ssd_D'], SSD_HEAD_DIM, axis=1)
    ssd_ng = w['ssd_norm_g']

    Xb = X.astype(BF16)
    z = _mm(Xb, w_z, name='ssd_in_z')
    xbc_pre = _mm(Xb, w_xbc, name='ssd_in_xbc')
    dt_pre = _mm(Xb, w_dt, name='ssd_in_dt')
    xbc, xbc_conv = _ssd_conv_fwd(xbc_pre, ssd_cw, ssd_cb, S, name='ssd_conv')
    dtb, cumb = _ssd_dt_prep(dt_pre, dt_bias, a_log, n_heads_ssd, name='ssd_dt_prep')
    y_scan, states, gathered = _scan_fwd(xbc, dtb, cumb, S, d_inner, name='ssd_scan',
                                         gather=late[0] if late else ())
    if late:
        w.update(late[1](gathered))

    w_out = w['ssd_out_proj'][0]
    w_kvc = w['kv_down_proj'][:, :kv_rank]
    w_kvr = _pad_cols(w['kv_down_proj'][:, kv_rank:], LANES)
    kv_g = w['kv_norm_g'].reshape(1, kv_rank)
    w_uk = w['kv_up_k']
    w_uv = w['kv_up_v']
    w_qd = w['q_down_proj'][0]
    q_g = w['q_norm_g']
    qu = w['q_up_proj'][0].reshape(q_rank, nh, MLA_NOPE + MLA_ROPE)
    w_qn = qu[:, :, :MLA_NOPE].reshape(q_rank, nh * LANES)
    w_qr = jnp.pad(qu[:, :, MLA_NOPE:], ((0, 0), (0, 0), (0, LANES - MLA_ROPE))).reshape(q_rank, nh * LANES)
    w_ao = w['attn_out_proj'][0]

    def ffn_parts(i):
        return dict(wu=w['ffn_up'][i], cw=w['ffn_conv_w'][i], cb=w['ffn_conv_b'][i:i + 1], wd=w['ffn_down'][i])

    def ln(name, i):
        return w[name][i:i + 1]

    def ffn_fwd(hb, i):
        f = ffn_parts(i)
        u = _mm(hb, f['wu'], name=f'ffn{i}_up')
        a = _ffn_act_fwd(u, f['cw'], f['cb'], S, name=f'ffn{i}_act')
        return _mm(a, f['wd'], name=f'ffn{i}_down'), (u, a)

    yn = _gate_fwd(y_scan, xbc, z, d_lane, ssd_ng, d_inner, name='ssd_gate')
    mix0 = _mm(yn, w_out, name='ssd_out')
    h1, s1, h1b = _ln_fwd(X, mix0, ln('ln_mix_g', 0), ln('ln_mix_b', 0), alpha, name='ln_mix0')
    ff0, ffn0_saved = ffn_fwd(h1b, 0)
    h2, s2, h2b = _ln_fwd(h1, ff0, ln('ln_ffn_g', 0), ln('ln_ffn_b', 0), alpha, name='ln_ffn0')

    ckv = _mm(h2b, w_kvc, name='kv_down_c')
    kr_pre = _mm(h2b, w_kvr, name='kv_down_r')
    ckvn = _rms_fwd(ckv, kv_g, name='kv_norm')
    kr = _rope(kr_pre, cc, ss, transpose=False, name='k_rope')
    kn = _mm(ckvn, w_uk, out_dtype=BF16, name='kv_up_k')
    v = _mm(ckvn, w_uv, out_dtype=BF16, name='kv_up_v')
    cq_pre = _mm(h2b, w_qd, name='q_down')
    cq = _rms_fwd(cq_pre, q_g, name='q_norm')
    qn = _mm(cq, w_qn, out_dtype=BF16, name='q_up_n')
    qr_pre = _mm(cq, w_qr, name='q_up_r')
    qr = _rope(qr_pre, cc, ss, transpose=False, name='q_rope')
    o, ob, lse = _attention_fwd(qn, qr, kn, kr, v, S, scale, name='attn_fwd')
    mix1 = _mm(ob, w_ao, name='attn_out')
    h3, s3, h3b = _ln_fwd(h2, mix1, ln('ln_mix_g', 1), ln('ln_mix_b', 1), alpha, name='ln_mix1')
    ff1, ffn1_saved = ffn_fwd(h3b, 1)
    h4, s4, _ = _ln_fwd(h3, ff1, ln('ln_ffn_g', 1), ln('ln_ffn_b', 1), alpha, name='ln_ffn1')
    sq, dh4 = _loss_head(h4, tgt, name='loss_head')

    g = {}

    def ffn_bwd(ds, dsb, hb_in, saved, i):
        f = ffn_parts(i)
        u, a = saved
        d_wd = _mm(a, dsb, ta=True, name=f'ffn{i}_down_dw')
        da = _mm(dsb, f['wd'], tb=True, out_dtype=BF16, name=f'ffn{i}_down_dx')
        dug, duv, dwg, dwv, dbg, dbv = _ffn_act_bwd(u, da, f['cw'], f['cb'], S, name=f'ffn{i}_act_bwd')
        d_wu = _mm(hb_in, dug, ta=True, name=f'ffn{i}_up_g_dw', into=lax.empty((D, 2 * H), F32))
        d_wu = _mm(hb_in, duv, ta=True, name=f'ffn{i}_up_v_dw', into=d_wu, into_col=H)
        dh = _mm(dug, f['wu'], tb=True, add=ds, add_scale=alpha, name=f'ffn{i}_up_g_dx')
        dh = _mm(duv, f['wu'], tb=True, add=dh, name=f'ffn{i}_up_v_dx', b_koff=H)
        return dh, d_wd, d_wu, jnp.concatenate([dwg, dwv], axis=1), jnp.concatenate([dbg, dbv], axis=1)

    ds4, ds4b, dg_f1, db_f1 = _ln_bwd(dh4, s4, ln('ln_ffn_g', 1), name='ln_ffn1_bwd')
    dh3, d_wd1, d_wu1, d_fcw1, d_fcb1 = ffn_bwd(ds4, ds4b, h3b, ffn1_saved, 1)
    ds3, ds3b, dg_m1, db_m1 = _ln_bwd(dh3, s3, ln('ln_mix_g', 1), name='ln_mix1_bwd')

    g['attn_out_proj'] = _mm(ob, ds3b, ta=True, name='attn_out_dw')[None]
    do = _mm(ds3b, w_ao, tb=True, name='attn_out_dx')
    dqn, dqr = _attention_dq(qn, qr, kn, kr, v, o, do, lse, S, scale, name='attn_bwd_dq')
    dkn, dv, dkr = _attention_dkv(qn, qr, kn, kr, v, o, do, lse, S, scale, name='attn_bwd_dkv')
    dqr_pre = _rope(dqr, cc, ss, transpose=True, name='q_rope_bwd')
    dkr_pre = _rope(dkr, cc, ss, transpose=True, name='k_rope_bwd')

    d_wqn = _mm(cq, dqn, ta=True, name='q_up_n_dw').reshape(q_rank, nh, LANES)
    d_wqr = _mm(cq, dqr_pre, ta=True, name='q_up_r_dw').reshape(q_rank, nh, LANES)[:, :, :MLA_ROPE]
    g['q_up_proj'] = jnp.concatenate([d_wqn, d_wqr], axis=2).reshape(1, q_rank, nh * (MLA_NOPE + MLA_ROPE))
    dcq = _mm(dqn, w_qn, tb=True, name='q_up_n_dx')
    dcq = _mm(dqr_pre, w_qr, tb=True, add=dcq, name='q_up_r_dx')
    dcq_pre, d_qg = _rms_bwd(dcq, cq_pre, q_g, name='q_norm_bwd')
    g['q_norm_g'] = d_qg
    g['q_down_proj'] = _mm(h2b, dcq_pre, ta=True, name='q_down_dw')[None]

    g['kv_up_k'] = _mm(ckvn, dkn, ta=True, name='kv_up_k_dw')
    g['kv_up_v'] = _mm(ckvn, dv, ta=True, name='kv_up_v_dw')
    dckvn = _mm(dkn, w_uk, tb=True, name='kv_up_k_dx')
    dckvn = _mm(dv, w_uv, tb=True, add=dckvn, name='kv_up_v_dx')
    dckv, d_kvg = _rms_bwd(dckvn, ckv, kv_g, name='kv_norm_bwd')
    g['kv_norm_g'] = d_kvg.reshape(kv_rank)
    g['kv_down_proj'] = jnp.concatenate(
        [_mm(h2b, dckv, ta=True, name='kv_down_c_dw'),
         _mm(h2b, dkr_pre, ta=True, name='kv_down_r_dw')[:, :MLA_ROPE]], axis=1)

    dh2 = _mm(dcq_pre, w_qd, tb=True, add=ds3, add_scale=alpha, name='q_down_dx')
    dh2 = _mm(dckv, w_kvc, tb=True, add=dh2, name='kv_down_c_dx')
    dh2 = _mm(dkr_pre, w_kvr, tb=True, add=dh2, name='kv_down_r_dx')

    ds2, ds2b, dg_f0, db_f0 = _ln_bwd(dh2, s2, ln('ln_ffn_g', 0), name='ln_ffn0_bwd')
    dh1, d_wd0, d_wu0, d_fcw0, d_fcb0 = ffn_bwd(ds2, ds2b, h1b, ffn0_saved, 0)
    ds1, ds1b, dg_m0, db_m0 = _ln_bwd(dh1, s1, ln('ln_mix_g', 0), name='ln_mix0_bwd')

    g['ssd_out_proj'] = _mm(yn, ds1b, ta=True, name='ssd_out_dw')[None]
    dyn = _mm(ds1b, w_out, tb=True, out_dtype=BF16, name='ssd_out_dx')
    g['ffn_up'] = jnp.stack([d_wu0, d_wu1])
    g['ffn_down'] = jnp.stack([d_wd0, d_wd1])
    early_blocks = reduce_early[0](g) if reduce_early else ()
    (dy1, dz, d_ng, d_dl), from_sibling = _gate_bwd(dyn, y_scan, xbc, z, d_lane, ssd_ng, d_inner,
                                                    name='ssd_gate_bwd', exchange=early_blocks)
    (dxs, dB, dC, ddt_g, da_g), early = _scan_bwd(
        xbc, dtb, cumb, states, dy1, d_lane, S, d_inner, name='ssd_scan_bwd',
        exchange=reduce_early[1](early_blocks, from_sibling) if reduce_early else ())
    ddt_pre, d_bias, d_alog = _dt_bwd(ddt_g, da_g, dt_pre, dt_bias, a_log, name='ssd_dt_bwd')
    dxbc_pre, d_scw, d_scb = _ssd_conv_bwd(xbc_pre, xbc_conv, [dxs, dB, dC], ssd_cw, S, name='ssd_conv_bwd')
    g['ssd_in_proj'] = jnp.concatenate(
        [_mm(Xb, dz, ta=True, name='ssd_in_z_dw'), _mm(Xb, dxbc_pre, ta=True, name='ssd_in_xbc_dw'),
         _mm(Xb, ddt_pre, ta=True, name='ssd_in_dt_dw')[:, :n_heads_ssd]], axis=1)[None]
    g['ssd_conv_w'] = d_scw[None]
    g['ssd_conv_b'] = d_scb
    g['ssd_norm_g'] = d_ng
    g['ffn_conv_w'] = jnp.stack([d_fcw0, d_fcw1])
    dx = _mm(dz, w_z, tb=True, add=ds1, add_scale=alpha, name='ssd_in_z_dx')
    dx = _mm(dxbc_pre, w_xbc, tb=True, add=dx, name='ssd_in_xbc_dx', exchange=reduce_last(g) if reduce_last else ())
    dx, last = dx if reduce_last else (dx, None)
    dx = _mm(ddt_pre, w_dt, tb=True, add=dx, name='ssd_in_dt_dx')

    g['ssd_dt_bias'] = d_bias[:, :n_heads_ssd]
    g['ssd_A_log'] = d_alog[:, :n_heads_ssd]
    g['ssd_D'] = jnp.sum(d_dl.reshape(1, n_heads_ssd, SSD_HEAD_DIM), axis=2)
    g['ffn_conv_b'] = jnp.concatenate([d_fcb0, d_fcb1], axis=0)
    g['ln_mix_g'] = jnp.concatenate([dg_m0, dg_m1], axis=0)
    g['ln_mix_b'] = jnp.concatenate([db_m0, db_m1], axis=0)
    g['ln_ffn_g'] = jnp.concatenate([dg_f0, dg_f1], axis=0)
    g['ln_ffn_b'] = jnp.concatenate([db_f0, db_f1], axis=0)
    return sq, dx.reshape(Bn, S, D), g, early, last


ANY = pl.BlockSpec(memory_space=pl.ANY)


def _all_gather(xs, name):
    n = len(xs)

    def body(*refs):
        start, forward, finish = _gather_plan(refs[:n], refs[n:2 * n], *refs[2 * n:])
        start()
        forward()
        finish()

    return pl.pallas_call(
        body, name=name, out_shape=_gather_shapes(xs), in_specs=[ANY] * n, out_specs=[ANY] * n,
        scratch_shapes=_gather_sems(n),
    )(*xs)


def _gather_shapes(xs):
    return [jax.ShapeDtypeStruct((N_DEV,) + a.shape, a.dtype) for a in xs]


def _gather_sems(n):
    return [pltpu.SemaphoreType.DMA((7 * n,)), pltpu.SemaphoreType.DMA((7 * n,)), pltpu.SemaphoreType.DMA((n,))]


def _gather_plan(x_refs, out_refs, send_sems, recv_sems, local_sems):
    n = len(x_refs)
    x, y, c = lax.axis_index("x"), lax.axis_index("y"), lax.axis_index("c")
    me, sibling = (x, y, c), (x, y, 1 - c)
    chips = [(1 - x, y), (x, 1 - y), (1 - x, 1 - y)]

    def rows(i, px, py, pc):
        return out_refs[i].at[4 * px + 2 * py + pc]

    def copy(i, k, block, to, own=False):
        return pltpu.make_async_remote_copy(
            src_ref=x_refs[i] if own else rows(i, *block), dst_ref=rows(i, *block),
            send_sem=send_sems.at[7 * i + k], recv_sem=recv_sems.at[7 * i + k],
            device_id=to, device_id_type=MESH)

    def mine():
        return [pltpu.make_async_copy(x_refs[i], rows(i, *me), local_sems.at[i]) for i in range(n)]

    def first():
        out = []
        for i in range(n):
            out.append(copy(i, 0, me, sibling, own=True))
            out += [copy(i, 1 + j, me, (*chip, c), own=True) for j, chip in enumerate(chips)]
        return out

    def passed():
        return [copy(i, 4 + j, (*chip, c), sibling) for j, chip in enumerate(chips) for i in range(n)]

    def start():
        for cp in mine() + first():
            cp.start()

    def forward():
        for j, chip in enumerate(chips):
            for i in range(n):
                copy(i, 1 + j, (*chip, c), me).wait_recv()
                copy(i, 4 + j, (*chip, c), sibling).start()

    def finish():
        for i in range(n):
            copy(i, 0, sibling, me).wait_recv()
            for j, chip in enumerate(chips):
                copy(i, 4 + j, (*chip, 1 - c), me).wait_recv()
        for cp in first() + passed():
            cp.wait_send()
        for cp in mine():
            cp.wait()

    return start, forward, finish


def _exchange_sibling(gs, name):
    n = len(gs)

    def body(*refs):
        start, finish = _sibling_plan(refs[:n], refs[n:2 * n], *refs[2 * n:])
        start()
        finish()

    return pl.pallas_call(
        body, name=name, out_shape=_sibling_shapes(gs), in_specs=[ANY] * n, out_specs=[ANY] * n,
        scratch_shapes=_sibling_sems(n),
    )(*gs)


def _sibling_shapes(gs):
    return [jax.ShapeDtypeStruct((4,) + g.shape[1:], g.dtype) for g in gs]


def _sibling_sems(n):
    return [pltpu.SemaphoreType.DMA((4 * n,)), pltpu.SemaphoreType.DMA((4 * n,))]


def _sibling_plan(g_refs, r_refs, send_sems, recv_sems):
    n = len(g_refs)
    x, y, c = lax.axis_index("x"), lax.axis_index("y"), lax.axis_index("c")

    def copies():
        return [pltpu.make_async_remote_copy(
            src_ref=g_refs[i].at[2 * k + (1 - c)], dst_ref=r_refs[i].at[k], send_sem=send_sems.at[4 * i + k],
            recv_sem=recv_sems.at[4 * i + k], device_id=(x, y, 1 - c), device_id_type=MESH)
            for i in range(n) for k in range(4)]

    def start():
        for cp in copies():
            cp.start()

    def finish():
        for cp in copies():
            cp.wait()

    return start, finish


def _chips_shapes(parts):
    return [jax.ShapeDtypeStruct((3,) + p.shape[1:], p.dtype) for p in parts]


def _chips_sems(n):
    return [pltpu.SemaphoreType.DMA((3 * n,)), pltpu.SemaphoreType.DMA((3 * n,))]


def _chips_plan(p_refs, r_refs, send_sems, recv_sems):
    n = len(p_refs)
    x, y, c = lax.axis_index("x"), lax.axis_index("y"), lax.axis_index("c")
    chips = [(1 - x, y), (x, 1 - y), (1 - x, 1 - y)]

    def copies():
        return [pltpu.make_async_remote_copy(
            src_ref=p_refs[i].at[2 * cx + cy], dst_ref=r_refs[i].at[j], send_sem=send_sems.at[3 * i + j],
            recv_sem=recv_sems.at[3 * i + j], device_id=(cx, cy, c), device_id_type=MESH)
            for i in range(n) for j, (cx, cy) in enumerate(chips)]

    def start():
        for cp in copies():
            cp.start()

    def finish():
        for cp in copies():
            cp.wait()

    return start, finish


def _row_tile(rows, cols):
    want = max(8, (256 * 1024) // cols)
    for t in range(min(rows, want) // 8 * 8, 7, -8):
        if rows % t == 0:
            return t
    return rows


def _add_sibling(gfull, r1, core, name, wire=None):
    _, rows, lanes = gfull.shape
    tr = _row_tile(rows, lanes)

    def body(c_ref, g_ref, r_ref, o_ref, *wire_ref):
        s = g_ref[...] + r_ref[...]
        o_ref[...] = s
        if wire is not None:
            wire_ref[0][...] = s.astype(wire)

    blk = pl.BlockSpec((1, tr, lanes), lambda k, r, c_ref: (k, r, 0))
    shapes = [jax.ShapeDtypeStruct((4, rows, lanes), F32)]
    if wire is not None:
        shapes.append(jax.ShapeDtypeStruct((4, rows, lanes), wire))
    outs = pl.pallas_call(
        body, name=name,
        grid_spec=pltpu.PrefetchScalarGridSpec(
            num_scalar_prefetch=1, grid=(4, rows // tr),
            in_specs=[pl.BlockSpec((1, tr, lanes), lambda k, r, c_ref: (2 * k + c_ref[0], r, 0)), blk],
            out_specs=[blk] * len(shapes)),
        out_shape=shapes, compiler_params=_cparams(),
    )(core, gfull, r1)
    return outs if wire is not None else outs[0]


def _adam_math(wv, gv, mv, vv):
    m = ADAM_B1 * mv + (1.0 - ADAM_B1) * gv
    v = ADAM_B2 * vv + (1.0 - ADAM_B2) * (gv * gv)
    m_hat = m / (1.0 - ADAM_B1 ** ADAM_STEP)
    v_hat = v / (1.0 - ADAM_B2 ** ADAM_STEP)
    delta = -ADAM_LR * (m_hat / (jnp.sqrt(v_hat) + ADAM_EPS) + ADAM_WD * wv)
    return delta, m, v


def _adam_sharded(part, r2, chip, wts, m, v, name):
    rows, lanes = wts.shape
    tr = _row_tile(rows, lanes)

    def body(k_ref, p_ref, r_ref, w_ref, m_ref, v_ref, g_ref, d_ref, mo_ref, vo_ref):
        gv = p_ref[0] + r_ref[0].astype(F32) + r_ref[1].astype(F32) + r_ref[2].astype(F32)
        delta, mn, vn = _adam_math(w_ref[...], gv, m_ref[...], v_ref[...])
        g_ref[...] = gv
        d_ref[...] = delta
        mo_ref[...] = mn
        vo_ref[...] = vn

    flat = pl.BlockSpec((tr, lanes), lambda r, k_ref: (r, 0))
    sh = jax.ShapeDtypeStruct((rows, lanes), F32)
    return pl.pallas_call(
        body, name=name,
        grid_spec=pltpu.PrefetchScalarGridSpec(
            num_scalar_prefetch=1, grid=(rows // tr,),
            in_specs=[pl.BlockSpec((1, tr, lanes), lambda r, k_ref: (k_ref[0], r, 0)),
                      pl.BlockSpec((3, tr, lanes), lambda r, k_ref: (0, r, 0)), flat, flat, flat],
            out_specs=[flat, flat, flat, flat]),
        out_shape=[sh, sh, sh, sh], compiler_params=_cparams(),
    )(chip, part, r2, wts, m, v)


def _adam_replicated(gall, wts, m, v, name):
    rows, lanes = wts.shape

    def body(ga_ref, w_ref, m_ref, v_ref, g_ref, d_ref, mo_ref, vo_ref):
        gv = ga_ref[0]
        for k in range(1, N_DEV):
            gv = gv + ga_ref[k]
        delta, mn, vn = _adam_math(w_ref[...], gv, m_ref[...], v_ref[...])
        g_ref[...] = gv
        d_ref[...] = delta
        mo_ref[...] = mn
        vo_ref[...] = vn

    sh = jax.ShapeDtypeStruct((rows, lanes), F32)
    return pl.pallas_call(body, name=name, out_shape=[sh, sh, sh, sh], compiler_params=_cparams())(gall, wts, m, v)


def _pack(arrs, dtype, row_mult):
    flat = jnp.concatenate([a.reshape(-1).astype(dtype) for a in arrs])
    n = flat.shape[0]
    unit = LANES * row_mult
    total = -(-n // unit) * unit
    return jnp.pad(flat, (0, total - n)).reshape(total // LANES, LANES)


def _unpack(flat2d, shapes):
    flat = flat2d.reshape(-1)
    out, off = [], 0
    for shp in shapes:
        n = math.prod(shp)
        out.append(flat[off:off + n].reshape(shp))
        off += n
    return out


def _unpack_gathered(gathered, shapes, axes):
    flat = gathered.reshape(N_DEV, -1)
    out, off = [], 0
    for shp, ax in zip(shapes, axes):
        n = math.prod(shp)
        seg = flat[:, off:off + n].reshape((N_DEV,) + tuple(shp))
        out.append(jnp.concatenate([seg[i] for i in range(N_DEV)], axis=ax))
        off += n
    return out


def _pack_chunks(fulls, axes, row_mult):
    per_dev = []
    for full, ax in zip(fulls, axes):
        parts = jnp.stack(jnp.split(full, N_DEV, axis=ax))
        per_dev.append(parts.reshape(N_DEV, -1))
    flat = jnp.concatenate(per_dev, axis=1)
    n = flat.shape[1]
    unit = LANES * row_mult
    total = -(-n // unit) * unit
    return jnp.pad(flat, ((0, 0), (0, total - n))).reshape(N_DEV, total // LANES, LANES)


def kernel(x, positions, ssd_in_proj, ssd_conv_w, ssd_conv_b, ssd_dt_bias, ssd_A_log, ssd_D, ssd_norm_g, ssd_out_proj, kv_down_proj, kv_norm_g, kv_up_k, kv_up_v, q_down_proj, q_norm_g, q_up_proj, attn_out_proj, ffn_up, ffn_conv_w, ffn_conv_b, ffn_down, ln_mix_g, ln_mix_b, ln_ffn_g, ln_ffn_b, loss_target, m_ssd_in_proj, m_ssd_conv_w, m_ssd_conv_b, m_ssd_dt_bias, m_ssd_A_log, m_ssd_D, m_ssd_norm_g, m_ssd_out_proj, m_kv_down_proj, m_kv_norm_g, m_kv_up_k, m_kv_up_v, m_q_down_proj, m_q_norm_g, m_q_up_proj, m_attn_out_proj, m_ffn_up, m_ffn_conv_w, m_ffn_conv_b, m_ffn_down, m_ln_mix_g, m_ln_mix_b, m_ln_ffn_g, m_ln_ffn_b, v_ssd_in_proj, v_ssd_conv_w, v_ssd_conv_b, v_ssd_dt_bias, v_ssd_A_log, v_ssd_D, v_ssd_norm_g, v_ssd_out_proj, v_kv_down_proj, v_kv_norm_g, v_kv_up_k, v_kv_up_v, v_q_down_proj, v_q_norm_g, v_q_up_proj, v_attn_out_proj, v_ffn_up, v_ffn_conv_w, v_ffn_conv_b, v_ffn_down, v_ln_mix_g, v_ln_mix_b, v_ln_ffn_g, v_ln_ffn_b):
    given = dict(locals())
    wl = {n: given[n] for n in WEIGHTS}
    ml = {n: given['m_' + n] for n in WEIGHTS}
    vl = {n: given['v_' + n] for n in WEIGHTS}
    sharded_axis = dict(SHARDED)
    big = [n for n, _ in SHARDED if n not in SHARDED_F32]
    small = [n for n, _ in SHARDED if n in SHARDED_F32]

    def as2d(a):
        return a.reshape(-1, a.shape[-1])

    def to_full(gathered, n):
        shp, ax = wl[n].shape, sharded_axis[n]
        moved = jnp.moveaxis(gathered.reshape((N_DEV,) + shp), 0, ax)
        return moved.reshape(shp[:ax] + (N_DEV * shp[ax],) + shp[ax + 1:])

    def to_chunks(full_grad, n):
        shp, ax = wl[n].shape, sharded_axis[n]
        split = full_grad.reshape(shp[:ax] + (N_DEV, shp[ax]) + shp[ax + 1:])
        return jnp.moveaxis(split, ax, 0).reshape((N_DEV,) + as2d(wl[n]).shape)

    first = [n for n in big if n in GATHERED_FIRST]
    late = [n for n in big if n not in GATHERED_FIRST]
    full = {n: wl[n] for n in REPLICATED}
    gathered = _all_gather([as2d(wl[n]).astype(BF16) for n in first] + [_pack([wl[n] for n in small], F32, 8)],
                           name='gather_weights')
    for n, gat in zip(first, gathered[:-1]):
        full[n] = to_full(gat, n)
    full.update(zip(small, _unpack_gathered(gathered[-1], [wl[n].shape for n in small],
                                            [sharded_axis[n] for n in small])))

    cx, cy, cc = lax.axis_index("x"), lax.axis_index("y"), lax.axis_index("c")
    core = jnp.reshape(cc, (1,)).astype(jnp.int32)
    chip = jnp.reshape(2 * cx + cy, (1,)).astype(jnp.int32)
    early_names = [n for n in big if n not in REDUCED_LAST]
    last_names = [n for n in big if n in REDUCED_LAST]
    parts = {}

    def add_sibling(names, chunked, received):
        parts.update((n, _add_sibling(gf, r, core, name=f'grads_add_sibling_{n}'))
                     for n, gf, r in zip(names, chunked, received))
        return [parts[n] for n in names]

    def early_blocks(grads):
        return [to_chunks(grads[n], n) for n in early_names]

    def early_parts(chunked, received):
        return add_sibling(early_names, chunked, received)

    def reduce_last(grads):
        chunked = [to_chunks(grads[n], n) for n in last_names]
        chunked.append(_pack_chunks([grads[n] for n in small], [sharded_axis[n] for n in small], 8))
        received = _exchange_sibling(chunked, name='grads_to_sibling_last')
        send = []
        for n, gf, r in zip(last_names, chunked, received):
            parts[n], wire = _add_sibling(gf, r, core, name=f'grads_add_sibling_{n}', wire=BF16)
            send.append(wire)
        return send + add_sibling(['small'], chunked[-1:], received[-1:])

    sq, grad_x, grads, r2_early, r2_last = _local_step(
        x, positions, full, loss_target,
        late=([as2d(wl[n]).astype(BF16) for n in late], lambda gats: {n: to_full(g, n) for n, g in zip(late, gats)}),
        reduce_early=(early_blocks, early_parts), reduce_last=reduce_last)
    loss = lax.psum(0.5 * jnp.sum(sq) / x.shape[-1], ("x", "y", "c"))

    r2 = dict(zip(early_names, r2_early))
    r2.update(zip(last_names + ['small'], r2_last))
    res = {}
    kinds = ('grad', 'delta', 'new_m', 'new_v')
    for n in big:
        outs = _adam_sharded(parts[n], r2[n], chip, as2d(wl[n]), as2d(ml[n]), as2d(vl[n]), name=f'adamw_{n}')
        for kind, a in zip(kinds, outs):
            res[kind, n] = a.reshape(wl[n].shape)
    outs = _adam_sharded(parts['small'], r2['small'], chip, _pack([wl[n] for n in small], F32, 8),
                         _pack([ml[n] for n in small], F32, 8), _pack([vl[n] for n in small], F32, 8),
                         name='adamw_small_sharded')
    for kind, packed in zip(kinds, outs):
        for n, a in zip(small, _unpack(packed, [wl[n].shape for n in small])):
            res[kind, n] = a

    rep = list(REPLICATED)
    rshapes = [wl[n].shape for n in rep]
    gall, = _all_gather([_pack([grads[n] for n in rep], F32, 8)], name='gather_replicated_grads')
    outs = _adam_replicated(gall, _pack([wl[n] for n in rep], F32, 8), _pack([ml[n] for n in rep], F32, 8),
                            _pack([vl[n] for n in rep], F32, 8), name='adamw_replicated')
    for kind, packed in zip(('grad', 'delta', 'new_m', 'new_v'), outs):
        for n, a in zip(rep, _unpack(packed, rshapes)):
            res[kind, n] = a

    return (loss, grad_x, *[res['grad', n] for n in WEIGHTS], *[res['delta', n] for n in WEIGHTS],
            *[res['new_m', n] for n in WEIGHTS], *[res['new_v', n] for n in WEIGHTS])
```
